```python
import jax, jax.numpy as jnp
from jax import lax
import numpy as np


D_MODEL = 1024
BATCH = 8
SEQ = 4096
DEPTH = 4

ATT_HEAD_DIM = 64
D_ATT = D_MODEL // 2
N_ATT_HEADS = D_ATT // ATT_HEAD_DIM
D_CONV = D_MODEL // 2
N_CONV_GROUPS = 8
CONV_WIDTH = 31
Q_BLOCK = 128
LN_EPS = 1e-5
GATE_INIT_STD = 0.02
DEEPNORM_ALPHA = (2 * DEPTH) ** 0.25
DEEPNORM_BETA = (8 * DEPTH) ** -0.25
IN_WIDTHS = (D_ATT, D_ATT, D_ATT, D_ATT, D_CONV, D_CONV, D_CONV, D_MODEL, D_MODEL)
D_IN = sum(IN_WIDTHS)

kernel_name = 'stickbreak_conformer_gated_hybrid'


def _split_points():
    pts, acc = [], 0
    for w in IN_WIDTHS[:-1]:
        acc += w
        pts.append(acc)
    return pts


def layer_norm(x, g, b):
    xf = x.astype(jnp.float32)
    mu = jnp.mean(xf, axis=-1, keepdims=True)
    var = jnp.mean(jnp.square(xf - mu), axis=-1, keepdims=True)
    return ((xf - mu) * lax.rsqrt(var + LN_EPS) * g + b).astype(x.dtype)


def stick_breaking_attention(q, k, v):
    S = q.shape[1]
    scale = ATT_HEAD_DIM ** -0.5
    outs = []
    for blk in range(S // Q_BLOCK):
        q0 = blk * Q_BLOCK
        q1 = q0 + Q_BLOCK
        qb = q[:, q0:q1]
        kb = k[:, :q1]
        vb = v[:, :q1]
        z = jnp.einsum('bthd,bshd->bhts', qb, kb).astype(jnp.float32) * scale
        t_pos = q0 + jnp.arange(Q_BLOCK)[:, None]
        s_pos = jnp.arange(q1)[None, :]
        causal = s_pos < t_pos
        log_fail = jnp.where(causal, jax.nn.log_sigmoid(-z), 0.0)
        later = lax.cumsum(log_fail, axis=3, reverse=True) - log_fail
        w = jnp.where(causal, jnp.exp(jax.nn.log_sigmoid(z) + later), 0.0)
        outs.append(jnp.einsum('bhts,bshd->bthd', w.astype(vb.dtype), vb))
    return jnp.concatenate(outs, axis=1)


def causal_depthwise_conv(u, w, b):
    C = u.shape[-1]
    up = jnp.pad(u, ((0, 0), (CONV_WIDTH - 1, 0), (0, 0)))
    out = lax.conv_general_dilated(up, w[:, None, :].astype(u.dtype), window_strides=(1,), padding='VALID',
                                   dimension_numbers=('NWC', 'WIO', 'NWC'), feature_group_count=C)
    return out + b


def hybrid_layer(x, w_in, b_in, conv_w, conv_b, conv_ln_g, conv_ln_b,
                 w_att_proj, w_conv_proj, b_conv_proj, w_out, ln_g, ln_b):
    B, S, _ = x.shape
    u = jnp.einsum('bsd,de->bse', x, w_in) + b_in
    q, k, v, z_att, glu_a, glu_b, z_conv, g_att, g_conv = jnp.split(u, _split_points(), axis=-1)

    heads = lambda t: t.reshape(B, S, N_ATT_HEADS, ATT_HEAD_DIM)
    att = stick_breaking_attention(heads(q), heads(k), heads(v)).reshape(B, S, D_ATT)
    att_branch = jnp.einsum('bsc,cd->bsd', att * jax.nn.silu(z_att), w_att_proj)

    c = glu_a * jax.nn.sigmoid(glu_b)
    c = causal_depthwise_conv(c, conv_w, conv_b)
    c = jax.nn.silu(layer_norm(c, conv_ln_g, conv_ln_b))
    conv_branch = jnp.einsum('bsc,cd->bsd', c * jax.nn.silu(z_conv), w_conv_proj) + b_conv_proj

    merged = jax.nn.sigmoid(g_att) * att_branch + jax.nn.sigmoid(g_conv) * conv_branch
    y = jnp.einsum('bsd,de->bse', merged, w_out)
    return layer_norm(DEEPNORM_ALPHA * x + y, ln_g, ln_b)


def _fwd_setup_inputs(seed: int = 0) -> dict:
    key = jax.random.key(seed)
    ks = jax.random.split(key, 13)
    L, D = DEPTH, D_MODEL
    nrm = lambda k, shape, s: jax.random.normal(k, shape, jnp.float32) * s
    x = nrm(ks[0], (BATCH, SEQ, D), 1.0)
    col_scale = jnp.ones((D_IN,), jnp.float32).at[2 * D_ATT:3 * D_ATT].set(DEEPNORM_BETA)
    w_in = nrm(ks[1], (L, D, D_IN), D ** -0.5) * col_scale
    b_in = nrm(ks[2], (L, D_IN), GATE_INIT_STD)
    conv_w = nrm(ks[3], (L, CONV_WIDTH, D_CONV), CONV_WIDTH ** -0.5)
    conv_b = nrm(ks[4], (L, D_CONV), GATE_INIT_STD)
    conv_ln_g = 1.0 + nrm(ks[5], (L, D_CONV), GATE_INIT_STD)
    conv_ln_b = nrm(ks[6], (L, D_CONV), GATE_INIT_STD)
    w_att_proj = nrm(ks[7], (L, D_ATT, D), D_ATT ** -0.5 * DEEPNORM_BETA)
    w_conv_proj = nrm(ks[8], (L, D_CONV, D), D_CONV ** -0.5 * DEEPNORM_BETA)
    b_conv_proj = nrm(ks[9], (L, D), GATE_INIT_STD)
    w_out = nrm(ks[10], (L, D, D), D ** -0.5 * DEEPNORM_BETA)
    ln_g = 1.0 + nrm(ks[11], (L, D), GATE_INIT_STD)
    ln_b = nrm(ks[12], (L, D), GATE_INIT_STD)
    return {'x': x, 'w_in': w_in, 'b_in': b_in, 'conv_w': conv_w, 'conv_b': conv_b,
            'conv_ln_g': conv_ln_g, 'conv_ln_b': conv_ln_b, 'w_att_proj': w_att_proj,
            'w_conv_proj': w_conv_proj, 'b_conv_proj': b_conv_proj, 'w_out': w_out,
            'ln_g': ln_g, 'ln_b': ln_b}


def _fwd_reference(x, w_in, b_in, conv_w, conv_b, conv_ln_g, conv_ln_b,
              w_att_proj, w_conv_proj, b_conv_proj, w_out, ln_g, ln_b):
    for l in range(DEPTH):
        x = hybrid_layer(x, w_in[l], b_in[l], conv_w[l], conv_b[l], conv_ln_g[l], conv_ln_b[l],
                         w_att_proj[l], w_conv_proj[l], b_conv_proj[l], w_out[l], ln_g[l], ln_b[l])
    return x


import jax as _jax
import jax.numpy as _jnp

TWIN_FORMAT = 'train_step'
FWD_PARAMS = ['x', 'w_in', 'b_in', 'conv_w', 'conv_b', 'conv_ln_g', 'conv_ln_b', 'w_att_proj', 'w_conv_proj', 'b_conv_proj', 'w_out', 'ln_g', 'ln_b']
TWIN_WEIGHTS = ['w_in', 'b_in', 'conv_w', 'conv_b', 'conv_ln_g', 'conv_ln_b', 'w_att_proj', 'w_conv_proj', 'b_conv_proj', 'w_out', 'ln_g', 'ln_b']
TWIN_DIFF_INPUT = 'x'
TWIN_INPUTS = ['x', 'w_in', 'b_in', 'conv_w', 'conv_b', 'conv_ln_g', 'conv_ln_b', 'w_att_proj', 'w_conv_proj', 'b_conv_proj', 'w_out', 'ln_g', 'ln_b', 'loss_target', 'm_w_in', 'm_b_in', 'm_conv_w', 'm_conv_b', 'm_conv_ln_g', 'm_conv_ln_b', 'm_w_att_proj', 'm_w_conv_proj', 'm_b_conv_proj', 'm_w_out', 'm_ln_g', 'm_ln_b', 'v_w_in', 'v_b_in', 'v_conv_w', 'v_conv_b', 'v_conv_ln_g', 'v_conv_ln_b', 'v_w_att_proj', 'v_w_conv_proj', 'v_b_conv_proj', 'v_w_out', 'v_ln_g', 'v_ln_b']
TWIN_OUTPUTS = ['loss', 'grad_x', 'grad_w_in', 'grad_b_in', 'grad_conv_w', 'grad_conv_b', 'grad_conv_ln_g', 'grad_conv_ln_b', 'grad_w_att_proj', 'grad_w_conv_proj', 'grad_b_conv_proj', 'grad_w_out', 'grad_ln_g', 'grad_ln_b', 'delta_w_in', 'delta_b_in', 'delta_conv_w', 'delta_conv_b', 'delta_conv_ln_g', 'delta_conv_ln_b', 'delta_w_att_proj', 'delta_w_conv_proj', 'delta_b_conv_proj', 'delta_w_out', 'delta_ln_g', 'delta_ln_b', 'new_m_w_in', 'new_m_b_in', 'new_m_conv_w', 'new_m_conv_b', 'new_m_conv_ln_g', 'new_m_conv_ln_b', 'new_m_w_att_proj', 'new_m_w_conv_proj', 'new_m_b_conv_proj', 'new_m_w_out', 'new_m_ln_g', 'new_m_ln_b', 'new_v_w_in', 'new_v_b_in', 'new_v_conv_w', 'new_v_conv_b', 'new_v_conv_ln_g', 'new_v_conv_ln_b', 'new_v_w_att_proj', 'new_v_w_conv_proj', 'new_v_b_conv_proj', 'new_v_w_out', 'new_v_ln_g', 'new_v_ln_b']
TWIN_LEAF_KINDS = {'loss': 'loss', 'grad_x': 'grad_x', 'grad_w_in': 'grad_w', 'grad_b_in': 'grad_w', 'grad_conv_w': 'grad_w', 'grad_conv_b': 'grad_w', 'grad_conv_ln_g': 'grad_w', 'grad_conv_ln_b': 'grad_w', 'grad_w_att_proj': 'grad_w', 'grad_w_conv_proj': 'grad_w', 'grad_b_conv_proj': 'grad_w', 'grad_w_out': 'grad_w', 'grad_ln_g': 'grad_w', 'grad_ln_b': 'grad_w', 'delta_w_in': 'delta_w', 'delta_b_in': 'delta_w', 'delta_conv_w': 'delta_w', 'delta_conv_b': 'delta_w', 'delta_conv_ln_g': 'delta_w', 'delta_conv_ln_b': 'delta_w', 'delta_w_att_proj': 'delta_w', 'delta_w_conv_proj': 'delta_w', 'delta_b_conv_proj': 'delta_w', 'delta_w_out': 'delta_w', 'delta_ln_g': 'delta_w', 'delta_ln_b': 'delta_w', 'new_m_w_in': 'new_m', 'new_m_b_in': 'new_m', 'new_m_conv_w': 'new_m', 'new_m_conv_b': 'new_m', 'new_m_conv_ln_g': 'new_m', 'new_m_conv_ln_b': 'new_m', 'new_m_w_att_proj': 'new_m', 'new_m_w_conv_proj': 'new_m', 'new_m_b_conv_proj': 'new_m', 'new_m_w_out': 'new_m', 'new_m_ln_g': 'new_m', 'new_m_ln_b': 'new_m', 'new_v_w_in': 'new_v', 'new_v_b_in': 'new_v', 'new_v_conv_w': 'new_v', 'new_v_conv_b': 'new_v', 'new_v_conv_ln_g': 'new_v', 'new_v_conv_ln_b': 'new_v', 'new_v_w_att_proj': 'new_v', 'new_v_w_conv_proj': 'new_v', 'new_v_b_conv_proj': 'new_v', 'new_v_w_out': 'new_v', 'new_v_ln_g': 'new_v', 'new_v_ln_b': 'new_v'}


def _forward(args):
    return _fwd_reference(*[args[k] for k in FWD_PARAMS])


def _output_shape():
    out = _jax.eval_shape(lambda: _forward(_fwd_setup_inputs(0)))
    return out.shape, out.dtype

N_MICROBATCH = 1
ADAM_LR = 0.001
ADAM_B1 = 0.9
ADAM_B2 = 0.999
ADAM_EPS = 1e-08
ADAM_WD = 0.01
ADAM_STEP = 10
PER_EXAMPLE_BATCH_AXIS = {'x': 0, 'loss_target': 0}
SHARED_INPUTS = []
_WEIGHT_DTYPES = {'w_in': _jnp.float32, 'b_in': _jnp.float32, 'conv_w': _jnp.float32, 'conv_b': _jnp.float32, 'conv_ln_g': _jnp.float32, 'conv_ln_b': _jnp.float32, 'w_att_proj': _jnp.float32, 'w_conv_proj': _jnp.float32, 'b_conv_proj': _jnp.float32, 'w_out': _jnp.float32, 'ln_g': _jnp.float32, 'ln_b': _jnp.float32}
MOMENT_SCALE = {'w_in': 3.051507e-03, 'b_in': 4.192600e-03, 'conv_w': 5.274233e-03, 'conv_b': 1.091145e-02, 'conv_ln_g': 6.433984e-03, 'conv_ln_b': 6.080252e-03, 'w_att_proj': 3.888745e-03, 'w_conv_proj': 8.711887e-03, 'b_conv_proj': 5.758470e-02, 'w_out': 9.805258e-03, 'ln_g': 1.603671e+01, 'ln_b': 6.289261e-01}


def _to_microbatches(a, axis):
    t = _jnp.moveaxis(a, axis, 0)
    t = t.reshape((N_MICROBATCH, t.shape[0] // N_MICROBATCH) + t.shape[1:])
    return _jnp.moveaxis(t, 1, axis + 1)


def setup_inputs(seed: int = 0) -> dict:
    inp = _fwd_setup_inputs(seed)
    key = _jax.random.fold_in(_jax.random.key(seed), 7919)
    shape, _ = _output_shape()
    out = dict(inp)
    out["loss_target"] = _jax.random.normal(_jax.random.fold_in(key, 0), shape, _jnp.float32)
    for i, name in enumerate(TWIN_WEIGHTS):
        w = inp[name].astype(_jnp.float32)
        if MOMENT_SCALE is None:
            s = _jnp.sqrt(_jnp.mean(_jnp.square(w)) + 1e-30)
        else:
            s = MOMENT_SCALE[name]
        km, kv = _jax.random.split(_jax.random.fold_in(key, i + 1))
        out[name] = w
        out["m_" + name] = s * _jax.random.normal(km, w.shape, _jnp.float32)
        out["v_" + name] = (s * s) * _jax.random.uniform(kv, w.shape, _jnp.float32, 0.5, 1.5)
    if N_MICROBATCH > 1:
        for name, axis in PER_EXAMPLE_BATCH_AXIS.items():
            out[name] = _to_microbatches(out[name], axis)
    return {'x': out['x'], 'w_in': out['w_in'], 'b_in': out['b_in'], 'conv_w': out['conv_w'], 'conv_b': out['conv_b'], 'conv_ln_g': out['conv_ln_g'], 'conv_ln_b': out['conv_ln_b'], 'w_att_proj': out['w_att_proj'], 'w_conv_proj': out['w_conv_proj'], 'b_conv_proj': out['b_conv_proj'], 'w_out': out['w_out'], 'ln_g': out['ln_g'], 'ln_b': out['ln_b'], 'loss_target': out['loss_target'], 'm_w_in': out['m_w_in'], 'm_b_in': out['m_b_in'], 'm_conv_w': out['m_conv_w'], 'm_conv_b': out['m_conv_b'], 'm_conv_ln_g': out['m_conv_ln_g'], 'm_conv_ln_b': out['m_conv_ln_b'], 'm_w_att_proj': out['m_w_att_proj'], 'm_w_conv_proj': out['m_w_conv_proj'], 'm_b_conv_proj': out['m_b_conv_proj'], 'm_w_out': out['m_w_out'], 'm_ln_g': out['m_ln_g'], 'm_ln_b': out['m_ln_b'], 'v_w_in': out['v_w_in'], 'v_b_in': out['v_b_in'], 'v_conv_w': out['v_conv_w'], 'v_conv_b': out['v_conv_b'], 'v_conv_ln_g': out['v_conv_ln_g'], 'v_conv_ln_b': out['v_conv_ln_b'], 'v_w_att_proj': out['v_w_att_proj'], 'v_w_conv_proj': out['v_w_conv_proj'], 'v_b_conv_proj': out['v_b_conv_proj'], 'v_w_out': out['v_w_out'], 'v_ln_g': out['v_ln_g'], 'v_ln_b': out['v_ln_b']}


def _loss(weights, diff, rest, loss_target):
    with _jax.named_scope("forward"):
        args = {**rest, TWIN_DIFF_INPUT: diff, **{k: w.astype(_WEIGHT_DTYPES[k]) for k, w in weights.items()}}
        y = _forward(args)
    with _jax.named_scope("loss_head"):
        err = _jnp.square(y.astype(_jnp.float32) - loss_target)
        return 0.5 * _jnp.sum(_jnp.mean(err, axis=-1)) if err.ndim else 0.5 * err


def _adamw(w, g, m, v):
    m = ADAM_B1 * m + (1.0 - ADAM_B1) * g
    v = ADAM_B2 * v + (1.0 - ADAM_B2) * _jnp.square(g)
    m_hat = m / (1.0 - ADAM_B1 ** ADAM_STEP)
    v_hat = v / (1.0 - ADAM_B2 ** ADAM_STEP)
    delta = -ADAM_LR * (m_hat / (_jnp.sqrt(v_hat) + ADAM_EPS) + ADAM_WD * w)
    return delta, m, v


def reference(x, w_in, b_in, conv_w, conv_b, conv_ln_g, conv_ln_b, w_att_proj, w_conv_proj, b_conv_proj, w_out, ln_g, ln_b, loss_target, m_w_in, m_b_in, m_conv_w, m_conv_b, m_conv_ln_g, m_conv_ln_b, m_w_att_proj, m_w_conv_proj, m_b_conv_proj, m_w_out, m_ln_g, m_ln_b, v_w_in, v_b_in, v_conv_w, v_conv_b, v_conv_ln_g, v_conv_ln_b, v_w_att_proj, v_w_conv_proj, v_b_conv_proj, v_w_out, v_ln_g, v_ln_b):
    given = dict(x=x, w_in=w_in, b_in=b_in, conv_w=conv_w, conv_b=conv_b, conv_ln_g=conv_ln_g, conv_ln_b=conv_ln_b, w_att_proj=w_att_proj, w_conv_proj=w_conv_proj, b_conv_proj=b_conv_proj, w_out=w_out, ln_g=ln_g, ln_b=ln_b, loss_target=loss_target, m_w_in=m_w_in, m_b_in=m_b_in, m_conv_w=m_conv_w, m_conv_b=m_conv_b, m_conv_ln_g=m_conv_ln_g, m_conv_ln_b=m_conv_ln_b, m_w_att_proj=m_w_att_proj, m_w_conv_proj=m_w_conv_proj, m_b_conv_proj=m_b_conv_proj, m_w_out=m_w_out, m_ln_g=m_ln_g, m_ln_b=m_ln_b, v_w_in=v_w_in, v_b_in=v_b_in, v_conv_w=v_conv_w, v_conv_b=v_conv_b, v_conv_ln_g=v_conv_ln_g, v_conv_ln_b=v_conv_ln_b, v_w_att_proj=v_w_att_proj, v_w_conv_proj=v_w_conv_proj, v_b_conv_proj=v_b_conv_proj, v_w_out=v_w_out, v_ln_g=v_ln_g, v_ln_b=v_ln_b)
    weights = {n: given[n] for n in TWIN_WEIGHTS}
    shared = {n: given[n] for n in SHARED_INPUTS}
    per_example = {n: given[n] for n in ['x']}
    grad_fn = _jax.value_and_grad(_loss, argnums=(0, 1))

    def one_microbatch(ex, loss_target):
        ex = dict(ex)
        diff = ex.pop(TWIN_DIFF_INPUT)
        return grad_fn(weights, diff, {**shared, **ex}, loss_target)

    if N_MICROBATCH == 1:
        loss, (grad_w, grad_x) = one_microbatch(per_example, given["loss_target"])
    else:
        def body(carry, xs):
            loss_sum, grad_sum = carry
            l_k, (gw_k, gx_k) = one_microbatch(xs[0], xs[1])
            with _jax.named_scope("update"):
                return (loss_sum + l_k, _jax.tree.map(_jnp.add, grad_sum, gw_k)), gx_k

        init = (_jnp.zeros((), _jnp.float32), _jax.tree.map(_jnp.zeros_like, weights))
        (loss, grad_w), grad_x = _jax.lax.scan(body, init, (per_example, given["loss_target"]))
    with _jax.named_scope("update"):
        delta_w, new_m, new_v = {}, {}, {}
        for n in TWIN_WEIGHTS:
            delta_w[n], new_m[n], new_v[n] = _adamw(weights[n], grad_w[n], given["m_" + n], given["v_" + n])
    return (loss, grad_x, *[grad_w[n] for n in TWIN_WEIGHTS], *[delta_w[n] for n in TWIN_WEIGHTS],
            *[new_m[n] for n in TWIN_WEIGHTS], *[new_v[n] for n in TWIN_WEIGHTS])
```

```python
import functools
import math

import jax
import jax.numpy as jnp
from jax import lax
from jax.experimental import pallas as pl
from jax.experimental.pallas import tpu as pltpu

F32 = jnp.float32
BF16 = jnp.bfloat16

HEAD_DIM = 64
LANES = 128
CONV_WIDTH = 31
CONV_PAD = 32
SUBLANES = 8
LN_EPS = 1e-5
DEPTH = 4
DEEPNORM_ALPHA = (2 * DEPTH) ** 0.25
ATT_SCALE = HEAD_DIM ** -0.5
ATT_TILE = 128
ATT_DEAD = -104.0

ADAM_LR = 0.001
ADAM_B1 = 0.9
ADAM_B2 = 0.999
ADAM_EPS = 1e-08
ADAM_WD = 0.01
ADAM_STEP = 10

VMEM_LIMIT = 56 * 1024 * 1024

N_CHIPS = 4


def _cparams(sem):
    return pltpu.CompilerParams(dimension_semantics=sem, vmem_limit_bytes=VMEM_LIMIT)


def _sigmoid(x):
    return 1.0 / (1.0 + jnp.exp(-x))


def _fit(tile, dim):
    assert dim % LANES == 0
    tile = min(tile, dim) // LANES * LANES
    while dim % tile:
        tile -= LANES
    return tile


_DIMS = {"nn": ((1,), (0,)), "nt": ((1,), (1,)), "tn": ((0,), (0,))}


def _matmul(a, b, *, mode, name, layer=None, bias=None, add=None, colsum=False, out_dtype=F32, tm=1024, tn=512, tk=1024):
    b_shape = b.shape if layer is None else b.shape[1:]
    if mode == "nn":
        (m, k), (k2, n) = a.shape, b_shape
    elif mode == "nt":
        (m, k), (n, k2) = a.shape, b_shape
    else:
        (k, m), (k2, n) = a.shape, b_shape
    assert k == k2
    tm, tn, tk = _fit(tm, m), _fit(tn, n), _fit(tk, k)
    gm, gn, nk = m // tm, n // tn, k // tk

    a_spec = pl.BlockSpec((tk, tm), lambda i, j, kk: (kk, i)) if mode == "tn" else pl.BlockSpec((tm, tk), lambda i, j, kk: (i, kk))
    if layer is None:
        b_spec = pl.BlockSpec((tn, tk), lambda i, j, kk: (j, kk)) if mode == "nt" else pl.BlockSpec((tk, tn), lambda i, j, kk: (kk, j))
    elif mode == "nt":
        b_spec = pl.BlockSpec((None, tn, tk), lambda i, j, kk: (layer, j, kk))
    else:
        b_spec = pl.BlockSpec((None, tk, tn), lambda i, j, kk: (layer, kk, j))
    in_specs, operands = [a_spec, b_spec], [a, b]
    if bias is not None:
        in_specs.append(pl.BlockSpec((1, tn), lambda i, j, kk: (0, j)))
        operands.append(bias)
    if add is not None:
        in_specs.append(pl.BlockSpec((tm, tn), lambda i, j, kk: (i, j)))
        operands.append(add)
    out_shape = [jax.ShapeDtypeStruct((m, n), out_dtype)]
    out_specs = [pl.BlockSpec((tm, tn), lambda i, j, kk: (i, j))]
    scratch = [pltpu.VMEM((tm, tn), F32)]
    if colsum:
        assert mode == "tn"
        out_shape.append(jax.ShapeDtypeStruct((gm, 1, n), F32))
        out_specs.append(pl.BlockSpec((1, 1, tn), lambda i, j, kk: (i, 0, j)))
        scratch.append(pltpu.VMEM((1, tn), F32))
    has_bias, has_add = bias is not None, add is not None

    def body(*refs):
        refs = list(refs)
        a_ref, b_ref = refs[0], refs[1]
        pos = 2
        bias_ref = add_ref = None
        if has_bias:
            bias_ref = refs[pos]
            pos += 1
        if has_add:
            add_ref = refs[pos]
            pos += 1
        o_ref = refs[pos]
        pos += 1
        cs_ref = None
        if colsum:
            cs_ref = refs[pos]
            pos += 1
        acc_ref = refs[pos]
        cs_acc = refs[pos + 1] if colsum else None
        kk = pl.program_id(2)

        @pl.when(kk == 0)
        def _():
            acc_ref[...] = jnp.zeros_like(acc_ref)
            if colsum:
                cs_acc[...] = jnp.zeros_like(cs_acc)

        bv = b_ref[...]
        acc_ref[...] += lax.dot_general(a_ref[...].astype(BF16), bv.astype(BF16), (_DIMS[mode], ((), ())),
                                        preferred_element_type=F32)
        if colsum:
            cs_acc[...] += jnp.sum(bv.astype(F32), axis=0, keepdims=True)

        @pl.when(kk == nk - 1)
        def _():
            out = acc_ref[...]
            if has_bias:
                out = out + bias_ref[...]
            if has_add:
                out = out + add_ref[...]
            o_ref[...] = out.astype(out_dtype)
            if colsum:
                cs_ref[0] = cs_acc[...]

    res = pl.pallas_call(
        body, name=name, grid=(gm, gn, nk), in_specs=in_specs, out_specs=out_specs, out_shape=out_shape,
        scratch_shapes=scratch, compiler_params=_cparams(("parallel", "parallel", "arbitrary")),
    )(*operands)
    if colsum:
        return res[0], res[1][0]
    return res[0]


def _scan_matrices():
    t = ATT_TILE
    r = lax.broadcasted_iota(jnp.int32, (t, t), 0)
    c = lax.broadcasted_iota(jnp.int32, (t, t), 1)
    ones = jnp.ones((t, t), F32)
    suffix = jnp.concatenate([(r > c).astype(F32), ones], axis=1)
    prefix = jnp.concatenate([(r < c).astype(F32), ones], axis=1)
    stack = lambda mat: jnp.concatenate([mat, mat], axis=0).astype(BF16)
    return stack(suffix), stack(prefix)


def _split_scan(val, mat_ref):
    hi = val.astype(BF16)
    lo = (val - hi.astype(F32)).astype(BF16)
    return jnp.dot(jnp.concatenate([hi, lo], axis=1), mat_ref[...], preferred_element_type=F32)


def _att_tile_scores(q, k, masked):
    t = ATT_TILE
    z = lax.dot_general(q, k, (((1,), (1,)), ((), ())), preferred_element_type=F32)
    sp = jnp.log(1.0 + jnp.exp(-jnp.abs(z)))
    f = jnp.minimum(-z, 0.0) - sp
    a = f + z
    causal = None
    if masked:
        causal = lax.broadcasted_iota(jnp.int32, (t, t), 1) < lax.broadcasted_iota(jnp.int32, (t, t), 0)
        f = jnp.where(causal, f, 0.0)
    return f, a, causal


def _head_copies(seq, src_ref, scale, lo_ref, hi_ref, plain_ref):
    chunk = min(256, seq)
    low = lax.broadcasted_iota(jnp.int32, (chunk, LANES), 1) < HEAD_DIM

    def step(r, carry):
        rows = pl.ds(pl.multiple_of(r * chunk, chunk), chunk)
        val = src_ref[rows, :]
        if scale != 1.0:
            val = val * scale
        if lo_ref is not None:
            lo_ref[rows, :] = jnp.where(low, val, 0.0).astype(BF16)
            hi_ref[rows, :] = jnp.where(low, 0.0, val).astype(BF16)
        if plain_ref is not None:
            plain_ref[rows, :] = val.astype(BF16)
        return carry

    lax.fori_loop(0, seq // chunk, step, 0)


def _attn_fwd(u, scan_suffix, *, seq, d_att, name):
    t = ATT_TILE
    nq = seq // t
    pairs = d_att // LANES

    def body(q_ref, k_ref, v_ref, um_ref, o_ref, q0, q1, kk, v0, v1):
        _head_copies(seq, q_ref, ATT_SCALE, q0, q1, None)
        _head_copies(seq, k_ref, 1.0, None, None, kk)
        _head_copies(seq, v_ref, 1.0, v0, v1, None)

        for head, (qs, vs) in enumerate(((q0, v0), (q1, v1))):
            def qblock(qb, carry, qs=qs, vs=vs, head=head):
                rows = pl.ds(pl.multiple_of(qb * t, t), t)
                q = qs[rows, :]

                def tile(kb, run, acc, masked):
                    krows = pl.ds(pl.multiple_of(kb * t, t), t)
                    f, a, causal = _att_tile_scores(q, kk[krows, :], masked)
                    scan = _split_scan(f, um_ref)
                    w = jnp.exp(a + run + scan[:, :t])
                    if masked:
                        w = jnp.where(causal, w, 0.0)
                    acc = acc + jnp.dot(w.astype(BF16), vs[krows, :], preferred_element_type=F32)
                    return run + scan[:, t:], acc

                run, acc = tile(qb, jnp.zeros((t, t), F32), jnp.zeros((t, LANES), F32), True)

                def cond(st):
                    return jnp.logical_and(st[0] >= 0, st[3] > 0)

                def wbody(st):
                    kb, run, acc, _ = st
                    run, acc = tile(kb, run, acc, False)
                    return kb - 1, run, acc, (jnp.max(run) > ATT_DEAD).astype(jnp.int32)

                st = (qb - 1, run, acc, (jnp.max(run) > ATT_DEAD).astype(jnp.int32))
                acc = lax.while_loop(cond, wbody, st)[2]
                if head == 0:
                    o_ref[rows, :] = acc
                else:
                    o_ref[rows, :] += acc
                return carry

            lax.fori_loop(0, nq, qblock, 0)

    blk = lambda base: pl.BlockSpec((seq, LANES), lambda j, base=base: (0, base + j))
    return pl.pallas_call(
        body, name=name, grid=(pairs,),
        in_specs=[blk(0), blk(pairs), blk(2 * pairs), pl.BlockSpec((2 * t, 2 * t), lambda j: (0, 0))],
        out_specs=pl.BlockSpec((seq, LANES), lambda j: (0, j)),
        out_shape=jax.ShapeDtypeStruct((seq, d_att), F32),
        scratch_shapes=[pltpu.VMEM((seq, LANES), BF16)] * 5,
        compiler_params=_cparams(("parallel",)),
    )(u, u, u, scan_suffix)


def _attn_bwd(u, d_att_out, scan_suffix, scan_prefix, *, seq, d_att, name):
    t = ATT_TILE
    nq = seq // t
    pairs = d_att // LANES

    def body(q_ref, k_ref, v_ref, do_ref, um_ref, pm_ref, dq_ref, dk_ref, dv_ref,
             q0, q1, kk, k0, k1, vv, do0, do1, dq_acc, dk_acc, dv_acc, g_st, b_st):
        _head_copies(seq, q_ref, ATT_SCALE, q0, q1, None)
        _head_copies(seq, k_ref, 1.0, k0, k1, kk)
        _head_copies(seq, v_ref, 1.0, None, None, vv)
        _head_copies(seq, do_ref, 1.0, do0, do1, None)
        dk_acc[...] = jnp.zeros_like(dk_acc)
        dv_acc[...] = jnp.zeros_like(dv_acc)

        for head, (qs, ks, dos) in enumerate(((q0, k0, do0), (q1, k1, do1))):
            def qblock(qb, carry, qs=qs, ks=ks, dos=dos, head=head):
                rows = pl.ds(pl.multiple_of(qb * t, t), t)
                q = qs[rows, :]
                do = dos[rows, :]

                def tile(kb, run, masked):
                    krows = pl.ds(pl.multiple_of(kb * t, t), t)
                    f, a, causal = _att_tile_scores(q, kk[krows, :], masked)
                    scan = _split_scan(f, um_ref)
                    w = jnp.exp(a + run + scan[:, :t])
                    if masked:
                        w = jnp.where(causal, w, 0.0)
                    dw = lax.dot_general(do, vv[krows, :], (((1,), (1,)), ((), ())), preferred_element_type=F32)
                    g_st[kb] = w * dw
                    b_st[kb] = jnp.exp(a)
                    dv_acc[krows, :] += lax.dot_general(w.astype(BF16), do, (((0,), (0,)), ((), ())),
                                                        preferred_element_type=F32)
                    return run + scan[:, t:]

                run = tile(qb, jnp.zeros((t, t), F32), True)

                def cond(st):
                    return jnp.logical_and(st[0] >= 0, st[2] > 0)

                def wbody(st):
                    kb, run, _ = st
                    run = tile(kb, run, False)
                    return kb - 1, run, (jnp.max(run) > ATT_DEAD).astype(jnp.int32)

                first = lax.while_loop(cond, wbody, (qb - 1, run, (jnp.max(run) > ATT_DEAD).astype(jnp.int32)))[0] + 1

                def back(kb, st):
                    grun, dq = st
                    krows = pl.ds(pl.multiple_of(kb * t, t), t)
                    g = g_st[kb]
                    scan = _split_scan(g, pm_ref)
                    dz = g - b_st[kb] * (g + grun + scan[:, :t])
                    dz = dz.astype(BF16)
                    dq = dq + jnp.dot(dz, ks[krows, :], preferred_element_type=F32)
                    dk_acc[krows, :] += lax.dot_general(dz, q, (((0,), (0,)), ((), ())), preferred_element_type=F32)
                    return grun + scan[:, t:], dq

                grun, dq = lax.fori_loop(first, qb, back, (jnp.zeros((t, t), F32), jnp.zeros((t, LANES), F32)))
                krows = rows
                g = g_st[qb]
                scan = _split_scan(g, pm_ref)
                causal = lax.broadcasted_iota(jnp.int32, (t, t), 1) < lax.broadcasted_iota(jnp.int32, (t, t), 0)
                dz = jnp.where(causal, g - b_st[qb] * (g + grun + scan[:, :t]), 0.0).astype(BF16)
                dq = dq + jnp.dot(dz, ks[krows, :], preferred_element_type=F32)
                dk_acc[krows, :] += lax.dot_general(dz, q, (((0,), (0,)), ((), ())), preferred_element_type=F32)
                if head == 0:
                    dq_acc[rows, :] = dq
                else:
                    dq_acc[rows, :] += dq
                return carry

            lax.fori_loop(0, nq, qblock, 0)

        chunk = min(256, seq)

        def emit(r, carry):
            rows = pl.ds(pl.multiple_of(r * chunk, chunk), chunk)
            dq_ref[rows, :] = (dq_acc[rows, :] * ATT_SCALE).astype(BF16)
            dk_ref[rows, :] = dk_acc[rows, :].astype(BF16)
            dv_ref[rows, :] = dv_acc[rows, :].astype(BF16)
            return carry

        lax.fori_loop(0, seq // chunk, emit, 0)

    blk = lambda base: pl.BlockSpec((seq, LANES), lambda j, base=base: (0, base + j))
    mat = pl.BlockSpec((2 * t, 2 * t), lambda j: (0, 0))
    out = jax.ShapeDtypeStruct((seq, d_att), BF16)
    return pl.pallas_call(
        body, name=name, grid=(pairs,),
        in_specs=[blk(0), blk(pairs), blk(2 * pairs), pl.BlockSpec((seq, LANES), lambda j: (0, j)), mat, mat],
        out_specs=[pl.BlockSpec((seq, LANES), lambda j: (0, j))] * 3,
        out_shape=[out, out, out],
        scratch_shapes=[pltpu.VMEM((seq, LANES), BF16)] * 8 + [pltpu.VMEM((seq, LANES), F32)] * 3
        + [pltpu.VMEM((nq, t, t), F32)] * 2,
        compiler_params=_cparams(("parallel",)),
    )(u, u, u, d_att_out, scan_suffix, scan_prefix)


CONV_ROWS = 256


def _shifted(window, residue, rows):
    total = rows + CONV_PAD
    return window if residue == 0 else pltpu.roll(window, total - residue, 0)


def _glu_to_pad(seq, a_ref, b_ref, pad_ref):
    chunk = min(CONV_ROWS, seq)
    pad_ref[pl.ds(0, CONV_PAD), :] = jnp.zeros((CONV_PAD, LANES), F32)

    def step(r, carry):
        rows = pl.ds(pl.multiple_of(r * chunk, chunk), chunk)
        pad_ref[pl.ds(pl.multiple_of(r * chunk + CONV_PAD, SUBLANES), chunk), :] = a_ref[rows, :] * _sigmoid(b_ref[rows, :])
        return carry

    lax.fori_loop(0, seq // chunk, step, 0)


def _conv_fwd(u, conv_w, conv_b, *, layer, seq, d_conv, col_a, col_b, name):
    blocks = d_conv // LANES
    rows_t = min(CONV_ROWS, seq)
    shift0 = CONV_PAD - (CONV_WIDTH - 1)

    def body(a_ref, b_ref, w_ref, bias_ref, o_ref, pad_ref):
        _glu_to_pad(seq, a_ref, b_ref, pad_ref)

        def step(r, carry):
            base = pl.multiple_of(r * rows_t, rows_t)
            window = pad_ref[pl.ds(base, rows_t + CONV_PAD), :]
            acc = jnp.zeros((rows_t, LANES), F32) + bias_ref[...]
            for residue in range(SUBLANES):
                moved = _shifted(window, residue, rows_t)
                for tap in range(CONV_WIDTH):
                    if (shift0 + tap) % SUBLANES == residue:
                        lo = (shift0 + tap) - residue
                        acc = acc + w_ref[tap:tap + 1, :] * moved[lo:lo + rows_t, :]
            o_ref[pl.ds(base, rows_t), :] = acc
            return carry

        lax.fori_loop(0, seq // rows_t, step, 0)

    return pl.pallas_call(
        body, name=name, grid=(blocks,),
        in_specs=[pl.BlockSpec((seq, LANES), lambda j: (0, col_a + j)), pl.BlockSpec((seq, LANES), lambda j: (0, col_b + j)),
                  pl.BlockSpec((None, CONV_PAD, LANES), lambda j: (layer, 0, j)),
                  pl.BlockSpec((None, 1, LANES), lambda j: (layer, 0, j))],
        out_specs=pl.BlockSpec((seq, LANES), lambda j: (0, j)),
        out_shape=jax.ShapeDtypeStruct((seq, d_conv), F32),
        scratch_shapes=[pltpu.VMEM((seq + CONV_PAD, LANES), F32)],
        compiler_params=_cparams(("parallel",)),
    )(u, u, conv_w, conv_b)


def _conv_bwd(u, dc1, conv_w, *, layer, seq, d_conv, col_a, col_b, name):
    blocks = d_conv // LANES
    rows_t = min(CONV_ROWS, seq)
    shift0 = CONV_PAD - (CONV_WIDTH - 1)

    def body(a_ref, b_ref, d_ref, w_ref, da_ref, db_ref, dw_ref, pad_ref, dpad_ref, dw_acc):
        _glu_to_pad(seq, a_ref, b_ref, pad_ref)
        dpad_ref[pl.ds(seq, CONV_PAD), :] = jnp.zeros((CONV_PAD, LANES), F32)

        def fill(r, carry):
            rows = pl.ds(pl.multiple_of(r * rows_t, rows_t), rows_t)
            dpad_ref[rows, :] = d_ref[rows, :]
            return carry

        lax.fori_loop(0, seq // rows_t, fill, 0)
        dw_acc[...] = jnp.zeros_like(dw_acc)

        def step(r, carry):
            base = pl.multiple_of(r * rows_t, rows_t)
            rows = pl.ds(base, rows_t)
            window = dpad_ref[pl.ds(base, rows_t + CONV_PAD), :]
            acc = jnp.zeros((rows_t, LANES), F32)
            for residue in range(SUBLANES):
                moved = _shifted(window, residue, rows_t)
                for tap in range(CONV_WIDTH):
                    off = CONV_WIDTH - 1 - tap
                    if off % SUBLANES == residue:
                        lo = off - residue
                        acc = acc + w_ref[tap:tap + 1, :] * moved[lo:lo + rows_t, :]
            sig = _sigmoid(b_ref[rows, :])
            a = a_ref[rows, :]
            da_ref[rows, :] = (acc * sig).astype(BF16)
            db_ref[rows, :] = (acc * a * sig * (1.0 - sig)).astype(BF16)
            dcur = d_ref[rows, :]
            cwin = pad_ref[pl.ds(base, rows_t + CONV_PAD), :]
            for residue in range(SUBLANES):
                moved = _shifted(cwin, residue, rows_t)
                for tap in range(CONV_WIDTH):
                    if (shift0 + tap) % SUBLANES == residue:
                        lo = (shift0 + tap) - residue
                        prod = dcur * moved[lo:lo + rows_t, :]
                        dw_acc[tap] += jnp.sum(prod.reshape(rows_t // SUBLANES, SUBLANES, LANES), axis=0)
            return carry

        lax.fori_loop(0, seq // rows_t, step, 0)
        dw_ref[...] = jnp.sum(dw_acc[...], axis=1)

    col = lambda base: pl.BlockSpec((seq, LANES), lambda j, base=base: (0, base + j))
    own = pl.BlockSpec((seq, LANES), lambda j: (0, j))
    return pl.pallas_call(
        body, name=name, grid=(blocks,),
        in_specs=[col(col_a), col(col_b), own, pl.BlockSpec((None, CONV_PAD, LANES), lambda j: (layer, 0, j))],
        out_specs=[own, own, pl.BlockSpec((CONV_PAD, LANES), lambda j: (0, j))],
        out_shape=[jax.ShapeDtypeStruct((seq, d_conv), BF16), jax.ShapeDtypeStruct((seq, d_conv), BF16),
                   jax.ShapeDtypeStruct((CONV_PAD, d_conv), F32)],
        scratch_shapes=[pltpu.VMEM((seq + CONV_PAD, LANES), F32), pltpu.VMEM((seq + CONV_PAD, LANES), F32),
                        pltpu.VMEM((CONV_PAD, SUBLANES, LANES), F32)],
        compiler_params=_cparams(("parallel",)),
    )(u, u, dc1, conv_w)


MIX_ROWS = 256


def _layer_norm_stats(val):
    mu = jnp.mean(val, axis=-1, keepdims=True)
    cen = val - mu
    var = jnp.mean(cen * cen, axis=-1, keepdims=True)
    rstd = lax.rsqrt(var + LN_EPS)
    return cen * rstd, rstd


def _layer_norm_bwd(dy, xhat, rstd, gain):
    dxhat = dy * gain
    m1 = jnp.mean(dxhat, axis=-1, keepdims=True)
    m2 = jnp.mean(dxhat * xhat, axis=-1, keepdims=True)
    dx = rstd * (dxhat - m1 - xhat * m2)
    return dx, jnp.sum(dy * xhat, axis=0, keepdims=True), jnp.sum(dy, axis=0, keepdims=True)


def _mix_forward(zatt, att, c1, zconv, gatt, gconv, x, w_att, w_conv, w_out, cln_g, cln_b, b_proj):
    s_zatt = _sigmoid(zatt)
    a_in = att * (zatt * s_zatt)
    chat, c_rstd = _layer_norm_stats(c1)
    c2 = chat * cln_g + cln_b
    s_c2 = _sigmoid(c2)
    c3 = c2 * s_c2
    s_zconv = _sigmoid(zconv)
    c_in = c3 * (zconv * s_zconv)
    a_in_b, c_in_b = a_in.astype(BF16), c_in.astype(BF16)
    ab = jnp.dot(a_in_b, w_att, preferred_element_type=F32)
    cb = jnp.dot(c_in_b, w_conv, preferred_element_type=F32) + b_proj
    s_gatt, s_gconv = _sigmoid(gatt), _sigmoid(gconv)
    merged_b = (s_gatt * ab + s_gconv * cb).astype(BF16)
    y = jnp.dot(merged_b, w_out, preferred_element_type=F32)
    h = DEEPNORM_ALPHA * x + y
    return dict(s_zatt=s_zatt, a_in_b=a_in_b, chat=chat, c_rstd=c_rstd, c2=c2, s_c2=s_c2, c3=c3, s_zconv=s_zconv,
                c_in_b=c_in_b, ab=ab, cb=cb, s_gatt=s_gatt, s_gconv=s_gconv, merged_b=merged_b, h=h)


def _u_blocks(rows_t, width, half):
    return [pl.BlockSpec((rows_t, half), lambda i, c=c: (i, c)) for c in (3, 6, 7, 8, 9, 10)]


def _of_layer(arr, layer):
    return pl.BlockSpec((None,) + arr.shape[1:], lambda i: (layer, 0, 0))


def _mix_fwd(u, att, c1, x, w_att, w_conv, w_out, cln_g, cln_b, b_proj, ln_g, ln_b, *, layer, seq, d_model, name):
    half = d_model // 2
    rows_t = min(MIX_ROWS, seq)

    def body(zatt_ref, zconv_ref, ga0, ga1, gc0, gc1, att_ref, c1_ref, x_ref, wa_ref, wc_ref, wo_ref,
             cg_ref, cb_ref, bp_ref, lg_ref, lb_ref, o_ref, ob_ref):
        gatt = jnp.concatenate([ga0[...], ga1[...]], axis=1)
        gconv = jnp.concatenate([gc0[...], gc1[...]], axis=1)
        mid = _mix_forward(zatt_ref[...], att_ref[...], c1_ref[...], zconv_ref[...], gatt, gconv, x_ref[...],
                           wa_ref[...], wc_ref[...], wo_ref[...], cg_ref[...], cb_ref[...], bp_ref[...])
        xhat, _ = _layer_norm_stats(mid["h"])
        out = xhat * lg_ref[...] + lb_ref[...]
        o_ref[...] = out
        ob_ref[...] = out.astype(BF16)

    row = lambda width: pl.BlockSpec((rows_t, width), lambda i: (i, 0))
    full = lambda arr: _of_layer(arr, layer)
    out = pl.BlockSpec((rows_t, d_model), lambda i: (i, 0))
    return pl.pallas_call(
        body, name=name, grid=(seq // rows_t,),
        in_specs=_u_blocks(rows_t, d_model, half) + [row(half), row(half), row(d_model), full(w_att), full(w_conv), full(w_out),
                                                     full(cln_g), full(cln_b), full(b_proj), full(ln_g), full(ln_b)],
        out_specs=[out, out],
        out_shape=[jax.ShapeDtypeStruct((seq, d_model), F32), jax.ShapeDtypeStruct((seq, d_model), BF16)],
        compiler_params=_cparams(("parallel",)),
    )(u, u, u, u, u, u, att, c1, x, w_att, w_conv, w_out, cln_g, cln_b, b_proj, ln_g, ln_b)


def _mix_bwd(u, att, c1, x, dxn, w_att, w_conv, w_out, cln_g, cln_b, b_proj, ln_g, *, layer, seq, d_model, name):
    half = d_model // 2
    rows_t = min(MIX_ROWS, seq)
    nt = ((1,), (1,))
    tn = ((0,), (0,))

    def body(zatt_ref, zconv_ref, ga0, ga1, gc0, gc1, att_ref, c1_ref, x_ref, dxn_ref, wa_ref, wc_ref, wo_ref,
             cg_ref, cb_ref, bp_ref, lg_ref,
             dzatt_ref, dzconv_ref, dgatt_ref, dgconv_ref, datt_ref, dc1_ref, dxres_ref, dwa_ref, dwc_ref, dwo_ref,
             dcg_ref, dcb_ref, dcbias_ref, dbp_ref, dlg_ref, dlb_ref):
        sums = (dwa_ref, dwc_ref, dwo_ref, dcg_ref, dcb_ref, dcbias_ref, dbp_ref, dlg_ref, dlb_ref)

        @pl.when(pl.program_id(0) == 0)
        def _():
            for ref in sums:
                ref[...] = jnp.zeros_like(ref)

        zatt, zconv, att = zatt_ref[...], zconv_ref[...], att_ref[...]
        gatt = jnp.concatenate([ga0[...], ga1[...]], axis=1)
        gconv = jnp.concatenate([gc0[...], gc1[...]], axis=1)
        wa, wc, wo = wa_ref[...], wc_ref[...], wo_ref[...]
        mid = _mix_forward(zatt, att, c1_ref[...], zconv, gatt, gconv, x_ref[...], wa, wc, wo,
                           cg_ref[...], cb_ref[...], bp_ref[...])
        xhat, rstd = _layer_norm_stats(mid["h"])
        dh, dlg, dlb = _layer_norm_bwd(dxn_ref[...], xhat, rstd, lg_ref[...])
        dlg_ref[...] += dlg
        dlb_ref[...] += dlb
        dxres_ref[...] = DEEPNORM_ALPHA * dh
        dy = dh.astype(BF16)
        dwo_ref[...] += lax.dot_general(mid["merged_b"], dy, (tn, ((), ())), preferred_element_type=F32)
        dmerged = lax.dot_general(dy, wo, (nt, ((), ())), preferred_element_type=F32)
        s_ga, s_gc, ab, cb = mid["s_gatt"], mid["s_gconv"], mid["ab"], mid["cb"]
        dgatt_ref[...] = (dmerged * ab * s_ga * (1.0 - s_ga)).astype(BF16)
        dgconv_ref[...] = (dmerged * cb * s_gc * (1.0 - s_gc)).astype(BF16)
        dab = dmerged * s_ga
        dcb = dmerged * s_gc
        dbp_ref[...] += jnp.sum(dcb, axis=0, keepdims=True)
        dab_b, dcb_b = dab.astype(BF16), dcb.astype(BF16)
        dwa_ref[...] += lax.dot_general(mid["a_in_b"], dab_b, (tn, ((), ())), preferred_element_type=F32)
        da_in = lax.dot_general(dab_b, wa, (nt, ((), ())), preferred_element_type=F32)
        s_za = mid["s_zatt"]
        datt_ref[...] = da_in * (zatt * s_za)
        dzatt_ref[...] = (da_in * att * (s_za * (1.0 + zatt * (1.0 - s_za)))).astype(BF16)
        dwc_ref[...] += lax.dot_general(mid["c_in_b"], dcb_b, (tn, ((), ())), preferred_element_type=F32)
        dc_in = lax.dot_general(dcb_b, wc, (nt, ((), ())), preferred_element_type=F32)
        s_zc, c2, s_c2 = mid["s_zconv"], mid["c2"], mid["s_c2"]
        dzconv_ref[...] = (dc_in * mid["c3"] * (s_zc * (1.0 + zconv * (1.0 - s_zc)))).astype(BF16)
        dc3 = dc_in * (zconv * s_zc)
        dc2 = dc3 * (s_c2 * (1.0 + c2 * (1.0 - s_c2)))
        dc1, dcg, dcbeta = _layer_norm_bwd(dc2, mid["chat"], mid["c_rstd"], cg_ref[...])
        dcg_ref[...] += dcg
        dcb_ref[...] += dcbeta
        dcbias_ref[...] += jnp.sum(dc1, axis=0, keepdims=True)
        dc1_ref[...] = dc1

    row = lambda width: pl.BlockSpec((rows_t, width), lambda i: (i, 0))
    full = lambda arr: _of_layer(arr, layer)
    whole = lambda r, c: pl.BlockSpec((r, c), lambda i: (0, 0))
    sds = jax.ShapeDtypeStruct
    out_specs = [row(half), row(half), row(d_model), row(d_model), row(half), row(half), row(d_model),
                 whole(half, d_model), whole(half, d_model), whole(d_model, d_model),
                 whole(1, half), whole(1, half), whole(1, half), whole(1, d_model), whole(1, d_model), whole(1, d_model)]
    out_shape = [sds((seq, half), BF16), sds((seq, half), BF16), sds((seq, d_model), BF16), sds((seq, d_model), BF16),
                 sds((seq, half), F32), sds((seq, half), F32), sds((seq, d_model), F32),
                 sds((half, d_model), F32), sds((half, d_model), F32), sds((d_model, d_model), F32),
                 sds((1, half), F32), sds((1, half), F32), sds((1, half), F32),
                 sds((1, d_model), F32), sds((1, d_model), F32), sds((1, d_model), F32)]
    return pl.pallas_call(
        body, name=name, grid=(seq // rows_t,),
        in_specs=_u_blocks(rows_t, d_model, half) + [row(half), row(half), row(d_model), row(d_model), full(w_att), full(w_conv),
                                                     full(w_out), full(cln_g), full(cln_b), full(b_proj), full(ln_g)],
        out_specs=out_specs, out_shape=out_shape,
        compiler_params=_cparams(("arbitrary",)),
    )(u, u, u, u, u, u, att, c1, x, dxn, w_att, w_conv, w_out, cln_g, cln_b, b_proj, ln_g)


def _loss_head(y, target, *, seq, d_model, name):
    rows_t = min(512, seq)

    def body(y_ref, t_ref, dy_ref, loss_ref):
        @pl.when(pl.program_id(0) == 0)
        def _():
            loss_ref[...] = jnp.zeros_like(loss_ref)

        err = y_ref[...] - t_ref[...]
        dy_ref[...] = err * (1.0 / d_model)
        per_token = jnp.sum(err * err, axis=-1, keepdims=True) * (1.0 / d_model)
        loss_ref[...] += 0.5 * jnp.sum(per_token, axis=0, keepdims=True)

    row = pl.BlockSpec((rows_t, d_model), lambda i: (i, 0))
    return pl.pallas_call(
        body, name=name, grid=(seq // rows_t,), in_specs=[row, row],
        out_specs=[row, pl.BlockSpec((1, 1), lambda i: (0, 0))],
        out_shape=[jax.ShapeDtypeStruct((seq, d_model), F32), jax.ShapeDtypeStruct((1, 1), F32)],
        compiler_params=_cparams(("arbitrary",)),
    )(y, target)


def _adamw(w, g, m, v, *, name):
    rows, cols = w.shape
    rows_t = rows
    for cand in (512, 256, 128, 64, 32, 16, 8):
        if rows % cand == 0 and cand * cols * 4 <= 2 * 1024 * 1024:
            rows_t = cand
            break

    def body(w_ref, g_ref, m_ref, v_ref, d_ref, nm_ref, nv_ref):
        grad = g_ref[...]
        new_m = ADAM_B1 * m_ref[...] + (1.0 - ADAM_B1) * grad
        new_v = ADAM_B2 * v_ref[...] + (1.0 - ADAM_B2) * (grad * grad)
        m_hat = new_m / (1.0 - ADAM_B1 ** ADAM_STEP)
        v_hat = new_v / (1.0 - ADAM_B2 ** ADAM_STEP)
        d_ref[...] = -ADAM_LR * (m_hat / (jnp.sqrt(v_hat) + ADAM_EPS) + ADAM_WD * w_ref[...])
        nm_ref[...] = new_m
        nv_ref[...] = new_v

    blk = pl.BlockSpec((rows_t, cols), lambda i: (i, 0))
    out = jax.ShapeDtypeStruct((rows, cols), F32)
    return pl.pallas_call(
        body, name=name, grid=(rows // rows_t,), in_specs=[blk] * 4, out_specs=[blk] * 3, out_shape=[out] * 3,
        compiler_params=_cparams(("parallel",)),
    )(w, g, m, v)


def _local_grads(x, target, w_in, b_in, conv_w, conv_b, cln_g, cln_b, w_att, w_conv, b_proj, w_out, ln_g, ln_b):
    seq, d_model = x.shape
    half = d_model // 2
    depth = w_in.shape[0]
    scan_suffix, scan_prefix = _scan_matrices()
    cols = half // LANES
    dims = dict(seq=seq, d_model=d_model)
    conv_dims = dict(seq=seq, d_conv=half, col_a=4 * cols, col_b=5 * cols)

    xs, xbs, us, atts, c1s = [x], [x.astype(BF16)], [], [], []
    for l in range(depth):
        u = _matmul(xbs[l], w_in, layer=l, mode="nn", bias=b_in[l].reshape(1, -1), name="in_proj")
        att = _attn_fwd(u, scan_suffix, seq=seq, d_att=half, name="attn_fwd")
        c1 = _conv_fwd(u, conv_w, conv_b, layer=l, name="conv_fwd", **conv_dims)
        xn, xnb = _mix_fwd(u, att, c1, xs[l], w_att, w_conv, w_out, cln_g, cln_b, b_proj, ln_g, ln_b, layer=l, name="mix_fwd", **dims)
        us.append(u)
        atts.append(att)
        c1s.append(c1)
        xs.append(xn)
        xbs.append(xnb)

    dx, loss = _loss_head(xs[depth], target, name="loss_head", **dims)
    grads = [None] * depth
    for l in reversed(range(depth)):
        u = us[l]
        (dzatt, dzconv, dgatt, dgconv, datt, dc1, dxres, dwa, dwc, dwo, dcg, dcb, dcbias, dbp, dlg, dlb) = _mix_bwd(
            u, atts[l], c1s[l], xs[l], dx, w_att, w_conv, w_out, cln_g, cln_b, b_proj, ln_g, layer=l, name="mix_bwd", **dims)
        dq, dk, dv = _attn_bwd(u, datt, scan_suffix, scan_prefix, seq=seq, d_att=half, name="attn_bwd")
        dglu_a, dglu_b, dconvw = _conv_bwd(u, dc1, conv_w, layer=l, name="conv_bwd", **conv_dims)
        du = jnp.concatenate([dq, dk, dv, dzatt, dglu_a, dglu_b, dzconv, dgatt, dgconv], axis=1)
        dwin, dbin = _matmul(xbs[l], du, mode="tn", colsum=True, name="in_proj_dw")
        dx = _matmul(du, w_in, layer=l, mode="nt", add=dxres, name="in_proj_dx", tn=1024, tk=512)
        grads[l] = dict(w_in=dwin, b_in=dbin[0], conv_w=dconvw[:CONV_WIDTH], conv_b=dcbias[0], conv_ln_g=dcg[0], conv_ln_b=dcb[0],
                        w_att_proj=dwa, w_conv_proj=dwc, b_conv_proj=dbp[0], w_out=dwo, ln_g=dlg[0], ln_b=dlb[0])
    return loss, dx, grads


MESH = pl.DeviceIdType.MESH
IN_HBM = pl.BlockSpec(memory_space=pl.ANY)


def _position():
    x, y, c = lax.axis_index("x"), lax.axis_index("y"), lax.axis_index("c")
    return x, y, c, [(1 - x, y), (x, 1 - y), (1 - x, 1 - y)]


def _cut(ref, axis, start, size):
    idx = [slice(None)] * len(ref.shape)
    idx[axis] = pl.ds(start, size)
    return ref.at[tuple(idx)]


def _remote(src, dst, send_sem, recv_sem, device):
    return pltpu.make_async_remote_copy(src_ref=src, dst_ref=dst, send_sem=send_sem, recv_sem=recv_sem,
                                        device_id=device, device_id_type=MESH)


def _comm_call(body, name, operands, out_shape, scratch):
    return pl.pallas_call(
        body, name=name, in_specs=[IN_HBM] * len(operands), out_specs=[IN_HBM] * len(out_shape), out_shape=out_shape,
        scratch_shapes=scratch,
    )(*operands)


def _gather_weights(shards, chip_axes):
    n = len(shards)
    depth = shards[0].shape[0]
    hl = depth // 2
    wholes = []
    for shard, axis in zip(shards, chip_axes):
        shape = list(shard.shape)
        shape[axis] *= N_CHIPS
        wholes.append(jax.ShapeDtypeStruct(tuple(shape), shard.dtype))

    def body(*refs):
        src, dst = refs[:n], refs[n:2 * n]
        ici_send, ici_recv, pair_send, pair_recv, local_sem = refs[2 * n:]
        x, y, c, chips = _position()
        me = 2 * x + y

        def block(t, chip, lo, count):
            size = shards[t].shape[chip_axes[t]]
            return _cut(_cut(dst[t], chip_axes[t], pl.multiple_of(chip * size, size), size), 0, lo, count)

        mine = [pltpu.make_async_copy(src[t], block(t, me, 0, depth), local_sem.at[t]) for t in range(n)]
        for cp in mine:
            cp.start()
        lo, other_lo = c * hl, (1 - c) * hl
        sends = []
        for t in range(n):
            for j, chip in enumerate(chips):
                cp = _remote(_cut(src[t], 0, lo, hl), block(t, me, lo, hl), ici_send.at[t, j], ici_recv.at[t, j], (*chip, c))
                cp.start()
                sends.append(cp)
        for t in range(n):
            for j, chip in enumerate(chips):
                landed = block(t, 2 * chip[0] + chip[1], lo, hl)
                _remote(landed, landed, ici_send.at[t, j], ici_recv.at[t, j], (*chip, c)).wait_recv()
                cp = _remote(landed, landed, pair_send.at[t, j], pair_recv.at[t, j], (x, y, 1 - c))
                cp.start()
                sends.append(cp)
        for t in range(n):
            for j, chip in enumerate(chips):
                passed = block(t, 2 * chip[0] + chip[1], other_lo, hl)
                _remote(passed, passed, pair_send.at[t, j], pair_recv.at[t, j], (x, y, 1 - c)).wait_recv()
        for cp in sends:
            cp.wait_send()
        for cp in mine:
            cp.wait()

    sems = [pltpu.SemaphoreType.DMA((n, 3))] * 4 + [pltpu.SemaphoreType.DMA((n,))]
    return _comm_call(body, "gather_weights", shards, wholes, sems)


def _pair_swap(parts, core_axes):
    n, depth = len(parts), len(parts[0])
    flat = [arr for per_layer in parts for arr in per_layer]
    halves = []
    for per_layer, axis in zip(parts, core_axes):
        shape = list(per_layer[0].shape)
        shape[axis] //= 2
        halves.append(jax.ShapeDtypeStruct((depth, *shape), per_layer[0].dtype))

    def body(*refs):
        src, dst = refs[:n * depth], refs[n * depth:n * depth + n]
        send_sem, recv_sem = refs[n * depth + n:]
        x, y, c, _ = _position()
        copies = []
        for t in range(n):
            size = halves[t].shape[1 + core_axes[t]]
            for l in range(depth):
                piece = _cut(src[t * depth + l], core_axes[t], pl.multiple_of((1 - c) * size, size), size)
                cp = _remote(piece, dst[t].at[l], send_sem.at[t, l], recv_sem.at[t, l], (x, y, 1 - c))
                cp.start()
                copies.append(cp)
        for cp in copies:
            cp.wait()

    return _comm_call(body, "pair_swap", flat, halves, [pltpu.SemaphoreType.DMA((n, depth))] * 2)


def _chip_exchange(partials, chip_axes):
    n = len(partials)
    quarters = []
    for arr, axis in zip(partials, chip_axes):
        shape = list(arr.shape)
        shape[axis] //= N_CHIPS
        quarters.append(jax.ShapeDtypeStruct((N_CHIPS - 1, *shape), arr.dtype))

    def body(*refs):
        src, dst = refs[:n], refs[n:2 * n]
        send_sem, recv_sem = refs[2 * n:]
        x, y, c, chips = _position()
        copies = []
        for t in range(n):
            size = quarters[t].shape[1 + chip_axes[t]]
            for j, chip in enumerate(chips):
                piece = _cut(src[t], chip_axes[t], pl.multiple_of((2 * chip[0] + chip[1]) * size, size), size)
                cp = _remote(piece, dst[t].at[j], send_sem.at[t, j], recv_sem.at[t, j], (*chip, c))
                cp.start()
                copies.append(cp)
        for cp in copies:
            cp.wait()

    return _comm_call(body, "chip_exchange", partials, quarters, [pltpu.SemaphoreType.DMA((n, N_CHIPS - 1))] * 2)


def _pair_join(finals, core_axes):
    n = len(finals)
    wholes = []
    for arr, axis in zip(finals, core_axes):
        shape = list(arr.shape)
        shape[axis] *= 2
        wholes.append(jax.ShapeDtypeStruct(tuple(shape), arr.dtype))

    def body(*refs):
        src, dst = refs[:n], refs[n:2 * n]
        send_sem, recv_sem, local_sem = refs[2 * n:]
        x, y, c, _ = _position()
        copies, mine = [], []
        for t in range(n):
            size = finals[t].shape[core_axes[t]]
            slot = _cut(dst[t], core_axes[t], pl.multiple_of(c * size, size), size)
            cp = pltpu.make_async_copy(src[t], slot, local_sem.at[t])
            cp.start()
            mine.append(cp)
            cp = _remote(src[t], slot, send_sem.at[t], recv_sem.at[t], (x, y, 1 - c))
            cp.start()
            copies.append(cp)
        for t, cp in enumerate(copies):
            size = finals[t].shape[core_axes[t]]
            other = _cut(dst[t], core_axes[t], pl.multiple_of((1 - c) * size, size), size)
            cp.wait_send()
            _remote(src[t], other, send_sem.at[t], recv_sem.at[t], (x, y, 1 - c)).wait_recv()
        for cp in mine:
            cp.wait()

    return _comm_call(body, "pair_join", finals, wholes, [pltpu.SemaphoreType.DMA((n,))] * 3)


def _gather_small(vec):
    n_dev = 2 * N_CHIPS

    def body(src, dst, send_sem, recv_sem, local_sem):
        x, y, c, _ = _position()
        flip = lambda v, bit: 1 - v if bit else v
        mine = pltpu.make_async_copy(src, dst.at[4 * x + 2 * y + c], local_sem)
        mine.start()
        copies = []
        for mask in range(1, n_dev):
            peer = (flip(x, mask & 4), flip(y, mask & 2), flip(c, mask & 1))
            cp = _remote(src, dst.at[4 * x + 2 * y + c], send_sem.at[mask - 1], recv_sem.at[mask - 1], peer)
            cp.start()
            copies.append((cp, peer))
        for mask, (cp, peer) in enumerate(copies, start=1):
            theirs = dst.at[4 * peer[0] + 2 * peer[1] + peer[2]]
            _remote(src, theirs, send_sem.at[mask - 1], recv_sem.at[mask - 1], peer).wait_recv()
        for cp, _ in copies:
            cp.wait_send()
        mine.wait()

    out = [jax.ShapeDtypeStruct((n_dev, *vec.shape), vec.dtype)]
    return _comm_call(body, "gather_small", [vec], out, [pltpu.SemaphoreType.DMA((n_dev - 1,))] * 2 + [pltpu.SemaphoreType.DMA(())])[0]


def _row_tile(rows, row_bytes, budget):
    tile = rows
    for cand in (512, 256, 128, 64, 32, 16, 8):
        if rows % cand == 0:
            tile = cand
            if cand * row_bytes <= budget:
                break
    return tile


def _pair_sum(parts, got, core_axis, place, *, name):
    depth, rows, cols = got.shape
    rows_t = _row_tile(rows, cols * 4 * depth * 6, 24 * 1024 * 1024)
    steps = rows // rows_t
    if core_axis == 0:
        part_spec = pl.BlockSpec((rows_t, cols), lambda i, p: (p[0] * steps + i, 0))
    else:
        part_spec = pl.BlockSpec((rows_t, cols), lambda i, p: (i, p[0]))
    stack_spec = pl.BlockSpec((depth, rows_t, cols), lambda i, p: (0, i, 0))

    def body(place_ref, *refs):
        got_ref, out_ref = refs[depth], refs[depth + 1]
        for l in range(depth):
            out_ref[l] = refs[l][...] + got_ref[l]

    return pl.pallas_call(
        body, name=name, out_shape=jax.ShapeDtypeStruct(got.shape, F32),
        grid_spec=pltpu.PrefetchScalarGridSpec(num_scalar_prefetch=1, grid=(steps,), in_specs=[part_spec] * depth + [stack_spec],
                                               out_specs=stack_spec),
        compiler_params=_cparams(("arbitrary",)),
    )(place, *parts, got)


def _chip_sum(partial, got, chip_axis, place, *, name):
    _, depth, rows, cols = got.shape
    rows_t = _row_tile(rows, cols * 4 * 10, 24 * 1024 * 1024)
    steps = rows // rows_t
    if chip_axis == 1:
        own_spec = pl.BlockSpec((None, rows_t, cols), lambda l, i, p: (l, p[1] * steps + i, 0))
    else:
        own_spec = pl.BlockSpec((None, rows_t, cols), lambda l, i, p: (l, i, p[1]))
    got_spec = pl.BlockSpec((N_CHIPS - 1, None, rows_t, cols), lambda l, i, p: (0, l, i, 0))

    def body(place_ref, own_ref, got_ref, out_ref):
        out_ref[...] = ((own_ref[...] + got_ref[0]) + got_ref[1]) + got_ref[2]

    return pl.pallas_call(
        body, name=name, out_shape=jax.ShapeDtypeStruct(got.shape[1:], F32),
        grid_spec=pltpu.PrefetchScalarGridSpec(num_scalar_prefetch=1, grid=(depth, steps), in_specs=[own_spec, got_spec],
                                               out_specs=pl.BlockSpec((None, rows_t, cols), lambda l, i, p: (l, i, 0))),
        compiler_params=_cparams(("arbitrary", "arbitrary")),
    )(place, partial, got)


def _sum_devices(stack, *, name):
    def body(src_ref, out_ref):
        total = src_ref[0]
        for d in range(1, stack.shape[0]):
            total = total + src_ref[d]
        out_ref[...] = total

    return pl.pallas_call(body, name=name, out_shape=jax.ShapeDtypeStruct(stack.shape[1:], F32))(stack)


def kernel(x, w_in, b_in, conv_w, conv_b, conv_ln_g, conv_ln_b, w_att_proj, w_conv_proj, b_conv_proj, w_out, ln_g, ln_b, loss_target, m_w_in, m_b_in, m_conv_w, m_conv_b, m_conv_ln_g, m_conv_ln_b, m_w_att_proj, m_w_conv_proj, m_b_conv_proj, m_w_out, m_ln_g, m_ln_b, v_w_in, v_b_in, v_conv_w, v_conv_b, v_conv_ln_g, v_conv_ln_b, v_w_att_proj, v_w_conv_proj, v_b_conv_proj, v_w_out, v_ln_g, v_ln_b):
    depth = w_in.shape[0]
    d_model = x.shape[-1]
    half = d_model // 2
    chip = 2 * lax.axis_index("x") + lax.axis_index("y")
    place = jnp.stack([lax.axis_index("c"), chip]).astype(jnp.int32)
    vec3 = lambda v: v.reshape(depth, 1, -1)

    taps = jnp.pad(conv_w, ((0, 0), (0, CONV_PAD - CONV_WIDTH), (0, 0)))
    win, watt, wconv, wout, convw = _gather_weights(
        [w_in.astype(BF16), w_att_proj.astype(BF16), w_conv_proj.astype(BF16), w_out.astype(BF16), taps], [2, 2, 2, 1, 2])
    loss, grad_x, grads = _local_grads(x[0], loss_target[0], win, b_in, convw, vec3(conv_b), vec3(conv_ln_g), vec3(conv_ln_b),
                                       watt, wconv, vec3(b_conv_proj), wout, vec3(ln_g), vec3(ln_b))
    loss = lax.psum(loss[0, 0], ("x", "y", "c"))

    big = ["w_in", "w_att_proj", "w_conv_proj", "w_out"]
    core_axes, chip_axes = [0, 0, 0, 1], [1, 1, 1, 0]
    parts = [[grads[l][n] for l in range(depth)] for n in big]
    got = _pair_swap(parts, core_axes)
    partials = [_pair_sum(parts[t], got[t], core_axes[t], place, name="pair_sum_" + big[t]) for t in range(len(big))]
    got = _chip_exchange(partials, [a + 1 for a in chip_axes])
    finals = [_chip_sum(partials[t], got[t], chip_axes[t] + 1, place, name="chip_sum_" + big[t]) for t in range(len(big))]
    reduced = dict(zip(big, _pair_join(finals, [a + 1 for a in core_axes])))

    small = ["b_in", "conv_b", "conv_ln_g", "conv_ln_b", "b_conv_proj", "ln_g", "ln_b"]
    packed = jnp.stack([jnp.concatenate([grads[l][n] for n in small] + [grads[l]["conv_w"].reshape(-1)]) for l in range(depth)])
    total = _sum_devices(_gather_small(packed), name="sum_devices")
    offset = 0
    for n in small:
        width = grads[0][n].shape[0]
        reduced[n] = total[:, offset:offset + width]
        offset += width
    taps = total[:, offset:].reshape(depth, CONV_WIDTH, half)
    reduced["conv_w"] = lax.dynamic_slice_in_dim(taps, chip * conv_w.shape[2], conv_w.shape[2], axis=2)

    names = ["w_in", "b_in", "conv_w", "conv_b", "conv_ln_g", "conv_ln_b", "w_att_proj", "w_conv_proj", "b_conv_proj", "w_out", "ln_g", "ln_b"]
    weights = dict(zip(names, (w_in, b_in, conv_w, conv_b, conv_ln_g, conv_ln_b, w_att_proj, w_conv_proj, b_conv_proj, w_out, ln_g, ln_b)))
    first = dict(zip(names, (m_w_in, m_b_in, m_conv_w, m_conv_b, m_conv_ln_g, m_conv_ln_b, m_w_att_proj, m_w_conv_proj, m_b_conv_proj, m_w_out, m_ln_g, m_ln_b)))
    second = dict(zip(names, (v_w_in, v_b_in, v_conv_w, v_conv_b, v_conv_ln_g, v_conv_ln_b, v_w_att_proj, v_w_conv_proj, v_b_conv_proj, v_w_out, v_ln_g, v_ln_b)))
    delta, new_m, new_v = {}, {}, {}
    for n in names:
        shape = weights[n].shape
        flat = lambda arr: arr.reshape(-1, shape[-1])
        d, m, v = _adamw(flat(weights[n]), flat(reduced[n]), flat(first[n]), flat(second[n]), name="adamw_" + n)
        delta[n], new_m[n], new_v[n] = d.reshape(shape), m.reshape(shape), v.reshape(shape)
    return (loss, grad_x[None], *[reduced[n].reshape(weights[n].shape) for n in names], *[delta[n] for n in names],
            *[new_m[n] for n in names], *[new_v[n] for n in names])
```

```python
import functools
import math

import jax
import jax.numpy as jnp
from jax import lax
from jax.experimental import pallas as pl
from jax.experimental.pallas import tpu as pltpu

F32 = jnp.float32
BF16 = jnp.bfloat16

HEAD_DIM = 64
LANES = 128
CONV_WIDTH = 31
CONV_PAD = 32
SUBLANES = 8
LN_EPS = 1e-5
DEPTH = 4
DEEPNORM_ALPHA = (2 * DEPTH) ** 0.25
ATT_SCALE = HEAD_DIM ** -0.5
ATT_TILE = 128
ATT_DEAD = -104.0
ATT_GROUP = 2
ATT_FILL = -1e30

ADAM_LR = 0.001
ADAM_B1 = 0.9
ADAM_B2 = 0.999
ADAM_EPS = 1e-08
ADAM_WD = 0.01
ADAM_STEP = 10

VMEM_LIMIT = 56 * 1024 * 1024

N_CHIPS = 4


def _cparams(sem):
    return pltpu.CompilerParams(dimension_semantics=sem, vmem_limit_bytes=VMEM_LIMIT)


def _sigmoid(x):
    return 1.0 / (1.0 + jnp.exp(-x))


def _fit(tile, dim):
    assert dim % LANES == 0
    tile = min(tile, dim) // LANES * LANES
    while dim % tile:
        tile -= LANES
    return tile


_DIMS = {"nn": ((1,), (0,)), "nt": ((1,), (1,)), "tn": ((0,), (0,))}


def _matmul(a, b, *, mode, name, layer=None, bias=None, add=None, colsum=False, out_dtype=F32, tm=1024, tn=512, tk=1024):
    b_shape = b.shape if layer is None else b.shape[1:]
    if mode == "nn":
        (m, k), (k2, n) = a.shape, b_shape
    elif mode == "nt":
        (m, k), (n, k2) = a.shape, b_shape
    else:
        (k, m), (k2, n) = a.shape, b_shape
    assert k == k2
    tm, tn, tk = _fit(tm, m), _fit(tn, n), _fit(tk, k)
    gm, gn, nk = m // tm, n // tn, k // tk

    a_spec = pl.BlockSpec((tk, tm), lambda i, j, kk: (kk, i)) if mode == "tn" else pl.BlockSpec((tm, tk), lambda i, j, kk: (i, kk))
    if layer is None:
        b_spec = pl.BlockSpec((tn, tk), lambda i, j, kk: (j, kk)) if mode == "nt" else pl.BlockSpec((tk, tn), lambda i, j, kk: (kk, j))
    elif mode == "nt":
        b_spec = pl.BlockSpec((None, tn, tk), lambda i, j, kk: (layer, j, kk))
    else:
        b_spec = pl.BlockSpec((None, tk, tn), lambda i, j, kk: (layer, kk, j))
    in_specs, operands = [a_spec, b_spec], [a, b]
    if bias is not None:
        in_specs.append(pl.BlockSpec((1, tn), lambda i, j, kk: (0, j)))
        operands.append(bias)
    if add is not None:
        in_specs.append(pl.BlockSpec((tm, tn), lambda i, j, kk: (i, j)))
        operands.append(add)
    out_shape = [jax.ShapeDtypeStruct((m, n), out_dtype)]
    out_specs = [pl.BlockSpec((tm, tn), lambda i, j, kk: (i, j))]
    scratch = [pltpu.VMEM((tm, tn), F32)]
    if colsum:
        assert mode == "tn"
        out_shape.append(jax.ShapeDtypeStruct((gm, 1, n), F32))
        out_specs.append(pl.BlockSpec((1, 1, tn), lambda i, j, kk: (i, 0, j)))
        scratch.append(pltpu.VMEM((1, tn), F32))
    has_bias, has_add = bias is not None, add is not None

    def body(*refs):
        refs = list(refs)
        a_ref, b_ref = refs[0], refs[1]
        pos = 2
        bias_ref = add_ref = None
        if has_bias:
            bias_ref = refs[pos]
            pos += 1
        if has_add:
            add_ref = refs[pos]
            pos += 1
        o_ref = refs[pos]
        pos += 1
        cs_ref = None
        if colsum:
            cs_ref = refs[pos]
            pos += 1
        acc_ref = refs[pos]
        cs_acc = refs[pos + 1] if colsum else None
        kk = pl.program_id(2)

        @pl.when(kk == 0)
        def _():
            acc_ref[...] = jnp.zeros_like(acc_ref)
            if colsum:
                cs_acc[...] = jnp.zeros_like(cs_acc)

        bv = b_ref[...]
        acc_ref[...] += lax.dot_general(a_ref[...].astype(BF16), bv.astype(BF16), (_DIMS[mode], ((), ())),
                                        preferred_element_type=F32)
        if colsum:
            cs_acc[...] += jnp.sum(bv.astype(F32), axis=0, keepdims=True)

        @pl.when(kk == nk - 1)
        def _():
            out = acc_ref[...]
            if has_bias:
                out = out + bias_ref[...]
            if has_add:
                out = out + add_ref[...]
            o_ref[...] = out.astype(out_dtype)
            if colsum:
                cs_ref[0] = cs_acc[...]

    res = pl.pallas_call(
        body, name=name, grid=(gm, gn, nk), in_specs=in_specs, out_specs=out_specs, out_shape=out_shape,
        scratch_shapes=scratch, compiler_params=_cparams(("parallel", "parallel", "arbitrary")),
    )(*operands)
    if colsum:
        return res[0], res[1][0]
    return res[0]


def _scan_matrices():
    t = ATT_TILE
    r = lax.broadcasted_iota(jnp.int32, (t, t), 0)
    c = lax.broadcasted_iota(jnp.int32, (t, t), 1)
    ones = jnp.ones((t, t), F32)
    suffix = jnp.concatenate([(r > c).astype(F32), ones], axis=1)
    prefix = jnp.concatenate([(r < c).astype(F32), ones], axis=1)
    stack = lambda mat: jnp.concatenate([mat, mat], axis=0).astype(BF16)
    return stack(suffix), stack(prefix)


def _split_scan(val, mat_ref):
    hi = val.astype(BF16)
    lo = (val - hi.astype(F32)).astype(BF16)
    return jnp.dot(jnp.concatenate([hi, lo], axis=1), mat_ref[...], preferred_element_type=F32)


def _att_tile_scores(q, k, masked):
    t = ATT_TILE
    z = lax.dot_general(q, k, (((1,), (1,)), ((), ())), preferred_element_type=F32)
    sp = jnp.log(1.0 + jnp.exp(-jnp.abs(z)))
    f = jnp.minimum(-z, 0.0) - sp
    a = f + z
    causal = None
    if masked:
        causal = lax.broadcasted_iota(jnp.int32, (t, t), 1) < lax.broadcasted_iota(jnp.int32, (t, t), 0)
        f = jnp.where(causal, f, 0.0)
    return f, a, causal


def _any_alive(runs):
    top = functools.reduce(jnp.maximum, [run for per_head in runs for run in per_head])
    return (jnp.max(top) > ATT_DEAD).astype(jnp.int32)


def _head_copies(seq, src_ref, scale, lo_ref, hi_ref, plain_ref):
    chunk = min(256, seq)
    low = lax.broadcasted_iota(jnp.int32, (chunk, LANES), 1) < HEAD_DIM

    def step(r, carry):
        rows = pl.ds(pl.multiple_of(r * chunk, chunk), chunk)
        val = src_ref[rows, :]
        if scale != 1.0:
            val = val * scale
        if lo_ref is not None:
            lo_ref[rows, :] = jnp.where(low, val, 0.0).astype(BF16)
            hi_ref[rows, :] = jnp.where(low, 0.0, val).astype(BF16)
        if plain_ref is not None:
            plain_ref[rows, :] = val.astype(BF16)
        return carry

    lax.fori_loop(0, seq // chunk, step, 0)


def _attn_fwd(u, scan_suffix, *, seq, d_att, name):
    t = ATT_TILE
    nq = seq // t
    pairs = d_att // LANES
    grp = ATT_GROUP
    assert nq % grp == 0

    def body(q_ref, k_ref, v_ref, um_ref, o_ref, q0, q1, kk, v0, v1):
        _head_copies(seq, q_ref, ATT_SCALE, q0, q1, None)
        _head_copies(seq, k_ref, 1.0, None, None, kk)
        _head_copies(seq, v_ref, 1.0, v0, v1, None)

        heads = ((q0, v0), (q1, v1))

        def group(gi, carry):
            qb0 = gi * grp
            qrows = [pl.ds(pl.multiple_of((qb0 + g) * t, t), t) for g in range(grp)]
            qv = [[qs[qrows[g], :] for g in range(grp)] for qs, _ in heads]

            def step(i, runs, accs, masked):
                runs = [list(r) for r in runs]
                accs = list(accs)
                for g in range(grp):
                    kb = qb0 + g - i
                    krows = pl.ds(pl.multiple_of(jnp.maximum(kb, 0) * t, t), t)
                    keys = kk[krows, :]
                    for h, (_, vs) in enumerate(heads):
                        run = runs[h][g] if masked else runs[h][g] + jnp.where(kb >= 0, 0.0, ATT_FILL)
                        f, a, causal = _att_tile_scores(qv[h][g], keys, masked)
                        scan = _split_scan(f, um_ref)
                        w = jnp.exp(a + run + scan[:, :t])
                        if masked:
                            w = jnp.where(causal, w, 0.0)
                        accs[g] = accs[g] + jnp.dot(w.astype(BF16), vs[krows, :], preferred_element_type=F32)
                        runs[h][g] = run + scan[:, t:]
                return runs, accs

            runs = [[jnp.zeros((t, t), F32)] * grp] * 2
            runs, accs = step(0, runs, [jnp.zeros((t, LANES), F32)] * grp, True)

            def cond(st):
                return jnp.logical_and(st[0] <= qb0 + grp - 1, st[3] > 0)

            def wbody(st):
                i, runs, accs, _ = st
                runs, accs = step(i, runs, accs, False)
                return i + 1, runs, accs, _any_alive(runs)

            accs = lax.while_loop(cond, wbody, (1, runs, accs, _any_alive(runs)))[2]
            for g in range(grp):
                o_ref[qrows[g], :] = accs[g]
            return carry

        lax.fori_loop(0, nq // grp, group, 0)

    blk = lambda base: pl.BlockSpec((seq, LANES), lambda j, base=base: (0, base + j))
    return pl.pallas_call(
        body, name=name, grid=(pairs,),
        in_specs=[blk(0), blk(pairs), blk(2 * pairs), pl.BlockSpec((2 * t, 2 * t), lambda j: (0, 0))],
        out_specs=pl.BlockSpec((seq, LANES), lambda j: (0, j)),
        out_shape=jax.ShapeDtypeStruct((seq, d_att), F32),
        scratch_shapes=[pltpu.VMEM((seq, LANES), BF16)] * 5,
        compiler_params=_cparams(("parallel",)),
    )(u, u, u, scan_suffix)


def _attn_bwd(u, d_att_out, scan_suffix, scan_prefix, *, seq, d_att, name):
    t = ATT_TILE
    nq = seq // t
    pairs = d_att // LANES
    grp = ATT_GROUP
    assert nq % grp == 0

    def body(q_ref, k_ref, v_ref, do_ref, um_ref, pm_ref, dq_ref, dk_ref, dv_ref,
             q0, q1, kk, k0, k1, vv, do0, do1, dq_acc, dk_acc, dv_acc, g_st, b_st):
        _head_copies(seq, q_ref, ATT_SCALE, q0, q1, None)
        _head_copies(seq, k_ref, 1.0, k0, k1, kk)
        _head_copies(seq, v_ref, 1.0, None, None, vv)
        _head_copies(seq, do_ref, 1.0, do0, do1, None)
        dk_acc[...] = jnp.zeros_like(dk_acc)
        dv_acc[...] = jnp.zeros_like(dv_acc)

        heads = ((q0, k0, do0), (q1, k1, do1))
        tn = (((0,), (0,)), ((), ()))
        nt = (((1,), (1,)), ((), ()))

        def group(gi, carry):
            qb0 = gi * grp
            qrows = [pl.ds(pl.multiple_of((qb0 + g) * t, t), t) for g in range(grp)]
            qv = [[qs[qrows[g], :] for g in range(grp)] for qs, _, _ in heads]
            dov = [[dos[qrows[g], :] for g in range(grp)] for _, _, dos in heads]

            def key_rows(g, i):
                return pl.ds(pl.multiple_of(jnp.maximum(qb0 + g - i, 0) * t, t), t)

            def step(i, runs, masked):
                runs = [list(r) for r in runs]
                for g in range(grp):
                    kb = qb0 + g - i
                    krows = key_rows(g, i)
                    keys, vals = kk[krows, :], vv[krows, :]
                    dv = jnp.zeros((t, LANES), F32)
                    for h in range(2):
                        run = runs[h][g] if masked else runs[h][g] + jnp.where(kb >= 0, 0.0, ATT_FILL)
                        f, a, causal = _att_tile_scores(qv[h][g], keys, masked)
                        scan = _split_scan(f, um_ref)
                        w = jnp.exp(a + run + scan[:, :t])
                        if masked:
                            w = jnp.where(causal, w, 0.0)
                        dw = lax.dot_general(dov[h][g], vals, nt, preferred_element_type=F32)
                        g_st[h * grp + g, i] = w * dw
                        b_st[h * grp + g, i] = jnp.exp(a)
                        dv = dv + lax.dot_general(w.astype(BF16), dov[h][g], tn, preferred_element_type=F32)
                        runs[h][g] = run + scan[:, t:]
                    dv_acc[krows, :] += dv
                return runs

            runs = step(0, [[jnp.zeros((t, t), F32)] * grp] * 2, True)

            def cond(st):
                return jnp.logical_and(st[0] <= qb0 + grp - 1, st[2] > 0)

            def wbody(st):
                i, runs, _ = st
                runs = step(i, runs, False)
                return i + 1, runs, _any_alive(runs)

            steps = lax.while_loop(cond, wbody, (1, runs, _any_alive(runs)))[0]

            def back(i, gruns, dqs, masked):
                gruns = [list(r) for r in gruns]
                dqs = list(dqs)
                for g in range(grp):
                    krows = key_rows(g, i)
                    dk = jnp.zeros((t, LANES), F32)
                    for h, (_, ks, _) in enumerate(heads):
                        gt = g_st[h * grp + g, i]
                        scan = _split_scan(gt, pm_ref)
                        dz = gt - b_st[h * grp + g, i] * (gt + gruns[h][g] + scan[:, :t])
                        if masked:
                            causal = lax.broadcasted_iota(jnp.int32, (t, t), 1) < lax.broadcasted_iota(jnp.int32, (t, t), 0)
                            dz = jnp.where(causal, dz, 0.0)
                        dz = dz.astype(BF16)
                        dqs[g] = dqs[g] + jnp.dot(dz, ks[krows, :], preferred_element_type=F32)
                        dk = dk + lax.dot_general(dz, qv[h][g], tn, preferred_element_type=F32)
                        gruns[h][g] = gruns[h][g] + scan[:, t:]
                    dk_acc[krows, :] += dk
                return gruns, dqs

            def bbody(j, st):
                return back(steps - 1 - j, st[0], st[1], False)

            init = ([[jnp.zeros((t, t), F32)] * grp] * 2, [jnp.zeros((t, LANES), F32)] * grp)
            gruns, dqs = lax.fori_loop(0, steps - 1, bbody, init)
            _, dqs = back(0, gruns, dqs, True)
            for g in range(grp):
                dq_acc[qrows[g], :] = dqs[g]
            return carry

        lax.fori_loop(0, nq // grp, group, 0)

        chunk = min(256, seq)

        def emit(r, carry):
            rows = pl.ds(pl.multiple_of(r * chunk, chunk), chunk)
            dq_ref[rows, :] = (dq_acc[rows, :] * ATT_SCALE).astype(BF16)
            dk_ref[rows, :] = dk_acc[rows, :].astype(BF16)
            dv_ref[rows, :] = dv_acc[rows, :].astype(BF16)
            return carry

        lax.fori_loop(0, seq // chunk, emit, 0)

    blk = lambda base: pl.BlockSpec((seq, LANES), lambda j, base=base: (0, base + j), pipeline_mode=pl.Buffered(1))
    mat = pl.BlockSpec((2 * t, 2 * t), lambda j: (0, 0))
    out = jax.ShapeDtypeStruct((seq, d_att), BF16)
    return pl.pallas_call(
        body, name=name, grid=(pairs,),
        in_specs=[blk(0), blk(pairs), blk(2 * pairs), blk(0), mat, mat],
        out_specs=[pl.BlockSpec((seq, LANES), lambda j: (0, j))] * 3,
        out_shape=[out, out, out],
        scratch_shapes=[pltpu.VMEM((seq, LANES), BF16)] * 8 + [pltpu.VMEM((seq, LANES), F32)] * 3
        + [pltpu.VMEM((2 * grp, nq, t, t), F32)] * 2,
        compiler_params=_cparams(("parallel",)),
    )(u, u, u, d_att_out, scan_suffix, scan_prefix)


CONV_ROWS = 256


def _shifted(window, residue, rows):
    total = rows + CONV_PAD
    return window if residue == 0 else pltpu.roll(window, total - residue, 0)


def _glu_to_pad(seq, a_ref, b_ref, pad_ref):
    chunk = min(CONV_ROWS, seq)
    pad_ref[pl.ds(0, CONV_PAD), :] = jnp.zeros((CONV_PAD, LANES), F32)

    def step(r, carry):
        rows = pl.ds(pl.multiple_of(r * chunk, chunk), chunk)
        pad_ref[pl.ds(pl.multiple_of(r * chunk + CONV_PAD, SUBLANES), chunk), :] = a_ref[rows, :] * _sigmoid(b_ref[rows, :])
        return carry

    lax.fori_loop(0, seq // chunk, step, 0)


def _conv_fwd(u, conv_w, conv_b, *, layer, seq, d_conv, col_a, col_b, name):
    blocks = d_conv // LANES
    rows_t = min(CONV_ROWS, seq)
    shift0 = CONV_PAD - (CONV_WIDTH - 1)

    def body(a_ref, b_ref, w_ref, bias_ref, o_ref, pad_ref):
        _glu_to_pad(seq, a_ref, b_ref, pad_ref)

        def step(r, carry):
            base = pl.multiple_of(r * rows_t, rows_t)
            window = pad_ref[pl.ds(base, rows_t + CONV_PAD), :]
            acc = jnp.zeros((rows_t, LANES), F32) + bias_ref[...]
            for residue in range(SUBLANES):
                moved = _shifted(window, residue, rows_t)
                for tap in range(CONV_WIDTH):
                    if (shift0 + tap) % SUBLANES == residue:
                        lo = (shift0 + tap) - residue
                        acc = acc + w_ref[tap:tap + 1, :] * moved[lo:lo + rows_t, :]
            o_ref[pl.ds(base, rows_t), :] = acc
            return carry

        lax.fori_loop(0, seq // rows_t, step, 0)

    return pl.pallas_call(
        body, name=name, grid=(blocks,),
        in_specs=[pl.BlockSpec((seq, LANES), lambda j: (0, col_a + j)), pl.BlockSpec((seq, LANES), lambda j: (0, col_b + j)),
                  pl.BlockSpec((None, CONV_PAD, LANES), lambda j: (layer, 0, j)),
                  pl.BlockSpec((None, 1, LANES), lambda j: (layer, 0, j))],
        out_specs=pl.BlockSpec((seq, LANES), lambda j: (0, j)),
        out_shape=jax.ShapeDtypeStruct((seq, d_conv), F32),
        scratch_shapes=[pltpu.VMEM((seq + CONV_PAD, LANES), F32)],
        compiler_params=_cparams(("parallel",)),
    )(u, u, conv_w, conv_b)


def _conv_bwd(u, dc1, conv_w, *, layer, seq, d_conv, col_a, col_b, name):
    blocks = d_conv // LANES
    rows_t = min(CONV_ROWS, seq)
    shift0 = CONV_PAD - (CONV_WIDTH - 1)

    def body(a_ref, b_ref, d_ref, w_ref, da_ref, db_ref, dw_ref, pad_ref, dpad_ref, dw_acc):
        _glu_to_pad(seq, a_ref, b_ref, pad_ref)
        dpad_ref[pl.ds(seq, CONV_PAD), :] = jnp.zeros((CONV_PAD, LANES), F32)

        def fill(r, carry):
            rows = pl.ds(pl.multiple_of(r * rows_t, rows_t), rows_t)
            dpad_ref[rows, :] = d_ref[rows, :]
            return carry

        lax.fori_loop(0, seq // rows_t, fill, 0)
        dw_acc[...] = jnp.zeros_like(dw_acc)

        def step(r, carry):
            base = pl.multiple_of(r * rows_t, rows_t)
            rows = pl.ds(base, rows_t)
            window = dpad_ref[pl.ds(base, rows_t + CONV_PAD), :]
            acc = jnp.zeros((rows_t, LANES), F32)
            for residue in range(SUBLANES):
                moved = _shifted(window, residue, rows_t)
                for tap in range(CONV_WIDTH):
                    off = CONV_WIDTH - 1 - tap
                    if off % SUBLANES == residue:
                        lo = off - residue
                        acc = acc + w_ref[tap:tap + 1, :] * moved[lo:lo + rows_t, :]
            sig = _sigmoid(b_ref[rows, :])
            a = a_ref[rows, :]
            da_ref[rows, :] = (acc * sig).astype(BF16)
            db_ref[rows, :] = (acc * a * sig * (1.0 - sig)).astype(BF16)
            dcur = d_ref[rows, :]
            cwin = pad_ref[pl.ds(base, rows_t + CONV_PAD), :]
            for residue in range(SUBLANES):
                moved = _shifted(cwin, residue, rows_t)
                for tap in range(CONV_WIDTH):
                    if (shift0 + tap) % SUBLANES == residue:
                        lo = (shift0 + tap) - residue
                        prod = dcur * moved[lo:lo + rows_t, :]
                        dw_acc[tap] += jnp.sum(prod.reshape(rows_t // SUBLANES, SUBLANES, LANES), axis=0)
            return carry

        lax.fori_loop(0, seq // rows_t, step, 0)
        dw_ref[...] = jnp.sum(dw_acc[...], axis=1)

    col = lambda base: pl.BlockSpec((seq, LANES), lambda j, base=base: (0, base + j))
    own = pl.BlockSpec((seq, LANES), lambda j: (0, j))
    return pl.pallas_call(
        body, name=name, grid=(blocks,),
        in_specs=[col(col_a), col(col_b), own, pl.BlockSpec((None, CONV_PAD, LANES), lambda j: (layer, 0, j))],
        out_specs=[own, own, pl.BlockSpec((CONV_PAD, LANES), lambda j: (0, j))],
        out_shape=[jax.ShapeDtypeStruct((seq, d_conv), BF16), jax.ShapeDtypeStruct((seq, d_conv), BF16),
                   jax.ShapeDtypeStruct((CONV_PAD, d_conv), F32)],
        scratch_shapes=[pltpu.VMEM((seq + CONV_PAD, LANES), F32), pltpu.VMEM((seq + CONV_PAD, LANES), F32),
                        pltpu.VMEM((CONV_PAD, SUBLANES, LANES), F32)],
        compiler_params=_cparams(("parallel",)),
    )(u, u, dc1, conv_w)


MIX_ROWS = 256


def _layer_norm_stats(val):
    mu = jnp.mean(val, axis=-1, keepdims=True)
    cen = val - mu
    var = jnp.mean(cen * cen, axis=-1, keepdims=True)
    rstd = lax.rsqrt(var + LN_EPS)
    return cen * rstd, rstd


def _layer_norm_bwd(dy, xhat, rstd, gain):
    dxhat = dy * gain
    m1 = jnp.mean(dxhat, axis=-1, keepdims=True)
    m2 = jnp.mean(dxhat * xhat, axis=-1, keepdims=True)
    dx = rstd * (dxhat - m1 - xhat * m2)
    return dx, jnp.sum(dy * xhat, axis=0, keepdims=True), jnp.sum(dy, axis=0, keepdims=True)


def _mix_forward(zatt, att, c1, zconv, gatt, gconv, x, w_att, w_conv, w_out, cln_g, cln_b, b_proj):
    s_zatt = _sigmoid(zatt)
    a_in = att * (zatt * s_zatt)
    chat, c_rstd = _layer_norm_stats(c1)
    c2 = chat * cln_g + cln_b
    s_c2 = _sigmoid(c2)
    c3 = c2 * s_c2
    s_zconv = _sigmoid(zconv)
    c_in = c3 * (zconv * s_zconv)
    a_in_b, c_in_b = a_in.astype(BF16), c_in.astype(BF16)
    ab = jnp.dot(a_in_b, w_att, preferred_element_type=F32)
    cb = jnp.dot(c_in_b, w_conv, preferred_element_type=F32) + b_proj
    s_gatt, s_gconv = _sigmoid(gatt), _sigmoid(gconv)
    merged_b = (s_gatt * ab + s_gconv * cb).astype(BF16)
    y = jnp.dot(merged_b, w_out, preferred_element_type=F32)
    h = DEEPNORM_ALPHA * x + y
    return dict(s_zatt=s_zatt, a_in_b=a_in_b, chat=chat, c_rstd=c_rstd, c2=c2, s_c2=s_c2, c3=c3, s_zconv=s_zconv,
                c_in_b=c_in_b, ab=ab, cb=cb, s_gatt=s_gatt, s_gconv=s_gconv, merged_b=merged_b, h=h)


def _u_blocks(rows_t, width, half):
    return [pl.BlockSpec((rows_t, half), lambda i, c=c: (i, c)) for c in (3, 6, 7, 8, 9, 10)]


def _of_layer(arr, layer):
    return pl.BlockSpec((None,) + arr.shape[1:], lambda i: (layer, 0, 0))


def _mix_fwd(u, att, c1, x, w_att, w_conv, w_out, cln_g, cln_b, b_proj, ln_g, ln_b, *, layer, seq, d_model, name):
    half = d_model // 2
    rows_t = min(MIX_ROWS, seq)

    def body(zatt_ref, zconv_ref, ga0, ga1, gc0, gc1, att_ref, c1_ref, x_ref, wa_ref, wc_ref, wo_ref,
             cg_ref, cb_ref, bp_ref, lg_ref, lb_ref, o_ref, ob_ref):
        gatt = jnp.concatenate([ga0[...], ga1[...]], axis=1)
        gconv = jnp.concatenate([gc0[...], gc1[...]], axis=1)
        mid = _mix_forward(zatt_ref[...], att_ref[...], c1_ref[...], zconv_ref[...], gatt, gconv, x_ref[...],
                           wa_ref[...], wc_ref[...], wo_ref[...], cg_ref[...], cb_ref[...], bp_ref[...])
        xhat, _ = _layer_norm_stats(mid["h"])
        out = xhat * lg_ref[...] + lb_ref[...]
        o_ref[...] = out
        ob_ref[...] = out.astype(BF16)

    row = lambda width: pl.BlockSpec((rows_t, width), lambda i: (i, 0))
    full = lambda arr: _of_layer(arr, layer)
    out = pl.BlockSpec((rows_t, d_model), lambda i: (i, 0))
    return pl.pallas_call(
        body, name=name, grid=(seq // rows_t,),
        in_specs=_u_blocks(rows_t, d_model, half) + [row(half), row(half), row(d_model), full(w_att), full(w_conv), full(w_out),
                                                     full(cln_g), full(cln_b), full(b_proj), full(ln_g), full(ln_b)],
        out_specs=[out, out],
        out_shape=[jax.ShapeDtypeStruct((seq, d_model), F32), jax.ShapeDtypeStruct((seq, d_model), BF16)],
        compiler_params=_cparams(("parallel",)),
    )(u, u, u, u, u, u, att, c1, x, w_att, w_conv, w_out, cln_g, cln_b, b_proj, ln_g, ln_b)


def _mix_bwd(u, att, c1, x, dxn, w_att, w_conv, w_out, cln_g, cln_b, b_proj, ln_g, *, layer, seq, d_model, name):
    half = d_model // 2
    rows_t = min(MIX_ROWS, seq)
    nt = ((1,), (1,))
    tn = ((0,), (0,))

    def body(zatt_ref, zconv_ref, ga0, ga1, gc0, gc1, att_ref, c1_ref, x_ref, dxn_ref, wa_ref, wc_ref, wo_ref,
             cg_ref, cb_ref, bp_ref, lg_ref,
             dzatt_ref, dzconv_ref, dgatt_ref, dgconv_ref, datt_ref, dc1_ref, dxres_ref, dwa_ref, dwc_ref, dwo_ref,
             dcg_ref, dcb_ref, dcbias_ref, dbp_ref, dlg_ref, dlb_ref):
        sums = (dwa_ref, dwc_ref, dwo_ref, dcg_ref, dcb_ref, dcbias_ref, dbp_ref, dlg_ref, dlb_ref)

        @pl.when(pl.program_id(0) == 0)
        def _():
            for ref in sums:
                ref[...] = jnp.zeros_like(ref)

        zatt, zconv, att = zatt_ref[...], zconv_ref[...], att_ref[...]
        gatt = jnp.concatenate([ga0[...], ga1[...]], axis=1)
        gconv = jnp.concatenate([gc0[...], gc1[...]], axis=1)
        wa, wc, wo = wa_ref[...], wc_ref[...], wo_ref[...]
        mid = _mix_forward(zatt, att, c1_ref[...], zconv, gatt, gconv, x_ref[...], wa, wc, wo,
                           cg_ref[...], cb_ref[...], bp_ref[...])
        xhat, rstd = _layer_norm_stats(mid["h"])
        dh, dlg, dlb = _layer_norm_bwd(dxn_ref[...], xhat, rstd, lg_ref[...])
        dlg_ref[...] += dlg
        dlb_ref[...] += dlb
        dxres_ref[...] = DEEPNORM_ALPHA * dh
        dy = dh.astype(BF16)
        dwo_ref[...] += lax.dot_general(mid["merged_b"], dy, (tn, ((), ())), preferred_element_type=F32)
        dmerged = lax.dot_general(dy, wo, (nt, ((), ())), preferred_element_type=F32)
        s_ga, s_gc, ab, cb = mid["s_gatt"], mid["s_gconv"], mid["ab"], mid["cb"]
        dgatt_ref[...] = (dmerged * ab * s_ga * (1.0 - s_ga)).astype(BF16)
        dgconv_ref[...] = (dmerged * cb * s_gc * (1.0 - s_gc)).astype(BF16)
        dab = dmerged * s_ga
        dcb = dmerged * s_gc
        dbp_ref[...] += jnp.sum(dcb, axis=0, keepdims=True)
        dab_b, dcb_b = dab.astype(BF16), dcb.astype(BF16)
        dwa_ref[...] += lax.dot_general(mid["a_in_b"], dab_b, (tn, ((), ())), preferred_element_type=F32)
        da_in = lax.dot_general(dab_b, wa, (nt, ((), ())), preferred_element_type=F32)
        s_za = mid["s_zatt"]
        datt_ref[...] = da_in * (zatt * s_za)
        dzatt_ref[...] = (da_in * att * (s_za * (1.0 + zatt * (1.0 - s_za)))).astype(BF16)
        dwc_ref[...] += lax.dot_general(mid["c_in_b"], dcb_b, (tn, ((), ())), preferred_element_type=F32)
        dc_in = lax.dot_general(dcb_b, wc, (nt, ((), ())), preferred_element_type=F32)
        s_zc, c2, s_c2 = mid["s_zconv"], mid["c2"], mid["s_c2"]
        dzconv_ref[...] = (dc_in * mid["c3"] * (s_zc * (1.0 + zconv * (1.0 - s_zc)))).astype(BF16)
        dc3 = dc_in * (zconv * s_zc)
        dc2 = dc3 * (s_c2 * (1.0 + c2 * (1.0 - s_c2)))
        dc1, dcg, dcbeta = _layer_norm_bwd(dc2, mid["chat"], mid["c_rstd"], cg_ref[...])
        dcg_ref[...] += dcg
        dcb_ref[...] += dcbeta
        dcbias_ref[...] += jnp.sum(dc1, axis=0, keepdims=True)
        dc1_ref[...] = dc1

    row = lambda width: pl.BlockSpec((rows_t, width), lambda i: (i, 0))
    full = lambda arr: _of_layer(arr, layer)
    whole = lambda r, c: pl.BlockSpec((r, c), lambda i: (0, 0))
    sds = jax.ShapeDtypeStruct
    out_specs = [row(half), row(half), row(d_model), row(d_model), row(half), row(half), row(d_model),
                 whole(half, d_model), whole(half, d_model), whole(d_model, d_model),
                 whole(1, half), whole(1, half), whole(1, half), whole(1, d_model), whole(1, d_model), whole(1, d_model)]
    out_shape = [sds((seq, half), BF16), sds((seq, half), BF16), sds((seq, d_model), BF16), sds((seq, d_model), BF16),
                 sds((seq, half), F32), sds((seq, half), F32), sds((seq, d_model), F32),
                 sds((half, d_model), F32), sds((half, d_model), F32), sds((d_model, d_model), F32),
                 sds((1, half), F32), sds((1, half), F32), sds((1, half), F32),
                 sds((1, d_model), F32), sds((1, d_model), F32), sds((1, d_model), F32)]
    return pl.pallas_call(
        body, name=name, grid=(seq // rows_t,),
        in_specs=_u_blocks(rows_t, d_model, half) + [row(half), row(half), row(d_model), row(d_model), full(w_att), full(w_conv),
                                                     full(w_out), full(cln_g), full(cln_b), full(b_proj), full(ln_g)],
        out_specs=out_specs, out_shape=out_shape,
        compiler_params=_cparams(("arbitrary",)),
    )(u, u, u, u, u, u, att, c1, x, dxn, w_att, w_conv, w_out, cln_g, cln_b, b_proj, ln_g)


def _loss_head(y, target, *, seq, d_model, name):
    rows_t = min(512, seq)

    def body(y_ref, t_ref, dy_ref, loss_ref):
        @pl.when(pl.program_id(0) == 0)
        def _():
            loss_ref[...] = jnp.zeros_like(loss_ref)

        err = y_ref[...] - t_ref[...]
        dy_ref[...] = err * (1.0 / d_model)
        per_token = jnp.sum(err * err, axis=-1, keepdims=True) * (1.0 / d_model)
        loss_ref[...] += 0.5 * jnp.sum(per_token, axis=0, keepdims=True)

    row = pl.BlockSpec((rows_t, d_model), lambda i: (i, 0))
    return pl.pallas_call(
        body, name=name, grid=(seq // rows_t,), in_specs=[row, row],
        out_specs=[row, pl.BlockSpec((1, 1), lambda i: (0, 0))],
        out_shape=[jax.ShapeDtypeStruct((seq, d_model), F32), jax.ShapeDtypeStruct((1, 1), F32)],
        compiler_params=_cparams(("arbitrary",)),
    )(y, target)


def _adamw(w, g, m, v, *, name):
    rows, cols = w.shape
    rows_t = rows
    for cand in (512, 256, 128, 64, 32, 16, 8):
        if rows % cand == 0 and cand * cols * 4 <= 2 * 1024 * 1024:
            rows_t = cand
            break

    def body(w_ref, g_ref, m_ref, v_ref, d_ref, nm_ref, nv_ref):
        grad = g_ref[...]
        new_m = ADAM_B1 * m_ref[...] + (1.0 - ADAM_B1) * grad
        new_v = ADAM_B2 * v_ref[...] + (1.0 - ADAM_B2) * (grad * grad)
        m_hat = new_m / (1.0 - ADAM_B1 ** ADAM_STEP)
        v_hat = new_v / (1.0 - ADAM_B2 ** ADAM_STEP)
        d_ref[...] = -ADAM_LR * (m_hat / (jnp.sqrt(v_hat) + ADAM_EPS) + ADAM_WD * w_ref[...])
        nm_ref[...] = new_m
        nv_ref[...] = new_v

    blk = pl.BlockSpec((rows_t, cols), lambda i: (i, 0))
    out = jax.ShapeDtypeStruct((rows, cols), F32)
    return pl.pallas_call(
        body, name=name, grid=(rows // rows_t,), in_specs=[blk] * 4, out_specs=[blk] * 3, out_shape=[out] * 3,
        compiler_params=_cparams(("parallel",)),
    )(w, g, m, v)


def _local_grads(x, target, w_in, b_in, conv_w, conv_b, cln_g, cln_b, w_att, w_conv, b_proj, w_out, ln_g, ln_b):
    seq, d_model = x.shape
    half = d_model // 2
    depth = w_in.shape[0]
    scan_suffix, scan_prefix = _scan_matrices()
    cols = half // LANES
    dims = dict(seq=seq, d_model=d_model)
    conv_dims = dict(seq=seq, d_conv=half, col_a=4 * cols, col_b=5 * cols)

    xs, xbs, us, atts, c1s = [x], [x.astype(BF16)], [], [], []
    for l in range(depth):
        u = _matmul(xbs[l], w_in, layer=l, mode="nn", bias=b_in[l].reshape(1, -1), name="in_proj")
        att = _attn_fwd(u, scan_suffix, seq=seq, d_att=half, name="attn_fwd")
        c1 = _conv_fwd(u, conv_w, conv_b, layer=l, name="conv_fwd", **conv_dims)
        xn, xnb = _mix_fwd(u, att, c1, xs[l], w_att, w_conv, w_out, cln_g, cln_b, b_proj, ln_g, ln_b, layer=l, name="mix_fwd", **dims)
        us.append(u)
        atts.append(att)
        c1s.append(c1)
        xs.append(xn)
        xbs.append(xnb)

    dx, loss = _loss_head(xs[depth], target, name="loss_head", **dims)
    grads = [None] * depth
    for l in reversed(range(depth)):
        u = us[l]
        (dzatt, dzconv, dgatt, dgconv, datt, dc1, dxres, dwa, dwc, dwo, dcg, dcb, dcbias, dbp, dlg, dlb) = _mix_bwd(
            u, atts[l], c1s[l], xs[l], dx, w_att, w_conv, w_out, cln_g, cln_b, b_proj, ln_g, layer=l, name="mix_bwd", **dims)
        dq, dk, dv = _attn_bwd(u, datt, scan_suffix, scan_prefix, seq=seq, d_att=half, name="attn_bwd")
        dglu_a, dglu_b, dconvw = _conv_bwd(u, dc1, conv_w, layer=l, name="conv_bwd", **conv_dims)
        du = jnp.concatenate([dq, dk, dv, dzatt, dglu_a, dglu_b, dzconv, dgatt, dgconv], axis=1)
        dwin, dbin = _matmul(xbs[l], du, mode="tn", colsum=True, name="in_proj_dw")
        dx = _matmul(du, w_in, layer=l, mode="nt", add=dxres, name="in_proj_dx", tn=1024, tk=512)
        grads[l] = dict(w_in=dwin, b_in=dbin[0], conv_w=dconvw[:CONV_WIDTH], conv_b=dcbias[0], conv_ln_g=dcg[0], conv_ln_b=dcb[0],
                        w_att_proj=dwa, w_conv_proj=dwc, b_conv_proj=dbp[0], w_out=dwo, ln_g=dlg[0], ln_b=dlb[0])
    return loss, dx, grads


MESH = pl.DeviceIdType.MESH
IN_HBM = pl.BlockSpec(memory_space=pl.ANY)


def _position():
    x, y, c = lax.axis_index("x"), lax.axis_index("y"), lax.axis_index("c")
    return x, y, c, [(1 - x, y), (x, 1 - y), (1 - x, 1 - y)]


def _cut(ref, axis, start, size):
    idx = [slice(None)] * len(ref.shape)
    idx[axis] = pl.ds(start, size)
    return ref.at[tuple(idx)]


def _remote(src, dst, send_sem, recv_sem, device):
    return pltpu.make_async_remote_copy(src_ref=src, dst_ref=dst, send_sem=send_sem, recv_sem=recv_sem,
                                        device_id=device, device_id_type=MESH)


def _comm_call(body, name, operands, out_shape, scratch, alias=False):
    return pl.pallas_call(
        body, name=name, in_specs=[IN_HBM] * len(operands), out_specs=[IN_HBM] * len(out_shape), out_shape=out_shape,
        scratch_shapes=scratch, input_output_aliases={t: t for t in range(len(operands))} if alias else {},
    )(*operands)


def _place_block(shard, chip_axis, place, dtype, *, name):
    depth, rows, cols = shard.shape
    rows_t = _row_tile(rows, cols * 4 * 4, 16 * 1024 * 1024)
    steps = rows // rows_t
    shape = list(shard.shape)
    shape[chip_axis] *= N_CHIPS
    if chip_axis == 1:
        out_spec = pl.BlockSpec((None, rows_t, cols), lambda l, i, p: (l, p[1] * steps + i, 0))
    else:
        out_spec = pl.BlockSpec((None, rows_t, cols), lambda l, i, p: (l, i, p[1]))

    def body(place_ref, src_ref, out_ref):
        out_ref[...] = src_ref[...].astype(dtype)

    return pl.pallas_call(
        body, name=name, out_shape=jax.ShapeDtypeStruct(tuple(shape), dtype),
        grid_spec=pltpu.PrefetchScalarGridSpec(num_scalar_prefetch=1, grid=(depth, steps),
                                               in_specs=[pl.BlockSpec((None, rows_t, cols), lambda l, i, p: (l, i, 0))],
                                               out_specs=out_spec),
        compiler_params=_cparams(("arbitrary", "arbitrary")),
    )(place, shard)


def _gather_weights(wholes, chip_axes):
    n = len(wholes)
    depth = wholes[0].shape[0]
    hl = depth // 2

    def body(*refs):
        dst = refs[n:2 * n]
        ici_send, ici_recv, pair_send, pair_recv = refs[2 * n:]
        x, y, c, chips = _position()
        me = 2 * x + y

        def block(t, chip, lo, count):
            size = wholes[t].shape[chip_axes[t]] // N_CHIPS
            return _cut(_cut(dst[t], chip_axes[t], pl.multiple_of(chip * size, size), size), 0, lo, count)

        lo, other_lo = c * hl, (1 - c) * hl
        sends = []
        for t in range(n):
            for j, chip in enumerate(chips):
                cp = _remote(block(t, me, lo, hl), block(t, me, lo, hl), ici_send.at[t, j], ici_recv.at[t, j], (*chip, c))
                cp.start()
                sends.append(cp)
        for t in range(n):
            for j, chip in enumerate(chips):
                landed = block(t, 2 * chip[0] + chip[1], lo, hl)
                _remote(landed, landed, ici_send.at[t, j], ici_recv.at[t, j], (*chip, c)).wait_recv()
                cp = _remote(landed, landed, pair_send.at[t, j], pair_recv.at[t, j], (x, y, 1 - c))
                cp.start()
                sends.append(cp)
        for t in range(n):
            for j, chip in enumerate(chips):
                passed = block(t, 2 * chip[0] + chip[1], other_lo, hl)
                _remote(passed, passed, pair_send.at[t, j], pair_recv.at[t, j], (x, y, 1 - c)).wait_recv()
        for cp in sends:
            cp.wait_send()

    return _comm_call(body, "gather_weights", wholes, [jax.ShapeDtypeStruct(w.shape, w.dtype) for w in wholes],
                      [pltpu.SemaphoreType.DMA((n, 3))] * 4, alias=True)


def _pair_swap(parts, core_axes):
    n, depth = len(parts), len(parts[0])
    flat = [arr for per_layer in parts for arr in per_layer]
    halves = []
    for per_layer, axis in zip(parts, core_axes):
        shape = list(per_layer[0].shape)
        shape[axis] //= 2
        halves.append(jax.ShapeDtypeStruct((depth, *shape), per_layer[0].dtype))

    def body(*refs):
        src, dst = refs[:n * depth], refs[n * depth:n * depth + n]
        send_sem, recv_sem = refs[n * depth + n:]
        x, y, c, _ = _position()
        copies = []
        for t in range(n):
            size = halves[t].shape[1 + core_axes[t]]
            for l in range(depth):
                piece = _cut(src[t * depth + l], core_axes[t], pl.multiple_of((1 - c) * size, size), size)
                cp = _remote(piece, dst[t].at[l], send_sem.at[t, l], recv_sem.at[t, l], (x, y, 1 - c))
                cp.start()
                copies.append(cp)
        for cp in copies:
            cp.wait()

    return _comm_call(body, "pair_swap", flat, halves, [pltpu.SemaphoreType.DMA((n, depth))] * 2)


def _chip_exchange(partials, chip_axes):
    n = len(partials)
    quarters = []
    for arr, axis in zip(partials, chip_axes):
        shape = list(arr.shape)
        shape[axis] //= N_CHIPS
        quarters.append(jax.ShapeDtypeStruct((N_CHIPS - 1, *shape), arr.dtype))

    def body(*refs):
        src, dst = refs[:n], refs[n:2 * n]
        send_sem, recv_sem = refs[2 * n:]
        x, y, c, chips = _position()
        copies = []
        for t in range(n):
            size = quarters[t].shape[1 + chip_axes[t]]
            for j, chip in enumerate(chips):
                piece = _cut(src[t], chip_axes[t], pl.multiple_of((2 * chip[0] + chip[1]) * size, size), size)
                cp = _remote(piece, dst[t].at[j], send_sem.at[t, j], recv_sem.at[t, j], (*chip, c))
                cp.start()
                copies.append(cp)
        for cp in copies:
            cp.wait()

    return _comm_call(body, "chip_exchange", partials, quarters, [pltpu.SemaphoreType.DMA((n, N_CHIPS - 1))] * 2)


def _pair_join(blocks, core_axes):
    n = len(blocks)

    def body(*refs):
        dst = refs[n:2 * n]
        send_sem, recv_sem = refs[2 * n:]
        x, y, c, _ = _position()
        copies = []
        for t in range(n):
            size = blocks[t].shape[core_axes[t]] // 2
            mine = _cut(dst[t], core_axes[t], pl.multiple_of(c * size, size), size)
            cp = _remote(mine, mine, send_sem.at[t], recv_sem.at[t], (x, y, 1 - c))
            cp.start()
            copies.append(cp)
        for t, cp in enumerate(copies):
            size = blocks[t].shape[core_axes[t]] // 2
            other = _cut(dst[t], core_axes[t], pl.multiple_of((1 - c) * size, size), size)
            cp.wait_send()
            _remote(other, other, send_sem.at[t], recv_sem.at[t], (x, y, 1 - c)).wait_recv()

    return _comm_call(body, "pair_join", blocks, [jax.ShapeDtypeStruct(b.shape, b.dtype) for b in blocks],
                      [pltpu.SemaphoreType.DMA((n,))] * 2, alias=True)


def _gather_small(vec):
    n_dev = 2 * N_CHIPS

    def body(src, dst, send_sem, recv_sem, local_sem):
        x, y, c, _ = _position()
        flip = lambda v, bit: 1 - v if bit else v
        mine = pltpu.make_async_copy(src, dst.at[4 * x + 2 * y + c], local_sem)
        mine.start()
        copies = []
        for mask in range(1, n_dev):
            peer = (flip(x, mask & 4), flip(y, mask & 2), flip(c, mask & 1))
            cp = _remote(src, dst.at[4 * x + 2 * y + c], send_sem.at[mask - 1], recv_sem.at[mask - 1], peer)
            cp.start()
            copies.append((cp, peer))
        for mask, (cp, peer) in enumerate(copies, start=1):
            theirs = dst.at[4 * peer[0] + 2 * peer[1] + peer[2]]
            _remote(src, theirs, send_sem.at[mask - 1], recv_sem.at[mask - 1], peer).wait_recv()
        for cp, _ in copies:
            cp.wait_send()
        mine.wait()

    out = [jax.ShapeDtypeStruct((n_dev, *vec.shape), vec.dtype)]
    return _comm_call(body, "gather_small", [vec], out, [pltpu.SemaphoreType.DMA((n_dev - 1,))] * 2 + [pltpu.SemaphoreType.DMA(())])[0]


def _row_tile(rows, row_bytes, budget):
    tile = rows
    for cand in (512, 256, 128, 64, 32, 16, 8):
        if rows % cand == 0:
            tile = cand
            if cand * row_bytes <= budget:
                break
    return tile


def _pair_sum(parts, got, core_axis, place, *, name):
    depth, rows, cols = got.shape
    rows_t = _row_tile(rows, cols * 4 * depth * 6, 24 * 1024 * 1024)
    steps = rows // rows_t
    if core_axis == 0:
        part_spec = pl.BlockSpec((rows_t, cols), lambda i, p: (p[0] * steps + i, 0))
    else:
        part_spec = pl.BlockSpec((rows_t, cols), lambda i, p: (i, p[0]))
    stack_spec = pl.BlockSpec((depth, rows_t, cols), lambda i, p: (0, i, 0))

    def body(place_ref, *refs):
        got_ref, out_ref = refs[depth], refs[depth + 1]
        for l in range(depth):
            out_ref[l] = (refs[l][...] + got_ref[l]).astype(BF16)

    return pl.pallas_call(
        body, name=name, out_shape=jax.ShapeDtypeStruct(got.shape, BF16),
        grid_spec=pltpu.PrefetchScalarGridSpec(num_scalar_prefetch=1, grid=(steps,), in_specs=[part_spec] * depth + [stack_spec],
                                               out_specs=stack_spec),
        compiler_params=_cparams(("arbitrary",)),
    )(place, *parts, got)


def _chip_sum(partial, got, chip_axis, core_axis, place, *, name):
    _, depth, rows, cols = got.shape
    rows_t = _row_tile(rows, cols * 4 * 10, 24 * 1024 * 1024)
    steps = rows // rows_t
    if chip_axis == 1:
        own_spec = pl.BlockSpec((None, rows_t, cols), lambda l, i, p: (l, p[1] * steps + i, 0))
    else:
        own_spec = pl.BlockSpec((None, rows_t, cols), lambda l, i, p: (l, i, p[1]))
    got_spec = pl.BlockSpec((N_CHIPS - 1, None, rows_t, cols), lambda l, i, p: (0, l, i, 0))
    shape = [depth, rows, cols]
    shape[core_axis] *= 2
    if core_axis == 1:
        out_spec = pl.BlockSpec((None, rows_t, cols), lambda l, i, p: (l, p[0] * steps + i, 0))
    else:
        out_spec = pl.BlockSpec((None, rows_t, cols), lambda l, i, p: (l, i, p[0]))

    def body(place_ref, own_ref, got_ref, out_ref):
        up = lambda val: val.astype(F32)
        out_ref[...] = ((up(own_ref[...]) + up(got_ref[0])) + up(got_ref[1])) + up(got_ref[2])

    return pl.pallas_call(
        body, name=name, out_shape=jax.ShapeDtypeStruct(tuple(shape), F32),
        grid_spec=pltpu.PrefetchScalarGridSpec(num_scalar_prefetch=1, grid=(depth, steps), in_specs=[own_spec, got_spec],
                                               out_specs=out_spec),
        compiler_params=_cparams(("arbitrary", "arbitrary")),
    )(place, partial, got)


def _sum_devices(stack, *, name):
    def body(src_ref, out_ref):
        total = src_ref[0]
        for d in range(1, stack.shape[0]):
            total = total + src_ref[d]
        out_ref[...] = total

    return pl.pallas_call(body, name=name, out_shape=jax.ShapeDtypeStruct(stack.shape[1:], F32))(stack)


def kernel(x, w_in, b_in, conv_w, conv_b, conv_ln_g, conv_ln_b, w_att_proj, w_conv_proj, b_conv_proj, w_out, ln_g, ln_b, loss_target, m_w_in, m_b_in, m_conv_w, m_conv_b, m_conv_ln_g, m_conv_ln_b, m_w_att_proj, m_w_conv_proj, m_b_conv_proj, m_w_out, m_ln_g, m_ln_b, v_w_in, v_b_in, v_conv_w, v_conv_b, v_conv_ln_g, v_conv_ln_b, v_w_att_proj, v_w_conv_proj, v_b_conv_proj, v_w_out, v_ln_g, v_ln_b):
    depth = w_in.shape[0]
    d_model = x.shape[-1]
    half = d_model // 2
    chip = 2 * lax.axis_index("x") + lax.axis_index("y")
    place = jnp.stack([lax.axis_index("c"), chip]).astype(jnp.int32)
    vec3 = lambda v: v.reshape(depth, 1, -1)

    taps = jnp.pad(conv_w, ((0, 0), (0, CONV_PAD - CONV_WIDTH), (0, 0)))
    gathered = [("w_in", w_in, 2, BF16), ("w_att_proj", w_att_proj, 2, BF16), ("w_conv_proj", w_conv_proj, 2, BF16),
                ("w_out", w_out, 1, BF16), ("conv_w", taps, 2, F32)]
    wholes = [_place_block(arr, axis, place, dtype, name="place_" + n) for n, arr, axis, dtype in gathered]
    win, watt, wconv, wout, convw = _gather_weights(wholes, [axis for _, _, axis, _ in gathered])
    loss, grad_x, grads = _local_grads(x[0], loss_target[0], win, b_in, convw, vec3(conv_b), vec3(conv_ln_g), vec3(conv_ln_b),
                                       watt, wconv, vec3(b_conv_proj), wout, vec3(ln_g), vec3(ln_b))
    loss = lax.psum(loss[0, 0], ("x", "y", "c"))

    big = ["w_in", "w_att_proj", "w_conv_proj", "w_out"]
    core_axes, chip_axes = [0, 0, 0, 1], [1, 1, 1, 0]
    parts = [[grads[l][n] for l in range(depth)] for n in big]
    got = _pair_swap(parts, core_axes)
    partials = [_pair_sum(parts[t], got[t], core_axes[t], place, name="pair_sum_" + big[t]) for t in range(len(big))]
    got = _chip_exchange(partials, [a + 1 for a in chip_axes])
    finals = [_chip_sum(partials[t], got[t], chip_axes[t] + 1, core_axes[t] + 1, place, name="chip_sum_" + big[t])
              for t in range(len(big))]
    reduced = dict(zip(big, _pair_join(finals, [a + 1 for a in core_axes])))

    small = ["b_in", "conv_b", "conv_ln_g", "conv_ln_b", "b_conv_proj", "ln_g", "ln_b"]
    packed = jnp.stack([jnp.concatenate([grads[l][n] for n in small] + [grads[l]["conv_w"].reshape(-1)]) for l in range(depth)])
    total = _sum_devices(_gather_small(packed), name="sum_devices")
    offset = 0
    for n in small:
        width = grads[0][n].shape[0]
        reduced[n] = total[:, offset:offset + width]
        offset += width
    taps = total[:, offset:].reshape(depth, CONV_WIDTH, half)
    reduced["conv_w"] = lax.dynamic_slice_in_dim(taps, chip * conv_w.shape[2], conv_w.shape[2], axis=2)

    names = ["w_in", "b_in", "conv_w", "conv_b", "conv_ln_g", "conv_ln_b", "w_att_proj", "w_conv_proj", "b_conv_proj", "w_out", "ln_g", "ln_b"]
    weights = dict(zip(names, (w_in, b_in, conv_w, conv_b, conv_ln_g, conv_ln_b, w_att_proj, w_conv_proj, b_conv_proj, w_out, ln_g, ln_b)))
    first = dict(zip(names, (m_w_in, m_b_in, m_conv_w, m_conv_b, m_conv_ln_g, m_conv_ln_b, m_w_att_proj, m_w_conv_proj, m_b_conv_proj, m_w_out, m_ln_g, m_ln_b)))
    second = dict(zip(names, (v_w_in, v_b_in, v_conv_w, v_conv_b, v_conv_ln_g, v_conv_ln_b, v_w_att_proj, v_w_conv_proj, v_b_conv_proj, v_w_out, v_ln_g, v_ln_b)))
    delta, new_m, new_v = {}, {}, {}
    for n in names:
        shape = weights[n].shape
        flat = lambda arr: arr.reshape(-1, shape[-1])
        d, m, v = _adamw(flat(weights[n]), flat(reduced[n]), flat(first[n]), flat(second[n]), name="adamw_" + n)
        delta[n], new_m[n], new_v[n] = d.reshape(shape), m.reshape(shape), v.reshape(shape)
    return (loss, grad_x[None], *[reduced[n].reshape(weights[n].shape) for n in names], *[delta[n] for n in names],
            *[new_m[n] for n in names], *[new_v[n] for n in names])
```

```python
import functools
import math

import jax
import jax.numpy as jnp
from jax import lax
from jax.experimental import pallas as pl
from jax.experimental.pallas import tpu as pltpu

F32 = jnp.float32
BF16 = jnp.bfloat16

HEAD_DIM = 64
LANES = 128
CONV_WIDTH = 31
CONV_PAD = 32
SUBLANES = 8
LN_EPS = 1e-5
DEPTH = 4
DEEPNORM_ALPHA = (2 * DEPTH) ** 0.25
ATT_SCALE = HEAD_DIM ** -0.5
ATT_TILE = 128
ATT_DEAD = -104.0
ATT_GROUP = 2
ATT_FILL = -1e30

ADAM_LR = 0.001
ADAM_B1 = 0.9
ADAM_B2 = 0.999
ADAM_EPS = 1e-08
ADAM_WD = 0.01
ADAM_STEP = 10

VMEM_LIMIT = 56 * 1024 * 1024

N_CHIPS = 4


def _cparams(sem):
    return pltpu.CompilerParams(dimension_semantics=sem, vmem_limit_bytes=VMEM_LIMIT)


def _sigmoid(x):
    return 1.0 / (1.0 + jnp.exp(-x))


def _fit(tile, dim):
    assert dim % LANES == 0
    tile = min(tile, dim) // LANES * LANES
    while dim % tile:
        tile -= LANES
    return tile


_DIMS = {"nn": ((1,), (0,)), "nt": ((1,), (1,)), "tn": ((0,), (0,))}


def _matmul(a, b, *, mode, name, layer=None, bias=None, add=None, colsum=False, out_dtype=F32, tm=1024, tn=512, tk=1024):
    b_shape = b.shape if layer is None else b.shape[1:]
    if mode == "nn":
        (m, k), (k2, n) = a.shape, b_shape
    elif mode == "nt":
        (m, k), (n, k2) = a.shape, b_shape
    else:
        (k, m), (k2, n) = a.shape, b_shape
    assert k == k2
    tm, tn, tk = _fit(tm, m), _fit(tn, n), _fit(tk, k)
    gm, gn, nk = m // tm, n // tn, k // tk

    a_spec = pl.BlockSpec((tk, tm), lambda i, j, kk: (kk, i)) if mode == "tn" else pl.BlockSpec((tm, tk), lambda i, j, kk: (i, kk))
    if layer is None:
        b_spec = pl.BlockSpec((tn, tk), lambda i, j, kk: (j, kk)) if mode == "nt" else pl.BlockSpec((tk, tn), lambda i, j, kk: (kk, j))
    elif mode == "nt":
        b_spec = pl.BlockSpec((None, tn, tk), lambda i, j, kk: (layer, j, kk))
    else:
        b_spec = pl.BlockSpec((None, tk, tn), lambda i, j, kk: (layer, kk, j))
    in_specs, operands = [a_spec, b_spec], [a, b]
    if bias is not None:
        in_specs.append(pl.BlockSpec((1, tn), lambda i, j, kk: (0, j)))
        operands.append(bias)
    if add is not None:
        in_specs.append(pl.BlockSpec((tm, tn), lambda i, j, kk: (i, j)))
        operands.append(add)
    out_shape = [jax.ShapeDtypeStruct((m, n), out_dtype)]
    out_specs = [pl.BlockSpec((tm, tn), lambda i, j, kk: (i, j))]
    scratch = [pltpu.VMEM((tm, tn), F32)]
    if colsum:
        assert mode == "tn"
        out_shape.append(jax.ShapeDtypeStruct((gm, 1, n), F32))
        out_specs.append(pl.BlockSpec((1, 1, tn), lambda i, j, kk: (i, 0, j)))
        scratch.append(pltpu.VMEM((1, tn), F32))
    has_bias, has_add = bias is not None, add is not None

    def body(*refs):
        refs = list(refs)
        a_ref, b_ref = refs[0], refs[1]
        pos = 2
        bias_ref = add_ref = None
        if has_bias:
            bias_ref = refs[pos]
            pos += 1
        if has_add:
            add_ref = refs[pos]
            pos += 1
        o_ref = refs[pos]
        pos += 1
        cs_ref = None
        if colsum:
            cs_ref = refs[pos]
            pos += 1
        acc_ref = refs[pos]
        cs_acc = refs[pos + 1] if colsum else None
        kk = pl.program_id(2)

        @pl.when(kk == 0)
        def _():
            acc_ref[...] = jnp.zeros_like(acc_ref)
            if colsum:
                cs_acc[...] = jnp.zeros_like(cs_acc)

        bv = b_ref[...]
        acc_ref[...] += lax.dot_general(a_ref[...].astype(BF16), bv.astype(BF16), (_DIMS[mode], ((), ())),
                                        preferred_element_type=F32)
        if colsum:
            cs_acc[...] += jnp.sum(bv.astype(F32), axis=0, keepdims=True)

        @pl.when(kk == nk - 1)
        def _():
            out = acc_ref[...]
            if has_bias:
                out = out + bias_ref[...]
            if has_add:
                out = out + add_ref[...]
            o_ref[...] = out.astype(out_dtype)
            if colsum:
                cs_ref[0] = cs_acc[...]

    res = pl.pallas_call(
        body, name=name, grid=(gm, gn, nk), in_specs=in_specs, out_specs=out_specs, out_shape=out_shape,
        scratch_shapes=scratch, compiler_params=_cparams(("parallel", "parallel", "arbitrary")),
    )(*operands)
    if colsum:
        return res[0], res[1][0]
    return res[0]


def _scan_matrices():
    t = ATT_TILE
    r = lax.broadcasted_iota(jnp.int32, (t, t), 0)
    c = lax.broadcasted_iota(jnp.int32, (t, t), 1)
    ones = jnp.ones((t, t), F32)
    suffix = jnp.concatenate([(r > c).astype(F32), ones], axis=1)
    prefix = jnp.concatenate([(r < c).astype(F32), ones], axis=1)
    stack = lambda mat: jnp.concatenate([mat, mat], axis=0).astype(BF16)
    return stack(suffix), stack(prefix)


def _split_halves(val):
    hi = val.astype(BF16)
    lo = (val - hi.astype(F32)).astype(BF16)
    return jnp.concatenate([hi, lo], axis=1)


def _split_scan(val, mat_ref):
    return jnp.dot(_split_halves(val), mat_ref[...], preferred_element_type=F32)


def _att_tile_scores(q, k, masked):
    t = ATT_TILE
    z = lax.dot_general(q, k, (((1,), (1,)), ((), ())), preferred_element_type=F32)
    sp = jnp.log(1.0 + jnp.exp(-jnp.abs(z)))
    f = jnp.minimum(-z, 0.0) - sp
    a = f + z
    causal = None
    if masked:
        causal = lax.broadcasted_iota(jnp.int32, (t, t), 1) < lax.broadcasted_iota(jnp.int32, (t, t), 0)
        f = jnp.where(causal, f, 0.0)
    return f, a, causal


def _any_alive(runs):
    top = functools.reduce(jnp.maximum, [run for per_head in runs for run in per_head])
    return (jnp.max(top) > ATT_DEAD).astype(jnp.int32)


def _head_copies(seq, src_ref, scale, lo_ref, hi_ref, plain_ref):
    chunk = min(256, seq)
    low = lax.broadcasted_iota(jnp.int32, (chunk, LANES), 1) < HEAD_DIM

    def step(r, carry):
        rows = pl.ds(pl.multiple_of(r * chunk, chunk), chunk)
        val = src_ref[rows, :]
        if scale != 1.0:
            val = val * scale
        if lo_ref is not None:
            lo_ref[rows, :] = jnp.where(low, val, 0.0).astype(BF16)
            hi_ref[rows, :] = jnp.where(low, 0.0, val).astype(BF16)
        if plain_ref is not None:
            plain_ref[rows, :] = val.astype(BF16)
        return carry

    lax.fori_loop(0, seq // chunk, step, 0)


def _attn_fwd(u, scan_suffix, *, seq, d_att, name):
    t = ATT_TILE
    nq = seq // t
    pairs = d_att // LANES
    grp = ATT_GROUP
    assert nq % grp == 0

    def body(q_ref, k_ref, v_ref, um_ref, o_ref, q0, q1, kk, v0, v1, f2_s, a_s, lg_s, tot_s, run_s, acc_s):
        _head_copies(seq, q_ref, ATT_SCALE, q0, q1, None)
        _head_copies(seq, k_ref, 1.0, None, None, kk)
        _head_copies(seq, v_ref, 1.0, v0, v1, None)

        heads = ((q0, v0), (q1, v1))

        def group(gi, carry):
            qb0 = gi * grp
            qrows = [pl.ds(pl.multiple_of((qb0 + g) * t, t), t) for g in range(grp)]
            qv = [[qs[qrows[g], :] for g in range(grp)] for qs, _ in heads]

            chains = [(h, g) for g in range(grp) for h in range(2)]

            def key_rows(g, i):
                return pl.ds(pl.multiple_of(jnp.maximum(qb0 + g - i, 0) * t, t), t)

            def stage1(i, masked):
                out = []
                for h, g in chains:
                    f, a, _ = _att_tile_scores(qv[h][g], kk[key_rows(g, i), :], masked)
                    out.append((_split_halves(f), a))
                return out

            def stage2(halves, a, masked):
                scan = jnp.dot(halves, um_ref[...], preferred_element_type=F32)
                logit = a + scan[:, :t]
                if masked:
                    causal = lax.broadcasted_iota(jnp.int32, (t, t), 1) < lax.broadcasted_iota(jnp.int32, (t, t), 0)
                    logit = jnp.where(causal, logit, ATT_FILL)
                return logit, scan[:, t:]

            def put(halves_a=None, logit_total=None):
                for c in range(len(chains)):
                    if halves_a is not None:
                        f2_s[c], a_s[c] = halves_a[c]
                    if logit_total is not None:
                        lg_s[c], tot_s[c] = logit_total[c]

            first = stage1(0, True)
            put(halves_a=stage1(1, False), logit_total=[stage2(f2, a, True) for f2, a in first])
            for c in range(len(chains)):
                run_s[c] = jnp.zeros((t, t), F32)
            for g in range(grp):
                acc_s[g] = jnp.zeros((t, LANES), F32)

            def wbody(st):
                i = st[0]
                held = [(f2_s[c], a_s[c]) for c in range(len(chains))]
                logits = [lg_s[c] for c in range(len(chains))]
                totals = [tot_s[c] for c in range(len(chains))]
                runs = [run_s[c] for c in range(len(chains))]
                accs = [acc_s[g] for g in range(grp)]
                for c, (h, g) in enumerate(chains):
                    run = runs[c] + jnp.where(qb0 + g - i >= 0, 0.0, ATT_FILL)
                    w = jnp.exp(logits[c] + run)
                    accs[g] = accs[g] + jnp.dot(w.astype(BF16), heads[h][1][key_rows(g, i), :], preferred_element_type=F32)
                    runs[c] = run + totals[c]
                ahead2 = [stage2(f2, a, False) for f2, a in held]
                ahead1 = stage1(i + 2, False)
                put(halves_a=ahead1, logit_total=ahead2)
                for c in range(len(chains)):
                    run_s[c] = runs[c]
                for g in range(grp):
                    acc_s[g] = accs[g]
                more = jnp.logical_and(i + 1 <= qb0 + grp - 1, _any_alive([runs]) > 0)
                return i + 1, more.astype(jnp.int32)

            lax.while_loop(lambda st: st[1] > 0, wbody, (jnp.int32(0), jnp.int32(1)))
            for g in range(grp):
                o_ref[qrows[g], :] = acc_s[g]
            return carry

        lax.fori_loop(0, nq // grp, group, 0)

    blk = lambda base: pl.BlockSpec((seq, LANES), lambda j, base=base: (0, base + j))
    return pl.pallas_call(
        body, name=name, grid=(pairs,),
        in_specs=[blk(0), blk(pairs), blk(2 * pairs), pl.BlockSpec((2 * t, 2 * t), lambda j: (0, 0))],
        out_specs=pl.BlockSpec((seq, LANES), lambda j: (0, j)),
        out_shape=jax.ShapeDtypeStruct((seq, d_att), F32),
        scratch_shapes=[pltpu.VMEM((seq, LANES), BF16)] * 5 + [pltpu.VMEM((2 * grp, t, 2 * t), BF16)]
        + [pltpu.VMEM((2 * grp, t, t), F32)] * 4 + [pltpu.VMEM((grp, t, LANES), F32)],
        compiler_params=_cparams(("parallel",)),
    )(u, u, u, scan_suffix)


def _attn_bwd(u, d_att_out, scan_suffix, scan_prefix, *, seq, d_att, name):
    t = ATT_TILE
    nq = seq // t
    pairs = d_att // LANES
    grp = ATT_GROUP
    assert nq % grp == 0

    def body(q_ref, k_ref, v_ref, do_ref, um_ref, pm_ref, dq_ref, dk_ref, dv_ref,
             q0, q1, kk, k0, k1, vv, do0, do1, dq_acc, dk_acc, dv_acc, g_st, b_st,
             f2_s, a_s, lg_s, tot_s, dw_s, run_s, p_s, pt_s, grun_s, dq_s):
        _head_copies(seq, q_ref, ATT_SCALE, q0, q1, None)
        _head_copies(seq, k_ref, 1.0, k0, k1, kk)
        _head_copies(seq, v_ref, 1.0, None, None, vv)
        _head_copies(seq, do_ref, 1.0, do0, do1, None)
        dk_acc[...] = jnp.zeros_like(dk_acc)
        dv_acc[...] = jnp.zeros_like(dv_acc)

        heads = ((q0, k0, do0), (q1, k1, do1))
        tn = (((0,), (0,)), ((), ()))
        nt = (((1,), (1,)), ((), ()))

        def group(gi, carry):
            qb0 = gi * grp
            qrows = [pl.ds(pl.multiple_of((qb0 + g) * t, t), t) for g in range(grp)]
            qv = [[qs[qrows[g], :] for g in range(grp)] for qs, _, _ in heads]
            dov = [[dos[qrows[g], :] for g in range(grp)] for _, _, dos in heads]

            def key_rows(g, i):
                return pl.ds(pl.multiple_of(jnp.maximum(qb0 + g - i, 0) * t, t), t)

            chains = [(h, g) for g in range(grp) for h in range(2)]
            every = range(len(chains))

            def stage1(i, masked):
                out = []
                for h, g in chains:
                    f, a, _ = _att_tile_scores(qv[h][g], kk[key_rows(g, i), :], masked)
                    out.append((_split_halves(f), a))
                return out

            def stage2(i, c, halves, a, masked):
                h, g = chains[c]
                scan = jnp.dot(halves, um_ref[...], preferred_element_type=F32)
                logit = a + scan[:, :t]
                if masked:
                    causal = lax.broadcasted_iota(jnp.int32, (t, t), 1) < lax.broadcasted_iota(jnp.int32, (t, t), 0)
                    logit = jnp.where(causal, logit, ATT_FILL)
                b_st[c, i] = jnp.exp(a)
                dw = lax.dot_general(dov[h][g], vv[key_rows(g, i), :], nt, preferred_element_type=F32)
                return logit, scan[:, t:], dw

            def put(held=None, ready=None):
                for c in every:
                    if held is not None:
                        f2_s[c], a_s[c] = held[c]
                    if ready is not None:
                        lg_s[c], tot_s[c], dw_s[c] = ready[c]

            head0 = stage1(0, True)
            put(held=stage1(1, False), ready=[stage2(0, c, *head0[c], True) for c in every])
            for c in every:
                run_s[c] = jnp.zeros((t, t), F32)

            def wbody(st):
                i = st[0]
                held = [(f2_s[c], a_s[c]) for c in every]
                ready = [(lg_s[c], tot_s[c], dw_s[c]) for c in every]
                runs = [run_s[c] for c in every]
                dvs = [jnp.zeros((t, LANES), F32)] * grp
                for c, (h, g) in enumerate(chains):
                    logit, total, dw = ready[c]
                    run = runs[c] + jnp.where(qb0 + g - i >= 0, 0.0, ATT_FILL)
                    w = jnp.exp(logit + run)
                    g_st[c, i] = w * dw
                    dvs[g] = dvs[g] + lax.dot_general(w.astype(BF16), dov[h][g], tn, preferred_element_type=F32)
                    runs[c] = run + total
                for g in range(grp):
                    dv_acc[key_rows(g, i), :] += dvs[g]
                ahead2 = [stage2(i + 1, c, *held[c], False) for c in every]
                ahead1 = stage1(i + 2, False)
                put(held=ahead1, ready=ahead2)
                for c in every:
                    run_s[c] = runs[c]
                more = jnp.logical_and(i + 1 <= qb0 + grp - 1, _any_alive([runs]) > 0)
                return i + 1, more.astype(jnp.int32)

            steps = lax.while_loop(lambda st: st[1] > 0, wbody, (jnp.int32(0), jnp.int32(1)))[0]

            def prefix(i):
                out = []
                for c in every:
                    scan = _split_scan(g_st[c, i], pm_ref)
                    out.append((scan[:, :t], scan[:, t:]))
                return out

            def back(i, masked):
                sums = [(p_s[c], pt_s[c]) for c in every]
                gruns = [grun_s[c] for c in every]
                dqs = [dq_s[g] for g in range(grp)]
                dks = [jnp.zeros((t, LANES), F32)] * grp
                for c, (h, g) in enumerate(chains):
                    gt = g_st[c, i]
                    dz = gt - b_st[c, i] * (gt + gruns[c] + sums[c][0])
                    if masked:
                        causal = lax.broadcasted_iota(jnp.int32, (t, t), 1) < lax.broadcasted_iota(jnp.int32, (t, t), 0)
                        dz = jnp.where(causal, dz, 0.0)
                    dz = dz.astype(BF16)
                    dqs[g] = dqs[g] + jnp.dot(dz, heads[h][1][key_rows(g, i), :], preferred_element_type=F32)
                    dks[g] = dks[g] + lax.dot_general(dz, qv[h][g], tn, preferred_element_type=F32)
                    gruns[c] = gruns[c] + sums[c][1]
                for g in range(grp):
                    dk_acc[key_rows(g, i), :] += dks[g]
                return gruns, dqs

            def keep(sums=None, gruns=None, dqs=None):
                for c in every:
                    if sums is not None:
                        p_s[c], pt_s[c] = sums[c]
                    if gruns is not None:
                        grun_s[c] = gruns[c]
                if dqs is not None:
                    for g in range(grp):
                        dq_s[g] = dqs[g]

            keep(sums=prefix(steps - 1), gruns=[jnp.zeros((t, t), F32)] * len(chains), dqs=[jnp.zeros((t, LANES), F32)] * grp)

            def bbody(j, carry2):
                i = steps - 1 - j
                gruns, dqs = back(i, False)
                keep(sums=prefix(i - 1), gruns=gruns, dqs=dqs)
                return carry2

            lax.fori_loop(0, steps - 1, bbody, 0)
            _, dqs = back(0, True)
            for g in range(grp):
                dq_acc[qrows[g], :] = dqs[g]
            return carry

        lax.fori_loop(0, nq // grp, group, 0)

        chunk = min(256, seq)

        def emit(r, carry):
            rows = pl.ds(pl.multiple_of(r * chunk, chunk), chunk)
            dq_ref[rows, :] = (dq_acc[rows, :] * ATT_SCALE).astype(BF16)
            dk_ref[rows, :] = dk_acc[rows, :].astype(BF16)
            dv_ref[rows, :] = dv_acc[rows, :].astype(BF16)
            return carry

        lax.fori_loop(0, seq // chunk, emit, 0)

    blk = lambda base: pl.BlockSpec((seq, LANES), lambda j, base=base: (0, base + j), pipeline_mode=pl.Buffered(1))
    mat = pl.BlockSpec((2 * t, 2 * t), lambda j: (0, 0))
    out = jax.ShapeDtypeStruct((seq, d_att), BF16)
    return pl.pallas_call(
        body, name=name, grid=(pairs,),
        in_specs=[blk(0), blk(pairs), blk(2 * pairs), blk(0), mat, mat],
        out_specs=[pl.BlockSpec((seq, LANES), lambda j: (0, j))] * 3,
        out_shape=[out, out, out],
        scratch_shapes=[pltpu.VMEM((seq, LANES), BF16)] * 8 + [pltpu.VMEM((seq, LANES), F32)] * 3
        + [pltpu.VMEM((2 * grp, nq + 1, t, t), F32)] * 2 + [pltpu.VMEM((2 * grp, t, 2 * t), BF16)]
        + [pltpu.VMEM((2 * grp, t, t), F32)] * 8 + [pltpu.VMEM((grp, t, LANES), F32)],
        compiler_params=_cparams(("parallel",)),
    )(u, u, u, d_att_out, scan_suffix, scan_prefix)


CONV_ROWS = 256


def _shifted(window, residue, rows):
    total = rows + CONV_PAD
    return window if residue == 0 else pltpu.roll(window, total - residue, 0)


def _glu_to_pad(seq, a_ref, b_ref, pad_ref):
    chunk = min(CONV_ROWS, seq)
    pad_ref[pl.ds(0, CONV_PAD), :] = jnp.zeros((CONV_PAD, LANES), F32)

    def step(r, carry):
        rows = pl.ds(pl.multiple_of(r * chunk, chunk), chunk)
        pad_ref[pl.ds(pl.multiple_of(r * chunk + CONV_PAD, SUBLANES), chunk), :] = a_ref[rows, :] * _sigmoid(b_ref[rows, :])
        return carry

    lax.fori_loop(0, seq // chunk, step, 0)


def _conv_fwd(u, conv_w, conv_b, *, layer, seq, d_conv, col_a, col_b, name):
    blocks = d_conv // LANES
    rows_t = min(CONV_ROWS, seq)
    shift0 = CONV_PAD - (CONV_WIDTH - 1)

    def body(a_ref, b_ref, w_ref, bias_ref, o_ref, pad_ref):
        _glu_to_pad(seq, a_ref, b_ref, pad_ref)

        def step(r, carry):
            base = pl.multiple_of(r * rows_t, rows_t)
            window = pad_ref[pl.ds(base, rows_t + CONV_PAD), :]
            acc = jnp.zeros((rows_t, LANES), F32) + bias_ref[...]
            for residue in range(SUBLANES):
                moved = _shifted(window, residue, rows_t)
                for tap in range(CONV_WIDTH):
                    if (shift0 + tap) % SUBLANES == residue:
                        lo = (shift0 + tap) - residue
                        acc = acc + w_ref[tap:tap + 1, :] * moved[lo:lo + rows_t, :]
            o_ref[pl.ds(base, rows_t), :] = acc
            return carry

        lax.fori_loop(0, seq // rows_t, step, 0)

    return pl.pallas_call(
        body, name=name, grid=(blocks,),
        in_specs=[pl.BlockSpec((seq, LANES), lambda j: (0, col_a + j)), pl.BlockSpec((seq, LANES), lambda j: (0, col_b + j)),
                  pl.BlockSpec((None, CONV_PAD, LANES), lambda j: (layer, 0, j)),
                  pl.BlockSpec((None, 1, LANES), lambda j: (layer, 0, j))],
        out_specs=pl.BlockSpec((seq, LANES), lambda j: (0, j)),
        out_shape=jax.ShapeDtypeStruct((seq, d_conv), F32),
        scratch_shapes=[pltpu.VMEM((seq + CONV_PAD, LANES), F32)],
        compiler_params=_cparams(("parallel",)),
    )(u, u, conv_w, conv_b)


def _conv_bwd(u, dc1, conv_w, *, layer, seq, d_conv, col_a, col_b, name):
    blocks = d_conv // LANES
    rows_t = min(CONV_ROWS, seq)
    shift0 = CONV_PAD - (CONV_WIDTH - 1)

    def body(a_ref, b_ref, d_ref, w_ref, da_ref, db_ref, dw_ref, pad_ref, dpad_ref, dw_acc):
        _glu_to_pad(seq, a_ref, b_ref, pad_ref)
        dpad_ref[pl.ds(seq, CONV_PAD), :] = jnp.zeros((CONV_PAD, LANES), F32)

        def fill(r, carry):
            rows = pl.ds(pl.multiple_of(r * rows_t, rows_t), rows_t)
            dpad_ref[rows, :] = d_ref[rows, :]
            return carry

        lax.fori_loop(0, seq // rows_t, fill, 0)
        dw_acc[...] = jnp.zeros_like(dw_acc)

        def step(r, carry):
            base = pl.multiple_of(r * rows_t, rows_t)
            rows = pl.ds(base, rows_t)
            window = dpad_ref[pl.ds(base, rows_t + CONV_PAD), :]
            acc = jnp.zeros((rows_t, LANES), F32)
            for residue in range(SUBLANES):
                moved = _shifted(window, residue, rows_t)
                for tap in range(CONV_WIDTH):
                    off = CONV_WIDTH - 1 - tap
                    if off % SUBLANES == residue:
                        lo = off - residue
                        acc = acc + w_ref[tap:tap + 1, :] * moved[lo:lo + rows_t, :]
            sig = _sigmoid(b_ref[rows, :])
            a = a_ref[rows, :]
            da_ref[rows, :] = (acc * sig).astype(BF16)
            db_ref[rows, :] = (acc * a * sig * (1.0 - sig)).astype(BF16)
            dcur = d_ref[rows, :]
            cwin = pad_ref[pl.ds(base, rows_t + CONV_PAD), :]
            for residue in range(SUBLANES):
                moved = _shifted(cwin, residue, rows_t)
                for tap in range(CONV_WIDTH):
                    if (shift0 + tap) % SUBLANES == residue:
                        lo = (shift0 + tap) - residue
                        prod = dcur * moved[lo:lo + rows_t, :]
                        dw_acc[tap] += jnp.sum(prod.reshape(rows_t // SUBLANES, SUBLANES, LANES), axis=0)
            return carry

        lax.fori_loop(0, seq // rows_t, step, 0)
        dw_ref[...] = jnp.sum(dw_acc[...], axis=1)

    col = lambda base: pl.BlockSpec((seq, LANES), lambda j, base=base: (0, base + j))
    own = pl.BlockSpec((seq, LANES), lambda j: (0, j))
    return pl.pallas_call(
        body, name=name, grid=(blocks,),
        in_specs=[col(col_a), col(col_b), own, pl.BlockSpec((None, CONV_PAD, LANES), lambda j: (layer, 0, j))],
        out_specs=[own, own, pl.BlockSpec((CONV_PAD, LANES), lambda j: (0, j))],
        out_shape=[jax.ShapeDtypeStruct((seq, d_conv), BF16), jax.ShapeDtypeStruct((seq, d_conv), BF16),
                   jax.ShapeDtypeStruct((CONV_PAD, d_conv), F32)],
        scratch_shapes=[pltpu.VMEM((seq + CONV_PAD, LANES), F32), pltpu.VMEM((seq + CONV_PAD, LANES), F32),
                        pltpu.VMEM((CONV_PAD, SUBLANES, LANES), F32)],
        compiler_params=_cparams(("parallel",)),
    )(u, u, dc1, conv_w)


MIX_ROWS = 256


def _layer_norm_stats(val):
    mu = jnp.mean(val, axis=-1, keepdims=True)
    cen = val - mu
    var = jnp.mean(cen * cen, axis=-1, keepdims=True)
    rstd = lax.rsqrt(var + LN_EPS)
    return cen * rstd, rstd


def _layer_norm_bwd(dy, xhat, rstd, gain):
    dxhat = dy * gain
    m1 = jnp.mean(dxhat, axis=-1, keepdims=True)
    m2 = jnp.mean(dxhat * xhat, axis=-1, keepdims=True)
    dx = rstd * (dxhat - m1 - xhat * m2)
    return dx, jnp.sum(dy * xhat, axis=0, keepdims=True), jnp.sum(dy, axis=0, keepdims=True)


def _mix_forward(zatt, att, c1, zconv, gatt, gconv, x, w_att, w_conv, w_out, cln_g, cln_b, b_proj):
    s_zatt = _sigmoid(zatt)
    a_in = att * (zatt * s_zatt)
    chat, c_rstd = _layer_norm_stats(c1)
    c2 = chat * cln_g + cln_b
    s_c2 = _sigmoid(c2)
    c3 = c2 * s_c2
    s_zconv = _sigmoid(zconv)
    c_in = c3 * (zconv * s_zconv)
    a_in_b, c_in_b = a_in.astype(BF16), c_in.astype(BF16)
    ab = jnp.dot(a_in_b, w_att, preferred_element_type=F32)
    cb = jnp.dot(c_in_b, w_conv, preferred_element_type=F32) + b_proj
    s_gatt, s_gconv = _sigmoid(gatt), _sigmoid(gconv)
    merged_b = (s_gatt * ab + s_gconv * cb).astype(BF16)
    y = jnp.dot(merged_b, w_out, preferred_element_type=F32)
    h = DEEPNORM_ALPHA * x + y
    return dict(s_zatt=s_zatt, a_in_b=a_in_b, chat=chat, c_rstd=c_rstd, c2=c2, s_c2=s_c2, c3=c3, s_zconv=s_zconv,
                c_in_b=c_in_b, ab=ab, cb=cb, s_gatt=s_gatt, s_gconv=s_gconv, merged_b=merged_b, h=h)


def _u_blocks(rows_t, width, half):
    return [pl.BlockSpec((rows_t, half), lambda i, c=c: (i, c)) for c in (3, 6, 7, 8, 9, 10)]


def _of_layer(arr, layer):
    return pl.BlockSpec((None,) + arr.shape[1:], lambda i: (layer, 0, 0))


def _mix_fwd(u, att, c1, x, w_att, w_conv, w_out, cln_g, cln_b, b_proj, ln_g, ln_b, *, layer, seq, d_model, name):
    half = d_model // 2
    rows_t = min(MIX_ROWS, seq)

    def body(zatt_ref, zconv_ref, ga0, ga1, gc0, gc1, att_ref, c1_ref, x_ref, wa_ref, wc_ref, wo_ref,
             cg_ref, cb_ref, bp_ref, lg_ref, lb_ref, o_ref, ob_ref):
        gatt = jnp.concatenate([ga0[...], ga1[...]], axis=1)
        gconv = jnp.concatenate([gc0[...], gc1[...]], axis=1)
        mid = _mix_forward(zatt_ref[...], att_ref[...], c1_ref[...], zconv_ref[...], gatt, gconv, x_ref[...],
                           wa_ref[...], wc_ref[...], wo_ref[...], cg_ref[...], cb_ref[...], bp_ref[...])
        xhat, _ = _layer_norm_stats(mid["h"])
        out = xhat * lg_ref[...] + lb_ref[...]
        o_ref[...] = out
        ob_ref[...] = out.astype(BF16)

    row = lambda width: pl.BlockSpec((rows_t, width), lambda i: (i, 0))
    full = lambda arr: _of_layer(arr, layer)
    out = pl.BlockSpec((rows_t, d_model), lambda i: (i, 0))
    return pl.pallas_call(
        body, name=name, grid=(seq // rows_t,),
        in_specs=_u_blocks(rows_t, d_model, half) + [row(half), row(half), row(d_model), full(w_att), full(w_conv), full(w_out),
                                                     full(cln_g), full(cln_b), full(b_proj), full(ln_g), full(ln_b)],
        out_specs=[out, out],
        out_shape=[jax.ShapeDtypeStruct((seq, d_model), F32), jax.ShapeDtypeStruct((seq, d_model), BF16)],
        compiler_params=_cparams(("parallel",)),
    )(u, u, u, u, u, u, att, c1, x, w_att, w_conv, w_out, cln_g, cln_b, b_proj, ln_g, ln_b)


def _mix_bwd(u, att, c1, x, dxn, w_att, w_conv, w_out, cln_g, cln_b, b_proj, ln_g, *, layer, seq, d_model, name):
    half = d_model // 2
    rows_t = min(MIX_ROWS, seq)
    nt = ((1,), (1,))
    tn = ((0,), (0,))

    def body(zatt_ref, zconv_ref, ga0, ga1, gc0, gc1, att_ref, c1_ref, x_ref, dxn_ref, wa_ref, wc_ref, wo_ref,
             cg_ref, cb_ref, bp_ref, lg_ref,
             dzatt_ref, dzconv_ref, dgatt_ref, dgconv_ref, datt_ref, dc1_ref, dxres_ref, dwa_ref, dwc_ref, dwo_ref,
             dcg_ref, dcb_ref, dcbias_ref, dbp_ref, dlg_ref, dlb_ref):
        sums = (dwa_ref, dwc_ref, dwo_ref, dcg_ref, dcb_ref, dcbias_ref, dbp_ref, dlg_ref, dlb_ref)

        @pl.when(pl.program_id(0) == 0)
        def _():
            for ref in sums:
                ref[...] = jnp.zeros_like(ref)

        zatt, zconv, att = zatt_ref[...], zconv_ref[...], att_ref[...]
        gatt = jnp.concatenate([ga0[...], ga1[...]], axis=1)
        gconv = jnp.concatenate([gc0[...], gc1[...]], axis=1)
        wa, wc, wo = wa_ref[...], wc_ref[...], wo_ref[...]
        mid = _mix_forward(zatt, att, c1_ref[...], zconv, gatt, gconv, x_ref[...], wa, wc, wo,
                           cg_ref[...], cb_ref[...], bp_ref[...])
        xhat, rstd = _layer_norm_stats(mid["h"])
        dh, dlg, dlb = _layer_norm_bwd(dxn_ref[...], xhat, rstd, lg_ref[...])
        dlg_ref[...] += dlg
        dlb_ref[...] += dlb
        dxres_ref[...] = DEEPNORM_ALPHA * dh
        dy = dh.astype(BF16)
        dwo_ref[...] += lax.dot_general(mid["merged_b"], dy, (tn, ((), ())), preferred_element_type=F32)
        dmerged = lax.dot_general(dy, wo, (nt, ((), ())), preferred_element_type=F32)
        s_ga, s_gc, ab, cb = mid["s_gatt"], mid["s_gconv"], mid["ab"], mid["cb"]
        dgatt_ref[...] = (dmerged * ab * s_ga * (1.0 - s_ga)).astype(BF16)
        dgconv_ref[...] = (dmerged * cb * s_gc * (1.0 - s_gc)).astype(BF16)
        dab = dmerged * s_ga
        dcb = dmerged * s_gc
        dbp_ref[...] += jnp.sum(dcb, axis=0, keepdims=True)
        dab_b, dcb_b = dab.astype(BF16), dcb.astype(BF16)
        dwa_ref[...] += lax.dot_general(mid["a_in_b"], dab_b, (tn, ((), ())), preferred_element_type=F32)
        da_in = lax.dot_general(dab_b, wa, (nt, ((), ())), preferred_element_type=F32)
        s_za = mid["s_zatt"]
        datt_ref[...] = da_in * (zatt * s_za)
        dzatt_ref[...] = (da_in * att * (s_za * (1.0 + zatt * (1.0 - s_za)))).astype(BF16)
        dwc_ref[...] += lax.dot_general(mid["c_in_b"], dcb_b, (tn, ((), ())), preferred_element_type=F32)
        dc_in = lax.dot_general(dcb_b, wc, (nt, ((), ())), preferred_element_type=F32)
        s_zc, c2, s_c2 = mid["s_zconv"], mid["c2"], mid["s_c2"]
        dzconv_ref[...] = (dc_in * mid["c3"] * (s_zc * (1.0 + zconv * (1.0 - s_zc)))).astype(BF16)
        dc3 = dc_in * (zconv * s_zc)
        dc2 = dc3 * (s_c2 * (1.0 + c2 * (1.0 - s_c2)))
        dc1, dcg, dcbeta = _layer_norm_bwd(dc2, mid["chat"], mid["c_rstd"], cg_ref[...])
        dcg_ref[...] += dcg
        dcb_ref[...] += dcbeta
        dcbias_ref[...] += jnp.sum(dc1, axis=0, keepdims=True)
        dc1_ref[...] = dc1

    row = lambda width: pl.BlockSpec((rows_t, width), lambda i: (i, 0))
    full = lambda arr: _of_layer(arr, layer)
    whole = lambda r, c: pl.BlockSpec((r, c), lambda i: (0, 0))
    sds = jax.ShapeDtypeStruct
    out_specs = [row(half), row(half), row(d_model), row(d_model), row(half), row(half), row(d_model),
                 whole(half, d_model), whole(half, d_model), whole(d_model, d_model),
                 whole(1, half), whole(1, half), whole(1, half), whole(1, d_model), whole(1, d_model), whole(1, d_model)]
    out_shape = [sds((seq, half), BF16), sds((seq, half), BF16), sds((seq, d_model), BF16), sds((seq, d_model), BF16),
                 sds((seq, half), F32), sds((seq, half), F32), sds((seq, d_model), F32),
                 sds((half, d_model), F32), sds((half, d_model), F32), sds((d_model, d_model), F32),
                 sds((1, half), F32), sds((1, half), F32), sds((1, half), F32),
                 sds((1, d_model), F32), sds((1, d_model), F32), sds((1, d_model), F32)]
    return pl.pallas_call(
        body, name=name, grid=(seq // rows_t,),
        in_specs=_u_blocks(rows_t, d_model, half) + [row(half), row(half), row(d_model), row(d_model), full(w_att), full(w_conv),
                                                     full(w_out), full(cln_g), full(cln_b), full(b_proj), full(ln_g)],
        out_specs=out_specs, out_shape=out_shape,
        compiler_params=_cparams(("arbitrary",)),
    )(u, u, u, u, u, u, att, c1, x, dxn, w_att, w_conv, w_out, cln_g, cln_b, b_proj, ln_g)


def _loss_head(y, target, *, seq, d_model, name):
    rows_t = min(512, seq)

    def body(y_ref, t_ref, dy_ref, loss_ref):
        @pl.when(pl.program_id(0) == 0)
        def _():
            loss_ref[...] = jnp.zeros_like(loss_ref)

        err = y_ref[...] - t_ref[...]
        dy_ref[...] = err * (1.0 / d_model)
        per_token = jnp.sum(err * err, axis=-1, keepdims=True) * (1.0 / d_model)
        loss_ref[...] += 0.5 * jnp.sum(per_token, axis=0, keepdims=True)

    row = pl.BlockSpec((rows_t, d_model), lambda i: (i, 0))
    return pl.pallas_call(
        body, name=name, grid=(seq // rows_t,), in_specs=[row, row],
        out_specs=[row, pl.BlockSpec((1, 1), lambda i: (0, 0))],
        out_shape=[jax.ShapeDtypeStruct((seq, d_model), F32), jax.ShapeDtypeStruct((1, 1), F32)],
        compiler_params=_cparams(("arbitrary",)),
    )(y, target)


def _adamw(w, g, m, v, *, name):
    rows, cols = w.shape
    rows_t = rows
    for cand in (512, 256, 128, 64, 32, 16, 8):
        if rows % cand == 0 and cand * cols * 4 <= 2 * 1024 * 1024:
            rows_t = cand
            break

    def body(w_ref, g_ref, m_ref, v_ref, d_ref, nm_ref, nv_ref):
        grad = g_ref[...]
        new_m = ADAM_B1 * m_ref[...] + (1.0 - ADAM_B1) * grad
        new_v = ADAM_B2 * v_ref[...] + (1.0 - ADAM_B2) * (grad * grad)
        m_hat = new_m / (1.0 - ADAM_B1 ** ADAM_STEP)
        v_hat = new_v / (1.0 - ADAM_B2 ** ADAM_STEP)
        d_ref[...] = -ADAM_LR * (m_hat / (jnp.sqrt(v_hat) + ADAM_EPS) + ADAM_WD * w_ref[...])
        nm_ref[...] = new_m
        nv_ref[...] = new_v

    blk = pl.BlockSpec((rows_t, cols), lambda i: (i, 0))
    out = jax.ShapeDtypeStruct((rows, cols), F32)
    return pl.pallas_call(
        body, name=name, grid=(rows // rows_t,), in_specs=[blk] * 4, out_specs=[blk] * 3, out_shape=[out] * 3,
        compiler_params=_cparams(("parallel",)),
    )(w, g, m, v)


def _local_grads(x, target, w_in, b_in, conv_w, conv_b, cln_g, cln_b, w_att, w_conv, b_proj, w_out, ln_g, ln_b):
    seq, d_model = x.shape
    half = d_model // 2
    depth = w_in.shape[0]
    scan_suffix, scan_prefix = _scan_matrices()
    cols = half // LANES
    dims = dict(seq=seq, d_model=d_model)
    conv_dims = dict(seq=seq, d_conv=half, col_a=4 * cols, col_b=5 * cols)

    xs, xbs, us, atts, c1s = [x], [x.astype(BF16)], [], [], []
    for l in range(depth):
        u = _matmul(xbs[l], w_in, layer=l, mode="nn", bias=b_in[l].reshape(1, -1), name="in_proj")
        att = _attn_fwd(u, scan_suffix, seq=seq, d_att=half, name="attn_fwd")
        c1 = _conv_fwd(u, conv_w, conv_b, layer=l, name="conv_fwd", **conv_dims)
        xn, xnb = _mix_fwd(u, att, c1, xs[l], w_att, w_conv, w_out, cln_g, cln_b, b_proj, ln_g, ln_b, layer=l, name="mix_fwd", **dims)
        us.append(u)
        atts.append(att)
        c1s.append(c1)
        xs.append(xn)
        xbs.append(xnb)

    dx, loss = _loss_head(xs[depth], target, name="loss_head", **dims)
    grads = [None] * depth
    for l in reversed(range(depth)):
        u = us[l]
        (dzatt, dzconv, dgatt, dgconv, datt, dc1, dxres, dwa, dwc, dwo, dcg, dcb, dcbias, dbp, dlg, dlb) = _mix_bwd(
            u, atts[l], c1s[l], xs[l], dx, w_att, w_conv, w_out, cln_g, cln_b, b_proj, ln_g, layer=l, name="mix_bwd", **dims)
        dq, dk, dv = _attn_bwd(u, datt, scan_suffix, scan_prefix, seq=seq, d_att=half, name="attn_bwd")
        dglu_a, dglu_b, dconvw = _conv_bwd(u, dc1, conv_w, layer=l, name="conv_bwd", **conv_dims)
        du = jnp.concatenate([dq, dk, dv, dzatt, dglu_a, dglu_b, dzconv, dgatt, dgconv], axis=1)
        dwin, dbin = _matmul(xbs[l], du, mode="tn", colsum=True, name="in_proj_dw")
        dx = _matmul(du, w_in, layer=l, mode="nt", add=dxres, name="in_proj_dx", tn=1024, tk=512)
        grads[l] = dict(w_in=dwin, b_in=dbin[0], conv_w=dconvw[:CONV_WIDTH], conv_b=dcbias[0], conv_ln_g=dcg[0], conv_ln_b=dcb[0],
                        w_att_proj=dwa, w_conv_proj=dwc, b_conv_proj=dbp[0], w_out=dwo, ln_g=dlg[0], ln_b=dlb[0])
    return loss, dx, grads


MESH = pl.DeviceIdType.MESH
IN_HBM = pl.BlockSpec(memory_space=pl.ANY)


def _position():
    x, y, c = lax.axis_index("x"), lax.axis_index("y"), lax.axis_index("c")
    return x, y, c, [(1 - x, y), (x, 1 - y), (1 - x, 1 - y)]


def _cut(ref, axis, start, size):
    idx = [slice(None)] * len(ref.shape)
    idx[axis] = pl.ds(start, size)
    return ref.at[tuple(idx)]


def _remote(src, dst, send_sem, recv_sem, device):
    return pltpu.make_async_remote_copy(src_ref=src, dst_ref=dst, send_sem=send_sem, recv_sem=recv_sem,
                                        device_id=device, device_id_type=MESH)


def _comm_call(body, name, operands, out_shape, scratch, alias=False):
    return pl.pallas_call(
        body, name=name, in_specs=[IN_HBM] * len(operands), out_specs=[IN_HBM] * len(out_shape), out_shape=out_shape,
        scratch_shapes=scratch, input_output_aliases={t: t for t in range(len(operands))} if alias else {},
    )(*operands)


def _place_block(shard, chip_axis, place, dtype, *, name):
    depth, rows, cols = shard.shape
    rows_t = _row_tile(rows, cols * 4 * 4, 16 * 1024 * 1024)
    steps = rows // rows_t
    shape = list(shard.shape)
    shape[chip_axis] *= N_CHIPS
    if chip_axis == 1:
        out_spec = pl.BlockSpec((None, rows_t, cols), lambda l, i, p: (l, p[1] * steps + i, 0))
    else:
        out_spec = pl.BlockSpec((None, rows_t, cols), lambda l, i, p: (l, i, p[1]))

    def body(place_ref, src_ref, out_ref):
        out_ref[...] = src_ref[...].astype(dtype)

    return pl.pallas_call(
        body, name=name, out_shape=jax.ShapeDtypeStruct(tuple(shape), dtype),
        grid_spec=pltpu.PrefetchScalarGridSpec(num_scalar_prefetch=1, grid=(depth, steps),
                                               in_specs=[pl.BlockSpec((None, rows_t, cols), lambda l, i, p: (l, i, 0))],
                                               out_specs=out_spec),
        compiler_params=_cparams(("arbitrary", "arbitrary")),
    )(place, shard)


def _gather_weights(wholes, chip_axes):
    n = len(wholes)
    depth = wholes[0].shape[0]
    hl = depth // 2

    def body(*refs):
        dst = refs[n:2 * n]
        ici_send, ici_recv, pair_send, pair_recv = refs[2 * n:]
        x, y, c, chips = _position()
        me = 2 * x + y

        def block(t, chip, lo, count):
            size = wholes[t].shape[chip_axes[t]] // N_CHIPS
            return _cut(_cut(dst[t], chip_axes[t], pl.multiple_of(chip * size, size), size), 0, lo, count)

        lo, other_lo = c * hl, (1 - c) * hl
        sends = []
        for t in range(n):
            for j, chip in enumerate(chips):
                cp = _remote(block(t, me, lo, hl), block(t, me, lo, hl), ici_send.at[t, j], ici_recv.at[t, j], (*chip, c))
                cp.start()
                sends.append(cp)
        for t in range(n):
            for j, chip in enumerate(chips):
                landed = block(t, 2 * chip[0] + chip[1], lo, hl)
                _remote(landed, landed, ici_send.at[t, j], ici_recv.at[t, j], (*chip, c)).wait_recv()
                cp = _remote(landed, landed, pair_send.at[t, j], pair_recv.at[t, j], (x, y, 1 - c))
                cp.start()
                sends.append(cp)
        for t in range(n):
            for j, chip in enumerate(chips):
                passed = block(t, 2 * chip[0] + chip[1], other_lo, hl)
                _remote(passed, passed, pair_send.at[t, j], pair_recv.at[t, j], (x, y, 1 - c)).wait_recv()
        for cp in sends:
            cp.wait_send()

    return _comm_call(body, "gather_weights", wholes, [jax.ShapeDtypeStruct(w.shape, w.dtype) for w in wholes],
                      [pltpu.SemaphoreType.DMA((n, 3))] * 4, alias=True)


def _pair_swap(parts, core_axes):
    n, depth = len(parts), len(parts[0])
    flat = [arr for per_layer in parts for arr in per_layer]
    halves = []
    for per_layer, axis in zip(parts, core_axes):
        shape = list(per_layer[0].shape)
        shape[axis] //= 2
        halves.append(jax.ShapeDtypeStruct((depth, *shape), per_layer[0].dtype))

    def body(*refs):
        src, dst = refs[:n * depth], refs[n * depth:n * depth + n]
        send_sem, recv_sem = refs[n * depth + n:]
        x, y, c, _ = _position()
        copies = []
        for t in range(n):
            size = halves[t].shape[1 + core_axes[t]]
            for l in range(depth):
                piece = _cut(src[t * depth + l], core_axes[t], pl.multiple_of((1 - c) * size, size), size)
                cp = _remote(piece, dst[t].at[l], send_sem.at[t, l], recv_sem.at[t, l], (x, y, 1 - c))
                cp.start()
                copies.append(cp)
        for cp in copies:
            cp.wait()

    return _comm_call(body, "pair_swap", flat, halves, [pltpu.SemaphoreType.DMA((n, depth))] * 2)


def _chip_exchange(partials, chip_axes):
    n = len(partials)
    quarters = []
    for arr, axis in zip(partials, chip_axes):
        shape = list(arr.shape)
        shape[axis] //= N_CHIPS
        quarters.append(jax.ShapeDtypeStruct((N_CHIPS - 1, *shape), arr.dtype))

    def body(*refs):
        src, dst = refs[:n], refs[n:2 * n]
        send_sem, recv_sem = refs[2 * n:]
        x, y, c, chips = _position()
        copies = []
        for t in range(n):
            size = quarters[t].shape[1 + chip_axes[t]]
            for j, chip in enumerate(chips):
                piece = _cut(src[t], chip_axes[t], pl.multiple_of((2 * chip[0] + chip[1]) * size, size), size)
                cp = _remote(piece, dst[t].at[j], send_sem.at[t, j], recv_sem.at[t, j], (*chip, c))
                cp.start()
                copies.append(cp)
        for cp in copies:
            cp.wait()

    return _comm_call(body, "chip_exchange", partials, quarters, [pltpu.SemaphoreType.DMA((n, N_CHIPS - 1))] * 2)


def _pair_join(blocks, core_axes):
    n = len(blocks)

    def body(*refs):
        dst = refs[n:2 * n]
        send_sem, recv_sem = refs[2 * n:]
        x, y, c, _ = _position()
        copies = []
        for t in range(n):
            size = blocks[t].shape[core_axes[t]] // 2
            mine = _cut(dst[t], core_axes[t], pl.multiple_of(c * size, size), size)
            cp = _remote(mine, mine, send_sem.at[t], recv_sem.at[t], (x, y, 1 - c))
            cp.start()
            copies.append(cp)
        for t, cp in enumerate(copies):
            size = blocks[t].shape[core_axes[t]] // 2
            other = _cut(dst[t], core_axes[t], pl.multiple_of((1 - c) * size, size), size)
            cp.wait_send()
            _remote(other, other, send_sem.at[t], recv_sem.at[t], (x, y, 1 - c)).wait_recv()

    return _comm_call(body, "pair_join", blocks, [jax.ShapeDtypeStruct(b.shape, b.dtype) for b in blocks],
                      [pltpu.SemaphoreType.DMA((n,))] * 2, alias=True)


def _gather_small(vec):
    n_dev = 2 * N_CHIPS

    def body(src, dst, send_sem, recv_sem, local_sem):
        x, y, c, _ = _position()
        flip = lambda v, bit: 1 - v if bit else v
        mine = pltpu.make_async_copy(src, dst.at[4 * x + 2 * y + c], local_sem)
        mine.start()
        copies = []
        for mask in range(1, n_dev):
            peer = (flip(x, mask & 4), flip(y, mask & 2), flip(c, mask & 1))
            cp = _remote(src, dst.at[4 * x + 2 * y + c], send_sem.at[mask - 1], recv_sem.at[mask - 1], peer)
            cp.start()
            copies.append((cp, peer))
        for mask, (cp, peer) in enumerate(copies, start=1):
            theirs = dst.at[4 * peer[0] + 2 * peer[1] + peer[2]]
            _remote(src, theirs, send_sem.at[mask - 1], recv_sem.at[mask - 1], peer).wait_recv()
        for cp, _ in copies:
            cp.wait_send()
        mine.wait()

    out = [jax.ShapeDtypeStruct((n_dev, *vec.shape), vec.dtype)]
    return _comm_call(body, "gather_small", [vec], out, [pltpu.SemaphoreType.DMA((n_dev - 1,))] * 2 + [pltpu.SemaphoreType.DMA(())])[0]


def _row_tile(rows, row_bytes, budget):
    tile = rows
    for cand in (512, 256, 128, 64, 32, 16, 8):
        if rows % cand == 0:
            tile = cand
            if cand * row_bytes <= budget:
                break
    return tile


def _pair_sum(parts, got, core_axis, place, *, name):
    depth, rows, cols = got.shape
    rows_t = _row_tile(rows, cols * 4 * depth * 6, 24 * 1024 * 1024)
    steps = rows // rows_t
    if core_axis == 0:
        part_spec = pl.BlockSpec((rows_t, cols), lambda i, p: (p[0] * steps + i, 0))
    else:
        part_spec = pl.BlockSpec((rows_t, cols), lambda i, p: (i, p[0]))
    stack_spec = pl.BlockSpec((depth, rows_t, cols), lambda i, p: (0, i, 0))

    def body(place_ref, *refs):
        got_ref, out_ref = refs[depth], refs[depth + 1]
        for l in range(depth):
            out_ref[l] = (refs[l][...] + got_ref[l]).astype(BF16)

    return pl.pallas_call(
        body, name=name, out_shape=jax.ShapeDtypeStruct(got.shape, BF16),
        grid_spec=pltpu.PrefetchScalarGridSpec(num_scalar_prefetch=1, grid=(steps,), in_specs=[part_spec] * depth + [stack_spec],
                                               out_specs=stack_spec),
        compiler_params=_cparams(("arbitrary",)),
    )(place, *parts, got)


def _chip_sum(partial, got, chip_axis, core_axis, place, *, name):
    _, depth, rows, cols = got.shape
    rows_t = _row_tile(rows, cols * 4 * 10, 24 * 1024 * 1024)
    steps = rows // rows_t
    if chip_axis == 1:
        own_spec = pl.BlockSpec((None, rows_t, cols), lambda l, i, p: (l, p[1] * steps + i, 0))
    else:
        own_spec = pl.BlockSpec((None, rows_t, cols), lambda l, i, p: (l, i, p[1]))
    got_spec = pl.BlockSpec((N_CHIPS - 1, None, rows_t, cols), lambda l, i, p: (0, l, i, 0))
    shape = [depth, rows, cols]
    shape[core_axis] *= 2
    if core_axis == 1:
        out_spec = pl.BlockSpec((None, rows_t, cols), lambda l, i, p: (l, p[0] * steps + i, 0))
    else:
        out_spec = pl.BlockSpec((None, rows_t, cols), lambda l, i, p: (l, i, p[0]))

    def body(place_ref, own_ref, got_ref, out_ref):
        up = lambda val: val.astype(F32)
        out_ref[...] = ((up(own_ref[...]) + up(got_ref[0])) + up(got_ref[1])) + up(got_ref[2])

    return pl.pallas_call(
        body, name=name, out_shape=jax.ShapeDtypeStruct(tuple(shape), F32),
        grid_spec=pltpu.PrefetchScalarGridSpec(num_scalar_prefetch=1, grid=(depth, steps), in_specs=[own_spec, got_spec],
                                               out_specs=out_spec),
        compiler_params=_cparams(("arbitrary", "arbitrary")),
    )(place, partial, got)


def _sum_devices(stack, *, name):
    def body(src_ref, out_ref):
        total = src_ref[0]
        for d in range(1, stack.shape[0]):
            total = total + src_ref[d]
        out_ref[...] = total

    return pl.pallas_call(body, name=name, out_shape=jax.ShapeDtypeStruct(stack.shape[1:], F32))(stack)


def kernel(x, w_in, b_in, conv_w, conv_b, conv_ln_g, conv_ln_b, w_att_proj, w_conv_proj, b_conv_proj, w_out, ln_g, ln_b, loss_target, m_w_in, m_b_in, m_conv_w, m_conv_b, m_conv_ln_g, m_conv_ln_b, m_w_att_proj, m_w_conv_proj, m_b_conv_proj, m_w_out, m_ln_g, m_ln_b, v_w_in, v_b_in, v_conv_w, v_conv_b, v_conv_ln_g, v_conv_ln_b, v_w_att_proj, v_w_conv_proj, v_b_conv_proj, v_w_out, v_ln_g, v_ln_b):
    depth = w_in.shape[0]
    d_model = x.shape[-1]
    half = d_model // 2
    chip = 2 * lax.axis_index("x") + lax.axis_index("y")
    place = jnp.stack([lax.axis_index("c"), chip]).astype(jnp.int32)
    vec3 = lambda v: v.reshape(depth, 1, -1)

    taps = jnp.pad(conv_w, ((0, 0), (0, CONV_PAD - CONV_WIDTH), (0, 0)))
    gathered = [("w_in", w_in, 2, BF16), ("w_att_proj", w_att_proj, 2, BF16), ("w_conv_proj", w_conv_proj, 2, BF16),
                ("w_out", w_out, 1, BF16), ("conv_w", taps, 2, F32)]
    wholes = [_place_block(arr, axis, place, dtype, name="place_" + n) for n, arr, axis, dtype in gathered]
    win, watt, wconv, wout, convw = _gather_weights(wholes, [axis for _, _, axis, _ in gathered])
    loss, grad_x, grads = _local_grads(x[0], loss_target[0], win, b_in, convw, vec3(conv_b), vec3(conv_ln_g), vec3(conv_ln_b),
                                       watt, wconv, vec3(b_conv_proj), wout, vec3(ln_g), vec3(ln_b))
    loss = lax.psum(loss[0, 0], ("x", "y", "c"))

    big = ["w_in", "w_att_proj", "w_conv_proj", "w_out"]
    core_axes, chip_axes = [0, 0, 0, 1], [1, 1, 1, 0]
    parts = [[grads[l][n] for l in range(depth)] for n in big]
    got = _pair_swap(parts, core_axes)
    partials = [_pair_sum(parts[t], got[t], core_axes[t], place, name="pair_sum_" + big[t]) for t in range(len(big))]
    got = _chip_exchange(partials, [a + 1 for a in chip_axes])
    finals = [_chip_sum(partials[t], got[t], chip_axes[t] + 1, core_axes[t] + 1, place, name="chip_sum_" + big[t])
              for t in range(len(big))]
    reduced = dict(zip(big, _pair_join(finals, [a + 1 for a in core_axes])))

    small = ["b_in", "conv_b", "conv_ln_g", "conv_ln_b", "b_conv_proj", "ln_g", "ln_b"]
    packed = jnp.stack([jnp.concatenate([grads[l][n] for n in small] + [grads[l]["conv_w"].reshape(-1)]) for l in range(depth)])
    total = _sum_devices(_gather_small(packed), name="sum_devices")
    offset = 0
    for n in small:
        width = grads[0][n].shape[0]
        reduced[n] = total[:, offset:offset + width]
        offset += width
    taps = total[:, offset:].reshape(depth, CONV_WIDTH, half)
    reduced["conv_w"] = lax.dynamic_slice_in_dim(taps, chip * conv_w.shape[2], conv_w.shape[2], axis=2)

    names = ["w_in", "b_in", "conv_w", "conv_b", "conv_ln_g", "conv_ln_b", "w_att_proj", "w_conv_proj", "b_conv_proj", "w_out", "ln_g", "ln_b"]
    weights = dict(zip(names, (w_in, b_in, conv_w, conv_b, conv_ln_g, conv_ln_b, w_att_proj, w_conv_proj, b_conv_proj, w_out, ln_g, ln_b)))
    first = dict(zip(names, (m_w_in, m_b_in, m_conv_w, m_conv_b, m_conv_ln_g, m_conv_ln_b, m_w_att_proj, m_w_conv_proj, m_b_conv_proj, m_w_out, m_ln_g, m_ln_b)))
    second = dict(zip(names, (v_w_in, v_b_in, v_conv_w, v_conv_b, v_conv_ln_g, v_conv_ln_b, v_w_att_proj, v_w_conv_proj, v_b_conv_proj, v_w_out, v_ln_g, v_ln_b)))
    delta, new_m, new_v = {}, {}, {}
    for n in names:
        shape = weights[n].shape
        flat = lambda arr: arr.reshape(-1, shape[-1])
        d, m, v = _adamw(flat(weights[n]), flat(reduced[n]), flat(first[n]), flat(second[n]), name="adamw_" + n)
        delta[n], new_m[n], new_v[n] = d.reshape(shape), m.reshape(shape), v.reshape(shape)
    return (loss, grad_x[None], *[reduced[n].reshape(weights[n].shape) for n in names], *[delta[n] for n in names],
            *[new_m[n] for n in names], *[new_v[n] for n in names])
```

```python
import functools
import math

import jax
import jax.numpy as jnp
from jax import lax
from jax.experimental import pallas as pl
from jax.experimental.pallas import tpu as pltpu

F32 = jnp.float32
BF16 = jnp.bfloat16

HEAD_DIM = 64
LANES = 128
CONV_WIDTH = 31
CONV_PAD = 32
SUBLANES = 8
LN_EPS = 1e-5
DEPTH = 4
DEEPNORM_ALPHA = (2 * DEPTH) ** 0.25
ATT_SCALE = HEAD_DIM ** -0.5
ATT_TILE = 128
ATT_DEAD = -104.0
ATT_GROUP = 2
ATT_FILL = -1e30

ADAM_LR = 0.001
ADAM_B1 = 0.9
ADAM_B2 = 0.999
ADAM_EPS = 1e-08
ADAM_WD = 0.01
ADAM_STEP = 10

VMEM_LIMIT = 56 * 1024 * 1024

N_CHIPS = 4


def _cparams(sem):
    return pltpu.CompilerParams(dimension_semantics=sem, vmem_limit_bytes=VMEM_LIMIT)


def _sigmoid(x):
    return 1.0 / (1.0 + jnp.exp(-x))


def _fit(tile, dim):
    assert dim % LANES == 0
    tile = min(tile, dim) // LANES * LANES
    while dim % tile:
        tile -= LANES
    return tile


_DIMS = {"nn": ((1,), (0,)), "nt": ((1,), (1,)), "tn": ((0,), (0,))}


def _matmul(a, b, *, mode, name, layer=None, bias=None, add=None, colsum=False, out_dtype=F32, tm=1024, tn=512, tk=1024):
    b_shape = b.shape if layer is None else b.shape[1:]
    if mode == "nn":
        (m, k), (k2, n) = a.shape, b_shape
    elif mode == "nt":
        (m, k), (n, k2) = a.shape, b_shape
    else:
        (k, m), (k2, n) = a.shape, b_shape
    assert k == k2
    tm, tn, tk = _fit(tm, m), _fit(tn, n), _fit(tk, k)
    gm, gn, nk = m // tm, n // tn, k // tk

    a_spec = pl.BlockSpec((tk, tm), lambda i, j, kk: (kk, i)) if mode == "tn" else pl.BlockSpec((tm, tk), lambda i, j, kk: (i, kk))
    if layer is None:
        b_spec = pl.BlockSpec((tn, tk), lambda i, j, kk: (j, kk)) if mode == "nt" else pl.BlockSpec((tk, tn), lambda i, j, kk: (kk, j))
    elif mode == "nt":
        b_spec = pl.BlockSpec((None, tn, tk), lambda i, j, kk: (layer, j, kk))
    else:
        b_spec = pl.BlockSpec((None, tk, tn), lambda i, j, kk: (layer, kk, j))
    in_specs, operands = [a_spec, b_spec], [a, b]
    if bias is not None:
        in_specs.append(pl.BlockSpec((1, tn), lambda i, j, kk: (0, j)))
        operands.append(bias)
    if add is not None:
        in_specs.append(pl.BlockSpec((tm, tn), lambda i, j, kk: (i, j)))
        operands.append(add)
    out_shape = [jax.ShapeDtypeStruct((m, n), out_dtype)]
    out_specs = [pl.BlockSpec((tm, tn), lambda i, j, kk: (i, j))]
    scratch = [pltpu.VMEM((tm, tn), F32)] if nk > 1 else []
    if colsum:
        assert mode == "tn"
        out_shape.append(jax.ShapeDtypeStruct((gm, 1, n), F32))
        out_specs.append(pl.BlockSpec((1, 1, tn), lambda i, j, kk: (i, 0, j)))
        if nk > 1:
            scratch.append(pltpu.VMEM((1, tn), F32))
    has_bias, has_add = bias is not None, add is not None

    def body(*refs):
        refs = list(refs)
        a_ref, b_ref = refs[0], refs[1]
        pos = 2
        bias_ref = add_ref = None
        if has_bias:
            bias_ref = refs[pos]
            pos += 1
        if has_add:
            add_ref = refs[pos]
            pos += 1
        o_ref = refs[pos]
        pos += 1
        cs_ref = None
        if colsum:
            cs_ref = refs[pos]
            pos += 1

        def finish(out, sums):
            if has_bias:
                out = out + bias_ref[...]
            if has_add:
                out = out + add_ref[...]
            o_ref[...] = out.astype(out_dtype)
            if colsum:
                cs_ref[0] = sums

        bv = b_ref[...]
        prod = lax.dot_general(a_ref[...].astype(BF16), bv.astype(BF16), (_DIMS[mode], ((), ())), preferred_element_type=F32)
        sums = jnp.sum(bv.astype(F32), axis=0, keepdims=True) if colsum else None
        if nk == 1:
            finish(prod, sums)
            return
        acc_ref = refs[pos]
        cs_acc = refs[pos + 1] if colsum else None
        kk = pl.program_id(2)

        @pl.when(kk == 0)
        def _():
            acc_ref[...] = jnp.zeros_like(acc_ref)
            if colsum:
                cs_acc[...] = jnp.zeros_like(cs_acc)

        acc_ref[...] += prod
        if colsum:
            cs_acc[...] += sums

        @pl.when(kk == nk - 1)
        def _():
            finish(acc_ref[...], cs_acc[...] if colsum else None)

    res = pl.pallas_call(
        body, name=name, grid=(gm, gn, nk), in_specs=in_specs, out_specs=out_specs, out_shape=out_shape,
        scratch_shapes=scratch, compiler_params=_cparams(("parallel", "parallel", "arbitrary")),
    )(*operands)
    if colsum:
        return res[0], res[1][0]
    return res[0]


def _scan_matrices():
    t = ATT_TILE
    r = lax.broadcasted_iota(jnp.int32, (t, t), 0)
    c = lax.broadcasted_iota(jnp.int32, (t, t), 1)
    ones = jnp.ones((t, t), F32)
    suffix = jnp.concatenate([(r > c).astype(F32), ones], axis=1)
    prefix = jnp.concatenate([(r < c).astype(F32), ones], axis=1)
    stack = lambda mat: jnp.concatenate([mat, mat], axis=0).astype(BF16)
    return stack(suffix), stack(prefix)


def _split_halves(val):
    hi = val.astype(BF16)
    lo = (val - hi.astype(F32)).astype(BF16)
    return jnp.concatenate([hi, lo], axis=1)


def _split_scan(val, mat_ref):
    return jnp.dot(_split_halves(val), mat_ref[...], preferred_element_type=F32)


def _att_tile_scores(q, k, masked):
    t = ATT_TILE
    z = lax.dot_general(q, k, (((1,), (1,)), ((), ())), preferred_element_type=F32)
    sp = jnp.log(1.0 + jnp.exp(-jnp.abs(z)))
    f = jnp.minimum(-z, 0.0) - sp
    a = f + z
    causal = None
    if masked:
        causal = lax.broadcasted_iota(jnp.int32, (t, t), 1) < lax.broadcasted_iota(jnp.int32, (t, t), 0)
        f = jnp.where(causal, f, 0.0)
    return f, a, causal


def _any_alive(runs):
    top = functools.reduce(jnp.maximum, [run for per_head in runs for run in per_head])
    return (jnp.max(top) > ATT_DEAD).astype(jnp.int32)


def _head_copies(seq, src_ref, scale, lo_ref, hi_ref, plain_ref):
    chunk = min(256, seq)
    low = lax.broadcasted_iota(jnp.int32, (chunk, LANES), 1) < HEAD_DIM

    def step(r, carry):
        rows = pl.ds(pl.multiple_of(r * chunk, chunk), chunk)
        val = src_ref[rows, :]
        if scale != 1.0:
            val = val * scale
        if lo_ref is not None:
            lo_ref[rows, :] = jnp.where(low, val, 0.0).astype(BF16)
            hi_ref[rows, :] = jnp.where(low, 0.0, val).astype(BF16)
        if plain_ref is not None:
            plain_ref[rows, :] = val.astype(BF16)
        return carry

    lax.fori_loop(0, seq // chunk, step, 0)


def _attn_fwd(u, scan_suffix, *, seq, d_att, name):
    t = ATT_TILE
    nq = seq // t
    pairs = d_att // LANES
    grp = ATT_GROUP
    assert nq % grp == 0

    def body(q_ref, k_ref, v_ref, um_ref, o_ref, q0, q1, kk, v0, v1, f2_s, a_s, lg_s, tot_s, run_s, acc_s):
        _head_copies(seq, q_ref, ATT_SCALE, q0, q1, None)
        _head_copies(seq, k_ref, 1.0, None, None, kk)
        _head_copies(seq, v_ref, 1.0, v0, v1, None)

        heads = ((q0, v0), (q1, v1))

        def group(gi, carry):
            qb0 = gi * grp
            qrows = [pl.ds(pl.multiple_of((qb0 + g) * t, t), t) for g in range(grp)]
            qv = [[qs[qrows[g], :] for g in range(grp)] for qs, _ in heads]

            chains = [(h, g) for g in range(grp) for h in range(2)]

            def key_rows(g, i):
                return pl.ds(pl.multiple_of(jnp.maximum(qb0 + g - i, 0) * t, t), t)

            def stage1(i, masked):
                out = []
                for h, g in chains:
                    f, a, _ = _att_tile_scores(qv[h][g], kk[key_rows(g, i), :], masked)
                    out.append((_split_halves(f), a))
                return out

            def stage2(halves, a, masked):
                scan = jnp.dot(halves, um_ref[...], preferred_element_type=F32)
                logit = a + scan[:, :t]
                if masked:
                    causal = lax.broadcasted_iota(jnp.int32, (t, t), 1) < lax.broadcasted_iota(jnp.int32, (t, t), 0)
                    logit = jnp.where(causal, logit, ATT_FILL)
                return logit, scan[:, t:]

            def put(halves_a=None, logit_total=None):
                for c in range(len(chains)):
                    if halves_a is not None:
                        f2_s[c], a_s[c] = halves_a[c]
                    if logit_total is not None:
                        lg_s[c], tot_s[c] = logit_total[c]

            first = stage1(0, True)
            put(halves_a=stage1(1, False), logit_total=[stage2(f2, a, True) for f2, a in first])
            for c in range(len(chains)):
                run_s[c] = jnp.zeros((t, t), F32)
            for g in range(grp):
                acc_s[g] = jnp.zeros((t, LANES), F32)

            def wbody(st):
                i = st[0]
                held = [(f2_s[c], a_s[c]) for c in range(len(chains))]
                logits = [lg_s[c] for c in range(len(chains))]
                totals = [tot_s[c] for c in range(len(chains))]
                runs = [run_s[c] for c in range(len(chains))]
                accs = [acc_s[g] for g in range(grp)]
                for c, (h, g) in enumerate(chains):
                    run = runs[c] + jnp.where(qb0 + g - i >= 0, 0.0, ATT_FILL)
                    w = jnp.exp(logits[c] + run)
                    accs[g] = accs[g] + jnp.dot(w.astype(BF16), heads[h][1][key_rows(g, i), :], preferred_element_type=F32)
                    runs[c] = run + totals[c]
                ahead2 = [stage2(f2, a, False) for f2, a in held]
                ahead1 = stage1(i + 2, False)
                put(halves_a=ahead1, logit_total=ahead2)
                for c in range(len(chains)):
                    run_s[c] = runs[c]
                for g in range(grp):
                    acc_s[g] = accs[g]
                more = jnp.logical_and(i + 1 <= qb0 + grp - 1, _any_alive([runs]) > 0)
                return i + 1, more.astype(jnp.int32)

            lax.while_loop(lambda st: st[1] > 0, wbody, (jnp.int32(0), jnp.int32(1)))
            for g in range(grp):
                o_ref[qrows[g], :] = acc_s[g]
            return carry

        lax.fori_loop(0, nq // grp, group, 0)

    blk = lambda base: pl.BlockSpec((seq, LANES), lambda j, base=base: (0, base + j))
    return pl.pallas_call(
        body, name=name, grid=(pairs,),
        in_specs=[blk(0), blk(pairs), blk(2 * pairs), pl.BlockSpec((2 * t, 2 * t), lambda j: (0, 0))],
        out_specs=pl.BlockSpec((seq, LANES), lambda j: (0, j)),
        out_shape=jax.ShapeDtypeStruct((seq, d_att), F32),
        scratch_shapes=[pltpu.VMEM((seq, LANES), BF16)] * 5 + [pltpu.VMEM((2 * grp, t, 2 * t), BF16)]
        + [pltpu.VMEM((2 * grp, t, t), F32)] * 4 + [pltpu.VMEM((grp, t, LANES), F32)],
        compiler_params=_cparams(("parallel",)),
    )(u, u, u, scan_suffix)


def _attn_bwd(u, d_att_out, scan_suffix, scan_prefix, *, seq, d_att, name):
    t = ATT_TILE
    nq = seq // t
    pairs = d_att // LANES
    grp = ATT_GROUP
    assert nq % grp == 0

    def body(q_ref, k_ref, v_ref, do_ref, um_ref, pm_ref, dq_ref, dk_ref, dv_ref,
             q0, q1, kk, k0, k1, vv, do0, do1, dq_acc, dk_acc, dv_acc, g_st, b_st,
             f2_s, a_s, lg_s, tot_s, dw_s, run_s, p_s, pt_s, grun_s, dq_s):
        _head_copies(seq, q_ref, ATT_SCALE, q0, q1, None)
        _head_copies(seq, k_ref, 1.0, k0, k1, kk)
        _head_copies(seq, v_ref, 1.0, None, None, vv)
        _head_copies(seq, do_ref, 1.0, do0, do1, None)
        dk_acc[...] = jnp.zeros_like(dk_acc)
        dv_acc[...] = jnp.zeros_like(dv_acc)

        heads = ((q0, k0, do0), (q1, k1, do1))
        tn = (((0,), (0,)), ((), ()))
        nt = (((1,), (1,)), ((), ()))

        def group(gi, carry):
            qb0 = gi * grp
            qrows = [pl.ds(pl.multiple_of((qb0 + g) * t, t), t) for g in range(grp)]
            qv = [[qs[qrows[g], :] for g in range(grp)] for qs, _, _ in heads]
            dov = [[dos[qrows[g], :] for g in range(grp)] for _, _, dos in heads]

            def key_rows(g, i):
                return pl.ds(pl.multiple_of(jnp.maximum(qb0 + g - i, 0) * t, t), t)

            chains = [(h, g) for g in range(grp) for h in range(2)]
            every = range(len(chains))

            def stage1(i, masked):
                out = []
                for h, g in chains:
                    f, a, _ = _att_tile_scores(qv[h][g], kk[key_rows(g, i), :], masked)
                    out.append((_split_halves(f), a))
                return out

            def stage2(i, c, halves, a, masked):
                h, g = chains[c]
                scan = jnp.dot(halves, um_ref[...], preferred_element_type=F32)
                logit = a + scan[:, :t]
                if masked:
                    causal = lax.broadcasted_iota(jnp.int32, (t, t), 1) < lax.broadcasted_iota(jnp.int32, (t, t), 0)
                    logit = jnp.where(causal, logit, ATT_FILL)
                b_st[c, i] = jnp.exp(a)
                dw = lax.dot_general(dov[h][g], vv[key_rows(g, i), :], nt, preferred_element_type=F32)
                return logit, scan[:, t:], dw

            def put(held=None, ready=None):
                for c in every:
                    if held is not None:
                        f2_s[c], a_s[c] = held[c]
                    if ready is not None:
                        lg_s[c], tot_s[c], dw_s[c] = ready[c]

            head0 = stage1(0, True)
            put(held=stage1(1, False), ready=[stage2(0, c, *head0[c], True) for c in every])
            for c in every:
                run_s[c] = jnp.zeros((t, t), F32)

            def wbody(st):
                i = st[0]
                held = [(f2_s[c], a_s[c]) for c in every]
                ready = [(lg_s[c], tot_s[c], dw_s[c]) for c in every]
                runs = [run_s[c] for c in every]
                dvs = [jnp.zeros((t, LANES), F32)] * grp
                for c, (h, g) in enumerate(chains):
                    logit, total, dw = ready[c]
                    run = runs[c] + jnp.where(qb0 + g - i >= 0, 0.0, ATT_FILL)
                    w = jnp.exp(logit + run)
                    g_st[c, i] = w * dw
                    dvs[g] = dvs[g] + lax.dot_general(w.astype(BF16), dov[h][g], tn, preferred_element_type=F32)
                    runs[c] = run + total
                for g in range(grp):
                    dv_acc[key_rows(g, i), :] += dvs[g]
                ahead2 = [stage2(i + 1, c, *held[c], False) for c in every]
                ahead1 = stage1(i + 2, False)
                put(held=ahead1, ready=ahead2)
                for c in every:
                    run_s[c] = runs[c]
                more = jnp.logical_and(i + 1 <= qb0 + grp - 1, _any_alive([runs]) > 0)
                return i + 1, more.astype(jnp.int32)

            steps = lax.while_loop(lambda st: st[1] > 0, wbody, (jnp.int32(0), jnp.int32(1)))[0]

            def prefix(i):
                out = []
                for c in every:
                    scan = _split_scan(g_st[c, i], pm_ref)
                    out.append((scan[:, :t], scan[:, t:]))
                return out

            def back(i, masked):
                sums = [(p_s[c], pt_s[c]) for c in every]
                gruns = [grun_s[c] for c in every]
                dqs = [dq_s[g] for g in range(grp)]
                dks = [jnp.zeros((t, LANES), F32)] * grp
                for c, (h, g) in enumerate(chains):
                    gt = g_st[c, i]
                    dz = gt - b_st[c, i] * (gt + gruns[c] + sums[c][0])
                    if masked:
                        causal = lax.broadcasted_iota(jnp.int32, (t, t), 1) < lax.broadcasted_iota(jnp.int32, (t, t), 0)
                        dz = jnp.where(causal, dz, 0.0)
                    dz = dz.astype(BF16)
                    dqs[g] = dqs[g] + jnp.dot(dz, heads[h][1][key_rows(g, i), :], preferred_element_type=F32)
                    dks[g] = dks[g] + lax.dot_general(dz, qv[h][g], tn, preferred_element_type=F32)
                    gruns[c] = gruns[c] + sums[c][1]
                for g in range(grp):
                    dk_acc[key_rows(g, i), :] += dks[g]
                return gruns, dqs

            def keep(sums=None, gruns=None, dqs=None):
                for c in every:
                    if sums is not None:
                        p_s[c], pt_s[c] = sums[c]
                    if gruns is not None:
                        grun_s[c] = gruns[c]
                if dqs is not None:
                    for g in range(grp):
                        dq_s[g] = dqs[g]

            keep(sums=prefix(steps - 1), gruns=[jnp.zeros((t, t), F32)] * len(chains), dqs=[jnp.zeros((t, LANES), F32)] * grp)

            def bbody(j, carry2):
                i = steps - 1 - j
                gruns, dqs = back(i, False)
                keep(sums=prefix(i - 1), gruns=gruns, dqs=dqs)
                return carry2

            lax.fori_loop(0, steps - 1, bbody, 0)
            _, dqs = back(0, True)
            for g in range(grp):
                dq_acc[qrows[g], :] = dqs[g]
            return carry

        lax.fori_loop(0, nq // grp, group, 0)

        chunk = min(256, seq)

        def emit(r, carry):
            rows = pl.ds(pl.multiple_of(r * chunk, chunk), chunk)
            dq_ref[rows, :] = (dq_acc[rows, :] * ATT_SCALE).astype(BF16)
            dk_ref[rows, :] = dk_acc[rows, :].astype(BF16)
            dv_ref[rows, :] = dv_acc[rows, :].astype(BF16)
            return carry

        lax.fori_loop(0, seq // chunk, emit, 0)

    blk = lambda base: pl.BlockSpec((seq, LANES), lambda j, base=base: (0, base + j), pipeline_mode=pl.Buffered(1))
    mat = pl.BlockSpec((2 * t, 2 * t), lambda j: (0, 0))
    out = jax.ShapeDtypeStruct((seq, d_att), BF16)
    return pl.pallas_call(
        body, name=name, grid=(pairs,),
        in_specs=[blk(0), blk(pairs), blk(2 * pairs), blk(0), mat, mat],
        out_specs=[pl.BlockSpec((seq, LANES), lambda j: (0, j))] * 3,
        out_shape=[out, out, out],
        scratch_shapes=[pltpu.VMEM((seq, LANES), BF16)] * 8 + [pltpu.VMEM((seq, LANES), F32)] * 3
        + [pltpu.VMEM((2 * grp, nq + 1, t, t), F32)] * 2 + [pltpu.VMEM((2 * grp, t, 2 * t), BF16)]
        + [pltpu.VMEM((2 * grp, t, t), F32)] * 8 + [pltpu.VMEM((grp, t, LANES), F32)],
        compiler_params=_cparams(("parallel",)),
    )(u, u, u, d_att_out, scan_suffix, scan_prefix)


CONV_ROWS = 256


def _shifted(window, residue, rows):
    total = rows + CONV_PAD
    return window if residue == 0 else pltpu.roll(window, total - residue, 0)


def _glu_to_pad(seq, a_ref, b_ref, pad_ref):
    chunk = min(CONV_ROWS, seq)
    pad_ref[pl.ds(0, CONV_PAD), :] = jnp.zeros((CONV_PAD, LANES), F32)

    def step(r, carry):
        rows = pl.ds(pl.multiple_of(r * chunk, chunk), chunk)
        pad_ref[pl.ds(pl.multiple_of(r * chunk + CONV_PAD, SUBLANES), chunk), :] = a_ref[rows, :] * _sigmoid(b_ref[rows, :])
        return carry

    lax.fori_loop(0, seq // chunk, step, 0)


def _conv_fwd(u, conv_w, conv_b, *, layer, seq, d_conv, col_a, col_b, name):
    blocks = d_conv // LANES
    rows_t = min(CONV_ROWS, seq)
    shift0 = CONV_PAD - (CONV_WIDTH - 1)

    def body(a_ref, b_ref, w_ref, bias_ref, o_ref, pad_ref):
        _glu_to_pad(seq, a_ref, b_ref, pad_ref)

        def step(r, carry):
            base = pl.multiple_of(r * rows_t, rows_t)
            window = pad_ref[pl.ds(base, rows_t + CONV_PAD), :]
            acc = jnp.zeros((rows_t, LANES), F32) + bias_ref[...]
            for residue in range(SUBLANES):
                moved = _shifted(window, residue, rows_t)
                for tap in range(CONV_WIDTH):
                    if (shift0 + tap) % SUBLANES == residue:
                        lo = (shift0 + tap) - residue
                        acc = acc + w_ref[tap:tap + 1, :] * moved[lo:lo + rows_t, :]
            o_ref[pl.ds(base, rows_t), :] = acc
            return carry

        lax.fori_loop(0, seq // rows_t, step, 0)

    return pl.pallas_call(
        body, name=name, grid=(blocks,),
        in_specs=[pl.BlockSpec((seq, LANES), lambda j: (0, col_a + j)), pl.BlockSpec((seq, LANES), lambda j: (0, col_b + j)),
                  pl.BlockSpec((None, CONV_PAD, LANES), lambda j: (layer, 0, j)),
                  pl.BlockSpec((None, 1, LANES), lambda j: (layer, 0, j))],
        out_specs=pl.BlockSpec((seq, LANES), lambda j: (0, j)),
        out_shape=jax.ShapeDtypeStruct((seq, d_conv), F32),
        scratch_shapes=[pltpu.VMEM((seq + CONV_PAD, LANES), F32)],
        compiler_params=_cparams(("parallel",)),
    )(u, u, conv_w, conv_b)


def _conv_bwd(u, dc1, conv_w, *, layer, seq, d_conv, col_a, col_b, name):
    blocks = d_conv // LANES
    rows_t = min(CONV_ROWS, seq)
    shift0 = CONV_PAD - (CONV_WIDTH - 1)

    def body(a_ref, b_ref, d_ref, w_ref, da_ref, db_ref, dw_ref, pad_ref, dpad_ref, dw_acc):
        _glu_to_pad(seq, a_ref, b_ref, pad_ref)
        dpad_ref[pl.ds(seq, CONV_PAD), :] = jnp.zeros((CONV_PAD, LANES), F32)

        def fill(r, carry):
            rows = pl.ds(pl.multiple_of(r * rows_t, rows_t), rows_t)
            dpad_ref[rows, :] = d_ref[rows, :]
            return carry

        lax.fori_loop(0, seq // rows_t, fill, 0)
        dw_acc[...] = jnp.zeros_like(dw_acc)

        def step(r, carry):
            base = pl.multiple_of(r * rows_t, rows_t)
            rows = pl.ds(base, rows_t)
            window = dpad_ref[pl.ds(base, rows_t + CONV_PAD), :]
            acc = jnp.zeros((rows_t, LANES), F32)
            for residue in range(SUBLANES):
                moved = _shifted(window, residue, rows_t)
                for tap in range(CONV_WIDTH):
                    off = CONV_WIDTH - 1 - tap
                    if off % SUBLANES == residue:
                        lo = off - residue
                        acc = acc + w_ref[tap:tap + 1, :] * moved[lo:lo + rows_t, :]
            sig = _sigmoid(b_ref[rows, :])
            a = a_ref[rows, :]
            da_ref[rows, :] = (acc * sig).astype(BF16)
            db_ref[rows, :] = (acc * a * sig * (1.0 - sig)).astype(BF16)
            dcur = d_ref[rows, :]
            cwin = pad_ref[pl.ds(base, rows_t + CONV_PAD), :]
            for residue in range(SUBLANES):
                moved = _shifted(cwin, residue, rows_t)
                for tap in range(CONV_WIDTH):
                    if (shift0 + tap) % SUBLANES == residue:
                        lo = (shift0 + tap) - residue
                        prod = dcur * moved[lo:lo + rows_t, :]
                        dw_acc[tap] += jnp.sum(prod.reshape(rows_t // SUBLANES, SUBLANES, LANES), axis=0)
            return carry

        lax.fori_loop(0, seq // rows_t, step, 0)
        dw_ref[...] = jnp.sum(dw_acc[...], axis=1)

    col = lambda base: pl.BlockSpec((seq, LANES), lambda j, base=base: (0, base + j))
    own = pl.BlockSpec((seq, LANES), lambda j: (0, j))
    return pl.pallas_call(
        body, name=name, grid=(blocks,),
        in_specs=[col(col_a), col(col_b), own, pl.BlockSpec((None, CONV_PAD, LANES), lambda j: (layer, 0, j))],
        out_specs=[own, own, pl.BlockSpec((CONV_PAD, LANES), lambda j: (0, j))],
        out_shape=[jax.ShapeDtypeStruct((seq, d_conv), BF16), jax.ShapeDtypeStruct((seq, d_conv), BF16),
                   jax.ShapeDtypeStruct((CONV_PAD, d_conv), F32)],
        scratch_shapes=[pltpu.VMEM((seq + CONV_PAD, LANES), F32), pltpu.VMEM((seq + CONV_PAD, LANES), F32),
                        pltpu.VMEM((CONV_PAD, SUBLANES, LANES), F32)],
        compiler_params=_cparams(("parallel",)),
    )(u, u, dc1, conv_w)


MIX_ROWS = 256


def _layer_norm_stats(val):
    mu = jnp.mean(val, axis=-1, keepdims=True)
    cen = val - mu
    var = jnp.mean(cen * cen, axis=-1, keepdims=True)
    rstd = lax.rsqrt(var + LN_EPS)
    return cen * rstd, rstd


def _layer_norm_bwd(dy, xhat, rstd, gain):
    dxhat = dy * gain
    m1 = jnp.mean(dxhat, axis=-1, keepdims=True)
    m2 = jnp.mean(dxhat * xhat, axis=-1, keepdims=True)
    dx = rstd * (dxhat - m1 - xhat * m2)
    return dx, jnp.sum(dy * xhat, axis=0, keepdims=True), jnp.sum(dy, axis=0, keepdims=True)


def _mix_forward(zatt, att, c1, zconv, gatt, gconv, x, w_att, w_conv, w_out, cln_g, cln_b, b_proj):
    s_zatt = _sigmoid(zatt)
    a_in = att * (zatt * s_zatt)
    chat, c_rstd = _layer_norm_stats(c1)
    c2 = chat * cln_g + cln_b
    s_c2 = _sigmoid(c2)
    c3 = c2 * s_c2
    s_zconv = _sigmoid(zconv)
    c_in = c3 * (zconv * s_zconv)
    a_in_b, c_in_b = a_in.astype(BF16), c_in.astype(BF16)
    ab = jnp.dot(a_in_b, w_att, preferred_element_type=F32)
    cb = jnp.dot(c_in_b, w_conv, preferred_element_type=F32) + b_proj
    s_gatt, s_gconv = _sigmoid(gatt), _sigmoid(gconv)
    merged_b = (s_gatt * ab + s_gconv * cb).astype(BF16)
    y = jnp.dot(merged_b, w_out, preferred_element_type=F32)
    h = DEEPNORM_ALPHA * x + y
    return dict(s_zatt=s_zatt, a_in_b=a_in_b, chat=chat, c_rstd=c_rstd, c2=c2, s_c2=s_c2, c3=c3, s_zconv=s_zconv,
                c_in_b=c_in_b, ab=ab, cb=cb, s_gatt=s_gatt, s_gconv=s_gconv, merged_b=merged_b, h=h)


def _u_blocks(rows_t, width, half):
    return [pl.BlockSpec((rows_t, half), lambda i, c=c: (i, c)) for c in (3, 6, 7, 8, 9, 10)]


def _of_layer(arr, layer):
    return pl.BlockSpec((None,) + arr.shape[1:], lambda i: (layer, 0, 0))


def _mix_fwd(u, att, c1, x, w_att, w_conv, w_out, cln_g, cln_b, b_proj, ln_g, ln_b, *, layer, seq, d_model, name):
    half = d_model // 2
    rows_t = min(MIX_ROWS, seq)

    def body(zatt_ref, zconv_ref, ga0, ga1, gc0, gc1, att_ref, c1_ref, x_ref, wa_ref, wc_ref, wo_ref,
             cg_ref, cb_ref, bp_ref, lg_ref, lb_ref, o_ref, ob_ref):
        gatt = jnp.concatenate([ga0[...], ga1[...]], axis=1)
        gconv = jnp.concatenate([gc0[...], gc1[...]], axis=1)
        mid = _mix_forward(zatt_ref[...], att_ref[...], c1_ref[...], zconv_ref[...], gatt, gconv, x_ref[...],
                           wa_ref[...], wc_ref[...], wo_ref[...], cg_ref[...], cb_ref[...], bp_ref[...])
        xhat, _ = _layer_norm_stats(mid["h"])
        out = xhat * lg_ref[...] + lb_ref[...]
        o_ref[...] = out
        ob_ref[...] = out.astype(BF16)

    row = lambda width: pl.BlockSpec((rows_t, width), lambda i: (i, 0))
    full = lambda arr: _of_layer(arr, layer)
    out = pl.BlockSpec((rows_t, d_model), lambda i: (i, 0))
    return pl.pallas_call(
        body, name=name, grid=(seq // rows_t,),
        in_specs=_u_blocks(rows_t, d_model, half) + [row(half), row(half), row(d_model), full(w_att), full(w_conv), full(w_out),
                                                     full(cln_g), full(cln_b), full(b_proj), full(ln_g), full(ln_b)],
        out_specs=[out, out],
        out_shape=[jax.ShapeDtypeStruct((seq, d_model), F32), jax.ShapeDtypeStruct((seq, d_model), BF16)],
        compiler_params=_cparams(("parallel",)),
    )(u, u, u, u, u, u, att, c1, x, w_att, w_conv, w_out, cln_g, cln_b, b_proj, ln_g, ln_b)


def _mix_bwd(u, att, c1, x, dxn, w_att, w_conv, w_out, cln_g, cln_b, b_proj, ln_g, *, layer, seq, d_model, name):
    half = d_model // 2
    rows_t = min(MIX_ROWS, seq)
    nt = ((1,), (1,))
    tn = ((0,), (0,))

    def body(zatt_ref, zconv_ref, ga0, ga1, gc0, gc1, att_ref, c1_ref, x_ref, dxn_ref, wa_ref, wc_ref, wo_ref,
             cg_ref, cb_ref, bp_ref, lg_ref,
             dzatt_ref, dzconv_ref, dgatt_ref, dgconv_ref, datt_ref, dc1_ref, dxres_ref, dwa_ref, dwc_ref, dwo_ref,
             dcg_ref, dcb_ref, dcbias_ref, dbp_ref, dlg_ref, dlb_ref):
        sums = (dwa_ref, dwc_ref, dwo_ref, dcg_ref, dcb_ref, dcbias_ref, dbp_ref, dlg_ref, dlb_ref)

        @pl.when(pl.program_id(0) == 0)
        def _():
            for ref in sums:
                ref[...] = jnp.zeros_like(ref)

        zatt, zconv, att = zatt_ref[...], zconv_ref[...], att_ref[...]
        gatt = jnp.concatenate([ga0[...], ga1[...]], axis=1)
        gconv = jnp.concatenate([gc0[...], gc1[...]], axis=1)
        wa, wc, wo = wa_ref[...], wc_ref[...], wo_ref[...]
        mid = _mix_forward(zatt, att, c1_ref[...], zconv, gatt, gconv, x_ref[...], wa, wc, wo,
                           cg_ref[...], cb_ref[...], bp_ref[...])
        xhat, rstd = _layer_norm_stats(mid["h"])
        dh, dlg, dlb = _layer_norm_bwd(dxn_ref[...], xhat, rstd, lg_ref[...])
        dlg_ref[...] += dlg
        dlb_ref[...] += dlb
        dxres_ref[...] = DEEPNORM_ALPHA * dh
        dy = dh.astype(BF16)
        dwo_ref[...] += lax.dot_general(mid["merged_b"], dy, (tn, ((), ())), preferred_element_type=F32)
        dmerged = lax.dot_general(dy, wo, (nt, ((), ())), preferred_element_type=F32)
        s_ga, s_gc, ab, cb = mid["s_gatt"], mid["s_gconv"], mid["ab"], mid["cb"]
        dgatt_ref[...] = (dmerged * ab * s_ga * (1.0 - s_ga)).astype(BF16)
        dgconv_ref[...] = (dmerged * cb * s_gc * (1.0 - s_gc)).astype(BF16)
        dab = dmerged * s_ga
        dcb = dmerged * s_gc
        dbp_ref[...] += jnp.sum(dcb, axis=0, keepdims=True)
        dab_b, dcb_b = dab.astype(BF16), dcb.astype(BF16)
        dwa_ref[...] += lax.dot_general(mid["a_in_b"], dab_b, (tn, ((), ())), preferred_element_type=F32)
        da_in = lax.dot_general(dab_b, wa, (nt, ((), ())), preferred_element_type=F32)
        s_za = mid["s_zatt"]
        datt_ref[...] = da_in * (zatt * s_za)
        dzatt_ref[...] = (da_in * att * (s_za * (1.0 + zatt * (1.0 - s_za)))).astype(BF16)
        dwc_ref[...] += lax.dot_general(mid["c_in_b"], dcb_b, (tn, ((), ())), preferred_element_type=F32)
        dc_in = lax.dot_general(dcb_b, wc, (nt, ((), ())), preferred_element_type=F32)
        s_zc, c2, s_c2 = mid["s_zconv"], mid["c2"], mid["s_c2"]
        dzconv_ref[...] = (dc_in * mid["c3"] * (s_zc * (1.0 + zconv * (1.0 - s_zc)))).astype(BF16)
        dc3 = dc_in * (zconv * s_zc)
        dc2 = dc3 * (s_c2 * (1.0 + c2 * (1.0 - s_c2)))
        dc1, dcg, dcbeta = _layer_norm_bwd(dc2, mid["chat"], mid["c_rstd"], cg_ref[...])
        dcg_ref[...] += dcg
        dcb_ref[...] += dcbeta
        dcbias_ref[...] += jnp.sum(dc1, axis=0, keepdims=True)
        dc1_ref[...] = dc1

    row = lambda width: pl.BlockSpec((rows_t, width), lambda i: (i, 0))
    full = lambda arr: _of_layer(arr, layer)
    whole = lambda r, c: pl.BlockSpec((r, c), lambda i: (0, 0))
    sds = jax.ShapeDtypeStruct
    out_specs = [row(half), row(half), row(d_model), row(d_model), row(half), row(half), row(d_model),
                 whole(half, d_model), whole(half, d_model), whole(d_model, d_model),
                 whole(1, half), whole(1, half), whole(1, half), whole(1, d_model), whole(1, d_model), whole(1, d_model)]
    out_shape = [sds((seq, half), BF16), sds((seq, half), BF16), sds((seq, d_model), BF16), sds((seq, d_model), BF16),
                 sds((seq, half), F32), sds((seq, half), F32), sds((seq, d_model), F32),
                 sds((half, d_model), F32), sds((half, d_model), F32), sds((d_model, d_model), F32),
                 sds((1, half), F32), sds((1, half), F32), sds((1, half), F32),
                 sds((1, d_model), F32), sds((1, d_model), F32), sds((1, d_model), F32)]
    return pl.pallas_call(
        body, name=name, grid=(seq // rows_t,),
        in_specs=_u_blocks(rows_t, d_model, half) + [row(half), row(half), row(d_model), row(d_model), full(w_att), full(w_conv),
                                                     full(w_out), full(cln_g), full(cln_b), full(b_proj), full(ln_g)],
        out_specs=out_specs, out_shape=out_shape,
        compiler_params=_cparams(("arbitrary",)),
    )(u, u, u, u, u, u, att, c1, x, dxn, w_att, w_conv, w_out, cln_g, cln_b, b_proj, ln_g)


def _loss_head(y, target, *, seq, d_model, name):
    rows_t = min(512, seq)

    def body(y_ref, t_ref, dy_ref, loss_ref):
        @pl.when(pl.program_id(0) == 0)
        def _():
            loss_ref[...] = jnp.zeros_like(loss_ref)

        err = y_ref[...] - t_ref[...]
        dy_ref[...] = err * (1.0 / d_model)
        per_token = jnp.sum(err * err, axis=-1, keepdims=True) * (1.0 / d_model)
        loss_ref[...] += 0.5 * jnp.sum(per_token, axis=0, keepdims=True)

    row = pl.BlockSpec((rows_t, d_model), lambda i: (i, 0))
    return pl.pallas_call(
        body, name=name, grid=(seq // rows_t,), in_specs=[row, row],
        out_specs=[row, pl.BlockSpec((1, 1), lambda i: (0, 0))],
        out_shape=[jax.ShapeDtypeStruct((seq, d_model), F32), jax.ShapeDtypeStruct((1, 1), F32)],
        compiler_params=_cparams(("arbitrary",)),
    )(y, target)


def _adamw(w, g, m, v, *, name):
    rows, cols = w.shape
    rows_t = rows
    for cand in (512, 256, 128, 64, 32, 16, 8):
        if rows % cand == 0 and cand * cols * 4 <= 2 * 1024 * 1024:
            rows_t = cand
            break

    def body(w_ref, g_ref, m_ref, v_ref, d_ref, nm_ref, nv_ref):
        grad = g_ref[...]
        new_m = ADAM_B1 * m_ref[...] + (1.0 - ADAM_B1) * grad
        new_v = ADAM_B2 * v_ref[...] + (1.0 - ADAM_B2) * (grad * grad)
        m_hat = new_m / (1.0 - ADAM_B1 ** ADAM_STEP)
        v_hat = new_v / (1.0 - ADAM_B2 ** ADAM_STEP)
        d_ref[...] = -ADAM_LR * (m_hat / (jnp.sqrt(v_hat) + ADAM_EPS) + ADAM_WD * w_ref[...])
        nm_ref[...] = new_m
        nv_ref[...] = new_v

    blk = pl.BlockSpec((rows_t, cols), lambda i: (i, 0))
    out = jax.ShapeDtypeStruct((rows, cols), F32)
    return pl.pallas_call(
        body, name=name, grid=(rows // rows_t,), in_specs=[blk] * 4, out_specs=[blk] * 3, out_shape=[out] * 3,
        compiler_params=_cparams(("parallel",)),
    )(w, g, m, v)


def _local_grads(x, target, w_in, b_in, conv_w, conv_b, cln_g, cln_b, w_att, w_conv, b_proj, w_out, ln_g, ln_b):
    seq, d_model = x.shape
    half = d_model // 2
    depth = w_in.shape[0]
    scan_suffix, scan_prefix = _scan_matrices()
    cols = half // LANES
    dims = dict(seq=seq, d_model=d_model)
    conv_dims = dict(seq=seq, d_conv=half, col_a=4 * cols, col_b=5 * cols)

    xs, xbs, us, atts, c1s = [x], [x.astype(BF16)], [], [], []
    for l in range(depth):
        u = _matmul(xbs[l], w_in, layer=l, mode="nn", bias=b_in[l].reshape(1, -1), name="in_proj", tm=1024, tn=512, tk=d_model)
        att = _attn_fwd(u, scan_suffix, seq=seq, d_att=half, name="attn_fwd")
        c1 = _conv_fwd(u, conv_w, conv_b, layer=l, name="conv_fwd", **conv_dims)
        xn, xnb = _mix_fwd(u, att, c1, xs[l], w_att, w_conv, w_out, cln_g, cln_b, b_proj, ln_g, ln_b, layer=l, name="mix_fwd", **dims)
        us.append(u)
        atts.append(att)
        c1s.append(c1)
        xs.append(xn)
        xbs.append(xnb)

    dx, loss = _loss_head(xs[depth], target, name="loss_head", **dims)
    grads = [None] * depth
    for l in reversed(range(depth)):
        u = us[l]
        (dzatt, dzconv, dgatt, dgconv, datt, dc1, dxres, dwa, dwc, dwo, dcg, dcb, dcbias, dbp, dlg, dlb) = _mix_bwd(
            u, atts[l], c1s[l], xs[l], dx, w_att, w_conv, w_out, cln_g, cln_b, b_proj, ln_g, layer=l, name="mix_bwd", **dims)
        dq, dk, dv = _attn_bwd(u, datt, scan_suffix, scan_prefix, seq=seq, d_att=half, name="attn_bwd")
        dglu_a, dglu_b, dconvw = _conv_bwd(u, dc1, conv_w, layer=l, name="conv_bwd", **conv_dims)
        du = jnp.concatenate([dq, dk, dv, dzatt, dglu_a, dglu_b, dzconv, dgatt, dgconv], axis=1)
        dwin, dbin = _matmul(xbs[l], du, mode="tn", colsum=True, name="in_proj_dw", tm=1024, tn=512, tk=seq)
        dx = _matmul(du, w_in, layer=l, mode="nt", add=dxres, name="in_proj_dx", tm=512, tn=1024, tk=du.shape[1])
        grads[l] = dict(w_in=dwin, b_in=dbin[0], conv_w=dconvw[:CONV_WIDTH], conv_b=dcbias[0], conv_ln_g=dcg[0], conv_ln_b=dcb[0],
                        w_att_proj=dwa, w_conv_proj=dwc, b_conv_proj=dbp[0], w_out=dwo, ln_g=dlg[0], ln_b=dlb[0])
    return loss, dx, grads


MESH = pl.DeviceIdType.MESH
IN_HBM = pl.BlockSpec(memory_space=pl.ANY)


def _position():
    x, y, c = lax.axis_index("x"), lax.axis_index("y"), lax.axis_index("c")
    return x, y, c, [(1 - x, y), (x, 1 - y), (1 - x, 1 - y)]


def _cut(ref, axis, start, size):
    idx = [slice(None)] * len(ref.shape)
    idx[axis] = pl.ds(start, size)
    return ref.at[tuple(idx)]


def _remote(src, dst, send_sem, recv_sem, device):
    return pltpu.make_async_remote_copy(src_ref=src, dst_ref=dst, send_sem=send_sem, recv_sem=recv_sem,
                                        device_id=device, device_id_type=MESH)


def _comm_call(body, name, operands, out_shape, scratch, alias=False):
    return pl.pallas_call(
        body, name=name, in_specs=[IN_HBM] * len(operands), out_specs=[IN_HBM] * len(out_shape), out_shape=out_shape,
        scratch_shapes=scratch, input_output_aliases={t: t for t in range(len(operands))} if alias else {},
    )(*operands)


def _place_block(shard, chip_axis, place, dtype, *, name):
    depth, rows, cols = shard.shape
    rows_t = _row_tile(rows, cols * 4 * 4, 16 * 1024 * 1024)
    steps = rows // rows_t
    shape = list(shard.shape)
    shape[chip_axis] *= N_CHIPS
    if chip_axis == 1:
        out_spec = pl.BlockSpec((None, rows_t, cols), lambda l, i, p: (l, p[1] * steps + i, 0))
    else:
        out_spec = pl.BlockSpec((None, rows_t, cols), lambda l, i, p: (l, i, p[1]))

    def body(place_ref, src_ref, out_ref):
        out_ref[...] = src_ref[...].astype(dtype)

    return pl.pallas_call(
        body, name=name, out_shape=jax.ShapeDtypeStruct(tuple(shape), dtype),
        grid_spec=pltpu.PrefetchScalarGridSpec(num_scalar_prefetch=1, grid=(depth, steps),
                                               in_specs=[pl.BlockSpec((None, rows_t, cols), lambda l, i, p: (l, i, 0))],
                                               out_specs=out_spec),
        compiler_params=_cparams(("arbitrary", "arbitrary")),
    )(place, shard)


def _gather_weights(wholes, chip_axes):
    n = len(wholes)
    depth = wholes[0].shape[0]
    hl = depth // 2

    def body(*refs):
        dst = refs[n:2 * n]
        ici_send, ici_recv, pair_send, pair_recv = refs[2 * n:]
        x, y, c, chips = _position()
        me = 2 * x + y

        def block(t, chip, lo, count):
            size = wholes[t].shape[chip_axes[t]] // N_CHIPS
            return _cut(_cut(dst[t], chip_axes[t], pl.multiple_of(chip * size, size), size), 0, lo, count)

        lo, other_lo = c * hl, (1 - c) * hl
        sends = []
        for t in range(n):
            for j, chip in enumerate(chips):
                cp = _remote(block(t, me, lo, hl), block(t, me, lo, hl), ici_send.at[t, j], ici_recv.at[t, j], (*chip, c))
                cp.start()
                sends.append(cp)
        for t in range(n):
            for j, chip in enumerate(chips):
                landed = block(t, 2 * chip[0] + chip[1], lo, hl)
                _remote(landed, landed, ici_send.at[t, j], ici_recv.at[t, j], (*chip, c)).wait_recv()
                cp = _remote(landed, landed, pair_send.at[t, j], pair_recv.at[t, j], (x, y, 1 - c))
                cp.start()
                sends.append(cp)
        for t in range(n):
            for j, chip in enumerate(chips):
                passed = block(t, 2 * chip[0] + chip[1], other_lo, hl)
                _remote(passed, passed, pair_send.at[t, j], pair_recv.at[t, j], (x, y, 1 - c)).wait_recv()
        for cp in sends:
            cp.wait_send()

    return _comm_call(body, "gather_weights", wholes, [jax.ShapeDtypeStruct(w.shape, w.dtype) for w in wholes],
                      [pltpu.SemaphoreType.DMA((n, 3))] * 4, alias=True)


def _pair_swap(parts, core_axes):
    n, depth = len(parts), len(parts[0])
    flat = [arr for per_layer in parts for arr in per_layer]
    halves = []
    for per_layer, axis in zip(parts, core_axes):
        shape = list(per_layer[0].shape)
        shape[axis] //= 2
        halves.append(jax.ShapeDtypeStruct((depth, *shape), per_layer[0].dtype))

    def body(*refs):
        src, dst = refs[:n * depth], refs[n * depth:n * depth + n]
        send_sem, recv_sem = refs[n * depth + n:]
        x, y, c, _ = _position()
        copies = []
        for t in range(n):
            size = halves[t].shape[1 + core_axes[t]]
            for l in range(depth):
                piece = _cut(src[t * depth + l], core_axes[t], pl.multiple_of((1 - c) * size, size), size)
                cp = _remote(piece, dst[t].at[l], send_sem.at[t, l], recv_sem.at[t, l], (x, y, 1 - c))
                cp.start()
                copies.append(cp)
        for cp in copies:
            cp.wait()

    return _comm_call(body, "pair_swap", flat, halves, [pltpu.SemaphoreType.DMA((n, depth))] * 2)


def _chip_exchange(partials, chip_axes):
    n = len(partials)
    quarters = []
    for arr, axis in zip(partials, chip_axes):
        shape = list(arr.shape)
        shape[axis] //= N_CHIPS
        quarters.append(jax.ShapeDtypeStruct((N_CHIPS - 1, *shape), arr.dtype))

    def body(*refs):
        src, dst = refs[:n], refs[n:2 * n]
        send_sem, recv_sem = refs[2 * n:]
        x, y, c, chips = _position()
        copies = []
        for t in range(n):
            size = quarters[t].shape[1 + chip_axes[t]]
            for j, chip in enumerate(chips):
                piece = _cut(src[t], chip_axes[t], pl.multiple_of((2 * chip[0] + chip[1]) * size, size), size)
                cp = _remote(piece, dst[t].at[j], send_sem.at[t, j], recv_sem.at[t, j], (*chip, c))
                cp.start()
                copies.append(cp)
        for cp in copies:
            cp.wait()

    return _comm_call(body, "chip_exchange", partials, quarters, [pltpu.SemaphoreType.DMA((n, N_CHIPS - 1))] * 2)


def _pair_join(blocks, core_axes):
    n = len(blocks)

    def body(*refs):
        dst = refs[n:2 * n]
        send_sem, recv_sem = refs[2 * n:]
        x, y, c, _ = _position()
        copies = []
        for t in range(n):
            size = blocks[t].shape[core_axes[t]] // 2
            mine = _cut(dst[t], core_axes[t], pl.multiple_of(c * size, size), size)
            cp = _remote(mine, mine, send_sem.at[t], recv_sem.at[t], (x, y, 1 - c))
            cp.start()
            copies.append(cp)
        for t, cp in enumerate(copies):
            size = blocks[t].shape[core_axes[t]] // 2
            other = _cut(dst[t], core_axes[t], pl.multiple_of((1 - c) * size, size), size)
            cp.wait_send()
            _remote(other, other, send_sem.at[t], recv_sem.at[t], (x, y, 1 - c)).wait_recv()

    return _comm_call(body, "pair_join", blocks, [jax.ShapeDtypeStruct(b.shape, b.dtype) for b in blocks],
                      [pltpu.SemaphoreType.DMA((n,))] * 2, alias=True)


def _gather_small(vec):
    n_dev = 2 * N_CHIPS

    def body(src, dst, send_sem, recv_sem, local_sem):
        x, y, c, _ = _position()
        flip = lambda v, bit: 1 - v if bit else v
        mine = pltpu.make_async_copy(src, dst.at[4 * x + 2 * y + c], local_sem)
        mine.start()
        copies = []
        for mask in range(1, n_dev):
            peer = (flip(x, mask & 4), flip(y, mask & 2), flip(c, mask & 1))
            cp = _remote(src, dst.at[4 * x + 2 * y + c], send_sem.at[mask - 1], recv_sem.at[mask - 1], peer)
            cp.start()
            copies.append((cp, peer))
        for mask, (cp, peer) in enumerate(copies, start=1):
            theirs = dst.at[4 * peer[0] + 2 * peer[1] + peer[2]]
            _remote(src, theirs, send_sem.at[mask - 1], recv_sem.at[mask - 1], peer).wait_recv()
        for cp, _ in copies:
            cp.wait_send()
        mine.wait()

    out = [jax.ShapeDtypeStruct((n_dev, *vec.shape), vec.dtype)]
    return _comm_call(body, "gather_small", [vec], out, [pltpu.SemaphoreType.DMA((n_dev - 1,))] * 2 + [pltpu.SemaphoreType.DMA(())])[0]


def _row_tile(rows, row_bytes, budget):
    tile = rows
    for cand in (512, 256, 128, 64, 32, 16, 8):
        if rows % cand == 0:
            tile = cand
            if cand * row_bytes <= budget:
                break
    return tile


def _pair_sum(parts, got, core_axis, place, *, name):
    depth, rows, cols = got.shape
    rows_t = _row_tile(rows, cols * 4 * depth * 6, 24 * 1024 * 1024)
    steps = rows // rows_t
    if core_axis == 0:
        part_spec = pl.BlockSpec((rows_t, cols), lambda i, p: (p[0] * steps + i, 0))
    else:
        part_spec = pl.BlockSpec((rows_t, cols), lambda i, p: (i, p[0]))
    stack_spec = pl.BlockSpec((depth, rows_t, cols), lambda i, p: (0, i, 0))

    def body(place_ref, *refs):
        got_ref, out_ref = refs[depth], refs[depth + 1]
        for l in range(depth):
            out_ref[l] = (refs[l][...] + got_ref[l]).astype(BF16)

    return pl.pallas_call(
        body, name=name, out_shape=jax.ShapeDtypeStruct(got.shape, BF16),
        grid_spec=pltpu.PrefetchScalarGridSpec(num_scalar_prefetch=1, grid=(steps,), in_specs=[part_spec] * depth + [stack_spec],
                                               out_specs=stack_spec),
        compiler_params=_cparams(("arbitrary",)),
    )(place, *parts, got)


def _chip_sum(partial, got, chip_axis, core_axis, place, *, name):
    _, depth, rows, cols = got.shape
    rows_t = _row_tile(rows, cols * 4 * 10, 24 * 1024 * 1024)
    steps = rows // rows_t
    if chip_axis == 1:
        own_spec = pl.BlockSpec((None, rows_t, cols), lambda l, i, p: (l, p[1] * steps + i, 0))
    else:
        own_spec = pl.BlockSpec((None, rows_t, cols), lambda l, i, p: (l, i, p[1]))
    got_spec = pl.BlockSpec((N_CHIPS - 1, None, rows_t, cols), lambda l, i, p: (0, l, i, 0))
    shape = [depth, rows, cols]
    shape[core_axis] *= 2
    if core_axis == 1:
        out_spec = pl.BlockSpec((None, rows_t, cols), lambda l, i, p: (l, p[0] * steps + i, 0))
    else:
        out_spec = pl.BlockSpec((None, rows_t, cols), lambda l, i, p: (l, i, p[0]))

    def body(place_ref, own_ref, got_ref, out_ref):
        up = lambda val: val.astype(F32)
        out_ref[...] = ((up(own_ref[...]) + up(got_ref[0])) + up(got_ref[1])) + up(got_ref[2])

    return pl.pallas_call(
        body, name=name, out_shape=jax.ShapeDtypeStruct(tuple(shape), F32),
        grid_spec=pltpu.PrefetchScalarGridSpec(num_scalar_prefetch=1, grid=(depth, steps), in_specs=[own_spec, got_spec],
                                               out_specs=out_spec),
        compiler_params=_cparams(("arbitrary", "arbitrary")),
    )(place, partial, got)


def _sum_devices(stack, *, name):
    def body(src_ref, out_ref):
        total = src_ref[0]
        for d in range(1, stack.shape[0]):
            total = total + src_ref[d]
        out_ref[...] = total

    return pl.pallas_call(body, name=name, out_shape=jax.ShapeDtypeStruct(stack.shape[1:], F32))(stack)


def kernel(x, w_in, b_in, conv_w, conv_b, conv_ln_g, conv_ln_b, w_att_proj, w_conv_proj, b_conv_proj, w_out, ln_g, ln_b, loss_target, m_w_in, m_b_in, m_conv_w, m_conv_b, m_conv_ln_g, m_conv_ln_b, m_w_att_proj, m_w_conv_proj, m_b_conv_proj, m_w_out, m_ln_g, m_ln_b, v_w_in, v_b_in, v_conv_w, v_conv_b, v_conv_ln_g, v_conv_ln_b, v_w_att_proj, v_w_conv_proj, v_b_conv_proj, v_w_out, v_ln_g, v_ln_b):
    depth = w_in.shape[0]
    d_model = x.shape[-1]
    half = d_model // 2
    chip = 2 * lax.axis_index("x") + lax.axis_index("y")
    place = jnp.stack([lax.axis_index("c"), chip]).astype(jnp.int32)
    vec3 = lambda v: v.reshape(depth, 1, -1)

    taps = jnp.pad(conv_w, ((0, 0), (0, CONV_PAD - CONV_WIDTH), (0, 0)))
    gathered = [("w_in", w_in, 2, BF16), ("w_att_proj", w_att_proj, 2, BF16), ("w_conv_proj", w_conv_proj, 2, BF16),
                ("w_out", w_out, 1, BF16), ("conv_w", taps, 2, F32)]
    wholes = [_place_block(arr, axis, place, dtype, name="place_" + n) for n, arr, axis, dtype in gathered]
    win, watt, wconv, wout, convw = _gather_weights(wholes, [axis for _, _, axis, _ in gathered])
    loss, grad_x, grads = _local_grads(x[0], loss_target[0], win, b_in, convw, vec3(conv_b), vec3(conv_ln_g), vec3(conv_ln_b),
                                       watt, wconv, vec3(b_conv_proj), wout, vec3(ln_g), vec3(ln_b))
    loss = lax.psum(loss[0, 0], ("x", "y", "c"))

    big = ["w_in", "w_att_proj", "w_conv_proj", "w_out"]
    core_axes, chip_axes = [0, 0, 0, 1], [1, 1, 1, 0]
    parts = [[grads[l][n] for l in range(depth)] for n in big]
    got = _pair_swap(parts, core_axes)
    partials = [_pair_sum(parts[t], got[t], core_axes[t], place, name="pair_sum_" + big[t]) for t in range(len(big))]
    got = _chip_exchange(partials, [a + 1 for a in chip_axes])
    finals = [_chip_sum(partials[t], got[t], chip_axes[t] + 1, core_axes[t] + 1, place, name="chip_sum_" + big[t])
              for t in range(len(big))]
    reduced = dict(zip(big, _pair_join(finals, [a + 1 for a in core_axes])))

    small = ["b_in", "conv_b", "conv_ln_g", "conv_ln_b", "b_conv_proj", "ln_g", "ln_b"]
    packed = jnp.stack([jnp.concatenate([grads[l][n] for n in small] + [grads[l]["conv_w"].reshape(-1)]) for l in range(depth)])
    total = _sum_devices(_gather_small(packed), name="sum_devices")
    offset = 0
    for n in small:
        width = grads[0][n].shape[0]
        reduced[n] = total[:, offset:offset + width]
        offset += width
    taps = total[:, offset:].reshape(depth, CONV_WIDTH, half)
    reduced["conv_w"] = lax.dynamic_slice_in_dim(taps, chip * conv_w.shape[2], conv_w.shape[2], axis=2)

    names = ["w_in", "b_in", "conv_w", "conv_b", "conv_ln_g", "conv_ln_b", "w_att_proj", "w_conv_proj", "b_conv_proj", "w_out", "ln_g", "ln_b"]
    weights = dict(zip(names, (w_in, b_in, conv_w, conv_b, conv_ln_g, conv_ln_b, w_att_proj, w_conv_proj, b_conv_proj, w_out, ln_g, ln_b)))
    first = dict(zip(names, (m_w_in, m_b_in, m_conv_w, m_conv_b, m_conv_ln_g, m_conv_ln_b, m_w_att_proj, m_w_conv_proj, m_b_conv_proj, m_w_out, m_ln_g, m_ln_b)))
    second = dict(zip(names, (v_w_in, v_b_in, v_conv_w, v_conv_b, v_conv_ln_g, v_conv_ln_b, v_w_att_proj, v_w_conv_proj, v_b_conv_proj, v_w_out, v_ln_g, v_ln_b)))
    delta, new_m, new_v = {}, {}, {}
    for n in names:
        shape = weights[n].shape
        flat = lambda arr: arr.reshape(-1, shape[-1])
        d, m, v = _adamw(flat(weights[n]), flat(reduced[n]), flat(first[n]), flat(second[n]), name="adamw_" + n)
        delta[n], new_m[n], new_v[n] = d.reshape(shape), m.reshape(shape), v.reshape(shape)
    return (loss, grad_x[None], *[reduced[n].reshape(weights[n].shape) for n in names], *[delta[n] for n in names],
            *[new_m[n] for n in names], *[new_v[n] for n in names])
```

```python
import functools
from typing import Callable, NamedTuple

import jax
import jax.numpy as jnp
from jax import lax
from jax.experimental import pallas as pl
from jax.experimental.pallas import tpu as pltpu

F32 = jnp.float32
BF16 = jnp.bfloat16

HEAD_DIM = 64
LANES = 128
CONV_WIDTH = 31
CONV_PAD = 32
SUBLANES = 8
LN_EPS = 1e-5
DEPTH = 4
DEEPNORM_ALPHA = (2 * DEPTH) ** 0.25
ATT_SCALE = HEAD_DIM ** -0.5
ATT_TILE = 128
ATT_DEAD = -104.0
ATT_GROUP = 2
ATT_FILL = -1e30

ADAM_LR = 0.001
ADAM_B1 = 0.9
ADAM_B2 = 0.999
ADAM_EPS = 1e-08
ADAM_WD = 0.01
ADAM_STEP = 10

VMEM_LIMIT = 56 * 1024 * 1024

N_CHIPS = 4


def _cparams(sem):
    return pltpu.CompilerParams(dimension_semantics=sem, vmem_limit_bytes=VMEM_LIMIT)


def _sigmoid(x):
    return 1.0 / (1.0 + jnp.exp(-x))


class _Rider(NamedTuple):
    operands: list
    out_shape: list
    aliases: dict
    scratch: list
    start: Callable
    finish: Callable


def _host_call(body, rider, *, name, grid, in_specs, out_specs, out_shape, scratch, operands, semantics):
    n_in, n_out = len(in_specs), len(out_specs)
    aliases = {}
    if rider is not None:
        r_in, r_out = len(rider.operands), len(rider.out_shape)
        host_body = body

        def body(*refs):
            base = n_in + r_in
            ins, rins = refs[:n_in], refs[n_in:base]
            outs, routs = refs[base:base + n_out], refs[base + n_out:base + n_out + r_out]
            rest = refs[base + n_out + r_out:]
            split = len(rest) - len(rider.scratch)
            ids = [pl.program_id(axis) for axis in range(len(grid))]
            first = functools.reduce(jnp.logical_and, [i == 0 for i in ids])
            last = functools.reduce(jnp.logical_and, [i == g - 1 for i, g in zip(ids, grid)])

            @pl.when(first)
            def _():
                rider.start(rins, routs, rest[split:])

            host_body(*ins, *outs, *rest[:split])

            @pl.when(last)
            def _():
                rider.finish(rins, routs, rest[split:])

        hbm = pl.BlockSpec(memory_space=pl.ANY)
        in_specs = list(in_specs) + [hbm] * r_in
        out_specs = list(out_specs) + [hbm] * r_out
        out_shape = list(out_shape) + list(rider.out_shape)
        scratch = list(scratch) + list(rider.scratch)
        operands = list(operands) + list(rider.operands)
        aliases = {n_in + i: n_out + o for i, o in rider.aliases.items()}
    res = pl.pallas_call(
        body, name=name, grid=grid, in_specs=list(in_specs), out_specs=list(out_specs), out_shape=list(out_shape),
        scratch_shapes=list(scratch), input_output_aliases=aliases, compiler_params=_cparams(semantics),
    )(*operands)
    return list(res[:n_out]), list(res[n_out:])


def _lone_call(rider, *, name):
    r_in = len(rider.operands)

    def body(*refs):
        ins, outs, sems = refs[:r_in], refs[r_in:r_in + len(rider.out_shape)], refs[r_in + len(rider.out_shape):]
        rider.start(ins, outs, sems)
        rider.finish(ins, outs, sems)

    hbm = pl.BlockSpec(memory_space=pl.ANY)
    return list(pl.pallas_call(
        body, name=name, in_specs=[hbm] * r_in, out_specs=[hbm] * len(rider.out_shape), out_shape=list(rider.out_shape),
        scratch_shapes=list(rider.scratch), input_output_aliases=dict(rider.aliases),
    )(*rider.operands))


def _fit(tile, dim):
    assert dim % LANES == 0
    tile = min(tile, dim) // LANES * LANES
    while dim % tile:
        tile -= LANES
    return tile


_DIMS = {"nn": ((1,), (0,)), "nt": ((1,), (1,)), "tn": ((0,), (0,))}


def _matmul(a, b, *, mode, name, layer=None, bias=None, add=None, colsum=False, out_dtype=F32, tm=1024, tn=512, tk=1024,
            rider=None):
    b_shape = b.shape if layer is None else b.shape[1:]
    if mode == "nn":
        (m, k), (k2, n) = a.shape, b_shape
    elif mode == "nt":
        (m, k), (n, k2) = a.shape, b_shape
    else:
        (k, m), (k2, n) = a.shape, b_shape
    assert k == k2
    tm, tn, tk = _fit(tm, m), _fit(tn, n), _fit(tk, k)
    gm, gn, nk = m // tm, n // tn, k // tk

    a_spec = pl.BlockSpec((tk, tm), lambda i, j, kk: (kk, i)) if mode == "tn" else pl.BlockSpec((tm, tk), lambda i, j, kk: (i, kk))
    if layer is None:
        b_spec = pl.BlockSpec((tn, tk), lambda i, j, kk: (j, kk)) if mode == "nt" else pl.BlockSpec((tk, tn), lambda i, j, kk: (kk, j))
    elif mode == "nt":
        b_spec = pl.BlockSpec((None, tn, tk), lambda i, j, kk: (layer, j, kk))
    else:
        b_spec = pl.BlockSpec((None, tk, tn), lambda i, j, kk: (layer, kk, j))
    in_specs, operands = [a_spec, b_spec], [a, b]
    if bias is not None:
        in_specs.append(pl.BlockSpec((1, tn), lambda i, j, kk: (0, j)))
        operands.append(bias)
    if add is not None:
        in_specs.append(pl.BlockSpec((tm, tn), lambda i, j, kk: (i, j)))
        operands.append(add)
    out_shape = [jax.ShapeDtypeStruct((m, n), out_dtype)]
    out_specs = [pl.BlockSpec((tm, tn), lambda i, j, kk: (i, j))]
    scratch = [pltpu.VMEM((tm, tn), F32)] if nk > 1 else []
    if colsum:
        assert mode == "tn"
        out_shape.append(jax.ShapeDtypeStruct((gm, 1, n), F32))
        out_specs.append(pl.BlockSpec((1, 1, tn), lambda i, j, kk: (i, 0, j)))
        if nk > 1:
            scratch.append(pltpu.VMEM((1, tn), F32))
    has_bias, has_add = bias is not None, add is not None

    def body(*refs):
        refs = list(refs)
        a_ref, b_ref = refs[0], refs[1]
        pos = 2
        bias_ref = add_ref = None
        if has_bias:
            bias_ref = refs[pos]
            pos += 1
        if has_add:
            add_ref = refs[pos]
            pos += 1
        o_ref = refs[pos]
        pos += 1
        cs_ref = None
        if colsum:
            cs_ref = refs[pos]
            pos += 1

        def finish(out, sums):
            if has_bias:
                out = out + bias_ref[...]
            if has_add:
                out = out + add_ref[...]
            o_ref[...] = out.astype(out_dtype)
            if colsum:
                cs_ref[0] = sums

        bv = b_ref[...]
        prod = lax.dot_general(a_ref[...].astype(BF16), bv.astype(BF16), (_DIMS[mode], ((), ())), preferred_element_type=F32)
        sums = jnp.sum(bv.astype(F32), axis=0, keepdims=True) if colsum else None
        if nk == 1:
            finish(prod, sums)
            return
        acc_ref = refs[pos]
        cs_acc = refs[pos + 1] if colsum else None
        kk = pl.program_id(2)

        @pl.when(kk == 0)
        def _():
            acc_ref[...] = jnp.zeros_like(acc_ref)
            if colsum:
                cs_acc[...] = jnp.zeros_like(cs_acc)

        acc_ref[...] += prod
        if colsum:
            cs_acc[...] += sums

        @pl.when(kk == nk - 1)
        def _():
            finish(acc_ref[...], cs_acc[...] if colsum else None)

    res, rode = _host_call(body, rider, name=name, grid=(gm, gn, nk), in_specs=in_specs, out_specs=out_specs, out_shape=out_shape,
                           scratch=scratch, operands=operands, semantics=("arbitrary", "arbitrary", "arbitrary"))
    out = (res[0], res[1][0]) if colsum else res[0]
    return out if rider is None else (out, rode)


def _scan_matrices():
    t = ATT_TILE
    r = lax.broadcasted_iota(jnp.int32, (t, t), 0)
    c = lax.broadcasted_iota(jnp.int32, (t, t), 1)
    ones = jnp.ones((t, t), F32)
    suffix = jnp.concatenate([(r > c).astype(F32), ones], axis=1)
    prefix = jnp.concatenate([(r < c).astype(F32), ones], axis=1)
    stack = lambda mat: jnp.concatenate([mat, mat], axis=0).astype(BF16)
    return stack(suffix), stack(prefix)


def _split_halves(val):
    hi = val.astype(BF16)
    lo = (val - hi.astype(F32)).astype(BF16)
    return jnp.concatenate([hi, lo], axis=1)


def _split_scan(val, mat_ref):
    return jnp.dot(_split_halves(val), mat_ref[...], preferred_element_type=F32)


def _att_tile_scores(q, k, masked):
    t = ATT_TILE
    z = lax.dot_general(q, k, (((1,), (1,)), ((), ())), preferred_element_type=F32)
    sp = jnp.log(1.0 + jnp.exp(-jnp.abs(z)))
    f = jnp.minimum(-z, 0.0) - sp
    a = f + z
    causal = None
    if masked:
        causal = lax.broadcasted_iota(jnp.int32, (t, t), 1) < lax.broadcasted_iota(jnp.int32, (t, t), 0)
        f = jnp.where(causal, f, 0.0)
    return f, a, causal


def _any_alive(runs):
    top = functools.reduce(jnp.maximum, [run for per_head in runs for run in per_head])
    return (jnp.max(top) > ATT_DEAD).astype(jnp.int32)


def _head_copies(seq, src_ref, scale, lo_ref, hi_ref, plain_ref):
    chunk = min(256, seq)
    low = lax.broadcasted_iota(jnp.int32, (chunk, LANES), 1) < HEAD_DIM

    def step(r, carry):
        rows = pl.ds(pl.multiple_of(r * chunk, chunk), chunk)
        val = src_ref[rows, :]
        if scale != 1.0:
            val = val * scale
        if lo_ref is not None:
            lo_ref[rows, :] = jnp.where(low, val, 0.0).astype(BF16)
            hi_ref[rows, :] = jnp.where(low, 0.0, val).astype(BF16)
        if plain_ref is not None:
            plain_ref[rows, :] = val.astype(BF16)
        return carry

    lax.fori_loop(0, seq // chunk, step, 0)


def _attn_fwd(u, scan_suffix, *, seq, d_att, name, rider=None):
    t = ATT_TILE
    nq = seq // t
    pairs = d_att // LANES
    grp = ATT_GROUP
    assert nq % grp == 0

    def body(q_ref, k_ref, v_ref, um_ref, o_ref, q0, q1, kk, v0, v1, f2_s, a_s, lg_s, tot_s, run_s, acc_s):
        _head_copies(seq, q_ref, ATT_SCALE, q0, q1, None)
        _head_copies(seq, k_ref, 1.0, None, None, kk)
        _head_copies(seq, v_ref, 1.0, v0, v1, None)

        heads = ((q0, v0), (q1, v1))

        def group(gi, carry):
            qb0 = gi * grp
            qrows = [pl.ds(pl.multiple_of((qb0 + g) * t, t), t) for g in range(grp)]
            qv = [[qs[qrows[g], :] for g in range(grp)] for qs, _ in heads]

            chains = [(h, g) for g in range(grp) for h in range(2)]

            def key_rows(g, i):
                return pl.ds(pl.multiple_of(jnp.maximum(qb0 + g - i, 0) * t, t), t)

            def stage1(i, masked):
                out = []
                for h, g in chains:
                    f, a, _ = _att_tile_scores(qv[h][g], kk[key_rows(g, i), :], masked)
                    out.append((_split_halves(f), a))
                return out

            def stage2(halves, a, masked):
                scan = jnp.dot(halves, um_ref[...], preferred_element_type=F32)
                logit = a + scan[:, :t]
                if masked:
                    causal = lax.broadcasted_iota(jnp.int32, (t, t), 1) < lax.broadcasted_iota(jnp.int32, (t, t), 0)
                    logit = jnp.where(causal, logit, ATT_FILL)
                return logit, scan[:, t:]

            def put(halves_a=None, logit_total=None):
                for c in range(len(chains)):
                    if halves_a is not None:
                        f2_s[c], a_s[c] = halves_a[c]
                    if logit_total is not None:
                        lg_s[c], tot_s[c] = logit_total[c]

            first = stage1(0, True)
            put(halves_a=stage1(1, False), logit_total=[stage2(f2, a, True) for f2, a in first])
            for c in range(len(chains)):
                run_s[c] = jnp.zeros((t, t), F32)
            for g in range(grp):
                acc_s[g] = jnp.zeros((t, LANES), F32)

            def wbody(st):
                i = st[0]
                held = [(f2_s[c], a_s[c]) for c in range(len(chains))]
                logits = [lg_s[c] for c in range(len(chains))]
                totals = [tot_s[c] for c in range(len(chains))]
                runs = [run_s[c] for c in range(len(chains))]
                accs = [acc_s[g] for g in range(grp)]
                for c, (h, g) in enumerate(chains):
                    run = runs[c] + jnp.where(qb0 + g - i >= 0, 0.0, ATT_FILL)
                    w = jnp.exp(logits[c] + run)
                    accs[g] = accs[g] + jnp.dot(w.astype(BF16), heads[h][1][key_rows(g, i), :], preferred_element_type=F32)
                    runs[c] = run + totals[c]
                ahead2 = [stage2(f2, a, False) for f2, a in held]
                ahead1 = stage1(i + 2, False)
                put(halves_a=ahead1, logit_total=ahead2)
                for c in range(len(chains)):
                    run_s[c] = runs[c]
                for g in range(grp):
                    acc_s[g] = accs[g]
                more = jnp.logical_and(i + 1 <= qb0 + grp - 1, _any_alive([runs]) > 0)
                return i + 1, more.astype(jnp.int32)

            lax.while_loop(lambda st: st[1] > 0, wbody, (jnp.int32(0), jnp.int32(1)))
            for g in range(grp):
                o_ref[qrows[g], :] = acc_s[g]
            return carry

        lax.fori_loop(0, nq // grp, group, 0)

    blk = lambda base: pl.BlockSpec((seq, LANES), lambda j, base=base: (0, base + j))
    res, rode = _host_call(
        body, rider, name=name, grid=(pairs,),
        in_specs=[blk(0), blk(pairs), blk(2 * pairs), pl.BlockSpec((2 * t, 2 * t), lambda j: (0, 0))],
        out_specs=[pl.BlockSpec((seq, LANES), lambda j: (0, j))],
        out_shape=[jax.ShapeDtypeStruct((seq, d_att), F32)],
        scratch=[pltpu.VMEM((seq, LANES), BF16)] * 5 + [pltpu.VMEM((2 * grp, t, 2 * t), BF16)]
        + [pltpu.VMEM((2 * grp, t, t), F32)] * 4 + [pltpu.VMEM((grp, t, LANES), F32)],
        operands=[u, u, u, scan_suffix], semantics=("arbitrary",))
    return res[0] if rider is None else (res[0], rode)


def _attn_bwd(u, d_att_out, scan_suffix, scan_prefix, *, seq, d_att, name, rider=None):
    t = ATT_TILE
    nq = seq // t
    pairs = d_att // LANES
    grp = ATT_GROUP
    assert nq % grp == 0

    def body(q_ref, k_ref, v_ref, do_ref, um_ref, pm_ref, dq_ref, dk_ref, dv_ref,
             q0, q1, kk, k0, k1, vv, do0, do1, dq_acc, dk_acc, dv_acc, g_st, b_st,
             f2_s, a_s, lg_s, tot_s, dw_s, run_s, p_s, pt_s, grun_s, dq_s):
        _head_copies(seq, q_ref, ATT_SCALE, q0, q1, None)
        _head_copies(seq, k_ref, 1.0, k0, k1, kk)
        _head_copies(seq, v_ref, 1.0, None, None, vv)
        _head_copies(seq, do_ref, 1.0, do0, do1, None)
        dk_acc[...] = jnp.zeros_like(dk_acc)
        dv_acc[...] = jnp.zeros_like(dv_acc)

        heads = ((q0, k0, do0), (q1, k1, do1))
        tn = (((0,), (0,)), ((), ()))
        nt = (((1,), (1,)), ((), ()))

        def group(gi, carry):
            qb0 = gi * grp
            qrows = [pl.ds(pl.multiple_of((qb0 + g) * t, t), t) for g in range(grp)]
            qv = [[qs[qrows[g], :] for g in range(grp)] for qs, _, _ in heads]
            dov = [[dos[qrows[g], :] for g in range(grp)] for _, _, dos in heads]

            def key_rows(g, i):
                return pl.ds(pl.multiple_of(jnp.maximum(qb0 + g - i, 0) * t, t), t)

            chains = [(h, g) for g in range(grp) for h in range(2)]
            every = range(len(chains))

            def stage1(i, masked):
                out = []
                for h, g in chains:
                    f, a, _ = _att_tile_scores(qv[h][g], kk[key_rows(g, i), :], masked)
                    out.append((_split_halves(f), a))
                return out

            def stage2(i, c, halves, a, masked):
                h, g = chains[c]
                scan = jnp.dot(halves, um_ref[...], preferred_element_type=F32)
                logit = a + scan[:, :t]
                if masked:
                    causal = lax.broadcasted_iota(jnp.int32, (t, t), 1) < lax.broadcasted_iota(jnp.int32, (t, t), 0)
                    logit = jnp.where(causal, logit, ATT_FILL)
                b_st[c, i] = jnp.exp(a)
                dw = lax.dot_general(dov[h][g], vv[key_rows(g, i), :], nt, preferred_element_type=F32)
                return logit, scan[:, t:], dw

            def put(held=None, ready=None):
                for c in every:
                    if held is not None:
                        f2_s[c], a_s[c] = held[c]
                    if ready is not None:
                        lg_s[c], tot_s[c], dw_s[c] = ready[c]

            head0 = stage1(0, True)
            put(held=stage1(1, False), ready=[stage2(0, c, *head0[c], True) for c in every])
            for c in every:
                run_s[c] = jnp.zeros((t, t), F32)

            def wbody(st):
                i = st[0]
                held = [(f2_s[c], a_s[c]) for c in every]
                ready = [(lg_s[c], tot_s[c], dw_s[c]) for c in every]
                runs = [run_s[c] for c in every]
                dvs = [jnp.zeros((t, LANES), F32)] * grp
                for c, (h, g) in enumerate(chains):
                    logit, total, dw = ready[c]
                    run = runs[c] + jnp.where(qb0 + g - i >= 0, 0.0, ATT_FILL)
                    w = jnp.exp(logit + run)
                    g_st[c, i] = w * dw
                    dvs[g] = dvs[g] + lax.dot_general(w.astype(BF16), dov[h][g], tn, preferred_element_type=F32)
                    runs[c] = run + total
                for g in range(grp):
                    dv_acc[key_rows(g, i), :] += dvs[g]
                ahead2 = [stage2(i + 1, c, *held[c], False) for c in every]
                ahead1 = stage1(i + 2, False)
                put(held=ahead1, ready=ahead2)
                for c in every:
                    run_s[c] = runs[c]
                more = jnp.logical_and(i + 1 <= qb0 + grp - 1, _any_alive([runs]) > 0)
                return i + 1, more.astype(jnp.int32)

            steps = lax.while_loop(lambda st: st[1] > 0, wbody, (jnp.int32(0), jnp.int32(1)))[0]

            def prefix(i):
                out = []
                for c in every:
                    scan = _split_scan(g_st[c, i], pm_ref)
                    out.append((scan[:, :t], scan[:, t:]))
                return out

            def back(i, masked):
                sums = [(p_s[c], pt_s[c]) for c in every]
                gruns = [grun_s[c] for c in every]
                dqs = [dq_s[g] for g in range(grp)]
                dks = [jnp.zeros((t, LANES), F32)] * grp
                for c, (h, g) in enumerate(chains):
                    gt = g_st[c, i]
                    dz = gt - b_st[c, i] * (gt + gruns[c] + sums[c][0])
                    if masked:
                        causal = lax.broadcasted_iota(jnp.int32, (t, t), 1) < lax.broadcasted_iota(jnp.int32, (t, t), 0)
                        dz = jnp.where(causal, dz, 0.0)
                    dz = dz.astype(BF16)
                    dqs[g] = dqs[g] + jnp.dot(dz, heads[h][1][key_rows(g, i), :], preferred_element_type=F32)
                    dks[g] = dks[g] + lax.dot_general(dz, qv[h][g], tn, preferred_element_type=F32)
                    gruns[c] = gruns[c] + sums[c][1]
                for g in range(grp):
                    dk_acc[key_rows(g, i), :] += dks[g]
                return gruns, dqs

            def keep(sums=None, gruns=None, dqs=None):
                for c in every:
                    if sums is not None:
                        p_s[c], pt_s[c] = sums[c]
                    if gruns is not None:
                        grun_s[c] = gruns[c]
                if dqs is not None:
                    for g in range(grp):
                        dq_s[g] = dqs[g]

            keep(sums=prefix(steps - 1), gruns=[jnp.zeros((t, t), F32)] * len(chains), dqs=[jnp.zeros((t, LANES), F32)] * grp)

            def bbody(j, carry2):
                i = steps - 1 - j
                gruns, dqs = back(i, False)
                keep(sums=prefix(i - 1), gruns=gruns, dqs=dqs)
                return carry2

            lax.fori_loop(0, steps - 1, bbody, 0)
            _, dqs = back(0, True)
            for g in range(grp):
                dq_acc[qrows[g], :] = dqs[g]
            return carry

        lax.fori_loop(0, nq // grp, group, 0)

        chunk = min(256, seq)

        def emit(r, carry):
            rows = pl.ds(pl.multiple_of(r * chunk, chunk), chunk)
            dq_ref[rows, :] = (dq_acc[rows, :] * ATT_SCALE).astype(BF16)
            dk_ref[rows, :] = dk_acc[rows, :].astype(BF16)
            dv_ref[rows, :] = dv_acc[rows, :].astype(BF16)
            return carry

        lax.fori_loop(0, seq // chunk, emit, 0)

    blk = lambda base: pl.BlockSpec((seq, LANES), lambda j, base=base: (0, base + j), pipeline_mode=pl.Buffered(1))
    mat = pl.BlockSpec((2 * t, 2 * t), lambda j: (0, 0))
    out = jax.ShapeDtypeStruct((seq, d_att), BF16)
    res, rode = _host_call(
        body, rider, name=name, grid=(pairs,),
        in_specs=[blk(0), blk(pairs), blk(2 * pairs), blk(0), mat, mat],
        out_specs=[pl.BlockSpec((seq, LANES), lambda j: (0, j))] * 3,
        out_shape=[out, out, out],
        scratch=[pltpu.VMEM((seq, LANES), BF16)] * 8 + [pltpu.VMEM((seq, LANES), F32)] * 3
        + [pltpu.VMEM((2 * grp, nq + 1, t, t), F32)] * 2 + [pltpu.VMEM((2 * grp, t, 2 * t), BF16)]
        + [pltpu.VMEM((2 * grp, t, t), F32)] * 8 + [pltpu.VMEM((grp, t, LANES), F32)],
        operands=[u, u, u, d_att_out, scan_suffix, scan_prefix], semantics=("arbitrary",))
    return res if rider is None else (res, rode)


CONV_ROWS = 256


def _shifted(window, residue, rows):
    total = rows + CONV_PAD
    return window if residue == 0 else pltpu.roll(window, total - residue, 0)


def _glu_to_pad(seq, a_ref, b_ref, pad_ref):
    chunk = min(CONV_ROWS, seq)
    pad_ref[pl.ds(0, CONV_PAD), :] = jnp.zeros((CONV_PAD, LANES), F32)

    def step(r, carry):
        rows = pl.ds(pl.multiple_of(r * chunk, chunk), chunk)
        pad_ref[pl.ds(pl.multiple_of(r * chunk + CONV_PAD, SUBLANES), chunk), :] = a_ref[rows, :] * _sigmoid(b_ref[rows, :])
        return carry

    lax.fori_loop(0, seq // chunk, step, 0)


def _conv_fwd(u, conv_w, conv_b, *, layer, seq, d_conv, col_a, col_b, name, rider=None):
    blocks = d_conv // LANES
    rows_t = min(CONV_ROWS, seq)
    shift0 = CONV_PAD - (CONV_WIDTH - 1)

    def body(a_ref, b_ref, w_ref, bias_ref, o_ref, pad_ref):
        _glu_to_pad(seq, a_ref, b_ref, pad_ref)

        def step(r, carry):
            base = pl.multiple_of(r * rows_t, rows_t)
            window = pad_ref[pl.ds(base, rows_t + CONV_PAD), :]
            acc = jnp.zeros((rows_t, LANES), F32) + bias_ref[...]
            for residue in range(SUBLANES):
                moved = _shifted(window, residue, rows_t)
                for tap in range(CONV_WIDTH):
                    if (shift0 + tap) % SUBLANES == residue:
                        lo = (shift0 + tap) - residue
                        acc = acc + w_ref[tap:tap + 1, :] * moved[lo:lo + rows_t, :]
            o_ref[pl.ds(base, rows_t), :] = acc
            return carry

        lax.fori_loop(0, seq // rows_t, step, 0)

    res, rode = _host_call(
        body, rider, name=name, grid=(blocks,),
        in_specs=[pl.BlockSpec((seq, LANES), lambda j: (0, col_a + j)), pl.BlockSpec((seq, LANES), lambda j: (0, col_b + j)),
                  pl.BlockSpec((None, CONV_PAD, LANES), lambda j: (layer, 0, j)),
                  pl.BlockSpec((None, 1, LANES), lambda j: (layer, 0, j))],
        out_specs=[pl.BlockSpec((seq, LANES), lambda j: (0, j))],
        out_shape=[jax.ShapeDtypeStruct((seq, d_conv), F32)],
        scratch=[pltpu.VMEM((seq + CONV_PAD, LANES), F32)],
        operands=[u, u, conv_w, conv_b], semantics=("arbitrary",))
    return res[0] if rider is None else (res[0], rode)


def _conv_bwd(u, dc1, conv_w, *, layer, seq, d_conv, col_a, col_b, name):
    blocks = d_conv // LANES
    rows_t = min(CONV_ROWS, seq)
    shift0 = CONV_PAD - (CONV_WIDTH - 1)

    def body(a_ref, b_ref, d_ref, w_ref, da_ref, db_ref, dw_ref, pad_ref, dpad_ref, dw_acc):
        _glu_to_pad(seq, a_ref, b_ref, pad_ref)
        dpad_ref[pl.ds(seq, CONV_PAD), :] = jnp.zeros((CONV_PAD, LANES), F32)

        def fill(r, carry):
            rows = pl.ds(pl.multiple_of(r * rows_t, rows_t), rows_t)
            dpad_ref[rows, :] = d_ref[rows, :]
            return carry

        lax.fori_loop(0, seq // rows_t, fill, 0)
        dw_acc[...] = jnp.zeros_like(dw_acc)

        def step(r, carry):
            base = pl.multiple_of(r * rows_t, rows_t)
            rows = pl.ds(base, rows_t)
            window = dpad_ref[pl.ds(base, rows_t + CONV_PAD), :]
            acc = jnp.zeros((rows_t, LANES), F32)
            for residue in range(SUBLANES):
                moved = _shifted(window, residue, rows_t)
                for tap in range(CONV_WIDTH):
                    off = CONV_WIDTH - 1 - tap
                    if off % SUBLANES == residue:
                        lo = off - residue
                        acc = acc + w_ref[tap:tap + 1, :] * moved[lo:lo + rows_t, :]
            sig = _sigmoid(b_ref[rows, :])
            a = a_ref[rows, :]
            da_ref[rows, :] = (acc * sig).astype(BF16)
            db_ref[rows, :] = (acc * a * sig * (1.0 - sig)).astype(BF16)
            dcur = d_ref[rows, :]
            cwin = pad_ref[pl.ds(base, rows_t + CONV_PAD), :]
            for residue in range(SUBLANES):
                moved = _shifted(cwin, residue, rows_t)
                for tap in range(CONV_WIDTH):
                    if (shift0 + tap) % SUBLANES == residue:
                        lo = (shift0 + tap) - residue
                        prod = dcur * moved[lo:lo + rows_t, :]
                        dw_acc[tap] += jnp.sum(prod.reshape(rows_t // SUBLANES, SUBLANES, LANES), axis=0)
            return carry

        lax.fori_loop(0, seq // rows_t, step, 0)
        dw_ref[...] = jnp.sum(dw_acc[...], axis=1)

    col = lambda base: pl.BlockSpec((seq, LANES), lambda j, base=base: (0, base + j))
    own = pl.BlockSpec((seq, LANES), lambda j: (0, j))
    return pl.pallas_call(
        body, name=name, grid=(blocks,),
        in_specs=[col(col_a), col(col_b), own, pl.BlockSpec((None, CONV_PAD, LANES), lambda j: (layer, 0, j))],
        out_specs=[own, own, pl.BlockSpec((CONV_PAD, LANES), lambda j: (0, j))],
        out_shape=[jax.ShapeDtypeStruct((seq, d_conv), BF16), jax.ShapeDtypeStruct((seq, d_conv), BF16),
                   jax.ShapeDtypeStruct((CONV_PAD, d_conv), F32)],
        scratch_shapes=[pltpu.VMEM((seq + CONV_PAD, LANES), F32), pltpu.VMEM((seq + CONV_PAD, LANES), F32),
                        pltpu.VMEM((CONV_PAD, SUBLANES, LANES), F32)],
        compiler_params=_cparams(("parallel",)),
    )(u, u, dc1, conv_w)


MIX_ROWS = 256


def _layer_norm_stats(val):
    mu = jnp.mean(val, axis=-1, keepdims=True)
    cen = val - mu
    var = jnp.mean(cen * cen, axis=-1, keepdims=True)
    rstd = lax.rsqrt(var + LN_EPS)
    return cen * rstd, rstd


def _layer_norm_bwd(dy, xhat, rstd, gain):
    dxhat = dy * gain
    m1 = jnp.mean(dxhat, axis=-1, keepdims=True)
    m2 = jnp.mean(dxhat * xhat, axis=-1, keepdims=True)
    dx = rstd * (dxhat - m1 - xhat * m2)
    return dx, jnp.sum(dy * xhat, axis=0, keepdims=True), jnp.sum(dy, axis=0, keepdims=True)


def _mix_forward(zatt, att, c1, zconv, gatt, gconv, x, w_att, w_conv, w_out, cln_g, cln_b, b_proj):
    s_zatt = _sigmoid(zatt)
    a_in = att * (zatt * s_zatt)
    chat, c_rstd = _layer_norm_stats(c1)
    c2 = chat * cln_g + cln_b
    s_c2 = _sigmoid(c2)
    c3 = c2 * s_c2
    s_zconv = _sigmoid(zconv)
    c_in = c3 * (zconv * s_zconv)
    a_in_b, c_in_b = a_in.astype(BF16), c_in.astype(BF16)
    ab = jnp.dot(a_in_b, w_att, preferred_element_type=F32)
    cb = jnp.dot(c_in_b, w_conv, preferred_element_type=F32) + b_proj
    s_gatt, s_gconv = _sigmoid(gatt), _sigmoid(gconv)
    merged_b = (s_gatt * ab + s_gconv * cb).astype(BF16)
    y = jnp.dot(merged_b, w_out, preferred_element_type=F32)
    h = DEEPNORM_ALPHA * x + y
    return dict(s_zatt=s_zatt, a_in_b=a_in_b, chat=chat, c_rstd=c_rstd, c2=c2, s_c2=s_c2, c3=c3, s_zconv=s_zconv,
                c_in_b=c_in_b, ab=ab, cb=cb, s_gatt=s_gatt, s_gconv=s_gconv, merged_b=merged_b, h=h)


def _u_blocks(rows_t, width, half):
    return [pl.BlockSpec((rows_t, half), lambda i, c=c: (i, c)) for c in (3, 6, 7, 8, 9, 10)]


def _of_layer(arr, layer):
    return pl.BlockSpec((None,) + arr.shape[1:], lambda i: (layer, 0, 0))


def _mix_fwd(u, att, c1, x, w_att, w_conv, w_out, cln_g, cln_b, b_proj, ln_g, ln_b, *, layer, seq, d_model, name):
    half = d_model // 2
    rows_t = min(MIX_ROWS, seq)

    def body(zatt_ref, zconv_ref, ga0, ga1, gc0, gc1, att_ref, c1_ref, x_ref, wa_ref, wc_ref, wo_ref,
             cg_ref, cb_ref, bp_ref, lg_ref, lb_ref, o_ref, ob_ref):
        gatt = jnp.concatenate([ga0[...], ga1[...]], axis=1)
        gconv = jnp.concatenate([gc0[...], gc1[...]], axis=1)
        mid = _mix_forward(zatt_ref[...], att_ref[...], c1_ref[...], zconv_ref[...], gatt, gconv, x_ref[...],
                           wa_ref[...], wc_ref[...], wo_ref[...], cg_ref[...], cb_ref[...], bp_ref[...])
        xhat, _ = _layer_norm_stats(mid["h"])
        out = xhat * lg_ref[...] + lb_ref[...]
        o_ref[...] = out
        ob_ref[...] = out.astype(BF16)

    row = lambda width: pl.BlockSpec((rows_t, width), lambda i: (i, 0))
    full = lambda arr: _of_layer(arr, layer)
    out = pl.BlockSpec((rows_t, d_model), lambda i: (i, 0))
    return pl.pallas_call(
        body, name=name, grid=(seq // rows_t,),
        in_specs=_u_blocks(rows_t, d_model, half) + [row(half), row(half), row(d_model), full(w_att), full(w_conv), full(w_out),
                                                     full(cln_g), full(cln_b), full(b_proj), full(ln_g), full(ln_b)],
        out_specs=[out, out],
        out_shape=[jax.ShapeDtypeStruct((seq, d_model), F32), jax.ShapeDtypeStruct((seq, d_model), BF16)],
        compiler_params=_cparams(("parallel",)),
    )(u, u, u, u, u, u, att, c1, x, w_att, w_conv, w_out, cln_g, cln_b, b_proj, ln_g, ln_b)


def _mix_bwd(u, att, c1, x, dxn, w_att, w_conv, w_out, cln_g, cln_b, b_proj, ln_g, *, layer, seq, d_model, name):
    half = d_model // 2
    rows_t = min(MIX_ROWS, seq)
    nt = ((1,), (1,))
    tn = ((0,), (0,))

    def body(zatt_ref, zconv_ref, ga0, ga1, gc0, gc1, att_ref, c1_ref, x_ref, dxn_ref, wa_ref, wc_ref, wo_ref,
             cg_ref, cb_ref, bp_ref, lg_ref,
             dzatt_ref, dzconv_ref, dgatt_ref, dgconv_ref, datt_ref, dc1_ref, dxres_ref, dwa_ref, dwc_ref, dwo_ref,
             dcg_ref, dcb_ref, dcbias_ref, dbp_ref, dlg_ref, dlb_ref):
        sums = (dwa_ref, dwc_ref, dwo_ref, dcg_ref, dcb_ref, dcbias_ref, dbp_ref, dlg_ref, dlb_ref)

        @pl.when(pl.program_id(0) == 0)
        def _():
            for ref in sums:
                ref[...] = jnp.zeros_like(ref)

        zatt, zconv, att = zatt_ref[...], zconv_ref[...], att_ref[...]
        gatt = jnp.concatenate([ga0[...], ga1[...]], axis=1)
        gconv = jnp.concatenate([gc0[...], gc1[...]], axis=1)
        wa, wc, wo = wa_ref[...], wc_ref[...], wo_ref[...]
        mid = _mix_forward(zatt, att, c1_ref[...], zconv, gatt, gconv, x_ref[...], wa, wc, wo,
                           cg_ref[...], cb_ref[...], bp_ref[...])
        xhat, rstd = _layer_norm_stats(mid["h"])
        dh, dlg, dlb = _layer_norm_bwd(dxn_ref[...], xhat, rstd, lg_ref[...])
        dlg_ref[...] += dlg
        dlb_ref[...] += dlb
        dxres_ref[...] = DEEPNORM_ALPHA * dh
        dy = dh.astype(BF16)
        dwo_ref[...] += lax.dot_general(mid["merged_b"], dy, (tn, ((), ())), preferred_element_type=F32)
        dmerged = lax.dot_general(dy, wo, (nt, ((), ())), preferred_element_type=F32)
        s_ga, s_gc, ab, cb = mid["s_gatt"], mid["s_gconv"], mid["ab"], mid["cb"]
        dgatt_ref[...] = (dmerged * ab * s_ga * (1.0 - s_ga)).astype(BF16)
        dgconv_ref[...] = (dmerged * cb * s_gc * (1.0 - s_gc)).astype(BF16)
        dab = dmerged * s_ga
        dcb = dmerged * s_gc
        dbp_ref[...] += jnp.sum(dcb, axis=0, keepdims=True)
        dab_b, dcb_b = dab.astype(BF16), dcb.astype(BF16)
        dwa_ref[...] += lax.dot_general(mid["a_in_b"], dab_b, (tn, ((), ())), preferred_element_type=F32)
        da_in = lax.dot_general(dab_b, wa, (nt, ((), ())), preferred_element_type=F32)
        s_za = mid["s_zatt"]
        datt_ref[...] = da_in * (zatt * s_za)
        dzatt_ref[...] = (da_in * att * (s_za * (1.0 + zatt * (1.0 - s_za)))).astype(BF16)
        dwc_ref[...] += lax.dot_general(mid["c_in_b"], dcb_b, (tn, ((), ())), preferred_element_type=F32)
        dc_in = lax.dot_general(dcb_b, wc, (nt, ((), ())), preferred_element_type=F32)
        s_zc, c2, s_c2 = mid["s_zconv"], mid["c2"], mid["s_c2"]
        dzconv_ref[...] = (dc_in * mid["c3"] * (s_zc * (1.0 + zconv * (1.0 - s_zc)))).astype(BF16)
        dc3 = dc_in * (zconv * s_zc)
        dc2 = dc3 * (s_c2 * (1.0 + c2 * (1.0 - s_c2)))
        dc1, dcg, dcbeta = _layer_norm_bwd(dc2, mid["chat"], mid["c_rstd"], cg_ref[...])
        dcg_ref[...] += dcg
        dcb_ref[...] += dcbeta
        dcbias_ref[...] += jnp.sum(dc1, axis=0, keepdims=True)
        dc1_ref[...] = dc1

    row = lambda width: pl.BlockSpec((rows_t, width), lambda i: (i, 0))
    full = lambda arr: _of_layer(arr, layer)
    whole = lambda r, c: pl.BlockSpec((r, c), lambda i: (0, 0))
    sds = jax.ShapeDtypeStruct
    out_specs = [row(half), row(half), row(d_model), row(d_model), row(half), row(half), row(d_model),
                 whole(half, d_model), whole(half, d_model), whole(d_model, d_model),
                 whole(1, half), whole(1, half), whole(1, half), whole(1, d_model), whole(1, d_model), whole(1, d_model)]
    out_shape = [sds((seq, half), BF16), sds((seq, half), BF16), sds((seq, d_model), BF16), sds((seq, d_model), BF16),
                 sds((seq, half), F32), sds((seq, half), F32), sds((seq, d_model), F32),
                 sds((half, d_model), F32), sds((half, d_model), F32), sds((d_model, d_model), F32),
                 sds((1, half), F32), sds((1, half), F32), sds((1, half), F32),
                 sds((1, d_model), F32), sds((1, d_model), F32), sds((1, d_model), F32)]
    return pl.pallas_call(
        body, name=name, grid=(seq // rows_t,),
        in_specs=_u_blocks(rows_t, d_model, half) + [row(half), row(half), row(d_model), row(d_model), full(w_att), full(w_conv),
                                                     full(w_out), full(cln_g), full(cln_b), full(b_proj), full(ln_g)],
        out_specs=out_specs, out_shape=out_shape,
        compiler_params=_cparams(("arbitrary",)),
    )(u, u, u, u, u, u, att, c1, x, dxn, w_att, w_conv, w_out, cln_g, cln_b, b_proj, ln_g)


def _loss_head(y, target, *, seq, d_model, name):
    rows_t = min(512, seq)

    def body(y_ref, t_ref, dy_ref, loss_ref):
        @pl.when(pl.program_id(0) == 0)
        def _():
            loss_ref[...] = jnp.zeros_like(loss_ref)

        err = y_ref[...] - t_ref[...]
        dy_ref[...] = err * (1.0 / d_model)
        per_token = jnp.sum(err * err, axis=-1, keepdims=True) * (1.0 / d_model)
        loss_ref[...] += 0.5 * jnp.sum(per_token, axis=0, keepdims=True)

    row = pl.BlockSpec((rows_t, d_model), lambda i: (i, 0))
    return pl.pallas_call(
        body, name=name, grid=(seq // rows_t,), in_specs=[row, row],
        out_specs=[row, pl.BlockSpec((1, 1), lambda i: (0, 0))],
        out_shape=[jax.ShapeDtypeStruct((seq, d_model), F32), jax.ShapeDtypeStruct((1, 1), F32)],
        compiler_params=_cparams(("arbitrary",)),
    )(y, target)


def _adamw(w, g, m, v, *, name):
    rows, cols = w.shape
    rows_t = rows
    for cand in (512, 256, 128, 64, 32, 16, 8):
        if rows % cand == 0 and cand * cols * 4 <= 2 * 1024 * 1024:
            rows_t = cand
            break

    def body(w_ref, g_ref, m_ref, v_ref, d_ref, nm_ref, nv_ref):
        grad = g_ref[...]
        new_m = ADAM_B1 * m_ref[...] + (1.0 - ADAM_B1) * grad
        new_v = ADAM_B2 * v_ref[...] + (1.0 - ADAM_B2) * (grad * grad)
        m_hat = new_m / (1.0 - ADAM_B1 ** ADAM_STEP)
        v_hat = new_v / (1.0 - ADAM_B2 ** ADAM_STEP)
        d_ref[...] = -ADAM_LR * (m_hat / (jnp.sqrt(v_hat) + ADAM_EPS) + ADAM_WD * w_ref[...])
        nm_ref[...] = new_m
        nv_ref[...] = new_v

    blk = pl.BlockSpec((rows_t, cols), lambda i: (i, 0))
    out = jax.ShapeDtypeStruct((rows, cols), F32)
    return pl.pallas_call(
        body, name=name, grid=(rows // rows_t,), in_specs=[blk] * 4, out_specs=[blk] * 3, out_shape=[out] * 3,
        compiler_params=_cparams(("parallel",)),
    )(w, g, m, v)


MATRICES = ("w_in", "w_att_proj", "w_conv_proj", "w_out")
MATRIX_CHIP_AXIS = (1, 1, 1, 0)
MATRIX_CORE_AXIS = (0, 0, 0, 1)


def _train_pass(x, target, mats, taps, b_in, conv_b, cln_g, cln_b, b_proj, ln_g, ln_b, place):
    seq, d_model = x.shape
    half = d_model // 2
    depth = b_in.shape[0]
    scan_suffix, scan_prefix = _scan_matrices()
    cols = half // LANES
    dims = dict(seq=seq, d_model=d_model)
    conv_dims = dict(seq=seq, d_conv=half, col_a=4 * cols, col_b=5 * cols)
    axes3 = [axis + 1 for axis in MATRIX_CHIP_AXIS]
    n_mat = len(mats)

    def first_layers(relay):
        return _gather_rider(list(mats) + [taps], axes3 + [2], [(0, 1)] * n_mat + [(0, depth)], relay)

    *mats, taps = _lone_call(_chain(first_layers(False), first_layers(True)), name="gather_first")

    xs, xbs, us, atts, c1s = [x], [x.astype(BF16)], [], [], []
    for l in range(depth):
        u = _matmul(xbs[l], mats[0], layer=l, mode="nn", bias=b_in[l].reshape(1, -1), name="in_proj", tm=256, tn=b_in.shape[1],
                    tk=d_model)
        nxt = [(l + 1, 1)] * n_mat
        if l + 1 < depth:
            att, mats = _attn_fwd(u, scan_suffix, seq=seq, d_att=half, name="attn_fwd", rider=_gather_rider(mats, axes3, nxt, False))
            c1, mats = _conv_fwd(u, taps, conv_b, layer=l, name="conv_fwd", rider=_gather_rider(mats, axes3, nxt, True), **conv_dims)
        else:
            att = _attn_fwd(u, scan_suffix, seq=seq, d_att=half, name="attn_fwd_last")
            c1 = _conv_fwd(u, taps, conv_b, layer=l, name="conv_fwd_last", **conv_dims)
        xn, xnb = _mix_fwd(u, att, c1, xs[l], mats[1], mats[2], mats[3], cln_g, cln_b, b_proj, ln_g, ln_b, layer=l, name="mix_fwd", **dims)
        us.append(u)
        atts.append(att)
        c1s.append(c1)
        xs.append(xn)
        xbs.append(xnb)
    w_in, w_att, w_conv, w_out = mats

    def chip_sums(layer, partials, got, blocks):
        return [_chip_sum_layer(partials[t], got[t], blocks[t], layer, depth, MATRIX_CHIP_AXIS[t], MATRIX_CORE_AXIS[t], place,
                                name="chip_sum_" + MATRICES[t]) for t in range(n_mat)]

    dx, loss = _loss_head(xs[depth], target, name="loss_head", **dims)
    grads = [None] * depth
    blocks = [None] * n_mat
    waiting = None
    for l in reversed(range(depth)):
        u = us[l]
        (dzatt, dzconv, dgatt, dgconv, datt, dc1, dxres, dwa, dwc, dwo, dcg, dcb, dcbias, dbp, dlg, dlb) = _mix_bwd(
            u, atts[l], c1s[l], xs[l], dx, w_att, w_conv, w_out, cln_g, cln_b, b_proj, ln_g, layer=l, name="mix_bwd", **dims)
        if waiting is None:
            dq, dk, dv = _attn_bwd(u, datt, scan_suffix, scan_prefix, seq=seq, d_att=half, name="attn_bwd_first")
        else:
            (dq, dk, dv), got = _attn_bwd(u, datt, scan_suffix, scan_prefix, seq=seq, d_att=half, name="attn_bwd",
                                          rider=_exchange_rider(waiting[1], MATRIX_CHIP_AXIS))
            blocks = chip_sums(waiting[0], waiting[1], got, blocks)
        dglu_a, dglu_b, dconvw = _conv_bwd(u, dc1, taps, layer=l, name="conv_bwd", **conv_dims)
        du = jnp.concatenate([dq, dk, dv, dzatt, dglu_a, dglu_b, dzconv, dgatt, dgconv], axis=1)
        dwin, dbin = _matmul(xbs[l], du, mode="tn", colsum=True, name="in_proj_dw", tm=1024, tn=512, tk=seq)
        parts = [dwin, dwa, dwc, dwo]
        dx, got = _matmul(du, w_in, layer=l, mode="nt", add=dxres, name="in_proj_dx", tm=512, tn=1024, tk=du.shape[1],
                          rider=_swap_rider(parts, MATRIX_CORE_AXIS))
        waiting = (l, [_pair_sum_layer(parts[t], got[t], MATRIX_CORE_AXIS[t], place, name="pair_sum_" + MATRICES[t])
                       for t in range(n_mat)])
        grads[l] = dict(b_in=dbin[0], conv_w=dconvw[:CONV_WIDTH], conv_b=dcbias[0], conv_ln_g=dcg[0], conv_ln_b=dcb[0],
                        b_conv_proj=dbp[0], ln_g=dlg[0], ln_b=dlb[0])
    got = _lone_call(_exchange_rider(waiting[1], MATRIX_CHIP_AXIS), name="exchange_last")
    blocks = chip_sums(waiting[0], waiting[1], got, blocks)
    return loss, dx, grads, blocks


MESH = pl.DeviceIdType.MESH
IN_HBM = pl.BlockSpec(memory_space=pl.ANY)


def _position():
    x, y, c = lax.axis_index("x"), lax.axis_index("y"), lax.axis_index("c")
    return x, y, c, [(1 - x, y), (x, 1 - y), (1 - x, 1 - y)]


def _cut(ref, axis, start, size):
    idx = [slice(None)] * len(ref.shape)
    idx[axis] = pl.ds(start, size)
    return ref.at[tuple(idx)]


def _remote(src, dst, send_sem, recv_sem, device):
    return pltpu.make_async_remote_copy(src_ref=src, dst_ref=dst, send_sem=send_sem, recv_sem=recv_sem,
                                        device_id=device, device_id_type=MESH)


def _comm_call(body, name, operands, out_shape, scratch, alias=False):
    return pl.pallas_call(
        body, name=name, in_specs=[IN_HBM] * len(operands), out_specs=[IN_HBM] * len(out_shape), out_shape=out_shape,
        scratch_shapes=scratch, input_output_aliases={t: t for t in range(len(operands))} if alias else {},
    )(*operands)


def _gather_rider(wholes, chip_axes, spans, relay):
    n = len(wholes)

    def region(dst, t, chip, half):
        first, count = spans[t]
        _, rows, cols = wholes[t].shape
        if chip_axes[t] == 2:
            size, part = cols // N_CHIPS, rows // 2
            ref = _cut(_cut(dst[t], 2, pl.multiple_of(chip * size, size), size), 1, pl.multiple_of(half * part, part), part)
        else:
            size = rows // N_CHIPS
            part = size // 2
            ref = _cut(dst[t], 1, pl.multiple_of(chip * size + half * part, part), part)
        return _cut(ref, 0, first, count)

    def copies(dst, sems, receiving):
        x, y, c, chips = _position()
        send_sem, recv_sem = sems
        out = []
        for t in range(n):
            for j, chip in enumerate(chips):
                theirs = 2 * chip[0] + chip[1]
                if relay:
                    ref = region(dst, t, theirs, 1 - c if receiving else c)
                    peer = (x, y, 1 - c)
                else:
                    ref = region(dst, t, theirs if receiving else 2 * x + y, c)
                    peer = (*chip, c)
                out.append(_remote(ref, ref, send_sem.at[t, j], recv_sem.at[t, j], peer))
        return out

    def start(ins, outs, sems):
        for cp in copies(outs, sems, False):
            cp.start()

    def finish(ins, outs, sems):
        for cp in copies(outs, sems, True):
            cp.wait_recv()
        for cp in copies(outs, sems, False):
            cp.wait_send()

    return _Rider(operands=list(wholes), out_shape=[jax.ShapeDtypeStruct(w.shape, w.dtype) for w in wholes],
                  aliases={t: t for t in range(n)}, scratch=[pltpu.SemaphoreType.DMA((n, N_CHIPS - 1))] * 2,
                  start=start, finish=finish)


def _chain(first, second):
    cut = len(first.scratch)

    def start(ins, outs, sems):
        first.start(ins, outs, sems[:cut])
        first.finish(ins, outs, sems[:cut])
        second.start(ins, outs, sems[cut:])

    def finish(ins, outs, sems):
        second.finish(ins, outs, sems[cut:])

    return first._replace(scratch=list(first.scratch) + list(second.scratch), start=start, finish=finish)


def _swap_rider(parts, core_axes):
    n = len(parts)
    halves = []
    for arr, axis in zip(parts, core_axes):
        shape = list(arr.shape)
        shape[axis] //= 2
        halves.append(jax.ShapeDtypeStruct(tuple(shape), arr.dtype))

    def copies(ins, outs, sems):
        x, y, c, _ = _position()
        out = []
        for t in range(n):
            size = halves[t].shape[core_axes[t]]
            piece = _cut(ins[t], core_axes[t], pl.multiple_of((1 - c) * size, size), size)
            out.append(_remote(piece, outs[t], sems[0].at[t], sems[1].at[t], (x, y, 1 - c)))
        return out

    def start(ins, outs, sems):
        for cp in copies(ins, outs, sems):
            cp.start()

    def finish(ins, outs, sems):
        for cp in copies(ins, outs, sems):
            cp.wait()

    return _Rider(operands=list(parts), out_shape=halves, aliases={}, scratch=[pltpu.SemaphoreType.DMA((n,))] * 2,
                  start=start, finish=finish)


def _exchange_rider(partials, chip_axes):
    n = len(partials)
    quarters = []
    for arr, axis in zip(partials, chip_axes):
        shape = list(arr.shape)
        shape[axis] //= N_CHIPS
        quarters.append(jax.ShapeDtypeStruct((N_CHIPS - 1, *shape), arr.dtype))

    def copies(ins, outs, sems):
        x, y, c, chips = _position()
        out = []
        for t in range(n):
            size = quarters[t].shape[1 + chip_axes[t]]
            for j, chip in enumerate(chips):
                piece = _cut(ins[t], chip_axes[t], pl.multiple_of((2 * chip[0] + chip[1]) * size, size), size)
                out.append(_remote(piece, outs[t].at[j], sems[0].at[t, j], sems[1].at[t, j], (*chip, c)))
        return out

    def start(ins, outs, sems):
        for cp in copies(ins, outs, sems):
            cp.start()

    def finish(ins, outs, sems):
        for cp in copies(ins, outs, sems):
            cp.wait()

    return _Rider(operands=list(partials), out_shape=quarters, aliases={}, scratch=[pltpu.SemaphoreType.DMA((n, N_CHIPS - 1))] * 2,
                  start=start, finish=finish)


def _place_block(shard, chip_axis, place, dtype, *, name):
    depth, rows, cols = shard.shape
    rows_t = _row_tile(rows, cols * 4 * 4, 16 * 1024 * 1024)
    steps = rows // rows_t
    shape = list(shard.shape)
    shape[chip_axis] *= N_CHIPS
    if chip_axis == 1:
        out_spec = pl.BlockSpec((None, rows_t, cols), lambda l, i, p: (l, p[1] * steps + i, 0))
    else:
        out_spec = pl.BlockSpec((None, rows_t, cols), lambda l, i, p: (l, i, p[1]))

    def body(place_ref, src_ref, out_ref):
        out_ref[...] = src_ref[...].astype(dtype)

    return pl.pallas_call(
        body, name=name, out_shape=jax.ShapeDtypeStruct(tuple(shape), dtype),
        grid_spec=pltpu.PrefetchScalarGridSpec(num_scalar_prefetch=1, grid=(depth, steps),
                                               in_specs=[pl.BlockSpec((None, rows_t, cols), lambda l, i, p: (l, i, 0))],
                                               out_specs=out_spec),
        compiler_params=_cparams(("arbitrary", "arbitrary")),
    )(place, shard)


def _pair_join(blocks, core_axes):
    n = len(blocks)

    def body(*refs):
        dst = refs[n:2 * n]
        send_sem, recv_sem = refs[2 * n:]
        x, y, c, _ = _position()
        copies = []
        for t in range(n):
            size = blocks[t].shape[core_axes[t]] // 2
            mine = _cut(dst[t], core_axes[t], pl.multiple_of(c * size, size), size)
            cp = _remote(mine, mine, send_sem.at[t], recv_sem.at[t], (x, y, 1 - c))
            cp.start()
            copies.append(cp)
        for t, cp in enumerate(copies):
            size = blocks[t].shape[core_axes[t]] // 2
            other = _cut(dst[t], core_axes[t], pl.multiple_of((1 - c) * size, size), size)
            cp.wait_send()
            _remote(other, other, send_sem.at[t], recv_sem.at[t], (x, y, 1 - c)).wait_recv()

    return _comm_call(body, "pair_join", blocks, [jax.ShapeDtypeStruct(b.shape, b.dtype) for b in blocks],
                      [pltpu.SemaphoreType.DMA((n,))] * 2, alias=True)


def _gather_small(vec):
    n_dev = 2 * N_CHIPS

    def body(src, dst, send_sem, recv_sem, local_sem):
        x, y, c, _ = _position()
        flip = lambda v, bit: 1 - v if bit else v
        mine = pltpu.make_async_copy(src, dst.at[4 * x + 2 * y + c], local_sem)
        mine.start()
        copies = []
        for mask in range(1, n_dev):
            peer = (flip(x, mask & 4), flip(y, mask & 2), flip(c, mask & 1))
            cp = _remote(src, dst.at[4 * x + 2 * y + c], send_sem.at[mask - 1], recv_sem.at[mask - 1], peer)
            cp.start()
            copies.append((cp, peer))
        for mask, (cp, peer) in enumerate(copies, start=1):
            theirs = dst.at[4 * peer[0] + 2 * peer[1] + peer[2]]
            _remote(src, theirs, send_sem.at[mask - 1], recv_sem.at[mask - 1], peer).wait_recv()
        for cp, _ in copies:
            cp.wait_send()
        mine.wait()

    out = [jax.ShapeDtypeStruct((n_dev, *vec.shape), vec.dtype)]
    return _comm_call(body, "gather_small", [vec], out, [pltpu.SemaphoreType.DMA((n_dev - 1,))] * 2 + [pltpu.SemaphoreType.DMA(())])[0]


def _row_tile(rows, row_bytes, budget):
    tile = rows
    for cand in (512, 256, 128, 64, 32, 16, 8):
        if rows % cand == 0:
            tile = cand
            if cand * row_bytes <= budget:
                break
    return tile


def _pair_sum_layer(part, got, core_axis, place, *, name):
    rows, cols = got.shape
    rows_t = _row_tile(rows, cols * 4 * 6, 16 * 1024 * 1024)
    steps = rows // rows_t
    if core_axis == 0:
        part_spec = pl.BlockSpec((rows_t, cols), lambda i, p: (p[0] * steps + i, 0))
    else:
        part_spec = pl.BlockSpec((rows_t, cols), lambda i, p: (i, p[0]))
    own_spec = pl.BlockSpec((rows_t, cols), lambda i, p: (i, 0))

    def body(place_ref, part_ref, got_ref, out_ref):
        out_ref[...] = (part_ref[...] + got_ref[...]).astype(BF16)

    return pl.pallas_call(
        body, name=name, out_shape=jax.ShapeDtypeStruct(got.shape, BF16),
        grid_spec=pltpu.PrefetchScalarGridSpec(num_scalar_prefetch=1, grid=(steps,), in_specs=[part_spec, own_spec], out_specs=own_spec),
        compiler_params=_cparams(("arbitrary",)),
    )(place, part, got)


def _chip_sum_layer(partial, got, blocks, layer, depth, chip_axis, core_axis, place, *, name):
    _, rows, cols = got.shape
    rows_t = _row_tile(rows, cols * 4 * 10, 24 * 1024 * 1024)
    steps = rows // rows_t
    if chip_axis == 0:
        own_spec = pl.BlockSpec((rows_t, cols), lambda i, p: (p[1] * steps + i, 0))
    else:
        own_spec = pl.BlockSpec((rows_t, cols), lambda i, p: (i, p[1]))
    got_spec = pl.BlockSpec((N_CHIPS - 1, rows_t, cols), lambda i, p: (0, i, 0))
    shape = [depth, rows, cols]
    shape[1 + core_axis] *= 2
    if core_axis == 0:
        out_spec = pl.BlockSpec((None, rows_t, cols), lambda i, p: (layer, p[0] * steps + i, 0))
    else:
        out_spec = pl.BlockSpec((None, rows_t, cols), lambda i, p: (layer, i, p[0]))

    def body(place_ref, own_ref, got_ref, *rest):
        out_ref = rest[-1]
        up = lambda val: val.astype(F32)
        out_ref[...] = ((up(own_ref[...]) + up(got_ref[0])) + up(got_ref[1])) + up(got_ref[2])

    in_specs, operands, aliases = [own_spec, got_spec], [place, partial, got], {}
    if blocks is not None:
        in_specs.append(pl.BlockSpec(memory_space=pl.ANY))
        operands.append(blocks)
        aliases = {3: 0}
    return pl.pallas_call(
        body, name=name, out_shape=jax.ShapeDtypeStruct(tuple(shape), F32),
        grid_spec=pltpu.PrefetchScalarGridSpec(num_scalar_prefetch=1, grid=(steps,), in_specs=in_specs, out_specs=out_spec),
        input_output_aliases=aliases, compiler_params=_cparams(("arbitrary",)),
    )(*operands)


def _sum_devices(stack, *, name):
    def body(src_ref, out_ref):
        total = src_ref[0]
        for d in range(1, stack.shape[0]):
            total = total + src_ref[d]
        out_ref[...] = total

    return pl.pallas_call(body, name=name, out_shape=jax.ShapeDtypeStruct(stack.shape[1:], F32))(stack)


def kernel(x, w_in, b_in, conv_w, conv_b, conv_ln_g, conv_ln_b, w_att_proj, w_conv_proj, b_conv_proj, w_out, ln_g, ln_b, loss_target, m_w_in, m_b_in, m_conv_w, m_conv_b, m_conv_ln_g, m_conv_ln_b, m_w_att_proj, m_w_conv_proj, m_b_conv_proj, m_w_out, m_ln_g, m_ln_b, v_w_in, v_b_in, v_conv_w, v_conv_b, v_conv_ln_g, v_conv_ln_b, v_w_att_proj, v_w_conv_proj, v_b_conv_proj, v_w_out, v_ln_g, v_ln_b):
    depth = w_in.shape[0]
    d_model = x.shape[-1]
    half = d_model // 2
    chip = 2 * lax.axis_index("x") + lax.axis_index("y")
    place = jnp.stack([lax.axis_index("c"), chip]).astype(jnp.int32)
    vec3 = lambda v: v.reshape(depth, 1, -1)

    taps = jnp.pad(conv_w, ((0, 0), (0, CONV_PAD - CONV_WIDTH), (0, 0)))
    gathered = [("w_in", w_in, 2, BF16), ("w_att_proj", w_att_proj, 2, BF16), ("w_conv_proj", w_conv_proj, 2, BF16),
                ("w_out", w_out, 1, BF16), ("conv_w", taps, 2, F32)]
    wholes = [_place_block(arr, axis, place, dtype, name="place_" + n) for n, arr, axis, dtype in gathered]
    loss, grad_x, grads, blocks = _train_pass(x[0], loss_target[0], wholes[:4], wholes[4], b_in, vec3(conv_b), vec3(conv_ln_g),
                                              vec3(conv_ln_b), vec3(b_conv_proj), vec3(ln_g), vec3(ln_b), place)
    loss = lax.psum(loss[0, 0], ("x", "y", "c"))
    reduced = dict(zip(MATRICES, _pair_join(blocks, [axis + 1 for axis in MATRIX_CORE_AXIS])))

    small = ["b_in", "conv_b", "conv_ln_g", "conv_ln_b", "b_conv_proj", "ln_g", "ln_b"]
    packed = jnp.stack([jnp.concatenate([grads[l][n] for n in small] + [grads[l]["conv_w"].reshape(-1)]) for l in range(depth)])
    total = _sum_devices(_gather_small(packed), name="sum_devices")
    offset = 0
    for n in small:
        width = grads[0][n].shape[0]
        reduced[n] = total[:, offset:offset + width]
        offset += width
    taps = total[:, offset:].reshape(depth, CONV_WIDTH, half)
    reduced["conv_w"] = lax.dynamic_slice_in_dim(taps, chip * conv_w.shape[2], conv_w.shape[2], axis=2)

    names = ["w_in", "b_in", "conv_w", "conv_b", "conv_ln_g", "conv_ln_b", "w_att_proj", "w_conv_proj", "b_conv_proj", "w_out", "ln_g", "ln_b"]
    weights = dict(zip(names, (w_in, b_in, conv_w, conv_b, conv_ln_g, conv_ln_b, w_att_proj, w_conv_proj, b_conv_proj, w_out, ln_g, ln_b)))
    first = dict(zip(names, (m_w_in, m_b_in, m_conv_w, m_conv_b, m_conv_ln_g, m_conv_ln_b, m_w_att_proj, m_w_conv_proj, m_b_conv_proj, m_w_out, m_ln_g, m_ln_b)))
    second = dict(zip(names, (v_w_in, v_b_in, v_conv_w, v_conv_b, v_conv_ln_g, v_conv_ln_b, v_w_att_proj, v_w_conv_proj, v_b_conv_proj, v_w_out, v_ln_g, v_ln_b)))
    delta, new_m, new_v = {}, {}, {}
    for n in names:
        shape = weights[n].shape
        flat = lambda arr: arr.reshape(-1, shape[-1])
        d, m, v = _adamw(flat(weights[n]), flat(reduced[n]), flat(first[n]), flat(second[n]), name="adamw_" + n)
        delta[n], new_m[n], new_v[n] = d.reshape(shape), m.reshape(shape), v.reshape(shape)
    return (loss, grad_x[None], *[reduced[n].reshape(weights[n].shape) for n in names], *[delta[n] for n in names],
            *[new_m[n] for n in names], *[new_v[n] for n in names])
```

```python
import functools
from typing import Callable, NamedTuple

import jax
import jax.numpy as jnp
from jax import lax
from jax.experimental import pallas as pl
from jax.experimental.pallas import tpu as pltpu

F32 = jnp.float32
BF16 = jnp.bfloat16

HEAD_DIM = 64
LANES = 128
CONV_WIDTH = 31
CONV_PAD = 32
SUBLANES = 8
LN_EPS = 1e-5
DEPTH = 4
DEEPNORM_ALPHA = (2 * DEPTH) ** 0.25
ATT_SCALE = HEAD_DIM ** -0.5
ATT_TILE = 128
ATT_DEAD = -104.0
ATT_GROUP = 2
ATT_GROUP_FWD = 4
ATT_FILL = -1e30

ADAM_LR = 0.001
ADAM_B1 = 0.9
ADAM_B2 = 0.999
ADAM_EPS = 1e-08
ADAM_WD = 0.01
ADAM_STEP = 10

VMEM_LIMIT = 56 * 1024 * 1024

N_CHIPS = 4


def _cparams(sem):
    return pltpu.CompilerParams(dimension_semantics=sem, vmem_limit_bytes=VMEM_LIMIT)


def _sigmoid(x):
    return 1.0 / (1.0 + jnp.exp(-x))


class _Rider(NamedTuple):
    operands: list
    out_shape: list
    aliases: dict
    scratch: list
    start: Callable
    finish: Callable


def _host_call(body, rider, *, name, grid, in_specs, out_specs, out_shape, scratch, operands, semantics):
    n_in, n_out = len(in_specs), len(out_specs)
    aliases = {}
    if rider is not None:
        r_in, r_out = len(rider.operands), len(rider.out_shape)
        host_body = body

        def body(*refs):
            base = n_in + r_in
            ins, rins = refs[:n_in], refs[n_in:base]
            outs, routs = refs[base:base + n_out], refs[base + n_out:base + n_out + r_out]
            rest = refs[base + n_out + r_out:]
            split = len(rest) - len(rider.scratch)
            ids = [pl.program_id(axis) for axis in range(len(grid))]
            first = functools.reduce(jnp.logical_and, [i == 0 for i in ids])
            last = functools.reduce(jnp.logical_and, [i == g - 1 for i, g in zip(ids, grid)])

            @pl.when(first)
            def _():
                rider.start(rins, routs, rest[split:])

            host_body(*ins, *outs, *rest[:split])

            @pl.when(last)
            def _():
                rider.finish(rins, routs, rest[split:])

        hbm = pl.BlockSpec(memory_space=pl.ANY)
        in_specs = list(in_specs) + [hbm] * r_in
        out_specs = list(out_specs) + [hbm] * r_out
        out_shape = list(out_shape) + list(rider.out_shape)
        scratch = list(scratch) + list(rider.scratch)
        operands = list(operands) + list(rider.operands)
        aliases = {n_in + i: n_out + o for i, o in rider.aliases.items()}
    res = pl.pallas_call(
        body, name=name, grid=grid, in_specs=list(in_specs), out_specs=list(out_specs), out_shape=list(out_shape),
        scratch_shapes=list(scratch), input_output_aliases=aliases, compiler_params=_cparams(semantics),
    )(*operands)
    return list(res[:n_out]), list(res[n_out:])


def _lone_call(rider, *, name):
    r_in = len(rider.operands)

    def body(*refs):
        ins, outs, sems = refs[:r_in], refs[r_in:r_in + len(rider.out_shape)], refs[r_in + len(rider.out_shape):]
        rider.start(ins, outs, sems)
        rider.finish(ins, outs, sems)

    hbm = pl.BlockSpec(memory_space=pl.ANY)
    return list(pl.pallas_call(
        body, name=name, in_specs=[hbm] * r_in, out_specs=[hbm] * len(rider.out_shape), out_shape=list(rider.out_shape),
        scratch_shapes=list(rider.scratch), input_output_aliases=dict(rider.aliases),
    )(*rider.operands))


def _fit(tile, dim):
    assert dim % LANES == 0
    tile = min(tile, dim) // LANES * LANES
    while dim % tile:
        tile -= LANES
    return tile


_DIMS = {"nn": ((1,), (0,)), "nt": ((1,), (1,)), "tn": ((0,), (0,))}


def _matmul(a, b, *, mode, name, layer=None, bias=None, add=None, colsum=False, out_dtype=F32, tm=1024, tn=512, tk=1024,
            rider=None):
    b_shape = b.shape if layer is None else b.shape[1:]
    if mode == "nn":
        (m, k), (k2, n) = a.shape, b_shape
    elif mode == "nt":
        (m, k), (n, k2) = a.shape, b_shape
    else:
        (k, m), (k2, n) = a.shape, b_shape
    assert k == k2
    tm, tn, tk = _fit(tm, m), _fit(tn, n), _fit(tk, k)
    gm, gn, nk = m // tm, n // tn, k // tk

    a_spec = pl.BlockSpec((tk, tm), lambda i, j, kk: (kk, i)) if mode == "tn" else pl.BlockSpec((tm, tk), lambda i, j, kk: (i, kk))
    if layer is None:
        b_spec = pl.BlockSpec((tn, tk), lambda i, j, kk: (j, kk)) if mode == "nt" else pl.BlockSpec((tk, tn), lambda i, j, kk: (kk, j))
    elif mode == "nt":
        b_spec = pl.BlockSpec((None, tn, tk), lambda i, j, kk: (layer, j, kk))
    else:
        b_spec = pl.BlockSpec((None, tk, tn), lambda i, j, kk: (layer, kk, j))
    in_specs, operands = [a_spec, b_spec], [a, b]
    if bias is not None:
        in_specs.append(pl.BlockSpec((1, tn), lambda i, j, kk: (0, j)))
        operands.append(bias)
    if add is not None:
        in_specs.append(pl.BlockSpec((tm, tn), lambda i, j, kk: (i, j)))
        operands.append(add)
    out_shape = [jax.ShapeDtypeStruct((m, n), out_dtype)]
    out_specs = [pl.BlockSpec((tm, tn), lambda i, j, kk: (i, j))]
    scratch = [pltpu.VMEM((tm, tn), F32)] if nk > 1 else []
    if colsum:
        assert mode == "tn"
        out_shape.append(jax.ShapeDtypeStruct((gm, 1, n), F32))
        out_specs.append(pl.BlockSpec((1, 1, tn), lambda i, j, kk: (i, 0, j)))
        if nk > 1:
            scratch.append(pltpu.VMEM((1, tn), F32))
    has_bias, has_add = bias is not None, add is not None

    def body(*refs):
        refs = list(refs)
        a_ref, b_ref = refs[0], refs[1]
        pos = 2
        bias_ref = add_ref = None
        if has_bias:
            bias_ref = refs[pos]
            pos += 1
        if has_add:
            add_ref = refs[pos]
            pos += 1
        o_ref = refs[pos]
        pos += 1
        cs_ref = None
        if colsum:
            cs_ref = refs[pos]
            pos += 1

        def finish(out, sums):
            if has_bias:
                out = out + bias_ref[...]
            if has_add:
                out = out + add_ref[...]
            o_ref[...] = out.astype(out_dtype)
            if colsum:
                cs_ref[0] = sums

        bv = b_ref[...]
        prod = lax.dot_general(a_ref[...].astype(BF16), bv.astype(BF16), (_DIMS[mode], ((), ())), preferred_element_type=F32)
        sums = jnp.sum(bv.astype(F32), axis=0, keepdims=True) if colsum else None
        if nk == 1:
            finish(prod, sums)
            return
        acc_ref = refs[pos]
        cs_acc = refs[pos + 1] if colsum else None
        kk = pl.program_id(2)

        @pl.when(kk == 0)
        def _():
            acc_ref[...] = jnp.zeros_like(acc_ref)
            if colsum:
                cs_acc[...] = jnp.zeros_like(cs_acc)

        acc_ref[...] += prod
        if colsum:
            cs_acc[...] += sums

        @pl.when(kk == nk - 1)
        def _():
            finish(acc_ref[...], cs_acc[...] if colsum else None)

    res, rode = _host_call(body, rider, name=name, grid=(gm, gn, nk), in_specs=in_specs, out_specs=out_specs, out_shape=out_shape,
                           scratch=scratch, operands=operands, semantics=("arbitrary", "arbitrary", "arbitrary"))
    out = (res[0], res[1][0]) if colsum else res[0]
    return out if rider is None else (out, rode)


def _scan_matrices():
    t = ATT_TILE
    r = lax.broadcasted_iota(jnp.int32, (t, t), 0)
    c = lax.broadcasted_iota(jnp.int32, (t, t), 1)
    ones = jnp.ones((t, t), F32)
    suffix = jnp.concatenate([(r > c).astype(F32), ones], axis=1)
    prefix = jnp.concatenate([(r < c).astype(F32), ones], axis=1)
    stack = lambda mat: jnp.concatenate([mat, mat], axis=0).astype(BF16)
    return stack(suffix), stack(prefix)


def _split_halves(val):
    hi = val.astype(BF16)
    lo = (val - hi.astype(F32)).astype(BF16)
    return jnp.concatenate([hi, lo], axis=1)


def _split_scan(val, mat_ref):
    return jnp.dot(_split_halves(val), mat_ref[...], preferred_element_type=F32)


def _pair_scores(q, k_lo, k_hi, masked):
    t = ATT_TILE
    z2 = lax.dot_general(q, jnp.concatenate([k_lo, k_hi], axis=0), (((1,), (1,)), ((), ())), preferred_element_type=F32)
    out = []
    for h in range(2):
        z = z2[:, h * t:(h + 1) * t]
        sp = jnp.log(1.0 + jnp.exp(-jnp.abs(z)))
        f = jnp.minimum(-z, 0.0) - sp
        a = f + z
        if masked:
            causal = lax.broadcasted_iota(jnp.int32, (t, t), 1) < lax.broadcasted_iota(jnp.int32, (t, t), 0)
            f = jnp.where(causal, f, 0.0)
        out.append((_split_halves(f), a))
    return out


def _any_alive(runs):
    top = functools.reduce(jnp.maximum, [run for per_head in runs for run in per_head])
    return (jnp.max(top) > ATT_DEAD).astype(jnp.int32)


def _head_copies(seq, src_ref, scale, lo_ref, hi_ref, plain_ref):
    chunk = min(256, seq)
    low = lax.broadcasted_iota(jnp.int32, (chunk, LANES), 1) < HEAD_DIM

    def step(r, carry):
        rows = pl.ds(pl.multiple_of(r * chunk, chunk), chunk)
        val = src_ref[rows, :]
        if scale != 1.0:
            val = val * scale
        if lo_ref is not None:
            lo_ref[rows, :] = jnp.where(low, val, 0.0).astype(BF16)
            hi_ref[rows, :] = jnp.where(low, 0.0, val).astype(BF16)
        if plain_ref is not None:
            plain_ref[rows, :] = val.astype(BF16)
        return carry

    lax.fori_loop(0, seq // chunk, step, 0)


def _attn_fwd(u, scan_suffix, *, seq, d_att, name, rider=None):
    t = ATT_TILE
    nq = seq // t
    pairs = d_att // LANES
    grp = ATT_GROUP_FWD
    assert nq % grp == 0

    def body(q_ref, k_ref, v_ref, um_ref, o_ref, qq, k0, k1, v0, v1, f2_s, a_s, lg_s, tot_s, run_s, acc_s):
        _head_copies(seq, q_ref, ATT_SCALE, None, None, qq)
        _head_copies(seq, k_ref, 1.0, k0, k1, None)
        _head_copies(seq, v_ref, 1.0, v0, v1, None)

        def group(gi, carry):
            qb0 = gi * grp
            qrows = [pl.ds(pl.multiple_of((qb0 + g) * t, t), t) for g in range(grp)]
            qv = [qq[qrows[g], :] for g in range(grp)]

            chains = [(h, g) for g in range(grp) for h in range(2)]

            def key_rows(g, i):
                return pl.ds(pl.multiple_of(jnp.maximum(qb0 + g - i, 0) * t, t), t)

            def stage1(i, masked):
                out = []
                for g in range(grp):
                    krows = key_rows(g, i)
                    out += _pair_scores(qv[g], k0[krows, :], k1[krows, :], masked)
                return out

            def stage2(halves, a, masked):
                scan = jnp.dot(halves, um_ref[...], preferred_element_type=F32)
                logit = a + scan[:, :t]
                if masked:
                    causal = lax.broadcasted_iota(jnp.int32, (t, t), 1) < lax.broadcasted_iota(jnp.int32, (t, t), 0)
                    logit = jnp.where(causal, logit, ATT_FILL)
                return logit, scan[:, t:]

            def put(halves_a=None, logit_total=None):
                for c in range(len(chains)):
                    if halves_a is not None:
                        f2_s[c], a_s[c] = halves_a[c]
                    if logit_total is not None:
                        lg_s[c], tot_s[c] = logit_total[c]

            first = stage1(0, True)
            put(halves_a=stage1(1, False), logit_total=[stage2(f2, a, True) for f2, a in first])
            for c in range(len(chains)):
                run_s[c] = jnp.zeros((t, t), F32)
            for g in range(grp):
                acc_s[g] = jnp.zeros((t, LANES), F32)

            def wbody(st):
                i = st[0]
                held = [(f2_s[c], a_s[c]) for c in range(len(chains))]
                logits = [lg_s[c] for c in range(len(chains))]
                totals = [tot_s[c] for c in range(len(chains))]
                runs = [run_s[c] for c in range(len(chains))]
                accs = [acc_s[g] for g in range(grp)]
                for g in range(grp):
                    krows = key_rows(g, i)
                    gone = jnp.where(qb0 + g - i >= 0, 0.0, ATT_FILL)
                    weights = []
                    for c in (2 * g, 2 * g + 1):
                        run = runs[c] + gone
                        weights.append(jnp.exp(logits[c] + run).astype(BF16))
                        runs[c] = run + totals[c]
                    accs[g] = accs[g] + jnp.dot(jnp.concatenate(weights, axis=1), jnp.concatenate([v0[krows, :], v1[krows, :]], axis=0),
                                                preferred_element_type=F32)
                ahead2 = [stage2(f2, a, False) for f2, a in held]
                ahead1 = stage1(i + 2, False)
                put(halves_a=ahead1, logit_total=ahead2)
                for c in range(len(chains)):
                    run_s[c] = runs[c]
                for g in range(grp):
                    acc_s[g] = accs[g]
                more = jnp.logical_and(i + 1 <= qb0 + grp - 1, _any_alive([runs]) > 0)
                return i + 1, more.astype(jnp.int32)

            lax.while_loop(lambda st: st[1] > 0, wbody, (jnp.int32(0), jnp.int32(1)))
            for g in range(grp):
                o_ref[qrows[g], :] = acc_s[g]
            return carry

        lax.fori_loop(0, nq // grp, group, 0)

    blk = lambda base: pl.BlockSpec((seq, LANES), lambda j, base=base: (0, base + j))
    res, rode = _host_call(
        body, rider, name=name, grid=(pairs,),
        in_specs=[blk(0), blk(pairs), blk(2 * pairs), pl.BlockSpec((2 * t, 2 * t), lambda j: (0, 0))],
        out_specs=[pl.BlockSpec((seq, LANES), lambda j: (0, j))],
        out_shape=[jax.ShapeDtypeStruct((seq, d_att), F32)],
        scratch=[pltpu.VMEM((seq, LANES), BF16)] * 5 + [pltpu.VMEM((2 * grp, t, 2 * t), BF16)]
        + [pltpu.VMEM((2 * grp, t, t), F32)] * 4 + [pltpu.VMEM((grp, t, LANES), F32)],
        operands=[u, u, u, scan_suffix], semantics=("arbitrary",))
    return res[0] if rider is None else (res[0], rode)


def _attn_bwd(u, d_att_out, scan_suffix, scan_prefix, *, seq, d_att, name, rider=None):
    t = ATT_TILE
    nq = seq // t
    pairs = d_att // LANES
    grp = ATT_GROUP
    assert nq % grp == 0

    def body(q_ref, k_ref, v_ref, do_ref, um_ref, pm_ref, dq_ref, dk_ref, dv_ref,
             qq, q0, q1, k0, k1, v0, v1, dd, do0, do1, dq_acc, dk_acc, dv_acc, g_st, b_st,
             f2_s, a_s, lg_s, tot_s, dw_s, run_s, p_s, pt_s, grun_s, dq_s):
        _head_copies(seq, q_ref, ATT_SCALE, q0, q1, qq)
        _head_copies(seq, k_ref, 1.0, k0, k1, None)
        _head_copies(seq, v_ref, 1.0, v0, v1, None)
        _head_copies(seq, do_ref, 1.0, do0, do1, dd)
        dk_acc[...] = jnp.zeros_like(dk_acc)
        dv_acc[...] = jnp.zeros_like(dv_acc)

        tn = (((0,), (0,)), ((), ()))
        nt = (((1,), (1,)), ((), ()))

        def stacked(lo_ref, hi_ref, rows):
            return jnp.concatenate([lo_ref[rows, :], hi_ref[rows, :]], axis=0)

        def group(gi, carry):
            qb0 = gi * grp
            qrows = [pl.ds(pl.multiple_of((qb0 + g) * t, t), t) for g in range(grp)]
            qv = [qq[qrows[g], :] for g in range(grp)]
            dov = [dd[qrows[g], :] for g in range(grp)]
            q_heads = [stacked(q0, q1, qrows[g]) for g in range(grp)]
            do_heads = [stacked(do0, do1, qrows[g]) for g in range(grp)]

            def key_rows(g, i):
                return pl.ds(pl.multiple_of(jnp.maximum(qb0 + g - i, 0) * t, t), t)

            chains = [(h, g) for g in range(grp) for h in range(2)]
            every = range(len(chains))

            def stage1(i, masked):
                out = []
                for g in range(grp):
                    krows = key_rows(g, i)
                    out += _pair_scores(qv[g], k0[krows, :], k1[krows, :], masked)
                return out

            def stage2(i, held, masked):
                out = []
                for g in range(grp):
                    dw2 = lax.dot_general(dov[g], stacked(v0, v1, key_rows(g, i)), nt, preferred_element_type=F32)
                    for h in range(2):
                        c = 2 * g + h
                        halves, a = held[c]
                        scan = jnp.dot(halves, um_ref[...], preferred_element_type=F32)
                        logit = a + scan[:, :t]
                        if masked:
                            causal = lax.broadcasted_iota(jnp.int32, (t, t), 1) < lax.broadcasted_iota(jnp.int32, (t, t), 0)
                            logit = jnp.where(causal, logit, ATT_FILL)
                        b_st[c, i] = jnp.exp(a)
                        out.append((logit, scan[:, t:], dw2[:, h * t:(h + 1) * t]))
                return out

            def put(held=None, ready=None):
                for c in every:
                    if held is not None:
                        f2_s[c], a_s[c] = held[c]
                    if ready is not None:
                        lg_s[c], tot_s[c], dw_s[c] = ready[c]

            put(held=stage1(1, False), ready=stage2(0, stage1(0, True), True))
            for c in every:
                run_s[c] = jnp.zeros((t, t), F32)

            def wbody(st):
                i = st[0]
                held = [(f2_s[c], a_s[c]) for c in every]
                ready = [(lg_s[c], tot_s[c], dw_s[c]) for c in every]
                runs = [run_s[c] for c in every]
                for g in range(grp):
                    gone = jnp.where(qb0 + g - i >= 0, 0.0, ATT_FILL)
                    weights = []
                    for c in (2 * g, 2 * g + 1):
                        logit, total, dw = ready[c]
                        run = runs[c] + gone
                        w = jnp.exp(logit + run)
                        g_st[c, i] = w * dw
                        weights.append(w.astype(BF16))
                        runs[c] = run + total
                    dv_acc[key_rows(g, i), :] += lax.dot_general(jnp.concatenate(weights, axis=0), do_heads[g], tn,
                                                                 preferred_element_type=F32)
                ahead2 = stage2(i + 1, held, False)
                ahead1 = stage1(i + 2, False)
                put(held=ahead1, ready=ahead2)
                for c in every:
                    run_s[c] = runs[c]
                more = jnp.logical_and(i + 1 <= qb0 + grp - 1, _any_alive([runs]) > 0)
                return i + 1, more.astype(jnp.int32)

            steps = lax.while_loop(lambda st: st[1] > 0, wbody, (jnp.int32(0), jnp.int32(1)))[0]

            def prefix(i):
                out = []
                for c in every:
                    scan = _split_scan(g_st[c, i], pm_ref)
                    out.append((scan[:, :t], scan[:, t:]))
                return out

            def back(i, masked):
                sums = [(p_s[c], pt_s[c]) for c in every]
                gruns = [grun_s[c] for c in every]
                dqs = [dq_s[g] for g in range(grp)]
                for g in range(grp):
                    krows = key_rows(g, i)
                    dzs = []
                    for c in (2 * g, 2 * g + 1):
                        gt = g_st[c, i]
                        dz = gt - b_st[c, i] * (gt + gruns[c] + sums[c][0])
                        if masked:
                            causal = lax.broadcasted_iota(jnp.int32, (t, t), 1) < lax.broadcasted_iota(jnp.int32, (t, t), 0)
                            dz = jnp.where(causal, dz, 0.0)
                        dzs.append(dz.astype(BF16))
                        gruns[c] = gruns[c] + sums[c][1]
                    dqs[g] = dqs[g] + jnp.dot(jnp.concatenate(dzs, axis=1), stacked(k0, k1, krows), preferred_element_type=F32)
                    dk_acc[krows, :] += lax.dot_general(jnp.concatenate(dzs, axis=0), q_heads[g], tn, preferred_element_type=F32)
                return gruns, dqs

            def keep(sums=None, gruns=None, dqs=None):
                for c in every:
                    if sums is not None:
                        p_s[c], pt_s[c] = sums[c]
                    if gruns is not None:
                        grun_s[c] = gruns[c]
                if dqs is not None:
                    for g in range(grp):
                        dq_s[g] = dqs[g]

            keep(sums=prefix(steps - 1), gruns=[jnp.zeros((t, t), F32)] * len(chains), dqs=[jnp.zeros((t, LANES), F32)] * grp)

            def bbody(j, carry2):
                i = steps - 1 - j
                gruns, dqs = back(i, False)
                keep(sums=prefix(i - 1), gruns=gruns, dqs=dqs)
                return carry2

            lax.fori_loop(0, steps - 1, bbody, 0)
            _, dqs = back(0, True)
            for g in range(grp):
                dq_acc[qrows[g], :] = dqs[g]
            return carry

        lax.fori_loop(0, nq // grp, group, 0)

        chunk = min(256, seq)

        def emit(r, carry):
            rows = pl.ds(pl.multiple_of(r * chunk, chunk), chunk)
            dq_ref[rows, :] = (dq_acc[rows, :] * ATT_SCALE).astype(BF16)
            dk_ref[rows, :] = dk_acc[rows, :].astype(BF16)
            dv_ref[rows, :] = dv_acc[rows, :].astype(BF16)
            return carry

        lax.fori_loop(0, seq // chunk, emit, 0)

    blk = lambda base: pl.BlockSpec((seq, LANES), lambda j, base=base: (0, base + j), pipeline_mode=pl.Buffered(1))
    mat = pl.BlockSpec((2 * t, 2 * t), lambda j: (0, 0))
    out = jax.ShapeDtypeStruct((seq, d_att), BF16)
    res, rode = _host_call(
        body, rider, name=name, grid=(pairs,),
        in_specs=[blk(0), blk(pairs), blk(2 * pairs), blk(0), mat, mat],
        out_specs=[pl.BlockSpec((seq, LANES), lambda j: (0, j))] * 3,
        out_shape=[out, out, out],
        scratch=[pltpu.VMEM((seq, LANES), BF16)] * 10 + [pltpu.VMEM((seq, LANES), F32)] * 3
        + [pltpu.VMEM((2 * grp, nq + 1, t, t), F32)] * 2 + [pltpu.VMEM((2 * grp, t, 2 * t), BF16)]
        + [pltpu.VMEM((2 * grp, t, t), F32)] * 8 + [pltpu.VMEM((grp, t, LANES), F32)],
        operands=[u, u, u, d_att_out, scan_suffix, scan_prefix], semantics=("arbitrary",))
    return res if rider is None else (res, rode)


CONV_ROWS = 256


def _shifted(window, residue, rows):
    total = rows + CONV_PAD
    return window if residue == 0 else pltpu.roll(window, total - residue, 0)


def _glu_to_pad(seq, a_ref, b_ref, pad_ref):
    chunk = min(CONV_ROWS, seq)
    pad_ref[pl.ds(0, CONV_PAD), :] = jnp.zeros((CONV_PAD, LANES), F32)

    def step(r, carry):
        rows = pl.ds(pl.multiple_of(r * chunk, chunk), chunk)
        pad_ref[pl.ds(pl.multiple_of(r * chunk + CONV_PAD, SUBLANES), chunk), :] = a_ref[rows, :] * _sigmoid(b_ref[rows, :])
        return carry

    lax.fori_loop(0, seq // chunk, step, 0)


def _conv_fwd(u, conv_w, conv_b, *, layer, seq, d_conv, col_a, col_b, name, rider=None):
    blocks = d_conv // LANES
    rows_t = min(CONV_ROWS, seq)
    shift0 = CONV_PAD - (CONV_WIDTH - 1)

    def body(a_ref, b_ref, w_ref, bias_ref, o_ref, pad_ref):
        _glu_to_pad(seq, a_ref, b_ref, pad_ref)

        def step(r, carry):
            base = pl.multiple_of(r * rows_t, rows_t)
            window = pad_ref[pl.ds(base, rows_t + CONV_PAD), :]
            acc = jnp.zeros((rows_t, LANES), F32) + bias_ref[...]
            for residue in range(SUBLANES):
                moved = _shifted(window, residue, rows_t)
                for tap in range(CONV_WIDTH):
                    if (shift0 + tap) % SUBLANES == residue:
                        lo = (shift0 + tap) - residue
                        acc = acc + w_ref[tap:tap + 1, :] * moved[lo:lo + rows_t, :]
            o_ref[pl.ds(base, rows_t), :] = acc
            return carry

        lax.fori_loop(0, seq // rows_t, step, 0)

    res, rode = _host_call(
        body, rider, name=name, grid=(blocks,),
        in_specs=[pl.BlockSpec((seq, LANES), lambda j: (0, col_a + j)), pl.BlockSpec((seq, LANES), lambda j: (0, col_b + j)),
                  pl.BlockSpec((None, CONV_PAD, LANES), lambda j: (layer, 0, j)),
                  pl.BlockSpec((None, 1, LANES), lambda j: (layer, 0, j))],
        out_specs=[pl.BlockSpec((seq, LANES), lambda j: (0, j))],
        out_shape=[jax.ShapeDtypeStruct((seq, d_conv), F32)],
        scratch=[pltpu.VMEM((seq + CONV_PAD, LANES), F32)],
        operands=[u, u, conv_w, conv_b], semantics=("arbitrary",))
    return res[0] if rider is None else (res[0], rode)


def _conv_bwd(u, dc1, conv_w, *, layer, seq, d_conv, col_a, col_b, name):
    blocks = d_conv // LANES
    rows_t = min(CONV_ROWS, seq)
    shift0 = CONV_PAD - (CONV_WIDTH - 1)

    def body(a_ref, b_ref, d_ref, w_ref, da_ref, db_ref, dw_ref, pad_ref, dpad_ref, dw_acc):
        _glu_to_pad(seq, a_ref, b_ref, pad_ref)
        dpad_ref[pl.ds(seq, CONV_PAD), :] = jnp.zeros((CONV_PAD, LANES), F32)

        def fill(r, carry):
            rows = pl.ds(pl.multiple_of(r * rows_t, rows_t), rows_t)
            dpad_ref[rows, :] = d_ref[rows, :]
            return carry

        lax.fori_loop(0, seq // rows_t, fill, 0)
        dw_acc[...] = jnp.zeros_like(dw_acc)

        def step(r, carry):
            base = pl.multiple_of(r * rows_t, rows_t)
            rows = pl.ds(base, rows_t)
            window = dpad_ref[pl.ds(base, rows_t + CONV_PAD), :]
            acc = jnp.zeros((rows_t, LANES), F32)
            for residue in range(SUBLANES):
                moved = _shifted(window, residue, rows_t)
                for tap in range(CONV_WIDTH):
                    off = CONV_WIDTH - 1 - tap
                    if off % SUBLANES == residue:
                        lo = off - residue
                        acc = acc + w_ref[tap:tap + 1, :] * moved[lo:lo + rows_t, :]
            sig = _sigmoid(b_ref[rows, :])
            a = a_ref[rows, :]
            da_ref[rows, :] = (acc * sig).astype(BF16)
            db_ref[rows, :] = (acc * a * sig * (1.0 - sig)).astype(BF16)
            dcur = d_ref[rows, :]
            cwin = pad_ref[pl.ds(base, rows_t + CONV_PAD), :]
            for residue in range(SUBLANES):
                moved = _shifted(cwin, residue, rows_t)
                for tap in range(CONV_WIDTH):
                    if (shift0 + tap) % SUBLANES == residue:
                        lo = (shift0 + tap) - residue
                        prod = dcur * moved[lo:lo + rows_t, :]
                        dw_acc[tap] += jnp.sum(prod.reshape(rows_t // SUBLANES, SUBLANES, LANES), axis=0)
            return carry

        lax.fori_loop(0, seq // rows_t, step, 0)
        dw_ref[...] = jnp.sum(dw_acc[...], axis=1)

    col = lambda base: pl.BlockSpec((seq, LANES), lambda j, base=base: (0, base + j))
    own = pl.BlockSpec((seq, LANES), lambda j: (0, j))
    return pl.pallas_call(
        body, name=name, grid=(blocks,),
        in_specs=[col(col_a), col(col_b), own, pl.BlockSpec((None, CONV_PAD, LANES), lambda j: (layer, 0, j))],
        out_specs=[own, own, pl.BlockSpec((CONV_PAD, LANES), lambda j: (0, j))],
        out_shape=[jax.ShapeDtypeStruct((seq, d_conv), BF16), jax.ShapeDtypeStruct((seq, d_conv), BF16),
                   jax.ShapeDtypeStruct((CONV_PAD, d_conv), F32)],
        scratch_shapes=[pltpu.VMEM((seq + CONV_PAD, LANES), F32), pltpu.VMEM((seq + CONV_PAD, LANES), F32),
                        pltpu.VMEM((CONV_PAD, SUBLANES, LANES), F32)],
        compiler_params=_cparams(("parallel",)),
    )(u, u, dc1, conv_w)


MIX_ROWS = 256


def _layer_norm_stats(val):
    mu = jnp.mean(val, axis=-1, keepdims=True)
    cen = val - mu
    var = jnp.mean(cen * cen, axis=-1, keepdims=True)
    rstd = lax.rsqrt(var + LN_EPS)
    return cen * rstd, rstd


def _layer_norm_bwd(dy, xhat, rstd, gain):
    dxhat = dy * gain
    m1 = jnp.mean(dxhat, axis=-1, keepdims=True)
    m2 = jnp.mean(dxhat * xhat, axis=-1, keepdims=True)
    dx = rstd * (dxhat - m1 - xhat * m2)
    return dx, jnp.sum(dy * xhat, axis=0, keepdims=True), jnp.sum(dy, axis=0, keepdims=True)


def _mix_forward(zatt, att, c1, zconv, gatt, gconv, x, w_att, w_conv, w_out, cln_g, cln_b, b_proj):
    s_zatt = _sigmoid(zatt)
    a_in = att * (zatt * s_zatt)
    chat, c_rstd = _layer_norm_stats(c1)
    c2 = chat * cln_g + cln_b
    s_c2 = _sigmoid(c2)
    c3 = c2 * s_c2
    s_zconv = _sigmoid(zconv)
    c_in = c3 * (zconv * s_zconv)
    a_in_b, c_in_b = a_in.astype(BF16), c_in.astype(BF16)
    ab = jnp.dot(a_in_b, w_att, preferred_element_type=F32)
    cb = jnp.dot(c_in_b, w_conv, preferred_element_type=F32) + b_proj
    s_gatt, s_gconv = _sigmoid(gatt), _sigmoid(gconv)
    merged_b = (s_gatt * ab + s_gconv * cb).astype(BF16)
    y = jnp.dot(merged_b, w_out, preferred_element_type=F32)
    h = DEEPNORM_ALPHA * x + y
    return dict(s_zatt=s_zatt, a_in_b=a_in_b, chat=chat, c_rstd=c_rstd, c2=c2, s_c2=s_c2, c3=c3, s_zconv=s_zconv,
                c_in_b=c_in_b, ab=ab, cb=cb, s_gatt=s_gatt, s_gconv=s_gconv, merged_b=merged_b, h=h)


def _u_blocks(rows_t, width, half):
    return [pl.BlockSpec((rows_t, half), lambda i, c=c: (i, c)) for c in (3, 6, 7, 8, 9, 10)]


def _of_layer(arr, layer):
    return pl.BlockSpec((None,) + arr.shape[1:], lambda i: (layer, 0, 0))


def _mix_fwd(u, att, c1, x, w_att, w_conv, w_out, cln_g, cln_b, b_proj, ln_g, ln_b, *, layer, seq, d_model, name):
    half = d_model // 2
    rows_t = min(MIX_ROWS, seq)

    def body(zatt_ref, zconv_ref, ga0, ga1, gc0, gc1, att_ref, c1_ref, x_ref, wa_ref, wc_ref, wo_ref,
             cg_ref, cb_ref, bp_ref, lg_ref, lb_ref, o_ref, ob_ref):
        gatt = jnp.concatenate([ga0[...], ga1[...]], axis=1)
        gconv = jnp.concatenate([gc0[...], gc1[...]], axis=1)
        mid = _mix_forward(zatt_ref[...], att_ref[...], c1_ref[...], zconv_ref[...], gatt, gconv, x_ref[...],
                           wa_ref[...], wc_ref[...], wo_ref[...], cg_ref[...], cb_ref[...], bp_ref[...])
        xhat, _ = _layer_norm_stats(mid["h"])
        out = xhat * lg_ref[...] + lb_ref[...]
        o_ref[...] = out
        ob_ref[...] = out.astype(BF16)

    row = lambda width: pl.BlockSpec((rows_t, width), lambda i: (i, 0))
    full = lambda arr: _of_layer(arr, layer)
    out = pl.BlockSpec((rows_t, d_model), lambda i: (i, 0))
    return pl.pallas_call(
        body, name=name, grid=(seq // rows_t,),
        in_specs=_u_blocks(rows_t, d_model, half) + [row(half), row(half), row(d_model), full(w_att), full(w_conv), full(w_out),
                                                     full(cln_g), full(cln_b), full(b_proj), full(ln_g), full(ln_b)],
        out_specs=[out, out],
        out_shape=[jax.ShapeDtypeStruct((seq, d_model), F32), jax.ShapeDtypeStruct((seq, d_model), BF16)],
        compiler_params=_cparams(("parallel",)),
    )(u, u, u, u, u, u, att, c1, x, w_att, w_conv, w_out, cln_g, cln_b, b_proj, ln_g, ln_b)


def _mix_bwd(u, att, c1, x, dxn, w_att, w_conv, w_out, cln_g, cln_b, b_proj, ln_g, *, layer, seq, d_model, name):
    half = d_model // 2
    rows_t = min(MIX_ROWS, seq)
    nt = ((1,), (1,))
    tn = ((0,), (0,))

    def body(zatt_ref, zconv_ref, ga0, ga1, gc0, gc1, att_ref, c1_ref, x_ref, dxn_ref, wa_ref, wc_ref, wo_ref,
             cg_ref, cb_ref, bp_ref, lg_ref,
             dzatt_ref, dzconv_ref, dgatt_ref, dgconv_ref, datt_ref, dc1_ref, dxres_ref, dwa_ref, dwc_ref, dwo_ref,
             dcg_ref, dcb_ref, dcbias_ref, dbp_ref, dlg_ref, dlb_ref):
        sums = (dwa_ref, dwc_ref, dwo_ref, dcg_ref, dcb_ref, dcbias_ref, dbp_ref, dlg_ref, dlb_ref)

        @pl.when(pl.program_id(0) == 0)
        def _():
            for ref in sums:
                ref[...] = jnp.zeros_like(ref)

        zatt, zconv, att = zatt_ref[...], zconv_ref[...], att_ref[...]
        gatt = jnp.concatenate([ga0[...], ga1[...]], axis=1)
        gconv = jnp.concatenate([gc0[...], gc1[...]], axis=1)
        wa, wc, wo = wa_ref[...], wc_ref[...], wo_ref[...]
        mid = _mix_forward(zatt, att, c1_ref[...], zconv, gatt, gconv, x_ref[...], wa, wc, wo,
                           cg_ref[...], cb_ref[...], bp_ref[...])
        xhat, rstd = _layer_norm_stats(mid["h"])
        dh, dlg, dlb = _layer_norm_bwd(dxn_ref[...], xhat, rstd, lg_ref[...])
        dlg_ref[...] += dlg
        dlb_ref[...] += dlb
        dxres_ref[...] = DEEPNORM_ALPHA * dh
        dy = dh.astype(BF16)
        dwo_ref[...] += lax.dot_general(mid["merged_b"], dy, (tn, ((), ())), preferred_element_type=F32)
        dmerged = lax.dot_general(dy, wo, (nt, ((), ())), preferred_element_type=F32)
        s_ga, s_gc, ab, cb = mid["s_gatt"], mid["s_gconv"], mid["ab"], mid["cb"]
        dgatt_ref[...] = (dmerged * ab * s_ga * (1.0 - s_ga)).astype(BF16)
        dgconv_ref[...] = (dmerged * cb * s_gc * (1.0 - s_gc)).astype(BF16)
        dab = dmerged * s_ga
        dcb = dmerged * s_gc
        dbp_ref[...] += jnp.sum(dcb, axis=0, keepdims=True)
        dab_b, dcb_b = dab.astype(BF16), dcb.astype(BF16)
        dwa_ref[...] += lax.dot_general(mid["a_in_b"], dab_b, (tn, ((), ())), preferred_element_type=F32)
        da_in = lax.dot_general(dab_b, wa, (nt, ((), ())), preferred_element_type=F32)
        s_za = mid["s_zatt"]
        datt_ref[...] = da_in * (zatt * s_za)
        dzatt_ref[...] = (da_in * att * (s_za * (1.0 + zatt * (1.0 - s_za)))).astype(BF16)
        dwc_ref[...] += lax.dot_general(mid["c_in_b"], dcb_b, (tn, ((), ())), preferred_element_type=F32)
        dc_in = lax.dot_general(dcb_b, wc, (nt, ((), ())), preferred_element_type=F32)
        s_zc, c2, s_c2 = mid["s_zconv"], mid["c2"], mid["s_c2"]
        dzconv_ref[...] = (dc_in * mid["c3"] * (s_zc * (1.0 + zconv * (1.0 - s_zc)))).astype(BF16)
        dc3 = dc_in * (zconv * s_zc)
        dc2 = dc3 * (s_c2 * (1.0 + c2 * (1.0 - s_c2)))
        dc1, dcg, dcbeta = _layer_norm_bwd(dc2, mid["chat"], mid["c_rstd"], cg_ref[...])
        dcg_ref[...] += dcg
        dcb_ref[...] += dcbeta
        dcbias_ref[...] += jnp.sum(dc1, axis=0, keepdims=True)
        dc1_ref[...] = dc1

    row = lambda width: pl.BlockSpec((rows_t, width), lambda i: (i, 0))
    full = lambda arr: _of_layer(arr, layer)
    whole = lambda r, c: pl.BlockSpec((r, c), lambda i: (0, 0))
    sds = jax.ShapeDtypeStruct
    out_specs = [row(half), row(half), row(d_model), row(d_model), row(half), row(half), row(d_model),
                 whole(half, d_model), whole(half, d_model), whole(d_model, d_model),
                 whole(1, half), whole(1, half), whole(1, half), whole(1, d_model), whole(1, d_model), whole(1, d_model)]
    out_shape = [sds((seq, half), BF16), sds((seq, half), BF16), sds((seq, d_model), BF16), sds((seq, d_model), BF16),
                 sds((seq, half), F32), sds((seq, half), F32), sds((seq, d_model), F32),
                 sds((half, d_model), F32), sds((half, d_model), F32), sds((d_model, d_model), F32),
                 sds((1, half), F32), sds((1, half), F32), sds((1, half), F32),
                 sds((1, d_model), F32), sds((1, d_model), F32), sds((1, d_model), F32)]
    return pl.pallas_call(
        body, name=name, grid=(seq // rows_t,),
        in_specs=_u_blocks(rows_t, d_model, half) + [row(half), row(half), row(d_model), row(d_model), full(w_att), full(w_conv),
                                                     full(w_out), full(cln_g), full(cln_b), full(b_proj), full(ln_g)],
        out_specs=out_specs, out_shape=out_shape,
        compiler_params=_cparams(("arbitrary",)),
    )(u, u, u, u, u, u, att, c1, x, dxn, w_att, w_conv, w_out, cln_g, cln_b, b_proj, ln_g)


def _loss_head(y, target, *, seq, d_model, name):
    rows_t = min(512, seq)

    def body(y_ref, t_ref, dy_ref, loss_ref):
        @pl.when(pl.program_id(0) == 0)
        def _():
            loss_ref[...] = jnp.zeros_like(loss_ref)

        err = y_ref[...] - t_ref[...]
        dy_ref[...] = err * (1.0 / d_model)
        per_token = jnp.sum(err * err, axis=-1, keepdims=True) * (1.0 / d_model)
        loss_ref[...] += 0.5 * jnp.sum(per_token, axis=0, keepdims=True)

    row = pl.BlockSpec((rows_t, d_model), lambda i: (i, 0))
    return pl.pallas_call(
        body, name=name, grid=(seq // rows_t,), in_specs=[row, row],
        out_specs=[row, pl.BlockSpec((1, 1), lambda i: (0, 0))],
        out_shape=[jax.ShapeDtypeStruct((seq, d_model), F32), jax.ShapeDtypeStruct((1, 1), F32)],
        compiler_params=_cparams(("arbitrary",)),
    )(y, target)


def _adamw(w, g, m, v, *, name):
    rows, cols = w.shape
    rows_t = rows
    for cand in (512, 256, 128, 64, 32, 16, 8):
        if rows % cand == 0 and cand * cols * 4 <= 2 * 1024 * 1024:
            rows_t = cand
            break

    def body(w_ref, g_ref, m_ref, v_ref, d_ref, nm_ref, nv_ref):
        grad = g_ref[...]
        new_m = ADAM_B1 * m_ref[...] + (1.0 - ADAM_B1) * grad
        new_v = ADAM_B2 * v_ref[...] + (1.0 - ADAM_B2) * (grad * grad)
        m_hat = new_m / (1.0 - ADAM_B1 ** ADAM_STEP)
        v_hat = new_v / (1.0 - ADAM_B2 ** ADAM_STEP)
        d_ref[...] = -ADAM_LR * (m_hat / (jnp.sqrt(v_hat) + ADAM_EPS) + ADAM_WD * w_ref[...])
        nm_ref[...] = new_m
        nv_ref[...] = new_v

    blk = pl.BlockSpec((rows_t, cols), lambda i: (i, 0))
    out = jax.ShapeDtypeStruct((rows, cols), F32)
    return pl.pallas_call(
        body, name=name, grid=(rows // rows_t,), in_specs=[blk] * 4, out_specs=[blk] * 3, out_shape=[out] * 3,
        compiler_params=_cparams(("parallel",)),
    )(w, g, m, v)


MATRICES = ("w_in", "w_att_proj", "w_conv_proj", "w_out")
MATRIX_CHIP_AXIS = (1, 1, 1, 0)
MATRIX_CORE_AXIS = (0, 0, 0, 1)


def _train_pass(x, target, mats, taps, b_in, conv_b, cln_g, cln_b, b_proj, ln_g, ln_b, place):
    seq, d_model = x.shape
    half = d_model // 2
    depth = b_in.shape[0]
    scan_suffix, scan_prefix = _scan_matrices()
    cols = half // LANES
    dims = dict(seq=seq, d_model=d_model)
    conv_dims = dict(seq=seq, d_conv=half, col_a=4 * cols, col_b=5 * cols)
    axes3 = [axis + 1 for axis in MATRIX_CHIP_AXIS]
    n_mat = len(mats)

    def first_layers(relay):
        return _gather_rider(list(mats) + [taps], axes3 + [2], [(0, 1)] * n_mat + [(0, depth)], relay)

    *mats, taps = _lone_call(_chain(first_layers(False), first_layers(True)), name="gather_first")

    xs, xbs, us, atts, c1s = [x], [x.astype(BF16)], [], [], []
    for l in range(depth):
        u = _matmul(xbs[l], mats[0], layer=l, mode="nn", bias=b_in[l].reshape(1, -1), name="in_proj", tm=256, tn=b_in.shape[1],
                    tk=d_model)
        nxt = [(l + 1, 1)] * n_mat
        if l + 1 < depth:
            att, mats = _attn_fwd(u, scan_suffix, seq=seq, d_att=half, name="attn_fwd", rider=_gather_rider(mats, axes3, nxt, False))
            c1, mats = _conv_fwd(u, taps, conv_b, layer=l, name="conv_fwd", rider=_gather_rider(mats, axes3, nxt, True), **conv_dims)
        else:
            att = _attn_fwd(u, scan_suffix, seq=seq, d_att=half, name="attn_fwd_last")
            c1 = _conv_fwd(u, taps, conv_b, layer=l, name="conv_fwd_last", **conv_dims)
        xn, xnb = _mix_fwd(u, att, c1, xs[l], mats[1], mats[2], mats[3], cln_g, cln_b, b_proj, ln_g, ln_b, layer=l, name="mix_fwd", **dims)
        us.append(u)
        atts.append(att)
        c1s.append(c1)
        xs.append(xn)
        xbs.append(xnb)
    w_in, w_att, w_conv, w_out = mats

    def chip_sums(layer, partials, got, blocks):
        return [_chip_sum_layer(partials[t], got[t], blocks[t], layer, depth, MATRIX_CHIP_AXIS[t], MATRIX_CORE_AXIS[t], place,
                                name="chip_sum_" + MATRICES[t]) for t in range(n_mat)]

    dx, loss = _loss_head(xs[depth], target, name="loss_head", **dims)
    grads = [None] * depth
    blocks = [None] * n_mat
    waiting = None
    for l in reversed(range(depth)):
        u = us[l]
        (dzatt, dzconv, dgatt, dgconv, datt, dc1, dxres, dwa, dwc, dwo, dcg, dcb, dcbias, dbp, dlg, dlb) = _mix_bwd(
            u, atts[l], c1s[l], xs[l], dx, w_att, w_conv, w_out, cln_g, cln_b, b_proj, ln_g, layer=l, name="mix_bwd", **dims)
        if waiting is None:
            dq, dk, dv = _attn_bwd(u, datt, scan_suffix, scan_prefix, seq=seq, d_att=half, name="attn_bwd_first")
        else:
            (dq, dk, dv), got = _attn_bwd(u, datt, scan_suffix, scan_prefix, seq=seq, d_att=half, name="attn_bwd",
                                          rider=_exchange_rider(waiting[1], MATRIX_CHIP_AXIS))
            blocks = chip_sums(waiting[0], waiting[1], got, blocks)
        dglu_a, dglu_b, dconvw = _conv_bwd(u, dc1, taps, layer=l, name="conv_bwd", **conv_dims)
        du = jnp.concatenate([dq, dk, dv, dzatt, dglu_a, dglu_b, dzconv, dgatt, dgconv], axis=1)
        dwin, dbin = _matmul(xbs[l], du, mode="tn", colsum=True, name="in_proj_dw", tm=1024, tn=512, tk=seq)
        parts = [dwin, dwa, dwc, dwo]
        dx, got = _matmul(du, w_in, layer=l, mode="nt", add=dxres, name="in_proj_dx", tm=512, tn=1024, tk=du.shape[1],
                          rider=_swap_rider(parts, MATRIX_CORE_AXIS))
        waiting = (l, [_pair_sum_layer(parts[t], got[t], MATRIX_CORE_AXIS[t], place, name="pair_sum_" + MATRICES[t])
                       for t in range(n_mat)])
        grads[l] = dict(b_in=dbin[0], conv_w=dconvw[:CONV_WIDTH], conv_b=dcbias[0], conv_ln_g=dcg[0], conv_ln_b=dcb[0],
                        b_conv_proj=dbp[0], ln_g=dlg[0], ln_b=dlb[0])
    got = _lone_call(_exchange_rider(waiting[1], MATRIX_CHIP_AXIS), name="exchange_last")
    blocks = chip_sums(waiting[0], waiting[1], got, blocks)
    return loss, dx, grads, blocks


MESH = pl.DeviceIdType.MESH
IN_HBM = pl.BlockSpec(memory_space=pl.ANY)


def _position():
    x, y, c = lax.axis_index("x"), lax.axis_index("y"), lax.axis_index("c")
    return x, y, c, [(1 - x, y), (x, 1 - y), (1 - x, 1 - y)]


def _cut(ref, axis, start, size):
    idx = [slice(None)] * len(ref.shape)
    idx[axis] = pl.ds(start, size)
    return ref.at[tuple(idx)]


def _remote(src, dst, send_sem, recv_sem, device):
    return pltpu.make_async_remote_copy(src_ref=src, dst_ref=dst, send_sem=send_sem, recv_sem=recv_sem,
                                        device_id=device, device_id_type=MESH)


def _comm_call(body, name, operands, out_shape, scratch, alias=False):
    return pl.pallas_call(
        body, name=name, in_specs=[IN_HBM] * len(operands), out_specs=[IN_HBM] * len(out_shape), out_shape=out_shape,
        scratch_shapes=scratch, input_output_aliases={t: t for t in range(len(operands))} if alias else {},
    )(*operands)


def _gather_rider(wholes, chip_axes, spans, relay):
    n = len(wholes)

    def region(dst, t, chip, half):
        first, count = spans[t]
        _, rows, cols = wholes[t].shape
        if chip_axes[t] == 2:
            size, part = cols // N_CHIPS, rows // 2
            ref = _cut(_cut(dst[t], 2, pl.multiple_of(chip * size, size), size), 1, pl.multiple_of(half * part, part), part)
        else:
            size = rows // N_CHIPS
            part = size // 2
            ref = _cut(dst[t], 1, pl.multiple_of(chip * size + half * part, part), part)
        return _cut(ref, 0, first, count)

    def copies(dst, sems, receiving):
        x, y, c, chips = _position()
        send_sem, recv_sem = sems
        out = []
        for t in range(n):
            for j, chip in enumerate(chips):
                theirs = 2 * chip[0] + chip[1]
                if relay:
                    ref = region(dst, t, theirs, 1 - c if receiving else c)
                    peer = (x, y, 1 - c)
                else:
                    ref = region(dst, t, theirs if receiving else 2 * x + y, c)
                    peer = (*chip, c)
                out.append(_remote(ref, ref, send_sem.at[t, j], recv_sem.at[t, j], peer))
        return out

    def start(ins, outs, sems):
        for cp in copies(outs, sems, False):
            cp.start()

    def finish(ins, outs, sems):
        for cp in copies(outs, sems, True):
            cp.wait_recv()
        for cp in copies(outs, sems, False):
            cp.wait_send()

    return _Rider(operands=list(wholes), out_shape=[jax.ShapeDtypeStruct(w.shape, w.dtype) for w in wholes],
                  aliases={t: t for t in range(n)}, scratch=[pltpu.SemaphoreType.DMA((n, N_CHIPS - 1))] * 2,
                  start=start, finish=finish)


def _chain(first, second):
    cut = len(first.scratch)

    def start(ins, outs, sems):
        first.start(ins, outs, sems[:cut])
        first.finish(ins, outs, sems[:cut])
        second.start(ins, outs, sems[cut:])

    def finish(ins, outs, sems):
        second.finish(ins, outs, sems[cut:])

    return first._replace(scratch=list(first.scratch) + list(second.scratch), start=start, finish=finish)


def _swap_rider(parts, core_axes):
    n = len(parts)
    halves = []
    for arr, axis in zip(parts, core_axes):
        shape = list(arr.shape)
        shape[axis] //= 2
        halves.append(jax.ShapeDtypeStruct(tuple(shape), arr.dtype))

    def copies(ins, outs, sems):
        x, y, c, _ = _position()
        out = []
        for t in range(n):
            size = halves[t].shape[core_axes[t]]
            piece = _cut(ins[t], core_axes[t], pl.multiple_of((1 - c) * size, size), size)
            out.append(_remote(piece, outs[t], sems[0].at[t], sems[1].at[t], (x, y, 1 - c)))
        return out

    def start(ins, outs, sems):
        for cp in copies(ins, outs, sems):
            cp.start()

    def finish(ins, outs, sems):
        for cp in copies(ins, outs, sems):
            cp.wait()

    return _Rider(operands=list(parts), out_shape=halves, aliases={}, scratch=[pltpu.SemaphoreType.DMA((n,))] * 2,
                  start=start, finish=finish)


def _exchange_rider(partials, chip_axes):
    n = len(partials)
    quarters = []
    for arr, axis in zip(partials, chip_axes):
        shape = list(arr.shape)
        shape[axis] //= N_CHIPS
        quarters.append(jax.ShapeDtypeStruct((N_CHIPS - 1, *shape), arr.dtype))

    def copies(ins, outs, sems):
        x, y, c, chips = _position()
        out = []
        for t in range(n):
            size = quarters[t].shape[1 + chip_axes[t]]
            for j, chip in enumerate(chips):
                piece = _cut(ins[t], chip_axes[t], pl.multiple_of((2 * chip[0] + chip[1]) * size, size), size)
                out.append(_remote(piece, outs[t].at[j], sems[0].at[t, j], sems[1].at[t, j], (*chip, c)))
        return out

    def start(ins, outs, sems):
        for cp in copies(ins, outs, sems):
            cp.start()

    def finish(ins, outs, sems):
        for cp in copies(ins, outs, sems):
            cp.wait()

    return _Rider(operands=list(partials), out_shape=quarters, aliases={}, scratch=[pltpu.SemaphoreType.DMA((n, N_CHIPS - 1))] * 2,
                  start=start, finish=finish)


def _place_block(shard, chip_axis, place, dtype, *, name):
    depth, rows, cols = shard.shape
    rows_t = _row_tile(rows, cols * 4 * 4, 16 * 1024 * 1024)
    steps = rows // rows_t
    shape = list(shard.shape)
    shape[chip_axis] *= N_CHIPS
    if chip_axis == 1:
        out_spec = pl.BlockSpec((None, rows_t, cols), lambda l, i, p: (l, p[1] * steps + i, 0))
    else:
        out_spec = pl.BlockSpec((None, rows_t, cols), lambda l, i, p: (l, i, p[1]))

    def body(place_ref, src_ref, out_ref):
        out_ref[...] = src_ref[...].astype(dtype)

    return pl.pallas_call(
        body, name=name, out_shape=jax.ShapeDtypeStruct(tuple(shape), dtype),
        grid_spec=pltpu.PrefetchScalarGridSpec(num_scalar_prefetch=1, grid=(depth, steps),
                                               in_specs=[pl.BlockSpec((None, rows_t, cols), lambda l, i, p: (l, i, 0))],
                                               out_specs=out_spec),
        compiler_params=_cparams(("arbitrary", "arbitrary")),
    )(place, shard)


def _pair_join(blocks, core_axes):
    n = len(blocks)

    def body(*refs):
        dst = refs[n:2 * n]
        send_sem, recv_sem = refs[2 * n:]
        x, y, c, _ = _position()
        copies = []
        for t in range(n):
            size = blocks[t].shape[core_axes[t]] // 2
            mine = _cut(dst[t], core_axes[t], pl.multiple_of(c * size, size), size)
            cp = _remote(mine, mine, send_sem.at[t], recv_sem.at[t], (x, y, 1 - c))
            cp.start()
            copies.append(cp)
        for t, cp in enumerate(copies):
            size = blocks[t].shape[core_axes[t]] // 2
            other = _cut(dst[t], core_axes[t], pl.multiple_of((1 - c) * size, size), size)
            cp.wait_send()
            _remote(other, other, send_sem.at[t], recv_sem.at[t], (x, y, 1 - c)).wait_recv()

    return _comm_call(body, "pair_join", blocks, [jax.ShapeDtypeStruct(b.shape, b.dtype) for b in blocks],
                      [pltpu.SemaphoreType.DMA((n,))] * 2, alias=True)


def _gather_small(vec):
    n_dev = 2 * N_CHIPS

    def body(src, dst, send_sem, recv_sem, local_sem):
        x, y, c, _ = _position()
        flip = lambda v, bit: 1 - v if bit else v
        mine = pltpu.make_async_copy(src, dst.at[4 * x + 2 * y + c], local_sem)
        mine.start()
        copies = []
        for mask in range(1, n_dev):
            peer = (flip(x, mask & 4), flip(y, mask & 2), flip(c, mask & 1))
            cp = _remote(src, dst.at[4 * x + 2 * y + c], send_sem.at[mask - 1], recv_sem.at[mask - 1], peer)
            cp.start()
            copies.append((cp, peer))
        for mask, (cp, peer) in enumerate(copies, start=1):
            theirs = dst.at[4 * peer[0] + 2 * peer[1] + peer[2]]
            _remote(src, theirs, send_sem.at[mask - 1], recv_sem.at[mask - 1], peer).wait_recv()
        for cp, _ in copies:
            cp.wait_send()
        mine.wait()

    out = [jax.ShapeDtypeStruct((n_dev, *vec.shape), vec.dtype)]
    return _comm_call(body, "gather_small", [vec], out, [pltpu.SemaphoreType.DMA((n_dev - 1,))] * 2 + [pltpu.SemaphoreType.DMA(())])[0]


def _row_tile(rows, row_bytes, budget):
    tile = rows
    for cand in (512, 256, 128, 64, 32, 16, 8):
        if rows % cand == 0:
            tile = cand
            if cand * row_bytes <= budget:
                break
    return tile


def _pair_sum_layer(part, got, core_axis, place, *, name):
    rows, cols = got.shape
    rows_t = _row_tile(rows, cols * 4 * 6, 16 * 1024 * 1024)
    steps = rows // rows_t
    if core_axis == 0:
        part_spec = pl.BlockSpec((rows_t, cols), lambda i, p: (p[0] * steps + i, 0))
    else:
        part_spec = pl.BlockSpec((rows_t, cols), lambda i, p: (i, p[0]))
    own_spec = pl.BlockSpec((rows_t, cols), lambda i, p: (i, 0))

    def body(place_ref, part_ref, got_ref, out_ref):
        out_ref[...] = (part_ref[...] + got_ref[...]).astype(BF16)

    return pl.pallas_call(
        body, name=name, out_shape=jax.ShapeDtypeStruct(got.shape, BF16),
        grid_spec=pltpu.PrefetchScalarGridSpec(num_scalar_prefetch=1, grid=(steps,), in_specs=[part_spec, own_spec], out_specs=own_spec),
        compiler_params=_cparams(("arbitrary",)),
    )(place, part, got)


def _chip_sum_layer(partial, got, blocks, layer, depth, chip_axis, core_axis, place, *, name):
    _, rows, cols = got.shape
    rows_t = _row_tile(rows, cols * 4 * 10, 24 * 1024 * 1024)
    steps = rows // rows_t
    if chip_axis == 0:
        own_spec = pl.BlockSpec((rows_t, cols), lambda i, p: (p[1] * steps + i, 0))
    else:
        own_spec = pl.BlockSpec((rows_t, cols), lambda i, p: (i, p[1]))
    got_spec = pl.BlockSpec((N_CHIPS - 1, rows_t, cols), lambda i, p: (0, i, 0))
    shape = [depth, rows, cols]
    shape[1 + core_axis] *= 2
    if core_axis == 0:
        out_spec = pl.BlockSpec((None, rows_t, cols), lambda i, p: (layer, p[0] * steps + i, 0))
    else:
        out_spec = pl.BlockSpec((None, rows_t, cols), lambda i, p: (layer, i, p[0]))

    def body(place_ref, own_ref, got_ref, *rest):
        out_ref = rest[-1]
        up = lambda val: val.astype(F32)
        out_ref[...] = ((up(own_ref[...]) + up(got_ref[0])) + up(got_ref[1])) + up(got_ref[2])

    in_specs, operands, aliases = [own_spec, got_spec], [place, partial, got], {}
    if blocks is not None:
        in_specs.append(pl.BlockSpec(memory_space=pl.ANY))
        operands.append(blocks)
        aliases = {3: 0}
    return pl.pallas_call(
        body, name=name, out_shape=jax.ShapeDtypeStruct(tuple(shape), F32),
        grid_spec=pltpu.PrefetchScalarGridSpec(num_scalar_prefetch=1, grid=(steps,), in_specs=in_specs, out_specs=out_spec),
        input_output_aliases=aliases, compiler_params=_cparams(("arbitrary",)),
    )(*operands)


def _sum_devices(stack, *, name):
    def body(src_ref, out_ref):
        total = src_ref[0]
        for d in range(1, stack.shape[0]):
            total = total + src_ref[d]
        out_ref[...] = total

    return pl.pallas_call(body, name=name, out_shape=jax.ShapeDtypeStruct(stack.shape[1:], F32))(stack)


def kernel(x, w_in, b_in, conv_w, conv_b, conv_ln_g, conv_ln_b, w_att_proj, w_conv_proj, b_conv_proj, w_out, ln_g, ln_b, loss_target, m_w_in, m_b_in, m_conv_w, m_conv_b, m_conv_ln_g, m_conv_ln_b, m_w_att_proj, m_w_conv_proj, m_b_conv_proj, m_w_out, m_ln_g, m_ln_b, v_w_in, v_b_in, v_conv_w, v_conv_b, v_conv_ln_g, v_conv_ln_b, v_w_att_proj, v_w_conv_proj, v_b_conv_proj, v_w_out, v_ln_g, v_ln_b):
    depth = w_in.shape[0]
    d_model = x.shape[-1]
    half = d_model // 2
    chip = 2 * lax.axis_index("x") + lax.axis_index("y")
    place = jnp.stack([lax.axis_index("c"), chip]).astype(jnp.int32)
    vec3 = lambda v: v.reshape(depth, 1, -1)

    taps = jnp.pad(conv_w, ((0, 0), (0, CONV_PAD - CONV_WIDTH), (0, 0)))
    gathered = [("w_in", w_in, 2, BF16), ("w_att_proj", w_att_proj, 2, BF16), ("w_conv_proj", w_conv_proj, 2, BF16),
                ("w_out", w_out, 1, BF16), ("conv_w", taps, 2, F32)]
    wholes = [_place_block(arr, axis, place, dtype, name="place_" + n) for n, arr, axis, dtype in gathered]
    loss, grad_x, grads, blocks = _train_pass(x[0], loss_target[0], wholes[:4], wholes[4], b_in, vec3(conv_b), vec3(conv_ln_g),
                                              vec3(conv_ln_b), vec3(b_conv_proj), vec3(ln_g), vec3(ln_b), place)
    loss = lax.psum(loss[0, 0], ("x", "y", "c"))
    reduced = dict(zip(MATRICES, _pair_join(blocks, [axis + 1 for axis in MATRIX_CORE_AXIS])))

    small = ["b_in", "conv_b", "conv_ln_g", "conv_ln_b", "b_conv_proj", "ln_g", "ln_b"]
    packed = jnp.stack([jnp.concatenate([grads[l][n] for n in small] + [grads[l]["conv_w"].reshape(-1)]) for l in range(depth)])
    total = _sum_devices(_gather_small(packed), name="sum_devices")
    offset = 0
    for n in small:
        width = grads[0][n].shape[0]
        reduced[n] = total[:, offset:offset + width]
        offset += width
    taps = total[:, offset:].reshape(depth, CONV_WIDTH, half)
    reduced["conv_w"] = lax.dynamic_slice_in_dim(taps, chip * conv_w.shape[2], conv_w.shape[2], axis=2)

    names = ["w_in", "b_in", "conv_w", "conv_b", "conv_ln_g", "conv_ln_b", "w_att_proj", "w_conv_proj", "b_conv_proj", "w_out", "ln_g", "ln_b"]
    weights = dict(zip(names, (w_in, b_in, conv_w, conv_b, conv_ln_g, conv_ln_b, w_att_proj, w_conv_proj, b_conv_proj, w_out, ln_g, ln_b)))
    first = dict(zip(names, (m_w_in, m_b_in, m_conv_w, m_conv_b, m_conv_ln_g, m_conv_ln_b, m_w_att_proj, m_w_conv_proj, m_b_conv_proj, m_w_out, m_ln_g, m_ln_b)))
    second = dict(zip(names, (v_w_in, v_b_in, v_conv_w, v_conv_b, v_conv_ln_g, v_conv_ln_b, v_w_att_proj, v_w_conv_proj, v_b_conv_proj, v_w_out, v_ln_g, v_ln_b)))
    delta, new_m, new_v = {}, {}, {}
    for n in names:
        shape = weights[n].shape
        flat = lambda arr: arr.reshape(-1, shape[-1])
        d, m, v = _adamw(flat(weights[n]), flat(reduced[n]), flat(first[n]), flat(second[n]), name="adamw_" + n)
        delta[n], new_m[n], new_v[n] = d.reshape(shape), m.reshape(shape), v.reshape(shape)
    return (loss, grad_x[None], *[reduced[n].reshape(weights[n].shape) for n in names], *[delta[n] for n in names],
            *[new_m[n] for n in names], *[new_v[n] for n in names])
```

```python
import functools
from typing import Callable, NamedTuple

import jax
import jax.numpy as jnp
from jax import lax
from jax.experimental import pallas as pl
from jax.experimental.pallas import tpu as pltpu

F32 = jnp.float32
BF16 = jnp.bfloat16

HEAD_DIM = 64
LANES = 128
CONV_WIDTH = 31
CONV_PAD = 32
SUBLANES = 8
LN_EPS = 1e-5
DEPTH = 4
DEEPNORM_ALPHA = (2 * DEPTH) ** 0.25
ATT_SCALE = HEAD_DIM ** -0.5
ATT_TILE = 128
ATT_DEAD = -104.0
ATT_GROUP = 2
ATT_GROUP_FWD = 4
ATT_FILL = -1e30

ADAM_LR = 0.001
ADAM_B1 = 0.9
ADAM_B2 = 0.999
ADAM_EPS = 1e-08
ADAM_WD = 0.01
ADAM_STEP = 10

VMEM_LIMIT = 56 * 1024 * 1024

N_CHIPS = 4


def _cparams(sem):
    return pltpu.CompilerParams(dimension_semantics=sem, vmem_limit_bytes=VMEM_LIMIT)


def _sigmoid(x):
    return 1.0 / (1.0 + jnp.exp(-x))


class _Rider(NamedTuple):
    operands: list
    out_shape: list
    aliases: dict
    scratch: list
    start: Callable
    finish: Callable


def _host_call(body, rider, *, name, grid, in_specs, out_specs, out_shape, scratch, operands, semantics, host_aliases=None):
    n_in, n_out = len(in_specs), len(out_specs)
    aliases = dict(host_aliases or {})
    if rider is not None:
        r_in, r_out = len(rider.operands), len(rider.out_shape)
        host_body = body

        def body(*refs):
            base = n_in + r_in
            ins, rins = refs[:n_in], refs[n_in:base]
            outs, routs = refs[base:base + n_out], refs[base + n_out:base + n_out + r_out]
            rest = refs[base + n_out + r_out:]
            split = len(rest) - len(rider.scratch)
            ids = [pl.program_id(axis) for axis in range(len(grid))]
            first = functools.reduce(jnp.logical_and, [i == 0 for i in ids])
            last = functools.reduce(jnp.logical_and, [i == g - 1 for i, g in zip(ids, grid)])

            @pl.when(first)
            def _():
                rider.start(rins, routs, rest[split:])

            host_body(*ins, *outs, *rest[:split])

            @pl.when(last)
            def _():
                rider.finish(rins, routs, rest[split:])

        hbm = pl.BlockSpec(memory_space=pl.ANY)
        in_specs = list(in_specs) + [hbm] * r_in
        out_specs = list(out_specs) + [hbm] * r_out
        out_shape = list(out_shape) + list(rider.out_shape)
        scratch = list(scratch) + list(rider.scratch)
        operands = list(operands) + list(rider.operands)
        aliases.update({n_in + i: n_out + o for i, o in rider.aliases.items()})
    res = pl.pallas_call(
        body, name=name, grid=grid, in_specs=list(in_specs), out_specs=list(out_specs), out_shape=list(out_shape),
        scratch_shapes=list(scratch), input_output_aliases=aliases, compiler_params=_cparams(semantics),
    )(*operands)
    return list(res[:n_out]), list(res[n_out:])


def _lone_call(rider, *, name):
    r_in = len(rider.operands)

    def body(*refs):
        ins, outs, sems = refs[:r_in], refs[r_in:r_in + len(rider.out_shape)], refs[r_in + len(rider.out_shape):]
        rider.start(ins, outs, sems)
        rider.finish(ins, outs, sems)

    hbm = pl.BlockSpec(memory_space=pl.ANY)
    return list(pl.pallas_call(
        body, name=name, in_specs=[hbm] * r_in, out_specs=[hbm] * len(rider.out_shape), out_shape=list(rider.out_shape),
        scratch_shapes=list(rider.scratch), input_output_aliases=dict(rider.aliases),
    )(*rider.operands))


def _fit(tile, dim):
    assert dim % LANES == 0
    tile = min(tile, dim) // LANES * LANES
    while dim % tile:
        tile -= LANES
    return tile


_DIMS = {"nn": ((1,), (0,)), "nt": ((1,), (1,)), "tn": ((0,), (0,))}


def _matmul(a, b, *, mode, name, layer=None, bias=None, add=None, colsum=False, out_dtype=F32, tm=1024, tn=512, tk=1024,
            rider=None):
    b_shape = b.shape if layer is None else b.shape[1:]
    if mode == "nn":
        (m, k), (k2, n) = a.shape, b_shape
    elif mode == "nt":
        (m, k), (n, k2) = a.shape, b_shape
    else:
        (k, m), (k2, n) = a.shape, b_shape
    assert k == k2
    tm, tn, tk = _fit(tm, m), _fit(tn, n), _fit(tk, k)
    gm, gn, nk = m // tm, n // tn, k // tk

    a_spec = pl.BlockSpec((tk, tm), lambda i, j, kk: (kk, i)) if mode == "tn" else pl.BlockSpec((tm, tk), lambda i, j, kk: (i, kk))
    if layer is None:
        b_spec = pl.BlockSpec((tn, tk), lambda i, j, kk: (j, kk)) if mode == "nt" else pl.BlockSpec((tk, tn), lambda i, j, kk: (kk, j))
    elif mode == "nt":
        b_spec = pl.BlockSpec((None, tn, tk), lambda i, j, kk: (layer, j, kk))
    else:
        b_spec = pl.BlockSpec((None, tk, tn), lambda i, j, kk: (layer, kk, j))
    in_specs, operands = [a_spec, b_spec], [a, b]
    if bias is not None:
        in_specs.append(pl.BlockSpec((1, tn), lambda i, j, kk: (0, j)))
        operands.append(bias)
    if add is not None:
        in_specs.append(pl.BlockSpec((tm, tn), lambda i, j, kk: (i, j)))
        operands.append(add)
    out_shape = [jax.ShapeDtypeStruct((m, n), out_dtype)]
    out_specs = [pl.BlockSpec((tm, tn), lambda i, j, kk: (i, j))]
    scratch = [pltpu.VMEM((tm, tn), F32)] if nk > 1 else []
    if colsum:
        assert mode == "tn"
        out_shape.append(jax.ShapeDtypeStruct((gm, 1, n), F32))
        out_specs.append(pl.BlockSpec((1, 1, tn), lambda i, j, kk: (i, 0, j)))
        if nk > 1:
            scratch.append(pltpu.VMEM((1, tn), F32))
    has_bias, has_add = bias is not None, add is not None

    def body(*refs):
        refs = list(refs)
        a_ref, b_ref = refs[0], refs[1]
        pos = 2
        bias_ref = add_ref = None
        if has_bias:
            bias_ref = refs[pos]
            pos += 1
        if has_add:
            add_ref = refs[pos]
            pos += 1
        o_ref = refs[pos]
        pos += 1
        cs_ref = None
        if colsum:
            cs_ref = refs[pos]
            pos += 1

        def finish(out, sums):
            if has_bias:
                out = out + bias_ref[...]
            if has_add:
                out = out + add_ref[...]
            o_ref[...] = out.astype(out_dtype)
            if colsum:
                cs_ref[0] = sums

        bv = b_ref[...]
        prod = lax.dot_general(a_ref[...].astype(BF16), bv.astype(BF16), (_DIMS[mode], ((), ())), preferred_element_type=F32)
        sums = jnp.sum(bv.astype(F32), axis=0, keepdims=True) if colsum else None
        if nk == 1:
            finish(prod, sums)
            return
        acc_ref = refs[pos]
        cs_acc = refs[pos + 1] if colsum else None
        kk = pl.program_id(2)

        @pl.when(kk == 0)
        def _():
            acc_ref[...] = jnp.zeros_like(acc_ref)
            if colsum:
                cs_acc[...] = jnp.zeros_like(cs_acc)

        acc_ref[...] += prod
        if colsum:
            cs_acc[...] += sums

        @pl.when(kk == nk - 1)
        def _():
            finish(acc_ref[...], cs_acc[...] if colsum else None)

    res, rode = _host_call(body, rider, name=name, grid=(gm, gn, nk), in_specs=in_specs, out_specs=out_specs, out_shape=out_shape,
                           scratch=scratch, operands=operands, semantics=("arbitrary", "arbitrary", "arbitrary"))
    out = (res[0], res[1][0]) if colsum else res[0]
    return out if rider is None else (out, rode)


def _scan_matrices():
    t = ATT_TILE
    r = lax.broadcasted_iota(jnp.int32, (t, t), 0)
    c = lax.broadcasted_iota(jnp.int32, (t, t), 1)
    ones = jnp.ones((t, t), F32)
    suffix = jnp.concatenate([(r > c).astype(F32), ones], axis=1)
    prefix = jnp.concatenate([(r < c).astype(F32), ones], axis=1)
    stack = lambda mat: jnp.concatenate([mat, mat], axis=0).astype(BF16)
    return stack(suffix), stack(prefix)


def _split_halves(val):
    hi = val.astype(BF16)
    lo = (val - hi.astype(F32)).astype(BF16)
    return jnp.concatenate([hi, lo], axis=1)


def _split_scan(val, mat_ref):
    return jnp.dot(_split_halves(val), mat_ref[...], preferred_element_type=F32)


def _pair_scores(q, k_lo, k_hi, masked):
    t = ATT_TILE
    z2 = lax.dot_general(q, jnp.concatenate([k_lo, k_hi], axis=0), (((1,), (1,)), ((), ())), preferred_element_type=F32)
    out = []
    for h in range(2):
        z = z2[:, h * t:(h + 1) * t]
        sp = jnp.log(1.0 + jnp.exp(-jnp.abs(z)))
        f = jnp.minimum(-z, 0.0) - sp
        a = f + z
        if masked:
            causal = lax.broadcasted_iota(jnp.int32, (t, t), 1) < lax.broadcasted_iota(jnp.int32, (t, t), 0)
            f = jnp.where(causal, f, 0.0)
        out.append((_split_halves(f), a))
    return out


def _any_alive(runs):
    top = functools.reduce(jnp.maximum, [run for per_head in runs for run in per_head])
    return (jnp.max(top) > ATT_DEAD).astype(jnp.int32)


def _head_copies(seq, src_ref, scale, lo_ref, hi_ref, plain_ref):
    chunk = min(256, seq)
    low = lax.broadcasted_iota(jnp.int32, (chunk, LANES), 1) < HEAD_DIM

    def step(r, carry):
        rows = pl.ds(pl.multiple_of(r * chunk, chunk), chunk)
        val = src_ref[rows, :]
        if scale != 1.0:
            val = val * scale
        if lo_ref is not None:
            lo_ref[rows, :] = jnp.where(low, val, 0.0).astype(BF16)
            hi_ref[rows, :] = jnp.where(low, 0.0, val).astype(BF16)
        if plain_ref is not None:
            plain_ref[rows, :] = val.astype(BF16)
        return carry

    lax.fori_loop(0, seq // chunk, step, 0)


def _attn_fwd(u, scan_suffix, *, seq, d_att, name, rider=None):
    t = ATT_TILE
    nq = seq // t
    pairs = d_att // LANES
    grp = ATT_GROUP_FWD
    assert nq % grp == 0

    def body(q_ref, k_ref, v_ref, um_ref, o_ref, qq, k0, k1, v0, v1, f2_s, a_s, lg_s, tot_s, run_s, acc_s):
        _head_copies(seq, q_ref, ATT_SCALE, None, None, qq)
        _head_copies(seq, k_ref, 1.0, k0, k1, None)
        _head_copies(seq, v_ref, 1.0, v0, v1, None)

        def group(gi, carry):
            qb0 = gi * grp
            qrows = [pl.ds(pl.multiple_of((qb0 + g) * t, t), t) for g in range(grp)]
            qv = [qq[qrows[g], :] for g in range(grp)]

            chains = [(h, g) for g in range(grp) for h in range(2)]

            def key_rows(g, i):
                return pl.ds(pl.multiple_of(jnp.maximum(qb0 + g - i, 0) * t, t), t)

            def stage1(i, masked):
                out = []
                for g in range(grp):
                    krows = key_rows(g, i)
                    out += _pair_scores(qv[g], k0[krows, :], k1[krows, :], masked)
                return out

            def stage2(halves, a, masked):
                scan = jnp.dot(halves, um_ref[...], preferred_element_type=F32)
                logit = a + scan[:, :t]
                if masked:
                    causal = lax.broadcasted_iota(jnp.int32, (t, t), 1) < lax.broadcasted_iota(jnp.int32, (t, t), 0)
                    logit = jnp.where(causal, logit, ATT_FILL)
                return logit, scan[:, t:]

            def put(halves_a=None, logit_total=None):
                for c in range(len(chains)):
                    if halves_a is not None:
                        f2_s[c], a_s[c] = halves_a[c]
                    if logit_total is not None:
                        lg_s[c], tot_s[c] = logit_total[c]

            first = stage1(0, True)
            put(halves_a=stage1(1, False), logit_total=[stage2(f2, a, True) for f2, a in first])
            for c in range(len(chains)):
                run_s[c] = jnp.zeros((t, t), F32)
            for g in range(grp):
                acc_s[g] = jnp.zeros((t, LANES), F32)

            def wbody(st):
                i = st[0]
                held = [(f2_s[c], a_s[c]) for c in range(len(chains))]
                logits = [lg_s[c] for c in range(len(chains))]
                totals = [tot_s[c] for c in range(len(chains))]
                runs = [run_s[c] for c in range(len(chains))]
                accs = [acc_s[g] for g in range(grp)]
                for g in range(grp):
                    krows = key_rows(g, i)
                    gone = jnp.where(qb0 + g - i >= 0, 0.0, ATT_FILL)
                    weights = []
                    for c in (2 * g, 2 * g + 1):
                        run = runs[c] + gone
                        weights.append(jnp.exp(logits[c] + run).astype(BF16))
                        runs[c] = run + totals[c]
                    accs[g] = accs[g] + jnp.dot(jnp.concatenate(weights, axis=1), jnp.concatenate([v0[krows, :], v1[krows, :]], axis=0),
                                                preferred_element_type=F32)
                ahead2 = [stage2(f2, a, False) for f2, a in held]
                ahead1 = stage1(i + 2, False)
                put(halves_a=ahead1, logit_total=ahead2)
                for c in range(len(chains)):
                    run_s[c] = runs[c]
                for g in range(grp):
                    acc_s[g] = accs[g]
                more = jnp.logical_and(i + 1 <= qb0 + grp - 1, _any_alive([runs]) > 0)
                return i + 1, more.astype(jnp.int32)

            lax.while_loop(lambda st: st[1] > 0, wbody, (jnp.int32(0), jnp.int32(1)))
            for g in range(grp):
                o_ref[qrows[g], :] = acc_s[g]
            return carry

        lax.fori_loop(0, nq // grp, group, 0)

    blk = lambda base: pl.BlockSpec((seq, LANES), lambda j, base=base: (0, base + j))
    res, rode = _host_call(
        body, rider, name=name, grid=(pairs,),
        in_specs=[blk(0), blk(pairs), blk(2 * pairs), pl.BlockSpec((2 * t, 2 * t), lambda j: (0, 0))],
        out_specs=[pl.BlockSpec((seq, LANES), lambda j: (0, j))],
        out_shape=[jax.ShapeDtypeStruct((seq, d_att), F32)],
        scratch=[pltpu.VMEM((seq, LANES), BF16)] * 5 + [pltpu.VMEM((2 * grp, t, 2 * t), BF16)]
        + [pltpu.VMEM((2 * grp, t, t), F32)] * 4 + [pltpu.VMEM((grp, t, LANES), F32)],
        operands=[u, u, u, scan_suffix], semantics=("arbitrary",))
    return res[0] if rider is None else (res[0], rode)


def _columns_copy(stage_ref, du_ref, rows, col, sem):
    width = stage_ref.shape[-1]
    cols = pl.ds(pl.multiple_of(col, LANES), width)
    return pltpu.make_async_copy(stage_ref, du_ref.at[slice(None) if rows is None else rows, cols], sem)


def _attn_bwd(u, d_att_out, du, scan_suffix, scan_prefix, *, seq, d_att, name, rider=None):
    t = ATT_TILE
    nq = seq // t
    pairs = d_att // LANES
    grp = ATT_GROUP
    assert nq % grp == 0

    def body(q_ref, k_ref, v_ref, do_ref, um_ref, pm_ref, du_old, du_ref,
             qq, q0, q1, k0, k1, v0, v1, dd, do0, do1, dq_acc, dk_acc, dv_acc, g_st, b_st,
             f2_s, a_s, lg_s, tot_s, dw_s, run_s, p_s, pt_s, grun_s, dq_s, dq_o, dk_o, dv_o, out_sem):
        _head_copies(seq, q_ref, ATT_SCALE, q0, q1, qq)
        _head_copies(seq, k_ref, 1.0, k0, k1, None)
        _head_copies(seq, v_ref, 1.0, v0, v1, None)
        _head_copies(seq, do_ref, 1.0, do0, do1, dd)
        dk_acc[...] = jnp.zeros_like(dk_acc)
        dv_acc[...] = jnp.zeros_like(dv_acc)

        tn = (((0,), (0,)), ((), ()))
        nt = (((1,), (1,)), ((), ()))

        def stacked(lo_ref, hi_ref, rows):
            return jnp.concatenate([lo_ref[rows, :], hi_ref[rows, :]], axis=0)

        def group(gi, carry):
            qb0 = gi * grp
            qrows = [pl.ds(pl.multiple_of((qb0 + g) * t, t), t) for g in range(grp)]
            qv = [qq[qrows[g], :] for g in range(grp)]
            dov = [dd[qrows[g], :] for g in range(grp)]
            q_heads = [stacked(q0, q1, qrows[g]) for g in range(grp)]
            do_heads = [stacked(do0, do1, qrows[g]) for g in range(grp)]

            def key_rows(g, i):
                return pl.ds(pl.multiple_of(jnp.maximum(qb0 + g - i, 0) * t, t), t)

            chains = [(h, g) for g in range(grp) for h in range(2)]
            every = range(len(chains))

            def stage1(i, masked):
                out = []
                for g in range(grp):
                    krows = key_rows(g, i)
                    out += _pair_scores(qv[g], k0[krows, :], k1[krows, :], masked)
                return out

            def stage2(i, held, masked):
                out = []
                for g in range(grp):
                    dw2 = lax.dot_general(dov[g], stacked(v0, v1, key_rows(g, i)), nt, preferred_element_type=F32)
                    for h in range(2):
                        c = 2 * g + h
                        halves, a = held[c]
                        scan = jnp.dot(halves, um_ref[...], preferred_element_type=F32)
                        logit = a + scan[:, :t]
                        if masked:
                            causal = lax.broadcasted_iota(jnp.int32, (t, t), 1) < lax.broadcasted_iota(jnp.int32, (t, t), 0)
                            logit = jnp.where(causal, logit, ATT_FILL)
                        b_st[c, i] = jnp.exp(a)
                        out.append((logit, scan[:, t:], dw2[:, h * t:(h + 1) * t]))
                return out

            def put(held=None, ready=None):
                for c in every:
                    if held is not None:
                        f2_s[c], a_s[c] = held[c]
                    if ready is not None:
                        lg_s[c], tot_s[c], dw_s[c] = ready[c]

            put(held=stage1(1, False), ready=stage2(0, stage1(0, True), True))
            for c in every:
                run_s[c] = jnp.zeros((t, t), F32)

            def wbody(st):
                i = st[0]
                held = [(f2_s[c], a_s[c]) for c in every]
                ready = [(lg_s[c], tot_s[c], dw_s[c]) for c in every]
                runs = [run_s[c] for c in every]
                for g in range(grp):
                    gone = jnp.where(qb0 + g - i >= 0, 0.0, ATT_FILL)
                    weights = []
                    for c in (2 * g, 2 * g + 1):
                        logit, total, dw = ready[c]
                        run = runs[c] + gone
                        w = jnp.exp(logit + run)
                        g_st[c, i] = w * dw
                        weights.append(w.astype(BF16))
                        runs[c] = run + total
                    dv_acc[key_rows(g, i), :] += lax.dot_general(jnp.concatenate(weights, axis=0), do_heads[g], tn,
                                                                 preferred_element_type=F32)
                ahead2 = stage2(i + 1, held, False)
                ahead1 = stage1(i + 2, False)
                put(held=ahead1, ready=ahead2)
                for c in every:
                    run_s[c] = runs[c]
                more = jnp.logical_and(i + 1 <= qb0 + grp - 1, _any_alive([runs]) > 0)
                return i + 1, more.astype(jnp.int32)

            steps = lax.while_loop(lambda st: st[1] > 0, wbody, (jnp.int32(0), jnp.int32(1)))[0]

            def prefix(i):
                out = []
                for c in every:
                    scan = _split_scan(g_st[c, i], pm_ref)
                    out.append((scan[:, :t], scan[:, t:]))
                return out

            def back(i, masked):
                sums = [(p_s[c], pt_s[c]) for c in every]
                gruns = [grun_s[c] for c in every]
                dqs = [dq_s[g] for g in range(grp)]
                for g in range(grp):
                    krows = key_rows(g, i)
                    dzs = []
                    for c in (2 * g, 2 * g + 1):
                        gt = g_st[c, i]
                        dz = gt - b_st[c, i] * (gt + gruns[c] + sums[c][0])
                        if masked:
                            causal = lax.broadcasted_iota(jnp.int32, (t, t), 1) < lax.broadcasted_iota(jnp.int32, (t, t), 0)
                            dz = jnp.where(causal, dz, 0.0)
                        dzs.append(dz.astype(BF16))
                        gruns[c] = gruns[c] + sums[c][1]
                    dqs[g] = dqs[g] + jnp.dot(jnp.concatenate(dzs, axis=1), stacked(k0, k1, krows), preferred_element_type=F32)
                    dk_acc[krows, :] += lax.dot_general(jnp.concatenate(dzs, axis=0), q_heads[g], tn, preferred_element_type=F32)
                return gruns, dqs

            def keep(sums=None, gruns=None, dqs=None):
                for c in every:
                    if sums is not None:
                        p_s[c], pt_s[c] = sums[c]
                    if gruns is not None:
                        grun_s[c] = gruns[c]
                if dqs is not None:
                    for g in range(grp):
                        dq_s[g] = dqs[g]

            keep(sums=prefix(steps - 1), gruns=[jnp.zeros((t, t), F32)] * len(chains), dqs=[jnp.zeros((t, LANES), F32)] * grp)

            def bbody(j, carry2):
                i = steps - 1 - j
                gruns, dqs = back(i, False)
                keep(sums=prefix(i - 1), gruns=gruns, dqs=dqs)
                return carry2

            lax.fori_loop(0, steps - 1, bbody, 0)
            _, dqs = back(0, True)
            for g in range(grp):
                dq_acc[qrows[g], :] = dqs[g]
            return carry

        lax.fori_loop(0, nq // grp, group, 0)

        pair = pl.program_id(0)
        stages = (dq_o, dk_o, dv_o)

        def copies(j):
            return [_columns_copy(stage, du_ref, None, k * d_att + j * LANES, out_sem.at[k]) for k, stage in enumerate(stages)]

        @pl.when(pair > 0)
        def _():
            for cp in copies(pair - 1):
                cp.wait()

        chunk = min(256, seq)

        def emit(r, carry):
            rows = pl.ds(pl.multiple_of(r * chunk, chunk), chunk)
            dq_o[rows, :] = (dq_acc[rows, :] * ATT_SCALE).astype(BF16)
            dk_o[rows, :] = dk_acc[rows, :].astype(BF16)
            dv_o[rows, :] = dv_acc[rows, :].astype(BF16)
            return carry

        lax.fori_loop(0, seq // chunk, emit, 0)
        for cp in copies(pair):
            cp.start()

        @pl.when(pair == pairs - 1)
        def _():
            for cp in copies(pair):
                cp.wait()

    blk = lambda base: pl.BlockSpec((seq, LANES), lambda j, base=base: (0, base + j), pipeline_mode=pl.Buffered(1))
    mat = pl.BlockSpec((2 * t, 2 * t), lambda j: (0, 0))
    hbm = pl.BlockSpec(memory_space=pl.ANY)
    res, rode = _host_call(
        body, rider, name=name, grid=(pairs,),
        in_specs=[blk(0), blk(pairs), blk(2 * pairs), blk(0), mat, mat, hbm],
        out_specs=[hbm], out_shape=[jax.ShapeDtypeStruct(du.shape, du.dtype)], host_aliases={6: 0},
        scratch=[pltpu.VMEM((seq, LANES), BF16)] * 10 + [pltpu.VMEM((seq, LANES), F32)] * 3
        + [pltpu.VMEM((2 * grp, nq + 1, t, t), F32)] * 2 + [pltpu.VMEM((2 * grp, t, 2 * t), BF16)]
        + [pltpu.VMEM((2 * grp, t, t), F32)] * 8 + [pltpu.VMEM((grp, t, LANES), F32)]
        + [pltpu.VMEM((seq, LANES), BF16)] * 3 + [pltpu.SemaphoreType.DMA((3,))],
        operands=[u, u, u, d_att_out, scan_suffix, scan_prefix, du], semantics=("arbitrary",))
    return res[0] if rider is None else (res[0], rode)


CONV_ROWS = 256


def _shifted(window, residue, rows):
    total = rows + CONV_PAD
    return window if residue == 0 else pltpu.roll(window, total - residue, 0)


def _glu_to_pad(seq, a_ref, b_ref, pad_ref):
    chunk = min(CONV_ROWS, seq)
    pad_ref[pl.ds(0, CONV_PAD), :] = jnp.zeros((CONV_PAD, LANES), F32)

    def step(r, carry):
        rows = pl.ds(pl.multiple_of(r * chunk, chunk), chunk)
        pad_ref[pl.ds(pl.multiple_of(r * chunk + CONV_PAD, SUBLANES), chunk), :] = a_ref[rows, :] * _sigmoid(b_ref[rows, :])
        return carry

    lax.fori_loop(0, seq // chunk, step, 0)


def _conv_fwd(u, conv_w, conv_b, *, layer, seq, d_conv, col_a, col_b, name, rider=None):
    blocks = d_conv // LANES
    rows_t = min(CONV_ROWS, seq)
    shift0 = CONV_PAD - (CONV_WIDTH - 1)

    def body(a_ref, b_ref, w_ref, bias_ref, o_ref, pad_ref):
        _glu_to_pad(seq, a_ref, b_ref, pad_ref)

        def step(r, carry):
            base = pl.multiple_of(r * rows_t, rows_t)
            window = pad_ref[pl.ds(base, rows_t + CONV_PAD), :]
            acc = jnp.zeros((rows_t, LANES), F32) + bias_ref[...]
            for residue in range(SUBLANES):
                moved = _shifted(window, residue, rows_t)
                for tap in range(CONV_WIDTH):
                    if (shift0 + tap) % SUBLANES == residue:
                        lo = (shift0 + tap) - residue
                        acc = acc + w_ref[tap:tap + 1, :] * moved[lo:lo + rows_t, :]
            o_ref[pl.ds(base, rows_t), :] = acc
            return carry

        lax.fori_loop(0, seq // rows_t, step, 0)

    res, rode = _host_call(
        body, rider, name=name, grid=(blocks,),
        in_specs=[pl.BlockSpec((seq, LANES), lambda j: (0, col_a + j)), pl.BlockSpec((seq, LANES), lambda j: (0, col_b + j)),
                  pl.BlockSpec((None, CONV_PAD, LANES), lambda j: (layer, 0, j)),
                  pl.BlockSpec((None, 1, LANES), lambda j: (layer, 0, j))],
        out_specs=[pl.BlockSpec((seq, LANES), lambda j: (0, j))],
        out_shape=[jax.ShapeDtypeStruct((seq, d_conv), F32)],
        scratch=[pltpu.VMEM((seq + CONV_PAD, LANES), F32)],
        operands=[u, u, conv_w, conv_b], semantics=("arbitrary",))
    return res[0] if rider is None else (res[0], rode)


def _conv_bwd(u, dc1, du, conv_w, *, layer, seq, d_conv, col_a, col_b, name):
    blocks = d_conv // LANES
    rows_t = min(CONV_ROWS, seq)
    shift0 = CONV_PAD - (CONV_WIDTH - 1)

    def body(a_ref, b_ref, d_ref, w_ref, du_old, du_ref, dw_ref, pad_ref, dpad_ref, dw_acc, da_ref, db_ref, out_sem):
        block = pl.program_id(0)

        def copies(j):
            return [_columns_copy(stage, du_ref, None, (base + j) * LANES, out_sem.at[k])
                    for k, (stage, base) in enumerate(((da_ref, col_a), (db_ref, col_b)))]

        _glu_to_pad(seq, a_ref, b_ref, pad_ref)
        dpad_ref[pl.ds(seq, CONV_PAD), :] = jnp.zeros((CONV_PAD, LANES), F32)

        def fill(r, carry):
            rows = pl.ds(pl.multiple_of(r * rows_t, rows_t), rows_t)
            dpad_ref[rows, :] = d_ref[rows, :]
            return carry

        lax.fori_loop(0, seq // rows_t, fill, 0)
        dw_acc[...] = jnp.zeros_like(dw_acc)

        @pl.when(block > 0)
        def _():
            for cp in copies(block - 1):
                cp.wait()

        def step(r, carry):
            base = pl.multiple_of(r * rows_t, rows_t)
            rows = pl.ds(base, rows_t)
            window = dpad_ref[pl.ds(base, rows_t + CONV_PAD), :]
            acc = jnp.zeros((rows_t, LANES), F32)
            for residue in range(SUBLANES):
                moved = _shifted(window, residue, rows_t)
                for tap in range(CONV_WIDTH):
                    off = CONV_WIDTH - 1 - tap
                    if off % SUBLANES == residue:
                        lo = off - residue
                        acc = acc + w_ref[tap:tap + 1, :] * moved[lo:lo + rows_t, :]
            sig = _sigmoid(b_ref[rows, :])
            a = a_ref[rows, :]
            da_ref[rows, :] = (acc * sig).astype(BF16)
            db_ref[rows, :] = (acc * a * sig * (1.0 - sig)).astype(BF16)
            dcur = d_ref[rows, :]
            cwin = pad_ref[pl.ds(base, rows_t + CONV_PAD), :]
            for residue in range(SUBLANES):
                moved = _shifted(cwin, residue, rows_t)
                for tap in range(CONV_WIDTH):
                    if (shift0 + tap) % SUBLANES == residue:
                        lo = (shift0 + tap) - residue
                        prod = dcur * moved[lo:lo + rows_t, :]
                        dw_acc[tap] += jnp.sum(prod.reshape(rows_t // SUBLANES, SUBLANES, LANES), axis=0)
            return carry

        lax.fori_loop(0, seq // rows_t, step, 0)
        dw_ref[...] = jnp.sum(dw_acc[...], axis=1)
        for cp in copies(block):
            cp.start()

        @pl.when(block == blocks - 1)
        def _():
            for cp in copies(block):
                cp.wait()

    col = lambda base: pl.BlockSpec((seq, LANES), lambda j, base=base: (0, base + j))
    own = pl.BlockSpec((seq, LANES), lambda j: (0, j))
    hbm = pl.BlockSpec(memory_space=pl.ANY)
    return pl.pallas_call(
        body, name=name, grid=(blocks,),
        in_specs=[col(col_a), col(col_b), own, pl.BlockSpec((None, CONV_PAD, LANES), lambda j: (layer, 0, j)), hbm],
        out_specs=[hbm, pl.BlockSpec((CONV_PAD, LANES), lambda j: (0, j))],
        out_shape=[jax.ShapeDtypeStruct(du.shape, du.dtype), jax.ShapeDtypeStruct((CONV_PAD, d_conv), F32)],
        scratch_shapes=[pltpu.VMEM((seq + CONV_PAD, LANES), F32), pltpu.VMEM((seq + CONV_PAD, LANES), F32),
                        pltpu.VMEM((CONV_PAD, SUBLANES, LANES), F32), pltpu.VMEM((seq, LANES), BF16), pltpu.VMEM((seq, LANES), BF16),
                        pltpu.SemaphoreType.DMA((2,))],
        input_output_aliases={4: 0}, compiler_params=_cparams(("arbitrary",)),
    )(u, u, dc1, conv_w, du)


MIX_ROWS = 256


def _layer_norm_stats(val):
    mu = jnp.mean(val, axis=-1, keepdims=True)
    cen = val - mu
    var = jnp.mean(cen * cen, axis=-1, keepdims=True)
    rstd = lax.rsqrt(var + LN_EPS)
    return cen * rstd, rstd


def _layer_norm_bwd(dy, xhat, rstd, gain):
    dxhat = dy * gain
    m1 = jnp.mean(dxhat, axis=-1, keepdims=True)
    m2 = jnp.mean(dxhat * xhat, axis=-1, keepdims=True)
    dx = rstd * (dxhat - m1 - xhat * m2)
    return dx, jnp.sum(dy * xhat, axis=0, keepdims=True), jnp.sum(dy, axis=0, keepdims=True)


def _mix_forward(zatt, att, c1, zconv, gatt, gconv, x, w_att, w_conv, w_out, cln_g, cln_b, b_proj):
    s_zatt = _sigmoid(zatt)
    a_in = att * (zatt * s_zatt)
    chat, c_rstd = _layer_norm_stats(c1)
    c2 = chat * cln_g + cln_b
    s_c2 = _sigmoid(c2)
    c3 = c2 * s_c2
    s_zconv = _sigmoid(zconv)
    c_in = c3 * (zconv * s_zconv)
    a_in_b, c_in_b = a_in.astype(BF16), c_in.astype(BF16)
    ab = jnp.dot(a_in_b, w_att, preferred_element_type=F32)
    cb = jnp.dot(c_in_b, w_conv, preferred_element_type=F32) + b_proj
    s_gatt, s_gconv = _sigmoid(gatt), _sigmoid(gconv)
    merged_b = (s_gatt * ab + s_gconv * cb).astype(BF16)
    y = jnp.dot(merged_b, w_out, preferred_element_type=F32)
    h = DEEPNORM_ALPHA * x + y
    return dict(s_zatt=s_zatt, a_in_b=a_in_b, chat=chat, c_rstd=c_rstd, c2=c2, s_c2=s_c2, c3=c3, s_zconv=s_zconv,
                c_in_b=c_in_b, ab=ab, cb=cb, s_gatt=s_gatt, s_gconv=s_gconv, merged_b=merged_b, h=h)


def _u_blocks(rows_t, width, half):
    return [pl.BlockSpec((rows_t, half), lambda i, c=c: (i, c)) for c in (3, 6, 7, 8, 9, 10)]


def _of_layer(arr, layer):
    return pl.BlockSpec((None,) + arr.shape[1:], lambda i: (layer, 0, 0))


def _mix_fwd(u, att, c1, x, w_att, w_conv, w_out, cln_g, cln_b, b_proj, ln_g, ln_b, *, layer, seq, d_model, name):
    half = d_model // 2
    rows_t = min(MIX_ROWS, seq)

    def body(zatt_ref, zconv_ref, ga0, ga1, gc0, gc1, att_ref, c1_ref, x_ref, wa_ref, wc_ref, wo_ref,
             cg_ref, cb_ref, bp_ref, lg_ref, lb_ref, o_ref, ob_ref):
        gatt = jnp.concatenate([ga0[...], ga1[...]], axis=1)
        gconv = jnp.concatenate([gc0[...], gc1[...]], axis=1)
        mid = _mix_forward(zatt_ref[...], att_ref[...], c1_ref[...], zconv_ref[...], gatt, gconv, x_ref[...],
                           wa_ref[...], wc_ref[...], wo_ref[...], cg_ref[...], cb_ref[...], bp_ref[...])
        xhat, _ = _layer_norm_stats(mid["h"])
        out = xhat * lg_ref[...] + lb_ref[...]
        o_ref[...] = out
        ob_ref[...] = out.astype(BF16)

    row = lambda width: pl.BlockSpec((rows_t, width), lambda i: (i, 0))
    full = lambda arr: _of_layer(arr, layer)
    out = pl.BlockSpec((rows_t, d_model), lambda i: (i, 0))
    return pl.pallas_call(
        body, name=name, grid=(seq // rows_t,),
        in_specs=_u_blocks(rows_t, d_model, half) + [row(half), row(half), row(d_model), full(w_att), full(w_conv), full(w_out),
                                                     full(cln_g), full(cln_b), full(b_proj), full(ln_g), full(ln_b)],
        out_specs=[out, out],
        out_shape=[jax.ShapeDtypeStruct((seq, d_model), F32), jax.ShapeDtypeStruct((seq, d_model), BF16)],
        compiler_params=_cparams(("parallel",)),
    )(u, u, u, u, u, u, att, c1, x, w_att, w_conv, w_out, cln_g, cln_b, b_proj, ln_g, ln_b)


def _mix_bwd(u, att, c1, x, dxn, w_att, w_conv, w_out, cln_g, cln_b, b_proj, ln_g, *, layer, seq, d_model, name):
    half = d_model // 2
    rows_t = min(MIX_ROWS, seq)
    steps = seq // rows_t
    nt = ((1,), (1,))
    tn = ((0,), (0,))

    def body(zatt_ref, zconv_ref, ga0, ga1, gc0, gc1, att_ref, c1_ref, x_ref, dxn_ref, wa_ref, wc_ref, wo_ref,
             cg_ref, cb_ref, bp_ref, lg_ref,
             du_ref, datt_ref, dc1_ref, dxres_ref, dwa_ref, dwc_ref, dwo_ref,
             dcg_ref, dcb_ref, dcbias_ref, dbp_ref, dlg_ref, dlb_ref, dzatt_ref, tail_ref, out_sem):
        sums = (dwa_ref, dwc_ref, dwo_ref, dcg_ref, dcb_ref, dcbias_ref, dbp_ref, dlg_ref, dlb_ref)
        tile = pl.program_id(0)
        dzconv_ref, dgatt_ref, dgconv_ref = tail_ref.at[:, :half], tail_ref.at[:, half:3 * half], tail_ref.at[:, 3 * half:]

        def copies(i):
            rows = pl.ds(pl.multiple_of(i * rows_t, rows_t), rows_t)
            return [_columns_copy(dzatt_ref, du_ref, rows, 3 * half, out_sem.at[0]),
                    _columns_copy(tail_ref, du_ref, rows, 6 * half, out_sem.at[1])]

        @pl.when(tile == 0)
        def _():
            for ref in sums:
                ref[...] = jnp.zeros_like(ref)

        zatt, zconv, att = zatt_ref[...], zconv_ref[...], att_ref[...]
        gatt = jnp.concatenate([ga0[...], ga1[...]], axis=1)
        gconv = jnp.concatenate([gc0[...], gc1[...]], axis=1)
        wa, wc, wo = wa_ref[...], wc_ref[...], wo_ref[...]
        mid = _mix_forward(zatt, att, c1_ref[...], zconv, gatt, gconv, x_ref[...], wa, wc, wo,
                           cg_ref[...], cb_ref[...], bp_ref[...])
        xhat, rstd = _layer_norm_stats(mid["h"])
        dh, dlg, dlb = _layer_norm_bwd(dxn_ref[...], xhat, rstd, lg_ref[...])
        dlg_ref[...] += dlg
        dlb_ref[...] += dlb
        dxres_ref[...] = DEEPNORM_ALPHA * dh
        dy = dh.astype(BF16)
        dwo_ref[...] += lax.dot_general(mid["merged_b"], dy, (tn, ((), ())), preferred_element_type=F32)
        dmerged = lax.dot_general(dy, wo, (nt, ((), ())), preferred_element_type=F32)
        s_ga, s_gc, ab, cb = mid["s_gatt"], mid["s_gconv"], mid["ab"], mid["cb"]

        @pl.when(tile > 0)
        def _():
            for cp in copies(tile - 1):
                cp.wait()

        dgatt_ref[...] = (dmerged * ab * s_ga * (1.0 - s_ga)).astype(BF16)
        dgconv_ref[...] = (dmerged * cb * s_gc * (1.0 - s_gc)).astype(BF16)
        dab = dmerged * s_ga
        dcb = dmerged * s_gc
        dbp_ref[...] += jnp.sum(dcb, axis=0, keepdims=True)
        dab_b, dcb_b = dab.astype(BF16), dcb.astype(BF16)
        dwa_ref[...] += lax.dot_general(mid["a_in_b"], dab_b, (tn, ((), ())), preferred_element_type=F32)
        da_in = lax.dot_general(dab_b, wa, (nt, ((), ())), preferred_element_type=F32)
        s_za = mid["s_zatt"]
        datt_ref[...] = da_in * (zatt * s_za)
        dzatt_ref[...] = (da_in * att * (s_za * (1.0 + zatt * (1.0 - s_za)))).astype(BF16)
        dwc_ref[...] += lax.dot_general(mid["c_in_b"], dcb_b, (tn, ((), ())), preferred_element_type=F32)
        dc_in = lax.dot_general(dcb_b, wc, (nt, ((), ())), preferred_element_type=F32)
        s_zc, c2, s_c2 = mid["s_zconv"], mid["c2"], mid["s_c2"]
        dzconv_ref[...] = (dc_in * mid["c3"] * (s_zc * (1.0 + zconv * (1.0 - s_zc)))).astype(BF16)
        dc3 = dc_in * (zconv * s_zc)
        dc2 = dc3 * (s_c2 * (1.0 + c2 * (1.0 - s_c2)))
        dc1, dcg, dcbeta = _layer_norm_bwd(dc2, mid["chat"], mid["c_rstd"], cg_ref[...])
        dcg_ref[...] += dcg
        dcb_ref[...] += dcbeta
        dcbias_ref[...] += jnp.sum(dc1, axis=0, keepdims=True)
        dc1_ref[...] = dc1
        for cp in copies(tile):
            cp.start()

        @pl.when(tile == steps - 1)
        def _():
            for cp in copies(tile):
                cp.wait()

    row = lambda width: pl.BlockSpec((rows_t, width), lambda i: (i, 0))
    full = lambda arr: _of_layer(arr, layer)
    whole = lambda r, c: pl.BlockSpec((r, c), lambda i: (0, 0))
    sds = jax.ShapeDtypeStruct
    out_specs = [pl.BlockSpec(memory_space=pl.ANY), row(half), row(half), row(d_model),
                 whole(half, d_model), whole(half, d_model), whole(d_model, d_model),
                 whole(1, half), whole(1, half), whole(1, half), whole(1, d_model), whole(1, d_model), whole(1, d_model)]
    out_shape = [sds((seq, u.shape[1]), BF16), sds((seq, half), F32), sds((seq, half), F32), sds((seq, d_model), F32),
                 sds((half, d_model), F32), sds((half, d_model), F32), sds((d_model, d_model), F32),
                 sds((1, half), F32), sds((1, half), F32), sds((1, half), F32),
                 sds((1, d_model), F32), sds((1, d_model), F32), sds((1, d_model), F32)]
    return pl.pallas_call(
        body, name=name, grid=(steps,),
        in_specs=_u_blocks(rows_t, d_model, half) + [row(half), row(half), row(d_model), row(d_model), full(w_att), full(w_conv),
                                                     full(w_out), full(cln_g), full(cln_b), full(b_proj), full(ln_g)],
        out_specs=out_specs, out_shape=out_shape,
        scratch_shapes=[pltpu.VMEM((rows_t, half), BF16), pltpu.VMEM((rows_t, 5 * half), BF16), pltpu.SemaphoreType.DMA((2,))],
        compiler_params=_cparams(("arbitrary",)),
    )(u, u, u, u, u, u, att, c1, x, dxn, w_att, w_conv, w_out, cln_g, cln_b, b_proj, ln_g)


def _loss_head(y, target, *, seq, d_model, name):
    rows_t = min(512, seq)

    def body(y_ref, t_ref, dy_ref, loss_ref):
        @pl.when(pl.program_id(0) == 0)
        def _():
            loss_ref[...] = jnp.zeros_like(loss_ref)

        err = y_ref[...] - t_ref[...]
        dy_ref[...] = err * (1.0 / d_model)
        per_token = jnp.sum(err * err, axis=-1, keepdims=True) * (1.0 / d_model)
        loss_ref[...] += 0.5 * jnp.sum(per_token, axis=0, keepdims=True)

    row = pl.BlockSpec((rows_t, d_model), lambda i: (i, 0))
    return pl.pallas_call(
        body, name=name, grid=(seq // rows_t,), in_specs=[row, row],
        out_specs=[row, pl.BlockSpec((1, 1), lambda i: (0, 0))],
        out_shape=[jax.ShapeDtypeStruct((seq, d_model), F32), jax.ShapeDtypeStruct((1, 1), F32)],
        compiler_params=_cparams(("arbitrary",)),
    )(y, target)


def _adamw(w, g, m, v, *, name, echo=False):
    rows, cols = w.shape
    rows_t = rows
    for cand in (512, 256, 128, 64, 32, 16, 8):
        if rows % cand == 0 and cand * cols * 4 <= 2 * 1024 * 1024:
            rows_t = cand
            break

    def body(w_ref, g_ref, m_ref, v_ref, *outs):
        d_ref, nm_ref, nv_ref = outs[-3:]
        grad = g_ref[...]
        if echo:
            outs[0][...] = grad
        new_m = ADAM_B1 * m_ref[...] + (1.0 - ADAM_B1) * grad
        new_v = ADAM_B2 * v_ref[...] + (1.0 - ADAM_B2) * (grad * grad)
        m_hat = new_m / (1.0 - ADAM_B1 ** ADAM_STEP)
        v_hat = new_v / (1.0 - ADAM_B2 ** ADAM_STEP)
        d_ref[...] = -ADAM_LR * (m_hat / (jnp.sqrt(v_hat) + ADAM_EPS) + ADAM_WD * w_ref[...])
        nm_ref[...] = new_m
        nv_ref[...] = new_v

    blk = pl.BlockSpec((rows_t, cols), lambda i: (i, 0))
    out = jax.ShapeDtypeStruct((rows, cols), F32)
    n_out = 4 if echo else 3
    return pl.pallas_call(
        body, name=name, grid=(rows // rows_t,), in_specs=[blk] * 4, out_specs=[blk] * n_out, out_shape=[out] * n_out,
        compiler_params=_cparams(("parallel",)),
    )(w, g, m, v)


MATRICES = ("w_in", "w_att_proj", "w_conv_proj", "w_out")
MATRIX_CHIP_AXIS = (1, 1, 1, 0)
MATRIX_CORE_AXIS = (0, 0, 0, 1)


def _train_pass(x, target, mats, taps, b_in, conv_b, cln_g, cln_b, b_proj, ln_g, ln_b, place):
    seq, d_model = x.shape
    half = d_model // 2
    depth = b_in.shape[0]
    scan_suffix, scan_prefix = _scan_matrices()
    cols = half // LANES
    dims = dict(seq=seq, d_model=d_model)
    conv_dims = dict(seq=seq, d_conv=half, col_a=4 * cols, col_b=5 * cols)
    axes3 = [axis + 1 for axis in MATRIX_CHIP_AXIS]
    n_mat = len(mats)

    def first_layers(relay):
        return _gather_rider([mats[0], taps], [axes3[0], 2], [(0, 1), (0, depth)], relay)

    mats = list(mats)
    mats[0], taps = _lone_call(_chain(first_layers(False), first_layers(True)), name="gather_first")

    xs, xbs, us, atts, c1s = [x], [x.astype(BF16)], [], [], []
    for l in range(depth):
        u = _matmul(xbs[l], mats[0], layer=l, mode="nn", bias=b_in[l].reshape(1, -1), name="in_proj", tm=256, tn=b_in.shape[1],
                    tk=d_model)
        nxt = [(l + 1, 1)] + [(l + 1, 1) if l else (0, 2)] * (n_mat - 1)
        if l + 1 < depth:
            att, mats = _attn_fwd(u, scan_suffix, seq=seq, d_att=half, name="attn_fwd", rider=_gather_rider(mats, axes3, nxt, False))
            c1, mats = _conv_fwd(u, taps, conv_b, layer=l, name="conv_fwd", rider=_gather_rider(mats, axes3, nxt, True), **conv_dims)
        else:
            att = _attn_fwd(u, scan_suffix, seq=seq, d_att=half, name="attn_fwd_last")
            c1 = _conv_fwd(u, taps, conv_b, layer=l, name="conv_fwd_last", **conv_dims)
        xn, xnb = _mix_fwd(u, att, c1, xs[l], mats[1], mats[2], mats[3], cln_g, cln_b, b_proj, ln_g, ln_b, layer=l, name="mix_fwd", **dims)
        us.append(u)
        atts.append(att)
        c1s.append(c1)
        xs.append(xn)
        xbs.append(xnb)
    w_in, w_att, w_conv, w_out = mats

    def chip_sums(layer, partials, got, blocks):
        return [_chip_sum_layer(partials[t], got[t], blocks[t], layer, depth, MATRIX_CHIP_AXIS[t], MATRIX_CORE_AXIS[t], place,
                                name="chip_sum_" + MATRICES[t]) for t in range(n_mat)]

    dx, loss = _loss_head(xs[depth], target, name="loss_head", **dims)
    grads = [None] * depth
    blocks = [None] * n_mat
    waiting = None
    for l in reversed(range(depth)):
        u = us[l]
        (du, datt, dc1, dxres, dwa, dwc, dwo, dcg, dcb, dcbias, dbp, dlg, dlb) = _mix_bwd(
            u, atts[l], c1s[l], xs[l], dx, w_att, w_conv, w_out, cln_g, cln_b, b_proj, ln_g, layer=l, name="mix_bwd", **dims)
        if waiting is None:
            du = _attn_bwd(u, datt, du, scan_suffix, scan_prefix, seq=seq, d_att=half, name="attn_bwd_first")
        else:
            du, got = _attn_bwd(u, datt, du, scan_suffix, scan_prefix, seq=seq, d_att=half, name="attn_bwd",
                                rider=_exchange_rider(waiting[1], MATRIX_CHIP_AXIS))
            blocks = chip_sums(waiting[0], waiting[1], got, blocks)
        du, dconvw = _conv_bwd(u, dc1, du, taps, layer=l, name="conv_bwd", **conv_dims)
        dwin, dbin = _matmul(xbs[l], du, mode="tn", colsum=True, name="in_proj_dw", tm=1024, tn=512, tk=seq)
        parts = [dwin, dwa, dwc, dwo]
        dx, got = _matmul(du, w_in, layer=l, mode="nt", add=dxres, name="in_proj_dx", tm=512, tn=1024, tk=du.shape[1],
                          rider=_swap_rider(parts, MATRIX_CORE_AXIS))
        waiting = (l, [_pair_sum_layer(parts[t], got[t], MATRIX_CORE_AXIS[t], place, name="pair_sum_" + MATRICES[t])
                       for t in range(n_mat)])
        grads[l] = dict(b_in=dbin[0], conv_w=dconvw[:CONV_WIDTH], conv_b=dcbias[0], conv_ln_g=dcg[0], conv_ln_b=dcb[0],
                        b_conv_proj=dbp[0], ln_g=dlg[0], ln_b=dlb[0])
    got = _lone_call(_exchange_rider(waiting[1], MATRIX_CHIP_AXIS), name="exchange_last")
    blocks = chip_sums(waiting[0], waiting[1], got, blocks)
    return loss, dx, grads, blocks


MESH = pl.DeviceIdType.MESH
IN_HBM = pl.BlockSpec(memory_space=pl.ANY)


def _position():
    x, y, c = lax.axis_index("x"), lax.axis_index("y"), lax.axis_index("c")
    return x, y, c, [(1 - x, y), (x, 1 - y), (1 - x, 1 - y)]


def _cut(ref, axis, start, size):
    idx = [slice(None)] * len(ref.shape)
    idx[axis] = pl.ds(start, size)
    return ref.at[tuple(idx)]


def _remote(src, dst, send_sem, recv_sem, device):
    return pltpu.make_async_remote_copy(src_ref=src, dst_ref=dst, send_sem=send_sem, recv_sem=recv_sem,
                                        device_id=device, device_id_type=MESH)


def _comm_call(body, name, operands, out_shape, scratch, alias=False):
    return pl.pallas_call(
        body, name=name, in_specs=[IN_HBM] * len(operands), out_specs=[IN_HBM] * len(out_shape), out_shape=out_shape,
        scratch_shapes=scratch, input_output_aliases={t: t for t in range(len(operands))} if alias else {},
    )(*operands)


def _gather_rider(wholes, chip_axes, spans, relay):
    n = len(wholes)

    def region(dst, t, chip, half):
        first, count = spans[t]
        _, rows, cols = wholes[t].shape
        if chip_axes[t] == 2:
            size, part = cols // N_CHIPS, rows // 2
            ref = _cut(_cut(dst[t], 2, pl.multiple_of(chip * size, size), size), 1, pl.multiple_of(half * part, part), part)
        else:
            size = rows // N_CHIPS
            part = size // 2
            ref = _cut(dst[t], 1, pl.multiple_of(chip * size + half * part, part), part)
        return _cut(ref, 0, first, count)

    def copies(dst, sems, receiving):
        x, y, c, chips = _position()
        send_sem, recv_sem = sems
        out = []
        for t in range(n):
            for j, chip in enumerate(chips):
                theirs = 2 * chip[0] + chip[1]
                if relay:
                    ref = region(dst, t, theirs, 1 - c if receiving else c)
                    peer = (x, y, 1 - c)
                else:
                    ref = region(dst, t, theirs if receiving else 2 * x + y, c)
                    peer = (*chip, c)
                out.append(_remote(ref, ref, send_sem.at[t, j], recv_sem.at[t, j], peer))
        return out

    def start(ins, outs, sems):
        for cp in copies(outs, sems, False):
            cp.start()

    def finish(ins, outs, sems):
        for cp in copies(outs, sems, True):
            cp.wait_recv()
        for cp in copies(outs, sems, False):
            cp.wait_send()

    return _Rider(operands=list(wholes), out_shape=[jax.ShapeDtypeStruct(w.shape, w.dtype) for w in wholes],
                  aliases={t: t for t in range(n)}, scratch=[pltpu.SemaphoreType.DMA((n, N_CHIPS - 1))] * 2,
                  start=start, finish=finish)


def _chain(first, second):
    cut = len(first.scratch)

    def start(ins, outs, sems):
        first.start(ins, outs, sems[:cut])
        first.finish(ins, outs, sems[:cut])
        second.start(ins, outs, sems[cut:])

    def finish(ins, outs, sems):
        second.finish(ins, outs, sems[cut:])

    return first._replace(scratch=list(first.scratch) + list(second.scratch), start=start, finish=finish)


def _swap_rider(parts, core_axes):
    n = len(parts)
    halves = []
    for arr, axis in zip(parts, core_axes):
        shape = list(arr.shape)
        shape[axis] //= 2
        halves.append(jax.ShapeDtypeStruct(tuple(shape), arr.dtype))

    def copies(ins, outs, sems):
        x, y, c, _ = _position()
        out = []
        for t in range(n):
            size = halves[t].shape[core_axes[t]]
            piece = _cut(ins[t], core_axes[t], pl.multiple_of((1 - c) * size, size), size)
            out.append(_remote(piece, outs[t], sems[0].at[t], sems[1].at[t], (x, y, 1 - c)))
        return out

    def start(ins, outs, sems):
        for cp in copies(ins, outs, sems):
            cp.start()

    def finish(ins, outs, sems):
        for cp in copies(ins, outs, sems):
            cp.wait()

    return _Rider(operands=list(parts), out_shape=halves, aliases={}, scratch=[pltpu.SemaphoreType.DMA((n,))] * 2,
                  start=start, finish=finish)


def _exchange_rider(partials, chip_axes):
    n = len(partials)
    quarters = []
    for arr, axis in zip(partials, chip_axes):
        shape = list(arr.shape)
        shape[axis] //= N_CHIPS
        quarters.append(jax.ShapeDtypeStruct((N_CHIPS - 1, *shape), arr.dtype))

    def copies(ins, outs, sems):
        x, y, c, chips = _position()
        out = []
        for t in range(n):
            size = quarters[t].shape[1 + chip_axes[t]]
            for j, chip in enumerate(chips):
                piece = _cut(ins[t], chip_axes[t], pl.multiple_of((2 * chip[0] + chip[1]) * size, size), size)
                out.append(_remote(piece, outs[t].at[j], sems[0].at[t, j], sems[1].at[t, j], (*chip, c)))
        return out

    def start(ins, outs, sems):
        for cp in copies(ins, outs, sems):
            cp.start()

    def finish(ins, outs, sems):
        for cp in copies(ins, outs, sems):
            cp.wait()

    return _Rider(operands=list(partials), out_shape=quarters, aliases={}, scratch=[pltpu.SemaphoreType.DMA((n, N_CHIPS - 1))] * 2,
                  start=start, finish=finish)


def _place_block(shard, chip_axis, place, dtype, *, name):
    depth, rows, cols = shard.shape
    rows_t = _row_tile(rows, cols * 4 * 4, 16 * 1024 * 1024)
    steps = rows // rows_t
    shape = list(shard.shape)
    shape[chip_axis] *= N_CHIPS
    if chip_axis == 1:
        out_spec = pl.BlockSpec((None, rows_t, cols), lambda l, i, p: (l, p[1] * steps + i, 0))
    else:
        out_spec = pl.BlockSpec((None, rows_t, cols), lambda l, i, p: (l, i, p[1]))

    def body(place_ref, src_ref, out_ref):
        out_ref[...] = src_ref[...].astype(dtype)

    return pl.pallas_call(
        body, name=name, out_shape=jax.ShapeDtypeStruct(tuple(shape), dtype),
        grid_spec=pltpu.PrefetchScalarGridSpec(num_scalar_prefetch=1, grid=(depth, steps),
                                               in_specs=[pl.BlockSpec((None, rows_t, cols), lambda l, i, p: (l, i, 0))],
                                               out_specs=out_spec),
        compiler_params=_cparams(("arbitrary", "arbitrary")),
    )(place, shard)


def _pair_join(blocks, core_axes):
    n = len(blocks)

    def body(*refs):
        dst = refs[n:2 * n]
        send_sem, recv_sem = refs[2 * n:]
        x, y, c, _ = _position()
        copies = []
        for t in range(n):
            size = blocks[t].shape[core_axes[t]] // 2
            mine = _cut(dst[t], core_axes[t], pl.multiple_of(c * size, size), size)
            cp = _remote(mine, mine, send_sem.at[t], recv_sem.at[t], (x, y, 1 - c))
            cp.start()
            copies.append(cp)
        for t, cp in enumerate(copies):
            size = blocks[t].shape[core_axes[t]] // 2
            other = _cut(dst[t], core_axes[t], pl.multiple_of((1 - c) * size, size), size)
            cp.wait_send()
            _remote(other, other, send_sem.at[t], recv_sem.at[t], (x, y, 1 - c)).wait_recv()

    return _comm_call(body, "pair_join", blocks, [jax.ShapeDtypeStruct(b.shape, b.dtype) for b in blocks],
                      [pltpu.SemaphoreType.DMA((n,))] * 2, alias=True)


def _gather_small(vec):
    n_dev = 2 * N_CHIPS

    def body(src, dst, send_sem, recv_sem, local_sem):
        x, y, c, _ = _position()
        flip = lambda v, bit: 1 - v if bit else v
        mine = pltpu.make_async_copy(src, dst.at[4 * x + 2 * y + c], local_sem)
        mine.start()
        copies = []
        for mask in range(1, n_dev):
            peer = (flip(x, mask & 4), flip(y, mask & 2), flip(c, mask & 1))
            cp = _remote(src, dst.at[4 * x + 2 * y + c], send_sem.at[mask - 1], recv_sem.at[mask - 1], peer)
            cp.start()
            copies.append((cp, peer))
        for mask, (cp, peer) in enumerate(copies, start=1):
            theirs = dst.at[4 * peer[0] + 2 * peer[1] + peer[2]]
            _remote(src, theirs, send_sem.at[mask - 1], recv_sem.at[mask - 1], peer).wait_recv()
        for cp, _ in copies:
            cp.wait_send()
        mine.wait()

    out = [jax.ShapeDtypeStruct((n_dev, *vec.shape), vec.dtype)]
    return _comm_call(body, "gather_small", [vec], out, [pltpu.SemaphoreType.DMA((n_dev - 1,))] * 2 + [pltpu.SemaphoreType.DMA(())])[0]


def _row_tile(rows, row_bytes, budget):
    tile = rows
    for cand in (512, 256, 128, 64, 32, 16, 8):
        if rows % cand == 0:
            tile = cand
            if cand * row_bytes <= budget:
                break
    return tile


def _pair_sum_layer(part, got, core_axis, place, *, name):
    rows, cols = got.shape
    rows_t = _row_tile(rows, cols * 4 * 6, 16 * 1024 * 1024)
    steps = rows // rows_t
    if core_axis == 0:
        part_spec = pl.BlockSpec((rows_t, cols), lambda i, p: (p[0] * steps + i, 0))
    else:
        part_spec = pl.BlockSpec((rows_t, cols), lambda i, p: (i, p[0]))
    own_spec = pl.BlockSpec((rows_t, cols), lambda i, p: (i, 0))

    def body(place_ref, part_ref, got_ref, out_ref):
        out_ref[...] = (part_ref[...] + got_ref[...]).astype(BF16)

    return pl.pallas_call(
        body, name=name, out_shape=jax.ShapeDtypeStruct(got.shape, BF16),
        grid_spec=pltpu.PrefetchScalarGridSpec(num_scalar_prefetch=1, grid=(steps,), in_specs=[part_spec, own_spec], out_specs=own_spec),
        compiler_params=_cparams(("arbitrary",)),
    )(place, part, got)


def _chip_sum_layer(partial, got, blocks, layer, depth, chip_axis, core_axis, place, *, name):
    _, rows, cols = got.shape
    rows_t = _row_tile(rows, cols * 4 * 10, 24 * 1024 * 1024)
    steps = rows // rows_t
    if chip_axis == 0:
        own_spec = pl.BlockSpec((rows_t, cols), lambda i, p: (p[1] * steps + i, 0))
    else:
        own_spec = pl.BlockSpec((rows_t, cols), lambda i, p: (i, p[1]))
    got_spec = pl.BlockSpec((N_CHIPS - 1, rows_t, cols), lambda i, p: (0, i, 0))
    shape = [depth, rows, cols]
    shape[1 + core_axis] *= 2
    if core_axis == 0:
        out_spec = pl.BlockSpec((None, rows_t, cols), lambda i, p: (layer, p[0] * steps + i, 0))
    else:
        out_spec = pl.BlockSpec((None, rows_t, cols), lambda i, p: (layer, i, p[0]))

    def body(place_ref, own_ref, got_ref, *rest):
        out_ref = rest[-1]
        up = lambda val: val.astype(F32)
        out_ref[...] = ((up(own_ref[...]) + up(got_ref[0])) + up(got_ref[1])) + up(got_ref[2])

    in_specs, operands, aliases = [own_spec, got_spec], [place, partial, got], {}
    if blocks is not None:
        in_specs.append(pl.BlockSpec(memory_space=pl.ANY))
        operands.append(blocks)
        aliases = {3: 0}
    return pl.pallas_call(
        body, name=name, out_shape=jax.ShapeDtypeStruct(tuple(shape), F32),
        grid_spec=pltpu.PrefetchScalarGridSpec(num_scalar_prefetch=1, grid=(steps,), in_specs=in_specs, out_specs=out_spec),
        input_output_aliases=aliases, compiler_params=_cparams(("arbitrary",)),
    )(*operands)


def _sum_devices(stack, *, name):
    def body(src_ref, out_ref):
        total = src_ref[0]
        for d in range(1, stack.shape[0]):
            total = total + src_ref[d]
        out_ref[...] = total

    return pl.pallas_call(body, name=name, out_shape=jax.ShapeDtypeStruct(stack.shape[1:], F32))(stack)


def kernel(x, w_in, b_in, conv_w, conv_b, conv_ln_g, conv_ln_b, w_att_proj, w_conv_proj, b_conv_proj, w_out, ln_g, ln_b, loss_target, m_w_in, m_b_in, m_conv_w, m_conv_b, m_conv_ln_g, m_conv_ln_b, m_w_att_proj, m_w_conv_proj, m_b_conv_proj, m_w_out, m_ln_g, m_ln_b, v_w_in, v_b_in, v_conv_w, v_conv_b, v_conv_ln_g, v_conv_ln_b, v_w_att_proj, v_w_conv_proj, v_b_conv_proj, v_w_out, v_ln_g, v_ln_b):
    depth = w_in.shape[0]
    d_model = x.shape[-1]
    half = d_model // 2
    chip = 2 * lax.axis_index("x") + lax.axis_index("y")
    place = jnp.stack([lax.axis_index("c"), chip]).astype(jnp.int32)
    vec3 = lambda v: v.reshape(depth, 1, -1)

    taps = jnp.pad(conv_w, ((0, 0), (0, CONV_PAD - CONV_WIDTH), (0, 0)))
    gathered = [("w_in", w_in, 2, BF16), ("w_att_proj", w_att_proj, 2, BF16), ("w_conv_proj", w_conv_proj, 2, BF16),
                ("w_out", w_out, 1, BF16), ("conv_w", taps, 2, F32)]
    wholes = [_place_block(arr, axis, place, dtype, name="place_" + n) for n, arr, axis, dtype in gathered]
    loss, grad_x, grads, blocks = _train_pass(x[0], loss_target[0], wholes[:4], wholes[4], b_in, vec3(conv_b), vec3(conv_ln_g),
                                              vec3(conv_ln_b), vec3(b_conv_proj), vec3(ln_g), vec3(ln_b), place)
    loss = lax.psum(loss[0, 0], ("x", "y", "c"))
    reduced = dict(zip(MATRICES, _pair_join(blocks, [axis + 1 for axis in MATRIX_CORE_AXIS])))

    small = ["b_in", "conv_b", "conv_ln_g", "conv_ln_b", "b_conv_proj", "ln_g", "ln_b"]
    packed = jnp.stack([jnp.concatenate([grads[l][n] for n in small] + [grads[l]["conv_w"].reshape(-1)]) for l in range(depth)])
    total = _sum_devices(_gather_small(packed), name="sum_devices")
    offset = 0
    for n in small:
        width = grads[0][n].shape[0]
        reduced[n] = total[:, offset:offset + width]
        offset += width
    taps = total[:, offset:].reshape(depth, CONV_WIDTH, half)
    reduced["conv_w"] = lax.dynamic_slice_in_dim(taps, chip * conv_w.shape[2], conv_w.shape[2], axis=2)

    names = ["w_in", "b_in", "conv_w", "conv_b", "conv_ln_g", "conv_ln_b", "w_att_proj", "w_conv_proj", "b_conv_proj", "w_out", "ln_g", "ln_b"]
    weights = dict(zip(names, (w_in, b_in, conv_w, conv_b, conv_ln_g, conv_ln_b, w_att_proj, w_conv_proj, b_conv_proj, w_out, ln_g, ln_b)))
    first = dict(zip(names, (m_w_in, m_b_in, m_conv_w, m_conv_b, m_conv_ln_g, m_conv_ln_b, m_w_att_proj, m_w_conv_proj, m_b_conv_proj, m_w_out, m_ln_g, m_ln_b)))
    second = dict(zip(names, (v_w_in, v_b_in, v_conv_w, v_conv_b, v_conv_ln_g, v_conv_ln_b, v_w_att_proj, v_w_conv_proj, v_b_conv_proj, v_w_out, v_ln_g, v_ln_b)))
    delta, new_m, new_v = {}, {}, {}
    for n in names:
        shape = weights[n].shape
        flat = lambda arr: arr.reshape(-1, shape[-1])
        *grad, d, m, v = _adamw(flat(weights[n]), flat(reduced[n]), flat(first[n]), flat(second[n]), name="adamw_" + n,
                                echo=n in MATRICES)
        if grad:
            reduced[n] = grad[0]
        delta[n], new_m[n], new_v[n] = d.reshape(shape), m.reshape(shape), v.reshape(shape)
    return (loss, grad_x[None], *[reduced[n].reshape(weights[n].shape) for n in names], *[delta[n] for n in names],
            *[new_m[n] for n in names], *[new_v[n] for n in names])
```

```python
import functools
from typing import Callable, NamedTuple

import jax
import jax.numpy as jnp
from jax import lax
from jax.experimental import pallas as pl
from jax.experimental.pallas import tpu as pltpu

F32 = jnp.float32
BF16 = jnp.bfloat16

HEAD_DIM = 64
LANES = 128
CONV_WIDTH = 31
CONV_PAD = 32
SUBLANES = 8
LN_EPS = 1e-5
DEPTH = 4
DEEPNORM_ALPHA = (2 * DEPTH) ** 0.25
ATT_SCALE = HEAD_DIM ** -0.5
ATT_TILE = 128
ATT_DEAD = -104.0
ATT_GROUP = 2
ATT_GROUP_FWD = 4
ATT_FILL = -1e30

ADAM_LR = 0.001
ADAM_B1 = 0.9
ADAM_B2 = 0.999
ADAM_EPS = 1e-08
ADAM_WD = 0.01
ADAM_STEP = 10

VMEM_LIMIT = 56 * 1024 * 1024

N_CHIPS = 4


def _cparams(sem):
    return pltpu.CompilerParams(dimension_semantics=sem, vmem_limit_bytes=VMEM_LIMIT)


def _sigmoid(x):
    return 1.0 / (1.0 + jnp.exp(-x))


class _Rider(NamedTuple):
    operands: list
    out_shape: list
    aliases: dict
    scratch: list
    start: Callable
    finish: Callable


def _host_call(body, rider, *, name, grid, in_specs, out_specs, out_shape, scratch, operands, semantics, host_aliases=None):
    n_in, n_out = len(in_specs), len(out_specs)
    aliases = dict(host_aliases or {})
    if rider is not None:
        r_in, r_out = len(rider.operands), len(rider.out_shape)
        host_body = body

        def body(*refs):
            base = n_in + r_in
            ins, rins = refs[:n_in], refs[n_in:base]
            outs, routs = refs[base:base + n_out], refs[base + n_out:base + n_out + r_out]
            rest = refs[base + n_out + r_out:]
            split = len(rest) - len(rider.scratch)
            ids = [pl.program_id(axis) for axis in range(len(grid))]
            first = functools.reduce(jnp.logical_and, [i == 0 for i in ids])
            last = functools.reduce(jnp.logical_and, [i == g - 1 for i, g in zip(ids, grid)])

            @pl.when(first)
            def _():
                rider.start(rins, routs, rest[split:])

            host_body(*ins, *outs, *rest[:split])

            @pl.when(last)
            def _():
                rider.finish(rins, routs, rest[split:])

        hbm = pl.BlockSpec(memory_space=pl.ANY)
        in_specs = list(in_specs) + [hbm] * r_in
        out_specs = list(out_specs) + [hbm] * r_out
        out_shape = list(out_shape) + list(rider.out_shape)
        scratch = list(scratch) + list(rider.scratch)
        operands = list(operands) + list(rider.operands)
        aliases.update({n_in + i: n_out + o for i, o in rider.aliases.items()})
    res = pl.pallas_call(
        body, name=name, grid=grid, in_specs=list(in_specs), out_specs=list(out_specs), out_shape=list(out_shape),
        scratch_shapes=list(scratch), input_output_aliases=aliases, compiler_params=_cparams(semantics),
    )(*operands)
    return list(res[:n_out]), list(res[n_out:])


def _lone_call(rider, *, name):
    r_in = len(rider.operands)

    def body(*refs):
        ins, outs, sems = refs[:r_in], refs[r_in:r_in + len(rider.out_shape)], refs[r_in + len(rider.out_shape):]
        rider.start(ins, outs, sems)
        rider.finish(ins, outs, sems)

    hbm = pl.BlockSpec(memory_space=pl.ANY)
    return list(pl.pallas_call(
        body, name=name, in_specs=[hbm] * r_in, out_specs=[hbm] * len(rider.out_shape), out_shape=list(rider.out_shape),
        scratch_shapes=list(rider.scratch), input_output_aliases=dict(rider.aliases),
    )(*rider.operands))


def _fit(tile, dim):
    assert dim % LANES == 0
    tile = min(tile, dim) // LANES * LANES
    while dim % tile:
        tile -= LANES
    return tile


_DIMS = {"nn": ((1,), (0,)), "nt": ((1,), (1,)), "tn": ((0,), (0,))}


def _matmul(a, b, *, mode, name, layer=None, bias=None, add=None, colsum=False, out_dtype=F32, tm=1024, tn=512, tk=1024,
            rider=None):
    b_shape = b.shape if layer is None else b.shape[1:]
    if mode == "nn":
        (m, k), (k2, n) = a.shape, b_shape
    elif mode == "nt":
        (m, k), (n, k2) = a.shape, b_shape
    else:
        (k, m), (k2, n) = a.shape, b_shape
    assert k == k2
    tm, tn, tk = _fit(tm, m), _fit(tn, n), _fit(tk, k)
    gm, gn, nk = m // tm, n // tn, k // tk

    a_spec = pl.BlockSpec((tk, tm), lambda i, j, kk: (kk, i)) if mode == "tn" else pl.BlockSpec((tm, tk), lambda i, j, kk: (i, kk))
    if layer is None:
        b_spec = pl.BlockSpec((tn, tk), lambda i, j, kk: (j, kk)) if mode == "nt" else pl.BlockSpec((tk, tn), lambda i, j, kk: (kk, j))
    elif mode == "nt":
        b_spec = pl.BlockSpec((None, tn, tk), lambda i, j, kk: (layer, j, kk))
    else:
        b_spec = pl.BlockSpec((None, tk, tn), lambda i, j, kk: (layer, kk, j))
    in_specs, operands = [a_spec, b_spec], [a, b]
    if bias is not None:
        in_specs.append(pl.BlockSpec((1, tn), lambda i, j, kk: (0, j)))
        operands.append(bias)
    if add is not None:
        in_specs.append(pl.BlockSpec((tm, tn), lambda i, j, kk: (i, j)))
        operands.append(add)
    out_shape = [jax.ShapeDtypeStruct((m, n), out_dtype)]
    out_specs = [pl.BlockSpec((tm, tn), lambda i, j, kk: (i, j))]
    scratch = [pltpu.VMEM((tm, tn), F32)] if nk > 1 else []
    if colsum:
        assert mode == "tn"
        out_shape.append(jax.ShapeDtypeStruct((gm, 1, n), F32))
        out_specs.append(pl.BlockSpec((1, 1, tn), lambda i, j, kk: (i, 0, j)))
        if nk > 1:
            scratch.append(pltpu.VMEM((1, tn), F32))
    has_bias, has_add = bias is not None, add is not None

    def body(*refs):
        refs = list(refs)
        a_ref, b_ref = refs[0], refs[1]
        pos = 2
        bias_ref = add_ref = None
        if has_bias:
            bias_ref = refs[pos]
            pos += 1
        if has_add:
            add_ref = refs[pos]
            pos += 1
        o_ref = refs[pos]
        pos += 1
        cs_ref = None
        if colsum:
            cs_ref = refs[pos]
            pos += 1

        def finish(out, sums):
            if has_bias:
                out = out + bias_ref[...]
            if has_add:
                out = out + add_ref[...]
            o_ref[...] = out.astype(out_dtype)
            if colsum:
                cs_ref[0] = sums

        bv = b_ref[...]
        prod = lax.dot_general(a_ref[...].astype(BF16), bv.astype(BF16), (_DIMS[mode], ((), ())), preferred_element_type=F32)
        sums = jnp.sum(bv.astype(F32), axis=0, keepdims=True) if colsum else None
        if nk == 1:
            finish(prod, sums)
            return
        acc_ref = refs[pos]
        cs_acc = refs[pos + 1] if colsum else None
        kk = pl.program_id(2)

        @pl.when(kk == 0)
        def _():
            acc_ref[...] = jnp.zeros_like(acc_ref)
            if colsum:
                cs_acc[...] = jnp.zeros_like(cs_acc)

        acc_ref[...] += prod
        if colsum:
            cs_acc[...] += sums

        @pl.when(kk == nk - 1)
        def _():
            finish(acc_ref[...], cs_acc[...] if colsum else None)

    res, rode = _host_call(body, rider, name=name, grid=(gm, gn, nk), in_specs=in_specs, out_specs=out_specs, out_shape=out_shape,
                           scratch=scratch, operands=operands, semantics=("arbitrary", "arbitrary", "arbitrary"))
    out = (res[0], res[1][0]) if colsum else res[0]
    return out if rider is None else (out, rode)


def _scan_matrices():
    t = ATT_TILE
    r = lax.broadcasted_iota(jnp.int32, (t, t), 0)
    c = lax.broadcasted_iota(jnp.int32, (t, t), 1)
    ones = jnp.ones((t, t), F32)
    suffix = jnp.concatenate([(r > c).astype(F32), ones], axis=1)
    prefix = jnp.concatenate([(r < c).astype(F32), ones], axis=1)
    stack = lambda mat: jnp.concatenate([mat, mat], axis=0).astype(BF16)
    return stack(suffix), stack(prefix)


def _split_halves(val):
    hi = val.astype(BF16)
    lo = (val - hi.astype(F32)).astype(BF16)
    return jnp.concatenate([hi, lo], axis=1)


def _split_scan(val, mat_ref):
    return jnp.dot(_split_halves(val), mat_ref[...], preferred_element_type=F32)


def _pair_scores(q, k_lo, k_hi, masked):
    t = ATT_TILE
    z2 = lax.dot_general(q, jnp.concatenate([k_lo, k_hi], axis=0), (((1,), (1,)), ((), ())), preferred_element_type=F32)
    out = []
    for h in range(2):
        z = z2[:, h * t:(h + 1) * t]
        sp = jnp.log(1.0 + jnp.exp(-jnp.abs(z)))
        f = jnp.minimum(-z, 0.0) - sp
        a = f + z
        if masked:
            causal = lax.broadcasted_iota(jnp.int32, (t, t), 1) < lax.broadcasted_iota(jnp.int32, (t, t), 0)
            f = jnp.where(causal, f, 0.0)
        out.append((_split_halves(f), a))
    return out


def _any_alive(runs):
    top = functools.reduce(jnp.maximum, [run for per_head in runs for run in per_head])
    return (jnp.max(top) > ATT_DEAD).astype(jnp.int32)


def _head_copies(seq, src_ref, scale, lo_ref, hi_ref, plain_ref):
    chunk = min(256, seq)
    low = lax.broadcasted_iota(jnp.int32, (chunk, LANES), 1) < HEAD_DIM

    def step(r, carry):
        rows = pl.ds(pl.multiple_of(r * chunk, chunk), chunk)
        val = src_ref[rows, :]
        if scale != 1.0:
            val = val * scale
        if lo_ref is not None:
            lo_ref[rows, :] = jnp.where(low, val, 0.0).astype(BF16)
            hi_ref[rows, :] = jnp.where(low, 0.0, val).astype(BF16)
        if plain_ref is not None:
            plain_ref[rows, :] = val.astype(BF16)
        return carry

    lax.fori_loop(0, seq // chunk, step, 0)


def _attn_fwd(u, scan_suffix, *, seq, d_att, name, rider=None):
    t = ATT_TILE
    nq = seq // t
    pairs = d_att // LANES
    grp = ATT_GROUP_FWD
    assert nq % grp == 0

    def body(q_ref, k_ref, v_ref, um_ref, o_ref, qq, k0, k1, v0, v1, f2_s, a_s, lg_s, tot_s, run_s, acc_s):
        _head_copies(seq, q_ref, ATT_SCALE, None, None, qq)
        _head_copies(seq, k_ref, 1.0, k0, k1, None)
        _head_copies(seq, v_ref, 1.0, v0, v1, None)

        def group(gi, carry):
            qb0 = gi * grp
            qrows = [pl.ds(pl.multiple_of((qb0 + g) * t, t), t) for g in range(grp)]
            qv = [qq[qrows[g], :] for g in range(grp)]

            chains = [(h, g) for g in range(grp) for h in range(2)]

            def key_rows(g, i):
                return pl.ds(pl.multiple_of(jnp.maximum(qb0 + g - i, 0) * t, t), t)

            def stage1(i, masked):
                out = []
                for g in range(grp):
                    krows = key_rows(g, i)
                    out += _pair_scores(qv[g], k0[krows, :], k1[krows, :], masked)
                return out

            def stage2(halves, a, masked):
                scan = jnp.dot(halves, um_ref[...], preferred_element_type=F32)
                logit = a + scan[:, :t]
                if masked:
                    causal = lax.broadcasted_iota(jnp.int32, (t, t), 1) < lax.broadcasted_iota(jnp.int32, (t, t), 0)
                    logit = jnp.where(causal, logit, ATT_FILL)
                return logit, scan[:, t:]

            def put(halves_a=None, logit_total=None):
                for c in range(len(chains)):
                    if halves_a is not None:
                        f2_s[c], a_s[c] = halves_a[c]
                    if logit_total is not None:
                        lg_s[c], tot_s[c] = logit_total[c]

            first = stage1(0, True)
            put(halves_a=stage1(1, False), logit_total=[stage2(f2, a, True) for f2, a in first])
            for c in range(len(chains)):
                run_s[c] = jnp.zeros((t, t), F32)
            for g in range(grp):
                acc_s[g] = jnp.zeros((t, LANES), F32)

            def wbody(st):
                i = st[0]
                held = [(f2_s[c], a_s[c]) for c in range(len(chains))]
                logits = [lg_s[c] for c in range(len(chains))]
                totals = [tot_s[c] for c in range(len(chains))]
                runs = [run_s[c] for c in range(len(chains))]
                accs = [acc_s[g] for g in range(grp)]
                for g in range(grp):
                    krows = key_rows(g, i)
                    gone = jnp.where(qb0 + g - i >= 0, 0.0, ATT_FILL)
                    weights = []
                    for c in (2 * g, 2 * g + 1):
                        run = runs[c] + gone
                        weights.append(jnp.exp(logits[c] + run).astype(BF16))
                        runs[c] = run + totals[c]
                    accs[g] = accs[g] + jnp.dot(jnp.concatenate(weights, axis=1), jnp.concatenate([v0[krows, :], v1[krows, :]], axis=0),
                                                preferred_element_type=F32)
                ahead2 = [stage2(f2, a, False) for f2, a in held]
                ahead1 = stage1(i + 2, False)
                put(halves_a=ahead1, logit_total=ahead2)
                for c in range(len(chains)):
                    run_s[c] = runs[c]
                for g in range(grp):
                    acc_s[g] = accs[g]
                more = jnp.logical_and(i + 1 <= qb0 + grp - 1, _any_alive([runs]) > 0)
                return i + 1, more.astype(jnp.int32)

            lax.while_loop(lambda st: st[1] > 0, wbody, (jnp.int32(0), jnp.int32(1)))
            for g in range(grp):
                o_ref[qrows[g], :] = acc_s[g]
            return carry

        lax.fori_loop(0, nq // grp, group, 0)

    blk = lambda base: pl.BlockSpec((seq, LANES), lambda j, base=base: (0, base + j))
    res, rode = _host_call(
        body, rider, name=name, grid=(pairs,),
        in_specs=[blk(0), blk(pairs), blk(2 * pairs), pl.BlockSpec((2 * t, 2 * t), lambda j: (0, 0))],
        out_specs=[pl.BlockSpec((seq, LANES), lambda j: (0, j))],
        out_shape=[jax.ShapeDtypeStruct((seq, d_att), F32)],
        scratch=[pltpu.VMEM((seq, LANES), BF16)] * 5 + [pltpu.VMEM((2 * grp, t, 2 * t), BF16)]
        + [pltpu.VMEM((2 * grp, t, t), F32)] * 4 + [pltpu.VMEM((grp, t, LANES), F32)],
        operands=[u, u, u, scan_suffix], semantics=("arbitrary",))
    return res[0] if rider is None else (res[0], rode)


def _columns_copy(stage_ref, du_ref, rows, col, sem):
    width = stage_ref.shape[-1]
    cols = pl.ds(pl.multiple_of(col, LANES), width)
    return pltpu.make_async_copy(stage_ref, du_ref.at[slice(None) if rows is None else rows, cols], sem)


def _attn_bwd(u, d_att_out, du, scan_suffix, scan_prefix, *, seq, d_att, name, rider=None):
    t = ATT_TILE
    nq = seq // t
    pairs = d_att // LANES
    grp = ATT_GROUP
    assert nq % grp == 0

    def body(q_ref, k_ref, v_ref, do_ref, um_ref, pm_ref, du_old, du_ref,
             qq, q0, q1, k0, k1, v0, v1, dd, do0, do1, dq_acc, dk_acc, dv_acc, g_st, b_st,
             f2_s, a_s, lg_s, tot_s, dw_s, run_s, p_s, pt_s, grun_s, dq_s, dq_o, dk_o, dv_o, out_sem):
        _head_copies(seq, q_ref, ATT_SCALE, q0, q1, qq)
        _head_copies(seq, k_ref, 1.0, k0, k1, None)
        _head_copies(seq, v_ref, 1.0, v0, v1, None)
        _head_copies(seq, do_ref, 1.0, do0, do1, dd)
        dk_acc[...] = jnp.zeros_like(dk_acc)
        dv_acc[...] = jnp.zeros_like(dv_acc)

        tn = (((0,), (0,)), ((), ()))
        nt = (((1,), (1,)), ((), ()))

        def stacked(lo_ref, hi_ref, rows):
            return jnp.concatenate([lo_ref[rows, :], hi_ref[rows, :]], axis=0)

        def group(gi, carry):
            qb0 = gi * grp
            qrows = [pl.ds(pl.multiple_of((qb0 + g) * t, t), t) for g in range(grp)]
            qv = [qq[qrows[g], :] for g in range(grp)]
            dov = [dd[qrows[g], :] for g in range(grp)]
            q_heads = [stacked(q0, q1, qrows[g]) for g in range(grp)]
            do_heads = [stacked(do0, do1, qrows[g]) for g in range(grp)]

            def key_rows(g, i):
                return pl.ds(pl.multiple_of(jnp.maximum(qb0 + g - i, 0) * t, t), t)

            chains = [(h, g) for g in range(grp) for h in range(2)]
            every = range(len(chains))

            def stage1(i, masked):
                out = []
                for g in range(grp):
                    krows = key_rows(g, i)
                    out += _pair_scores(qv[g], k0[krows, :], k1[krows, :], masked)
                return out

            def stage2(i, held, masked):
                out = []
                for g in range(grp):
                    dw2 = lax.dot_general(dov[g], stacked(v0, v1, key_rows(g, i)), nt, preferred_element_type=F32)
                    for h in range(2):
                        c = 2 * g + h
                        halves, a = held[c]
                        scan = jnp.dot(halves, um_ref[...], preferred_element_type=F32)
                        logit = a + scan[:, :t]
                        if masked:
                            causal = lax.broadcasted_iota(jnp.int32, (t, t), 1) < lax.broadcasted_iota(jnp.int32, (t, t), 0)
                            logit = jnp.where(causal, logit, ATT_FILL)
                        b_st[c, i] = jnp.exp(a)
                        out.append((logit, scan[:, t:], dw2[:, h * t:(h + 1) * t]))
                return out

            def put(held=None, ready=None):
                for c in every:
                    if held is not None:
                        f2_s[c], a_s[c] = held[c]
                    if ready is not None:
                        lg_s[c], tot_s[c], dw_s[c] = ready[c]

            put(held=stage1(1, False), ready=stage2(0, stage1(0, True), True))
            for c in every:
                run_s[c] = jnp.zeros((t, t), F32)

            def wbody(st):
                i = st[0]
                held = [(f2_s[c], a_s[c]) for c in every]
                ready = [(lg_s[c], tot_s[c], dw_s[c]) for c in every]
                runs = [run_s[c] for c in every]
                for g in range(grp):
                    gone = jnp.where(qb0 + g - i >= 0, 0.0, ATT_FILL)
                    weights = []
                    for c in (2 * g, 2 * g + 1):
                        logit, total, dw = ready[c]
                        run = runs[c] + gone
                        w = jnp.exp(logit + run)
                        g_st[c, i] = w * dw
                        weights.append(w.astype(BF16))
                        runs[c] = run + total
                    dv_acc[key_rows(g, i), :] += lax.dot_general(jnp.concatenate(weights, axis=0), do_heads[g], tn,
                                                                 preferred_element_type=F32)
                ahead2 = stage2(i + 1, held, False)
                ahead1 = stage1(i + 2, False)
                put(held=ahead1, ready=ahead2)
                for c in every:
                    run_s[c] = runs[c]
                more = jnp.logical_and(i + 1 <= qb0 + grp - 1, _any_alive([runs]) > 0)
                return i + 1, more.astype(jnp.int32)

            steps = lax.while_loop(lambda st: st[1] > 0, wbody, (jnp.int32(0), jnp.int32(1)))[0]

            def prefix(i):
                out = []
                for c in every:
                    scan = _split_scan(g_st[c, i], pm_ref)
                    out.append((scan[:, :t], scan[:, t:]))
                return out

            def back(i, masked):
                sums = [(p_s[c], pt_s[c]) for c in every]
                gruns = [grun_s[c] for c in every]
                dqs = [dq_s[g] for g in range(grp)]
                for g in range(grp):
                    krows = key_rows(g, i)
                    dzs = []
                    for c in (2 * g, 2 * g + 1):
                        gt = g_st[c, i]
                        dz = gt - b_st[c, i] * (gt + gruns[c] + sums[c][0])
                        if masked:
                            causal = lax.broadcasted_iota(jnp.int32, (t, t), 1) < lax.broadcasted_iota(jnp.int32, (t, t), 0)
                            dz = jnp.where(causal, dz, 0.0)
                        dzs.append(dz.astype(BF16))
                        gruns[c] = gruns[c] + sums[c][1]
                    dqs[g] = dqs[g] + jnp.dot(jnp.concatenate(dzs, axis=1), stacked(k0, k1, krows), preferred_element_type=F32)
                    dk_acc[krows, :] += lax.dot_general(jnp.concatenate(dzs, axis=0), q_heads[g], tn, preferred_element_type=F32)
                return gruns, dqs

            def keep(sums=None, gruns=None, dqs=None):
                for c in every:
                    if sums is not None:
                        p_s[c], pt_s[c] = sums[c]
                    if gruns is not None:
                        grun_s[c] = gruns[c]
                if dqs is not None:
                    for g in range(grp):
                        dq_s[g] = dqs[g]

            keep(sums=prefix(steps - 1), gruns=[jnp.zeros((t, t), F32)] * len(chains), dqs=[jnp.zeros((t, LANES), F32)] * grp)

            def bbody(j, carry2):
                i = steps - 1 - j
                gruns, dqs = back(i, False)
                keep(sums=prefix(i - 1), gruns=gruns, dqs=dqs)
                return carry2

            lax.fori_loop(0, steps - 1, bbody, 0)
            _, dqs = back(0, True)
            for g in range(grp):
                dq_acc[qrows[g], :] = dqs[g]
            return carry

        lax.fori_loop(0, nq // grp, group, 0)

        pair = pl.program_id(0)
        stages = (dq_o, dk_o, dv_o)

        def copies(j):
            return [_columns_copy(stage, du_ref, None, k * d_att + j * LANES, out_sem.at[k]) for k, stage in enumerate(stages)]

        @pl.when(pair > 0)
        def _():
            for cp in copies(pair - 1):
                cp.wait()

        chunk = min(256, seq)

        def emit(r, carry):
            rows = pl.ds(pl.multiple_of(r * chunk, chunk), chunk)
            dq_o[rows, :] = (dq_acc[rows, :] * ATT_SCALE).astype(BF16)
            dk_o[rows, :] = dk_acc[rows, :].astype(BF16)
            dv_o[rows, :] = dv_acc[rows, :].astype(BF16)
            return carry

        lax.fori_loop(0, seq // chunk, emit, 0)
        for cp in copies(pair):
            cp.start()

        @pl.when(pair == pairs - 1)
        def _():
            for cp in copies(pair):
                cp.wait()

    blk = lambda base: pl.BlockSpec((seq, LANES), lambda j, base=base: (0, base + j), pipeline_mode=pl.Buffered(1))
    mat = pl.BlockSpec((2 * t, 2 * t), lambda j: (0, 0))
    hbm = pl.BlockSpec(memory_space=pl.ANY)
    res, rode = _host_call(
        body, rider, name=name, grid=(pairs,),
        in_specs=[blk(0), blk(pairs), blk(2 * pairs), blk(0), mat, mat, hbm],
        out_specs=[hbm], out_shape=[jax.ShapeDtypeStruct(du.shape, du.dtype)], host_aliases={6: 0},
        scratch=[pltpu.VMEM((seq, LANES), BF16)] * 10 + [pltpu.VMEM((seq, LANES), F32)] * 3
        + [pltpu.VMEM((2 * grp, nq + 1, t, t), F32)] * 2 + [pltpu.VMEM((2 * grp, t, 2 * t), BF16)]
        + [pltpu.VMEM((2 * grp, t, t), F32)] * 8 + [pltpu.VMEM((grp, t, LANES), F32)]
        + [pltpu.VMEM((seq, LANES), BF16)] * 3 + [pltpu.SemaphoreType.DMA((3,))],
        operands=[u, u, u, d_att_out, scan_suffix, scan_prefix, du], semantics=("arbitrary",))
    return res[0] if rider is None else (res[0], rode)


CONV_ROWS = 256


def _shifted(window, residue, rows):
    total = rows + CONV_PAD
    return window if residue == 0 else pltpu.roll(window, total - residue, 0)


def _glu_to_pad(seq, a_ref, b_ref, pad_ref):
    chunk = min(CONV_ROWS, seq)
    pad_ref[pl.ds(0, CONV_PAD), :] = jnp.zeros((CONV_PAD, LANES), F32)

    def step(r, carry):
        rows = pl.ds(pl.multiple_of(r * chunk, chunk), chunk)
        pad_ref[pl.ds(pl.multiple_of(r * chunk + CONV_PAD, SUBLANES), chunk), :] = a_ref[rows, :] * _sigmoid(b_ref[rows, :])
        return carry

    lax.fori_loop(0, seq // chunk, step, 0)


def _conv_fwd(u, conv_w, conv_b, *, layer, seq, d_conv, col_a, col_b, name, rider=None):
    blocks = d_conv // LANES
    rows_t = min(CONV_ROWS, seq)
    shift0 = CONV_PAD - (CONV_WIDTH - 1)

    def body(a_ref, b_ref, w_ref, bias_ref, o_ref, pad_ref):
        _glu_to_pad(seq, a_ref, b_ref, pad_ref)

        def step(r, carry):
            base = pl.multiple_of(r * rows_t, rows_t)
            window = pad_ref[pl.ds(base, rows_t + CONV_PAD), :]
            acc = jnp.zeros((rows_t, LANES), F32) + bias_ref[...]
            for residue in range(SUBLANES):
                moved = _shifted(window, residue, rows_t)
                for tap in range(CONV_WIDTH):
                    if (shift0 + tap) % SUBLANES == residue:
                        lo = (shift0 + tap) - residue
                        acc = acc + w_ref[tap:tap + 1, :] * moved[lo:lo + rows_t, :]
            o_ref[pl.ds(base, rows_t), :] = acc
            return carry

        lax.fori_loop(0, seq // rows_t, step, 0)

    res, rode = _host_call(
        body, rider, name=name, grid=(blocks,),
        in_specs=[pl.BlockSpec((seq, LANES), lambda j: (0, col_a + j)), pl.BlockSpec((seq, LANES), lambda j: (0, col_b + j)),
                  pl.BlockSpec((None, CONV_PAD, LANES), lambda j: (layer, 0, j)),
                  pl.BlockSpec((None, 1, LANES), lambda j: (layer, 0, j))],
        out_specs=[pl.BlockSpec((seq, LANES), lambda j: (0, j))],
        out_shape=[jax.ShapeDtypeStruct((seq, d_conv), F32)],
        scratch=[pltpu.VMEM((seq + CONV_PAD, LANES), F32)],
        operands=[u, u, conv_w, conv_b], semantics=("arbitrary",))
    return res[0] if rider is None else (res[0], rode)


def _conv_bwd(u, dc1, du, conv_w, *, layer, seq, d_conv, col_a, col_b, name):
    blocks = d_conv // LANES
    rows_t = min(CONV_ROWS, seq)
    shift0 = CONV_PAD - (CONV_WIDTH - 1)

    def body(a_ref, b_ref, d_ref, w_ref, du_old, du_ref, dw_ref, pad_ref, dpad_ref, dw_acc, da_ref, db_ref, out_sem):
        block = pl.program_id(0)

        def copies(j):
            return [_columns_copy(stage, du_ref, None, (base + j) * LANES, out_sem.at[k])
                    for k, (stage, base) in enumerate(((da_ref, col_a), (db_ref, col_b)))]

        _glu_to_pad(seq, a_ref, b_ref, pad_ref)
        dpad_ref[pl.ds(seq, CONV_PAD), :] = jnp.zeros((CONV_PAD, LANES), F32)

        def fill(r, carry):
            rows = pl.ds(pl.multiple_of(r * rows_t, rows_t), rows_t)
            dpad_ref[rows, :] = d_ref[rows, :]
            return carry

        lax.fori_loop(0, seq // rows_t, fill, 0)
        dw_acc[...] = jnp.zeros_like(dw_acc)

        @pl.when(block > 0)
        def _():
            for cp in copies(block - 1):
                cp.wait()

        def step(r, carry):
            base = pl.multiple_of(r * rows_t, rows_t)
            rows = pl.ds(base, rows_t)
            window = dpad_ref[pl.ds(base, rows_t + CONV_PAD), :]
            acc = jnp.zeros((rows_t, LANES), F32)
            for residue in range(SUBLANES):
                moved = _shifted(window, residue, rows_t)
                for tap in range(CONV_WIDTH):
                    off = CONV_WIDTH - 1 - tap
                    if off % SUBLANES == residue:
                        lo = off - residue
                        acc = acc + w_ref[tap:tap + 1, :] * moved[lo:lo + rows_t, :]
            sig = _sigmoid(b_ref[rows, :])
            a = a_ref[rows, :]
            da_ref[rows, :] = (acc * sig).astype(BF16)
            db_ref[rows, :] = (acc * a * sig * (1.0 - sig)).astype(BF16)
            dcur = d_ref[rows, :]
            cwin = pad_ref[pl.ds(base, rows_t + CONV_PAD), :]
            for residue in range(SUBLANES):
                moved = _shifted(cwin, residue, rows_t)
                for tap in range(CONV_WIDTH):
                    if (shift0 + tap) % SUBLANES == residue:
                        lo = (shift0 + tap) - residue
                        prod = dcur * moved[lo:lo + rows_t, :]
                        dw_acc[tap] += jnp.sum(prod.reshape(rows_t // SUBLANES, SUBLANES, LANES), axis=0)
            return carry

        lax.fori_loop(0, seq // rows_t, step, 0)
        dw_ref[...] = jnp.sum(dw_acc[...], axis=1)
        for cp in copies(block):
            cp.start()

        @pl.when(block == blocks - 1)
        def _():
            for cp in copies(block):
                cp.wait()

    col = lambda base: pl.BlockSpec((seq, LANES), lambda j, base=base: (0, base + j))
    own = pl.BlockSpec((seq, LANES), lambda j: (0, j))
    hbm = pl.BlockSpec(memory_space=pl.ANY)
    return pl.pallas_call(
        body, name=name, grid=(blocks,),
        in_specs=[col(col_a), col(col_b), own, pl.BlockSpec((None, CONV_PAD, LANES), lambda j: (layer, 0, j)), hbm],
        out_specs=[hbm, pl.BlockSpec((CONV_PAD, LANES), lambda j: (0, j))],
        out_shape=[jax.ShapeDtypeStruct(du.shape, du.dtype), jax.ShapeDtypeStruct((CONV_PAD, d_conv), F32)],
        scratch_shapes=[pltpu.VMEM((seq + CONV_PAD, LANES), F32), pltpu.VMEM((seq + CONV_PAD, LANES), F32),
                        pltpu.VMEM((CONV_PAD, SUBLANES, LANES), F32), pltpu.VMEM((seq, LANES), BF16), pltpu.VMEM((seq, LANES), BF16),
                        pltpu.SemaphoreType.DMA((2,))],
        input_output_aliases={4: 0}, compiler_params=_cparams(("arbitrary",)),
    )(u, u, dc1, conv_w, du)


MIX_ROWS = 256


def _layer_norm_stats(val):
    mu = jnp.mean(val, axis=-1, keepdims=True)
    cen = val - mu
    var = jnp.mean(cen * cen, axis=-1, keepdims=True)
    rstd = lax.rsqrt(var + LN_EPS)
    return cen * rstd, rstd


def _layer_norm_bwd(dy, xhat, rstd, gain):
    dxhat = dy * gain
    m1 = jnp.mean(dxhat, axis=-1, keepdims=True)
    m2 = jnp.mean(dxhat * xhat, axis=-1, keepdims=True)
    dx = rstd * (dxhat - m1 - xhat * m2)
    return dx, jnp.sum(dy * xhat, axis=0, keepdims=True), jnp.sum(dy, axis=0, keepdims=True)


def _mix_forward(zatt, att, c1, zconv, gatt, gconv, x, w_att, w_conv, w_out, cln_g, cln_b, b_proj):
    s_zatt = _sigmoid(zatt)
    a_in = att * (zatt * s_zatt)
    chat, c_rstd = _layer_norm_stats(c1)
    c2 = chat * cln_g + cln_b
    s_c2 = _sigmoid(c2)
    c3 = c2 * s_c2
    s_zconv = _sigmoid(zconv)
    c_in = c3 * (zconv * s_zconv)
    a_in_b, c_in_b = a_in.astype(BF16), c_in.astype(BF16)
    ab = jnp.dot(a_in_b, w_att, preferred_element_type=F32)
    cb = jnp.dot(c_in_b, w_conv, preferred_element_type=F32) + b_proj
    s_gatt, s_gconv = _sigmoid(gatt), _sigmoid(gconv)
    merged_b = (s_gatt * ab + s_gconv * cb).astype(BF16)
    y = jnp.dot(merged_b, w_out, preferred_element_type=F32)
    h = DEEPNORM_ALPHA * x + y
    return dict(s_zatt=s_zatt, a_in_b=a_in_b, chat=chat, c_rstd=c_rstd, c2=c2, s_c2=s_c2, c3=c3, s_zconv=s_zconv,
                c_in_b=c_in_b, ab=ab, cb=cb, s_gatt=s_gatt, s_gconv=s_gconv, merged_b=merged_b, h=h)


def _u_blocks(rows_t, width, half):
    return [pl.BlockSpec((rows_t, half), lambda i, c=c: (i, c)) for c in (3, 6, 7, 8, 9, 10)]


def _of_layer(arr, layer):
    return pl.BlockSpec((None,) + arr.shape[1:], lambda i: (layer, 0, 0))


def _mix_fwd(u, att, c1, x, w_att, w_conv, w_out, cln_g, cln_b, b_proj, ln_g, ln_b, *, layer, seq, d_model, name):
    half = d_model // 2
    rows_t = min(MIX_ROWS, seq)

    def body(zatt_ref, zconv_ref, ga0, ga1, gc0, gc1, att_ref, c1_ref, x_ref, wa_ref, wc_ref, wo_ref,
             cg_ref, cb_ref, bp_ref, lg_ref, lb_ref, o_ref, ob_ref):
        gatt = jnp.concatenate([ga0[...], ga1[...]], axis=1)
        gconv = jnp.concatenate([gc0[...], gc1[...]], axis=1)
        mid = _mix_forward(zatt_ref[...], att_ref[...], c1_ref[...], zconv_ref[...], gatt, gconv, x_ref[...],
                           wa_ref[...], wc_ref[...], wo_ref[...], cg_ref[...], cb_ref[...], bp_ref[...])
        xhat, _ = _layer_norm_stats(mid["h"])
        out = xhat * lg_ref[...] + lb_ref[...]
        o_ref[...] = out
        ob_ref[...] = out.astype(BF16)

    row = lambda width: pl.BlockSpec((rows_t, width), lambda i: (i, 0))
    full = lambda arr: _of_layer(arr, layer)
    out = pl.BlockSpec((rows_t, d_model), lambda i: (i, 0))
    return pl.pallas_call(
        body, name=name, grid=(seq // rows_t,),
        in_specs=_u_blocks(rows_t, d_model, half) + [row(half), row(half), row(d_model), full(w_att), full(w_conv), full(w_out),
                                                     full(cln_g), full(cln_b), full(b_proj), full(ln_g), full(ln_b)],
        out_specs=[out, out],
        out_shape=[jax.ShapeDtypeStruct((seq, d_model), F32), jax.ShapeDtypeStruct((seq, d_model), BF16)],
        compiler_params=_cparams(("parallel",)),
    )(u, u, u, u, u, u, att, c1, x, w_att, w_conv, w_out, cln_g, cln_b, b_proj, ln_g, ln_b)


def _mix_bwd(u, att, c1, x, dxn, w_att, w_conv, w_out, cln_g, cln_b, b_proj, ln_g, *, layer, seq, d_model, name):
    half = d_model // 2
    rows_t = min(MIX_ROWS, seq)
    steps = seq // rows_t
    nt = ((1,), (1,))
    tn = ((0,), (0,))

    def body(zatt_ref, zconv_ref, ga0, ga1, gc0, gc1, att_ref, c1_ref, x_ref, dxn_ref, wa_ref, wc_ref, wo_ref,
             cg_ref, cb_ref, bp_ref, lg_ref,
             du_ref, datt_ref, dc1_ref, dxres_ref, dwa_ref, dwc_ref, dwo_ref,
             dcg_ref, dcb_ref, dcbias_ref, dbp_ref, dlg_ref, dlb_ref, zatt_stage, tail_stage, out_sem):
        sums = (dwa_ref, dwc_ref, dwo_ref, dcg_ref, dcb_ref, dcbias_ref, dbp_ref, dlg_ref, dlb_ref)
        tile = pl.program_id(0)
        slot = tile % 2
        dzatt_ref, tail_ref = zatt_stage.at[slot], tail_stage.at[slot]
        dzconv_ref, dgatt_ref, dgconv_ref = tail_ref.at[:, :half], tail_ref.at[:, half:3 * half], tail_ref.at[:, 3 * half:]

        def copies(i):
            rows = pl.ds(pl.multiple_of(i * rows_t, rows_t), rows_t)
            return [_columns_copy(zatt_stage.at[i % 2], du_ref, rows, 3 * half, out_sem.at[i % 2, 0]),
                    _columns_copy(tail_stage.at[i % 2], du_ref, rows, 6 * half, out_sem.at[i % 2, 1])]

        @pl.when(tile == 0)
        def _():
            for ref in sums:
                ref[...] = jnp.zeros_like(ref)

        zatt, zconv, att = zatt_ref[...], zconv_ref[...], att_ref[...]
        gatt = jnp.concatenate([ga0[...], ga1[...]], axis=1)
        gconv = jnp.concatenate([gc0[...], gc1[...]], axis=1)
        wa, wc, wo = wa_ref[...], wc_ref[...], wo_ref[...]
        mid = _mix_forward(zatt, att, c1_ref[...], zconv, gatt, gconv, x_ref[...], wa, wc, wo,
                           cg_ref[...], cb_ref[...], bp_ref[...])
        xhat, rstd = _layer_norm_stats(mid["h"])
        dh, dlg, dlb = _layer_norm_bwd(dxn_ref[...], xhat, rstd, lg_ref[...])
        dlg_ref[...] += dlg
        dlb_ref[...] += dlb
        dxres_ref[...] = DEEPNORM_ALPHA * dh
        dy = dh.astype(BF16)
        dwo_ref[...] += lax.dot_general(mid["merged_b"], dy, (tn, ((), ())), preferred_element_type=F32)
        dmerged = lax.dot_general(dy, wo, (nt, ((), ())), preferred_element_type=F32)
        s_ga, s_gc, ab, cb = mid["s_gatt"], mid["s_gconv"], mid["ab"], mid["cb"]
        dgatt_ref[...] = (dmerged * ab * s_ga * (1.0 - s_ga)).astype(BF16)
        dgconv_ref[...] = (dmerged * cb * s_gc * (1.0 - s_gc)).astype(BF16)
        dab = dmerged * s_ga
        dcb = dmerged * s_gc
        dbp_ref[...] += jnp.sum(dcb, axis=0, keepdims=True)
        dab_b, dcb_b = dab.astype(BF16), dcb.astype(BF16)
        dwa_ref[...] += lax.dot_general(mid["a_in_b"], dab_b, (tn, ((), ())), preferred_element_type=F32)
        da_in = lax.dot_general(dab_b, wa, (nt, ((), ())), preferred_element_type=F32)
        s_za = mid["s_zatt"]
        datt_ref[...] = da_in * (zatt * s_za)
        dzatt_ref[...] = (da_in * att * (s_za * (1.0 + zatt * (1.0 - s_za)))).astype(BF16)
        dwc_ref[...] += lax.dot_general(mid["c_in_b"], dcb_b, (tn, ((), ())), preferred_element_type=F32)
        dc_in = lax.dot_general(dcb_b, wc, (nt, ((), ())), preferred_element_type=F32)
        s_zc, c2, s_c2 = mid["s_zconv"], mid["c2"], mid["s_c2"]
        dzconv_ref[...] = (dc_in * mid["c3"] * (s_zc * (1.0 + zconv * (1.0 - s_zc)))).astype(BF16)
        dc3 = dc_in * (zconv * s_zc)
        dc2 = dc3 * (s_c2 * (1.0 + c2 * (1.0 - s_c2)))
        dc1, dcg, dcbeta = _layer_norm_bwd(dc2, mid["chat"], mid["c_rstd"], cg_ref[...])
        dcg_ref[...] += dcg
        dcb_ref[...] += dcbeta
        dcbias_ref[...] += jnp.sum(dc1, axis=0, keepdims=True)
        dc1_ref[...] = dc1
        for cp in copies(tile):
            cp.start()

        @pl.when(tile > 0)
        def _():
            for cp in copies(tile - 1):
                cp.wait()

        @pl.when(tile == steps - 1)
        def _():
            for cp in copies(tile):
                cp.wait()

    row = lambda width: pl.BlockSpec((rows_t, width), lambda i: (i, 0))
    full = lambda arr: _of_layer(arr, layer)
    whole = lambda r, c: pl.BlockSpec((r, c), lambda i: (0, 0))
    sds = jax.ShapeDtypeStruct
    out_specs = [pl.BlockSpec(memory_space=pl.ANY), row(half), row(half), row(d_model),
                 whole(half, d_model), whole(half, d_model), whole(d_model, d_model),
                 whole(1, half), whole(1, half), whole(1, half), whole(1, d_model), whole(1, d_model), whole(1, d_model)]
    out_shape = [sds((seq, u.shape[1]), BF16), sds((seq, half), F32), sds((seq, half), F32), sds((seq, d_model), F32),
                 sds((half, d_model), F32), sds((half, d_model), F32), sds((d_model, d_model), F32),
                 sds((1, half), F32), sds((1, half), F32), sds((1, half), F32),
                 sds((1, d_model), F32), sds((1, d_model), F32), sds((1, d_model), F32)]
    return pl.pallas_call(
        body, name=name, grid=(steps,),
        in_specs=_u_blocks(rows_t, d_model, half) + [row(half), row(half), row(d_model), row(d_model), full(w_att), full(w_conv),
                                                     full(w_out), full(cln_g), full(cln_b), full(b_proj), full(ln_g)],
        out_specs=out_specs, out_shape=out_shape,
        scratch_shapes=[pltpu.VMEM((2, rows_t, half), BF16), pltpu.VMEM((2, rows_t, 5 * half), BF16), pltpu.SemaphoreType.DMA((2, 2))],
        compiler_params=_cparams(("arbitrary",)),
    )(u, u, u, u, u, u, att, c1, x, dxn, w_att, w_conv, w_out, cln_g, cln_b, b_proj, ln_g)


def _loss_head(y, target, *, seq, d_model, name):
    rows_t = min(512, seq)

    def body(y_ref, t_ref, dy_ref, loss_ref):
        @pl.when(pl.program_id(0) == 0)
        def _():
            loss_ref[...] = jnp.zeros_like(loss_ref)

        err = y_ref[...] - t_ref[...]
        dy_ref[...] = err * (1.0 / d_model)
        per_token = jnp.sum(err * err, axis=-1, keepdims=True) * (1.0 / d_model)
        loss_ref[...] += 0.5 * jnp.sum(per_token, axis=0, keepdims=True)

    row = pl.BlockSpec((rows_t, d_model), lambda i: (i, 0))
    return pl.pallas_call(
        body, name=name, grid=(seq // rows_t,), in_specs=[row, row],
        out_specs=[row, pl.BlockSpec((1, 1), lambda i: (0, 0))],
        out_shape=[jax.ShapeDtypeStruct((seq, d_model), F32), jax.ShapeDtypeStruct((1, 1), F32)],
        compiler_params=_cparams(("arbitrary",)),
    )(y, target)


def _adamw(w, g, m, v, *, name, echo=False):
    rows, cols = w.shape
    rows_t = rows
    for cand in (512, 256, 128, 64, 32, 16, 8):
        if rows % cand == 0 and cand * cols * 4 <= 2 * 1024 * 1024:
            rows_t = cand
            break

    def body(w_ref, g_ref, m_ref, v_ref, *outs):
        d_ref, nm_ref, nv_ref = outs[-3:]
        grad = g_ref[...]
        if echo:
            outs[0][...] = grad
        new_m = ADAM_B1 * m_ref[...] + (1.0 - ADAM_B1) * grad
        new_v = ADAM_B2 * v_ref[...] + (1.0 - ADAM_B2) * (grad * grad)
        m_hat = new_m / (1.0 - ADAM_B1 ** ADAM_STEP)
        v_hat = new_v / (1.0 - ADAM_B2 ** ADAM_STEP)
        d_ref[...] = -ADAM_LR * (m_hat / (jnp.sqrt(v_hat) + ADAM_EPS) + ADAM_WD * w_ref[...])
        nm_ref[...] = new_m
        nv_ref[...] = new_v

    blk = pl.BlockSpec((rows_t, cols), lambda i: (i, 0))
    out = jax.ShapeDtypeStruct((rows, cols), F32)
    n_out = 4 if echo else 3
    return pl.pallas_call(
        body, name=name, grid=(rows // rows_t,), in_specs=[blk] * 4, out_specs=[blk] * n_out, out_shape=[out] * n_out,
        compiler_params=_cparams(("parallel",)),
    )(w, g, m, v)


MATRICES = ("w_in", "w_att_proj", "w_conv_proj", "w_out")
VECTORS = ("b_in", "conv_b", "conv_ln_g", "conv_ln_b", "b_conv_proj", "ln_g", "ln_b")
MATRIX_CHIP_AXIS = (1, 1, 1, 0)
MATRIX_CORE_AXIS = (0, 0, 0, 1)


def _train_pass(x, target, mats, taps, b_in, conv_b, cln_g, cln_b, b_proj, ln_g, ln_b, place):
    seq, d_model = x.shape
    half = d_model // 2
    depth = b_in.shape[0]
    scan_suffix, scan_prefix = _scan_matrices()
    cols = half // LANES
    dims = dict(seq=seq, d_model=d_model)
    conv_dims = dict(seq=seq, d_conv=half, col_a=4 * cols, col_b=5 * cols)
    axes3 = [axis + 1 for axis in MATRIX_CHIP_AXIS]
    n_mat = len(mats)

    def first_layers(relay):
        return _gather_rider([mats[0], taps], [axes3[0], 2], [(0, 1), (0, depth)], relay)

    mats = list(mats)
    mats[0], taps = _lone_call(_chain(first_layers(False), first_layers(True)), name="gather_first")

    xs, xbs, us, atts, c1s = [x], [x.astype(BF16)], [], [], []
    for l in range(depth):
        u = _matmul(xbs[l], mats[0], layer=l, mode="nn", bias=b_in[l].reshape(1, -1), name="in_proj", tm=256, tn=b_in.shape[1],
                    tk=d_model)
        nxt = [(l + 1, 1)] + [(l + 1, 1) if l else (0, 2)] * (n_mat - 1)
        if l + 1 < depth:
            att, mats = _attn_fwd(u, scan_suffix, seq=seq, d_att=half, name="attn_fwd", rider=_gather_rider(mats, axes3, nxt, False))
            c1, mats = _conv_fwd(u, taps, conv_b, layer=l, name="conv_fwd", rider=_gather_rider(mats, axes3, nxt, True), **conv_dims)
        else:
            att = _attn_fwd(u, scan_suffix, seq=seq, d_att=half, name="attn_fwd_last")
            c1 = _conv_fwd(u, taps, conv_b, layer=l, name="conv_fwd_last", **conv_dims)
        xn, xnb = _mix_fwd(u, att, c1, xs[l], mats[1], mats[2], mats[3], cln_g, cln_b, b_proj, ln_g, ln_b, layer=l, name="mix_fwd", **dims)
        us.append(u)
        atts.append(att)
        c1s.append(c1)
        xs.append(xn)
        xbs.append(xnb)
    w_in, w_att, w_conv, w_out = mats

    def chip_sums(layer, partials, got, blocks):
        return [_chip_sum_layer(partials[t], got[t], blocks[t], layer, depth, MATRIX_CHIP_AXIS[t], MATRIX_CORE_AXIS[t], place,
                                name="chip_sum_" + MATRICES[t]) for t in range(n_mat)]

    dx, loss = _loss_head(xs[depth], target, name="loss_head", **dims)
    grads = [None] * depth
    blocks = [None] * n_mat
    waiting = None
    for l in reversed(range(depth)):
        u = us[l]
        (du, datt, dc1, dxres, dwa, dwc, dwo, dcg, dcb, dcbias, dbp, dlg, dlb) = _mix_bwd(
            u, atts[l], c1s[l], xs[l], dx, w_att, w_conv, w_out, cln_g, cln_b, b_proj, ln_g, layer=l, name="mix_bwd", **dims)
        if waiting is None:
            du = _attn_bwd(u, datt, du, scan_suffix, scan_prefix, seq=seq, d_att=half, name="attn_bwd_first")
        else:
            du, got = _attn_bwd(u, datt, du, scan_suffix, scan_prefix, seq=seq, d_att=half, name="attn_bwd",
                                rider=_exchange_rider(waiting[1], MATRIX_CHIP_AXIS))
            blocks = chip_sums(waiting[0], waiting[1], got, blocks)
        du, dconvw = _conv_bwd(u, dc1, du, taps, layer=l, name="conv_bwd", **conv_dims)
        dwin, dbin = _matmul(xbs[l], du, mode="tn", colsum=True, name="in_proj_dw", tm=1024, tn=512, tk=seq)
        parts = [dwin, dwa, dwc, dwo]
        dx, got = _matmul(du, w_in, layer=l, mode="nt", add=dxres, name="in_proj_dx", tm=512, tn=1024, tk=du.shape[1],
                          rider=_swap_rider(parts, MATRIX_CORE_AXIS))
        waiting = (l, [_pair_sum_layer(parts[t], got[t], MATRIX_CORE_AXIS[t], place, name="pair_sum_" + MATRICES[t])
                       for t in range(n_mat)])
        grads[l] = dict(b_in=dbin[0], conv_w=dconvw[:CONV_WIDTH], conv_b=dcbias[0], conv_ln_g=dcg[0], conv_ln_b=dcb[0],
                        b_conv_proj=dbp[0], ln_g=dlg[0], ln_b=dlb[0])
    packed = jnp.stack([jnp.concatenate([grads[l][n] for n in VECTORS] + [grads[l]["conv_w"].reshape(-1)]) for l in range(depth)])
    *got, gathered = _lone_call(_both(_exchange_rider(waiting[1], MATRIX_CHIP_AXIS), _small_rider(packed)), name="exchange_last")
    blocks = chip_sums(waiting[0], waiting[1], got, blocks)
    return loss, dx, gathered, blocks


MESH = pl.DeviceIdType.MESH
IN_HBM = pl.BlockSpec(memory_space=pl.ANY)


def _position():
    x, y, c = lax.axis_index("x"), lax.axis_index("y"), lax.axis_index("c")
    return x, y, c, [(1 - x, y), (x, 1 - y), (1 - x, 1 - y)]


def _cut(ref, axis, start, size):
    idx = [slice(None)] * len(ref.shape)
    idx[axis] = pl.ds(start, size)
    return ref.at[tuple(idx)]


def _remote(src, dst, send_sem, recv_sem, device):
    return pltpu.make_async_remote_copy(src_ref=src, dst_ref=dst, send_sem=send_sem, recv_sem=recv_sem,
                                        device_id=device, device_id_type=MESH)


def _comm_call(body, name, operands, out_shape, scratch, alias=False):
    return pl.pallas_call(
        body, name=name, in_specs=[IN_HBM] * len(operands), out_specs=[IN_HBM] * len(out_shape), out_shape=out_shape,
        scratch_shapes=scratch, input_output_aliases={t: t for t in range(len(operands))} if alias else {},
    )(*operands)


def _gather_rider(wholes, chip_axes, spans, relay):
    n = len(wholes)

    def region(dst, t, chip, half):
        first, count = spans[t]
        _, rows, cols = wholes[t].shape
        if chip_axes[t] == 2:
            size, part = cols // N_CHIPS, rows // 2
            ref = _cut(_cut(dst[t], 2, pl.multiple_of(chip * size, size), size), 1, pl.multiple_of(half * part, part), part)
        else:
            size = rows // N_CHIPS
            part = size // 2
            ref = _cut(dst[t], 1, pl.multiple_of(chip * size + half * part, part), part)
        return _cut(ref, 0, first, count)

    def copies(dst, sems, receiving):
        x, y, c, chips = _position()
        send_sem, recv_sem = sems
        out = []
        for t in range(n):
            for j, chip in enumerate(chips):
                theirs = 2 * chip[0] + chip[1]
                if relay:
                    ref = region(dst, t, theirs, 1 - c if receiving else c)
                    peer = (x, y, 1 - c)
                else:
                    ref = region(dst, t, theirs if receiving else 2 * x + y, c)
                    peer = (*chip, c)
                out.append(_remote(ref, ref, send_sem.at[t, j], recv_sem.at[t, j], peer))
        return out

    def start(ins, outs, sems):
        for cp in copies(outs, sems, False):
            cp.start()

    def finish(ins, outs, sems):
        for cp in copies(outs, sems, True):
            cp.wait_recv()
        for cp in copies(outs, sems, False):
            cp.wait_send()

    return _Rider(operands=list(wholes), out_shape=[jax.ShapeDtypeStruct(w.shape, w.dtype) for w in wholes],
                  aliases={t: t for t in range(n)}, scratch=[pltpu.SemaphoreType.DMA((n, N_CHIPS - 1))] * 2,
                  start=start, finish=finish)


def _chain(first, second):
    cut = len(first.scratch)

    def start(ins, outs, sems):
        first.start(ins, outs, sems[:cut])
        first.finish(ins, outs, sems[:cut])
        second.start(ins, outs, sems[cut:])

    def finish(ins, outs, sems):
        second.finish(ins, outs, sems[cut:])

    return first._replace(scratch=list(first.scratch) + list(second.scratch), start=start, finish=finish)


def _swap_rider(parts, core_axes):
    n = len(parts)
    halves = []
    for arr, axis in zip(parts, core_axes):
        shape = list(arr.shape)
        shape[axis] //= 2
        halves.append(jax.ShapeDtypeStruct(tuple(shape), arr.dtype))

    def copies(ins, outs, sems):
        x, y, c, _ = _position()
        out = []
        for t in range(n):
            size = halves[t].shape[core_axes[t]]
            piece = _cut(ins[t], core_axes[t], pl.multiple_of((1 - c) * size, size), size)
            out.append(_remote(piece, outs[t], sems[0].at[t], sems[1].at[t], (x, y, 1 - c)))
        return out

    def start(ins, outs, sems):
        for cp in copies(ins, outs, sems):
            cp.start()

    def finish(ins, outs, sems):
        for cp in copies(ins, outs, sems):
            cp.wait()

    return _Rider(operands=list(parts), out_shape=halves, aliases={}, scratch=[pltpu.SemaphoreType.DMA((n,))] * 2,
                  start=start, finish=finish)


def _exchange_rider(partials, chip_axes):
    n = len(partials)
    quarters = []
    for arr, axis in zip(partials, chip_axes):
        shape = list(arr.shape)
        shape[axis] //= N_CHIPS
        quarters.append(jax.ShapeDtypeStruct((N_CHIPS - 1, *shape), arr.dtype))

    def copies(ins, outs, sems):
        x, y, c, chips = _position()
        out = []
        for t in range(n):
            size = quarters[t].shape[1 + chip_axes[t]]
            for j, chip in enumerate(chips):
                piece = _cut(ins[t], chip_axes[t], pl.multiple_of((2 * chip[0] + chip[1]) * size, size), size)
                out.append(_remote(piece, outs[t].at[j], sems[0].at[t, j], sems[1].at[t, j], (*chip, c)))
        return out

    def start(ins, outs, sems):
        for cp in copies(ins, outs, sems):
            cp.start()

    def finish(ins, outs, sems):
        for cp in copies(ins, outs, sems):
            cp.wait()

    return _Rider(operands=list(partials), out_shape=quarters, aliases={}, scratch=[pltpu.SemaphoreType.DMA((n, N_CHIPS - 1))] * 2,
                  start=start, finish=finish)


def _place_block(shard, chip_axis, place, dtype, *, name):
    depth, rows, cols = shard.shape
    rows_t = _row_tile(rows, cols * 4 * 4, 16 * 1024 * 1024)
    steps = rows // rows_t
    shape = list(shard.shape)
    shape[chip_axis] *= N_CHIPS
    if chip_axis == 1:
        out_spec = pl.BlockSpec((None, rows_t, cols), lambda l, i, p: (l, p[1] * steps + i, 0))
    else:
        out_spec = pl.BlockSpec((None, rows_t, cols), lambda l, i, p: (l, i, p[1]))

    def body(place_ref, src_ref, out_ref):
        out_ref[...] = src_ref[...].astype(dtype)

    return pl.pallas_call(
        body, name=name, out_shape=jax.ShapeDtypeStruct(tuple(shape), dtype),
        grid_spec=pltpu.PrefetchScalarGridSpec(num_scalar_prefetch=1, grid=(depth, steps),
                                               in_specs=[pl.BlockSpec((None, rows_t, cols), lambda l, i, p: (l, i, 0))],
                                               out_specs=out_spec),
        compiler_params=_cparams(("arbitrary", "arbitrary")),
    )(place, shard)


def _pair_join(blocks, core_axes):
    n = len(blocks)

    def body(*refs):
        dst = refs[n:2 * n]
        send_sem, recv_sem = refs[2 * n:]
        x, y, c, _ = _position()
        copies = []
        for t in range(n):
            size = blocks[t].shape[core_axes[t]] // 2
            mine = _cut(dst[t], core_axes[t], pl.multiple_of(c * size, size), size)
            cp = _remote(mine, mine, send_sem.at[t], recv_sem.at[t], (x, y, 1 - c))
            cp.start()
            copies.append(cp)
        for t, cp in enumerate(copies):
            size = blocks[t].shape[core_axes[t]] // 2
            other = _cut(dst[t], core_axes[t], pl.multiple_of((1 - c) * size, size), size)
            cp.wait_send()
            _remote(other, other, send_sem.at[t], recv_sem.at[t], (x, y, 1 - c)).wait_recv()

    return _comm_call(body, "pair_join", blocks, [jax.ShapeDtypeStruct(b.shape, b.dtype) for b in blocks],
                      [pltpu.SemaphoreType.DMA((n,))] * 2, alias=True)


def _small_rider(vec):
    n_dev = 2 * N_CHIPS

    def copies(ins, outs, sems, receiving):
        x, y, c, _ = _position()
        flip = lambda v, bit: 1 - v if bit else v
        out = []
        for mask in range(1, n_dev):
            peer = (flip(x, mask & 4), flip(y, mask & 2), flip(c, mask & 1))
            row = 4 * peer[0] + 2 * peer[1] + peer[2] if receiving else 4 * x + 2 * y + c
            out.append(_remote(ins[0], outs[0].at[row], sems[0].at[mask - 1], sems[1].at[mask - 1], peer))
        return out

    def own(ins, outs, sems):
        x, y, c, _ = _position()
        return pltpu.make_async_copy(ins[0], outs[0].at[4 * x + 2 * y + c], sems[2])

    def start(ins, outs, sems):
        own(ins, outs, sems).start()
        for cp in copies(ins, outs, sems, False):
            cp.start()

    def finish(ins, outs, sems):
        for cp in copies(ins, outs, sems, True):
            cp.wait_recv()
        for cp in copies(ins, outs, sems, False):
            cp.wait_send()
        own(ins, outs, sems).wait()

    return _Rider(operands=[vec], out_shape=[jax.ShapeDtypeStruct((n_dev, *vec.shape), vec.dtype)], aliases={},
                  scratch=[pltpu.SemaphoreType.DMA((n_dev - 1,))] * 2 + [pltpu.SemaphoreType.DMA(())], start=start, finish=finish)


def _both(first, second):
    n_in, n_out, n_sem = len(first.operands), len(first.out_shape), len(first.scratch)

    def start(ins, outs, sems):
        first.start(ins[:n_in], outs[:n_out], sems[:n_sem])
        second.start(ins[n_in:], outs[n_out:], sems[n_sem:])

    def finish(ins, outs, sems):
        first.finish(ins[:n_in], outs[:n_out], sems[:n_sem])
        second.finish(ins[n_in:], outs[n_out:], sems[n_sem:])

    aliases = dict(first.aliases)
    aliases.update({n_in + i: n_out + o for i, o in second.aliases.items()})
    return _Rider(operands=list(first.operands) + list(second.operands), out_shape=list(first.out_shape) + list(second.out_shape),
                  aliases=aliases, scratch=list(first.scratch) + list(second.scratch), start=start, finish=finish)


def _row_tile(rows, row_bytes, budget):
    tile = rows
    for cand in (512, 256, 128, 64, 32, 16, 8):
        if rows % cand == 0:
            tile = cand
            if cand * row_bytes <= budget:
                break
    return tile


def _pair_sum_layer(part, got, core_axis, place, *, name):
    rows, cols = got.shape
    rows_t = _row_tile(rows, cols * 4 * 6, 16 * 1024 * 1024)
    steps = rows // rows_t
    if core_axis == 0:
        part_spec = pl.BlockSpec((rows_t, cols), lambda i, p: (p[0] * steps + i, 0))
    else:
        part_spec = pl.BlockSpec((rows_t, cols), lambda i, p: (i, p[0]))
    own_spec = pl.BlockSpec((rows_t, cols), lambda i, p: (i, 0))

    def body(place_ref, part_ref, got_ref, out_ref):
        out_ref[...] = (part_ref[...] + got_ref[...]).astype(BF16)

    return pl.pallas_call(
        body, name=name, out_shape=jax.ShapeDtypeStruct(got.shape, BF16),
        grid_spec=pltpu.PrefetchScalarGridSpec(num_scalar_prefetch=1, grid=(steps,), in_specs=[part_spec, own_spec], out_specs=own_spec),
        compiler_params=_cparams(("arbitrary",)),
    )(place, part, got)


def _chip_sum_layer(partial, got, blocks, layer, depth, chip_axis, core_axis, place, *, name):
    _, rows, cols = got.shape
    rows_t = _row_tile(rows, cols * 4 * 10, 24 * 1024 * 1024)
    steps = rows // rows_t
    if chip_axis == 0:
        own_spec = pl.BlockSpec((rows_t, cols), lambda i, p: (p[1] * steps + i, 0))
    else:
        own_spec = pl.BlockSpec((rows_t, cols), lambda i, p: (i, p[1]))
    got_spec = pl.BlockSpec((N_CHIPS - 1, rows_t, cols), lambda i, p: (0, i, 0))
    shape = [depth, rows, cols]
    shape[1 + core_axis] *= 2
    if core_axis == 0:
        out_spec = pl.BlockSpec((None, rows_t, cols), lambda i, p: (layer, p[0] * steps + i, 0))
    else:
        out_spec = pl.BlockSpec((None, rows_t, cols), lambda i, p: (layer, i, p[0]))

    def body(place_ref, own_ref, got_ref, *rest):
        out_ref = rest[-1]
        up = lambda val: val.astype(F32)
        out_ref[...] = ((up(own_ref[...]) + up(got_ref[0])) + up(got_ref[1])) + up(got_ref[2])

    in_specs, operands, aliases = [own_spec, got_spec], [place, partial, got], {}
    if blocks is not None:
        in_specs.append(pl.BlockSpec(memory_space=pl.ANY))
        operands.append(blocks)
        aliases = {3: 0}
    return pl.pallas_call(
        body, name=name, out_shape=jax.ShapeDtypeStruct(tuple(shape), F32),
        grid_spec=pltpu.PrefetchScalarGridSpec(num_scalar_prefetch=1, grid=(steps,), in_specs=in_specs, out_specs=out_spec),
        input_output_aliases=aliases, compiler_params=_cparams(("arbitrary",)),
    )(*operands)


def _sum_devices(stack, *, name):
    def body(src_ref, out_ref):
        total = src_ref[0]
        for d in range(1, stack.shape[0]):
            total = total + src_ref[d]
        out_ref[...] = total

    return pl.pallas_call(body, name=name, out_shape=jax.ShapeDtypeStruct(stack.shape[1:], F32))(stack)


def kernel(x, w_in, b_in, conv_w, conv_b, conv_ln_g, conv_ln_b, w_att_proj, w_conv_proj, b_conv_proj, w_out, ln_g, ln_b, loss_target, m_w_in, m_b_in, m_conv_w, m_conv_b, m_conv_ln_g, m_conv_ln_b, m_w_att_proj, m_w_conv_proj, m_b_conv_proj, m_w_out, m_ln_g, m_ln_b, v_w_in, v_b_in, v_conv_w, v_conv_b, v_conv_ln_g, v_conv_ln_b, v_w_att_proj, v_w_conv_proj, v_b_conv_proj, v_w_out, v_ln_g, v_ln_b):
    depth = w_in.shape[0]
    d_model = x.shape[-1]
    half = d_model // 2
    chip = 2 * lax.axis_index("x") + lax.axis_index("y")
    place = jnp.stack([lax.axis_index("c"), chip]).astype(jnp.int32)
    vec3 = lambda v: v.reshape(depth, 1, -1)

    taps = jnp.pad(conv_w, ((0, 0), (0, CONV_PAD - CONV_WIDTH), (0, 0)))
    gathered = [("w_in", w_in, 2, BF16), ("w_att_proj", w_att_proj, 2, BF16), ("w_conv_proj", w_conv_proj, 2, BF16),
                ("w_out", w_out, 1, BF16), ("conv_w", taps, 2, F32)]
    wholes = [_place_block(arr, axis, place, dtype, name="place_" + n) for n, arr, axis, dtype in gathered]
    loss, grad_x, gathered, blocks = _train_pass(x[0], loss_target[0], wholes[:4], wholes[4], b_in, vec3(conv_b), vec3(conv_ln_g),
                                                 vec3(conv_ln_b), vec3(b_conv_proj), vec3(ln_g), vec3(ln_b), place)
    loss = lax.psum(loss[0, 0], ("x", "y", "c"))
    reduced = dict(zip(MATRICES, _pair_join(blocks, [axis + 1 for axis in MATRIX_CORE_AXIS])))

    total = _sum_devices(gathered, name="sum_devices")
    widths = dict(b_in=b_in.shape[1], conv_b=half, conv_ln_g=half, conv_ln_b=half, b_conv_proj=d_model, ln_g=d_model, ln_b=d_model)
    offset = 0
    for n in VECTORS:
        reduced[n] = total[:, offset:offset + widths[n]]
        offset += widths[n]
    taps = total[:, offset:].reshape(depth, CONV_WIDTH, half)
    reduced["conv_w"] = lax.dynamic_slice_in_dim(taps, chip * conv_w.shape[2], conv_w.shape[2], axis=2)

    names = ["w_in", "b_in", "conv_w", "conv_b", "conv_ln_g", "conv_ln_b", "w_att_proj", "w_conv_proj", "b_conv_proj", "w_out", "ln_g", "ln_b"]
    weights = dict(zip(names, (w_in, b_in, conv_w, conv_b, conv_ln_g, conv_ln_b, w_att_proj, w_conv_proj, b_conv_proj, w_out, ln_g, ln_b)))
    first = dict(zip(names, (m_w_in, m_b_in, m_conv_w, m_conv_b, m_conv_ln_g, m_conv_ln_b, m_w_att_proj, m_w_conv_proj, m_b_conv_proj, m_w_out, m_ln_g, m_ln_b)))
    second = dict(zip(names, (v_w_in, v_b_in, v_conv_w, v_conv_b, v_conv_ln_g, v_conv_ln_b, v_w_att_proj, v_w_conv_proj, v_b_conv_proj, v_w_out, v_ln_g, v_ln_b)))
    delta, new_m, new_v = {}, {}, {}
    for n in names:
        shape = weights[n].shape
        flat = lambda arr: arr.reshape(-1, shape[-1])
        *grad, d, m, v = _adamw(flat(weights[n]), flat(reduced[n]), flat(first[n]), flat(second[n]), name="adamw_" + n,
                                echo=n in MATRICES)
        if grad:
            reduced[n] = grad[0]
        delta[n], new_m[n], new_v[n] = d.reshape(shape), m.reshape(shape), v.reshape(shape)
    return (loss, grad_x[None], *[reduced[n].reshape(weights[n].shape) for n in names], *[delta[n] for n in names],
            *[new_m[n] for n in names], *[new_v[n] for n in names])
```

```python
import functools
from typing import Callable, NamedTuple

import jax
import jax.numpy as jnp
from jax import lax
from jax.experimental import pallas as pl
from jax.experimental.pallas import tpu as pltpu

F32 = jnp.float32
BF16 = jnp.bfloat16

HEAD_DIM = 64
LANES = 128
CONV_WIDTH = 31
CONV_PAD = 32
SUBLANES = 8
LN_EPS = 1e-5
DEPTH = 4
DEEPNORM_ALPHA = (2 * DEPTH) ** 0.25
ATT_SCALE = HEAD_DIM ** -0.5
ATT_TILE = 128
ATT_DEAD = -88.0
ATT_GROUP = 2
ATT_GROUP_FWD = 4
ATT_FILL = -1e30

ADAM_LR = 0.001
ADAM_B1 = 0.9
ADAM_B2 = 0.999
ADAM_EPS = 1e-08
ADAM_WD = 0.01
ADAM_STEP = 10

VMEM_LIMIT = 56 * 1024 * 1024

N_CHIPS = 4


def _cparams(sem):
    return pltpu.CompilerParams(dimension_semantics=sem, vmem_limit_bytes=VMEM_LIMIT)


def _sigmoid(x):
    return 1.0 / (1.0 + jnp.exp(-x))


class _Rider(NamedTuple):
    operands: list
    out_shape: list
    aliases: dict
    scratch: list
    start: Callable
    finish: Callable


def _host_call(body, rider, *, name, grid, in_specs, out_specs, out_shape, scratch, operands, semantics, host_aliases=None):
    n_in, n_out = len(in_specs), len(out_specs)
    aliases = dict(host_aliases or {})
    if rider is not None:
        r_in, r_out = len(rider.operands), len(rider.out_shape)
        host_body = body

        def body(*refs):
            base = n_in + r_in
            ins, rins = refs[:n_in], refs[n_in:base]
            outs, routs = refs[base:base + n_out], refs[base + n_out:base + n_out + r_out]
            rest = refs[base + n_out + r_out:]
            split = len(rest) - len(rider.scratch)
            ids = [pl.program_id(axis) for axis in range(len(grid))]
            first = functools.reduce(jnp.logical_and, [i == 0 for i in ids])
            last = functools.reduce(jnp.logical_and, [i == g - 1 for i, g in zip(ids, grid)])

            @pl.when(first)
            def _():
                rider.start(rins, routs, rest[split:])

            host_body(*ins, *outs, *rest[:split])

            @pl.when(last)
            def _():
                rider.finish(rins, routs, rest[split:])

        hbm = pl.BlockSpec(memory_space=pl.ANY)
        in_specs = list(in_specs) + [hbm] * r_in
        out_specs = list(out_specs) + [hbm] * r_out
        out_shape = list(out_shape) + list(rider.out_shape)
        scratch = list(scratch) + list(rider.scratch)
        operands = list(operands) + list(rider.operands)
        aliases.update({n_in + i: n_out + o for i, o in rider.aliases.items()})
    res = pl.pallas_call(
        body, name=name, grid=grid, in_specs=list(in_specs), out_specs=list(out_specs), out_shape=list(out_shape),
        scratch_shapes=list(scratch), input_output_aliases=aliases, compiler_params=_cparams(semantics),
    )(*operands)
    return list(res[:n_out]), list(res[n_out:])


def _lone_call(rider, *, name):
    r_in = len(rider.operands)

    def body(*refs):
        ins, outs, sems = refs[:r_in], refs[r_in:r_in + len(rider.out_shape)], refs[r_in + len(rider.out_shape):]
        rider.start(ins, outs, sems)
        rider.finish(ins, outs, sems)

    hbm = pl.BlockSpec(memory_space=pl.ANY)
    return list(pl.pallas_call(
        body, name=name, in_specs=[hbm] * r_in, out_specs=[hbm] * len(rider.out_shape), out_shape=list(rider.out_shape),
        scratch_shapes=list(rider.scratch), input_output_aliases=dict(rider.aliases),
    )(*rider.operands))


def _fit(tile, dim):
    assert dim % LANES == 0
    tile = min(tile, dim) // LANES * LANES
    while dim % tile:
        tile -= LANES
    return tile


_DIMS = {"nn": ((1,), (0,)), "nt": ((1,), (1,)), "tn": ((0,), (0,))}


def _matmul(a, b, *, mode, name, layer=None, bias=None, add=None, colsum=False, out_dtype=F32, tm=1024, tn=512, tk=1024,
            rider=None):
    b_shape = b.shape if layer is None else b.shape[1:]
    if mode == "nn":
        (m, k), (k2, n) = a.shape, b_shape
    elif mode == "nt":
        (m, k), (n, k2) = a.shape, b_shape
    else:
        (k, m), (k2, n) = a.shape, b_shape
    assert k == k2
    tm, tn, tk = _fit(tm, m), _fit(tn, n), _fit(tk, k)
    gm, gn, nk = m // tm, n // tn, k // tk

    a_spec = pl.BlockSpec((tk, tm), lambda i, j, kk: (kk, i)) if mode == "tn" else pl.BlockSpec((tm, tk), lambda i, j, kk: (i, kk))
    if layer is None:
        b_spec = pl.BlockSpec((tn, tk), lambda i, j, kk: (j, kk)) if mode == "nt" else pl.BlockSpec((tk, tn), lambda i, j, kk: (kk, j))
    elif mode == "nt":
        b_spec = pl.BlockSpec((None, tn, tk), lambda i, j, kk: (layer, j, kk))
    else:
        b_spec = pl.BlockSpec((None, tk, tn), lambda i, j, kk: (layer, kk, j))
    in_specs, operands = [a_spec, b_spec], [a, b]
    if bias is not None:
        in_specs.append(pl.BlockSpec((1, tn), lambda i, j, kk: (0, j)))
        operands.append(bias)
    if add is not None:
        in_specs.append(pl.BlockSpec((tm, tn), lambda i, j, kk: (i, j)))
        operands.append(add)
    out_shape = [jax.ShapeDtypeStruct((m, n), out_dtype)]
    out_specs = [pl.BlockSpec((tm, tn), lambda i, j, kk: (i, j))]
    scratch = [pltpu.VMEM((tm, tn), F32)] if nk > 1 else []
    if colsum:
        assert mode == "tn"
        out_shape.append(jax.ShapeDtypeStruct((gm, 1, n), F32))
        out_specs.append(pl.BlockSpec((1, 1, tn), lambda i, j, kk: (i, 0, j)))
        if nk > 1:
            scratch.append(pltpu.VMEM((1, tn), F32))
    has_bias, has_add = bias is not None, add is not None

    def body(*refs):
        refs = list(refs)
        a_ref, b_ref = refs[0], refs[1]
        pos = 2
        bias_ref = add_ref = None
        if has_bias:
            bias_ref = refs[pos]
            pos += 1
        if has_add:
            add_ref = refs[pos]
            pos += 1
        o_ref = refs[pos]
        pos += 1
        cs_ref = None
        if colsum:
            cs_ref = refs[pos]
            pos += 1

        def finish(out, sums):
            if has_bias:
                out = out + bias_ref[...]
            if has_add:
                out = out + add_ref[...]
            o_ref[...] = out.astype(out_dtype)
            if colsum:
                cs_ref[0] = sums

        bv = b_ref[...]
        prod = lax.dot_general(a_ref[...].astype(BF16), bv.astype(BF16), (_DIMS[mode], ((), ())), preferred_element_type=F32)
        sums = jnp.sum(bv.astype(F32), axis=0, keepdims=True) if colsum else None
        if nk == 1:
            finish(prod, sums)
            return
        acc_ref = refs[pos]
        cs_acc = refs[pos + 1] if colsum else None
        kk = pl.program_id(2)

        @pl.when(kk == 0)
        def _():
            acc_ref[...] = jnp.zeros_like(acc_ref)
            if colsum:
                cs_acc[...] = jnp.zeros_like(cs_acc)

        acc_ref[...] += prod
        if colsum:
            cs_acc[...] += sums

        @pl.when(kk == nk - 1)
        def _():
            finish(acc_ref[...], cs_acc[...] if colsum else None)

    res, rode = _host_call(body, rider, name=name, grid=(gm, gn, nk), in_specs=in_specs, out_specs=out_specs, out_shape=out_shape,
                           scratch=scratch, operands=operands, semantics=("arbitrary", "arbitrary", "arbitrary"))
    out = (res[0], res[1][0]) if colsum else res[0]
    return out if rider is None else (out, rode)


def _scan_matrices():
    t = ATT_TILE
    r = lax.broadcasted_iota(jnp.int32, (t, t), 0)
    c = lax.broadcasted_iota(jnp.int32, (t, t), 1)
    ones = jnp.ones((t, t), F32)
    suffix = jnp.concatenate([(r > c).astype(F32), ones], axis=1)
    prefix = jnp.concatenate([(r < c).astype(F32), ones], axis=1)
    stack = lambda mat: jnp.concatenate([mat, mat], axis=0).astype(BF16)
    return stack(suffix), stack(prefix)


def _split_halves(val):
    hi = val.astype(BF16)
    lo = (val - hi.astype(F32)).astype(BF16)
    return jnp.concatenate([hi, lo], axis=1)


def _split_scan(val, mat_ref):
    return jnp.dot(_split_halves(val), mat_ref[...], preferred_element_type=F32)


def _pair_scores(q, k_lo, k_hi, masked):
    t = ATT_TILE
    z2 = lax.dot_general(q, jnp.concatenate([k_lo, k_hi], axis=0), (((1,), (1,)), ((), ())), preferred_element_type=F32)
    out = []
    for h in range(2):
        z = z2[:, h * t:(h + 1) * t]
        sp = jnp.log(1.0 + jnp.exp(-jnp.abs(z)))
        f = jnp.minimum(-z, 0.0) - sp
        a = f + z
        if masked:
            causal = lax.broadcasted_iota(jnp.int32, (t, t), 1) < lax.broadcasted_iota(jnp.int32, (t, t), 0)
            f = jnp.where(causal, f, 0.0)
        out.append((_split_halves(f), a))
    return out


def _any_alive(runs):
    top = functools.reduce(jnp.maximum, [run for per_head in runs for run in per_head])
    return (jnp.max(top) > ATT_DEAD).astype(jnp.int32)


def _head_copies(seq, src_ref, scale, lo_ref, hi_ref, plain_ref):
    chunk = min(256, seq)
    low = lax.broadcasted_iota(jnp.int32, (chunk, LANES), 1) < HEAD_DIM

    def step(r, carry):
        rows = pl.ds(pl.multiple_of(r * chunk, chunk), chunk)
        val = src_ref[rows, :]
        if scale != 1.0:
            val = val * scale
        if lo_ref is not None:
            lo_ref[rows, :] = jnp.where(low, val, 0.0).astype(BF16)
            hi_ref[rows, :] = jnp.where(low, 0.0, val).astype(BF16)
        if plain_ref is not None:
            plain_ref[rows, :] = val.astype(BF16)
        return carry

    lax.fori_loop(0, seq // chunk, step, 0)


def _attn_fwd(u, scan_suffix, *, seq, d_att, name, rider=None):
    t = ATT_TILE
    nq = seq // t
    pairs = d_att // LANES
    grp = ATT_GROUP_FWD
    assert nq % grp == 0

    def body(q_ref, k_ref, v_ref, um_ref, o_ref, qq, k0, k1, v0, v1, f2_s, a_s, lg_s, tot_s, run_s, acc_s):
        _head_copies(seq, q_ref, ATT_SCALE, None, None, qq)
        _head_copies(seq, k_ref, 1.0, k0, k1, None)
        _head_copies(seq, v_ref, 1.0, v0, v1, None)

        def group(gi, carry):
            qb0 = gi * grp
            qrows = [pl.ds(pl.multiple_of((qb0 + g) * t, t), t) for g in range(grp)]
            qv = [qq[qrows[g], :] for g in range(grp)]

            chains = [(h, g) for g in range(grp) for h in range(2)]

            def key_rows(g, i):
                return pl.ds(pl.multiple_of(jnp.maximum(qb0 + g - i, 0) * t, t), t)

            def stage1(i, masked):
                out = []
                for g in range(grp):
                    krows = key_rows(g, i)
                    out += _pair_scores(qv[g], k0[krows, :], k1[krows, :], masked)
                return out

            def stage2(halves, a, masked):
                scan = jnp.dot(halves, um_ref[...], preferred_element_type=F32)
                logit = a + scan[:, :t]
                if masked:
                    causal = lax.broadcasted_iota(jnp.int32, (t, t), 1) < lax.broadcasted_iota(jnp.int32, (t, t), 0)
                    logit = jnp.where(causal, logit, ATT_FILL)
                return logit, scan[:, t:]

            def put(halves_a=None, logit_total=None):
                for c in range(len(chains)):
                    if halves_a is not None:
                        f2_s[c], a_s[c] = halves_a[c]
                    if logit_total is not None:
                        lg_s[c], tot_s[c] = logit_total[c]

            first = stage1(0, True)
            put(halves_a=stage1(1, False), logit_total=[stage2(f2, a, True) for f2, a in first])
            for c in range(len(chains)):
                run_s[c] = jnp.zeros((t, t), F32)
            for g in range(grp):
                acc_s[g] = jnp.zeros((t, LANES), F32)

            def wbody(st):
                i = st[0]
                held = [(f2_s[c], a_s[c]) for c in range(len(chains))]
                logits = [lg_s[c] for c in range(len(chains))]
                totals = [tot_s[c] for c in range(len(chains))]
                runs = [run_s[c] for c in range(len(chains))]
                accs = [acc_s[g] for g in range(grp)]
                for g in range(grp):
                    krows = key_rows(g, i)
                    gone = jnp.where(qb0 + g - i >= 0, 0.0, ATT_FILL)
                    weights = []
                    for c in (2 * g, 2 * g + 1):
                        run = runs[c] + gone
                        weights.append(jnp.exp(logits[c] + run).astype(BF16))
                        runs[c] = run + totals[c]
                    accs[g] = accs[g] + jnp.dot(jnp.concatenate(weights, axis=1), jnp.concatenate([v0[krows, :], v1[krows, :]], axis=0),
                                                preferred_element_type=F32)
                ahead2 = [stage2(f2, a, False) for f2, a in held]
                ahead1 = stage1(i + 2, False)
                put(halves_a=ahead1, logit_total=ahead2)
                for c in range(len(chains)):
                    run_s[c] = runs[c]
                for g in range(grp):
                    acc_s[g] = accs[g]
                more = jnp.logical_and(i + 1 <= qb0 + grp - 1, _any_alive([runs]) > 0)
                return i + 1, more.astype(jnp.int32)

            lax.while_loop(lambda st: st[1] > 0, wbody, (jnp.int32(0), jnp.int32(1)))
            for g in range(grp):
                o_ref[qrows[g], :] = acc_s[g]
            return carry

        lax.fori_loop(0, nq // grp, group, 0)

    blk = lambda base: pl.BlockSpec((seq, LANES), lambda j, base=base: (0, base + j))
    res, rode = _host_call(
        body, rider, name=name, grid=(pairs,),
        in_specs=[blk(0), blk(pairs), blk(2 * pairs), pl.BlockSpec((2 * t, 2 * t), lambda j: (0, 0))],
        out_specs=[pl.BlockSpec((seq, LANES), lambda j: (0, j))],
        out_shape=[jax.ShapeDtypeStruct((seq, d_att), F32)],
        scratch=[pltpu.VMEM((seq, LANES), BF16)] * 5 + [pltpu.VMEM((2 * grp, t, 2 * t), BF16)]
        + [pltpu.VMEM((2 * grp, t, t), F32)] * 4 + [pltpu.VMEM((grp, t, LANES), F32)],
        operands=[u, u, u, scan_suffix], semantics=("arbitrary",))
    return res[0] if rider is None else (res[0], rode)


def _columns_copy(stage_ref, du_ref, rows, col, sem):
    width = stage_ref.shape[-1]
    cols = pl.ds(pl.multiple_of(col, LANES), width)
    return pltpu.make_async_copy(stage_ref, du_ref.at[slice(None) if rows is None else rows, cols], sem)


def _attn_bwd(u, d_att_out, du, scan_suffix, scan_prefix, *, seq, d_att, name, rider=None):
    t = ATT_TILE
    nq = seq // t
    pairs = d_att // LANES
    grp = ATT_GROUP
    assert nq % grp == 0

    def body(q_ref, k_ref, v_ref, do_ref, um_ref, pm_ref, du_old, du_ref,
             qq, q0, q1, k0, k1, v0, v1, dd, do0, do1, dq_acc, dk_acc, dv_acc, g_st, b_st,
             f2_s, a_s, lg_s, tot_s, dw_s, run_s, p_s, pt_s, grun_s, dq_s, dq_o, dk_o, dv_o, out_sem):
        _head_copies(seq, q_ref, ATT_SCALE, q0, q1, qq)
        _head_copies(seq, k_ref, 1.0, k0, k1, None)
        _head_copies(seq, v_ref, 1.0, v0, v1, None)
        _head_copies(seq, do_ref, 1.0, do0, do1, dd)
        dk_acc[...] = jnp.zeros_like(dk_acc)
        dv_acc[...] = jnp.zeros_like(dv_acc)

        tn = (((0,), (0,)), ((), ()))
        nt = (((1,), (1,)), ((), ()))

        def stacked(lo_ref, hi_ref, rows):
            return jnp.concatenate([lo_ref[rows, :], hi_ref[rows, :]], axis=0)

        def group(gi, carry):
            qb0 = gi * grp
            qrows = [pl.ds(pl.multiple_of((qb0 + g) * t, t), t) for g in range(grp)]
            qv = [qq[qrows[g], :] for g in range(grp)]
            dov = [dd[qrows[g], :] for g in range(grp)]
            q_heads = [stacked(q0, q1, qrows[g]) for g in range(grp)]
            do_heads = [stacked(do0, do1, qrows[g]) for g in range(grp)]

            def key_rows(g, i):
                return pl.ds(pl.multiple_of(jnp.maximum(qb0 + g - i, 0) * t, t), t)

            chains = [(h, g) for g in range(grp) for h in range(2)]
            every = range(len(chains))

            def stage1(i, masked):
                out = []
                for g in range(grp):
                    krows = key_rows(g, i)
                    out += _pair_scores(qv[g], k0[krows, :], k1[krows, :], masked)
                return out

            def stage2(i, held, masked):
                out = []
                for g in range(grp):
                    dw2 = lax.dot_general(dov[g], stacked(v0, v1, key_rows(g, i)), nt, preferred_element_type=F32)
                    for h in range(2):
                        c = 2 * g + h
                        halves, a = held[c]
                        scan = jnp.dot(halves, um_ref[...], preferred_element_type=F32)
                        logit = a + scan[:, :t]
                        if masked:
                            causal = lax.broadcasted_iota(jnp.int32, (t, t), 1) < lax.broadcasted_iota(jnp.int32, (t, t), 0)
                            logit = jnp.where(causal, logit, ATT_FILL)
                        b_st[c, i] = jnp.exp(a)
                        out.append((logit, scan[:, t:], dw2[:, h * t:(h + 1) * t]))
                return out

            def put(held=None, ready=None):
                for c in every:
                    if held is not None:
                        f2_s[c], a_s[c] = held[c]
                    if ready is not None:
                        lg_s[c], tot_s[c], dw_s[c] = ready[c]

            put(held=stage1(1, False), ready=stage2(0, stage1(0, True), True))
            for c in every:
                run_s[c] = jnp.zeros((t, t), F32)

            def wbody(st):
                i = st[0]
                held = [(f2_s[c], a_s[c]) for c in every]
                ready = [(lg_s[c], tot_s[c], dw_s[c]) for c in every]
                runs = [run_s[c] for c in every]
                for g in range(grp):
                    gone = jnp.where(qb0 + g - i >= 0, 0.0, ATT_FILL)
                    weights = []
                    for c in (2 * g, 2 * g + 1):
                        logit, total, dw = ready[c]
                        run = runs[c] + gone
                        w = jnp.exp(logit + run)
                        g_st[c, i] = w * dw
                        weights.append(w.astype(BF16))
                        runs[c] = run + total
                    dv_acc[key_rows(g, i), :] += lax.dot_general(jnp.concatenate(weights, axis=0), do_heads[g], tn,
                                                                 preferred_element_type=F32)
                ahead2 = stage2(i + 1, held, False)
                ahead1 = stage1(i + 2, False)
                put(held=ahead1, ready=ahead2)
                for c in every:
                    run_s[c] = runs[c]
                more = jnp.logical_and(i + 1 <= qb0 + grp - 1, _any_alive([runs]) > 0)
                return i + 1, more.astype(jnp.int32)

            steps = lax.while_loop(lambda st: st[1] > 0, wbody, (jnp.int32(0), jnp.int32(1)))[0]

            def prefix(i):
                out = []
                for c in every:
                    scan = _split_scan(g_st[c, i], pm_ref)
                    out.append((scan[:, :t], scan[:, t:]))
                return out

            def back(i, masked):
                sums = [(p_s[c], pt_s[c]) for c in every]
                gruns = [grun_s[c] for c in every]
                dqs = [dq_s[g] for g in range(grp)]
                for g in range(grp):
                    krows = key_rows(g, i)
                    dzs = []
                    for c in (2 * g, 2 * g + 1):
                        gt = g_st[c, i]
                        dz = gt - b_st[c, i] * (gt + gruns[c] + sums[c][0])
                        if masked:
                            causal = lax.broadcasted_iota(jnp.int32, (t, t), 1) < lax.broadcasted_iota(jnp.int32, (t, t), 0)
                            dz = jnp.where(causal, dz, 0.0)
                        dzs.append(dz.astype(BF16))
                        gruns[c] = gruns[c] + sums[c][1]
                    dqs[g] = dqs[g] + jnp.dot(jnp.concatenate(dzs, axis=1), stacked(k0, k1, krows), preferred_element_type=F32)
                    dk_acc[krows, :] += lax.dot_general(jnp.concatenate(dzs, axis=0), q_heads[g], tn, preferred_element_type=F32)
                return gruns, dqs

            def keep(sums=None, gruns=None, dqs=None):
                for c in every:
                    if sums is not None:
                        p_s[c], pt_s[c] = sums[c]
                    if gruns is not None:
                        grun_s[c] = gruns[c]
                if dqs is not None:
                    for g in range(grp):
                        dq_s[g] = dqs[g]

            keep(sums=prefix(steps - 1), gruns=[jnp.zeros((t, t), F32)] * len(chains), dqs=[jnp.zeros((t, LANES), F32)] * grp)

            def bbody(j, carry2):
                i = steps - 1 - j
                gruns, dqs = back(i, False)
                keep(sums=prefix(i - 1), gruns=gruns, dqs=dqs)
                return carry2

            lax.fori_loop(0, steps - 1, bbody, 0)
            _, dqs = back(0, True)
            for g in range(grp):
                dq_acc[qrows[g], :] = dqs[g]
            return carry

        lax.fori_loop(0, nq // grp, group, 0)

        pair = pl.program_id(0)
        stages = (dq_o, dk_o, dv_o)

        def copies(j):
            return [_columns_copy(stage, du_ref, None, k * d_att + j * LANES, out_sem.at[k]) for k, stage in enumerate(stages)]

        @pl.when(pair > 0)
        def _():
            for cp in copies(pair - 1):
                cp.wait()

        chunk = min(256, seq)

        def emit(r, carry):
            rows = pl.ds(pl.multiple_of(r * chunk, chunk), chunk)
            dq_o[rows, :] = (dq_acc[rows, :] * ATT_SCALE).astype(BF16)
            dk_o[rows, :] = dk_acc[rows, :].astype(BF16)
            dv_o[rows, :] = dv_acc[rows, :].astype(BF16)
            return carry

        lax.fori_loop(0, seq // chunk, emit, 0)
        for cp in copies(pair):
            cp.start()

        @pl.when(pair == pairs - 1)
        def _():
            for cp in copies(pair):
                cp.wait()

    blk = lambda base: pl.BlockSpec((seq, LANES), lambda j, base=base: (0, base + j), pipeline_mode=pl.Buffered(1))
    mat = pl.BlockSpec((2 * t, 2 * t), lambda j: (0, 0))
    hbm = pl.BlockSpec(memory_space=pl.ANY)
    res, rode = _host_call(
        body, rider, name=name, grid=(pairs,),
        in_specs=[blk(0), blk(pairs), blk(2 * pairs), blk(0), mat, mat, hbm],
        out_specs=[hbm], out_shape=[jax.ShapeDtypeStruct(du.shape, du.dtype)], host_aliases={6: 0},
        scratch=[pltpu.VMEM((seq, LANES), BF16)] * 10 + [pltpu.VMEM((seq, LANES), F32)] * 3
        + [pltpu.VMEM((2 * grp, nq + 1, t, t), F32)] * 2 + [pltpu.VMEM((2 * grp, t, 2 * t), BF16)]
        + [pltpu.VMEM((2 * grp, t, t), F32)] * 8 + [pltpu.VMEM((grp, t, LANES), F32)]
        + [pltpu.VMEM((seq, LANES), BF16)] * 3 + [pltpu.SemaphoreType.DMA((3,))],
        operands=[u, u, u, d_att_out, scan_suffix, scan_prefix, du], semantics=("arbitrary",))
    return res[0] if rider is None else (res[0], rode)


CONV_ROWS = 256


def _shifted(window, residue, rows):
    total = rows + CONV_PAD
    return window if residue == 0 else pltpu.roll(window, total - residue, 0)


def _glu_to_pad(seq, a_ref, b_ref, pad_ref):
    chunk = min(CONV_ROWS, seq)
    pad_ref[pl.ds(0, CONV_PAD), :] = jnp.zeros((CONV_PAD, LANES), F32)

    def step(r, carry):
        rows = pl.ds(pl.multiple_of(r * chunk, chunk), chunk)
        pad_ref[pl.ds(pl.multiple_of(r * chunk + CONV_PAD, SUBLANES), chunk), :] = a_ref[rows, :] * _sigmoid(b_ref[rows, :])
        return carry

    lax.fori_loop(0, seq // chunk, step, 0)


def _conv_fwd(u, conv_w, conv_b, *, layer, seq, d_conv, col_a, col_b, name, rider=None):
    blocks = d_conv // LANES
    rows_t = min(CONV_ROWS, seq)
    shift0 = CONV_PAD - (CONV_WIDTH - 1)

    def body(a_ref, b_ref, w_ref, bias_ref, o_ref, pad_ref):
        _glu_to_pad(seq, a_ref, b_ref, pad_ref)

        def step(r, carry):
            base = pl.multiple_of(r * rows_t, rows_t)
            window = pad_ref[pl.ds(base, rows_t + CONV_PAD), :]
            acc = jnp.zeros((rows_t, LANES), F32) + bias_ref[...]
            for residue in range(SUBLANES):
                moved = _shifted(window, residue, rows_t)
                for tap in range(CONV_WIDTH):
                    if (shift0 + tap) % SUBLANES == residue:
                        lo = (shift0 + tap) - residue
                        acc = acc + w_ref[tap:tap + 1, :] * moved[lo:lo + rows_t, :]
            o_ref[pl.ds(base, rows_t), :] = acc
            return carry

        lax.fori_loop(0, seq // rows_t, step, 0)

    res, rode = _host_call(
        body, rider, name=name, grid=(blocks,),
        in_specs=[pl.BlockSpec((seq, LANES), lambda j: (0, col_a + j)), pl.BlockSpec((seq, LANES), lambda j: (0, col_b + j)),
                  pl.BlockSpec((None, CONV_PAD, LANES), lambda j: (layer, 0, j)),
                  pl.BlockSpec((None, 1, LANES), lambda j: (layer, 0, j))],
        out_specs=[pl.BlockSpec((seq, LANES), lambda j: (0, j))],
        out_shape=[jax.ShapeDtypeStruct((seq, d_conv), F32)],
        scratch=[pltpu.VMEM((seq + CONV_PAD, LANES), F32)],
        operands=[u, u, conv_w, conv_b], semantics=("arbitrary",))
    return res[0] if rider is None else (res[0], rode)


def _conv_bwd(u, dc1, du, conv_w, *, layer, seq, d_conv, col_a, col_b, name):
    blocks = d_conv // LANES
    rows_t = min(CONV_ROWS, seq)
    shift0 = CONV_PAD - (CONV_WIDTH - 1)

    def body(a_ref, b_ref, d_ref, w_ref, du_old, du_ref, dw_ref, pad_ref, dpad_ref, dw_acc, da_ref, db_ref, out_sem):
        block = pl.program_id(0)

        def copies(j):
            return [_columns_copy(stage, du_ref, None, (base + j) * LANES, out_sem.at[k])
                    for k, (stage, base) in enumerate(((da_ref, col_a), (db_ref, col_b)))]

        _glu_to_pad(seq, a_ref, b_ref, pad_ref)
        dpad_ref[pl.ds(seq, CONV_PAD), :] = jnp.zeros((CONV_PAD, LANES), F32)

        def fill(r, carry):
            rows = pl.ds(pl.multiple_of(r * rows_t, rows_t), rows_t)
            dpad_ref[rows, :] = d_ref[rows, :]
            return carry

        lax.fori_loop(0, seq // rows_t, fill, 0)
        dw_acc[...] = jnp.zeros_like(dw_acc)

        @pl.when(block > 0)
        def _():
            for cp in copies(block - 1):
                cp.wait()

        def step(r, carry):
            base = pl.multiple_of(r * rows_t, rows_t)
            rows = pl.ds(base, rows_t)
            window = dpad_ref[pl.ds(base, rows_t + CONV_PAD), :]
            acc = jnp.zeros((rows_t, LANES), F32)
            for residue in range(SUBLANES):
                moved = _shifted(window, residue, rows_t)
                for tap in range(CONV_WIDTH):
                    off = CONV_WIDTH - 1 - tap
                    if off % SUBLANES == residue:
                        lo = off - residue
                        acc = acc + w_ref[tap:tap + 1, :] * moved[lo:lo + rows_t, :]
            sig = _sigmoid(b_ref[rows, :])
            a = a_ref[rows, :]
            da_ref[rows, :] = (acc * sig).astype(BF16)
            db_ref[rows, :] = (acc * a * sig * (1.0 - sig)).astype(BF16)
            dcur = d_ref[rows, :]
            cwin = pad_ref[pl.ds(base, rows_t + CONV_PAD), :]
            for residue in range(SUBLANES):
                moved = _shifted(cwin, residue, rows_t)
                for tap in range(CONV_WIDTH):
                    if (shift0 + tap) % SUBLANES == residue:
                        lo = (shift0 + tap) - residue
                        prod = dcur * moved[lo:lo + rows_t, :]
                        dw_acc[tap] += jnp.sum(prod.reshape(rows_t // SUBLANES, SUBLANES, LANES), axis=0)
            return carry

        lax.fori_loop(0, seq // rows_t, step, 0)
        dw_ref[...] = jnp.sum(dw_acc[...], axis=1)
        for cp in copies(block):
            cp.start()

        @pl.when(block == blocks - 1)
        def _():
            for cp in copies(block):
                cp.wait()

    col = lambda base: pl.BlockSpec((seq, LANES), lambda j, base=base: (0, base + j))
    own = pl.BlockSpec((seq, LANES), lambda j: (0, j))
    hbm = pl.BlockSpec(memory_space=pl.ANY)
    return pl.pallas_call(
        body, name=name, grid=(blocks,),
        in_specs=[col(col_a), col(col_b), own, pl.BlockSpec((None, CONV_PAD, LANES), lambda j: (layer, 0, j)), hbm],
        out_specs=[hbm, pl.BlockSpec((CONV_PAD, LANES), lambda j: (0, j))],
        out_shape=[jax.ShapeDtypeStruct(du.shape, du.dtype), jax.ShapeDtypeStruct((CONV_PAD, d_conv), F32)],
        scratch_shapes=[pltpu.VMEM((seq + CONV_PAD, LANES), F32), pltpu.VMEM((seq + CONV_PAD, LANES), F32),
                        pltpu.VMEM((CONV_PAD, SUBLANES, LANES), F32), pltpu.VMEM((seq, LANES), BF16), pltpu.VMEM((seq, LANES), BF16),
                        pltpu.SemaphoreType.DMA((2,))],
        input_output_aliases={4: 0}, compiler_params=_cparams(("arbitrary",)),
    )(u, u, dc1, conv_w, du)


MIX_ROWS = 256


def _layer_norm_stats(val):
    mu = jnp.mean(val, axis=-1, keepdims=True)
    cen = val - mu
    var = jnp.mean(cen * cen, axis=-1, keepdims=True)
    rstd = lax.rsqrt(var + LN_EPS)
    return cen * rstd, rstd


def _layer_norm_bwd(dy, xhat, rstd, gain):
    dxhat = dy * gain
    m1 = jnp.mean(dxhat, axis=-1, keepdims=True)
    m2 = jnp.mean(dxhat * xhat, axis=-1, keepdims=True)
    dx = rstd * (dxhat - m1 - xhat * m2)
    return dx, jnp.sum(dy * xhat, axis=0, keepdims=True), jnp.sum(dy, axis=0, keepdims=True)


def _mix_forward(zatt, att, c1, zconv, gatt, gconv, x, w_att, w_conv, w_out, cln_g, cln_b, b_proj):
    s_zatt = _sigmoid(zatt)
    a_in = att * (zatt * s_zatt)
    chat, c_rstd = _layer_norm_stats(c1)
    c2 = chat * cln_g + cln_b
    s_c2 = _sigmoid(c2)
    c3 = c2 * s_c2
    s_zconv = _sigmoid(zconv)
    c_in = c3 * (zconv * s_zconv)
    a_in_b, c_in_b = a_in.astype(BF16), c_in.astype(BF16)
    ab = jnp.dot(a_in_b, w_att, preferred_element_type=F32)
    cb = jnp.dot(c_in_b, w_conv, preferred_element_type=F32) + b_proj
    s_gatt, s_gconv = _sigmoid(gatt), _sigmoid(gconv)
    merged_b = (s_gatt * ab + s_gconv * cb).astype(BF16)
    y = jnp.dot(merged_b, w_out, preferred_element_type=F32)
    h = DEEPNORM_ALPHA * x + y
    return dict(s_zatt=s_zatt, a_in_b=a_in_b, chat=chat, c_rstd=c_rstd, c2=c2, s_c2=s_c2, c3=c3, s_zconv=s_zconv,
                c_in_b=c_in_b, ab=ab, cb=cb, s_gatt=s_gatt, s_gconv=s_gconv, merged_b=merged_b, h=h)


def _u_blocks(rows_t, width, half):
    return [pl.BlockSpec((rows_t, half), lambda i, c=c: (i, c)) for c in (3, 6, 7, 8, 9, 10)]


def _of_layer(arr, layer):
    return pl.BlockSpec((None,) + arr.shape[1:], lambda i: (layer, 0, 0))


def _mix_fwd(u, att, c1, x, w_att, w_conv, w_out, cln_g, cln_b, b_proj, ln_g, ln_b, *, layer, seq, d_model, name):
    half = d_model // 2
    rows_t = min(MIX_ROWS, seq)

    def body(zatt_ref, zconv_ref, ga0, ga1, gc0, gc1, att_ref, c1_ref, x_ref, wa_ref, wc_ref, wo_ref,
             cg_ref, cb_ref, bp_ref, lg_ref, lb_ref, o_ref, ob_ref):
        gatt = jnp.concatenate([ga0[...], ga1[...]], axis=1)
        gconv = jnp.concatenate([gc0[...], gc1[...]], axis=1)
        mid = _mix_forward(zatt_ref[...], att_ref[...], c1_ref[...], zconv_ref[...], gatt, gconv, x_ref[...],
                           wa_ref[...], wc_ref[...], wo_ref[...], cg_ref[...], cb_ref[...], bp_ref[...])
        xhat, _ = _layer_norm_stats(mid["h"])
        out = xhat * lg_ref[...] + lb_ref[...]
        o_ref[...] = out
        ob_ref[...] = out.astype(BF16)

    row = lambda width: pl.BlockSpec((rows_t, width), lambda i: (i, 0))
    full = lambda arr: _of_layer(arr, layer)
    out = pl.BlockSpec((rows_t, d_model), lambda i: (i, 0))
    return pl.pallas_call(
        body, name=name, grid=(seq // rows_t,),
        in_specs=_u_blocks(rows_t, d_model, half) + [row(half), row(half), row(d_model), full(w_att), full(w_conv), full(w_out),
                                                     full(cln_g), full(cln_b), full(b_proj), full(ln_g), full(ln_b)],
        out_specs=[out, out],
        out_shape=[jax.ShapeDtypeStruct((seq, d_model), F32), jax.ShapeDtypeStruct((seq, d_model), BF16)],
        compiler_params=_cparams(("parallel",)),
    )(u, u, u, u, u, u, att, c1, x, w_att, w_conv, w_out, cln_g, cln_b, b_proj, ln_g, ln_b)


def _mix_bwd(u, att, c1, x, dxn, w_att, w_conv, w_out, cln_g, cln_b, b_proj, ln_g, *, layer, seq, d_model, name):
    half = d_model // 2
    rows_t = min(MIX_ROWS, seq)
    steps = seq // rows_t
    nt = ((1,), (1,))
    tn = ((0,), (0,))

    def body(zatt_ref, zconv_ref, ga0, ga1, gc0, gc1, att_ref, c1_ref, x_ref, dxn_ref, wa_ref, wc_ref, wo_ref,
             cg_ref, cb_ref, bp_ref, lg_ref,
             du_ref, datt_ref, dc1_ref, dxres_ref, dwa_ref, dwc_ref, dwo_ref,
             dcg_ref, dcb_ref, dcbias_ref, dbp_ref, dlg_ref, dlb_ref, zatt_stage, tail_stage, out_sem):
        sums = (dwa_ref, dwc_ref, dwo_ref, dcg_ref, dcb_ref, dcbias_ref, dbp_ref, dlg_ref, dlb_ref)
        tile = pl.program_id(0)
        slot = tile % 2
        dzatt_ref, tail_ref = zatt_stage.at[slot], tail_stage.at[slot]
        dzconv_ref, dgatt_ref, dgconv_ref = tail_ref.at[:, :half], tail_ref.at[:, half:3 * half], tail_ref.at[:, 3 * half:]

        def copies(i):
            rows = pl.ds(pl.multiple_of(i * rows_t, rows_t), rows_t)
            return [_columns_copy(zatt_stage.at[i % 2], du_ref, rows, 3 * half, out_sem.at[i % 2, 0]),
                    _columns_copy(tail_stage.at[i % 2], du_ref, rows, 6 * half, out_sem.at[i % 2, 1])]

        @pl.when(tile == 0)
        def _():
            for ref in sums:
                ref[...] = jnp.zeros_like(ref)

        zatt, zconv, att = zatt_ref[...], zconv_ref[...], att_ref[...]
        gatt = jnp.concatenate([ga0[...], ga1[...]], axis=1)
        gconv = jnp.concatenate([gc0[...], gc1[...]], axis=1)
        wa, wc, wo = wa_ref[...], wc_ref[...], wo_ref[...]
        mid = _mix_forward(zatt, att, c1_ref[...], zconv, gatt, gconv, x_ref[...], wa, wc, wo,
                           cg_ref[...], cb_ref[...], bp_ref[...])
        xhat, rstd = _layer_norm_stats(mid["h"])
        dh, dlg, dlb = _layer_norm_bwd(dxn_ref[...], xhat, rstd, lg_ref[...])
        dlg_ref[...] += dlg
        dlb_ref[...] += dlb
        dxres_ref[...] = DEEPNORM_ALPHA * dh
        dy = dh.astype(BF16)
        dwo_ref[...] += lax.dot_general(mid["merged_b"], dy, (tn, ((), ())), preferred_element_type=F32)
        dmerged = lax.dot_general(dy, wo, (nt, ((), ())), preferred_element_type=F32)
        s_ga, s_gc, ab, cb = mid["s_gatt"], mid["s_gconv"], mid["ab"], mid["cb"]
        dgatt_ref[...] = (dmerged * ab * s_ga * (1.0 - s_ga)).astype(BF16)
        dgconv_ref[...] = (dmerged * cb * s_gc * (1.0 - s_gc)).astype(BF16)
        dab = dmerged * s_ga
        dcb = dmerged * s_gc
        dbp_ref[...] += jnp.sum(dcb, axis=0, keepdims=True)
        dab_b, dcb_b = dab.astype(BF16), dcb.astype(BF16)
        dwa_ref[...] += lax.dot_general(mid["a_in_b"], dab_b, (tn, ((), ())), preferred_element_type=F32)
        da_in = lax.dot_general(dab_b, wa, (nt, ((), ())), preferred_element_type=F32)
        s_za = mid["s_zatt"]
        datt_ref[...] = da_in * (zatt * s_za)
        dzatt_ref[...] = (da_in * att * (s_za * (1.0 + zatt * (1.0 - s_za)))).astype(BF16)
        dwc_ref[...] += lax.dot_general(mid["c_in_b"], dcb_b, (tn, ((), ())), preferred_element_type=F32)
        dc_in = lax.dot_general(dcb_b, wc, (nt, ((), ())), preferred_element_type=F32)
        s_zc, c2, s_c2 = mid["s_zconv"], mid["c2"], mid["s_c2"]
        dzconv_ref[...] = (dc_in * mid["c3"] * (s_zc * (1.0 + zconv * (1.0 - s_zc)))).astype(BF16)
        dc3 = dc_in * (zconv * s_zc)
        dc2 = dc3 * (s_c2 * (1.0 + c2 * (1.0 - s_c2)))
        dc1, dcg, dcbeta = _layer_norm_bwd(dc2, mid["chat"], mid["c_rstd"], cg_ref[...])
        dcg_ref[...] += dcg
        dcb_ref[...] += dcbeta
        dcbias_ref[...] += jnp.sum(dc1, axis=0, keepdims=True)
        dc1_ref[...] = dc1
        for cp in copies(tile):
            cp.start()

        @pl.when(tile > 0)
        def _():
            for cp in copies(tile - 1):
                cp.wait()

        @pl.when(tile == steps - 1)
        def _():
            for cp in copies(tile):
                cp.wait()

    row = lambda width: pl.BlockSpec((rows_t, width), lambda i: (i, 0))
    full = lambda arr: _of_layer(arr, layer)
    whole = lambda r, c: pl.BlockSpec((r, c), lambda i: (0, 0))
    sds = jax.ShapeDtypeStruct
    out_specs = [pl.BlockSpec(memory_space=pl.ANY), row(half), row(half), row(d_model),
                 whole(half, d_model), whole(half, d_model), whole(d_model, d_model),
                 whole(1, half), whole(1, half), whole(1, half), whole(1, d_model), whole(1, d_model), whole(1, d_model)]
    out_shape = [sds((seq, u.shape[1]), BF16), sds((seq, half), F32), sds((seq, half), F32), sds((seq, d_model), F32),
                 sds((half, d_model), F32), sds((half, d_model), F32), sds((d_model, d_model), F32),
                 sds((1, half), F32), sds((1, half), F32), sds((1, half), F32),
                 sds((1, d_model), F32), sds((1, d_model), F32), sds((1, d_model), F32)]
    return pl.pallas_call(
        body, name=name, grid=(steps,),
        in_specs=_u_blocks(rows_t, d_model, half) + [row(half), row(half), row(d_model), row(d_model), full(w_att), full(w_conv),
                                                     full(w_out), full(cln_g), full(cln_b), full(b_proj), full(ln_g)],
        out_specs=out_specs, out_shape=out_shape,
        scratch_shapes=[pltpu.VMEM((2, rows_t, half), BF16), pltpu.VMEM((2, rows_t, 5 * half), BF16), pltpu.SemaphoreType.DMA((2, 2))],
        compiler_params=_cparams(("arbitrary",)),
    )(u, u, u, u, u, u, att, c1, x, dxn, w_att, w_conv, w_out, cln_g, cln_b, b_proj, ln_g)


def _loss_head(y, target, *, seq, d_model, name):
    rows_t = min(512, seq)

    def body(y_ref, t_ref, dy_ref, loss_ref):
        @pl.when(pl.program_id(0) == 0)
        def _():
            loss_ref[...] = jnp.zeros_like(loss_ref)

        err = y_ref[...] - t_ref[...]
        dy_ref[...] = err * (1.0 / d_model)
        per_token = jnp.sum(err * err, axis=-1, keepdims=True) * (1.0 / d_model)
        loss_ref[...] += 0.5 * jnp.sum(per_token, axis=0, keepdims=True)

    row = pl.BlockSpec((rows_t, d_model), lambda i: (i, 0))
    return pl.pallas_call(
        body, name=name, grid=(seq // rows_t,), in_specs=[row, row],
        out_specs=[row, pl.BlockSpec((1, 1), lambda i: (0, 0))],
        out_shape=[jax.ShapeDtypeStruct((seq, d_model), F32), jax.ShapeDtypeStruct((1, 1), F32)],
        compiler_params=_cparams(("arbitrary",)),
    )(y, target)


def _adamw(w, g, m, v, *, name, echo=False, span=None, into=None, rider=None):
    rows, cols = w.shape
    first, count = span or (0, rows)
    rows_t = count
    for cand in (512, 256, 128, 64, 32, 16, 8):
        if count % cand == 0 and first % cand == 0 and cand * cols * 4 <= 2 * 1024 * 1024:
            rows_t = cand
            break
    n_out = 4 if echo else 3

    def body(w_ref, g_ref, m_ref, v_ref, *rest):
        outs = rest[-n_out:]
        d_ref, nm_ref, nv_ref = outs[-3:]
        grad = g_ref[...]
        if echo:
            outs[0][...] = grad
        new_m = ADAM_B1 * m_ref[...] + (1.0 - ADAM_B1) * grad
        new_v = ADAM_B2 * v_ref[...] + (1.0 - ADAM_B2) * (grad * grad)
        m_hat = new_m / (1.0 - ADAM_B1 ** ADAM_STEP)
        v_hat = new_v / (1.0 - ADAM_B2 ** ADAM_STEP)
        d_ref[...] = -ADAM_LR * (m_hat / (jnp.sqrt(v_hat) + ADAM_EPS) + ADAM_WD * w_ref[...])
        nm_ref[...] = new_m
        nv_ref[...] = new_v

    blk = pl.BlockSpec((rows_t, cols), lambda i: (first // rows_t + i, 0))
    out = jax.ShapeDtypeStruct((rows, cols), F32)
    earlier = list(into or [])
    res, rode = _host_call(
        body, rider, name=name, grid=(count // rows_t,), in_specs=[blk] * 4 + [pl.BlockSpec(memory_space=pl.ANY)] * len(earlier),
        out_specs=[blk] * n_out, out_shape=[out] * n_out, scratch=[], operands=[w, g, m, v] + earlier,
        semantics=("arbitrary",), host_aliases={4 + k: k for k in range(len(earlier))})
    return res if rider is None else (res, rode)


MATRICES = ("w_in", "w_att_proj", "w_conv_proj", "w_out")
VECTORS = ("b_in", "conv_b", "conv_ln_g", "conv_ln_b", "b_conv_proj", "ln_g", "ln_b")
MATRIX_CHIP_AXIS = (1, 1, 1, 0)
MATRIX_CORE_AXIS = (0, 0, 0, 1)


def _chip_sums(layer, depth, partials, got, blocks, place):
    return [_chip_sum_layer(partials[t], got[t], blocks[t], layer, depth, MATRIX_CHIP_AXIS[t], MATRIX_CORE_AXIS[t], place,
                            name="chip_sum_" + MATRICES[t]) for t in range(len(MATRICES))]


def _train_pass(x, target, mats, taps, b_in, conv_b, cln_g, cln_b, b_proj, ln_g, ln_b, place):
    seq, d_model = x.shape
    half = d_model // 2
    depth = b_in.shape[0]
    scan_suffix, scan_prefix = _scan_matrices()
    cols = half // LANES
    dims = dict(seq=seq, d_model=d_model)
    conv_dims = dict(seq=seq, d_conv=half, col_a=4 * cols, col_b=5 * cols)
    axes3 = [axis + 1 for axis in MATRIX_CHIP_AXIS]
    n_mat = len(mats)

    def first_layers(relay):
        return _gather_rider([mats[0], taps], [axes3[0], 2], [(0, 1), (0, depth)], relay)

    mats = list(mats)
    mats[0], taps = _lone_call(_chain(first_layers(False), first_layers(True)), name="gather_first")

    xs, xbs, us, atts, c1s = [x], [x.astype(BF16)], [], [], []
    for l in range(depth):
        u = _matmul(xbs[l], mats[0], layer=l, mode="nn", bias=b_in[l].reshape(1, -1), name="in_proj", tm=256, tn=b_in.shape[1],
                    tk=d_model)
        nxt = [(l + 1, 1)] + [(l + 1, 1) if l else (0, 2)] * (n_mat - 1)
        if l + 1 < depth:
            att, mats = _attn_fwd(u, scan_suffix, seq=seq, d_att=half, name="attn_fwd", rider=_gather_rider(mats, axes3, nxt, False))
            c1, mats = _conv_fwd(u, taps, conv_b, layer=l, name="conv_fwd", rider=_gather_rider(mats, axes3, nxt, True), **conv_dims)
        else:
            att = _attn_fwd(u, scan_suffix, seq=seq, d_att=half, name="attn_fwd_last")
            c1 = _conv_fwd(u, taps, conv_b, layer=l, name="conv_fwd_last", **conv_dims)
        xn, xnb = _mix_fwd(u, att, c1, xs[l], mats[1], mats[2], mats[3], cln_g, cln_b, b_proj, ln_g, ln_b, layer=l, name="mix_fwd", **dims)
        us.append(u)
        atts.append(att)
        c1s.append(c1)
        xs.append(xn)
        xbs.append(xnb)
    w_in, w_att, w_conv, w_out = mats

    dx, loss = _loss_head(xs[depth], target, name="loss_head", **dims)
    grads = [None] * depth
    blocks = [None] * n_mat
    waiting = None
    for l in reversed(range(depth)):
        u = us[l]
        (du, datt, dc1, dxres, dwa, dwc, dwo, dcg, dcb, dcbias, dbp, dlg, dlb) = _mix_bwd(
            u, atts[l], c1s[l], xs[l], dx, w_att, w_conv, w_out, cln_g, cln_b, b_proj, ln_g, layer=l, name="mix_bwd", **dims)
        if waiting is None:
            du = _attn_bwd(u, datt, du, scan_suffix, scan_prefix, seq=seq, d_att=half, name="attn_bwd_first")
        else:
            du, got = _attn_bwd(u, datt, du, scan_suffix, scan_prefix, seq=seq, d_att=half, name="attn_bwd",
                                rider=_exchange_rider(waiting[1], MATRIX_CHIP_AXIS))
            blocks = _chip_sums(waiting[0], depth, waiting[1], got, blocks, place)
        du, dconvw = _conv_bwd(u, dc1, du, taps, layer=l, name="conv_bwd", **conv_dims)
        dwin, dbin = _matmul(xbs[l], du, mode="tn", colsum=True, name="in_proj_dw", tm=1024, tn=512, tk=seq)
        parts = [dwin, dwa, dwc, dwo]
        swap = _swap_rider(parts, MATRIX_CORE_AXIS)
        if l == 0:
            swap = _both(swap, _join_rider(blocks, [axis + 1 for axis in MATRIX_CORE_AXIS], 1, depth - 1))
        dx, rode = _matmul(du, w_in, layer=l, mode="nt", add=dxres, name="in_proj_dx", tm=512, tn=1024, tk=du.shape[1], rider=swap)
        got, blocks = rode[:n_mat], (rode[n_mat:] if l == 0 else blocks)
        waiting = (l, [_pair_sum_layer(parts[t], got[t], MATRIX_CORE_AXIS[t], place, name="pair_sum_" + MATRICES[t])
                       for t in range(n_mat)])
        grads[l] = dict(b_in=dbin[0], conv_w=dconvw[:CONV_WIDTH], conv_b=dcbias[0], conv_ln_g=dcg[0], conv_ln_b=dcb[0],
                        b_conv_proj=dbp[0], ln_g=dlg[0], ln_b=dlb[0])
    packed = jnp.stack([jnp.concatenate([grads[l][n] for n in VECTORS] + [grads[l]["conv_w"].reshape(-1)]) for l in range(depth)])
    return loss, dx, packed, waiting[1], blocks


MESH = pl.DeviceIdType.MESH


def _position():
    x, y, c = lax.axis_index("x"), lax.axis_index("y"), lax.axis_index("c")
    return x, y, c, [(1 - x, y), (x, 1 - y), (1 - x, 1 - y)]


def _cut(ref, axis, start, size):
    idx = [slice(None)] * len(ref.shape)
    idx[axis] = pl.ds(start, size)
    return ref.at[tuple(idx)]


def _remote(src, dst, send_sem, recv_sem, device):
    return pltpu.make_async_remote_copy(src_ref=src, dst_ref=dst, send_sem=send_sem, recv_sem=recv_sem,
                                        device_id=device, device_id_type=MESH)


def _gather_rider(wholes, chip_axes, spans, relay):
    n = len(wholes)

    def region(dst, t, chip, half):
        first, count = spans[t]
        _, rows, cols = wholes[t].shape
        if chip_axes[t] == 2:
            size, part = cols // N_CHIPS, rows // 2
            ref = _cut(_cut(dst[t], 2, pl.multiple_of(chip * size, size), size), 1, pl.multiple_of(half * part, part), part)
        else:
            size = rows // N_CHIPS
            part = size // 2
            ref = _cut(dst[t], 1, pl.multiple_of(chip * size + half * part, part), part)
        return _cut(ref, 0, first, count)

    def copies(dst, sems, receiving):
        x, y, c, chips = _position()
        send_sem, recv_sem = sems
        out = []
        for t in range(n):
            for j, chip in enumerate(chips):
                theirs = 2 * chip[0] + chip[1]
                if relay:
                    ref = region(dst, t, theirs, 1 - c if receiving else c)
                    peer = (x, y, 1 - c)
                else:
                    ref = region(dst, t, theirs if receiving else 2 * x + y, c)
                    peer = (*chip, c)
                out.append(_remote(ref, ref, send_sem.at[t, j], recv_sem.at[t, j], peer))
        return out

    def start(ins, outs, sems):
        for cp in copies(outs, sems, False):
            cp.start()

    def finish(ins, outs, sems):
        for cp in copies(outs, sems, True):
            cp.wait_recv()
        for cp in copies(outs, sems, False):
            cp.wait_send()

    return _Rider(operands=list(wholes), out_shape=[jax.ShapeDtypeStruct(w.shape, w.dtype) for w in wholes],
                  aliases={t: t for t in range(n)}, scratch=[pltpu.SemaphoreType.DMA((n, N_CHIPS - 1))] * 2,
                  start=start, finish=finish)


def _chain(first, second):
    cut = len(first.scratch)

    def start(ins, outs, sems):
        first.start(ins, outs, sems[:cut])
        first.finish(ins, outs, sems[:cut])
        second.start(ins, outs, sems[cut:])

    def finish(ins, outs, sems):
        second.finish(ins, outs, sems[cut:])

    return first._replace(scratch=list(first.scratch) + list(second.scratch), start=start, finish=finish)


def _swap_rider(parts, core_axes):
    n = len(parts)
    halves = []
    for arr, axis in zip(parts, core_axes):
        shape = list(arr.shape)
        shape[axis] //= 2
        halves.append(jax.ShapeDtypeStruct(tuple(shape), arr.dtype))

    def copies(ins, outs, sems):
        x, y, c, _ = _position()
        out = []
        for t in range(n):
            size = halves[t].shape[core_axes[t]]
            piece = _cut(ins[t], core_axes[t], pl.multiple_of((1 - c) * size, size), size)
            out.append(_remote(piece, outs[t], sems[0].at[t], sems[1].at[t], (x, y, 1 - c)))
        return out

    def start(ins, outs, sems):
        for cp in copies(ins, outs, sems):
            cp.start()

    def finish(ins, outs, sems):
        for cp in copies(ins, outs, sems):
            cp.wait()

    return _Rider(operands=list(parts), out_shape=halves, aliases={}, scratch=[pltpu.SemaphoreType.DMA((n,))] * 2,
                  start=start, finish=finish)


def _exchange_rider(partials, chip_axes):
    n = len(partials)
    quarters = []
    for arr, axis in zip(partials, chip_axes):
        shape = list(arr.shape)
        shape[axis] //= N_CHIPS
        quarters.append(jax.ShapeDtypeStruct((N_CHIPS - 1, *shape), arr.dtype))

    def copies(ins, outs, sems):
        x, y, c, chips = _position()
        out = []
        for t in range(n):
            size = quarters[t].shape[1 + chip_axes[t]]
            for j, chip in enumerate(chips):
                piece = _cut(ins[t], chip_axes[t], pl.multiple_of((2 * chip[0] + chip[1]) * size, size), size)
                out.append(_remote(piece, outs[t].at[j], sems[0].at[t, j], sems[1].at[t, j], (*chip, c)))
        return out

    def start(ins, outs, sems):
        for cp in copies(ins, outs, sems):
            cp.start()

    def finish(ins, outs, sems):
        for cp in copies(ins, outs, sems):
            cp.wait()

    return _Rider(operands=list(partials), out_shape=quarters, aliases={}, scratch=[pltpu.SemaphoreType.DMA((n, N_CHIPS - 1))] * 2,
                  start=start, finish=finish)


def _place_block(shard, chip_axis, place, dtype, *, name):
    depth, rows, cols = shard.shape
    rows_t = _row_tile(rows, cols * 4 * 4, 16 * 1024 * 1024)
    steps = rows // rows_t
    shape = list(shard.shape)
    shape[chip_axis] *= N_CHIPS
    if chip_axis == 1:
        out_spec = pl.BlockSpec((None, rows_t, cols), lambda l, i, p: (l, p[1] * steps + i, 0))
    else:
        out_spec = pl.BlockSpec((None, rows_t, cols), lambda l, i, p: (l, i, p[1]))

    def body(place_ref, src_ref, out_ref):
        out_ref[...] = src_ref[...].astype(dtype)

    return pl.pallas_call(
        body, name=name, out_shape=jax.ShapeDtypeStruct(tuple(shape), dtype),
        grid_spec=pltpu.PrefetchScalarGridSpec(num_scalar_prefetch=1, grid=(depth, steps),
                                               in_specs=[pl.BlockSpec((None, rows_t, cols), lambda l, i, p: (l, i, 0))],
                                               out_specs=out_spec),
        compiler_params=_cparams(("arbitrary", "arbitrary")),
    )(place, shard)


def _join_rider(blocks, core_axes, first, count):
    n = len(blocks)

    def copies(outs, sems, receiving):
        x, y, c, _ = _position()
        out = []
        for t in range(n):
            size = blocks[t].shape[core_axes[t]] // 2
            half = 1 - c if receiving else c
            ref = _cut(_cut(outs[t], core_axes[t], pl.multiple_of(half * size, size), size), 0, first, count)
            out.append(_remote(ref, ref, sems[0].at[t], sems[1].at[t], (x, y, 1 - c)))
        return out

    def start(ins, outs, sems):
        for cp in copies(outs, sems, False):
            cp.start()

    def finish(ins, outs, sems):
        for cp in copies(outs, sems, True):
            cp.wait_recv()
        for cp in copies(outs, sems, False):
            cp.wait_send()

    return _Rider(operands=list(blocks), out_shape=[jax.ShapeDtypeStruct(b.shape, b.dtype) for b in blocks],
                  aliases={t: t for t in range(n)}, scratch=[pltpu.SemaphoreType.DMA((n,))] * 2, start=start, finish=finish)


def _small_rider(vec):
    n_dev = 2 * N_CHIPS

    def copies(ins, outs, sems, receiving):
        x, y, c, _ = _position()
        flip = lambda v, bit: 1 - v if bit else v
        out = []
        for mask in range(1, n_dev):
            peer = (flip(x, mask & 4), flip(y, mask & 2), flip(c, mask & 1))
            row = 4 * peer[0] + 2 * peer[1] + peer[2] if receiving else 4 * x + 2 * y + c
            out.append(_remote(ins[0], outs[0].at[row], sems[0].at[mask - 1], sems[1].at[mask - 1], peer))
        return out

    def own(ins, outs, sems):
        x, y, c, _ = _position()
        return pltpu.make_async_copy(ins[0], outs[0].at[4 * x + 2 * y + c], sems[2])

    def start(ins, outs, sems):
        own(ins, outs, sems).start()
        for cp in copies(ins, outs, sems, False):
            cp.start()

    def finish(ins, outs, sems):
        for cp in copies(ins, outs, sems, True):
            cp.wait_recv()
        for cp in copies(ins, outs, sems, False):
            cp.wait_send()
        own(ins, outs, sems).wait()

    return _Rider(operands=[vec], out_shape=[jax.ShapeDtypeStruct((n_dev, *vec.shape), vec.dtype)], aliases={},
                  scratch=[pltpu.SemaphoreType.DMA((n_dev - 1,))] * 2 + [pltpu.SemaphoreType.DMA(())], start=start, finish=finish)


def _both(first, second):
    n_in, n_out, n_sem = len(first.operands), len(first.out_shape), len(first.scratch)

    def start(ins, outs, sems):
        first.start(ins[:n_in], outs[:n_out], sems[:n_sem])
        second.start(ins[n_in:], outs[n_out:], sems[n_sem:])

    def finish(ins, outs, sems):
        first.finish(ins[:n_in], outs[:n_out], sems[:n_sem])
        second.finish(ins[n_in:], outs[n_out:], sems[n_sem:])

    aliases = dict(first.aliases)
    aliases.update({n_in + i: n_out + o for i, o in second.aliases.items()})
    return _Rider(operands=list(first.operands) + list(second.operands), out_shape=list(first.out_shape) + list(second.out_shape),
                  aliases=aliases, scratch=list(first.scratch) + list(second.scratch), start=start, finish=finish)


def _row_tile(rows, row_bytes, budget):
    tile = rows
    for cand in (512, 256, 128, 64, 32, 16, 8):
        if rows % cand == 0:
            tile = cand
            if cand * row_bytes <= budget:
                break
    return tile


def _pair_sum_layer(part, got, core_axis, place, *, name):
    rows, cols = got.shape
    rows_t = _row_tile(rows, cols * 4 * 6, 16 * 1024 * 1024)
    steps = rows // rows_t
    if core_axis == 0:
        part_spec = pl.BlockSpec((rows_t, cols), lambda i, p: (p[0] * steps + i, 0))
    else:
        part_spec = pl.BlockSpec((rows_t, cols), lambda i, p: (i, p[0]))
    own_spec = pl.BlockSpec((rows_t, cols), lambda i, p: (i, 0))

    def body(place_ref, part_ref, got_ref, out_ref):
        out_ref[...] = (part_ref[...] + got_ref[...]).astype(BF16)

    return pl.pallas_call(
        body, name=name, out_shape=jax.ShapeDtypeStruct(got.shape, BF16),
        grid_spec=pltpu.PrefetchScalarGridSpec(num_scalar_prefetch=1, grid=(steps,), in_specs=[part_spec, own_spec], out_specs=own_spec),
        compiler_params=_cparams(("arbitrary",)),
    )(place, part, got)


def _chip_sum_layer(partial, got, blocks, layer, depth, chip_axis, core_axis, place, *, name):
    _, rows, cols = got.shape
    rows_t = _row_tile(rows, cols * 4 * 10, 24 * 1024 * 1024)
    steps = rows // rows_t
    if chip_axis == 0:
        own_spec = pl.BlockSpec((rows_t, cols), lambda i, p: (p[1] * steps + i, 0))
    else:
        own_spec = pl.BlockSpec((rows_t, cols), lambda i, p: (i, p[1]))
    got_spec = pl.BlockSpec((N_CHIPS - 1, rows_t, cols), lambda i, p: (0, i, 0))
    shape = [depth, rows, cols]
    shape[1 + core_axis] *= 2
    if core_axis == 0:
        out_spec = pl.BlockSpec((None, rows_t, cols), lambda i, p: (layer, p[0] * steps + i, 0))
    else:
        out_spec = pl.BlockSpec((None, rows_t, cols), lambda i, p: (layer, i, p[0]))

    def body(place_ref, own_ref, got_ref, *rest):
        out_ref = rest[-1]
        up = lambda val: val.astype(F32)
        out_ref[...] = ((up(own_ref[...]) + up(got_ref[0])) + up(got_ref[1])) + up(got_ref[2])

    in_specs, operands, aliases = [own_spec, got_spec], [place, partial, got], {}
    if blocks is not None:
        in_specs.append(pl.BlockSpec(memory_space=pl.ANY))
        operands.append(blocks)
        aliases = {3: 0}
    return pl.pallas_call(
        body, name=name, out_shape=jax.ShapeDtypeStruct(tuple(shape), F32),
        grid_spec=pltpu.PrefetchScalarGridSpec(num_scalar_prefetch=1, grid=(steps,), in_specs=in_specs, out_specs=out_spec),
        input_output_aliases=aliases, compiler_params=_cparams(("arbitrary",)),
    )(*operands)


def _sum_devices(stack, *, name):
    def body(src_ref, out_ref):
        total = src_ref[0]
        for d in range(1, stack.shape[0]):
            total = total + src_ref[d]
        out_ref[...] = total

    return pl.pallas_call(body, name=name, out_shape=jax.ShapeDtypeStruct(stack.shape[1:], F32))(stack)


def kernel(x, w_in, b_in, conv_w, conv_b, conv_ln_g, conv_ln_b, w_att_proj, w_conv_proj, b_conv_proj, w_out, ln_g, ln_b, loss_target, m_w_in, m_b_in, m_conv_w, m_conv_b, m_conv_ln_g, m_conv_ln_b, m_w_att_proj, m_w_conv_proj, m_b_conv_proj, m_w_out, m_ln_g, m_ln_b, v_w_in, v_b_in, v_conv_w, v_conv_b, v_conv_ln_g, v_conv_ln_b, v_w_att_proj, v_w_conv_proj, v_b_conv_proj, v_w_out, v_ln_g, v_ln_b):
    depth = w_in.shape[0]
    d_model = x.shape[-1]
    half = d_model // 2
    chip = 2 * lax.axis_index("x") + lax.axis_index("y")
    place = jnp.stack([lax.axis_index("c"), chip]).astype(jnp.int32)
    vec3 = lambda v: v.reshape(depth, 1, -1)

    taps = jnp.pad(conv_w, ((0, 0), (0, CONV_PAD - CONV_WIDTH), (0, 0)))
    gathered = [("w_in", w_in, 2, BF16), ("w_att_proj", w_att_proj, 2, BF16), ("w_conv_proj", w_conv_proj, 2, BF16),
                ("w_out", w_out, 1, BF16), ("conv_w", taps, 2, F32)]
    wholes = [_place_block(arr, axis, place, dtype, name="place_" + n) for n, arr, axis, dtype in gathered]
    loss, grad_x, packed, waiting, blocks = _train_pass(x[0], loss_target[0], wholes[:4], wholes[4], b_in, vec3(conv_b),
                                                        vec3(conv_ln_g), vec3(conv_ln_b), vec3(b_conv_proj), vec3(ln_g), vec3(ln_b), place)
    loss = lax.psum(loss[0, 0], ("x", "y", "c"))

    names = ["w_in", "b_in", "conv_w", "conv_b", "conv_ln_g", "conv_ln_b", "w_att_proj", "w_conv_proj", "b_conv_proj", "w_out", "ln_g", "ln_b"]
    weights = dict(zip(names, (w_in, b_in, conv_w, conv_b, conv_ln_g, conv_ln_b, w_att_proj, w_conv_proj, b_conv_proj, w_out, ln_g, ln_b)))
    first = dict(zip(names, (m_w_in, m_b_in, m_conv_w, m_conv_b, m_conv_ln_g, m_conv_ln_b, m_w_att_proj, m_w_conv_proj, m_b_conv_proj, m_w_out, m_ln_g, m_ln_b)))
    second = dict(zip(names, (v_w_in, v_b_in, v_conv_w, v_conv_b, v_conv_ln_g, v_conv_ln_b, v_w_att_proj, v_w_conv_proj, v_b_conv_proj, v_w_out, v_ln_g, v_ln_b)))
    flat = lambda arr: arr.reshape(-1, arr.shape[-1])

    def matrix_update(n, grad, span, into=None, rider=None):
        return _adamw(flat(weights[n]), flat(grad), flat(first[n]), flat(second[n]), name="adamw_" + n, echo=True, span=span,
                      into=into, rider=rider)

    last = _both(_exchange_rider(waiting, MATRIX_CHIP_AXIS), _small_rider(packed))
    upper, lower = {}, {}
    for t, n in enumerate(MATRICES):
        rows = weights[n].shape[1]
        span = (rows, (depth - 1) * rows)
        if t == 0:
            upper[n], (*got, gathered) = matrix_update(n, blocks[t], span, rider=last)
        else:
            upper[n] = matrix_update(n, blocks[t], span)
    blocks = _lone_call(_join_rider(_chip_sums(0, depth, waiting, got, blocks, place), [axis + 1 for axis in MATRIX_CORE_AXIS], 0, 1),
                        name="pair_join")
    for t, n in enumerate(MATRICES):
        lower[n] = matrix_update(n, blocks[t], (0, weights[n].shape[1]), into=upper[n])
    reduced = {}

    total = _sum_devices(gathered, name="sum_devices")
    widths = dict(b_in=b_in.shape[1], conv_b=half, conv_ln_g=half, conv_ln_b=half, b_conv_proj=d_model, ln_g=d_model, ln_b=d_model)
    offset = 0
    for n in VECTORS:
        reduced[n] = total[:, offset:offset + widths[n]]
        offset += widths[n]
    taps = total[:, offset:].reshape(depth, CONV_WIDTH, half)
    reduced["conv_w"] = lax.dynamic_slice_in_dim(taps, chip * conv_w.shape[2], conv_w.shape[2], axis=2)

    delta, new_m, new_v = {}, {}, {}
    for n in names:
        shape = weights[n].shape
        if n in MATRICES:
            reduced[n], d, m, v = lower[n]
        else:
            d, m, v = _adamw(flat(weights[n]), flat(reduced[n]), flat(first[n]), flat(second[n]), name="adamw_" + n)
        delta[n], new_m[n], new_v[n] = d.reshape(shape), m.reshape(shape), v.reshape(shape)
    return (loss, grad_x[None], *[reduced[n].reshape(weights[n].shape) for n in names], *[delta[n] for n in names],
            *[new_m[n] for n in names], *[new_v[n] for n in names])
```

```python
import functools
from typing import Callable, NamedTuple

import jax
import jax.numpy as jnp
from jax import lax
from jax.experimental import pallas as pl
from jax.experimental.pallas import tpu as pltpu

F32 = jnp.float32
BF16 = jnp.bfloat16

HEAD_DIM = 64
LANES = 128
CONV_WIDTH = 31
CONV_PAD = 32
SUBLANES = 8
LN_EPS = 1e-5
DEPTH = 4
DEEPNORM_ALPHA = (2 * DEPTH) ** 0.25
ATT_SCALE = HEAD_DIM ** -0.5
ATT_TILE = 128
ATT_DEAD = -88.0
ATT_GROUP = 2
ATT_GROUP_FWD = 4
ATT_FILL = -1e30

ADAM_LR = 0.001
ADAM_B1 = 0.9
ADAM_B2 = 0.999
ADAM_EPS = 1e-08
ADAM_WD = 0.01
ADAM_STEP = 10

VMEM_LIMIT = 56 * 1024 * 1024

N_CHIPS = 4


def _cparams(sem):
    return pltpu.CompilerParams(dimension_semantics=sem, vmem_limit_bytes=VMEM_LIMIT)


def _sigmoid(x):
    return 1.0 / (1.0 + jnp.exp(-x))


class _Rider(NamedTuple):
    operands: list
    out_shape: list
    aliases: dict
    scratch: list
    start: Callable
    finish: Callable


def _host_call(body, rider, *, name, grid, in_specs, out_specs, out_shape, scratch, operands, semantics, host_aliases=None):
    n_in, n_out = len(in_specs), len(out_specs)
    aliases = dict(host_aliases or {})
    if rider is not None:
        r_in, r_out = len(rider.operands), len(rider.out_shape)
        host_body = body

        def body(*refs):
            base = n_in + r_in
            ins, rins = refs[:n_in], refs[n_in:base]
            outs, routs = refs[base:base + n_out], refs[base + n_out:base + n_out + r_out]
            rest = refs[base + n_out + r_out:]
            split = len(rest) - len(rider.scratch)
            ids = [pl.program_id(axis) for axis in range(len(grid))]
            first = functools.reduce(jnp.logical_and, [i == 0 for i in ids])
            last = functools.reduce(jnp.logical_and, [i == g - 1 for i, g in zip(ids, grid)])

            @pl.when(first)
            def _():
                rider.start(rins, routs, rest[split:])

            host_body(*ins, *outs, *rest[:split])

            @pl.when(last)
            def _():
                rider.finish(rins, routs, rest[split:])

        hbm = pl.BlockSpec(memory_space=pl.ANY)
        in_specs = list(in_specs) + [hbm] * r_in
        out_specs = list(out_specs) + [hbm] * r_out
        out_shape = list(out_shape) + list(rider.out_shape)
        scratch = list(scratch) + list(rider.scratch)
        operands = list(operands) + list(rider.operands)
        aliases.update({n_in + i: n_out + o for i, o in rider.aliases.items()})
    res = pl.pallas_call(
        body, name=name, grid=grid, in_specs=list(in_specs), out_specs=list(out_specs), out_shape=list(out_shape),
        scratch_shapes=list(scratch), input_output_aliases=aliases, compiler_params=_cparams(semantics),
    )(*operands)
    return list(res[:n_out]), list(res[n_out:])


def _lone_call(rider, *, name):
    r_in = len(rider.operands)

    def body(*refs):
        ins, outs, sems = refs[:r_in], refs[r_in:r_in + len(rider.out_shape)], refs[r_in + len(rider.out_shape):]
        rider.start(ins, outs, sems)
        rider.finish(ins, outs, sems)

    hbm = pl.BlockSpec(memory_space=pl.ANY)
    return list(pl.pallas_call(
        body, name=name, in_specs=[hbm] * r_in, out_specs=[hbm] * len(rider.out_shape), out_shape=list(rider.out_shape),
        scratch_shapes=list(rider.scratch), input_output_aliases=dict(rider.aliases),
    )(*rider.operands))


def _fit(tile, dim):
    assert dim % LANES == 0
    tile = min(tile, dim) // LANES * LANES
    while dim % tile:
        tile -= LANES
    return tile


_DIMS = {"nn": ((1,), (0,)), "nt": ((1,), (1,)), "tn": ((0,), (0,))}


def _matmul(a, b, *, mode, name, layer=None, bias=None, add=None, colsum=False, out_dtype=F32, tm=1024, tn=512, tk=1024,
            rider=None):
    b_shape = b.shape if layer is None else b.shape[1:]
    if mode == "nn":
        (m, k), (k2, n) = a.shape, b_shape
    elif mode == "nt":
        (m, k), (n, k2) = a.shape, b_shape
    else:
        (k, m), (k2, n) = a.shape, b_shape
    assert k == k2
    tm, tn, tk = _fit(tm, m), _fit(tn, n), _fit(tk, k)
    gm, gn, nk = m // tm, n // tn, k // tk

    a_spec = pl.BlockSpec((tk, tm), lambda i, j, kk: (kk, i)) if mode == "tn" else pl.BlockSpec((tm, tk), lambda i, j, kk: (i, kk))
    if layer is None:
        b_spec = pl.BlockSpec((tn, tk), lambda i, j, kk: (j, kk)) if mode == "nt" else pl.BlockSpec((tk, tn), lambda i, j, kk: (kk, j))
    elif mode == "nt":
        b_spec = pl.BlockSpec((None, tn, tk), lambda i, j, kk: (layer, j, kk))
    else:
        b_spec = pl.BlockSpec((None, tk, tn), lambda i, j, kk: (layer, kk, j))
    in_specs, operands = [a_spec, b_spec], [a, b]
    if bias is not None:
        in_specs.append(pl.BlockSpec((1, tn), lambda i, j, kk: (0, j)))
        operands.append(bias)
    if add is not None:
        in_specs.append(pl.BlockSpec((tm, tn), lambda i, j, kk: (i, j)))
        operands.append(add)
    out_shape = [jax.ShapeDtypeStruct((m, n), out_dtype)]
    out_specs = [pl.BlockSpec((tm, tn), lambda i, j, kk: (i, j))]
    scratch = [pltpu.VMEM((tm, tn), F32)] if nk > 1 else []
    if colsum:
        assert mode == "tn"
        out_shape.append(jax.ShapeDtypeStruct((gm, 1, n), F32))
        out_specs.append(pl.BlockSpec((1, 1, tn), lambda i, j, kk: (i, 0, j)))
        if nk > 1:
            scratch.append(pltpu.VMEM((1, tn), F32))
    has_bias, has_add = bias is not None, add is not None

    def body(*refs):
        refs = list(refs)
        a_ref, b_ref = refs[0], refs[1]
        pos = 2
        bias_ref = add_ref = None
        if has_bias:
            bias_ref = refs[pos]
            pos += 1
        if has_add:
            add_ref = refs[pos]
            pos += 1
        o_ref = refs[pos]
        pos += 1
        cs_ref = None
        if colsum:
            cs_ref = refs[pos]
            pos += 1

        def finish(out, sums):
            if has_bias:
                out = out + bias_ref[...]
            if has_add:
                out = out + add_ref[...]
            o_ref[...] = out.astype(out_dtype)
            if colsum:
                cs_ref[0] = sums

        bv = b_ref[...]
        prod = lax.dot_general(a_ref[...].astype(BF16), bv.astype(BF16), (_DIMS[mode], ((), ())), preferred_element_type=F32)
        sums = jnp.sum(bv.astype(F32), axis=0, keepdims=True) if colsum else None
        if nk == 1:
            finish(prod, sums)
            return
        acc_ref = refs[pos]
        cs_acc = refs[pos + 1] if colsum else None
        kk = pl.program_id(2)

        @pl.when(kk == 0)
        def _():
            acc_ref[...] = jnp.zeros_like(acc_ref)
            if colsum:
                cs_acc[...] = jnp.zeros_like(cs_acc)

        acc_ref[...] += prod
        if colsum:
            cs_acc[...] += sums

        @pl.when(kk == nk - 1)
        def _():
            finish(acc_ref[...], cs_acc[...] if colsum else None)

    res, rode = _host_call(body, rider, name=name, grid=(gm, gn, nk), in_specs=in_specs, out_specs=out_specs, out_shape=out_shape,
                           scratch=scratch, operands=operands, semantics=("arbitrary", "arbitrary", "arbitrary"))
    out = (res[0], res[1][0]) if colsum else res[0]
    return out if rider is None else (out, rode)


def _scan_matrices():
    t = ATT_TILE
    r = lax.broadcasted_iota(jnp.int32, (t, t), 0)
    c = lax.broadcasted_iota(jnp.int32, (t, t), 1)
    ones = jnp.ones((t, t), F32)
    suffix = jnp.concatenate([(r > c).astype(F32), ones], axis=1)
    prefix = jnp.concatenate([(r < c).astype(F32), ones], axis=1)
    stack = lambda mat: jnp.concatenate([mat, mat], axis=0).astype(BF16)
    return stack(suffix), stack(prefix)


def _split_halves(val):
    hi = val.astype(BF16)
    lo = (val - hi.astype(F32)).astype(BF16)
    return jnp.concatenate([hi, lo], axis=1)


def _split_scan(val, mat_ref):
    return jnp.dot(_split_halves(val), mat_ref[...], preferred_element_type=F32)


def _pair_scores(q, k_lo, k_hi, masked):
    t = ATT_TILE
    z2 = lax.dot_general(q, jnp.concatenate([k_lo, k_hi], axis=0), (((1,), (1,)), ((), ())), preferred_element_type=F32)
    out = []
    for h in range(2):
        z = z2[:, h * t:(h + 1) * t]
        sp = jnp.log(1.0 + jnp.exp(-jnp.abs(z)))
        f = jnp.minimum(-z, 0.0) - sp
        a = f + z
        if masked:
            causal = lax.broadcasted_iota(jnp.int32, (t, t), 1) < lax.broadcasted_iota(jnp.int32, (t, t), 0)
            f = jnp.where(causal, f, 0.0)
        out.append((_split_halves(f), a))
    return out


def _any_alive(runs):
    top = functools.reduce(jnp.maximum, [run for per_head in runs for run in per_head])
    return (jnp.max(top) > ATT_DEAD).astype(jnp.int32)


def _head_copies(seq, src_ref, scale, lo_ref, hi_ref, plain_ref):
    chunk = min(256, seq)
    low = lax.broadcasted_iota(jnp.int32, (chunk, LANES), 1) < HEAD_DIM

    def step(r, carry):
        rows = pl.ds(pl.multiple_of(r * chunk, chunk), chunk)
        val = src_ref[rows, :]
        if scale != 1.0:
            val = val * scale
        if lo_ref is not None:
            lo_ref[rows, :] = jnp.where(low, val, 0.0).astype(BF16)
            hi_ref[rows, :] = jnp.where(low, 0.0, val).astype(BF16)
        if plain_ref is not None:
            plain_ref[rows, :] = val.astype(BF16)
        return carry

    lax.fori_loop(0, seq // chunk, step, 0)


def _attn_fwd(u, scan_suffix, *, seq, d_att, name, rider=None):
    t = ATT_TILE
    nq = seq // t
    pairs = d_att // LANES
    grp = ATT_GROUP_FWD
    assert nq % grp == 0

    def body(q_ref, k_ref, v_ref, um_ref, o_ref, qq, k0, k1, v0, v1, f2_s, a_s, lg_s, tot_s, run_s, acc_s):
        _head_copies(seq, q_ref, ATT_SCALE, None, None, qq)
        _head_copies(seq, k_ref, 1.0, k0, k1, None)
        _head_copies(seq, v_ref, 1.0, v0, v1, None)

        def group(gi, carry):
            qb0 = gi * grp
            qrows = [pl.ds(pl.multiple_of((qb0 + g) * t, t), t) for g in range(grp)]
            qv = [qq[qrows[g], :] for g in range(grp)]

            chains = [(h, g) for g in range(grp) for h in range(2)]

            def key_rows(g, i):
                return pl.ds(pl.multiple_of(jnp.maximum(qb0 + g - i, 0) * t, t), t)

            def stage1(i, masked):
                out = []
                for g in range(grp):
                    krows = key_rows(g, i)
                    out += _pair_scores(qv[g], k0[krows, :], k1[krows, :], masked)
                return out

            def stage2(halves, a, masked):
                scan = jnp.dot(halves, um_ref[...], preferred_element_type=F32)
                logit = a + scan[:, :t]
                if masked:
                    causal = lax.broadcasted_iota(jnp.int32, (t, t), 1) < lax.broadcasted_iota(jnp.int32, (t, t), 0)
                    logit = jnp.where(causal, logit, ATT_FILL)
                return logit, scan[:, t:]

            def put(halves_a=None, logit_total=None):
                for c in range(len(chains)):
                    if halves_a is not None:
                        f2_s[c], a_s[c] = halves_a[c]
                    if logit_total is not None:
                        lg_s[c], tot_s[c] = logit_total[c]

            first = stage1(0, True)
            put(halves_a=stage1(1, False), logit_total=[stage2(f2, a, True) for f2, a in first])
            for c in range(len(chains)):
                run_s[c] = jnp.zeros((t, t), F32)
            for g in range(grp):
                acc_s[g] = jnp.zeros((t, LANES), F32)

            def wbody(st):
                i = st[0]
                held = [(f2_s[c], a_s[c]) for c in range(len(chains))]
                logits = [lg_s[c] for c in range(len(chains))]
                totals = [tot_s[c] for c in range(len(chains))]
                runs = [run_s[c] for c in range(len(chains))]
                accs = [acc_s[g] for g in range(grp)]
                for g in range(grp):
                    krows = key_rows(g, i)
                    gone = jnp.where(qb0 + g - i >= 0, 0.0, ATT_FILL)
                    weights = []
                    for c in (2 * g, 2 * g + 1):
                        run = runs[c] + gone
                        weights.append(jnp.exp(logits[c] + run).astype(BF16))
                        runs[c] = run + totals[c]
                    accs[g] = accs[g] + jnp.dot(jnp.concatenate(weights, axis=1), jnp.concatenate([v0[krows, :], v1[krows, :]], axis=0),
                                                preferred_element_type=F32)
                ahead2 = [stage2(f2, a, False) for f2, a in held]
                ahead1 = stage1(i + 2, False)
                put(halves_a=ahead1, logit_total=ahead2)
                for c in range(len(chains)):
                    run_s[c] = runs[c]
                for g in range(grp):
                    acc_s[g] = accs[g]
                more = jnp.logical_and(i + 1 <= qb0 + grp - 1, _any_alive([runs]) > 0)
                return i + 1, more.astype(jnp.int32)

            lax.while_loop(lambda st: st[1] > 0, wbody, (jnp.int32(0), jnp.int32(1)))
            for g in range(grp):
                o_ref[qrows[g], :] = acc_s[g]
            return carry

        lax.fori_loop(0, nq // grp, group, 0)

    blk = lambda base: pl.BlockSpec((seq, LANES), lambda j, base=base: (0, base + j))
    res, rode = _host_call(
        body, rider, name=name, grid=(pairs,),
        in_specs=[blk(0), blk(pairs), blk(2 * pairs), pl.BlockSpec((2 * t, 2 * t), lambda j: (0, 0))],
        out_specs=[pl.BlockSpec((seq, LANES), lambda j: (0, j))],
        out_shape=[jax.ShapeDtypeStruct((seq, d_att), F32)],
        scratch=[pltpu.VMEM((seq, LANES), BF16)] * 5 + [pltpu.VMEM((2 * grp, t, 2 * t), BF16)]
        + [pltpu.VMEM((2 * grp, t, t), F32)] * 4 + [pltpu.VMEM((grp, t, LANES), F32)],
        operands=[u, u, u, scan_suffix], semantics=("arbitrary",))
    return res[0] if rider is None else (res[0], rode)


def _columns_copy(stage_ref, du_ref, rows, col, sem):
    width = stage_ref.shape[-1]
    cols = pl.ds(pl.multiple_of(col, LANES), width)
    return pltpu.make_async_copy(stage_ref, du_ref.at[slice(None) if rows is None else rows, cols], sem)


def _attn_bwd(u, d_att_out, du, scan_suffix, scan_prefix, *, seq, d_att, name, rider=None):
    t = ATT_TILE
    nq = seq // t
    pairs = d_att // LANES
    grp = ATT_GROUP
    assert nq % grp == 0

    def body(q_ref, k_ref, v_ref, do_ref, um_ref, pm_ref, du_old, du_ref,
             qq, q0, q1, k0, k1, v0, v1, dd, do0, do1, dq_acc, dk_acc, dv_acc, g_st, b_st,
             f2_s, a_s, lg_s, tot_s, dw_s, run_s, p_s, pt_s, grun_s, dq_s, dq_o, dk_o, dv_o, out_sem):
        _head_copies(seq, q_ref, ATT_SCALE, q0, q1, qq)
        _head_copies(seq, k_ref, 1.0, k0, k1, None)
        _head_copies(seq, v_ref, 1.0, v0, v1, None)
        _head_copies(seq, do_ref, 1.0, do0, do1, dd)
        dk_acc[...] = jnp.zeros_like(dk_acc)
        dv_acc[...] = jnp.zeros_like(dv_acc)

        tn = (((0,), (0,)), ((), ()))
        nt = (((1,), (1,)), ((), ()))

        def stacked(lo_ref, hi_ref, rows):
            return jnp.concatenate([lo_ref[rows, :], hi_ref[rows, :]], axis=0)

        def group(gi, carry):
            qb0 = gi * grp
            qrows = [pl.ds(pl.multiple_of((qb0 + g) * t, t), t) for g in range(grp)]
            qv = [qq[qrows[g], :] for g in range(grp)]
            dov = [dd[qrows[g], :] for g in range(grp)]
            q_heads = [stacked(q0, q1, qrows[g]) for g in range(grp)]
            do_heads = [stacked(do0, do1, qrows[g]) for g in range(grp)]

            def key_rows(g, i):
                return pl.ds(pl.multiple_of(jnp.maximum(qb0 + g - i, 0) * t, t), t)

            chains = [(h, g) for g in range(grp) for h in range(2)]
            every = range(len(chains))

            def stage1(i, masked):
                out = []
                for g in range(grp):
                    krows = key_rows(g, i)
                    out += _pair_scores(qv[g], k0[krows, :], k1[krows, :], masked)
                return out

            def stage2(i, held, masked):
                out = []
                for g in range(grp):
                    dw2 = lax.dot_general(dov[g], stacked(v0, v1, key_rows(g, i)), nt, preferred_element_type=F32)
                    for h in range(2):
                        c = 2 * g + h
                        halves, a = held[c]
                        scan = jnp.dot(halves, um_ref[...], preferred_element_type=F32)
                        logit = a + scan[:, :t]
                        if masked:
                            causal = lax.broadcasted_iota(jnp.int32, (t, t), 1) < lax.broadcasted_iota(jnp.int32, (t, t), 0)
                            logit = jnp.where(causal, logit, ATT_FILL)
                        b_st[c, i] = jnp.exp(a)
                        out.append((logit, scan[:, t:], dw2[:, h * t:(h + 1) * t]))
                return out

            def put(held=None, ready=None):
                for c in every:
                    if held is not None:
                        f2_s[c], a_s[c] = held[c]
                    if ready is not None:
                        lg_s[c], tot_s[c], dw_s[c] = ready[c]

            put(held=stage1(1, False), ready=stage2(0, stage1(0, True), True))
            for c in every:
                run_s[c] = jnp.zeros((t, t), F32)

            def wbody(st):
                i = st[0]
                held = [(f2_s[c], a_s[c]) for c in every]
                ready = [(lg_s[c], tot_s[c], dw_s[c]) for c in every]
                runs = [run_s[c] for c in every]
                for g in range(grp):
                    gone = jnp.where(qb0 + g - i >= 0, 0.0, ATT_FILL)
                    weights = []
                    for c in (2 * g, 2 * g + 1):
                        logit, total, dw = ready[c]
                        run = runs[c] + gone
                        w = jnp.exp(logit + run)
                        g_st[c, i] = w * dw
                        weights.append(w.astype(BF16))
                        runs[c] = run + total
                    dv_acc[key_rows(g, i), :] += lax.dot_general(jnp.concatenate(weights, axis=0), do_heads[g], tn,
                                                                 preferred_element_type=F32)
                ahead2 = stage2(i + 1, held, False)
                ahead1 = stage1(i + 2, False)
                put(held=ahead1, ready=ahead2)
                for c in every:
                    run_s[c] = runs[c]
                more = jnp.logical_and(i + 1 <= qb0 + grp - 1, _any_alive([runs]) > 0)
                return i + 1, more.astype(jnp.int32)

            steps = lax.while_loop(lambda st: st[1] > 0, wbody, (jnp.int32(0), jnp.int32(1)))[0]

            def prefix(i):
                out = []
                for c in every:
                    scan = _split_scan(g_st[c, i], pm_ref)
                    out.append((scan[:, :t], scan[:, t:]))
                return out

            def back(i, masked):
                sums = [(p_s[c], pt_s[c]) for c in every]
                gruns = [grun_s[c] for c in every]
                dqs = [dq_s[g] for g in range(grp)]
                for g in range(grp):
                    krows = key_rows(g, i)
                    dzs = []
                    for c in (2 * g, 2 * g + 1):
                        gt = g_st[c, i]
                        dz = gt - b_st[c, i] * (gt + gruns[c] + sums[c][0])
                        if masked:
                            causal = lax.broadcasted_iota(jnp.int32, (t, t), 1) < lax.broadcasted_iota(jnp.int32, (t, t), 0)
                            dz = jnp.where(causal, dz, 0.0)
                        dzs.append(dz.astype(BF16))
                        gruns[c] = gruns[c] + sums[c][1]
                    dqs[g] = dqs[g] + jnp.dot(jnp.concatenate(dzs, axis=1), stacked(k0, k1, krows), preferred_element_type=F32)
                    dk_acc[krows, :] += lax.dot_general(jnp.concatenate(dzs, axis=0), q_heads[g], tn, preferred_element_type=F32)
                return gruns, dqs

            def keep(sums=None, gruns=None, dqs=None):
                for c in every:
                    if sums is not None:
                        p_s[c], pt_s[c] = sums[c]
                    if gruns is not None:
                        grun_s[c] = gruns[c]
                if dqs is not None:
                    for g in range(grp):
                        dq_s[g] = dqs[g]

            keep(sums=prefix(steps - 1), gruns=[jnp.zeros((t, t), F32)] * len(chains), dqs=[jnp.zeros((t, LANES), F32)] * grp)

            def bbody(j, carry2):
                i = steps - 1 - j
                gruns, dqs = back(i, False)
                keep(sums=prefix(i - 1), gruns=gruns, dqs=dqs)
                return carry2

            lax.fori_loop(0, steps - 1, bbody, 0)
            _, dqs = back(0, True)
            for g in range(grp):
                dq_acc[qrows[g], :] = dqs[g]
            return carry

        lax.fori_loop(0, nq // grp, group, 0)

        pair = pl.program_id(0)
        stages = (dq_o, dk_o, dv_o)

        def copies(j):
            return [_columns_copy(stage, du_ref, None, k * d_att + j * LANES, out_sem.at[k]) for k, stage in enumerate(stages)]

        @pl.when(pair > 0)
        def _():
            for cp in copies(pair - 1):
                cp.wait()

        chunk = min(256, seq)

        def emit(r, carry):
            rows = pl.ds(pl.multiple_of(r * chunk, chunk), chunk)
            dq_o[rows, :] = (dq_acc[rows, :] * ATT_SCALE).astype(BF16)
            dk_o[rows, :] = dk_acc[rows, :].astype(BF16)
            dv_o[rows, :] = dv_acc[rows, :].astype(BF16)
            return carry

        lax.fori_loop(0, seq // chunk, emit, 0)
        for cp in copies(pair):
            cp.start()

        @pl.when(pair == pairs - 1)
        def _():
            for cp in copies(pair):
                cp.wait()

    blk = lambda base: pl.BlockSpec((seq, LANES), lambda j, base=base: (0, base + j), pipeline_mode=pl.Buffered(1))
    mat = pl.BlockSpec((2 * t, 2 * t), lambda j: (0, 0))
    hbm = pl.BlockSpec(memory_space=pl.ANY)
    res, rode = _host_call(
        body, rider, name=name, grid=(pairs,),
        in_specs=[blk(0), blk(pairs), blk(2 * pairs), blk(0), mat, mat, hbm],
        out_specs=[hbm], out_shape=[jax.ShapeDtypeStruct(du.shape, du.dtype)], host_aliases={6: 0},
        scratch=[pltpu.VMEM((seq, LANES), BF16)] * 10 + [pltpu.VMEM((seq, LANES), F32)] * 3
        + [pltpu.VMEM((2 * grp, nq + 1, t, t), F32)] * 2 + [pltpu.VMEM((2 * grp, t, 2 * t), BF16)]
        + [pltpu.VMEM((2 * grp, t, t), F32)] * 8 + [pltpu.VMEM((grp, t, LANES), F32)]
        + [pltpu.VMEM((seq, LANES), BF16)] * 3 + [pltpu.SemaphoreType.DMA((3,))],
        operands=[u, u, u, d_att_out, scan_suffix, scan_prefix, du], semantics=("arbitrary",))
    return res[0] if rider is None else (res[0], rode)


CONV_ROWS = 256


def _shifted(window, residue, rows):
    total = rows + CONV_PAD
    return window if residue == 0 else pltpu.roll(window, total - residue, 0)


def _glu_to_pad(seq, a_ref, b_ref, pad_ref):
    chunk = min(CONV_ROWS, seq)
    pad_ref[pl.ds(0, CONV_PAD), :] = jnp.zeros((CONV_PAD, LANES), F32)

    def step(r, carry):
        rows = pl.ds(pl.multiple_of(r * chunk, chunk), chunk)
        pad_ref[pl.ds(pl.multiple_of(r * chunk + CONV_PAD, SUBLANES), chunk), :] = a_ref[rows, :] * _sigmoid(b_ref[rows, :])
        return carry

    lax.fori_loop(0, seq // chunk, step, 0)


def _conv_fwd(u, conv_w, conv_b, *, layer, seq, d_conv, col_a, col_b, name, rider=None):
    blocks = d_conv // LANES
    rows_t = min(CONV_ROWS, seq)
    shift0 = CONV_PAD - (CONV_WIDTH - 1)

    def body(a_ref, b_ref, w_ref, bias_ref, o_ref, pad_ref):
        _glu_to_pad(seq, a_ref, b_ref, pad_ref)

        def step(r, carry):
            base = pl.multiple_of(r * rows_t, rows_t)
            window = pad_ref[pl.ds(base, rows_t + CONV_PAD), :]
            acc = jnp.zeros((rows_t, LANES), F32) + bias_ref[...]
            for residue in range(SUBLANES):
                moved = _shifted(window, residue, rows_t)
                for tap in range(CONV_WIDTH):
                    if (shift0 + tap) % SUBLANES == residue:
                        lo = (shift0 + tap) - residue
                        acc = acc + w_ref[tap:tap + 1, :] * moved[lo:lo + rows_t, :]
            o_ref[pl.ds(base, rows_t), :] = acc
            return carry

        lax.fori_loop(0, seq // rows_t, step, 0)

    res, rode = _host_call(
        body, rider, name=name, grid=(blocks,),
        in_specs=[pl.BlockSpec((seq, LANES), lambda j: (0, col_a + j)), pl.BlockSpec((seq, LANES), lambda j: (0, col_b + j)),
                  pl.BlockSpec((None, CONV_PAD, LANES), lambda j: (layer, 0, j)),
                  pl.BlockSpec((None, 1, LANES), lambda j: (layer, 0, j))],
        out_specs=[pl.BlockSpec((seq, LANES), lambda j: (0, j))],
        out_shape=[jax.ShapeDtypeStruct((seq, d_conv), F32)],
        scratch=[pltpu.VMEM((seq + CONV_PAD, LANES), F32)],
        operands=[u, u, conv_w, conv_b], semantics=("arbitrary",))
    return res[0] if rider is None else (res[0], rode)


def _conv_bwd(u, dc1, du, conv_w, *, layer, seq, d_conv, col_a, col_b, name):
    blocks = d_conv // LANES
    rows_t = min(CONV_ROWS, seq)
    shift0 = CONV_PAD - (CONV_WIDTH - 1)

    def body(a_ref, b_ref, d_ref, w_ref, du_old, du_ref, dw_ref, pad_ref, dpad_ref, dw_acc, da_ref, db_ref, out_sem):
        block = pl.program_id(0)

        def copies(j):
            return [_columns_copy(stage, du_ref, None, (base + j) * LANES, out_sem.at[k])
                    for k, (stage, base) in enumerate(((da_ref, col_a), (db_ref, col_b)))]

        _glu_to_pad(seq, a_ref, b_ref, pad_ref)
        dpad_ref[pl.ds(seq, CONV_PAD), :] = jnp.zeros((CONV_PAD, LANES), F32)

        def fill(r, carry):
            rows = pl.ds(pl.multiple_of(r * rows_t, rows_t), rows_t)
            dpad_ref[rows, :] = d_ref[rows, :]
            return carry

        lax.fori_loop(0, seq // rows_t, fill, 0)
        dw_acc[...] = jnp.zeros_like(dw_acc)

        @pl.when(block > 0)
        def _():
            for cp in copies(block - 1):
                cp.wait()

        def step(r, carry):
            base = pl.multiple_of(r * rows_t, rows_t)
            rows = pl.ds(base, rows_t)
            window = dpad_ref[pl.ds(base, rows_t + CONV_PAD), :]
            acc = jnp.zeros((rows_t, LANES), F32)
            for residue in range(SUBLANES):
                moved = _shifted(window, residue, rows_t)
                for tap in range(CONV_WIDTH):
                    off = CONV_WIDTH - 1 - tap
                    if off % SUBLANES == residue:
                        lo = off - residue
                        acc = acc + w_ref[tap:tap + 1, :] * moved[lo:lo + rows_t, :]
            sig = _sigmoid(b_ref[rows, :])
            a = a_ref[rows, :]
            da_ref[rows, :] = (acc * sig).astype(BF16)
            db_ref[rows, :] = (acc * a * sig * (1.0 - sig)).astype(BF16)
            dcur = d_ref[rows, :]
            cwin = pad_ref[pl.ds(base, rows_t + CONV_PAD), :]
            for residue in range(SUBLANES):
                moved = _shifted(cwin, residue, rows_t)
                for tap in range(CONV_WIDTH):
                    if (shift0 + tap) % SUBLANES == residue:
                        lo = (shift0 + tap) - residue
                        prod = dcur * moved[lo:lo + rows_t, :]
                        dw_acc[tap] += jnp.sum(prod.reshape(rows_t // SUBLANES, SUBLANES, LANES), axis=0)
            return carry

        lax.fori_loop(0, seq // rows_t, step, 0)
        dw_ref[...] = jnp.sum(dw_acc[...], axis=1)
        for cp in copies(block):
            cp.start()

        @pl.when(block == blocks - 1)
        def _():
            for cp in copies(block):
                cp.wait()

    col = lambda base: pl.BlockSpec((seq, LANES), lambda j, base=base: (0, base + j))
    own = pl.BlockSpec((seq, LANES), lambda j: (0, j))
    hbm = pl.BlockSpec(memory_space=pl.ANY)
    return pl.pallas_call(
        body, name=name, grid=(blocks,),
        in_specs=[col(col_a), col(col_b), own, pl.BlockSpec((None, CONV_PAD, LANES), lambda j: (layer, 0, j)), hbm],
        out_specs=[hbm, pl.BlockSpec((CONV_PAD, LANES), lambda j: (0, j))],
        out_shape=[jax.ShapeDtypeStruct(du.shape, du.dtype), jax.ShapeDtypeStruct((CONV_PAD, d_conv), F32)],
        scratch_shapes=[pltpu.VMEM((seq + CONV_PAD, LANES), F32), pltpu.VMEM((seq + CONV_PAD, LANES), F32),
                        pltpu.VMEM((CONV_PAD, SUBLANES, LANES), F32), pltpu.VMEM((seq, LANES), BF16), pltpu.VMEM((seq, LANES), BF16),
                        pltpu.SemaphoreType.DMA((2,))],
        input_output_aliases={4: 0}, compiler_params=_cparams(("arbitrary",)),
    )(u, u, dc1, conv_w, du)


MIX_ROWS = 256


def _layer_norm_stats(val):
    mu = jnp.mean(val, axis=-1, keepdims=True)
    cen = val - mu
    var = jnp.mean(cen * cen, axis=-1, keepdims=True)
    rstd = lax.rsqrt(var + LN_EPS)
    return cen * rstd, rstd


def _layer_norm_bwd(dy, xhat, rstd, gain):
    dxhat = dy * gain
    m1 = jnp.mean(dxhat, axis=-1, keepdims=True)
    m2 = jnp.mean(dxhat * xhat, axis=-1, keepdims=True)
    dx = rstd * (dxhat - m1 - xhat * m2)
    return dx, jnp.sum(dy * xhat, axis=0, keepdims=True), jnp.sum(dy, axis=0, keepdims=True)


def _mix_forward(zatt, att, c1, zconv, gatt, gconv, x, w_att, w_conv, w_out, cln_g, cln_b, b_proj):
    s_zatt = _sigmoid(zatt)
    a_in = att * (zatt * s_zatt)
    chat, c_rstd = _layer_norm_stats(c1)
    c2 = chat * cln_g + cln_b
    s_c2 = _sigmoid(c2)
    c3 = c2 * s_c2
    s_zconv = _sigmoid(zconv)
    c_in = c3 * (zconv * s_zconv)
    a_in_b, c_in_b = a_in.astype(BF16), c_in.astype(BF16)
    ab = jnp.dot(a_in_b, w_att, preferred_element_type=F32)
    cb = jnp.dot(c_in_b, w_conv, preferred_element_type=F32) + b_proj
    s_gatt, s_gconv = _sigmoid(gatt), _sigmoid(gconv)
    merged_b = (s_gatt * ab + s_gconv * cb).astype(BF16)
    y = jnp.dot(merged_b, w_out, preferred_element_type=F32)
    h = DEEPNORM_ALPHA * x + y
    return dict(s_zatt=s_zatt, a_in_b=a_in_b, chat=chat, c_rstd=c_rstd, c2=c2, s_c2=s_c2, c3=c3, s_zconv=s_zconv,
                c_in_b=c_in_b, ab=ab, cb=cb, s_gatt=s_gatt, s_gconv=s_gconv, merged_b=merged_b, h=h)


def _u_blocks(rows_t, width, half):
    return [pl.BlockSpec((rows_t, half), lambda i, c=c: (i, c)) for c in (3, 6, 7, 8, 9, 10)]


def _of_layer(arr, layer):
    return pl.BlockSpec((None,) + arr.shape[1:], lambda i: (layer, 0, 0))


def _mix_fwd(u, att, c1, x, w_att, w_conv, w_out, cln_g, cln_b, b_proj, ln_g, ln_b, *, layer, seq, d_model, name):
    half = d_model // 2
    rows_t = min(MIX_ROWS, seq)

    def body(zatt_ref, zconv_ref, ga0, ga1, gc0, gc1, att_ref, c1_ref, x_ref, wa_ref, wc_ref, wo_ref,
             cg_ref, cb_ref, bp_ref, lg_ref, lb_ref, o_ref, ob_ref):
        gatt = jnp.concatenate([ga0[...], ga1[...]], axis=1)
        gconv = jnp.concatenate([gc0[...], gc1[...]], axis=1)
        mid = _mix_forward(zatt_ref[...], att_ref[...], c1_ref[...], zconv_ref[...], gatt, gconv, x_ref[...],
                           wa_ref[...], wc_ref[...], wo_ref[...], cg_ref[...], cb_ref[...], bp_ref[...])
        xhat, _ = _layer_norm_stats(mid["h"])
        out = xhat * lg_ref[...] + lb_ref[...]
        o_ref[...] = out
        ob_ref[...] = out.astype(BF16)

    row = lambda width: pl.BlockSpec((rows_t, width), lambda i: (i, 0))
    full = lambda arr: _of_layer(arr, layer)
    out = pl.BlockSpec((rows_t, d_model), lambda i: (i, 0))
    return pl.pallas_call(
        body, name=name, grid=(seq // rows_t,),
        in_specs=_u_blocks(rows_t, d_model, half) + [row(half), row(half), row(d_model), full(w_att), full(w_conv), full(w_out),
                                                     full(cln_g), full(cln_b), full(b_proj), full(ln_g), full(ln_b)],
        out_specs=[out, out],
        out_shape=[jax.ShapeDtypeStruct((seq, d_model), F32), jax.ShapeDtypeStruct((seq, d_model), BF16)],
        compiler_params=_cparams(("parallel",)),
    )(u, u, u, u, u, u, att, c1, x, w_att, w_conv, w_out, cln_g, cln_b, b_proj, ln_g, ln_b)


def _mix_bwd(u, att, c1, x, dxn, w_att, w_conv, w_out, cln_g, cln_b, b_proj, ln_g, *, layer, seq, d_model, name):
    half = d_model // 2
    rows_t = min(MIX_ROWS, seq)
    steps = seq // rows_t
    nt = ((1,), (1,))
    tn = ((0,), (0,))

    def body(zatt_ref, zconv_ref, ga0, ga1, gc0, gc1, att_ref, c1_ref, x_ref, dxn_ref, wa_ref, wc_ref, wo_ref,
             cg_ref, cb_ref, bp_ref, lg_ref,
             du_ref, datt_ref, dc1_ref, dxres_ref, dwa_ref, dwc_ref, dwo_ref,
             dcg_ref, dcb_ref, dcbias_ref, dbp_ref, dlg_ref, dlb_ref, zatt_stage, tail_stage, out_sem):
        sums = (dwa_ref, dwc_ref, dwo_ref, dcg_ref, dcb_ref, dcbias_ref, dbp_ref, dlg_ref, dlb_ref)
        tile = pl.program_id(0)
        slot = tile % 2
        dzatt_ref, tail_ref = zatt_stage.at[slot], tail_stage.at[slot]
        dzconv_ref, dgatt_ref, dgconv_ref = tail_ref.at[:, :half], tail_ref.at[:, half:3 * half], tail_ref.at[:, 3 * half:]

        def copies(i):
            rows = pl.ds(pl.multiple_of(i * rows_t, rows_t), rows_t)
            return [_columns_copy(zatt_stage.at[i % 2], du_ref, rows, 3 * half, out_sem.at[i % 2, 0]),
                    _columns_copy(tail_stage.at[i % 2], du_ref, rows, 6 * half, out_sem.at[i % 2, 1])]

        @pl.when(tile == 0)
        def _():
            for ref in sums:
                ref[...] = jnp.zeros_like(ref)

        zatt, zconv, att = zatt_ref[...], zconv_ref[...], att_ref[...]
        gatt = jnp.concatenate([ga0[...], ga1[...]], axis=1)
        gconv = jnp.concatenate([gc0[...], gc1[...]], axis=1)
        wa, wc, wo = wa_ref[...], wc_ref[...], wo_ref[...]
        mid = _mix_forward(zatt, att, c1_ref[...], zconv, gatt, gconv, x_ref[...], wa, wc, wo,
                           cg_ref[...], cb_ref[...], bp_ref[...])
        xhat, rstd = _layer_norm_stats(mid["h"])
        dh, dlg, dlb = _layer_norm_bwd(dxn_ref[...], xhat, rstd, lg_ref[...])
        dlg_ref[...] += dlg
        dlb_ref[...] += dlb
        dxres_ref[...] = DEEPNORM_ALPHA * dh
        dy = dh.astype(BF16)
        dwo_ref[...] += lax.dot_general(mid["merged_b"], dy, (tn, ((), ())), preferred_element_type=F32)
        dmerged = lax.dot_general(dy, wo, (nt, ((), ())), preferred_element_type=F32)
        s_ga, s_gc, ab, cb = mid["s_gatt"], mid["s_gconv"], mid["ab"], mid["cb"]
        dgatt_ref[...] = (dmerged * ab * s_ga * (1.0 - s_ga)).astype(BF16)
        dgconv_ref[...] = (dmerged * cb * s_gc * (1.0 - s_gc)).astype(BF16)
        dab = dmerged * s_ga
        dcb = dmerged * s_gc
        dbp_ref[...] += jnp.sum(dcb, axis=0, keepdims=True)
        dab_b, dcb_b = dab.astype(BF16), dcb.astype(BF16)
        dwa_ref[...] += lax.dot_general(mid["a_in_b"], dab_b, (tn, ((), ())), preferred_element_type=F32)
        da_in = lax.dot_general(dab_b, wa, (nt, ((), ())), preferred_element_type=F32)
        s_za = mid["s_zatt"]
        datt_ref[...] = da_in * (zatt * s_za)
        dzatt_ref[...] = (da_in * att * (s_za * (1.0 + zatt * (1.0 - s_za)))).astype(BF16)
        dwc_ref[...] += lax.dot_general(mid["c_in_b"], dcb_b, (tn, ((), ())), preferred_element_type=F32)
        dc_in = lax.dot_general(dcb_b, wc, (nt, ((), ())), preferred_element_type=F32)
        s_zc, c2, s_c2 = mid["s_zconv"], mid["c2"], mid["s_c2"]
        dzconv_ref[...] = (dc_in * mid["c3"] * (s_zc * (1.0 + zconv * (1.0 - s_zc)))).astype(BF16)
        dc3 = dc_in * (zconv * s_zc)
        dc2 = dc3 * (s_c2 * (1.0 + c2 * (1.0 - s_c2)))
        dc1, dcg, dcbeta = _layer_norm_bwd(dc2, mid["chat"], mid["c_rstd"], cg_ref[...])
        dcg_ref[...] += dcg
        dcb_ref[...] += dcbeta
        dcbias_ref[...] += jnp.sum(dc1, axis=0, keepdims=True)
        dc1_ref[...] = dc1
        for cp in copies(tile):
            cp.start()

        @pl.when(tile > 0)
        def _():
            for cp in copies(tile - 1):
                cp.wait()

        @pl.when(tile == steps - 1)
        def _():
            for cp in copies(tile):
                cp.wait()

    row = lambda width: pl.BlockSpec((rows_t, width), lambda i: (i, 0))
    full = lambda arr: _of_layer(arr, layer)
    whole = lambda r, c: pl.BlockSpec((r, c), lambda i: (0, 0))
    sds = jax.ShapeDtypeStruct
    out_specs = [pl.BlockSpec(memory_space=pl.ANY), row(half), row(half), row(d_model),
                 whole(half, d_model), whole(half, d_model), whole(d_model, d_model),
                 whole(1, half), whole(1, half), whole(1, half), whole(1, d_model), whole(1, d_model), whole(1, d_model)]
    out_shape = [sds((seq, u.shape[1]), BF16), sds((seq, half), F32), sds((seq, half), F32), sds((seq, d_model), F32),
                 sds((half, d_model), F32), sds((half, d_model), F32), sds((d_model, d_model), F32),
                 sds((1, half), F32), sds((1, half), F32), sds((1, half), F32),
                 sds((1, d_model), F32), sds((1, d_model), F32), sds((1, d_model), F32)]
    return pl.pallas_call(
        body, name=name, grid=(steps,),
        in_specs=_u_blocks(rows_t, d_model, half) + [row(half), row(half), row(d_model), row(d_model), full(w_att), full(w_conv),
                                                     full(w_out), full(cln_g), full(cln_b), full(b_proj), full(ln_g)],
        out_specs=out_specs, out_shape=out_shape,
        scratch_shapes=[pltpu.VMEM((2, rows_t, half), BF16), pltpu.VMEM((2, rows_t, 5 * half), BF16), pltpu.SemaphoreType.DMA((2, 2))],
        compiler_params=_cparams(("arbitrary",)),
    )(u, u, u, u, u, u, att, c1, x, dxn, w_att, w_conv, w_out, cln_g, cln_b, b_proj, ln_g)


def _loss_head(y, target, *, seq, d_model, name):
    rows_t = min(512, seq)

    def body(y_ref, t_ref, dy_ref, loss_ref):
        @pl.when(pl.program_id(0) == 0)
        def _():
            loss_ref[...] = jnp.zeros_like(loss_ref)

        err = y_ref[...] - t_ref[...]
        dy_ref[...] = err * (1.0 / d_model)
        per_token = jnp.sum(err * err, axis=-1, keepdims=True) * (1.0 / d_model)
        loss_ref[...] += 0.5 * jnp.sum(per_token, axis=0, keepdims=True)

    row = pl.BlockSpec((rows_t, d_model), lambda i: (i, 0))
    return pl.pallas_call(
        body, name=name, grid=(seq // rows_t,), in_specs=[row, row],
        out_specs=[row, pl.BlockSpec((1, 1), lambda i: (0, 0))],
        out_shape=[jax.ShapeDtypeStruct((seq, d_model), F32), jax.ShapeDtypeStruct((1, 1), F32)],
        compiler_params=_cparams(("arbitrary",)),
    )(y, target)


def _adamw(w, g, m, v, *, name, echo=False):
    rows, cols = w.shape
    rows_t = rows
    for cand in (512, 256, 128, 64, 32, 16, 8):
        if rows % cand == 0 and cand * cols * 4 <= 2 * 1024 * 1024:
            rows_t = cand
            break
    n_out = 4 if echo else 3

    def body(w_ref, g_ref, m_ref, v_ref, *outs):
        d_ref, nm_ref, nv_ref = outs[-3:]
        grad = g_ref[...]
        if echo:
            outs[0][...] = grad
        new_m = ADAM_B1 * m_ref[...] + (1.0 - ADAM_B1) * grad
        new_v = ADAM_B2 * v_ref[...] + (1.0 - ADAM_B2) * (grad * grad)
        m_hat = new_m / (1.0 - ADAM_B1 ** ADAM_STEP)
        v_hat = new_v / (1.0 - ADAM_B2 ** ADAM_STEP)
        d_ref[...] = -ADAM_LR * (m_hat / (jnp.sqrt(v_hat) + ADAM_EPS) + ADAM_WD * w_ref[...])
        nm_ref[...] = new_m
        nv_ref[...] = new_v

    blk = pl.BlockSpec((rows_t, cols), lambda i: (i, 0))
    out = jax.ShapeDtypeStruct((rows, cols), F32)
    return pl.pallas_call(
        body, name=name, grid=(rows // rows_t,), in_specs=[blk] * 4, out_specs=[blk] * n_out, out_shape=[out] * n_out,
        compiler_params=_cparams(("parallel",)),
    )(w, g, m, v)


MATRICES = ("w_in", "w_att_proj", "w_conv_proj", "w_out")
VECTORS = ("b_in", "conv_b", "conv_ln_g", "conv_ln_b", "b_conv_proj", "ln_g", "ln_b")
MATRIX_CHIP_AXIS = (1, 1, 1, 0)
MATRIX_CORE_AXIS = (0, 0, 0, 1)


def _chip_sums(layer, depth, partials, got, blocks, place):
    return [_chip_sum_layer(partials[t], got[t], blocks[t], layer, depth, MATRIX_CHIP_AXIS[t], MATRIX_CORE_AXIS[t], place,
                            name="chip_sum_" + MATRICES[t]) for t in range(len(MATRICES))]


def _train_pass(x, target, mats, taps, b_in, conv_b, cln_g, cln_b, b_proj, ln_g, ln_b, place):
    seq, d_model = x.shape
    half = d_model // 2
    depth = b_in.shape[0]
    scan_suffix, scan_prefix = _scan_matrices()
    cols = half // LANES
    dims = dict(seq=seq, d_model=d_model)
    conv_dims = dict(seq=seq, d_conv=half, col_a=4 * cols, col_b=5 * cols)
    axes3 = [axis + 1 for axis in MATRIX_CHIP_AXIS]
    n_mat = len(mats)

    def first_layers(relay):
        return _gather_rider([mats[0], taps], [axes3[0], 2], [(0, 1), (0, depth)], relay)

    mats = list(mats)
    mats[0], taps = _lone_call(_chain(first_layers(False), first_layers(True)), name="gather_first")

    xs, xbs, us, atts, c1s = [x], [x.astype(BF16)], [], [], []
    for l in range(depth):
        u = _matmul(xbs[l], mats[0], layer=l, mode="nn", bias=b_in[l].reshape(1, -1), name="in_proj", tm=256, tn=b_in.shape[1],
                    tk=d_model)
        nxt = [(l + 1, 1)] + [(l + 1, 1) if l else (0, 2)] * (n_mat - 1)
        if l + 1 < depth:
            att, mats = _attn_fwd(u, scan_suffix, seq=seq, d_att=half, name="attn_fwd", rider=_gather_rider(mats, axes3, nxt, False))
            c1, mats = _conv_fwd(u, taps, conv_b, layer=l, name="conv_fwd", rider=_gather_rider(mats, axes3, nxt, True), **conv_dims)
        else:
            att = _attn_fwd(u, scan_suffix, seq=seq, d_att=half, name="attn_fwd_last")
            c1 = _conv_fwd(u, taps, conv_b, layer=l, name="conv_fwd_last", **conv_dims)
        xn, xnb = _mix_fwd(u, att, c1, xs[l], mats[1], mats[2], mats[3], cln_g, cln_b, b_proj, ln_g, ln_b, layer=l, name="mix_fwd", **dims)
        us.append(u)
        atts.append(att)
        c1s.append(c1)
        xs.append(xn)
        xbs.append(xnb)
    w_in, w_att, w_conv, w_out = mats

    dx, loss = _loss_head(xs[depth], target, name="loss_head", **dims)
    grads = [None] * depth
    blocks = [None] * n_mat
    waiting = None
    for l in reversed(range(depth)):
        u = us[l]
        (du, datt, dc1, dxres, dwa, dwc, dwo, dcg, dcb, dcbias, dbp, dlg, dlb) = _mix_bwd(
            u, atts[l], c1s[l], xs[l], dx, w_att, w_conv, w_out, cln_g, cln_b, b_proj, ln_g, layer=l, name="mix_bwd", **dims)
        if waiting is None:
            du = _attn_bwd(u, datt, du, scan_suffix, scan_prefix, seq=seq, d_att=half, name="attn_bwd_first")
        else:
            du, got = _attn_bwd(u, datt, du, scan_suffix, scan_prefix, seq=seq, d_att=half, name="attn_bwd",
                                rider=_exchange_rider(waiting[1], MATRIX_CHIP_AXIS))
            blocks = _chip_sums(waiting[0], depth, waiting[1], got, blocks, place)
        du, dconvw = _conv_bwd(u, dc1, du, taps, layer=l, name="conv_bwd", **conv_dims)
        dwin, dbin = _matmul(xbs[l], du, mode="tn", colsum=True, name="in_proj_dw", tm=1024, tn=512, tk=seq)
        parts = [dwin, dwa, dwc, dwo]
        swap = _swap_rider(parts, MATRIX_CORE_AXIS)
        if l == 0:
            swap = _both(swap, _join_rider(blocks, [axis + 1 for axis in MATRIX_CORE_AXIS], 1, depth - 1))
        dx, rode = _matmul(du, w_in, layer=l, mode="nt", add=dxres, name="in_proj_dx", tm=512, tn=1024, tk=du.shape[1], rider=swap)
        got, blocks = rode[:n_mat], (rode[n_mat:] if l == 0 else blocks)
        waiting = (l, [_pair_sum_layer(parts[t], got[t], MATRIX_CORE_AXIS[t], place, name="pair_sum_" + MATRICES[t])
                       for t in range(n_mat)])
        grads[l] = dict(b_in=dbin[0], conv_w=dconvw[:CONV_WIDTH], conv_b=dcbias[0], conv_ln_g=dcg[0], conv_ln_b=dcb[0],
                        b_conv_proj=dbp[0], ln_g=dlg[0], ln_b=dlb[0])
    packed = jnp.stack([jnp.concatenate([grads[l][n] for n in VECTORS] + [grads[l]["conv_w"].reshape(-1)]) for l in range(depth)])
    return loss, dx, packed, waiting[1], blocks


MESH = pl.DeviceIdType.MESH


def _position():
    x, y, c = lax.axis_index("x"), lax.axis_index("y"), lax.axis_index("c")
    return x, y, c, [(1 - x, y), (x, 1 - y), (1 - x, 1 - y)]


def _cut(ref, axis, start, size):
    idx = [slice(None)] * len(ref.shape)
    idx[axis] = pl.ds(start, size)
    return ref.at[tuple(idx)]


def _remote(src, dst, send_sem, recv_sem, device):
    return pltpu.make_async_remote_copy(src_ref=src, dst_ref=dst, send_sem=send_sem, recv_sem=recv_sem,
                                        device_id=device, device_id_type=MESH)


def _gather_rider(wholes, chip_axes, spans, relay):
    n = len(wholes)

    def region(dst, t, chip, half):
        first, count = spans[t]
        _, rows, cols = wholes[t].shape
        if chip_axes[t] == 2:
            size, part = cols // N_CHIPS, rows // 2
            ref = _cut(_cut(dst[t], 2, pl.multiple_of(chip * size, size), size), 1, pl.multiple_of(half * part, part), part)
        else:
            size = rows // N_CHIPS
            part = size // 2
            ref = _cut(dst[t], 1, pl.multiple_of(chip * size + half * part, part), part)
        return _cut(ref, 0, first, count)

    def copies(dst, sems, receiving):
        x, y, c, chips = _position()
        send_sem, recv_sem = sems
        out = []
        for t in range(n):
            for j, chip in enumerate(chips):
                theirs = 2 * chip[0] + chip[1]
                if relay:
                    ref = region(dst, t, theirs, 1 - c if receiving else c)
                    peer = (x, y, 1 - c)
                else:
                    ref = region(dst, t, theirs if receiving else 2 * x + y, c)
                    peer = (*chip, c)
                out.append(_remote(ref, ref, send_sem.at[t, j], recv_sem.at[t, j], peer))
        return out

    def start(ins, outs, sems):
        for cp in copies(outs, sems, False):
            cp.start()

    def finish(ins, outs, sems):
        for cp in copies(outs, sems, True):
            cp.wait_recv()
        for cp in copies(outs, sems, False):
            cp.wait_send()

    return _Rider(operands=list(wholes), out_shape=[jax.ShapeDtypeStruct(w.shape, w.dtype) for w in wholes],
                  aliases={t: t for t in range(n)}, scratch=[pltpu.SemaphoreType.DMA((n, N_CHIPS - 1))] * 2,
                  start=start, finish=finish)


def _chain(first, second):
    cut = len(first.scratch)

    def start(ins, outs, sems):
        first.start(ins, outs, sems[:cut])
        first.finish(ins, outs, sems[:cut])
        second.start(ins, outs, sems[cut:])

    def finish(ins, outs, sems):
        second.finish(ins, outs, sems[cut:])

    return first._replace(scratch=list(first.scratch) + list(second.scratch), start=start, finish=finish)


def _swap_rider(parts, core_axes):
    n = len(parts)
    halves = []
    for arr, axis in zip(parts, core_axes):
        shape = list(arr.shape)
        shape[axis] //= 2
        halves.append(jax.ShapeDtypeStruct(tuple(shape), arr.dtype))

    def copies(ins, outs, sems):
        x, y, c, _ = _position()
        out = []
        for t in range(n):
            size = halves[t].shape[core_axes[t]]
            piece = _cut(ins[t], core_axes[t], pl.multiple_of((1 - c) * size, size), size)
            out.append(_remote(piece, outs[t], sems[0].at[t], sems[1].at[t], (x, y, 1 - c)))
        return out

    def start(ins, outs, sems):
        for cp in copies(ins, outs, sems):
            cp.start()

    def finish(ins, outs, sems):
        for cp in copies(ins, outs, sems):
            cp.wait()

    return _Rider(operands=list(parts), out_shape=halves, aliases={}, scratch=[pltpu.SemaphoreType.DMA((n,))] * 2,
                  start=start, finish=finish)


def _exchange_rider(partials, chip_axes):
    n = len(partials)
    quarters = []
    for arr, axis in zip(partials, chip_axes):
        shape = list(arr.shape)
        shape[axis] //= N_CHIPS
        quarters.append(jax.ShapeDtypeStruct((N_CHIPS - 1, *shape), arr.dtype))

    def copies(ins, outs, sems):
        x, y, c, chips = _position()
        out = []
        for t in range(n):
            size = quarters[t].shape[1 + chip_axes[t]]
            for j, chip in enumerate(chips):
                piece = _cut(ins[t], chip_axes[t], pl.multiple_of((2 * chip[0] + chip[1]) * size, size), size)
                out.append(_remote(piece, outs[t].at[j], sems[0].at[t, j], sems[1].at[t, j], (*chip, c)))
        return out

    def start(ins, outs, sems):
        for cp in copies(ins, outs, sems):
            cp.start()

    def finish(ins, outs, sems):
        for cp in copies(ins, outs, sems):
            cp.wait()

    return _Rider(operands=list(partials), out_shape=quarters, aliases={}, scratch=[pltpu.SemaphoreType.DMA((n, N_CHIPS - 1))] * 2,
                  start=start, finish=finish)


def _place_block(shard, chip_axis, place, dtype, *, name):
    depth, rows, cols = shard.shape
    rows_t = _row_tile(rows, cols * 4 * 4, 16 * 1024 * 1024)
    steps = rows // rows_t
    shape = list(shard.shape)
    shape[chip_axis] *= N_CHIPS
    if chip_axis == 1:
        out_spec = pl.BlockSpec((None, rows_t, cols), lambda l, i, p: (l, p[1] * steps + i, 0))
    else:
        out_spec = pl.BlockSpec((None, rows_t, cols), lambda l, i, p: (l, i, p[1]))

    def body(place_ref, src_ref, out_ref):
        out_ref[...] = src_ref[...].astype(dtype)

    return pl.pallas_call(
        body, name=name, out_shape=jax.ShapeDtypeStruct(tuple(shape), dtype),
        grid_spec=pltpu.PrefetchScalarGridSpec(num_scalar_prefetch=1, grid=(depth, steps),
                                               in_specs=[pl.BlockSpec((None, rows_t, cols), lambda l, i, p: (l, i, 0))],
                                               out_specs=out_spec),
        compiler_params=_cparams(("arbitrary", "arbitrary")),
    )(place, shard)


def _join_rider(blocks, core_axes, first, count):
    n = len(blocks)

    def copies(outs, sems, receiving):
        x, y, c, _ = _position()
        out = []
        for t in range(n):
            size = blocks[t].shape[core_axes[t]] // 2
            half = 1 - c if receiving else c
            ref = _cut(_cut(outs[t], core_axes[t], pl.multiple_of(half * size, size), size), 0, first, count)
            out.append(_remote(ref, ref, sems[0].at[t], sems[1].at[t], (x, y, 1 - c)))
        return out

    def start(ins, outs, sems):
        for cp in copies(outs, sems, False):
            cp.start()

    def finish(ins, outs, sems):
        for cp in copies(outs, sems, True):
            cp.wait_recv()
        for cp in copies(outs, sems, False):
            cp.wait_send()

    return _Rider(operands=list(blocks), out_shape=[jax.ShapeDtypeStruct(b.shape, b.dtype) for b in blocks],
                  aliases={t: t for t in range(n)}, scratch=[pltpu.SemaphoreType.DMA((n,))] * 2, start=start, finish=finish)


def _small_rider(vec):
    n_dev = 2 * N_CHIPS

    def copies(ins, outs, sems, receiving):
        x, y, c, _ = _position()
        flip = lambda v, bit: 1 - v if bit else v
        out = []
        for mask in range(1, n_dev):
            peer = (flip(x, mask & 4), flip(y, mask & 2), flip(c, mask & 1))
            row = 4 * peer[0] + 2 * peer[1] + peer[2] if receiving else 4 * x + 2 * y + c
            out.append(_remote(ins[0], outs[0].at[row], sems[0].at[mask - 1], sems[1].at[mask - 1], peer))
        return out

    def own(ins, outs, sems):
        x, y, c, _ = _position()
        return pltpu.make_async_copy(ins[0], outs[0].at[4 * x + 2 * y + c], sems[2])

    def start(ins, outs, sems):
        own(ins, outs, sems).start()
        for cp in copies(ins, outs, sems, False):
            cp.start()

    def finish(ins, outs, sems):
        for cp in copies(ins, outs, sems, True):
            cp.wait_recv()
        for cp in copies(ins, outs, sems, False):
            cp.wait_send()
        own(ins, outs, sems).wait()

    return _Rider(operands=[vec], out_shape=[jax.ShapeDtypeStruct((n_dev, *vec.shape), vec.dtype)], aliases={},
                  scratch=[pltpu.SemaphoreType.DMA((n_dev - 1,))] * 2 + [pltpu.SemaphoreType.DMA(())], start=start, finish=finish)


def _both(first, second):
    n_in, n_out, n_sem = len(first.operands), len(first.out_shape), len(first.scratch)

    def start(ins, outs, sems):
        first.start(ins[:n_in], outs[:n_out], sems[:n_sem])
        second.start(ins[n_in:], outs[n_out:], sems[n_sem:])

    def finish(ins, outs, sems):
        first.finish(ins[:n_in], outs[:n_out], sems[:n_sem])
        second.finish(ins[n_in:], outs[n_out:], sems[n_sem:])

    aliases = dict(first.aliases)
    aliases.update({n_in + i: n_out + o for i, o in second.aliases.items()})
    return _Rider(operands=list(first.operands) + list(second.operands), out_shape=list(first.out_shape) + list(second.out_shape),
                  aliases=aliases, scratch=list(first.scratch) + list(second.scratch), start=start, finish=finish)


def _row_tile(rows, row_bytes, budget):
    tile = rows
    for cand in (512, 256, 128, 64, 32, 16, 8):
        if rows % cand == 0:
            tile = cand
            if cand * row_bytes <= budget:
                break
    return tile


def _pair_sum_layer(part, got, core_axis, place, *, name):
    rows, cols = got.shape
    rows_t = _row_tile(rows, cols * 4 * 6, 16 * 1024 * 1024)
    steps = rows // rows_t
    if core_axis == 0:
        part_spec = pl.BlockSpec((rows_t, cols), lambda i, p: (p[0] * steps + i, 0))
    else:
        part_spec = pl.BlockSpec((rows_t, cols), lambda i, p: (i, p[0]))
    own_spec = pl.BlockSpec((rows_t, cols), lambda i, p: (i, 0))

    def body(place_ref, part_ref, got_ref, out_ref):
        out_ref[...] = (part_ref[...] + got_ref[...]).astype(BF16)

    return pl.pallas_call(
        body, name=name, out_shape=jax.ShapeDtypeStruct(got.shape, BF16),
        grid_spec=pltpu.PrefetchScalarGridSpec(num_scalar_prefetch=1, grid=(steps,), in_specs=[part_spec, own_spec], out_specs=own_spec),
        compiler_params=_cparams(("arbitrary",)),
    )(place, part, got)


def _chip_sum_layer(partial, got, blocks, layer, depth, chip_axis, core_axis, place, *, name):
    _, rows, cols = got.shape
    rows_t = _row_tile(rows, cols * 4 * 10, 24 * 1024 * 1024)
    steps = rows // rows_t
    if chip_axis == 0:
        own_spec = pl.BlockSpec((rows_t, cols), lambda i, p: (p[1] * steps + i, 0))
    else:
        own_spec = pl.BlockSpec((rows_t, cols), lambda i, p: (i, p[1]))
    got_spec = pl.BlockSpec((N_CHIPS - 1, rows_t, cols), lambda i, p: (0, i, 0))
    shape = [depth, rows, cols]
    shape[1 + core_axis] *= 2
    if core_axis == 0:
        out_spec = pl.BlockSpec((None, rows_t, cols), lambda i, p: (layer, p[0] * steps + i, 0))
    else:
        out_spec = pl.BlockSpec((None, rows_t, cols), lambda i, p: (layer, i, p[0]))

    def body(place_ref, own_ref, got_ref, *rest):
        out_ref = rest[-1]
        up = lambda val: val.astype(F32)
        out_ref[...] = ((up(own_ref[...]) + up(got_ref[0])) + up(got_ref[1])) + up(got_ref[2])

    in_specs, operands, aliases = [own_spec, got_spec], [place, partial, got], {}
    if blocks is not None:
        in_specs.append(pl.BlockSpec(memory_space=pl.ANY))
        operands.append(blocks)
        aliases = {3: 0}
    return pl.pallas_call(
        body, name=name, out_shape=jax.ShapeDtypeStruct(tuple(shape), F32),
        grid_spec=pltpu.PrefetchScalarGridSpec(num_scalar_prefetch=1, grid=(steps,), in_specs=in_specs, out_specs=out_spec),
        input_output_aliases=aliases, compiler_params=_cparams(("arbitrary",)),
    )(*operands)


def _sum_devices(stack, *, name):
    def body(src_ref, out_ref):
        total = src_ref[0]
        for d in range(1, stack.shape[0]):
            total = total + src_ref[d]
        out_ref[...] = total

    return pl.pallas_call(body, name=name, out_shape=jax.ShapeDtypeStruct(stack.shape[1:], F32))(stack)


def kernel(x, w_in, b_in, conv_w, conv_b, conv_ln_g, conv_ln_b, w_att_proj, w_conv_proj, b_conv_proj, w_out, ln_g, ln_b, loss_target, m_w_in, m_b_in, m_conv_w, m_conv_b, m_conv_ln_g, m_conv_ln_b, m_w_att_proj, m_w_conv_proj, m_b_conv_proj, m_w_out, m_ln_g, m_ln_b, v_w_in, v_b_in, v_conv_w, v_conv_b, v_conv_ln_g, v_conv_ln_b, v_w_att_proj, v_w_conv_proj, v_b_conv_proj, v_w_out, v_ln_g, v_ln_b):
    depth = w_in.shape[0]
    d_model = x.shape[-1]
    half = d_model // 2
    chip = 2 * lax.axis_index("x") + lax.axis_index("y")
    place = jnp.stack([lax.axis_index("c"), chip]).astype(jnp.int32)
    vec3 = lambda v: v.reshape(depth, 1, -1)

    taps = jnp.pad(conv_w, ((0, 0), (0, CONV_PAD - CONV_WIDTH), (0, 0)))
    gathered = [("w_in", w_in, 2, BF16), ("w_att_proj", w_att_proj, 2, BF16), ("w_conv_proj", w_conv_proj, 2, BF16),
                ("w_out", w_out, 1, BF16), ("conv_w", taps, 2, F32)]
    wholes = [_place_block(arr, axis, place, dtype, name="place_" + n) for n, arr, axis, dtype in gathered]
    loss, grad_x, packed, waiting, blocks = _train_pass(x[0], loss_target[0], wholes[:4], wholes[4], b_in, vec3(conv_b),
                                                        vec3(conv_ln_g), vec3(conv_ln_b), vec3(b_conv_proj), vec3(ln_g), vec3(ln_b), place)
    loss = lax.psum(loss[0, 0], ("x", "y", "c"))

    names = ["w_in", "b_in", "conv_w", "conv_b", "conv_ln_g", "conv_ln_b", "w_att_proj", "w_conv_proj", "b_conv_proj", "w_out", "ln_g", "ln_b"]
    weights = dict(zip(names, (w_in, b_in, conv_w, conv_b, conv_ln_g, conv_ln_b, w_att_proj, w_conv_proj, b_conv_proj, w_out, ln_g, ln_b)))
    first = dict(zip(names, (m_w_in, m_b_in, m_conv_w, m_conv_b, m_conv_ln_g, m_conv_ln_b, m_w_att_proj, m_w_conv_proj, m_b_conv_proj, m_w_out, m_ln_g, m_ln_b)))
    second = dict(zip(names, (v_w_in, v_b_in, v_conv_w, v_conv_b, v_conv_ln_g, v_conv_ln_b, v_w_att_proj, v_w_conv_proj, v_b_conv_proj, v_w_out, v_ln_g, v_ln_b)))
    flat = lambda arr: arr.reshape(-1, arr.shape[-1])

    *got, gathered = _lone_call(_both(_exchange_rider(waiting, MATRIX_CHIP_AXIS), _small_rider(packed)), name="exchange_last")
    blocks = _lone_call(_join_rider(_chip_sums(0, depth, waiting, got, blocks, place), [axis + 1 for axis in MATRIX_CORE_AXIS], 0, 1),
                        name="pair_join")
    reduced = dict(zip(MATRICES, blocks))

    total = _sum_devices(gathered, name="sum_devices")
    widths = dict(b_in=b_in.shape[1], conv_b=half, conv_ln_g=half, conv_ln_b=half, b_conv_proj=d_model, ln_g=d_model, ln_b=d_model)
    offset = 0
    for n in VECTORS:
        reduced[n] = total[:, offset:offset + widths[n]]
        offset += widths[n]
    taps = total[:, offset:].reshape(depth, CONV_WIDTH, half)
    reduced["conv_w"] = lax.dynamic_slice_in_dim(taps, chip * conv_w.shape[2], conv_w.shape[2], axis=2)

    delta, new_m, new_v = {}, {}, {}
    for n in names:
        shape = weights[n].shape
        *grad, d, m, v = _adamw(flat(weights[n]), flat(reduced[n]), flat(first[n]), flat(second[n]), name="adamw_" + n,
                                echo=n in MATRICES)
        if grad:
            reduced[n] = grad[0]
        delta[n], new_m[n], new_v[n] = d.reshape(shape), m.reshape(shape), v.reshape(shape)
    return (loss, grad_x[None], *[reduced[n].reshape(weights[n].shape) for n in names], *[delta[n] for n in names],
            *[new_m[n] for n in names], *[new_v[n] for n in names])
```

```python
import functools
from typing import Callable, NamedTuple

import jax
import jax.numpy as jnp
from jax import lax
from jax.experimental import pallas as pl
from jax.experimental.pallas import tpu as pltpu

F32 = jnp.float32
BF16 = jnp.bfloat16

HEAD_DIM = 64
LANES = 128
CONV_WIDTH = 31
CONV_PAD = 32
SUBLANES = 8
LN_EPS = 1e-5
DEPTH = 4
DEEPNORM_ALPHA = (2 * DEPTH) ** 0.25
ATT_SCALE = HEAD_DIM ** -0.5
ATT_TILE = 128
ATT_DEAD = -88.0
ATT_GROUP = 2
ATT_GROUP_FWD = 4
ATT_FILL = -1e30

ADAM_LR = 0.001
ADAM_B1 = 0.9
ADAM_B2 = 0.999
ADAM_EPS = 1e-08
ADAM_WD = 0.01
ADAM_STEP = 10

VMEM_LIMIT = 56 * 1024 * 1024

N_CHIPS = 4


def _cparams(sem):
    return pltpu.CompilerParams(dimension_semantics=sem, vmem_limit_bytes=VMEM_LIMIT)


def _sigmoid(x):
    return 1.0 / (1.0 + jnp.exp(-x))


class _Rider(NamedTuple):
    operands: list
    out_shape: list
    aliases: dict
    scratch: list
    start: Callable
    finish: Callable


def _host_call(body, rider, *, name, grid, in_specs, out_specs, out_shape, scratch, operands, semantics, host_aliases=None):
    n_in, n_out = len(in_specs), len(out_specs)
    aliases = dict(host_aliases or {})
    if rider is not None:
        r_in, r_out = len(rider.operands), len(rider.out_shape)
        host_body = body

        def body(*refs):
            base = n_in + r_in
            ins, rins = refs[:n_in], refs[n_in:base]
            outs, routs = refs[base:base + n_out], refs[base + n_out:base + n_out + r_out]
            rest = refs[base + n_out + r_out:]
            split = len(rest) - len(rider.scratch)
            ids = [pl.program_id(axis) for axis in range(len(grid))]
            first = functools.reduce(jnp.logical_and, [i == 0 for i in ids])
            last = functools.reduce(jnp.logical_and, [i == g - 1 for i, g in zip(ids, grid)])

            @pl.when(first)
            def _():
                rider.start(rins, routs, rest[split:])

            host_body(*ins, *outs, *rest[:split])

            @pl.when(last)
            def _():
                rider.finish(rins, routs, rest[split:])

        hbm = pl.BlockSpec(memory_space=pl.ANY)
        in_specs = list(in_specs) + [hbm] * r_in
        out_specs = list(out_specs) + [hbm] * r_out
        out_shape = list(out_shape) + list(rider.out_shape)
        scratch = list(scratch) + list(rider.scratch)
        operands = list(operands) + list(rider.operands)
        aliases.update({n_in + i: n_out + o for i, o in rider.aliases.items()})
    res = pl.pallas_call(
        body, name=name, grid=grid, in_specs=list(in_specs), out_specs=list(out_specs), out_shape=list(out_shape),
        scratch_shapes=list(scratch), input_output_aliases=aliases, compiler_params=_cparams(semantics),
    )(*operands)
    return list(res[:n_out]), list(res[n_out:])


def _lone_call(rider, *, name):
    r_in = len(rider.operands)

    def body(*refs):
        ins, outs, sems = refs[:r_in], refs[r_in:r_in + len(rider.out_shape)], refs[r_in + len(rider.out_shape):]
        rider.start(ins, outs, sems)
        rider.finish(ins, outs, sems)

    hbm = pl.BlockSpec(memory_space=pl.ANY)
    return list(pl.pallas_call(
        body, name=name, in_specs=[hbm] * r_in, out_specs=[hbm] * len(rider.out_shape), out_shape=list(rider.out_shape),
        scratch_shapes=list(rider.scratch), input_output_aliases=dict(rider.aliases),
    )(*rider.operands))


def _fit(tile, dim):
    assert dim % LANES == 0
    tile = min(tile, dim) // LANES * LANES
    while dim % tile:
        tile -= LANES
    return tile


_DIMS = {"nn": ((1,), (0,)), "nt": ((1,), (1,)), "tn": ((0,), (0,))}


def _matmul(a, b, *, mode, name, layer=None, bias=None, add=None, colsum=False, out_dtype=F32, tm=1024, tn=512, tk=1024,
            rider=None):
    b_shape = b.shape if layer is None else b.shape[1:]
    if mode == "nn":
        (m, k), (k2, n) = a.shape, b_shape
    elif mode == "nt":
        (m, k), (n, k2) = a.shape, b_shape
    else:
        (k, m), (k2, n) = a.shape, b_shape
    assert k == k2
    tm, tn, tk = _fit(tm, m), _fit(tn, n), _fit(tk, k)
    gm, gn, nk = m // tm, n // tn, k // tk

    a_spec = pl.BlockSpec((tk, tm), lambda i, j, kk: (kk, i)) if mode == "tn" else pl.BlockSpec((tm, tk), lambda i, j, kk: (i, kk))
    if layer is None:
        b_spec = pl.BlockSpec((tn, tk), lambda i, j, kk: (j, kk)) if mode == "nt" else pl.BlockSpec((tk, tn), lambda i, j, kk: (kk, j))
    elif mode == "nt":
        b_spec = pl.BlockSpec((None, tn, tk), lambda i, j, kk: (layer, j, kk))
    else:
        b_spec = pl.BlockSpec((None, tk, tn), lambda i, j, kk: (layer, kk, j))
    in_specs, operands = [a_spec, b_spec], [a, b]
    if bias is not None:
        in_specs.append(pl.BlockSpec((1, tn), lambda i, j, kk: (0, j)))
        operands.append(bias)
    if add is not None:
        in_specs.append(pl.BlockSpec((tm, tn), lambda i, j, kk: (i, j)))
        operands.append(add)
    out_shape = [jax.ShapeDtypeStruct((m, n), out_dtype)]
    out_specs = [pl.BlockSpec((tm, tn), lambda i, j, kk: (i, j))]
    scratch = [pltpu.VMEM((tm, tn), F32)] if nk > 1 else []
    if colsum:
        assert mode == "tn"
        out_shape.append(jax.ShapeDtypeStruct((gm, 1, n), F32))
        out_specs.append(pl.BlockSpec((1, 1, tn), lambda i, j, kk: (i, 0, j)))
        if nk > 1:
            scratch.append(pltpu.VMEM((1, tn), F32))
    has_bias, has_add = bias is not None, add is not None

    def body(*refs):
        refs = list(refs)
        a_ref, b_ref = refs[0], refs[1]
        pos = 2
        bias_ref = add_ref = None
        if has_bias:
            bias_ref = refs[pos]
            pos += 1
        if has_add:
            add_ref = refs[pos]
            pos += 1
        o_ref = refs[pos]
        pos += 1
        cs_ref = None
        if colsum:
            cs_ref = refs[pos]
            pos += 1

        def finish(out, sums):
            if has_bias:
                out = out + bias_ref[...]
            if has_add:
                out = out + add_ref[...]
            o_ref[...] = out.astype(out_dtype)
            if colsum:
                cs_ref[0] = sums

        bv = b_ref[...]
        prod = lax.dot_general(a_ref[...].astype(BF16), bv.astype(BF16), (_DIMS[mode], ((), ())), preferred_element_type=F32)
        sums = jnp.sum(bv.astype(F32), axis=0, keepdims=True) if colsum else None
        if nk == 1:
            finish(prod, sums)
            return
        acc_ref = refs[pos]
        cs_acc = refs[pos + 1] if colsum else None
        kk = pl.program_id(2)

        @pl.when(kk == 0)
        def _():
            acc_ref[...] = jnp.zeros_like(acc_ref)
            if colsum:
                cs_acc[...] = jnp.zeros_like(cs_acc)

        acc_ref[...] += prod
        if colsum:
            cs_acc[...] += sums

        @pl.when(kk == nk - 1)
        def _():
            finish(acc_ref[...], cs_acc[...] if colsum else None)

    res, rode = _host_call(body, rider, name=name, grid=(gm, gn, nk), in_specs=in_specs, out_specs=out_specs, out_shape=out_shape,
                           scratch=scratch, operands=operands, semantics=("arbitrary", "arbitrary", "arbitrary"))
    out = (res[0], res[1][0]) if colsum else res[0]
    return out if rider is None else (out, rode)


def _scan_matrices():
    t = ATT_TILE
    r = lax.broadcasted_iota(jnp.int32, (t, t), 0)
    c = lax.broadcasted_iota(jnp.int32, (t, t), 1)
    ones = jnp.ones((t, t), F32)
    suffix = jnp.concatenate([(r > c).astype(F32), ones], axis=1)
    prefix = jnp.concatenate([(r < c).astype(F32), ones], axis=1)
    stack = lambda mat: jnp.concatenate([mat, mat], axis=0).astype(BF16)
    return stack(suffix), stack(prefix)


def _split_halves(val):
    hi = val.astype(BF16)
    lo = (val - hi.astype(F32)).astype(BF16)
    return jnp.concatenate([hi, lo], axis=1)


def _split_scan(val, mat_ref):
    return jnp.dot(_split_halves(val), mat_ref[...], preferred_element_type=F32)


def _pair_scores(q, k_lo, k_hi, masked):
    t = ATT_TILE
    z2 = lax.dot_general(q, jnp.concatenate([k_lo, k_hi], axis=0), (((1,), (1,)), ((), ())), preferred_element_type=F32)
    out = []
    for h in range(2):
        z = z2[:, h * t:(h + 1) * t]
        sp = jnp.log(1.0 + jnp.exp(-jnp.abs(z)))
        f = jnp.minimum(-z, 0.0) - sp
        a = f + z
        if masked:
            causal = lax.broadcasted_iota(jnp.int32, (t, t), 1) < lax.broadcasted_iota(jnp.int32, (t, t), 0)
            f = jnp.where(causal, f, 0.0)
        out.append((_split_halves(f), a))
    return out


def _any_alive(runs):
    top = functools.reduce(jnp.maximum, [run for per_head in runs for run in per_head])
    return (jnp.max(top) > ATT_DEAD).astype(jnp.int32)


def _head_copies(seq, src_ref, scale, lo_ref, hi_ref, plain_ref):
    chunk = min(256, seq)
    low = lax.broadcasted_iota(jnp.int32, (chunk, LANES), 1) < HEAD_DIM

    def step(r, carry):
        rows = pl.ds(pl.multiple_of(r * chunk, chunk), chunk)
        val = src_ref[rows, :]
        if scale != 1.0:
            val = val * scale
        if lo_ref is not None:
            lo_ref[rows, :] = jnp.where(low, val, 0.0).astype(BF16)
            hi_ref[rows, :] = jnp.where(low, 0.0, val).astype(BF16)
        if plain_ref is not None:
            plain_ref[rows, :] = val.astype(BF16)
        return carry

    lax.fori_loop(0, seq // chunk, step, 0)


def _attn_fwd(u, scan_suffix, *, seq, d_att, name, rider=None):
    t = ATT_TILE
    nq = seq // t
    pairs = d_att // LANES
    grp = ATT_GROUP_FWD
    assert nq % grp == 0

    def body(q_ref, k_ref, v_ref, um_ref, o_ref, qq, k0, k1, v0, v1, f2_s, a_s, lg_s, tot_s, run_s, acc_s):
        _head_copies(seq, q_ref, ATT_SCALE, None, None, qq)
        _head_copies(seq, k_ref, 1.0, k0, k1, None)
        _head_copies(seq, v_ref, 1.0, v0, v1, None)

        def group(gi, carry):
            qb0 = gi * grp
            qrows = [pl.ds(pl.multiple_of((qb0 + g) * t, t), t) for g in range(grp)]
            qv = [qq[qrows[g], :] for g in range(grp)]

            chains = [(h, g) for g in range(grp) for h in range(2)]

            def key_rows(g, i):
                return pl.ds(pl.multiple_of(jnp.maximum(qb0 + g - i, 0) * t, t), t)

            def stage1(i, masked):
                out = []
                for g in range(grp):
                    krows = key_rows(g, i)
                    out += _pair_scores(qv[g], k0[krows, :], k1[krows, :], masked)
                return out

            def stage2(halves, a, masked):
                scan = jnp.dot(halves, um_ref[...], preferred_element_type=F32)
                logit = a + scan[:, :t]
                if masked:
                    causal = lax.broadcasted_iota(jnp.int32, (t, t), 1) < lax.broadcasted_iota(jnp.int32, (t, t), 0)
                    logit = jnp.where(causal, logit, ATT_FILL)
                return logit, scan[:, t:]

            def put(halves_a=None, logit_total=None):
                for c in range(len(chains)):
                    if halves_a is not None:
                        f2_s[c], a_s[c] = halves_a[c]
                    if logit_total is not None:
                        lg_s[c], tot_s[c] = logit_total[c]

            first = stage1(0, True)
            put(halves_a=stage1(1, False), logit_total=[stage2(f2, a, True) for f2, a in first])
            for c in range(len(chains)):
                run_s[c] = jnp.zeros((t, t), F32)
            for g in range(grp):
                acc_s[g] = jnp.zeros((t, LANES), F32)

            def wbody(st):
                i = st[0]
                held = [(f2_s[c], a_s[c]) for c in range(len(chains))]
                logits = [lg_s[c] for c in range(len(chains))]
                totals = [tot_s[c] for c in range(len(chains))]
                runs = [run_s[c] for c in range(len(chains))]
                accs = [acc_s[g] for g in range(grp)]
                for g in range(grp):
                    krows = key_rows(g, i)
                    gone = jnp.where(qb0 + g - i >= 0, 0.0, ATT_FILL)
                    weights = []
                    for c in (2 * g, 2 * g + 1):
                        run = runs[c] + gone
                        weights.append(jnp.exp(logits[c] + run).astype(BF16))
                        runs[c] = run + totals[c]
                    accs[g] = accs[g] + jnp.dot(jnp.concatenate(weights, axis=1), jnp.concatenate([v0[krows, :], v1[krows, :]], axis=0),
                                                preferred_element_type=F32)
                ahead2 = [stage2(f2, a, False) for f2, a in held]
                ahead1 = stage1(i + 2, False)
                put(halves_a=ahead1, logit_total=ahead2)
                for c in range(len(chains)):
                    run_s[c] = runs[c]
                for g in range(grp):
                    acc_s[g] = accs[g]
                more = jnp.logical_and(i + 1 <= qb0 + grp - 1, _any_alive([runs]) > 0)
                return i + 1, more.astype(jnp.int32)

            lax.while_loop(lambda st: st[1] > 0, wbody, (jnp.int32(0), jnp.int32(1)))
            for g in range(grp):
                o_ref[qrows[g], :] = acc_s[g]
            return carry

        lax.fori_loop(0, nq // grp, group, 0)

    blk = lambda base: pl.BlockSpec((seq, LANES), lambda j, base=base: (0, base + j))
    res, rode = _host_call(
        body, rider, name=name, grid=(pairs,),
        in_specs=[blk(0), blk(pairs), blk(2 * pairs), pl.BlockSpec((2 * t, 2 * t), lambda j: (0, 0))],
        out_specs=[pl.BlockSpec((seq, LANES), lambda j: (0, j))],
        out_shape=[jax.ShapeDtypeStruct((seq, d_att), F32)],
        scratch=[pltpu.VMEM((seq, LANES), BF16)] * 5 + [pltpu.VMEM((2 * grp, t, 2 * t), BF16)]
        + [pltpu.VMEM((2 * grp, t, t), F32)] * 4 + [pltpu.VMEM((grp, t, LANES), F32)],
        operands=[u, u, u, scan_suffix], semantics=("arbitrary",))
    return res[0] if rider is None else (res[0], rode)


def _columns_copy(stage_ref, du_ref, rows, col, sem):
    width = stage_ref.shape[-1]
    cols = pl.ds(pl.multiple_of(col, LANES), width)
    return pltpu.make_async_copy(stage_ref, du_ref.at[slice(None) if rows is None else rows, cols], sem)


def _attn_bwd(u, d_att_out, du, scan_suffix, scan_prefix, *, seq, d_att, name, rider=None):
    t = ATT_TILE
    nq = seq // t
    pairs = d_att // LANES
    grp = ATT_GROUP
    assert nq % grp == 0

    def body(q_ref, k_ref, v_ref, do_ref, um_ref, pm_ref, du_old, du_ref,
             qq, q0, q1, k0, k1, v0, v1, dd, do0, do1, dq_acc, dk_acc, dv_acc, g_st, b_st,
             f2_s, a_s, lg_s, tot_s, dw_s, run_s, p_s, pt_s, grun_s, dq_s, dq_o, dk_o, dv_o, out_sem):
        _head_copies(seq, q_ref, ATT_SCALE, q0, q1, qq)
        _head_copies(seq, k_ref, 1.0, k0, k1, None)
        _head_copies(seq, v_ref, 1.0, v0, v1, None)
        _head_copies(seq, do_ref, 1.0, do0, do1, dd)
        dk_acc[...] = jnp.zeros_like(dk_acc)
        dv_acc[...] = jnp.zeros_like(dv_acc)

        tn = (((0,), (0,)), ((), ()))
        nt = (((1,), (1,)), ((), ()))

        def stacked(lo_ref, hi_ref, rows):
            return jnp.concatenate([lo_ref[rows, :], hi_ref[rows, :]], axis=0)

        def group(gi, carry):
            qb0 = gi * grp
            qrows = [pl.ds(pl.multiple_of((qb0 + g) * t, t), t) for g in range(grp)]
            qv = [qq[qrows[g], :] for g in range(grp)]
            dov = [dd[qrows[g], :] for g in range(grp)]
            q_heads = [stacked(q0, q1, qrows[g]) for g in range(grp)]
            do_heads = [stacked(do0, do1, qrows[g]) for g in range(grp)]

            def key_rows(g, i):
                return pl.ds(pl.multiple_of(jnp.maximum(qb0 + g - i, 0) * t, t), t)

            chains = [(h, g) for g in range(grp) for h in range(2)]
            every = range(len(chains))

            def stage1(i, masked):
                out = []
                for g in range(grp):
                    krows = key_rows(g, i)
                    out += _pair_scores(qv[g], k0[krows, :], k1[krows, :], masked)
                return out

            def stage2(i, held, masked):
                out = []
                for g in range(grp):
                    dw2 = lax.dot_general(dov[g], stacked(v0, v1, key_rows(g, i)), nt, preferred_element_type=F32)
                    for h in range(2):
                        c = 2 * g + h
                        halves, a = held[c]
                        scan = jnp.dot(halves, um_ref[...], preferred_element_type=F32)
                        logit = a + scan[:, :t]
                        if masked:
                            causal = lax.broadcasted_iota(jnp.int32, (t, t), 1) < lax.broadcasted_iota(jnp.int32, (t, t), 0)
                            logit = jnp.where(causal, logit, ATT_FILL)
                        b_st[c, i] = jnp.exp(a)
                        out.append((logit, scan[:, t:], dw2[:, h * t:(h + 1) * t]))
                return out

            def put(held=None, ready=None):
                for c in every:
                    if held is not None:
                        f2_s[c], a_s[c] = held[c]
                    if ready is not None:
                        lg_s[c], tot_s[c], dw_s[c] = ready[c]

            put(held=stage1(1, False), ready=stage2(0, stage1(0, True), True))
            for c in every:
                run_s[c] = jnp.zeros((t, t), F32)

            def wbody(st):
                i = st[0]
                held = [(f2_s[c], a_s[c]) for c in every]
                ready = [(lg_s[c], tot_s[c], dw_s[c]) for c in every]
                runs = [run_s[c] for c in every]
                for g in range(grp):
                    gone = jnp.where(qb0 + g - i >= 0, 0.0, ATT_FILL)
                    weights = []
                    for c in (2 * g, 2 * g + 1):
                        logit, total, dw = ready[c]
                        run = runs[c] + gone
                        w = jnp.exp(logit + run)
                        g_st[c, i] = w * dw
                        weights.append(w.astype(BF16))
                        runs[c] = run + total
                    dv_acc[key_rows(g, i), :] += lax.dot_general(jnp.concatenate(weights, axis=0), do_heads[g], tn,
                                                                 preferred_element_type=F32)
                ahead2 = stage2(i + 1, held, False)
                ahead1 = stage1(i + 2, False)
                put(held=ahead1, ready=ahead2)
                for c in every:
                    run_s[c] = runs[c]
                more = jnp.logical_and(i + 1 <= qb0 + grp - 1, _any_alive([runs]) > 0)
                return i + 1, more.astype(jnp.int32)

            steps = lax.while_loop(lambda st: st[1] > 0, wbody, (jnp.int32(0), jnp.int32(1)))[0]

            def prefix(i):
                out = []
                for c in every:
                    scan = _split_scan(g_st[c, i], pm_ref)
                    out.append((scan[:, :t], scan[:, t:]))
                return out

            def back(i, masked):
                sums = [(p_s[c], pt_s[c]) for c in every]
                gruns = [grun_s[c] for c in every]
                dqs = [dq_s[g] for g in range(grp)]
                for g in range(grp):
                    krows = key_rows(g, i)
                    dzs = []
                    for c in (2 * g, 2 * g + 1):
                        gt = g_st[c, i]
                        dz = gt - b_st[c, i] * (gt + gruns[c] + sums[c][0])
                        if masked:
                            causal = lax.broadcasted_iota(jnp.int32, (t, t), 1) < lax.broadcasted_iota(jnp.int32, (t, t), 0)
                            dz = jnp.where(causal, dz, 0.0)
                        dzs.append(dz.astype(BF16))
                        gruns[c] = gruns[c] + sums[c][1]
                    dqs[g] = dqs[g] + jnp.dot(jnp.concatenate(dzs, axis=1), stacked(k0, k1, krows), preferred_element_type=F32)
                    dk_acc[krows, :] += lax.dot_general(jnp.concatenate(dzs, axis=0), q_heads[g], tn, preferred_element_type=F32)
                return gruns, dqs

            def keep(sums=None, gruns=None, dqs=None):
                for c in every:
                    if sums is not None:
                        p_s[c], pt_s[c] = sums[c]
                    if gruns is not None:
                        grun_s[c] = gruns[c]
                if dqs is not None:
                    for g in range(grp):
                        dq_s[g] = dqs[g]

            keep(sums=prefix(steps - 1), gruns=[jnp.zeros((t, t), F32)] * len(chains), dqs=[jnp.zeros((t, LANES), F32)] * grp)

            def bbody(j, carry2):
                i = steps - 1 - j
                gruns, dqs = back(i, False)
                keep(sums=prefix(i - 1), gruns=gruns, dqs=dqs)
                return carry2

            lax.fori_loop(0, steps - 1, bbody, 0)
            _, dqs = back(0, True)
            for g in range(grp):
                dq_acc[qrows[g], :] = dqs[g]
            return carry

        lax.fori_loop(0, nq // grp, group, 0)

        pair = pl.program_id(0)
        stages = (dq_o, dk_o, dv_o)

        def copies(j):
            return [_columns_copy(stage, du_ref, None, k * d_att + j * LANES, out_sem.at[k]) for k, stage in enumerate(stages)]

        @pl.when(pair > 0)
        def _():
            for cp in copies(pair - 1):
                cp.wait()

        chunk = min(256, seq)

        def emit(r, carry):
            rows = pl.ds(pl.multiple_of(r * chunk, chunk), chunk)
            dq_o[rows, :] = (dq_acc[rows, :] * ATT_SCALE).astype(BF16)
            dk_o[rows, :] = dk_acc[rows, :].astype(BF16)
            dv_o[rows, :] = dv_acc[rows, :].astype(BF16)
            return carry

        lax.fori_loop(0, seq // chunk, emit, 0)
        for cp in copies(pair):
            cp.start()

        @pl.when(pair == pairs - 1)
        def _():
            for cp in copies(pair):
                cp.wait()

    blk = lambda base: pl.BlockSpec((seq, LANES), lambda j, base=base: (0, base + j), pipeline_mode=pl.Buffered(1))
    mat = pl.BlockSpec((2 * t, 2 * t), lambda j: (0, 0))
    hbm = pl.BlockSpec(memory_space=pl.ANY)
    res, rode = _host_call(
        body, rider, name=name, grid=(pairs,),
        in_specs=[blk(0), blk(pairs), blk(2 * pairs), blk(0), mat, mat, hbm],
        out_specs=[hbm], out_shape=[jax.ShapeDtypeStruct(du.shape, du.dtype)], host_aliases={6: 0},
        scratch=[pltpu.VMEM((seq, LANES), BF16)] * 10 + [pltpu.VMEM((seq, LANES), F32)] * 3
        + [pltpu.VMEM((2 * grp, nq + 1, t, t), F32)] * 2 + [pltpu.VMEM((2 * grp, t, 2 * t), BF16)]
        + [pltpu.VMEM((2 * grp, t, t), F32)] * 8 + [pltpu.VMEM((grp, t, LANES), F32)]
        + [pltpu.VMEM((seq, LANES), BF16)] * 3 + [pltpu.SemaphoreType.DMA((3,))],
        operands=[u, u, u, d_att_out, scan_suffix, scan_prefix, du], semantics=("arbitrary",))
    return res[0] if rider is None else (res[0], rode)


CONV_ROWS = 256


def _shifted(window, residue, rows):
    total = rows + CONV_PAD
    return window if residue == 0 else pltpu.roll(window, total - residue, 0)


def _glu_to_pad(seq, a_ref, b_ref, pad_ref):
    chunk = min(CONV_ROWS, seq)
    pad_ref[pl.ds(0, CONV_PAD), :] = jnp.zeros((CONV_PAD, LANES), F32)

    def step(r, carry):
        rows = pl.ds(pl.multiple_of(r * chunk, chunk), chunk)
        pad_ref[pl.ds(pl.multiple_of(r * chunk + CONV_PAD, SUBLANES), chunk), :] = a_ref[rows, :] * _sigmoid(b_ref[rows, :])
        return carry

    lax.fori_loop(0, seq // chunk, step, 0)


def _conv_fwd(u, conv_w, conv_b, *, layer, seq, d_conv, col_a, col_b, name, rider=None):
    blocks = d_conv // LANES
    rows_t = min(CONV_ROWS, seq)
    shift0 = CONV_PAD - (CONV_WIDTH - 1)

    def body(a_ref, b_ref, w_ref, bias_ref, o_ref, pad_ref):
        _glu_to_pad(seq, a_ref, b_ref, pad_ref)

        def step(r, carry):
            base = pl.multiple_of(r * rows_t, rows_t)
            window = pad_ref[pl.ds(base, rows_t + CONV_PAD), :]
            acc = jnp.zeros((rows_t, LANES), F32) + bias_ref[...]
            for residue in range(SUBLANES):
                moved = _shifted(window, residue, rows_t)
                for tap in range(CONV_WIDTH):
                    if (shift0 + tap) % SUBLANES == residue:
                        lo = (shift0 + tap) - residue
                        acc = acc + w_ref[tap:tap + 1, :] * moved[lo:lo + rows_t, :]
            o_ref[pl.ds(base, rows_t), :] = acc
            return carry

        lax.fori_loop(0, seq // rows_t, step, 0)

    res, rode = _host_call(
        body, rider, name=name, grid=(blocks,),
        in_specs=[pl.BlockSpec((seq, LANES), lambda j: (0, col_a + j)), pl.BlockSpec((seq, LANES), lambda j: (0, col_b + j)),
                  pl.BlockSpec((None, CONV_PAD, LANES), lambda j: (layer, 0, j)),
                  pl.BlockSpec((None, 1, LANES), lambda j: (layer, 0, j))],
        out_specs=[pl.BlockSpec((seq, LANES), lambda j: (0, j))],
        out_shape=[jax.ShapeDtypeStruct((seq, d_conv), F32)],
        scratch=[pltpu.VMEM((seq + CONV_PAD, LANES), F32)],
        operands=[u, u, conv_w, conv_b], semantics=("arbitrary",))
    return res[0] if rider is None else (res[0], rode)


def _conv_bwd(u, dc1, du, conv_w, *, layer, seq, d_conv, col_a, col_b, name):
    blocks = d_conv // LANES
    rows_t = min(CONV_ROWS, seq)
    shift0 = CONV_PAD - (CONV_WIDTH - 1)

    def body(a_ref, b_ref, d_ref, w_ref, du_old, du_ref, dw_ref, pad_ref, dpad_ref, dw_acc, da_ref, db_ref, out_sem):
        block = pl.program_id(0)

        def copies(j):
            return [_columns_copy(stage, du_ref, None, (base + j) * LANES, out_sem.at[k])
                    for k, (stage, base) in enumerate(((da_ref, col_a), (db_ref, col_b)))]

        _glu_to_pad(seq, a_ref, b_ref, pad_ref)
        dpad_ref[pl.ds(seq, CONV_PAD), :] = jnp.zeros((CONV_PAD, LANES), F32)

        def fill(r, carry):
            rows = pl.ds(pl.multiple_of(r * rows_t, rows_t), rows_t)
            dpad_ref[rows, :] = d_ref[rows, :]
            return carry

        lax.fori_loop(0, seq // rows_t, fill, 0)
        dw_acc[...] = jnp.zeros_like(dw_acc)

        @pl.when(block > 0)
        def _():
            for cp in copies(block - 1):
                cp.wait()

        def step(r, carry):
            base = pl.multiple_of(r * rows_t, rows_t)
            rows = pl.ds(base, rows_t)
            window = dpad_ref[pl.ds(base, rows_t + CONV_PAD), :]
            acc = jnp.zeros((rows_t, LANES), F32)
            for residue in range(SUBLANES):
                moved = _shifted(window, residue, rows_t)
                for tap in range(CONV_WIDTH):
                    off = CONV_WIDTH - 1 - tap
                    if off % SUBLANES == residue:
                        lo = off - residue
                        acc = acc + w_ref[tap:tap + 1, :] * moved[lo:lo + rows_t, :]
            sig = _sigmoid(b_ref[rows, :])
            a = a_ref[rows, :]
            da_ref[rows, :] = (acc * sig).astype(BF16)
            db_ref[rows, :] = (acc * a * sig * (1.0 - sig)).astype(BF16)
            dcur = d_ref[rows, :]
            cwin = pad_ref[pl.ds(base, rows_t + CONV_PAD), :]
            for residue in range(SUBLANES):
                moved = _shifted(cwin, residue, rows_t)
                for tap in range(CONV_WIDTH):
                    if (shift0 + tap) % SUBLANES == residue:
                        lo = (shift0 + tap) - residue
                        prod = dcur * moved[lo:lo + rows_t, :]
                        dw_acc[tap] += jnp.sum(prod.reshape(rows_t // SUBLANES, SUBLANES, LANES), axis=0)
            return carry

        lax.fori_loop(0, seq // rows_t, step, 0)
        dw_ref[...] = jnp.sum(dw_acc[...], axis=1)
        for cp in copies(block):
            cp.start()

        @pl.when(block == blocks - 1)
        def _():
            for cp in copies(block):
                cp.wait()

    col = lambda base: pl.BlockSpec((seq, LANES), lambda j, base=base: (0, base + j))
    own = pl.BlockSpec((seq, LANES), lambda j: (0, j))
    hbm = pl.BlockSpec(memory_space=pl.ANY)
    return pl.pallas_call(
        body, name=name, grid=(blocks,),
        in_specs=[col(col_a), col(col_b), own, pl.BlockSpec((None, CONV_PAD, LANES), lambda j: (layer, 0, j)), hbm],
        out_specs=[hbm, pl.BlockSpec((CONV_PAD, LANES), lambda j: (0, j))],
        out_shape=[jax.ShapeDtypeStruct(du.shape, du.dtype), jax.ShapeDtypeStruct((CONV_PAD, d_conv), F32)],
        scratch_shapes=[pltpu.VMEM((seq + CONV_PAD, LANES), F32), pltpu.VMEM((seq + CONV_PAD, LANES), F32),
                        pltpu.VMEM((CONV_PAD, SUBLANES, LANES), F32), pltpu.VMEM((seq, LANES), BF16), pltpu.VMEM((seq, LANES), BF16),
                        pltpu.SemaphoreType.DMA((2,))],
        input_output_aliases={4: 0}, compiler_params=_cparams(("arbitrary",)),
    )(u, u, dc1, conv_w, du)


MIX_ROWS = 256


def _layer_norm_stats(val):
    mu = jnp.mean(val, axis=-1, keepdims=True)
    cen = val - mu
    var = jnp.mean(cen * cen, axis=-1, keepdims=True)
    rstd = lax.rsqrt(var + LN_EPS)
    return cen * rstd, rstd


def _layer_norm_bwd(dy, xhat, rstd, gain):
    dxhat = dy * gain
    m1 = jnp.mean(dxhat, axis=-1, keepdims=True)
    m2 = jnp.mean(dxhat * xhat, axis=-1, keepdims=True)
    dx = rstd * (dxhat - m1 - xhat * m2)
    return dx, jnp.sum(dy * xhat, axis=0, keepdims=True), jnp.sum(dy, axis=0, keepdims=True)


def _mix_forward(zatt, att, c1, zconv, gatt, gconv, x, w_att, w_conv, w_out, cln_g, cln_b, b_proj):
    s_zatt = _sigmoid(zatt)
    a_in = att * (zatt * s_zatt)
    chat, c_rstd = _layer_norm_stats(c1)
    c2 = chat * cln_g + cln_b
    s_c2 = _sigmoid(c2)
    c3 = c2 * s_c2
    s_zconv = _sigmoid(zconv)
    c_in = c3 * (zconv * s_zconv)
    a_in_b, c_in_b = a_in.astype(BF16), c_in.astype(BF16)
    ab = jnp.dot(a_in_b, w_att, preferred_element_type=F32)
    cb = jnp.dot(c_in_b, w_conv, preferred_element_type=F32) + b_proj
    s_gatt, s_gconv = _sigmoid(gatt), _sigmoid(gconv)
    merged_b = (s_gatt * ab + s_gconv * cb).astype(BF16)
    y = jnp.dot(merged_b, w_out, preferred_element_type=F32)
    h = DEEPNORM_ALPHA * x + y
    return dict(s_zatt=s_zatt, a_in_b=a_in_b, chat=chat, c_rstd=c_rstd, c2=c2, s_c2=s_c2, c3=c3, s_zconv=s_zconv,
                c_in_b=c_in_b, ab=ab, cb=cb, s_gatt=s_gatt, s_gconv=s_gconv, merged_b=merged_b, h=h)


def _u_blocks(rows_t, width, half):
    return [pl.BlockSpec((rows_t, half), lambda i, c=c: (i, c)) for c in (3, 6, 7, 8, 9, 10)]


def _of_layer(arr, layer):
    return pl.BlockSpec((None,) + arr.shape[1:], lambda i: (layer, 0, 0))


def _mix_fwd(u, att, c1, x, w_att, w_conv, w_out, cln_g, cln_b, b_proj, ln_g, ln_b, *, layer, seq, d_model, name):
    half = d_model // 2
    rows_t = min(MIX_ROWS, seq)

    def body(zatt_ref, zconv_ref, ga0, ga1, gc0, gc1, att_ref, c1_ref, x_ref, wa_ref, wc_ref, wo_ref,
             cg_ref, cb_ref, bp_ref, lg_ref, lb_ref, o_ref, ob_ref):
        gatt = jnp.concatenate([ga0[...], ga1[...]], axis=1)
        gconv = jnp.concatenate([gc0[...], gc1[...]], axis=1)
        mid = _mix_forward(zatt_ref[...], att_ref[...], c1_ref[...], zconv_ref[...], gatt, gconv, x_ref[...],
                           wa_ref[...], wc_ref[...], wo_ref[...], cg_ref[...], cb_ref[...], bp_ref[...])
        xhat, _ = _layer_norm_stats(mid["h"])
        out = xhat * lg_ref[...] + lb_ref[...]
        o_ref[...] = out
        ob_ref[...] = out.astype(BF16)

    row = lambda width: pl.BlockSpec((rows_t, width), lambda i: (i, 0))
    full = lambda arr: _of_layer(arr, layer)
    out = pl.BlockSpec((rows_t, d_model), lambda i: (i, 0))
    return pl.pallas_call(
        body, name=name, grid=(seq // rows_t,),
        in_specs=_u_blocks(rows_t, d_model, half) + [row(half), row(half), row(d_model), full(w_att), full(w_conv), full(w_out),
                                                     full(cln_g), full(cln_b), full(b_proj), full(ln_g), full(ln_b)],
        out_specs=[out, out],
        out_shape=[jax.ShapeDtypeStruct((seq, d_model), F32), jax.ShapeDtypeStruct((seq, d_model), BF16)],
        compiler_params=_cparams(("parallel",)),
    )(u, u, u, u, u, u, att, c1, x, w_att, w_conv, w_out, cln_g, cln_b, b_proj, ln_g, ln_b)


def _mix_bwd(u, att, c1, x, dxn, w_att, w_conv, w_out, cln_g, cln_b, b_proj, ln_g, *, layer, seq, d_model, name):
    half = d_model // 2
    rows_t = min(MIX_ROWS, seq)
    steps = seq // rows_t
    nt = ((1,), (1,))
    tn = ((0,), (0,))

    def body(zatt_ref, zconv_ref, ga0, ga1, gc0, gc1, att_ref, c1_ref, x_ref, dxn_ref, wa_ref, wc_ref, wo_ref,
             cg_ref, cb_ref, bp_ref, lg_ref,
             du_ref, datt_ref, dc1_ref, dxres_ref, dwa_ref, dwc_ref, dwo_ref,
             dcg_ref, dcb_ref, dcbias_ref, dbp_ref, dlg_ref, dlb_ref, zatt_stage, tail_stage, out_sem):
        sums = (dwa_ref, dwc_ref, dwo_ref, dcg_ref, dcb_ref, dcbias_ref, dbp_ref, dlg_ref, dlb_ref)
        tile = pl.program_id(0)
        slot = tile % 2
        dzatt_ref, tail_ref = zatt_stage.at[slot], tail_stage.at[slot]
        dzconv_ref, dgatt_ref, dgconv_ref = tail_ref.at[:, :half], tail_ref.at[:, half:3 * half], tail_ref.at[:, 3 * half:]

        def copies(i):
            rows = pl.ds(pl.multiple_of(i * rows_t, rows_t), rows_t)
            return [_columns_copy(zatt_stage.at[i % 2], du_ref, rows, 3 * half, out_sem.at[i % 2, 0]),
                    _columns_copy(tail_stage.at[i % 2], du_ref, rows, 6 * half, out_sem.at[i % 2, 1])]

        @pl.when(tile == 0)
        def _():
            for ref in sums:
                ref[...] = jnp.zeros_like(ref)

        zatt, zconv, att = zatt_ref[...], zconv_ref[...], att_ref[...]
        gatt = jnp.concatenate([ga0[...], ga1[...]], axis=1)
        gconv = jnp.concatenate([gc0[...], gc1[...]], axis=1)
        wa, wc, wo = wa_ref[...], wc_ref[...], wo_ref[...]
        mid = _mix_forward(zatt, att, c1_ref[...], zconv, gatt, gconv, x_ref[...], wa, wc, wo,
                           cg_ref[...], cb_ref[...], bp_ref[...])
        xhat, rstd = _layer_norm_stats(mid["h"])
        dh, dlg, dlb = _layer_norm_bwd(dxn_ref[...], xhat, rstd, lg_ref[...])
        dlg_ref[...] += dlg
        dlb_ref[...] += dlb
        dxres_ref[...] = DEEPNORM_ALPHA * dh
        dy = dh.astype(BF16)
        dwo_ref[...] += lax.dot_general(mid["merged_b"], dy, (tn, ((), ())), preferred_element_type=F32)
        dmerged = lax.dot_general(dy, wo, (nt, ((), ())), preferred_element_type=F32)
        s_ga, s_gc, ab, cb = mid["s_gatt"], mid["s_gconv"], mid["ab"], mid["cb"]
        dgatt_ref[...] = (dmerged * ab * s_ga * (1.0 - s_ga)).astype(BF16)
        dgconv_ref[...] = (dmerged * cb * s_gc * (1.0 - s_gc)).astype(BF16)
        dab = dmerged * s_ga
        dcb = dmerged * s_gc
        dbp_ref[...] += jnp.sum(dcb, axis=0, keepdims=True)
        dab_b, dcb_b = dab.astype(BF16), dcb.astype(BF16)
        dwa_ref[...] += lax.dot_general(mid["a_in_b"], dab_b, (tn, ((), ())), preferred_element_type=F32)
        da_in = lax.dot_general(dab_b, wa, (nt, ((), ())), preferred_element_type=F32)
        s_za = mid["s_zatt"]
        datt_ref[...] = da_in * (zatt * s_za)
        dzatt_ref[...] = (da_in * att * (s_za * (1.0 + zatt * (1.0 - s_za)))).astype(BF16)
        dwc_ref[...] += lax.dot_general(mid["c_in_b"], dcb_b, (tn, ((), ())), preferred_element_type=F32)
        dc_in = lax.dot_general(dcb_b, wc, (nt, ((), ())), preferred_element_type=F32)
        s_zc, c2, s_c2 = mid["s_zconv"], mid["c2"], mid["s_c2"]
        dzconv_ref[...] = (dc_in * mid["c3"] * (s_zc * (1.0 + zconv * (1.0 - s_zc)))).astype(BF16)
        dc3 = dc_in * (zconv * s_zc)
        dc2 = dc3 * (s_c2 * (1.0 + c2 * (1.0 - s_c2)))
        dc1, dcg, dcbeta = _layer_norm_bwd(dc2, mid["chat"], mid["c_rstd"], cg_ref[...])
        dcg_ref[...] += dcg
        dcb_ref[...] += dcbeta
        dcbias_ref[...] += jnp.sum(dc1, axis=0, keepdims=True)
        dc1_ref[...] = dc1
        for cp in copies(tile):
            cp.start()

        @pl.when(tile > 0)
        def _():
            for cp in copies(tile - 1):
                cp.wait()

        @pl.when(tile == steps - 1)
        def _():
            for cp in copies(tile):
                cp.wait()

    row = lambda width: pl.BlockSpec((rows_t, width), lambda i: (i, 0))
    full = lambda arr: _of_layer(arr, layer)
    whole = lambda r, c: pl.BlockSpec((r, c), lambda i: (0, 0))
    sds = jax.ShapeDtypeStruct
    out_specs = [pl.BlockSpec(memory_space=pl.ANY), row(half), row(half), row(d_model),
                 whole(half, d_model), whole(half, d_model), whole(d_model, d_model),
                 whole(1, half), whole(1, half), whole(1, half), whole(1, d_model), whole(1, d_model), whole(1, d_model)]
    out_shape = [sds((seq, u.shape[1]), BF16), sds((seq, half), F32), sds((seq, half), F32), sds((seq, d_model), F32),
                 sds((half, d_model), F32), sds((half, d_model), F32), sds((d_model, d_model), F32),
                 sds((1, half), F32), sds((1, half), F32), sds((1, half), F32),
                 sds((1, d_model), F32), sds((1, d_model), F32), sds((1, d_model), F32)]
    return pl.pallas_call(
        body, name=name, grid=(steps,),
        in_specs=_u_blocks(rows_t, d_model, half) + [row(half), row(half), row(d_model), row(d_model), full(w_att), full(w_conv),
                                                     full(w_out), full(cln_g), full(cln_b), full(b_proj), full(ln_g)],
        out_specs=out_specs, out_shape=out_shape,
        scratch_shapes=[pltpu.VMEM((2, rows_t, half), BF16), pltpu.VMEM((2, rows_t, 5 * half), BF16), pltpu.SemaphoreType.DMA((2, 2))],
        compiler_params=_cparams(("arbitrary",)),
    )(u, u, u, u, u, u, att, c1, x, dxn, w_att, w_conv, w_out, cln_g, cln_b, b_proj, ln_g)


def _loss_head(y, target, *, seq, d_model, name):
    rows_t = min(512, seq)

    def body(y_ref, t_ref, dy_ref, loss_ref):
        @pl.when(pl.program_id(0) == 0)
        def _():
            loss_ref[...] = jnp.zeros_like(loss_ref)

        err = y_ref[...] - t_ref[...]
        dy_ref[...] = err * (1.0 / d_model)
        per_token = jnp.sum(err * err, axis=-1, keepdims=True) * (1.0 / d_model)
        loss_ref[...] += 0.5 * jnp.sum(per_token, axis=0, keepdims=True)

    row = pl.BlockSpec((rows_t, d_model), lambda i: (i, 0))
    return pl.pallas_call(
        body, name=name, grid=(seq // rows_t,), in_specs=[row, row],
        out_specs=[row, pl.BlockSpec((1, 1), lambda i: (0, 0))],
        out_shape=[jax.ShapeDtypeStruct((seq, d_model), F32), jax.ShapeDtypeStruct((1, 1), F32)],
        compiler_params=_cparams(("arbitrary",)),
    )(y, target)


def _adamw(w, g, m, v, *, name, echo=False):
    rows, cols = w.shape
    rows_t = rows
    for cand in (512, 256, 128, 64, 32, 16, 8):
        if rows % cand == 0 and cand * cols * 4 <= 2 * 1024 * 1024:
            rows_t = cand
            break
    n_out = 4 if echo else 3

    def body(w_ref, g_ref, m_ref, v_ref, *outs):
        d_ref, nm_ref, nv_ref = outs[-3:]
        grad = g_ref[...]
        if echo:
            outs[0][...] = grad
        new_m = ADAM_B1 * m_ref[...] + (1.0 - ADAM_B1) * grad
        new_v = ADAM_B2 * v_ref[...] + (1.0 - ADAM_B2) * (grad * grad)
        m_hat = new_m / (1.0 - ADAM_B1 ** ADAM_STEP)
        v_hat = new_v / (1.0 - ADAM_B2 ** ADAM_STEP)
        d_ref[...] = -ADAM_LR * (m_hat / (jnp.sqrt(v_hat) + ADAM_EPS) + ADAM_WD * w_ref[...])
        nm_ref[...] = new_m
        nv_ref[...] = new_v

    blk = pl.BlockSpec((rows_t, cols), lambda i: (i, 0))
    out = jax.ShapeDtypeStruct((rows, cols), F32)
    return pl.pallas_call(
        body, name=name, grid=(rows // rows_t,), in_specs=[blk] * 4, out_specs=[blk] * n_out, out_shape=[out] * n_out,
        compiler_params=_cparams(("parallel",)),
    )(w, g, m, v)


MATRICES = ("w_in", "w_att_proj", "w_conv_proj", "w_out")
VECTORS = ("b_in", "conv_b", "conv_ln_g", "conv_ln_b", "b_conv_proj", "ln_g", "ln_b")
MATRIX_CHIP_AXIS = (1, 1, 1, 0)
MATRIX_CORE_AXIS = (0, 0, 0, 1)


def _chip_sums(layer, depth, partials, got, blocks, place):
    return [_chip_sum_layer(partials[t], got[t], blocks[t], layer, depth, MATRIX_CHIP_AXIS[t], MATRIX_CORE_AXIS[t], place,
                            name="chip_sum_" + MATRICES[t]) for t in range(len(MATRICES))]


def _train_pass(x, target, mats, taps, b_in, conv_b, cln_g, cln_b, b_proj, ln_g, ln_b, place):
    seq, d_model = x.shape
    half = d_model // 2
    depth = b_in.shape[0]
    scan_suffix, scan_prefix = _scan_matrices()
    cols = half // LANES
    dims = dict(seq=seq, d_model=d_model)
    conv_dims = dict(seq=seq, d_conv=half, col_a=4 * cols, col_b=5 * cols)
    axes3 = [axis + 1 for axis in MATRIX_CHIP_AXIS]
    n_mat = len(mats)

    def first_layers(relay):
        return _gather_rider([mats[0], taps], [axes3[0], 2], [(0, 1), (0, depth)], relay)

    mats = list(mats)
    mats[0], taps = _lone_call(_chain(first_layers(False), first_layers(True)), name="gather_first")

    xs, xbs, us, atts, c1s = [x], [x.astype(BF16)], [], [], []
    for l in range(depth):
        u = _matmul(xbs[l], mats[0], layer=l, mode="nn", bias=b_in[l].reshape(1, -1), name="in_proj", tm=256, tn=b_in.shape[1],
                    tk=d_model)
        nxt = [(l + 1, 1)] + [(l + 1, 1) if l else (0, 2)] * (n_mat - 1)
        if l + 1 < depth:
            att, mats = _attn_fwd(u, scan_suffix, seq=seq, d_att=half, name="attn_fwd", rider=_gather_rider(mats, axes3, nxt, False))
            c1, mats = _conv_fwd(u, taps, conv_b, layer=l, name="conv_fwd", rider=_gather_rider(mats, axes3, nxt, True), **conv_dims)
        else:
            att = _attn_fwd(u, scan_suffix, seq=seq, d_att=half, name="attn_fwd_last")
            c1 = _conv_fwd(u, taps, conv_b, layer=l, name="conv_fwd_last", **conv_dims)
        xn, xnb = _mix_fwd(u, att, c1, xs[l], mats[1], mats[2], mats[3], cln_g, cln_b, b_proj, ln_g, ln_b, layer=l, name="mix_fwd", **dims)
        us.append(u)
        atts.append(att)
        c1s.append(c1)
        xs.append(xn)
        xbs.append(xnb)
    w_in, w_att, w_conv, w_out = mats

    dx, loss = _loss_head(xs[depth], target, name="loss_head", **dims)
    grads = [None] * depth
    blocks = [None] * n_mat
    waiting = None
    for l in reversed(range(depth)):
        u = us[l]
        (du, datt, dc1, dxres, dwa, dwc, dwo, dcg, dcb, dcbias, dbp, dlg, dlb) = _mix_bwd(
            u, atts[l], c1s[l], xs[l], dx, w_att, w_conv, w_out, cln_g, cln_b, b_proj, ln_g, layer=l, name="mix_bwd", **dims)
        if waiting is None:
            du = _attn_bwd(u, datt, du, scan_suffix, scan_prefix, seq=seq, d_att=half, name="attn_bwd_first")
        else:
            du, got = _attn_bwd(u, datt, du, scan_suffix, scan_prefix, seq=seq, d_att=half, name="attn_bwd",
                                rider=_exchange_rider(waiting[1], MATRIX_CHIP_AXIS))
            blocks = _chip_sums(waiting[0], depth, waiting[1], got, blocks, place)
        du, dconvw = _conv_bwd(u, dc1, du, taps, layer=l, name="conv_bwd", **conv_dims)
        dwin, dbin = _matmul(xbs[l], du, mode="tn", colsum=True, name="in_proj_dw", tm=1024, tn=512, tk=seq)
        parts = [dwin, dwa, dwc, dwo]
        swap = _swap_rider(parts, MATRIX_CORE_AXIS)
        if l == 0:
            swap = _both(swap, _join_rider(blocks, [axis + 1 for axis in MATRIX_CORE_AXIS], 1, depth - 1))
        dx, rode = _matmul(du, w_in, layer=l, mode="nt", add=dxres, name="in_proj_dx", tm=512, tn=1024, tk=du.shape[1], rider=swap)
        got, blocks = rode[:n_mat], (rode[n_mat:] if l == 0 else blocks)
        waiting = (l, [_pair_sum_layer(parts[t], got[t], MATRIX_CORE_AXIS[t], place, name="pair_sum_" + MATRICES[t])
                       for t in range(n_mat)])
        grads[l] = dict(b_in=dbin[0], conv_w=dconvw[:CONV_WIDTH], conv_b=dcbias[0], conv_ln_g=dcg[0], conv_ln_b=dcb[0],
                        b_conv_proj=dbp[0], ln_g=dlg[0], ln_b=dlb[0])
    packed = jnp.stack([jnp.concatenate([grads[l][n] for n in VECTORS] + [grads[l]["conv_w"].reshape(-1)]) for l in range(depth)])
    return loss, dx, packed, waiting[1], blocks


MESH = pl.DeviceIdType.MESH


def _position():
    x, y, c = lax.axis_index("x"), lax.axis_index("y"), lax.axis_index("c")
    return x, y, c, [(1 - x, y), (x, 1 - y), (1 - x, 1 - y)]


def _cut(ref, axis, start, size):
    idx = [slice(None)] * len(ref.shape)
    idx[axis] = pl.ds(start, size)
    return ref.at[tuple(idx)]


def _remote(src, dst, send_sem, recv_sem, device):
    return pltpu.make_async_remote_copy(src_ref=src, dst_ref=dst, send_sem=send_sem, recv_sem=recv_sem,
                                        device_id=device, device_id_type=MESH)


def _gather_rider(wholes, chip_axes, spans, relay):
    n = len(wholes)

    def region(dst, t, chip, half):
        first, count = spans[t]
        _, rows, cols = wholes[t].shape
        if chip_axes[t] == 2:
            size, part = cols // N_CHIPS, rows // 2
            ref = _cut(_cut(dst[t], 2, pl.multiple_of(chip * size, size), size), 1, pl.multiple_of(half * part, part), part)
        else:
            size = rows // N_CHIPS
            part = size // 2
            ref = _cut(dst[t], 1, pl.multiple_of(chip * size + half * part, part), part)
        return _cut(ref, 0, first, count)

    def copies(dst, sems, receiving):
        x, y, c, chips = _position()
        send_sem, recv_sem = sems
        out = []
        for t in range(n):
            for j, chip in enumerate(chips):
                theirs = 2 * chip[0] + chip[1]
                if relay:
                    ref = region(dst, t, theirs, 1 - c if receiving else c)
                    peer = (x, y, 1 - c)
                else:
                    ref = region(dst, t, theirs if receiving else 2 * x + y, c)
                    peer = (*chip, c)
                out.append(_remote(ref, ref, send_sem.at[t, j], recv_sem.at[t, j], peer))
        return out

    def start(ins, outs, sems):
        for cp in copies(outs, sems, False):
            cp.start()

    def finish(ins, outs, sems):
        for cp in copies(outs, sems, True):
            cp.wait_recv()
        for cp in copies(outs, sems, False):
            cp.wait_send()

    return _Rider(operands=list(wholes), out_shape=[jax.ShapeDtypeStruct(w.shape, w.dtype) for w in wholes],
                  aliases={t: t for t in range(n)}, scratch=[pltpu.SemaphoreType.DMA((n, N_CHIPS - 1))] * 2,
                  start=start, finish=finish)


def _chain(first, second):
    cut = len(first.scratch)

    def start(ins, outs, sems):
        first.start(ins, outs, sems[:cut])
        first.finish(ins, outs, sems[:cut])
        second.start(ins, outs, sems[cut:])

    def finish(ins, outs, sems):
        second.finish(ins, outs, sems[cut:])

    return first._replace(scratch=list(first.scratch) + list(second.scratch), start=start, finish=finish)


def _swap_rider(parts, core_axes):
    n = len(parts)
    halves = []
    for arr, axis in zip(parts, core_axes):
        shape = list(arr.shape)
        shape[axis] //= 2
        halves.append(jax.ShapeDtypeStruct(tuple(shape), arr.dtype))

    def copies(ins, outs, sems):
        x, y, c, _ = _position()
        out = []
        for t in range(n):
            size = halves[t].shape[core_axes[t]]
            piece = _cut(ins[t], core_axes[t], pl.multiple_of((1 - c) * size, size), size)
            out.append(_remote(piece, outs[t], sems[0].at[t], sems[1].at[t], (x, y, 1 - c)))
        return out

    def start(ins, outs, sems):
        for cp in copies(ins, outs, sems):
            cp.start()

    def finish(ins, outs, sems):
        for cp in copies(ins, outs, sems):
            cp.wait()

    return _Rider(operands=list(parts), out_shape=halves, aliases={}, scratch=[pltpu.SemaphoreType.DMA((n,))] * 2,
                  start=start, finish=finish)


def _exchange_rider(partials, chip_axes):
    n = len(partials)
    quarters = []
    for arr, axis in zip(partials, chip_axes):
        shape = list(arr.shape)
        shape[axis] //= N_CHIPS
        quarters.append(jax.ShapeDtypeStruct((N_CHIPS - 1, *shape), arr.dtype))

    def copies(ins, outs, sems):
        x, y, c, chips = _position()
        out = []
        for t in range(n):
            size = quarters[t].shape[1 + chip_axes[t]]
            for j, chip in enumerate(chips):
                piece = _cut(ins[t], chip_axes[t], pl.multiple_of((2 * chip[0] + chip[1]) * size, size), size)
                out.append(_remote(piece, outs[t].at[j], sems[0].at[t, j], sems[1].at[t, j], (*chip, c)))
        return out

    def start(ins, outs, sems):
        for cp in copies(ins, outs, sems):
            cp.start()

    def finish(ins, outs, sems):
        for cp in copies(ins, outs, sems):
            cp.wait()

    return _Rider(operands=list(partials), out_shape=quarters, aliases={}, scratch=[pltpu.SemaphoreType.DMA((n, N_CHIPS - 1))] * 2,
                  start=start, finish=finish)


def _place_block(shard, chip_axis, place, dtype, *, name):
    depth, rows, cols = shard.shape
    rows_t = _row_tile(rows, cols * 4 * 4, 16 * 1024 * 1024)
    steps = rows // rows_t
    shape = list(shard.shape)
    shape[chip_axis] *= N_CHIPS
    if chip_axis == 1:
        out_spec = pl.BlockSpec((None, rows_t, cols), lambda l, i, p: (l, p[1] * steps + i, 0))
    else:
        out_spec = pl.BlockSpec((None, rows_t, cols), lambda l, i, p: (l, i, p[1]))

    def body(place_ref, src_ref, out_ref):
        out_ref[...] = src_ref[...].astype(dtype)

    return pl.pallas_call(
        body, name=name, out_shape=jax.ShapeDtypeStruct(tuple(shape), dtype),
        grid_spec=pltpu.PrefetchScalarGridSpec(num_scalar_prefetch=1, grid=(depth, steps),
                                               in_specs=[pl.BlockSpec((None, rows_t, cols), lambda l, i, p: (l, i, 0))],
                                               out_specs=out_spec),
        compiler_params=_cparams(("arbitrary", "arbitrary")),
    )(place, shard)


def _join_rider(blocks, core_axes, first, count):
    n = len(blocks)

    def copies(outs, sems, receiving):
        x, y, c, _ = _position()
        out = []
        for t in range(n):
            size = blocks[t].shape[core_axes[t]] // 2
            half = 1 - c if receiving else c
            ref = _cut(_cut(outs[t], core_axes[t], pl.multiple_of(half * size, size), size), 0, first, count)
            out.append(_remote(ref, ref, sems[0].at[t], sems[1].at[t], (x, y, 1 - c)))
        return out

    def start(ins, outs, sems):
        for cp in copies(outs, sems, False):
            cp.start()

    def finish(ins, outs, sems):
        for cp in copies(outs, sems, True):
            cp.wait_recv()
        for cp in copies(outs, sems, False):
            cp.wait_send()

    return _Rider(operands=list(blocks), out_shape=[jax.ShapeDtypeStruct(b.shape, b.dtype) for b in blocks],
                  aliases={t: t for t in range(n)}, scratch=[pltpu.SemaphoreType.DMA((n,))] * 2, start=start, finish=finish)


def _small_rider(vec):
    n_dev = 2 * N_CHIPS

    def copies(ins, outs, sems, receiving):
        x, y, c, _ = _position()
        flip = lambda v, bit: 1 - v if bit else v
        out = []
        for mask in range(1, n_dev):
            peer = (flip(x, mask & 4), flip(y, mask & 2), flip(c, mask & 1))
            row = 4 * peer[0] + 2 * peer[1] + peer[2] if receiving else 4 * x + 2 * y + c
            out.append(_remote(ins[0], outs[0].at[row], sems[0].at[mask - 1], sems[1].at[mask - 1], peer))
        return out

    def own(ins, outs, sems):
        x, y, c, _ = _position()
        return pltpu.make_async_copy(ins[0], outs[0].at[4 * x + 2 * y + c], sems[2])

    def start(ins, outs, sems):
        own(ins, outs, sems).start()
        for cp in copies(ins, outs, sems, False):
            cp.start()

    def finish(ins, outs, sems):
        for cp in copies(ins, outs, sems, True):
            cp.wait_recv()
        for cp in copies(ins, outs, sems, False):
            cp.wait_send()
        own(ins, outs, sems).wait()

    return _Rider(operands=[vec], out_shape=[jax.ShapeDtypeStruct((n_dev, *vec.shape), vec.dtype)], aliases={},
                  scratch=[pltpu.SemaphoreType.DMA((n_dev - 1,))] * 2 + [pltpu.SemaphoreType.DMA(())], start=start, finish=finish)


def _both(first, second):
    n_in, n_out, n_sem = len(first.operands), len(first.out_shape), len(first.scratch)

    def start(ins, outs, sems):
        first.start(ins[:n_in], outs[:n_out], sems[:n_sem])
        second.start(ins[n_in:], outs[n_out:], sems[n_sem:])

    def finish(ins, outs, sems):
        first.finish(ins[:n_in], outs[:n_out], sems[:n_sem])
        second.finish(ins[n_in:], outs[n_out:], sems[n_sem:])

    aliases = dict(first.aliases)
    aliases.update({n_in + i: n_out + o for i, o in second.aliases.items()})
    return _Rider(operands=list(first.operands) + list(second.operands), out_shape=list(first.out_shape) + list(second.out_shape),
                  aliases=aliases, scratch=list(first.scratch) + list(second.scratch), start=start, finish=finish)


def _row_tile(rows, row_bytes, budget):
    tile = rows
    for cand in (512, 256, 128, 64, 32, 16, 8):
        if rows % cand == 0:
            tile = cand
            if cand * row_bytes <= budget:
                break
    return tile


def _pair_sum_layer(part, got, core_axis, place, *, name):
    rows, cols = got.shape
    rows_t = _row_tile(rows, cols * 4 * 6, 16 * 1024 * 1024)
    steps = rows // rows_t
    if core_axis == 0:
        part_spec = pl.BlockSpec((rows_t, cols), lambda i, p: (p[0] * steps + i, 0))
    else:
        part_spec = pl.BlockSpec((rows_t, cols), lambda i, p: (i, p[0]))
    own_spec = pl.BlockSpec((rows_t, cols), lambda i, p: (i, 0))

    def body(place_ref, part_ref, got_ref, out_ref):
        out_ref[...] = (part_ref[...] + got_ref[...]).astype(BF16)

    return pl.pallas_call(
        body, name=name, out_shape=jax.ShapeDtypeStruct(got.shape, BF16),
        grid_spec=pltpu.PrefetchScalarGridSpec(num_scalar_prefetch=1, grid=(steps,), in_specs=[part_spec, own_spec], out_specs=own_spec),
        compiler_params=_cparams(("arbitrary",)),
    )(place, part, got)


def _chip_sum_layer(partial, got, blocks, layer, depth, chip_axis, core_axis, place, *, name):
    _, rows, cols = got.shape
    rows_t = _row_tile(rows, cols * 4 * 10, 24 * 1024 * 1024)
    steps = rows // rows_t
    if chip_axis == 0:
        own_spec = pl.BlockSpec((rows_t, cols), lambda i, p: (p[1] * steps + i, 0))
    else:
        own_spec = pl.BlockSpec((rows_t, cols), lambda i, p: (i, p[1]))
    got_spec = pl.BlockSpec((N_CHIPS - 1, rows_t, cols), lambda i, p: (0, i, 0))
    shape = [depth, rows, cols]
    shape[1 + core_axis] *= 2
    if core_axis == 0:
        out_spec = pl.BlockSpec((None, rows_t, cols), lambda i, p: (layer, p[0] * steps + i, 0))
    else:
        out_spec = pl.BlockSpec((None, rows_t, cols), lambda i, p: (layer, i, p[0]))

    def body(place_ref, own_ref, got_ref, *rest):
        out_ref = rest[-1]
        up = lambda val: val.astype(F32)
        out_ref[...] = ((up(own_ref[...]) + up(got_ref[0])) + up(got_ref[1])) + up(got_ref[2])

    in_specs, operands, aliases = [own_spec, got_spec], [place, partial, got], {}
    if blocks is not None:
        in_specs.append(pl.BlockSpec(memory_space=pl.ANY))
        operands.append(blocks)
        aliases = {3: 0}
    return pl.pallas_call(
        body, name=name, out_shape=jax.ShapeDtypeStruct(tuple(shape), F32),
        grid_spec=pltpu.PrefetchScalarGridSpec(num_scalar_prefetch=1, grid=(steps,), in_specs=in_specs, out_specs=out_spec),
        input_output_aliases=aliases, compiler_params=_cparams(("arbitrary",)),
    )(*operands)


def _sum_devices(stack, *, name):
    def body(src_ref, out_ref):
        total = src_ref[0]
        for d in range(1, stack.shape[0]):
            total = total + src_ref[d]
        out_ref[...] = total

    return pl.pallas_call(body, name=name, out_shape=jax.ShapeDtypeStruct(stack.shape[1:], F32))(stack)


def kernel(x, w_in, b_in, conv_w, conv_b, conv_ln_g, conv_ln_b, w_att_proj, w_conv_proj, b_conv_proj, w_out, ln_g, ln_b, loss_target, m_w_in, m_b_in, m_conv_w, m_conv_b, m_conv_ln_g, m_conv_ln_b, m_w_att_proj, m_w_conv_proj, m_b_conv_proj, m_w_out, m_ln_g, m_ln_b, v_w_in, v_b_in, v_conv_w, v_conv_b, v_conv_ln_g, v_conv_ln_b, v_w_att_proj, v_w_conv_proj, v_b_conv_proj, v_w_out, v_ln_g, v_ln_b):
    depth = w_in.shape[0]
    d_model = x.shape[-1]
    half = d_model // 2
    chip = 2 * lax.axis_index("x") + lax.axis_index("y")
    place = jnp.stack([lax.axis_index("c"), chip]).astype(jnp.int32)
    vec3 = lambda v: v.reshape(depth, 1, -1)

    taps = jnp.pad(conv_w, ((0, 0), (0, CONV_PAD - CONV_WIDTH), (0, 0)))
    gathered = [("w_in", w_in, 2, BF16), ("w_att_proj", w_att_proj, 2, BF16), ("w_conv_proj", w_conv_proj, 2, BF16),
                ("w_out", w_out, 1, BF16), ("conv_w", taps, 2, F32)]
    wholes = [_place_block(arr, axis, place, dtype, name="place_" + n) for n, arr, axis, dtype in gathered]
    loss, grad_x, packed, waiting, blocks = _train_pass(x[0], loss_target[0], wholes[:4], wholes[4], b_in, vec3(conv_b),
                                                        vec3(conv_ln_g), vec3(conv_ln_b), vec3(b_conv_proj), vec3(ln_g), vec3(ln_b), place)
    loss = lax.psum(loss[0, 0], ("x", "y", "c"))

    names = ["w_in", "b_in", "conv_w", "conv_b", "conv_ln_g", "conv_ln_b", "w_att_proj", "w_conv_proj", "b_conv_proj", "w_out", "ln_g", "ln_b"]
    weights = dict(zip(names, (w_in, b_in, conv_w, conv_b, conv_ln_g, conv_ln_b, w_att_proj, w_conv_proj, b_conv_proj, w_out, ln_g, ln_b)))
    first = dict(zip(names, (m_w_in, m_b_in, m_conv_w, m_conv_b, m_conv_ln_g, m_conv_ln_b, m_w_att_proj, m_w_conv_proj, m_b_conv_proj, m_w_out, m_ln_g, m_ln_b)))
    second = dict(zip(names, (v_w_in, v_b_in, v_conv_w, v_conv_b, v_conv_ln_g, v_conv_ln_b, v_w_att_proj, v_w_conv_proj, v_b_conv_proj, v_w_out, v_ln_g, v_ln_b)))
    flat = lambda arr: arr.reshape(-1, arr.shape[-1])

    gathered, *got = _lone_call(_both(_small_rider(packed), _exchange_rider(waiting, MATRIX_CHIP_AXIS)), name="exchange_last")
    blocks = _lone_call(_join_rider(_chip_sums(0, depth, waiting, got, blocks, place), [axis + 1 for axis in MATRIX_CORE_AXIS], 0, 1),
                        name="pair_join")
    reduced = dict(zip(MATRICES, blocks))

    total = _sum_devices(gathered, name="sum_devices")
    widths = dict(b_in=b_in.shape[1], conv_b=half, conv_ln_g=half, conv_ln_b=half, b_conv_proj=d_model, ln_g=d_model, ln_b=d_model)
    offset = 0
    for n in VECTORS:
        reduced[n] = total[:, offset:offset + widths[n]]
        offset += widths[n]
    taps = total[:, offset:].reshape(depth, CONV_WIDTH, half)
    reduced["conv_w"] = lax.dynamic_slice_in_dim(taps, chip * conv_w.shape[2], conv_w.shape[2], axis=2)

    delta, new_m, new_v = {}, {}, {}
    for n in names:
        shape = weights[n].shape
        *grad, d, m, v = _adamw(flat(weights[n]), flat(reduced[n]), flat(first[n]), flat(second[n]), name="adamw_" + n,
                                echo=n in MATRICES)
        if grad:
            reduced[n] = grad[0]
        delta[n], new_m[n], new_v[n] = d.reshape(shape), m.reshape(shape), v.reshape(shape)
    return (loss, grad_x[None], *[reduced[n].reshape(weights[n].shape) for n in names], *[delta[n] for n in names],
            *[new_m[n] for n in names], *[new_v[n] for n in names])
```

```python
import functools
from typing import Callable, NamedTuple

import jax
import jax.numpy as jnp
from jax import lax
from jax.experimental import pallas as pl
from jax.experimental.pallas import tpu as pltpu

F32 = jnp.float32
BF16 = jnp.bfloat16

HEAD_DIM = 64
LANES = 128
CONV_WIDTH = 31
CONV_PAD = 32
SUBLANES = 8
LN_EPS = 1e-5
DEPTH = 4
DEEPNORM_ALPHA = (2 * DEPTH) ** 0.25
ATT_SCALE = HEAD_DIM ** -0.5
ATT_TILE = 128
ATT_DEAD = -88.0
ATT_GROUP = 2
ATT_GROUP_FWD = 8
ATT_FILL = -1e30

ADAM_LR = 0.001
ADAM_B1 = 0.9
ADAM_B2 = 0.999
ADAM_EPS = 1e-08
ADAM_WD = 0.01
ADAM_STEP = 10

VMEM_LIMIT = 56 * 1024 * 1024

N_CHIPS = 4


def _cparams(sem):
    return pltpu.CompilerParams(dimension_semantics=sem, vmem_limit_bytes=VMEM_LIMIT)


def _sigmoid(x):
    return 1.0 / (1.0 + jnp.exp(-x))


class _Rider(NamedTuple):
    operands: list
    out_shape: list
    aliases: dict
    scratch: list
    start: Callable
    finish: Callable


def _host_call(body, rider, *, name, grid, in_specs, out_specs, out_shape, scratch, operands, semantics, host_aliases=None):
    n_in, n_out = len(in_specs), len(out_specs)
    aliases = dict(host_aliases or {})
    if rider is not None:
        r_in, r_out = len(rider.operands), len(rider.out_shape)
        host_body = body

        def body(*refs):
            base = n_in + r_in
            ins, rins = refs[:n_in], refs[n_in:base]
            outs, routs = refs[base:base + n_out], refs[base + n_out:base + n_out + r_out]
            rest = refs[base + n_out + r_out:]
            split = len(rest) - len(rider.scratch)
            ids = [pl.program_id(axis) for axis in range(len(grid))]
            first = functools.reduce(jnp.logical_and, [i == 0 for i in ids])
            last = functools.reduce(jnp.logical_and, [i == g - 1 for i, g in zip(ids, grid)])

            @pl.when(first)
            def _():
                rider.start(rins, routs, rest[split:])

            host_body(*ins, *outs, *rest[:split])

            @pl.when(last)
            def _():
                rider.finish(rins, routs, rest[split:])

        hbm = pl.BlockSpec(memory_space=pl.ANY)
        in_specs = list(in_specs) + [hbm] * r_in
        out_specs = list(out_specs) + [hbm] * r_out
        out_shape = list(out_shape) + list(rider.out_shape)
        scratch = list(scratch) + list(rider.scratch)
        operands = list(operands) + list(rider.operands)
        aliases.update({n_in + i: n_out + o for i, o in rider.aliases.items()})
    res = pl.pallas_call(
        body, name=name, grid=grid, in_specs=list(in_specs), out_specs=list(out_specs), out_shape=list(out_shape),
        scratch_shapes=list(scratch), input_output_aliases=aliases, compiler_params=_cparams(semantics),
    )(*operands)
    return list(res[:n_out]), list(res[n_out:])


def _lone_call(rider, *, name):
    r_in = len(rider.operands)

    def body(*refs):
        ins, outs, sems = refs[:r_in], refs[r_in:r_in + len(rider.out_shape)], refs[r_in + len(rider.out_shape):]
        rider.start(ins, outs, sems)
        rider.finish(ins, outs, sems)

    hbm = pl.BlockSpec(memory_space=pl.ANY)
    return list(pl.pallas_call(
        body, name=name, in_specs=[hbm] * r_in, out_specs=[hbm] * len(rider.out_shape), out_shape=list(rider.out_shape),
        scratch_shapes=list(rider.scratch), input_output_aliases=dict(rider.aliases),
    )(*rider.operands))


def _fit(tile, dim):
    assert dim % LANES == 0
    tile = min(tile, dim) // LANES * LANES
    while dim % tile:
        tile -= LANES
    return tile


_DIMS = {"nn": ((1,), (0,)), "nt": ((1,), (1,)), "tn": ((0,), (0,))}


def _matmul(a, b, *, mode, name, layer=None, bias=None, add=None, colsum=False, out_dtype=F32, tm=1024, tn=512, tk=1024,
            rider=None):
    b_shape = b.shape if layer is None else b.shape[1:]
    if mode == "nn":
        (m, k), (k2, n) = a.shape, b_shape
    elif mode == "nt":
        (m, k), (n, k2) = a.shape, b_shape
    else:
        (k, m), (k2, n) = a.shape, b_shape
    assert k == k2
    tm, tn, tk = _fit(tm, m), _fit(tn, n), _fit(tk, k)
    gm, gn, nk = m // tm, n // tn, k // tk

    a_spec = pl.BlockSpec((tk, tm), lambda i, j, kk: (kk, i)) if mode == "tn" else pl.BlockSpec((tm, tk), lambda i, j, kk: (i, kk))
    if layer is None:
        b_spec = pl.BlockSpec((tn, tk), lambda i, j, kk: (j, kk)) if mode == "nt" else pl.BlockSpec((tk, tn), lambda i, j, kk: (kk, j))
    elif mode == "nt":
        b_spec = pl.BlockSpec((None, tn, tk), lambda i, j, kk: (layer, j, kk))
    else:
        b_spec = pl.BlockSpec((None, tk, tn), lambda i, j, kk: (layer, kk, j))
    in_specs, operands = [a_spec, b_spec], [a, b]
    if bias is not None:
        in_specs.append(pl.BlockSpec((1, tn), lambda i, j, kk: (0, j)))
        operands.append(bias)
    if add is not None:
        in_specs.append(pl.BlockSpec((tm, tn), lambda i, j, kk: (i, j)))
        operands.append(add)
    out_shape = [jax.ShapeDtypeStruct((m, n), out_dtype)]
    out_specs = [pl.BlockSpec((tm, tn), lambda i, j, kk: (i, j))]
    scratch = [pltpu.VMEM((tm, tn), F32)] if nk > 1 else []
    if colsum:
        assert mode == "tn"
        out_shape.append(jax.ShapeDtypeStruct((gm, 1, n), F32))
        out_specs.append(pl.BlockSpec((1, 1, tn), lambda i, j, kk: (i, 0, j)))
        if nk > 1:
            scratch.append(pltpu.VMEM((1, tn), F32))
    has_bias, has_add = bias is not None, add is not None

    def body(*refs):
        refs = list(refs)
        a_ref, b_ref = refs[0], refs[1]
        pos = 2
        bias_ref = add_ref = None
        if has_bias:
            bias_ref = refs[pos]
            pos += 1
        if has_add:
            add_ref = refs[pos]
            pos += 1
        o_ref = refs[pos]
        pos += 1
        cs_ref = None
        if colsum:
            cs_ref = refs[pos]
            pos += 1

        def finish(out, sums):
            if has_bias:
                out = out + bias_ref[...]
            if has_add:
                out = out + add_ref[...]
            o_ref[...] = out.astype(out_dtype)
            if colsum:
                cs_ref[0] = sums

        bv = b_ref[...]
        prod = lax.dot_general(a_ref[...].astype(BF16), bv.astype(BF16), (_DIMS[mode], ((), ())), preferred_element_type=F32)
        sums = jnp.sum(bv.astype(F32), axis=0, keepdims=True) if colsum else None
        if nk == 1:
            finish(prod, sums)
            return
        acc_ref = refs[pos]
        cs_acc = refs[pos + 1] if colsum else None
        kk = pl.program_id(2)

        @pl.when(kk == 0)
        def _():
            acc_ref[...] = jnp.zeros_like(acc_ref)
            if colsum:
                cs_acc[...] = jnp.zeros_like(cs_acc)

        acc_ref[...] += prod
        if colsum:
            cs_acc[...] += sums

        @pl.when(kk == nk - 1)
        def _():
            finish(acc_ref[...], cs_acc[...] if colsum else None)

    res, rode = _host_call(body, rider, name=name, grid=(gm, gn, nk), in_specs=in_specs, out_specs=out_specs, out_shape=out_shape,
                           scratch=scratch, operands=operands, semantics=("arbitrary", "arbitrary", "arbitrary"))
    out = (res[0], res[1][0]) if colsum else res[0]
    return out if rider is None else (out, rode)


def _scan_matrices():
    t = ATT_TILE
    r = lax.broadcasted_iota(jnp.int32, (t, t), 0)
    c = lax.broadcasted_iota(jnp.int32, (t, t), 1)
    ones = jnp.ones((t, t), F32)
    suffix = jnp.concatenate([(r > c).astype(F32), ones], axis=1)
    prefix = jnp.concatenate([(r < c).astype(F32), ones], axis=1)
    stack = lambda mat: jnp.concatenate([mat, mat], axis=0).astype(BF16)
    return stack(suffix), stack(prefix)


def _split_halves(val):
    hi = val.astype(BF16)
    lo = (val - hi.astype(F32)).astype(BF16)
    return jnp.concatenate([hi, lo], axis=1)


def _split_scan(val, mat_ref):
    return jnp.dot(_split_halves(val), mat_ref[...], preferred_element_type=F32)


def _pair_scores(q, k_lo, k_hi, masked):
    t = ATT_TILE
    z2 = lax.dot_general(q, jnp.concatenate([k_lo, k_hi], axis=0), (((1,), (1,)), ((), ())), preferred_element_type=F32)
    out = []
    for h in range(2):
        z = z2[:, h * t:(h + 1) * t]
        sp = jnp.log(1.0 + jnp.exp(-jnp.abs(z)))
        f = jnp.minimum(-z, 0.0) - sp
        a = f + z
        if masked:
            causal = lax.broadcasted_iota(jnp.int32, (t, t), 1) < lax.broadcasted_iota(jnp.int32, (t, t), 0)
            f = jnp.where(causal, f, 0.0)
        out.append((_split_halves(f), a))
    return out


def _any_alive(runs):
    top = functools.reduce(jnp.maximum, [run for per_head in runs for run in per_head])
    return (jnp.max(top) > ATT_DEAD).astype(jnp.int32)


def _head_copies(seq, src_ref, scale, lo_ref, hi_ref, plain_ref):
    chunk = min(256, seq)
    low = lax.broadcasted_iota(jnp.int32, (chunk, LANES), 1) < HEAD_DIM

    def step(r, carry):
        rows = pl.ds(pl.multiple_of(r * chunk, chunk), chunk)
        val = src_ref[rows, :]
        if scale != 1.0:
            val = val * scale
        if lo_ref is not None:
            lo_ref[rows, :] = jnp.where(low, val, 0.0).astype(BF16)
            hi_ref[rows, :] = jnp.where(low, 0.0, val).astype(BF16)
        if plain_ref is not None:
            plain_ref[rows, :] = val.astype(BF16)
        return carry

    lax.fori_loop(0, seq // chunk, step, 0)


def _attn_fwd(u, scan_suffix, *, seq, d_att, name, rider=None):
    t = ATT_TILE
    nq = seq // t
    pairs = d_att // LANES
    grp = ATT_GROUP_FWD
    assert nq % grp == 0

    def body(q_ref, k_ref, v_ref, um_ref, o_ref, qq, k0, k1, v0, v1, f2_s, a_s, lg_s, tot_s, run_s, acc_s):
        _head_copies(seq, q_ref, ATT_SCALE, None, None, qq)
        _head_copies(seq, k_ref, 1.0, k0, k1, None)
        _head_copies(seq, v_ref, 1.0, v0, v1, None)

        def group(gi, carry):
            qb0 = gi * grp
            qrows = [pl.ds(pl.multiple_of((qb0 + g) * t, t), t) for g in range(grp)]
            qv = [qq[qrows[g], :] for g in range(grp)]

            chains = [(h, g) for g in range(grp) for h in range(2)]

            def key_rows(g, i):
                return pl.ds(pl.multiple_of(jnp.maximum(qb0 + g - i, 0) * t, t), t)

            def stage1(i, masked):
                out = []
                for g in range(grp):
                    krows = key_rows(g, i)
                    out += _pair_scores(qv[g], k0[krows, :], k1[krows, :], masked)
                return out

            def stage2(halves, a, masked):
                scan = jnp.dot(halves, um_ref[...], preferred_element_type=F32)
                logit = a + scan[:, :t]
                if masked:
                    causal = lax.broadcasted_iota(jnp.int32, (t, t), 1) < lax.broadcasted_iota(jnp.int32, (t, t), 0)
                    logit = jnp.where(causal, logit, ATT_FILL)
                return logit, scan[:, t:]

            def put(halves_a=None, logit_total=None):
                for c in range(len(chains)):
                    if halves_a is not None:
                        f2_s[c], a_s[c] = halves_a[c]
                    if logit_total is not None:
                        lg_s[c], tot_s[c] = logit_total[c]

            first = stage1(0, True)
            put(halves_a=stage1(1, False), logit_total=[stage2(f2, a, True) for f2, a in first])
            for c in range(len(chains)):
                run_s[c] = jnp.zeros((t, t), F32)
            for g in range(grp):
                acc_s[g] = jnp.zeros((t, LANES), F32)

            def wbody(st):
                i = st[0]
                held = [(f2_s[c], a_s[c]) for c in range(len(chains))]
                logits = [lg_s[c] for c in range(len(chains))]
                totals = [tot_s[c] for c in range(len(chains))]
                runs = [run_s[c] for c in range(len(chains))]
                accs = [acc_s[g] for g in range(grp)]
                for g in range(grp):
                    krows = key_rows(g, i)
                    gone = jnp.where(qb0 + g - i >= 0, 0.0, ATT_FILL)
                    weights = []
                    for c in (2 * g, 2 * g + 1):
                        run = runs[c] + gone
                        weights.append(jnp.exp(logits[c] + run).astype(BF16))
                        runs[c] = run + totals[c]
                    accs[g] = accs[g] + jnp.dot(jnp.concatenate(weights, axis=1), jnp.concatenate([v0[krows, :], v1[krows, :]], axis=0),
                                                preferred_element_type=F32)
                ahead2 = [stage2(f2, a, False) for f2, a in held]
                ahead1 = stage1(i + 2, False)
                put(halves_a=ahead1, logit_total=ahead2)
                for c in range(len(chains)):
                    run_s[c] = runs[c]
                for g in range(grp):
                    acc_s[g] = accs[g]
                more = jnp.logical_and(i + 1 <= qb0 + grp - 1, _any_alive([runs]) > 0)
                return i + 1, more.astype(jnp.int32)

            lax.while_loop(lambda st: st[1] > 0, wbody, (jnp.int32(0), jnp.int32(1)))
            for g in range(grp):
                o_ref[qrows[g], :] = acc_s[g]
            return carry

        lax.fori_loop(0, nq // grp, group, 0)

    blk = lambda base: pl.BlockSpec((seq, LANES), lambda j, base=base: (0, base + j))
    res, rode = _host_call(
        body, rider, name=name, grid=(pairs,),
        in_specs=[blk(0), blk(pairs), blk(2 * pairs), pl.BlockSpec((2 * t, 2 * t), lambda j: (0, 0))],
        out_specs=[pl.BlockSpec((seq, LANES), lambda j: (0, j))],
        out_shape=[jax.ShapeDtypeStruct((seq, d_att), F32)],
        scratch=[pltpu.VMEM((seq, LANES), BF16)] * 5 + [pltpu.VMEM((2 * grp, t, 2 * t), BF16)]
        + [pltpu.VMEM((2 * grp, t, t), F32)] * 4 + [pltpu.VMEM((grp, t, LANES), F32)],
        operands=[u, u, u, scan_suffix], semantics=("arbitrary",))
    return res[0] if rider is None else (res[0], rode)


def _columns_copy(stage_ref, du_ref, rows, col, sem):
    width = stage_ref.shape[-1]
    cols = pl.ds(pl.multiple_of(col, LANES), width)
    return pltpu.make_async_copy(stage_ref, du_ref.at[slice(None) if rows is None else rows, cols], sem)


def _attn_bwd(u, d_att_out, du, scan_suffix, scan_prefix, *, seq, d_att, name, rider=None):
    t = ATT_TILE
    nq = seq // t
    pairs = d_att // LANES
    grp = ATT_GROUP
    assert nq % grp == 0

    def body(q_ref, k_ref, v_ref, do_ref, um_ref, pm_ref, du_old, du_ref,
             qq, q0, q1, k0, k1, v0, v1, dd, do0, do1, dq_acc, dk_acc, dv_acc, g_st, b_st,
             f2_s, a_s, lg_s, tot_s, dw_s, run_s, p_s, pt_s, grun_s, dq_s, dq_o, dk_o, dv_o, out_sem):
        _head_copies(seq, q_ref, ATT_SCALE, q0, q1, qq)
        _head_copies(seq, k_ref, 1.0, k0, k1, None)
        _head_copies(seq, v_ref, 1.0, v0, v1, None)
        _head_copies(seq, do_ref, 1.0, do0, do1, dd)
        dk_acc[...] = jnp.zeros_like(dk_acc)
        dv_acc[...] = jnp.zeros_like(dv_acc)

        tn = (((0,), (0,)), ((), ()))
        nt = (((1,), (1,)), ((), ()))

        def stacked(lo_ref, hi_ref, rows):
            return jnp.concatenate([lo_ref[rows, :], hi_ref[rows, :]], axis=0)

        def group(gi, carry):
            qb0 = gi * grp
            qrows = [pl.ds(pl.multiple_of((qb0 + g) * t, t), t) for g in range(grp)]
            qv = [qq[qrows[g], :] for g in range(grp)]
            dov = [dd[qrows[g], :] for g in range(grp)]
            q_heads = [stacked(q0, q1, qrows[g]) for g in range(grp)]
            do_heads = [stacked(do0, do1, qrows[g]) for g in range(grp)]

            def key_rows(g, i):
                return pl.ds(pl.multiple_of(jnp.maximum(qb0 + g - i, 0) * t, t), t)

            chains = [(h, g) for g in range(grp) for h in range(2)]
            every = range(len(chains))

            def stage1(i, masked):
                out = []
                for g in range(grp):
                    krows = key_rows(g, i)
                    out += _pair_scores(qv[g], k0[krows, :], k1[krows, :], masked)
                return out

            def stage2(i, held, masked):
                out = []
                for g in range(grp):
                    dw2 = lax.dot_general(dov[g], stacked(v0, v1, key_rows(g, i)), nt, preferred_element_type=F32)
                    for h in range(2):
                        c = 2 * g + h
                        halves, a = held[c]
                        scan = jnp.dot(halves, um_ref[...], preferred_element_type=F32)
                        logit = a + scan[:, :t]
                        if masked:
                            causal = lax.broadcasted_iota(jnp.int32, (t, t), 1) < lax.broadcasted_iota(jnp.int32, (t, t), 0)
                            logit = jnp.where(causal, logit, ATT_FILL)
                        b_st[c, i] = jnp.exp(a)
                        out.append((logit, scan[:, t:], dw2[:, h * t:(h + 1) * t]))
                return out

            def put(held=None, ready=None):
                for c in every:
                    if held is not None:
                        f2_s[c], a_s[c] = held[c]
                    if ready is not None:
                        lg_s[c], tot_s[c], dw_s[c] = ready[c]

            put(held=stage1(1, False), ready=stage2(0, stage1(0, True), True))
            for c in every:
                run_s[c] = jnp.zeros((t, t), F32)

            def wbody(st):
                i = st[0]
                held = [(f2_s[c], a_s[c]) for c in every]
                ready = [(lg_s[c], tot_s[c], dw_s[c]) for c in every]
                runs = [run_s[c] for c in every]
                for g in range(grp):
                    gone = jnp.where(qb0 + g - i >= 0, 0.0, ATT_FILL)
                    weights = []
                    for c in (2 * g, 2 * g + 1):
                        logit, total, dw = ready[c]
                        run = runs[c] + gone
                        w = jnp.exp(logit + run)
                        g_st[c, i] = w * dw
                        weights.append(w.astype(BF16))
                        runs[c] = run + total
                    dv_acc[key_rows(g, i), :] += lax.dot_general(jnp.concatenate(weights, axis=0), do_heads[g], tn,
                                                                 preferred_element_type=F32)
                ahead2 = stage2(i + 1, held, False)
                ahead1 = stage1(i + 2, False)
                put(held=ahead1, ready=ahead2)
                for c in every:
                    run_s[c] = runs[c]
                more = jnp.logical_and(i + 1 <= qb0 + grp - 1, _any_alive([runs]) > 0)
                return i + 1, more.astype(jnp.int32)

            steps = lax.while_loop(lambda st: st[1] > 0, wbody, (jnp.int32(0), jnp.int32(1)))[0]

            def prefix(i):
                out = []
                for c in every:
                    scan = _split_scan(g_st[c, i], pm_ref)
                    out.append((scan[:, :t], scan[:, t:]))
                return out

            def back(i, masked):
                sums = [(p_s[c], pt_s[c]) for c in every]
                gruns = [grun_s[c] for c in every]
                dqs = [dq_s[g] for g in range(grp)]
                for g in range(grp):
                    krows = key_rows(g, i)
                    dzs = []
                    for c in (2 * g, 2 * g + 1):
                        gt = g_st[c, i]
                        dz = gt - b_st[c, i] * (gt + gruns[c] + sums[c][0])
                        if masked:
                            causal = lax.broadcasted_iota(jnp.int32, (t, t), 1) < lax.broadcasted_iota(jnp.int32, (t, t), 0)
                            dz = jnp.where(causal, dz, 0.0)
                        dzs.append(dz.astype(BF16))
                        gruns[c] = gruns[c] + sums[c][1]
                    dqs[g] = dqs[g] + jnp.dot(jnp.concatenate(dzs, axis=1), stacked(k0, k1, krows), preferred_element_type=F32)
                    dk_acc[krows, :] += lax.dot_general(jnp.concatenate(dzs, axis=0), q_heads[g], tn, preferred_element_type=F32)
                return gruns, dqs

            def keep(sums=None, gruns=None, dqs=None):
                for c in every:
                    if sums is not None:
                        p_s[c], pt_s[c] = sums[c]
                    if gruns is not None:
                        grun_s[c] = gruns[c]
                if dqs is not None:
                    for g in range(grp):
                        dq_s[g] = dqs[g]

            keep(sums=prefix(steps - 1), gruns=[jnp.zeros((t, t), F32)] * len(chains), dqs=[jnp.zeros((t, LANES), F32)] * grp)

            def bbody(j, carry2):
                i = steps - 1 - j
                gruns, dqs = back(i, False)
                keep(sums=prefix(i - 1), gruns=gruns, dqs=dqs)
                return carry2

            lax.fori_loop(0, steps - 1, bbody, 0)
            _, dqs = back(0, True)
            for g in range(grp):
                dq_acc[qrows[g], :] = dqs[g]
            return carry

        lax.fori_loop(0, nq // grp, group, 0)

        pair = pl.program_id(0)
        stages = (dq_o, dk_o, dv_o)

        def copies(j):
            return [_columns_copy(stage, du_ref, None, k * d_att + j * LANES, out_sem.at[k]) for k, stage in enumerate(stages)]

        @pl.when(pair > 0)
        def _():
            for cp in copies(pair - 1):
                cp.wait()

        chunk = min(256, seq)

        def emit(r, carry):
            rows = pl.ds(pl.multiple_of(r * chunk, chunk), chunk)
            dq_o[rows, :] = (dq_acc[rows, :] * ATT_SCALE).astype(BF16)
            dk_o[rows, :] = dk_acc[rows, :].astype(BF16)
            dv_o[rows, :] = dv_acc[rows, :].astype(BF16)
            return carry

        lax.fori_loop(0, seq // chunk, emit, 0)
        for cp in copies(pair):
            cp.start()

        @pl.when(pair == pairs - 1)
        def _():
            for cp in copies(pair):
                cp.wait()

    blk = lambda base: pl.BlockSpec((seq, LANES), lambda j, base=base: (0, base + j), pipeline_mode=pl.Buffered(1))
    mat = pl.BlockSpec((2 * t, 2 * t), lambda j: (0, 0))
    hbm = pl.BlockSpec(memory_space=pl.ANY)
    res, rode = _host_call(
        body, rider, name=name, grid=(pairs,),
        in_specs=[blk(0), blk(pairs), blk(2 * pairs), blk(0), mat, mat, hbm],
        out_specs=[hbm], out_shape=[jax.ShapeDtypeStruct(du.shape, du.dtype)], host_aliases={6: 0},
        scratch=[pltpu.VMEM((seq, LANES), BF16)] * 10 + [pltpu.VMEM((seq, LANES), F32)] * 3
        + [pltpu.VMEM((2 * grp, nq + 1, t, t), F32)] * 2 + [pltpu.VMEM((2 * grp, t, 2 * t), BF16)]
        + [pltpu.VMEM((2 * grp, t, t), F32)] * 8 + [pltpu.VMEM((grp, t, LANES), F32)]
        + [pltpu.VMEM((seq, LANES), BF16)] * 3 + [pltpu.SemaphoreType.DMA((3,))],
        operands=[u, u, u, d_att_out, scan_suffix, scan_prefix, du], semantics=("arbitrary",))
    return res[0] if rider is None else (res[0], rode)


CONV_ROWS = 256


def _shifted(window, residue, rows):
    total = rows + CONV_PAD
    return window if residue == 0 else pltpu.roll(window, total - residue, 0)


def _glu_to_pad(seq, a_ref, b_ref, pad_ref):
    chunk = min(CONV_ROWS, seq)
    pad_ref[pl.ds(0, CONV_PAD), :] = jnp.zeros((CONV_PAD, LANES), F32)

    def step(r, carry):
        rows = pl.ds(pl.multiple_of(r * chunk, chunk), chunk)
        pad_ref[pl.ds(pl.multiple_of(r * chunk + CONV_PAD, SUBLANES), chunk), :] = a_ref[rows, :] * _sigmoid(b_ref[rows, :])
        return carry

    lax.fori_loop(0, seq // chunk, step, 0)


def _conv_fwd(u, conv_w, conv_b, *, layer, seq, d_conv, col_a, col_b, name, rider=None):
    blocks = d_conv // LANES
    rows_t = min(CONV_ROWS, seq)
    shift0 = CONV_PAD - (CONV_WIDTH - 1)

    def body(a_ref, b_ref, w_ref, bias_ref, o_ref, pad_ref):
        _glu_to_pad(seq, a_ref, b_ref, pad_ref)

        def step(r, carry):
            base = pl.multiple_of(r * rows_t, rows_t)
            window = pad_ref[pl.ds(base, rows_t + CONV_PAD), :]
            acc = jnp.zeros((rows_t, LANES), F32) + bias_ref[...]
            for residue in range(SUBLANES):
                moved = _shifted(window, residue, rows_t)
                for tap in range(CONV_WIDTH):
                    if (shift0 + tap) % SUBLANES == residue:
                        lo = (shift0 + tap) - residue
                        acc = acc + w_ref[tap:tap + 1, :] * moved[lo:lo + rows_t, :]
            o_ref[pl.ds(base, rows_t), :] = acc
            return carry

        lax.fori_loop(0, seq // rows_t, step, 0)

    res, rode = _host_call(
        body, rider, name=name, grid=(blocks,),
        in_specs=[pl.BlockSpec((seq, LANES), lambda j: (0, col_a + j)), pl.BlockSpec((seq, LANES), lambda j: (0, col_b + j)),
                  pl.BlockSpec((None, CONV_PAD, LANES), lambda j: (layer, 0, j)),
                  pl.BlockSpec((None, 1, LANES), lambda j: (layer, 0, j))],
        out_specs=[pl.BlockSpec((seq, LANES), lambda j: (0, j))],
        out_shape=[jax.ShapeDtypeStruct((seq, d_conv), F32)],
        scratch=[pltpu.VMEM((seq + CONV_PAD, LANES), F32)],
        operands=[u, u, conv_w, conv_b], semantics=("arbitrary",))
    return res[0] if rider is None else (res[0], rode)


def _conv_bwd(u, dc1, du, conv_w, *, layer, seq, d_conv, col_a, col_b, name):
    blocks = d_conv // LANES
    rows_t = min(CONV_ROWS, seq)
    shift0 = CONV_PAD - (CONV_WIDTH - 1)

    def body(a_ref, b_ref, d_ref, w_ref, du_old, du_ref, dw_ref, pad_ref, dpad_ref, dw_acc, da_ref, db_ref, out_sem):
        block = pl.program_id(0)

        def copies(j):
            return [_columns_copy(stage, du_ref, None, (base + j) * LANES, out_sem.at[k])
                    for k, (stage, base) in enumerate(((da_ref, col_a), (db_ref, col_b)))]

        _glu_to_pad(seq, a_ref, b_ref, pad_ref)
        dpad_ref[pl.ds(seq, CONV_PAD), :] = jnp.zeros((CONV_PAD, LANES), F32)

        def fill(r, carry):
            rows = pl.ds(pl.multiple_of(r * rows_t, rows_t), rows_t)
            dpad_ref[rows, :] = d_ref[rows, :]
            return carry

        lax.fori_loop(0, seq // rows_t, fill, 0)
        dw_acc[...] = jnp.zeros_like(dw_acc)

        @pl.when(block > 0)
        def _():
            for cp in copies(block - 1):
                cp.wait()

        def step(r, carry):
            base = pl.multiple_of(r * rows_t, rows_t)
            rows = pl.ds(base, rows_t)
            window = dpad_ref[pl.ds(base, rows_t + CONV_PAD), :]
            acc = jnp.zeros((rows_t, LANES), F32)
            for residue in range(SUBLANES):
                moved = _shifted(window, residue, rows_t)
                for tap in range(CONV_WIDTH):
                    off = CONV_WIDTH - 1 - tap
                    if off % SUBLANES == residue:
                        lo = off - residue
                        acc = acc + w_ref[tap:tap + 1, :] * moved[lo:lo + rows_t, :]
            sig = _sigmoid(b_ref[rows, :])
            a = a_ref[rows, :]
            da_ref[rows, :] = (acc * sig).astype(BF16)
            db_ref[rows, :] = (acc * a * sig * (1.0 - sig)).astype(BF16)
            dcur = d_ref[rows, :]
            cwin = pad_ref[pl.ds(base, rows_t + CONV_PAD), :]
            for residue in range(SUBLANES):
                moved = _shifted(cwin, residue, rows_t)
                for tap in range(CONV_WIDTH):
                    if (shift0 + tap) % SUBLANES == residue:
                        lo = (shift0 + tap) - residue
                        prod = dcur * moved[lo:lo + rows_t, :]
                        dw_acc[tap] += jnp.sum(prod.reshape(rows_t // SUBLANES, SUBLANES, LANES), axis=0)
            return carry

        lax.fori_loop(0, seq // rows_t, step, 0)
        dw_ref[...] = jnp.sum(dw_acc[...], axis=1)
        for cp in copies(block):
            cp.start()

        @pl.when(block == blocks - 1)
        def _():
            for cp in copies(block):
                cp.wait()

    col = lambda base: pl.BlockSpec((seq, LANES), lambda j, base=base: (0, base + j))
    own = pl.BlockSpec((seq, LANES), lambda j: (0, j))
    hbm = pl.BlockSpec(memory_space=pl.ANY)
    return pl.pallas_call(
        body, name=name, grid=(blocks,),
        in_specs=[col(col_a), col(col_b), own, pl.BlockSpec((None, CONV_PAD, LANES), lambda j: (layer, 0, j)), hbm],
        out_specs=[hbm, pl.BlockSpec((CONV_PAD, LANES), lambda j: (0, j))],
        out_shape=[jax.ShapeDtypeStruct(du.shape, du.dtype), jax.ShapeDtypeStruct((CONV_PAD, d_conv), F32)],
        scratch_shapes=[pltpu.VMEM((seq + CONV_PAD, LANES), F32), pltpu.VMEM((seq + CONV_PAD, LANES), F32),
                        pltpu.VMEM((CONV_PAD, SUBLANES, LANES), F32), pltpu.VMEM((seq, LANES), BF16), pltpu.VMEM((seq, LANES), BF16),
                        pltpu.SemaphoreType.DMA((2,))],
        input_output_aliases={4: 0}, compiler_params=_cparams(("arbitrary",)),
    )(u, u, dc1, conv_w, du)


MIX_ROWS = 256


def _layer_norm_stats(val):
    mu = jnp.mean(val, axis=-1, keepdims=True)
    cen = val - mu
    var = jnp.mean(cen * cen, axis=-1, keepdims=True)
    rstd = lax.rsqrt(var + LN_EPS)
    return cen * rstd, rstd


def _layer_norm_bwd(dy, xhat, rstd, gain):
    dxhat = dy * gain
    m1 = jnp.mean(dxhat, axis=-1, keepdims=True)
    m2 = jnp.mean(dxhat * xhat, axis=-1, keepdims=True)
    dx = rstd * (dxhat - m1 - xhat * m2)
    return dx, jnp.sum(dy * xhat, axis=0, keepdims=True), jnp.sum(dy, axis=0, keepdims=True)


def _mix_forward(zatt, att, c1, zconv, gatt, gconv, x, w_att, w_conv, w_out, cln_g, cln_b, b_proj):
    s_zatt = _sigmoid(zatt)
    a_in = att * (zatt * s_zatt)
    chat, c_rstd = _layer_norm_stats(c1)
    c2 = chat * cln_g + cln_b
    s_c2 = _sigmoid(c2)
    c3 = c2 * s_c2
    s_zconv = _sigmoid(zconv)
    c_in = c3 * (zconv * s_zconv)
    a_in_b, c_in_b = a_in.astype(BF16), c_in.astype(BF16)
    ab = jnp.dot(a_in_b, w_att, preferred_element_type=F32)
    cb = jnp.dot(c_in_b, w_conv, preferred_element_type=F32) + b_proj
    s_gatt, s_gconv = _sigmoid(gatt), _sigmoid(gconv)
    merged_b = (s_gatt * ab + s_gconv * cb).astype(BF16)
    y = jnp.dot(merged_b, w_out, preferred_element_type=F32)
    h = DEEPNORM_ALPHA * x + y
    return dict(s_zatt=s_zatt, a_in_b=a_in_b, chat=chat, c_rstd=c_rstd, c2=c2, s_c2=s_c2, c3=c3, s_zconv=s_zconv,
                c_in_b=c_in_b, ab=ab, cb=cb, s_gatt=s_gatt, s_gconv=s_gconv, merged_b=merged_b, h=h)


def _u_blocks(rows_t, width, half):
    return [pl.BlockSpec((rows_t, half), lambda i, c=c: (i, c)) for c in (3, 6, 7, 8, 9, 10)]


def _of_layer(arr, layer):
    return pl.BlockSpec((None,) + arr.shape[1:], lambda i: (layer, 0, 0))


def _mix_fwd(u, att, c1, x, w_att, w_conv, w_out, cln_g, cln_b, b_proj, ln_g, ln_b, *, layer, seq, d_model, name):
    half = d_model // 2
    rows_t = min(MIX_ROWS, seq)

    def body(zatt_ref, zconv_ref, ga0, ga1, gc0, gc1, att_ref, c1_ref, x_ref, wa_ref, wc_ref, wo_ref,
             cg_ref, cb_ref, bp_ref, lg_ref, lb_ref, o_ref, ob_ref):
        gatt = jnp.concatenate([ga0[...], ga1[...]], axis=1)
        gconv = jnp.concatenate([gc0[...], gc1[...]], axis=1)
        mid = _mix_forward(zatt_ref[...], att_ref[...], c1_ref[...], zconv_ref[...], gatt, gconv, x_ref[...],
                           wa_ref[...], wc_ref[...], wo_ref[...], cg_ref[...], cb_ref[...], bp_ref[...])
        xhat, _ = _layer_norm_stats(mid["h"])
        out = xhat * lg_ref[...] + lb_ref[...]
        o_ref[...] = out
        ob_ref[...] = out.astype(BF16)

    row = lambda width: pl.BlockSpec((rows_t, width), lambda i: (i, 0))
    full = lambda arr: _of_layer(arr, layer)
    out = pl.BlockSpec((rows_t, d_model), lambda i: (i, 0))
    return pl.pallas_call(
        body, name=name, grid=(seq // rows_t,),
        in_specs=_u_blocks(rows_t, d_model, half) + [row(half), row(half), row(d_model), full(w_att), full(w_conv), full(w_out),
                                                     full(cln_g), full(cln_b), full(b_proj), full(ln_g), full(ln_b)],
        out_specs=[out, out],
        out_shape=[jax.ShapeDtypeStruct((seq, d_model), F32), jax.ShapeDtypeStruct((seq, d_model), BF16)],
        compiler_params=_cparams(("parallel",)),
    )(u, u, u, u, u, u, att, c1, x, w_att, w_conv, w_out, cln_g, cln_b, b_proj, ln_g, ln_b)


def _mix_bwd(u, att, c1, x, dxn, w_att, w_conv, w_out, cln_g, cln_b, b_proj, ln_g, *, layer, seq, d_model, name):
    half = d_model // 2
    rows_t = min(MIX_ROWS, seq)
    steps = seq // rows_t
    nt = ((1,), (1,))
    tn = ((0,), (0,))

    def body(zatt_ref, zconv_ref, ga0, ga1, gc0, gc1, att_ref, c1_ref, x_ref, dxn_ref, wa_ref, wc_ref, wo_ref,
             cg_ref, cb_ref, bp_ref, lg_ref,
             du_ref, datt_ref, dc1_ref, dxres_ref, dwa_ref, dwc_ref, dwo_ref,
             dcg_ref, dcb_ref, dcbias_ref, dbp_ref, dlg_ref, dlb_ref, zatt_stage, tail_stage, out_sem):
        sums = (dwa_ref, dwc_ref, dwo_ref, dcg_ref, dcb_ref, dcbias_ref, dbp_ref, dlg_ref, dlb_ref)
        tile = pl.program_id(0)
        slot = tile % 2
        dzatt_ref, tail_ref = zatt_stage.at[slot], tail_stage.at[slot]
        dzconv_ref, dgatt_ref, dgconv_ref = tail_ref.at[:, :half], tail_ref.at[:, half:3 * half], tail_ref.at[:, 3 * half:]

        def copies(i):
            rows = pl.ds(pl.multiple_of(i * rows_t, rows_t), rows_t)
            return [_columns_copy(zatt_stage.at[i % 2], du_ref, rows, 3 * half, out_sem.at[i % 2, 0]),
                    _columns_copy(tail_stage.at[i % 2], du_ref, rows, 6 * half, out_sem.at[i % 2, 1])]

        @pl.when(tile == 0)
        def _():
            for ref in sums:
                ref[...] = jnp.zeros_like(ref)

        zatt, zconv, att = zatt_ref[...], zconv_ref[...], att_ref[...]
        gatt = jnp.concatenate([ga0[...], ga1[...]], axis=1)
        gconv = jnp.concatenate([gc0[...], gc1[...]], axis=1)
        wa, wc, wo = wa_ref[...], wc_ref[...], wo_ref[...]
        mid = _mix_forward(zatt, att, c1_ref[...], zconv, gatt, gconv, x_ref[...], wa, wc, wo,
                           cg_ref[...], cb_ref[...], bp_ref[...])
        xhat, rstd = _layer_norm_stats(mid["h"])
        dh, dlg, dlb = _layer_norm_bwd(dxn_ref[...], xhat, rstd, lg_ref[...])
        dlg_ref[...] += dlg
        dlb_ref[...] += dlb
        dxres_ref[...] = DEEPNORM_ALPHA * dh
        dy = dh.astype(BF16)
        dwo_ref[...] += lax.dot_general(mid["merged_b"], dy, (tn, ((), ())), preferred_element_type=F32)
        dmerged = lax.dot_general(dy, wo, (nt, ((), ())), preferred_element_type=F32)
        s_ga, s_gc, ab, cb = mid["s_gatt"], mid["s_gconv"], mid["ab"], mid["cb"]
        dgatt_ref[...] = (dmerged * ab * s_ga * (1.0 - s_ga)).astype(BF16)
        dgconv_ref[...] = (dmerged * cb * s_gc * (1.0 - s_gc)).astype(BF16)
        dab = dmerged * s_ga
        dcb = dmerged * s_gc
        dbp_ref[...] += jnp.sum(dcb, axis=0, keepdims=True)
        dab_b, dcb_b = dab.astype(BF16), dcb.astype(BF16)
        dwa_ref[...] += lax.dot_general(mid["a_in_b"], dab_b, (tn, ((), ())), preferred_element_type=F32)
        da_in = lax.dot_general(dab_b, wa, (nt, ((), ())), preferred_element_type=F32)
        s_za = mid["s_zatt"]
        datt_ref[...] = da_in * (zatt * s_za)
        dzatt_ref[...] = (da_in * att * (s_za * (1.0 + zatt * (1.0 - s_za)))).astype(BF16)
        dwc_ref[...] += lax.dot_general(mid["c_in_b"], dcb_b, (tn, ((), ())), preferred_element_type=F32)
        dc_in = lax.dot_general(dcb_b, wc, (nt, ((), ())), preferred_element_type=F32)
        s_zc, c2, s_c2 = mid["s_zconv"], mid["c2"], mid["s_c2"]
        dzconv_ref[...] = (dc_in * mid["c3"] * (s_zc * (1.0 + zconv * (1.0 - s_zc)))).astype(BF16)
        dc3 = dc_in * (zconv * s_zc)
        dc2 = dc3 * (s_c2 * (1.0 + c2 * (1.0 - s_c2)))
        dc1, dcg, dcbeta = _layer_norm_bwd(dc2, mid["chat"], mid["c_rstd"], cg_ref[...])
        dcg_ref[...] += dcg
        dcb_ref[...] += dcbeta
        dcbias_ref[...] += jnp.sum(dc1, axis=0, keepdims=True)
        dc1_ref[...] = dc1
        for cp in copies(tile):
            cp.start()

        @pl.when(tile > 0)
        def _():
            for cp in copies(tile - 1):
                cp.wait()

        @pl.when(tile == steps - 1)
        def _():
            for cp in copies(tile):
                cp.wait()

    row = lambda width: pl.BlockSpec((rows_t, width), lambda i: (i, 0))
    full = lambda arr: _of_layer(arr, layer)
    whole = lambda r, c: pl.BlockSpec((r, c), lambda i: (0, 0))
    sds = jax.ShapeDtypeStruct
    out_specs = [pl.BlockSpec(memory_space=pl.ANY), row(half), row(half), row(d_model),
                 whole(half, d_model), whole(half, d_model), whole(d_model, d_model),
                 whole(1, half), whole(1, half), whole(1, half), whole(1, d_model), whole(1, d_model), whole(1, d_model)]
    out_shape = [sds((seq, u.shape[1]), BF16), sds((seq, half), F32), sds((seq, half), F32), sds((seq, d_model), F32),
                 sds((half, d_model), F32), sds((half, d_model), F32), sds((d_model, d_model), F32),
                 sds((1, half), F32), sds((1, half), F32), sds((1, half), F32),
                 sds((1, d_model), F32), sds((1, d_model), F32), sds((1, d_model), F32)]
    return pl.pallas_call(
        body, name=name, grid=(steps,),
        in_specs=_u_blocks(rows_t, d_model, half) + [row(half), row(half), row(d_model), row(d_model), full(w_att), full(w_conv),
                                                     full(w_out), full(cln_g), full(cln_b), full(b_proj), full(ln_g)],
        out_specs=out_specs, out_shape=out_shape,
        scratch_shapes=[pltpu.VMEM((2, rows_t, half), BF16), pltpu.VMEM((2, rows_t, 5 * half), BF16), pltpu.SemaphoreType.DMA((2, 2))],
        compiler_params=_cparams(("arbitrary",)),
    )(u, u, u, u, u, u, att, c1, x, dxn, w_att, w_conv, w_out, cln_g, cln_b, b_proj, ln_g)


def _loss_head(y, target, *, seq, d_model, name):
    rows_t = min(512, seq)

    def body(y_ref, t_ref, dy_ref, loss_ref):
        @pl.when(pl.program_id(0) == 0)
        def _():
            loss_ref[...] = jnp.zeros_like(loss_ref)

        err = y_ref[...] - t_ref[...]
        dy_ref[...] = err * (1.0 / d_model)
        per_token = jnp.sum(err * err, axis=-1, keepdims=True) * (1.0 / d_model)
        loss_ref[...] += 0.5 * jnp.sum(per_token, axis=0, keepdims=True)

    row = pl.BlockSpec((rows_t, d_model), lambda i: (i, 0))
    return pl.pallas_call(
        body, name=name, grid=(seq // rows_t,), in_specs=[row, row],
        out_specs=[row, pl.BlockSpec((1, 1), lambda i: (0, 0))],
        out_shape=[jax.ShapeDtypeStruct((seq, d_model), F32), jax.ShapeDtypeStruct((1, 1), F32)],
        compiler_params=_cparams(("arbitrary",)),
    )(y, target)


def _adamw(w, g, m, v, *, name, echo=False):
    rows, cols = w.shape
    rows_t = rows
    for cand in (512, 256, 128, 64, 32, 16, 8):
        if rows % cand == 0 and cand * cols * 4 <= 2 * 1024 * 1024:
            rows_t = cand
            break
    n_out = 4 if echo else 3

    def body(w_ref, g_ref, m_ref, v_ref, *outs):
        d_ref, nm_ref, nv_ref = outs[-3:]
        grad = g_ref[...]
        if echo:
            outs[0][...] = grad
        new_m = ADAM_B1 * m_ref[...] + (1.0 - ADAM_B1) * grad
        new_v = ADAM_B2 * v_ref[...] + (1.0 - ADAM_B2) * (grad * grad)
        m_hat = new_m / (1.0 - ADAM_B1 ** ADAM_STEP)
        v_hat = new_v / (1.0 - ADAM_B2 ** ADAM_STEP)
        d_ref[...] = -ADAM_LR * (m_hat / (jnp.sqrt(v_hat) + ADAM_EPS) + ADAM_WD * w_ref[...])
        nm_ref[...] = new_m
        nv_ref[...] = new_v

    blk = pl.BlockSpec((rows_t, cols), lambda i: (i, 0))
    out = jax.ShapeDtypeStruct((rows, cols), F32)
    return pl.pallas_call(
        body, name=name, grid=(rows // rows_t,), in_specs=[blk] * 4, out_specs=[blk] * n_out, out_shape=[out] * n_out,
        compiler_params=_cparams(("parallel",)),
    )(w, g, m, v)


MATRICES = ("w_in", "w_att_proj", "w_conv_proj", "w_out")
VECTORS = ("b_in", "conv_b", "conv_ln_g", "conv_ln_b", "b_conv_proj", "ln_g", "ln_b")
MATRIX_CHIP_AXIS = (1, 1, 1, 0)
MATRIX_CORE_AXIS = (0, 0, 0, 1)


def _chip_sums(layer, depth, partials, got, blocks, place):
    return [_chip_sum_layer(partials[t], got[t], blocks[t], layer, depth, MATRIX_CHIP_AXIS[t], MATRIX_CORE_AXIS[t], place,
                            name="chip_sum_" + MATRICES[t]) for t in range(len(MATRICES))]


def _train_pass(x, target, mats, taps, b_in, conv_b, cln_g, cln_b, b_proj, ln_g, ln_b, place):
    seq, d_model = x.shape
    half = d_model // 2
    depth = b_in.shape[0]
    scan_suffix, scan_prefix = _scan_matrices()
    cols = half // LANES
    dims = dict(seq=seq, d_model=d_model)
    conv_dims = dict(seq=seq, d_conv=half, col_a=4 * cols, col_b=5 * cols)
    axes3 = [axis + 1 for axis in MATRIX_CHIP_AXIS]
    n_mat = len(mats)

    def first_layers(relay):
        return _gather_rider([mats[0], taps], [axes3[0], 2], [(0, 1), (0, depth)], relay)

    mats = list(mats)
    mats[0], taps = _lone_call(_chain(first_layers(False), first_layers(True)), name="gather_first")

    xs, xbs, us, atts, c1s = [x], [x.astype(BF16)], [], [], []
    for l in range(depth):
        u = _matmul(xbs[l], mats[0], layer=l, mode="nn", bias=b_in[l].reshape(1, -1), name="in_proj", tm=256, tn=b_in.shape[1],
                    tk=d_model)
        nxt = [(l + 1, 1)] + [(l + 1, 1) if l else (0, 2)] * (n_mat - 1)
        if l + 1 < depth:
            att, mats = _attn_fwd(u, scan_suffix, seq=seq, d_att=half, name="attn_fwd", rider=_gather_rider(mats, axes3, nxt, False))
            c1, mats = _conv_fwd(u, taps, conv_b, layer=l, name="conv_fwd", rider=_gather_rider(mats, axes3, nxt, True), **conv_dims)
        else:
            att = _attn_fwd(u, scan_suffix, seq=seq, d_att=half, name="attn_fwd_last")
            c1 = _conv_fwd(u, taps, conv_b, layer=l, name="conv_fwd_last", **conv_dims)
        xn, xnb = _mix_fwd(u, att, c1, xs[l], mats[1], mats[2], mats[3], cln_g, cln_b, b_proj, ln_g, ln_b, layer=l, name="mix_fwd", **dims)
        us.append(u)
        atts.append(att)
        c1s.append(c1)
        xs.append(xn)
        xbs.append(xnb)
    w_in, w_att, w_conv, w_out = mats

    dx, loss = _loss_head(xs[depth], target, name="loss_head", **dims)
    grads = [None] * depth
    blocks = [None] * n_mat
    waiting = None
    for l in reversed(range(depth)):
        u = us[l]
        (du, datt, dc1, dxres, dwa, dwc, dwo, dcg, dcb, dcbias, dbp, dlg, dlb) = _mix_bwd(
            u, atts[l], c1s[l], xs[l], dx, w_att, w_conv, w_out, cln_g, cln_b, b_proj, ln_g, layer=l, name="mix_bwd", **dims)
        if waiting is None:
            du = _attn_bwd(u, datt, du, scan_suffix, scan_prefix, seq=seq, d_att=half, name="attn_bwd_first")
        else:
            du, got = _attn_bwd(u, datt, du, scan_suffix, scan_prefix, seq=seq, d_att=half, name="attn_bwd",
                                rider=_exchange_rider(waiting[1], MATRIX_CHIP_AXIS))
            blocks = _chip_sums(waiting[0], depth, waiting[1], got, blocks, place)
        du, dconvw = _conv_bwd(u, dc1, du, taps, layer=l, name="conv_bwd", **conv_dims)
        dwin, dbin = _matmul(xbs[l], du, mode="tn", colsum=True, name="in_proj_dw", tm=1024, tn=512, tk=seq)
        parts = [dwin, dwa, dwc, dwo]
        swap = _swap_rider(parts, MATRIX_CORE_AXIS)
        if l == 0:
            swap = _both(swap, _join_rider(blocks, [axis + 1 for axis in MATRIX_CORE_AXIS], 1, depth - 1))
        dx, rode = _matmul(du, w_in, layer=l, mode="nt", add=dxres, name="in_proj_dx", tm=512, tn=1024, tk=du.shape[1], rider=swap)
        got, blocks = rode[:n_mat], (rode[n_mat:] if l == 0 else blocks)
        waiting = (l, [_pair_sum_layer(parts[t], got[t], MATRIX_CORE_AXIS[t], place, name="pair_sum_" + MATRICES[t])
                       for t in range(n_mat)])
        grads[l] = dict(b_in=dbin[0], conv_w=dconvw[:CONV_WIDTH], conv_b=dcbias[0], conv_ln_g=dcg[0], conv_ln_b=dcb[0],
                        b_conv_proj=dbp[0], ln_g=dlg[0], ln_b=dlb[0])
    packed = jnp.stack([jnp.concatenate([grads[l][n] for n in VECTORS] + [grads[l]["conv_w"].reshape(-1)]) for l in range(depth)])
    return loss, dx, packed, waiting[1], blocks


MESH = pl.DeviceIdType.MESH


def _position():
    x, y, c = lax.axis_index("x"), lax.axis_index("y"), lax.axis_index("c")
    return x, y, c, [(1 - x, y), (x, 1 - y), (1 - x, 1 - y)]


def _cut(ref, axis, start, size):
    idx = [slice(None)] * len(ref.shape)
    idx[axis] = pl.ds(start, size)
    return ref.at[tuple(idx)]


def _remote(src, dst, send_sem, recv_sem, device):
    return pltpu.make_async_remote_copy(src_ref=src, dst_ref=dst, send_sem=send_sem, recv_sem=recv_sem,
                                        device_id=device, device_id_type=MESH)


def _gather_rider(wholes, chip_axes, spans, relay):
    n = len(wholes)

    def region(dst, t, chip, half):
        first, count = spans[t]
        _, rows, cols = wholes[t].shape
        if chip_axes[t] == 2:
            size, part = cols // N_CHIPS, rows // 2
            ref = _cut(_cut(dst[t], 2, pl.multiple_of(chip * size, size), size), 1, pl.multiple_of(half * part, part), part)
        else:
            size = rows // N_CHIPS
            part = size // 2
            ref = _cut(dst[t], 1, pl.multiple_of(chip * size + half * part, part), part)
        return _cut(ref, 0, first, count)

    def copies(dst, sems, receiving):
        x, y, c, chips = _position()
        send_sem, recv_sem = sems
        out = []
        for t in range(n):
            for j, chip in enumerate(chips):
                theirs = 2 * chip[0] + chip[1]
                if relay:
                    ref = region(dst, t, theirs, 1 - c if receiving else c)
                    peer = (x, y, 1 - c)
                else:
                    ref = region(dst, t, theirs if receiving else 2 * x + y, c)
                    peer = (*chip, c)
                out.append(_remote(ref, ref, send_sem.at[t, j], recv_sem.at[t, j], peer))
        return out

    def start(ins, outs, sems):
        for cp in copies(outs, sems, False):
            cp.start()

    def finish(ins, outs, sems):
        for cp in copies(outs, sems, True):
            cp.wait_recv()
        for cp in copies(outs, sems, False):
            cp.wait_send()

    return _Rider(operands=list(wholes), out_shape=[jax.ShapeDtypeStruct(w.shape, w.dtype) for w in wholes],
                  aliases={t: t for t in range(n)}, scratch=[pltpu.SemaphoreType.DMA((n, N_CHIPS - 1))] * 2,
                  start=start, finish=finish)


def _chain(first, second):
    cut = len(first.scratch)

    def start(ins, outs, sems):
        first.start(ins, outs, sems[:cut])
        first.finish(ins, outs, sems[:cut])
        second.start(ins, outs, sems[cut:])

    def finish(ins, outs, sems):
        second.finish(ins, outs, sems[cut:])

    return first._replace(scratch=list(first.scratch) + list(second.scratch), start=start, finish=finish)


def _swap_rider(parts, core_axes):
    n = len(parts)
    halves = []
    for arr, axis in zip(parts, core_axes):
        shape = list(arr.shape)
        shape[axis] //= 2
        halves.append(jax.ShapeDtypeStruct(tuple(shape), arr.dtype))

    def copies(ins, outs, sems):
        x, y, c, _ = _position()
        out = []
        for t in range(n):
            size = halves[t].shape[core_axes[t]]
            piece = _cut(ins[t], core_axes[t], pl.multiple_of((1 - c) * size, size), size)
            out.append(_remote(piece, outs[t], sems[0].at[t], sems[1].at[t], (x, y, 1 - c)))
        return out

    def start(ins, outs, sems):
        for cp in copies(ins, outs, sems):
            cp.start()

    def finish(ins, outs, sems):
        for cp in copies(ins, outs, sems):
            cp.wait()

    return _Rider(operands=list(parts), out_shape=halves, aliases={}, scratch=[pltpu.SemaphoreType.DMA((n,))] * 2,
                  start=start, finish=finish)


def _exchange_rider(partials, chip_axes):
    n = len(partials)
    quarters = []
    for arr, axis in zip(partials, chip_axes):
        shape = list(arr.shape)
        shape[axis] //= N_CHIPS
        quarters.append(jax.ShapeDtypeStruct((N_CHIPS - 1, *shape), arr.dtype))

    def copies(ins, outs, sems):
        x, y, c, chips = _position()
        out = []
        for t in range(n):
            size = quarters[t].shape[1 + chip_axes[t]]
            for j, chip in enumerate(chips):
                piece = _cut(ins[t], chip_axes[t], pl.multiple_of((2 * chip[0] + chip[1]) * size, size), size)
                out.append(_remote(piece, outs[t].at[j], sems[0].at[t, j], sems[1].at[t, j], (*chip, c)))
        return out

    def start(ins, outs, sems):
        for cp in copies(ins, outs, sems):
            cp.start()

    def finish(ins, outs, sems):
        for cp in copies(ins, outs, sems):
            cp.wait()

    return _Rider(operands=list(partials), out_shape=quarters, aliases={}, scratch=[pltpu.SemaphoreType.DMA((n, N_CHIPS - 1))] * 2,
                  start=start, finish=finish)


def _place_block(shard, chip_axis, place, dtype, *, name):
    depth, rows, cols = shard.shape
    rows_t = _row_tile(rows, cols * 4 * 4, 16 * 1024 * 1024)
    steps = rows // rows_t
    shape = list(shard.shape)
    shape[chip_axis] *= N_CHIPS
    if chip_axis == 1:
        out_spec = pl.BlockSpec((None, rows_t, cols), lambda l, i, p: (l, p[1] * steps + i, 0))
    else:
        out_spec = pl.BlockSpec((None, rows_t, cols), lambda l, i, p: (l, i, p[1]))

    def body(place_ref, src_ref, out_ref):
        out_ref[...] = src_ref[...].astype(dtype)

    return pl.pallas_call(
        body, name=name, out_shape=jax.ShapeDtypeStruct(tuple(shape), dtype),
        grid_spec=pltpu.PrefetchScalarGridSpec(num_scalar_prefetch=1, grid=(depth, steps),
                                               in_specs=[pl.BlockSpec((None, rows_t, cols), lambda l, i, p: (l, i, 0))],
                                               out_specs=out_spec),
        compiler_params=_cparams(("arbitrary", "arbitrary")),
    )(place, shard)


def _join_rider(blocks, core_axes, first, count):
    n = len(blocks)

    def copies(outs, sems, receiving):
        x, y, c, _ = _position()
        out = []
        for t in range(n):
            size = blocks[t].shape[core_axes[t]] // 2
            half = 1 - c if receiving else c
            ref = _cut(_cut(outs[t], core_axes[t], pl.multiple_of(half * size, size), size), 0, first, count)
            out.append(_remote(ref, ref, sems[0].at[t], sems[1].at[t], (x, y, 1 - c)))
        return out

    def start(ins, outs, sems):
        for cp in copies(outs, sems, False):
            cp.start()

    def finish(ins, outs, sems):
        for cp in copies(outs, sems, True):
            cp.wait_recv()
        for cp in copies(outs, sems, False):
            cp.wait_send()

    return _Rider(operands=list(blocks), out_shape=[jax.ShapeDtypeStruct(b.shape, b.dtype) for b in blocks],
                  aliases={t: t for t in range(n)}, scratch=[pltpu.SemaphoreType.DMA((n,))] * 2, start=start, finish=finish)


def _small_rider(vec):
    n_dev = 2 * N_CHIPS

    def copies(ins, outs, sems, receiving):
        x, y, c, _ = _position()
        flip = lambda v, bit: 1 - v if bit else v
        out = []
        for mask in range(1, n_dev):
            peer = (flip(x, mask & 4), flip(y, mask & 2), flip(c, mask & 1))
            row = 4 * peer[0] + 2 * peer[1] + peer[2] if receiving else 4 * x + 2 * y + c
            out.append(_remote(ins[0], outs[0].at[row], sems[0].at[mask - 1], sems[1].at[mask - 1], peer))
        return out

    def own(ins, outs, sems):
        x, y, c, _ = _position()
        return pltpu.make_async_copy(ins[0], outs[0].at[4 * x + 2 * y + c], sems[2])

    def start(ins, outs, sems):
        own(ins, outs, sems).start()
        for cp in copies(ins, outs, sems, False):
            cp.start()

    def finish(ins, outs, sems):
        for cp in copies(ins, outs, sems, True):
            cp.wait_recv()
        for cp in copies(ins, outs, sems, False):
            cp.wait_send()
        own(ins, outs, sems).wait()

    return _Rider(operands=[vec], out_shape=[jax.ShapeDtypeStruct((n_dev, *vec.shape), vec.dtype)], aliases={},
                  scratch=[pltpu.SemaphoreType.DMA((n_dev - 1,))] * 2 + [pltpu.SemaphoreType.DMA(())], start=start, finish=finish)


def _both(first, second):
    n_in, n_out, n_sem = len(first.operands), len(first.out_shape), len(first.scratch)

    def start(ins, outs, sems):
        first.start(ins[:n_in], outs[:n_out], sems[:n_sem])
        second.start(ins[n_in:], outs[n_out:], sems[n_sem:])

    def finish(ins, outs, sems):
        first.finish(ins[:n_in], outs[:n_out], sems[:n_sem])
        second.finish(ins[n_in:], outs[n_out:], sems[n_sem:])

    aliases = dict(first.aliases)
    aliases.update({n_in + i: n_out + o for i, o in second.aliases.items()})
    return _Rider(operands=list(first.operands) + list(second.operands), out_shape=list(first.out_shape) + list(second.out_shape),
                  aliases=aliases, scratch=list(first.scratch) + list(second.scratch), start=start, finish=finish)


def _row_tile(rows, row_bytes, budget):
    tile = rows
    for cand in (512, 256, 128, 64, 32, 16, 8):
        if rows % cand == 0:
            tile = cand
            if cand * row_bytes <= budget:
                break
    return tile


def _pair_sum_layer(part, got, core_axis, place, *, name):
    rows, cols = got.shape
    rows_t = _row_tile(rows, cols * 4 * 6, 16 * 1024 * 1024)
    steps = rows // rows_t
    if core_axis == 0:
        part_spec = pl.BlockSpec((rows_t, cols), lambda i, p: (p[0] * steps + i, 0))
    else:
        part_spec = pl.BlockSpec((rows_t, cols), lambda i, p: (i, p[0]))
    own_spec = pl.BlockSpec((rows_t, cols), lambda i, p: (i, 0))

    def body(place_ref, part_ref, got_ref, out_ref):
        out_ref[...] = (part_ref[...] + got_ref[...]).astype(BF16)

    return pl.pallas_call(
        body, name=name, out_shape=jax.ShapeDtypeStruct(got.shape, BF16),
        grid_spec=pltpu.PrefetchScalarGridSpec(num_scalar_prefetch=1, grid=(steps,), in_specs=[part_spec, own_spec], out_specs=own_spec),
        compiler_params=_cparams(("arbitrary",)),
    )(place, part, got)


def _chip_sum_layer(partial, got, blocks, layer, depth, chip_axis, core_axis, place, *, name):
    _, rows, cols = got.shape
    rows_t = _row_tile(rows, cols * 4 * 10, 24 * 1024 * 1024)
    steps = rows // rows_t
    if chip_axis == 0:
        own_spec = pl.BlockSpec((rows_t, cols), lambda i, p: (p[1] * steps + i, 0))
    else:
        own_spec = pl.BlockSpec((rows_t, cols), lambda i, p: (i, p[1]))
    got_spec = pl.BlockSpec((N_CHIPS - 1, rows_t, cols), lambda i, p: (0, i, 0))
    shape = [depth, rows, cols]
    shape[1 + core_axis] *= 2
    if core_axis == 0:
        out_spec = pl.BlockSpec((None, rows_t, cols), lambda i, p: (layer, p[0] * steps + i, 0))
    else:
        out_spec = pl.BlockSpec((None, rows_t, cols), lambda i, p: (layer, i, p[0]))

    def body(place_ref, own_ref, got_ref, *rest):
        out_ref = rest[-1]
        up = lambda val: val.astype(F32)
        out_ref[...] = ((up(own_ref[...]) + up(got_ref[0])) + up(got_ref[1])) + up(got_ref[2])

    in_specs, operands, aliases = [own_spec, got_spec], [place, partial, got], {}
    if blocks is not None:
        in_specs.append(pl.BlockSpec(memory_space=pl.ANY))
        operands.append(blocks)
        aliases = {3: 0}
    return pl.pallas_call(
        body, name=name, out_shape=jax.ShapeDtypeStruct(tuple(shape), F32),
        grid_spec=pltpu.PrefetchScalarGridSpec(num_scalar_prefetch=1, grid=(steps,), in_specs=in_specs, out_specs=out_spec),
        input_output_aliases=aliases, compiler_params=_cparams(("arbitrary",)),
    )(*operands)


def _sum_devices(stack, *, name):
    def body(src_ref, out_ref):
        total = src_ref[0]
        for d in range(1, stack.shape[0]):
            total = total + src_ref[d]
        out_ref[...] = total

    return pl.pallas_call(body, name=name, out_shape=jax.ShapeDtypeStruct(stack.shape[1:], F32))(stack)


def kernel(x, w_in, b_in, conv_w, conv_b, conv_ln_g, conv_ln_b, w_att_proj, w_conv_proj, b_conv_proj, w_out, ln_g, ln_b, loss_target, m_w_in, m_b_in, m_conv_w, m_conv_b, m_conv_ln_g, m_conv_ln_b, m_w_att_proj, m_w_conv_proj, m_b_conv_proj, m_w_out, m_ln_g, m_ln_b, v_w_in, v_b_in, v_conv_w, v_conv_b, v_conv_ln_g, v_conv_ln_b, v_w_att_proj, v_w_conv_proj, v_b_conv_proj, v_w_out, v_ln_g, v_ln_b):
    depth = w_in.shape[0]
    d_model = x.shape[-1]
    half = d_model // 2
    chip = 2 * lax.axis_index("x") + lax.axis_index("y")
    place = jnp.stack([lax.axis_index("c"), chip]).astype(jnp.int32)
    vec3 = lambda v: v.reshape(depth, 1, -1)

    taps = jnp.pad(conv_w, ((0, 0), (0, CONV_PAD - CONV_WIDTH), (0, 0)))
    gathered = [("w_in", w_in, 2, BF16), ("w_att_proj", w_att_proj, 2, BF16), ("w_conv_proj", w_conv_proj, 2, BF16),
                ("w_out", w_out, 1, BF16), ("conv_w", taps, 2, F32)]
    wholes = [_place_block(arr, axis, place, dtype, name="place_" + n) for n, arr, axis, dtype in gathered]
    loss, grad_x, packed, waiting, blocks = _train_pass(x[0], loss_target[0], wholes[:4], wholes[4], b_in, vec3(conv_b),
                                                        vec3(conv_ln_g), vec3(conv_ln_b), vec3(b_conv_proj), vec3(ln_g), vec3(ln_b), place)
    loss = lax.psum(loss[0, 0], ("x", "y", "c"))

    names = ["w_in", "b_in", "conv_w", "conv_b", "conv_ln_g", "conv_ln_b", "w_att_proj", "w_conv_proj", "b_conv_proj", "w_out", "ln_g", "ln_b"]
    weights = dict(zip(names, (w_in, b_in, conv_w, conv_b, conv_ln_g, conv_ln_b, w_att_proj, w_conv_proj, b_conv_proj, w_out, ln_g, ln_b)))
    first = dict(zip(names, (m_w_in, m_b_in, m_conv_w, m_conv_b, m_conv_ln_g, m_conv_ln_b, m_w_att_proj, m_w_conv_proj, m_b_conv_proj, m_w_out, m_ln_g, m_ln_b)))
    second = dict(zip(names, (v_w_in, v_b_in, v_conv_w, v_conv_b, v_conv_ln_g, v_conv_ln_b, v_w_att_proj, v_w_conv_proj, v_b_conv_proj, v_w_out, v_ln_g, v_ln_b)))
    flat = lambda arr: arr.reshape(-1, arr.shape[-1])

    gathered, *got = _lone_call(_both(_small_rider(packed), _exchange_rider(waiting, MATRIX_CHIP_AXIS)), name="exchange_last")
    blocks = _lone_call(_join_rider(_chip_sums(0, depth, waiting, got, blocks, place), [axis + 1 for axis in MATRIX_CORE_AXIS], 0, 1),
                        name="pair_join")
    reduced = dict(zip(MATRICES, blocks))

    total = _sum_devices(gathered, name="sum_devices")
    widths = dict(b_in=b_in.shape[1], conv_b=half, conv_ln_g=half, conv_ln_b=half, b_conv_proj=d_model, ln_g=d_model, ln_b=d_model)
    offset = 0
    for n in VECTORS:
        reduced[n] = total[:, offset:offset + widths[n]]
        offset += widths[n]
    taps = total[:, offset:].reshape(depth, CONV_WIDTH, half)
    reduced["conv_w"] = lax.dynamic_slice_in_dim(taps, chip * conv_w.shape[2], conv_w.shape[2], axis=2)

    delta, new_m, new_v = {}, {}, {}
    for n in names:
        shape = weights[n].shape
        *grad, d, m, v = _adamw(flat(weights[n]), flat(reduced[n]), flat(first[n]), flat(second[n]), name="adamw_" + n,
                                echo=n in MATRICES)
        if grad:
            reduced[n] = grad[0]
        delta[n], new_m[n], new_v[n] = d.reshape(shape), m.reshape(shape), v.reshape(shape)
    return (loss, grad_x[None], *[reduced[n].reshape(weights[n].shape) for n in names], *[delta[n] for n in names],
            *[new_m[n] for n in names], *[new_v[n] for n in names])
```

```python
import functools
from typing import Callable, NamedTuple

import jax
import jax.numpy as jnp
from jax import lax
from jax.experimental import pallas as pl
from jax.experimental.pallas import tpu as pltpu

F32 = jnp.float32
BF16 = jnp.bfloat16

HEAD_DIM = 64
LANES = 128
CONV_WIDTH = 31
CONV_PAD = 32
SUBLANES = 8
LN_EPS = 1e-5
DEPTH = 4
DEEPNORM_ALPHA = (2 * DEPTH) ** 0.25
ATT_SCALE = HEAD_DIM ** -0.5
ATT_TILE = 128
ATT_DEAD = -88.0
ATT_GROUP = 2
ATT_GROUP_FWD = 4
ATT_FILL = -1e30

ADAM_LR = 0.001
ADAM_B1 = 0.9
ADAM_B2 = 0.999
ADAM_EPS = 1e-08
ADAM_WD = 0.01
ADAM_STEP = 10

VMEM_LIMIT = 56 * 1024 * 1024

N_CHIPS = 4


def _cparams(sem):
    return pltpu.CompilerParams(dimension_semantics=sem, vmem_limit_bytes=VMEM_LIMIT)


def _sigmoid(x):
    return 1.0 / (1.0 + jnp.exp(-x))


class _Rider(NamedTuple):
    operands: list
    out_shape: list
    aliases: dict
    scratch: list
    start: Callable
    finish: Callable


def _host_call(body, rider, *, name, grid, in_specs, out_specs, out_shape, scratch, operands, semantics, host_aliases=None):
    n_in, n_out = len(in_specs), len(out_specs)
    aliases = dict(host_aliases or {})
    if rider is not None:
        r_in, r_out = len(rider.operands), len(rider.out_shape)
        host_body = body

        def body(*refs):
            base = n_in + r_in
            ins, rins = refs[:n_in], refs[n_in:base]
            outs, routs = refs[base:base + n_out], refs[base + n_out:base + n_out + r_out]
            rest = refs[base + n_out + r_out:]
            split = len(rest) - len(rider.scratch)
            ids = [pl.program_id(axis) for axis in range(len(grid))]
            first = functools.reduce(jnp.logical_and, [i == 0 for i in ids])
            last = functools.reduce(jnp.logical_and, [i == g - 1 for i, g in zip(ids, grid)])

            @pl.when(first)
            def _():
                rider.start(rins, routs, rest[split:])

            host_body(*ins, *outs, *rest[:split])

            @pl.when(last)
            def _():
                rider.finish(rins, routs, rest[split:])

        hbm = pl.BlockSpec(memory_space=pl.ANY)
        in_specs = list(in_specs) + [hbm] * r_in
        out_specs = list(out_specs) + [hbm] * r_out
        out_shape = list(out_shape) + list(rider.out_shape)
        scratch = list(scratch) + list(rider.scratch)
        operands = list(operands) + list(rider.operands)
        aliases.update({n_in + i: n_out + o for i, o in rider.aliases.items()})
    res = pl.pallas_call(
        body, name=name, grid=grid, in_specs=list(in_specs), out_specs=list(out_specs), out_shape=list(out_shape),
        scratch_shapes=list(scratch), input_output_aliases=aliases, compiler_params=_cparams(semantics),
    )(*operands)
    return list(res[:n_out]), list(res[n_out:])


def _lone_call(rider, *, name):
    r_in = len(rider.operands)

    def body(*refs):
        ins, outs, sems = refs[:r_in], refs[r_in:r_in + len(rider.out_shape)], refs[r_in + len(rider.out_shape):]
        rider.start(ins, outs, sems)
        rider.finish(ins, outs, sems)

    hbm = pl.BlockSpec(memory_space=pl.ANY)
    return list(pl.pallas_call(
        body, name=name, in_specs=[hbm] * r_in, out_specs=[hbm] * len(rider.out_shape), out_shape=list(rider.out_shape),
        scratch_shapes=list(rider.scratch), input_output_aliases=dict(rider.aliases),
    )(*rider.operands))


def _fit(tile, dim):
    assert dim % LANES == 0
    tile = min(tile, dim) // LANES * LANES
    while dim % tile:
        tile -= LANES
    return tile


_DIMS = {"nn": ((1,), (0,)), "nt": ((1,), (1,)), "tn": ((0,), (0,))}


def _matmul(a, b, *, mode, name, layer=None, bias=None, add=None, colsum=False, out_dtype=F32, tm=1024, tn=512, tk=1024,
            rider=None):
    b_shape = b.shape if layer is None else b.shape[1:]
    if mode == "nn":
        (m, k), (k2, n) = a.shape, b_shape
    elif mode == "nt":
        (m, k), (n, k2) = a.shape, b_shape
    else:
        (k, m), (k2, n) = a.shape, b_shape
    assert k == k2
    tm, tn, tk = _fit(tm, m), _fit(tn, n), _fit(tk, k)
    gm, gn, nk = m // tm, n // tn, k // tk

    a_spec = pl.BlockSpec((tk, tm), lambda i, j, kk: (kk, i)) if mode == "tn" else pl.BlockSpec((tm, tk), lambda i, j, kk: (i, kk))
    if layer is None:
        b_spec = pl.BlockSpec((tn, tk), lambda i, j, kk: (j, kk)) if mode == "nt" else pl.BlockSpec((tk, tn), lambda i, j, kk: (kk, j))
    elif mode == "nt":
        b_spec = pl.BlockSpec((None, tn, tk), lambda i, j, kk: (layer, j, kk))
    else:
        b_spec = pl.BlockSpec((None, tk, tn), lambda i, j, kk: (layer, kk, j))
    in_specs, operands = [a_spec, b_spec], [a, b]
    if bias is not None:
        in_specs.append(pl.BlockSpec((1, tn), lambda i, j, kk: (0, j)))
        operands.append(bias)
    if add is not None:
        in_specs.append(pl.BlockSpec((tm, tn), lambda i, j, kk: (i, j)))
        operands.append(add)
    out_shape = [jax.ShapeDtypeStruct((m, n), out_dtype)]
    out_specs = [pl.BlockSpec((tm, tn), lambda i, j, kk: (i, j))]
    scratch = [pltpu.VMEM((tm, tn), F32)] if nk > 1 else []
    if colsum:
        assert mode == "tn"
        out_shape.append(jax.ShapeDtypeStruct((gm, 1, n), F32))
        out_specs.append(pl.BlockSpec((1, 1, tn), lambda i, j, kk: (i, 0, j)))
        if nk > 1:
            scratch.append(pltpu.VMEM((1, tn), F32))
    has_bias, has_add = bias is not None, add is not None

    def body(*refs):
        refs = list(refs)
        a_ref, b_ref = refs[0], refs[1]
        pos = 2
        bias_ref = add_ref = None
        if has_bias:
            bias_ref = refs[pos]
            pos += 1
        if has_add:
            add_ref = refs[pos]
            pos += 1
        o_ref = refs[pos]
        pos += 1
        cs_ref = None
        if colsum:
            cs_ref = refs[pos]
            pos += 1

        def finish(out, sums):
            if has_bias:
                out = out + bias_ref[...]
            if has_add:
                out = out + add_ref[...]
            o_ref[...] = out.astype(out_dtype)
            if colsum:
                cs_ref[0] = sums

        bv = b_ref[...]
        prod = lax.dot_general(a_ref[...].astype(BF16), bv.astype(BF16), (_DIMS[mode], ((), ())), preferred_element_type=F32)
        sums = jnp.sum(bv.astype(F32), axis=0, keepdims=True) if colsum else None
        if nk == 1:
            finish(prod, sums)
            return
        acc_ref = refs[pos]
        cs_acc = refs[pos + 1] if colsum else None
        kk = pl.program_id(2)

        @pl.when(kk == 0)
        def _():
            acc_ref[...] = jnp.zeros_like(acc_ref)
            if colsum:
                cs_acc[...] = jnp.zeros_like(cs_acc)

        acc_ref[...] += prod
        if colsum:
            cs_acc[...] += sums

        @pl.when(kk == nk - 1)
        def _():
            finish(acc_ref[...], cs_acc[...] if colsum else None)

    res, rode = _host_call(body, rider, name=name, grid=(gm, gn, nk), in_specs=in_specs, out_specs=out_specs, out_shape=out_shape,
                           scratch=scratch, operands=operands, semantics=("arbitrary", "arbitrary", "arbitrary"))
    out = (res[0], res[1][0]) if colsum else res[0]
    return out if rider is None else (out, rode)


def _scan_matrices():
    t = ATT_TILE
    r = lax.broadcasted_iota(jnp.int32, (t, t), 0)
    c = lax.broadcasted_iota(jnp.int32, (t, t), 1)
    ones = jnp.ones((t, t), F32)
    suffix = jnp.concatenate([(r > c).astype(F32), ones], axis=1)
    prefix = jnp.concatenate([(r < c).astype(F32), ones], axis=1)
    stack = lambda mat: jnp.concatenate([mat, mat], axis=0).astype(BF16)
    return stack(suffix), stack(prefix)


def _split_halves(val):
    hi = val.astype(BF16)
    lo = (val - hi.astype(F32)).astype(BF16)
    return jnp.concatenate([hi, lo], axis=1)


def _split_scan(val, mat_ref):
    return jnp.dot(_split_halves(val), mat_ref[...], preferred_element_type=F32)


def _pair_scores(q, k_lo, k_hi, masked):
    t = ATT_TILE
    z2 = lax.dot_general(q, jnp.concatenate([k_lo, k_hi], axis=0), (((1,), (1,)), ((), ())), preferred_element_type=F32)
    out = []
    for h in range(2):
        z = z2[:, h * t:(h + 1) * t]
        sp = jnp.log(1.0 + jnp.exp(-jnp.abs(z)))
        f = jnp.minimum(-z, 0.0) - sp
        a = f + z
        if masked:
            causal = lax.broadcasted_iota(jnp.int32, (t, t), 1) < lax.broadcasted_iota(jnp.int32, (t, t), 0)
            f = jnp.where(causal, f, 0.0)
        out.append((_split_halves(f), a))
    return out


def _any_alive(runs):
    top = functools.reduce(jnp.maximum, [run for per_head in runs for run in per_head])
    return (jnp.max(top) > ATT_DEAD).astype(jnp.int32)


def _head_copies(seq, src_ref, scale, lo_ref, hi_ref, plain_ref):
    chunk = min(256, seq)
    low = lax.broadcasted_iota(jnp.int32, (chunk, LANES), 1) < HEAD_DIM

    def step(r, carry):
        rows = pl.ds(pl.multiple_of(r * chunk, chunk), chunk)
        val = src_ref[rows, :]
        if scale != 1.0:
            val = val * scale
        if lo_ref is not None:
            lo_ref[rows, :] = jnp.where(low, val, 0.0).astype(BF16)
            hi_ref[rows, :] = jnp.where(low, 0.0, val).astype(BF16)
        if plain_ref is not None:
            plain_ref[rows, :] = val.astype(BF16)
        return carry

    lax.fori_loop(0, seq // chunk, step, 0)


def _attn_fwd(u, scan_suffix, *, seq, d_att, name, rider=None):
    t = ATT_TILE
    nq = seq // t
    pairs = d_att // LANES
    grp = ATT_GROUP_FWD
    assert nq % grp == 0

    def body(q_ref, k_ref, v_ref, um_ref, o_ref, qq, k0, k1, v0, v1, f2_s, a_s, lg_s, tot_s, run_s, acc_s):
        _head_copies(seq, q_ref, ATT_SCALE, None, None, qq)
        _head_copies(seq, k_ref, 1.0, k0, k1, None)
        _head_copies(seq, v_ref, 1.0, v0, v1, None)

        def group(gi, carry):
            qb0 = gi * grp
            qrows = [pl.ds(pl.multiple_of((qb0 + g) * t, t), t) for g in range(grp)]
            qv = [qq[qrows[g], :] for g in range(grp)]

            chains = [(h, g) for g in range(grp) for h in range(2)]

            def key_rows(g, i):
                return pl.ds(pl.multiple_of(jnp.maximum(qb0 + g - i, 0) * t, t), t)

            def stage1(i, masked):
                out = []
                for g in range(grp):
                    krows = key_rows(g, i)
                    out += _pair_scores(qv[g], k0[krows, :], k1[krows, :], masked)
                return out

            def stage2(halves, a, masked):
                scan = jnp.dot(halves, um_ref[...], preferred_element_type=F32)
                logit = a + scan[:, :t]
                if masked:
                    causal = lax.broadcasted_iota(jnp.int32, (t, t), 1) < lax.broadcasted_iota(jnp.int32, (t, t), 0)
                    logit = jnp.where(causal, logit, ATT_FILL)
                return logit, scan[:, t:]

            def put(halves_a=None, logit_total=None):
                for c in range(len(chains)):
                    if halves_a is not None:
                        f2_s[c], a_s[c] = halves_a[c]
                    if logit_total is not None:
                        lg_s[c], tot_s[c] = logit_total[c]

            first = stage1(0, True)
            put(halves_a=stage1(1, False), logit_total=[stage2(f2, a, True) for f2, a in first])
            for c in range(len(chains)):
                run_s[c] = jnp.zeros((t, t), F32)
            for g in range(grp):
                acc_s[g] = jnp.zeros((t, LANES), F32)

            def wbody(st):
                i = st[0]
                held = [(f2_s[c], a_s[c]) for c in range(len(chains))]
                logits = [lg_s[c] for c in range(len(chains))]
                totals = [tot_s[c] for c in range(len(chains))]
                runs = [run_s[c] for c in range(len(chains))]
                accs = [acc_s[g] for g in range(grp)]
                for g in range(grp):
                    krows = key_rows(g, i)
                    gone = jnp.where(qb0 + g - i >= 0, 0.0, ATT_FILL)
                    weights = []
                    for c in (2 * g, 2 * g + 1):
                        run = runs[c] + gone
                        weights.append(jnp.exp(logits[c] + run).astype(BF16))
                        runs[c] = run + totals[c]
                    accs[g] = accs[g] + jnp.dot(jnp.concatenate(weights, axis=1), jnp.concatenate([v0[krows, :], v1[krows, :]], axis=0),
                                                preferred_element_type=F32)
                ahead2 = [stage2(f2, a, False) for f2, a in held]
                ahead1 = stage1(i + 2, False)
                put(halves_a=ahead1, logit_total=ahead2)
                for c in range(len(chains)):
                    run_s[c] = runs[c]
                for g in range(grp):
                    acc_s[g] = accs[g]
                more = jnp.logical_and(i + 1 <= qb0 + grp - 1, _any_alive([runs]) > 0)
                return i + 1, more.astype(jnp.int32)

            lax.while_loop(lambda st: st[1] > 0, wbody, (jnp.int32(0), jnp.int32(1)))
            for g in range(grp):
                o_ref[qrows[g], :] = acc_s[g]
            return carry

        lax.fori_loop(0, nq // grp, group, 0)

    blk = lambda base: pl.BlockSpec((seq, LANES), lambda j, base=base: (0, base + j))
    res, rode = _host_call(
        body, rider, name=name, grid=(pairs,),
        in_specs=[blk(0), blk(pairs), blk(2 * pairs), pl.BlockSpec((2 * t, 2 * t), lambda j: (0, 0))],
        out_specs=[pl.BlockSpec((seq, LANES), lambda j: (0, j))],
        out_shape=[jax.ShapeDtypeStruct((seq, d_att), F32)],
        scratch=[pltpu.VMEM((seq, LANES), BF16)] * 5 + [pltpu.VMEM((2 * grp, t, 2 * t), BF16)]
        + [pltpu.VMEM((2 * grp, t, t), F32)] * 4 + [pltpu.VMEM((grp, t, LANES), F32)],
        operands=[u, u, u, scan_suffix], semantics=("arbitrary",))
    return res[0] if rider is None else (res[0], rode)


def _columns_copy(stage_ref, du_ref, rows, col, sem):
    width = stage_ref.shape[-1]
    cols = pl.ds(pl.multiple_of(col, LANES), width)
    return pltpu.make_async_copy(stage_ref, du_ref.at[slice(None) if rows is None else rows, cols], sem)


def _attn_bwd(u, d_att_out, du, scan_suffix, scan_prefix, *, seq, d_att, name, rider=None):
    t = ATT_TILE
    nq = seq // t
    pairs = d_att // LANES
    grp = ATT_GROUP
    assert nq % grp == 0

    def body(q_ref, k_ref, v_ref, do_ref, um_ref, pm_ref, du_old, du_ref,
             qq, q0, q1, k0, k1, v0, v1, dd, do0, do1, dq_acc, dk_acc, dv_acc, g_st, b_st,
             f2_s, a_s, lg_s, tot_s, dw_s, run_s, p_s, pt_s, grun_s, dq_s, dq_o, dk_o, dv_o, out_sem):
        _head_copies(seq, q_ref, ATT_SCALE, q0, q1, qq)
        _head_copies(seq, k_ref, 1.0, k0, k1, None)
        _head_copies(seq, v_ref, 1.0, v0, v1, None)
        _head_copies(seq, do_ref, 1.0, do0, do1, dd)
        dk_acc[...] = jnp.zeros_like(dk_acc)
        dv_acc[...] = jnp.zeros_like(dv_acc)

        tn = (((0,), (0,)), ((), ()))
        nt = (((1,), (1,)), ((), ()))

        def stacked(lo_ref, hi_ref, rows):
            return jnp.concatenate([lo_ref[rows, :], hi_ref[rows, :]], axis=0)

        def group(gi, carry):
            qb0 = gi * grp
            qrows = [pl.ds(pl.multiple_of((qb0 + g) * t, t), t) for g in range(grp)]
            qv = [qq[qrows[g], :] for g in range(grp)]
            dov = [dd[qrows[g], :] for g in range(grp)]
            q_heads = [stacked(q0, q1, qrows[g]) for g in range(grp)]
            do_heads = [stacked(do0, do1, qrows[g]) for g in range(grp)]

            def key_rows(g, i):
                return pl.ds(pl.multiple_of(jnp.maximum(qb0 + g - i, 0) * t, t), t)

            chains = [(h, g) for g in range(grp) for h in range(2)]
            every = range(len(chains))

            def stage1(i, masked):
                out = []
                for g in range(grp):
                    krows = key_rows(g, i)
                    out += _pair_scores(qv[g], k0[krows, :], k1[krows, :], masked)
                return out

            def stage2(i, held, masked):
                out = []
                for g in range(grp):
                    dw2 = lax.dot_general(dov[g], stacked(v0, v1, key_rows(g, i)), nt, preferred_element_type=F32)
                    for h in range(2):
                        c = 2 * g + h
                        halves, a = held[c]
                        scan = jnp.dot(halves, um_ref[...], preferred_element_type=F32)
                        logit = a + scan[:, :t]
                        if masked:
                            causal = lax.broadcasted_iota(jnp.int32, (t, t), 1) < lax.broadcasted_iota(jnp.int32, (t, t), 0)
                            logit = jnp.where(causal, logit, ATT_FILL)
                        b_st[c, i] = jnp.exp(a)
                        out.append((logit, scan[:, t:], dw2[:, h * t:(h + 1) * t]))
                return out

            def put(held=None, ready=None):
                for c in every:
                    if held is not None:
                        f2_s[c], a_s[c] = held[c]
                    if ready is not None:
                        lg_s[c], tot_s[c], dw_s[c] = ready[c]

            put(held=stage1(1, False), ready=stage2(0, stage1(0, True), True))
            for c in every:
                run_s[c] = jnp.zeros((t, t), F32)

            def wbody(st):
                i = st[0]
                held = [(f2_s[c], a_s[c]) for c in every]
                ready = [(lg_s[c], tot_s[c], dw_s[c]) for c in every]
                runs = [run_s[c] for c in every]
                for g in range(grp):
                    gone = jnp.where(qb0 + g - i >= 0, 0.0, ATT_FILL)
                    weights = []
                    for c in (2 * g, 2 * g + 1):
                        logit, total, dw = ready[c]
                        run = runs[c] + gone
                        w = jnp.exp(logit + run)
                        g_st[c, i] = w * dw
                        weights.append(w.astype(BF16))
                        runs[c] = run + total
                    dv_acc[key_rows(g, i), :] += lax.dot_general(jnp.concatenate(weights, axis=0), do_heads[g], tn,
                                                                 preferred_element_type=F32)
                ahead2 = stage2(i + 1, held, False)
                ahead1 = stage1(i + 2, False)
                put(held=ahead1, ready=ahead2)
                for c in every:
                    run_s[c] = runs[c]
                more = jnp.logical_and(i + 1 <= qb0 + grp - 1, _any_alive([runs]) > 0)
                return i + 1, more.astype(jnp.int32)

            steps = lax.while_loop(lambda st: st[1] > 0, wbody, (jnp.int32(0), jnp.int32(1)))[0]

            def prefix(i):
                out = []
                for c in every:
                    scan = _split_scan(g_st[c, i], pm_ref)
                    out.append((scan[:, :t], scan[:, t:]))
                return out

            def back(i, masked):
                sums = [(p_s[c], pt_s[c]) for c in every]
                gruns = [grun_s[c] for c in every]
                dqs = [dq_s[g] for g in range(grp)]
                for g in range(grp):
                    krows = key_rows(g, i)
                    dzs = []
                    for c in (2 * g, 2 * g + 1):
                        gt = g_st[c, i]
                        dz = gt - b_st[c, i] * (gt + gruns[c] + sums[c][0])
                        if masked:
                            causal = lax.broadcasted_iota(jnp.int32, (t, t), 1) < lax.broadcasted_iota(jnp.int32, (t, t), 0)
                            dz = jnp.where(causal, dz, 0.0)
                        dzs.append(dz.astype(BF16))
                        gruns[c] = gruns[c] + sums[c][1]
                    dqs[g] = dqs[g] + jnp.dot(jnp.concatenate(dzs, axis=1), stacked(k0, k1, krows), preferred_element_type=F32)
                    dk_acc[krows, :] += lax.dot_general(jnp.concatenate(dzs, axis=0), q_heads[g], tn, preferred_element_type=F32)
                return gruns, dqs

            def keep(sums=None, gruns=None, dqs=None):
                for c in every:
                    if sums is not None:
                        p_s[c], pt_s[c] = sums[c]
                    if gruns is not None:
                        grun_s[c] = gruns[c]
                if dqs is not None:
                    for g in range(grp):
                        dq_s[g] = dqs[g]

            keep(sums=prefix(steps - 1), gruns=[jnp.zeros((t, t), F32)] * len(chains), dqs=[jnp.zeros((t, LANES), F32)] * grp)

            def bbody(j, carry2):
                i = steps - 1 - j
                gruns, dqs = back(i, False)
                keep(sums=prefix(i - 1), gruns=gruns, dqs=dqs)
                return carry2

            lax.fori_loop(0, steps - 1, bbody, 0)
            _, dqs = back(0, True)
            for g in range(grp):
                dq_acc[qrows[g], :] = dqs[g]
            return carry

        lax.fori_loop(0, nq // grp, group, 0)

        pair = pl.program_id(0)
        stages = (dq_o, dk_o, dv_o)

        def copies(j):
            return [_columns_copy(stage, du_ref, None, k * d_att + j * LANES, out_sem.at[k]) for k, stage in enumerate(stages)]

        @pl.when(pair > 0)
        def _():
            for cp in copies(pair - 1):
                cp.wait()

        chunk = min(256, seq)

        def emit(r, carry):
            rows = pl.ds(pl.multiple_of(r * chunk, chunk), chunk)
            dq_o[rows, :] = (dq_acc[rows, :] * ATT_SCALE).astype(BF16)
            dk_o[rows, :] = dk_acc[rows, :].astype(BF16)
            dv_o[rows, :] = dv_acc[rows, :].astype(BF16)
            return carry

        lax.fori_loop(0, seq // chunk, emit, 0)
        for cp in copies(pair):
            cp.start()

        @pl.when(pair == pairs - 1)
        def _():
            for cp in copies(pair):
                cp.wait()

    blk = lambda base: pl.BlockSpec((seq, LANES), lambda j, base=base: (0, base + j), pipeline_mode=pl.Buffered(1))
    mat = pl.BlockSpec((2 * t, 2 * t), lambda j: (0, 0))
    hbm = pl.BlockSpec(memory_space=pl.ANY)
    res, rode = _host_call(
        body, rider, name=name, grid=(pairs,),
        in_specs=[blk(0), blk(pairs), blk(2 * pairs), blk(0), mat, mat, hbm],
        out_specs=[hbm], out_shape=[jax.ShapeDtypeStruct(du.shape, du.dtype)], host_aliases={6: 0},
        scratch=[pltpu.VMEM((seq, LANES), BF16)] * 10 + [pltpu.VMEM((seq, LANES), F32)] * 3
        + [pltpu.VMEM((2 * grp, nq + 1, t, t), F32)] * 2 + [pltpu.VMEM((2 * grp, t, 2 * t), BF16)]
        + [pltpu.VMEM((2 * grp, t, t), F32)] * 8 + [pltpu.VMEM((grp, t, LANES), F32)]
        + [pltpu.VMEM((seq, LANES), BF16)] * 3 + [pltpu.SemaphoreType.DMA((3,))],
        operands=[u, u, u, d_att_out, scan_suffix, scan_prefix, du], semantics=("arbitrary",))
    return res[0] if rider is None else (res[0], rode)


CONV_ROWS = 256


def _shifted(window, residue, rows):
    total = rows + CONV_PAD
    return window if residue == 0 else pltpu.roll(window, total - residue, 0)


def _glu_to_pad(seq, a_ref, b_ref, pad_ref):
    chunk = min(CONV_ROWS, seq)
    pad_ref[pl.ds(0, CONV_PAD), :] = jnp.zeros((CONV_PAD, LANES), F32)

    def step(r, carry):
        rows = pl.ds(pl.multiple_of(r * chunk, chunk), chunk)
        pad_ref[pl.ds(pl.multiple_of(r * chunk + CONV_PAD, SUBLANES), chunk), :] = a_ref[rows, :] * _sigmoid(b_ref[rows, :])
        return carry

    lax.fori_loop(0, seq // chunk, step, 0)


def _conv_fwd(u, conv_w, conv_b, *, layer, seq, d_conv, col_a, col_b, name, rider=None):
    blocks = d_conv // LANES
    rows_t = min(CONV_ROWS, seq)
    shift0 = CONV_PAD - (CONV_WIDTH - 1)

    def body(a_ref, b_ref, w_ref, bias_ref, o_ref, pad_ref):
        _glu_to_pad(seq, a_ref, b_ref, pad_ref)

        def step(r, carry):
            base = pl.multiple_of(r * rows_t, rows_t)
            window = pad_ref[pl.ds(base, rows_t + CONV_PAD), :]
            acc = jnp.zeros((rows_t, LANES), F32) + bias_ref[...]
            for residue in range(SUBLANES):
                moved = _shifted(window, residue, rows_t)
                for tap in range(CONV_WIDTH):
                    if (shift0 + tap) % SUBLANES == residue:
                        lo = (shift0 + tap) - residue
                        acc = acc + w_ref[tap:tap + 1, :] * moved[lo:lo + rows_t, :]
            o_ref[pl.ds(base, rows_t), :] = acc
            return carry

        lax.fori_loop(0, seq // rows_t, step, 0)

    res, rode = _host_call(
        body, rider, name=name, grid=(blocks,),
        in_specs=[pl.BlockSpec((seq, LANES), lambda j: (0, col_a + j)), pl.BlockSpec((seq, LANES), lambda j: (0, col_b + j)),
                  pl.BlockSpec((None, CONV_PAD, LANES), lambda j: (layer, 0, j)),
                  pl.BlockSpec((None, 1, LANES), lambda j: (layer, 0, j))],
        out_specs=[pl.BlockSpec((seq, LANES), lambda j: (0, j))],
        out_shape=[jax.ShapeDtypeStruct((seq, d_conv), F32)],
        scratch=[pltpu.VMEM((seq + CONV_PAD, LANES), F32)],
        operands=[u, u, conv_w, conv_b], semantics=("arbitrary",))
    return res[0] if rider is None else (res[0], rode)


def _conv_bwd(u, dc1, du, conv_w, *, layer, seq, d_conv, col_a, col_b, name):
    blocks = d_conv // LANES
    rows_t = min(CONV_ROWS, seq)
    shift0 = CONV_PAD - (CONV_WIDTH - 1)

    def body(a_ref, b_ref, d_ref, w_ref, du_old, du_ref, dw_ref, pad_ref, dpad_ref, dw_acc, da_ref, db_ref, out_sem):
        block = pl.program_id(0)

        def copies(j):
            return [_columns_copy(stage, du_ref, None, (base + j) * LANES, out_sem.at[k])
                    for k, (stage, base) in enumerate(((da_ref, col_a), (db_ref, col_b)))]

        _glu_to_pad(seq, a_ref, b_ref, pad_ref)
        dpad_ref[pl.ds(seq, CONV_PAD), :] = jnp.zeros((CONV_PAD, LANES), F32)

        def fill(r, carry):
            rows = pl.ds(pl.multiple_of(r * rows_t, rows_t), rows_t)
            dpad_ref[rows, :] = d_ref[rows, :]
            return carry

        lax.fori_loop(0, seq // rows_t, fill, 0)
        dw_acc[...] = jnp.zeros_like(dw_acc)

        @pl.when(block > 0)
        def _():
            for cp in copies(block - 1):
                cp.wait()

        def step(r, carry):
            base = pl.multiple_of(r * rows_t, rows_t)
            rows = pl.ds(base, rows_t)
            window = dpad_ref[pl.ds(base, rows_t + CONV_PAD), :]
            acc = jnp.zeros((rows_t, LANES), F32)
            for residue in range(SUBLANES):
                moved = _shifted(window, residue, rows_t)
                for tap in range(CONV_WIDTH):
                    off = CONV_WIDTH - 1 - tap
                    if off % SUBLANES == residue:
                        lo = off - residue
                        acc = acc + w_ref[tap:tap + 1, :] * moved[lo:lo + rows_t, :]
            sig = _sigmoid(b_ref[rows, :])
            a = a_ref[rows, :]
            da_ref[rows, :] = (acc * sig).astype(BF16)
            db_ref[rows, :] = (acc * a * sig * (1.0 - sig)).astype(BF16)
            dcur = d_ref[rows, :]
            cwin = pad_ref[pl.ds(base, rows_t + CONV_PAD), :]
            for residue in range(SUBLANES):
                moved = _shifted(cwin, residue, rows_t)
                for tap in range(CONV_WIDTH):
                    if (shift0 + tap) % SUBLANES == residue:
                        lo = (shift0 + tap) - residue
                        prod = dcur * moved[lo:lo + rows_t, :]
                        dw_acc[tap] += jnp.sum(prod.reshape(rows_t // SUBLANES, SUBLANES, LANES), axis=0)
            return carry

        lax.fori_loop(0, seq // rows_t, step, 0)
        dw_ref[...] = jnp.sum(dw_acc[...], axis=1)
        for cp in copies(block):
            cp.start()

        @pl.when(block == blocks - 1)
        def _():
            for cp in copies(block):
                cp.wait()

    col = lambda base: pl.BlockSpec((seq, LANES), lambda j, base=base: (0, base + j))
    own = pl.BlockSpec((seq, LANES), lambda j: (0, j))
    hbm = pl.BlockSpec(memory_space=pl.ANY)
    return pl.pallas_call(
        body, name=name, grid=(blocks,),
        in_specs=[col(col_a), col(col_b), own, pl.BlockSpec((None, CONV_PAD, LANES), lambda j: (layer, 0, j)), hbm],
        out_specs=[hbm, pl.BlockSpec((CONV_PAD, LANES), lambda j: (0, j))],
        out_shape=[jax.ShapeDtypeStruct(du.shape, du.dtype), jax.ShapeDtypeStruct((CONV_PAD, d_conv), F32)],
        scratch_shapes=[pltpu.VMEM((seq + CONV_PAD, LANES), F32), pltpu.VMEM((seq + CONV_PAD, LANES), F32),
                        pltpu.VMEM((CONV_PAD, SUBLANES, LANES), F32), pltpu.VMEM((seq, LANES), BF16), pltpu.VMEM((seq, LANES), BF16),
                        pltpu.SemaphoreType.DMA((2,))],
        input_output_aliases={4: 0}, compiler_params=_cparams(("arbitrary",)),
    )(u, u, dc1, conv_w, du)


MIX_ROWS = 256


def _layer_norm_stats(val):
    mu = jnp.mean(val, axis=-1, keepdims=True)
    cen = val - mu
    var = jnp.mean(cen * cen, axis=-1, keepdims=True)
    rstd = lax.rsqrt(var + LN_EPS)
    return cen * rstd, rstd


def _layer_norm_bwd(dy, xhat, rstd, gain):
    dxhat = dy * gain
    m1 = jnp.mean(dxhat, axis=-1, keepdims=True)
    m2 = jnp.mean(dxhat * xhat, axis=-1, keepdims=True)
    dx = rstd * (dxhat - m1 - xhat * m2)
    return dx, jnp.sum(dy * xhat, axis=0, keepdims=True), jnp.sum(dy, axis=0, keepdims=True)


def _mix_forward(zatt, att, c1, zconv, gatt, gconv, x, w_att, w_conv, w_out, cln_g, cln_b, b_proj):
    s_zatt = _sigmoid(zatt)
    a_in = att * (zatt * s_zatt)
    chat, c_rstd = _layer_norm_stats(c1)
    c2 = chat * cln_g + cln_b
    s_c2 = _sigmoid(c2)
    c3 = c2 * s_c2
    s_zconv = _sigmoid(zconv)
    c_in = c3 * (zconv * s_zconv)
    a_in_b, c_in_b = a_in.astype(BF16), c_in.astype(BF16)
    ab = jnp.dot(a_in_b, w_att, preferred_element_type=F32)
    cb = jnp.dot(c_in_b, w_conv, preferred_element_type=F32) + b_proj
    s_gatt, s_gconv = _sigmoid(gatt), _sigmoid(gconv)
    merged_b = (s_gatt * ab + s_gconv * cb).astype(BF16)
    y = jnp.dot(merged_b, w_out, preferred_element_type=F32)
    h = DEEPNORM_ALPHA * x + y
    return dict(s_zatt=s_zatt, a_in_b=a_in_b, chat=chat, c_rstd=c_rstd, c2=c2, s_c2=s_c2, c3=c3, s_zconv=s_zconv,
                c_in_b=c_in_b, ab=ab, cb=cb, s_gatt=s_gatt, s_gconv=s_gconv, merged_b=merged_b, h=h)


def _u_blocks(rows_t, width, half):
    return [pl.BlockSpec((rows_t, half), lambda i, c=c: (i, c)) for c in (3, 6, 7, 8, 9, 10)]


def _of_layer(arr, layer):
    return pl.BlockSpec((None,) + arr.shape[1:], lambda i: (layer, 0, 0))


def _mix_fwd(u, att, c1, x, w_att, w_conv, w_out, cln_g, cln_b, b_proj, ln_g, ln_b, *, layer, seq, d_model, name):
    half = d_model // 2
    rows_t = min(MIX_ROWS, seq)

    def body(zatt_ref, zconv_ref, ga0, ga1, gc0, gc1, att_ref, c1_ref, x_ref, wa_ref, wc_ref, wo_ref,
             cg_ref, cb_ref, bp_ref, lg_ref, lb_ref, o_ref, ob_ref):
        gatt = jnp.concatenate([ga0[...], ga1[...]], axis=1)
        gconv = jnp.concatenate([gc0[...], gc1[...]], axis=1)
        mid = _mix_forward(zatt_ref[...], att_ref[...], c1_ref[...], zconv_ref[...], gatt, gconv, x_ref[...],
                           wa_ref[...], wc_ref[...], wo_ref[...], cg_ref[...], cb_ref[...], bp_ref[...])
        xhat, _ = _layer_norm_stats(mid["h"])
        out = xhat * lg_ref[...] + lb_ref[...]
        o_ref[...] = out
        ob_ref[...] = out.astype(BF16)

    row = lambda width: pl.BlockSpec((rows_t, width), lambda i: (i, 0))
    full = lambda arr: _of_layer(arr, layer)
    out = pl.BlockSpec((rows_t, d_model), lambda i: (i, 0))
    return pl.pallas_call(
        body, name=name, grid=(seq // rows_t,),
        in_specs=_u_blocks(rows_t, d_model, half) + [row(half), row(half), row(d_model), full(w_att), full(w_conv), full(w_out),
                                                     full(cln_g), full(cln_b), full(b_proj), full(ln_g), full(ln_b)],
        out_specs=[out, out],
        out_shape=[jax.ShapeDtypeStruct((seq, d_model), F32), jax.ShapeDtypeStruct((seq, d_model), BF16)],
        compiler_params=_cparams(("parallel",)),
    )(u, u, u, u, u, u, att, c1, x, w_att, w_conv, w_out, cln_g, cln_b, b_proj, ln_g, ln_b)


def _mix_bwd(u, att, c1, x, dxn, w_att, w_conv, w_out, cln_g, cln_b, b_proj, ln_g, *, layer, seq, d_model, name):
    half = d_model // 2
    rows_t = min(MIX_ROWS, seq)
    steps = seq // rows_t
    nt = ((1,), (1,))
    tn = ((0,), (0,))

    def body(zatt_ref, zconv_ref, ga0, ga1, gc0, gc1, att_ref, c1_ref, x_ref, dxn_ref, wa_ref, wc_ref, wo_ref,
             cg_ref, cb_ref, bp_ref, lg_ref,
             du_ref, datt_ref, dc1_ref, dxres_ref, dwa_ref, dwc_ref, dwo_ref,
             dcg_ref, dcb_ref, dcbias_ref, dbp_ref, dlg_ref, dlb_ref, zatt_stage, tail_stage, out_sem):
        sums = (dwa_ref, dwc_ref, dwo_ref, dcg_ref, dcb_ref, dcbias_ref, dbp_ref, dlg_ref, dlb_ref)
        tile = pl.program_id(0)
        slot = tile % 2
        dzatt_ref, tail_ref = zatt_stage.at[slot], tail_stage.at[slot]
        dzconv_ref, dgatt_ref, dgconv_ref = tail_ref.at[:, :half], tail_ref.at[:, half:3 * half], tail_ref.at[:, 3 * half:]

        def copies(i):
            rows = pl.ds(pl.multiple_of(i * rows_t, rows_t), rows_t)
            return [_columns_copy(zatt_stage.at[i % 2], du_ref, rows, 3 * half, out_sem.at[i % 2, 0]),
                    _columns_copy(tail_stage.at[i % 2], du_ref, rows, 6 * half, out_sem.at[i % 2, 1])]

        @pl.when(tile == 0)
        def _():
            for ref in sums:
                ref[...] = jnp.zeros_like(ref)

        zatt, zconv, att = zatt_ref[...], zconv_ref[...], att_ref[...]
        gatt = jnp.concatenate([ga0[...], ga1[...]], axis=1)
        gconv = jnp.concatenate([gc0[...], gc1[...]], axis=1)
        wa, wc, wo = wa_ref[...], wc_ref[...], wo_ref[...]
        mid = _mix_forward(zatt, att, c1_ref[...], zconv, gatt, gconv, x_ref[...], wa, wc, wo,
                           cg_ref[...], cb_ref[...], bp_ref[...])
        xhat, rstd = _layer_norm_stats(mid["h"])
        dh, dlg, dlb = _layer_norm_bwd(dxn_ref[...], xhat, rstd, lg_ref[...])
        dlg_ref[...] += dlg
        dlb_ref[...] += dlb
        dxres_ref[...] = DEEPNORM_ALPHA * dh
        dy = dh.astype(BF16)
        dwo_ref[...] += lax.dot_general(mid["merged_b"], dy, (tn, ((), ())), preferred_element_type=F32)
        dmerged = lax.dot_general(dy, wo, (nt, ((), ())), preferred_element_type=F32)
        s_ga, s_gc, ab, cb = mid["s_gatt"], mid["s_gconv"], mid["ab"], mid["cb"]
        dgatt_ref[...] = (dmerged * ab * s_ga * (1.0 - s_ga)).astype(BF16)
        dgconv_ref[...] = (dmerged * cb * s_gc * (1.0 - s_gc)).astype(BF16)
        dab = dmerged * s_ga
        dcb = dmerged * s_gc
        dbp_ref[...] += jnp.sum(dcb, axis=0, keepdims=True)
        dab_b, dcb_b = dab.astype(BF16), dcb.astype(BF16)
        dwa_ref[...] += lax.dot_general(mid["a_in_b"], dab_b, (tn, ((), ())), preferred_element_type=F32)
        da_in = lax.dot_general(dab_b, wa, (nt, ((), ())), preferred_element_type=F32)
        s_za = mid["s_zatt"]
        datt_ref[...] = da_in * (zatt * s_za)
        dzatt_ref[...] = (da_in * att * (s_za * (1.0 + zatt * (1.0 - s_za)))).astype(BF16)
        dwc_ref[...] += lax.dot_general(mid["c_in_b"], dcb_b, (tn, ((), ())), preferred_element_type=F32)
        dc_in = lax.dot_general(dcb_b, wc, (nt, ((), ())), preferred_element_type=F32)
        s_zc, c2, s_c2 = mid["s_zconv"], mid["c2"], mid["s_c2"]
        dzconv_ref[...] = (dc_in * mid["c3"] * (s_zc * (1.0 + zconv * (1.0 - s_zc)))).astype(BF16)
        dc3 = dc_in * (zconv * s_zc)
        dc2 = dc3 * (s_c2 * (1.0 + c2 * (1.0 - s_c2)))
        dc1, dcg, dcbeta = _layer_norm_bwd(dc2, mid["chat"], mid["c_rstd"], cg_ref[...])
        dcg_ref[...] += dcg
        dcb_ref[...] += dcbeta
        dcbias_ref[...] += jnp.sum(dc1, axis=0, keepdims=True)
        dc1_ref[...] = dc1
        for cp in copies(tile):
            cp.start()

        @pl.when(tile > 0)
        def _():
            for cp in copies(tile - 1):
                cp.wait()

        @pl.when(tile == steps - 1)
        def _():
            for cp in copies(tile):
                cp.wait()

    row = lambda width: pl.BlockSpec((rows_t, width), lambda i: (i, 0))
    full = lambda arr: _of_layer(arr, layer)
    whole = lambda r, c: pl.BlockSpec((r, c), lambda i: (0, 0))
    sds = jax.ShapeDtypeStruct
    out_specs = [pl.BlockSpec(memory_space=pl.ANY), row(half), row(half), row(d_model),
                 whole(half, d_model), whole(half, d_model), whole(d_model, d_model),
                 whole(1, half), whole(1, half), whole(1, half), whole(1, d_model), whole(1, d_model), whole(1, d_model)]
    out_shape = [sds((seq, u.shape[1]), BF16), sds((seq, half), F32), sds((seq, half), F32), sds((seq, d_model), F32),
                 sds((half, d_model), F32), sds((half, d_model), F32), sds((d_model, d_model), F32),
                 sds((1, half), F32), sds((1, half), F32), sds((1, half), F32),
                 sds((1, d_model), F32), sds((1, d_model), F32), sds((1, d_model), F32)]
    return pl.pallas_call(
        body, name=name, grid=(steps,),
        in_specs=_u_blocks(rows_t, d_model, half) + [row(half), row(half), row(d_model), row(d_model), full(w_att), full(w_conv),
                                                     full(w_out), full(cln_g), full(cln_b), full(b_proj), full(ln_g)],
        out_specs=out_specs, out_shape=out_shape,
        scratch_shapes=[pltpu.VMEM((2, rows_t, half), BF16), pltpu.VMEM((2, rows_t, 5 * half), BF16), pltpu.SemaphoreType.DMA((2, 2))],
        compiler_params=_cparams(("arbitrary",)),
    )(u, u, u, u, u, u, att, c1, x, dxn, w_att, w_conv, w_out, cln_g, cln_b, b_proj, ln_g)


def _loss_head(y, target, *, seq, d_model, name):
    rows_t = min(512, seq)

    def body(y_ref, t_ref, dy_ref, loss_ref):
        @pl.when(pl.program_id(0) == 0)
        def _():
            loss_ref[...] = jnp.zeros_like(loss_ref)

        err = y_ref[...] - t_ref[...]
        dy_ref[...] = err * (1.0 / d_model)
        per_token = jnp.sum(err * err, axis=-1, keepdims=True) * (1.0 / d_model)
        loss_ref[...] += 0.5 * jnp.sum(per_token, axis=0, keepdims=True)

    row = pl.BlockSpec((rows_t, d_model), lambda i: (i, 0))
    return pl.pallas_call(
        body, name=name, grid=(seq // rows_t,), in_specs=[row, row],
        out_specs=[row, pl.BlockSpec((1, 1), lambda i: (0, 0))],
        out_shape=[jax.ShapeDtypeStruct((seq, d_model), F32), jax.ShapeDtypeStruct((1, 1), F32)],
        compiler_params=_cparams(("arbitrary",)),
    )(y, target)


def _adamw(w, g, m, v, *, name, echo=False):
    rows, cols = w.shape
    rows_t = rows
    for cand in (512, 256, 128, 64, 32, 16, 8):
        if rows % cand == 0 and cand * cols * 4 <= 2 * 1024 * 1024:
            rows_t = cand
            break
    n_out = 4 if echo else 3

    def body(w_ref, g_ref, m_ref, v_ref, *outs):
        d_ref, nm_ref, nv_ref = outs[-3:]
        grad = g_ref[...]
        if echo:
            outs[0][...] = grad
        new_m = ADAM_B1 * m_ref[...] + (1.0 - ADAM_B1) * grad
        new_v = ADAM_B2 * v_ref[...] + (1.0 - ADAM_B2) * (grad * grad)
        m_hat = new_m / (1.0 - ADAM_B1 ** ADAM_STEP)
        v_hat = new_v / (1.0 - ADAM_B2 ** ADAM_STEP)
        d_ref[...] = -ADAM_LR * (m_hat / (jnp.sqrt(v_hat) + ADAM_EPS) + ADAM_WD * w_ref[...])
        nm_ref[...] = new_m
        nv_ref[...] = new_v

    blk = pl.BlockSpec((rows_t, cols), lambda i: (i, 0))
    out = jax.ShapeDtypeStruct((rows, cols), F32)
    return pl.pallas_call(
        body, name=name, grid=(rows // rows_t,), in_specs=[blk] * 4, out_specs=[blk] * n_out, out_shape=[out] * n_out,
        compiler_params=_cparams(("parallel",)),
    )(w, g, m, v)


MATRICES = ("w_in", "w_att_proj", "w_conv_proj", "w_out")
VECTORS = ("b_in", "conv_b", "conv_ln_g", "conv_ln_b", "b_conv_proj", "ln_g", "ln_b")
MATRIX_CHIP_AXIS = (1, 1, 1, 0)
MATRIX_CORE_AXIS = (0, 0, 0, 1)


def _chip_sums(layer, depth, partials, got, blocks, place):
    return [_chip_sum_layer(partials[t], got[t], blocks[t], layer, depth, MATRIX_CHIP_AXIS[t], MATRIX_CORE_AXIS[t], place,
                            name="chip_sum_" + MATRICES[t]) for t in range(len(MATRICES))]


def _train_pass(x, target, mats, taps, b_in, conv_b, cln_g, cln_b, b_proj, ln_g, ln_b, place):
    seq, d_model = x.shape
    half = d_model // 2
    depth = b_in.shape[0]
    scan_suffix, scan_prefix = _scan_matrices()
    cols = half // LANES
    dims = dict(seq=seq, d_model=d_model)
    conv_dims = dict(seq=seq, d_conv=half, col_a=4 * cols, col_b=5 * cols)
    axes3 = [axis + 1 for axis in MATRIX_CHIP_AXIS]
    n_mat = len(mats)

    def first_layers(relay):
        return _gather_rider([mats[0], taps], [axes3[0], 2], [(0, 1), (0, depth)], relay)

    mats = list(mats)
    mats[0], taps = _lone_call(_chain(first_layers(False), first_layers(True)), name="gather_first")

    xs, xbs, us, atts, c1s = [x], [x.astype(BF16)], [], [], []
    for l in range(depth):
        proj = dict(mode="nn", bias=b_in[l].reshape(1, -1), tm=256, tn=b_in.shape[1], tk=d_model)
        nxt = [(l + 1, 1)] + [(l + 1, 1) if l else (0, 2)] * (n_mat - 1)
        if l + 1 < depth:
            u, mats[1:] = _matmul(xbs[l], mats[0], layer=l, name="in_proj", rider=_gather_rider(mats[1:], axes3[1:], nxt[1:], False),
                                  **proj)
            att, mats[:1] = _attn_fwd(u, scan_suffix, seq=seq, d_att=half, name="attn_fwd",
                                      rider=_gather_rider(mats[:1], axes3[:1], nxt[:1], False))
            c1, mats = _conv_fwd(u, taps, conv_b, layer=l, name="conv_fwd", rider=_gather_rider(mats, axes3, nxt, True), **conv_dims)
        else:
            u = _matmul(xbs[l], mats[0], layer=l, name="in_proj_last", **proj)
            att = _attn_fwd(u, scan_suffix, seq=seq, d_att=half, name="attn_fwd_last")
            c1 = _conv_fwd(u, taps, conv_b, layer=l, name="conv_fwd_last", **conv_dims)
        xn, xnb = _mix_fwd(u, att, c1, xs[l], mats[1], mats[2], mats[3], cln_g, cln_b, b_proj, ln_g, ln_b, layer=l, name="mix_fwd", **dims)
        us.append(u)
        atts.append(att)
        c1s.append(c1)
        xs.append(xn)
        xbs.append(xnb)
    w_in, w_att, w_conv, w_out = mats

    dx, loss = _loss_head(xs[depth], target, name="loss_head", **dims)
    grads = [None] * depth
    blocks = [None] * n_mat
    waiting = None
    for l in reversed(range(depth)):
        u = us[l]
        (du, datt, dc1, dxres, dwa, dwc, dwo, dcg, dcb, dcbias, dbp, dlg, dlb) = _mix_bwd(
            u, atts[l], c1s[l], xs[l], dx, w_att, w_conv, w_out, cln_g, cln_b, b_proj, ln_g, layer=l, name="mix_bwd", **dims)
        if waiting is None:
            du = _attn_bwd(u, datt, du, scan_suffix, scan_prefix, seq=seq, d_att=half, name="attn_bwd_first")
        else:
            du, got = _attn_bwd(u, datt, du, scan_suffix, scan_prefix, seq=seq, d_att=half, name="attn_bwd",
                                rider=_exchange_rider(waiting[1], MATRIX_CHIP_AXIS))
            blocks = _chip_sums(waiting[0], depth, waiting[1], got, blocks, place)
        du, dconvw = _conv_bwd(u, dc1, du, taps, layer=l, name="conv_bwd", **conv_dims)
        dwin, dbin = _matmul(xbs[l], du, mode="tn", colsum=True, name="in_proj_dw", tm=1024, tn=512, tk=seq)
        parts = [dwin, dwa, dwc, dwo]
        swap = _swap_rider(parts, MATRIX_CORE_AXIS)
        if l == 0:
            swap = _both(swap, _join_rider(blocks, [axis + 1 for axis in MATRIX_CORE_AXIS], 1, depth - 1))
        dx, rode = _matmul(du, w_in, layer=l, mode="nt", add=dxres, name="in_proj_dx", tm=512, tn=1024, tk=du.shape[1], rider=swap)
        got, blocks = rode[:n_mat], (rode[n_mat:] if l == 0 else blocks)
        waiting = (l, [_pair_sum_layer(parts[t], got[t], MATRIX_CORE_AXIS[t], place, name="pair_sum_" + MATRICES[t])
                       for t in range(n_mat)])
        grads[l] = dict(b_in=dbin[0], conv_w=dconvw[:CONV_WIDTH], conv_b=dcbias[0], conv_ln_g=dcg[0], conv_ln_b=dcb[0],
                        b_conv_proj=dbp[0], ln_g=dlg[0], ln_b=dlb[0])
    packed = jnp.stack([jnp.concatenate([grads[l][n] for n in VECTORS] + [grads[l]["conv_w"].reshape(-1)]) for l in range(depth)])
    return loss, dx, packed, waiting[1], blocks


MESH = pl.DeviceIdType.MESH


def _position():
    x, y, c = lax.axis_index("x"), lax.axis_index("y"), lax.axis_index("c")
    return x, y, c, [(1 - x, y), (x, 1 - y), (1 - x, 1 - y)]


def _cut(ref, axis, start, size):
    idx = [slice(None)] * len(ref.shape)
    idx[axis] = pl.ds(start, size)
    return ref.at[tuple(idx)]


def _remote(src, dst, send_sem, recv_sem, device):
    return pltpu.make_async_remote_copy(src_ref=src, dst_ref=dst, send_sem=send_sem, recv_sem=recv_sem,
                                        device_id=device, device_id_type=MESH)


def _gather_rider(wholes, chip_axes, spans, relay):
    n = len(wholes)

    def region(dst, t, chip, half):
        first, count = spans[t]
        _, rows, cols = wholes[t].shape
        if chip_axes[t] == 2:
            size, part = cols // N_CHIPS, rows // 2
            ref = _cut(_cut(dst[t], 2, pl.multiple_of(chip * size, size), size), 1, pl.multiple_of(half * part, part), part)
        else:
            size = rows // N_CHIPS
            part = size // 2
            ref = _cut(dst[t], 1, pl.multiple_of(chip * size + half * part, part), part)
        return _cut(ref, 0, first, count)

    def copies(dst, sems, receiving):
        x, y, c, chips = _position()
        send_sem, recv_sem = sems
        out = []
        for t in range(n):
            for j, chip in enumerate(chips):
                theirs = 2 * chip[0] + chip[1]
                if relay:
                    ref = region(dst, t, theirs, 1 - c if receiving else c)
                    peer = (x, y, 1 - c)
                else:
                    ref = region(dst, t, theirs if receiving else 2 * x + y, c)
                    peer = (*chip, c)
                out.append(_remote(ref, ref, send_sem.at[t, j], recv_sem.at[t, j], peer))
        return out

    def start(ins, outs, sems):
        for cp in copies(outs, sems, False):
            cp.start()

    def finish(ins, outs, sems):
        for cp in copies(outs, sems, True):
            cp.wait_recv()
        for cp in copies(outs, sems, False):
            cp.wait_send()

    return _Rider(operands=list(wholes), out_shape=[jax.ShapeDtypeStruct(w.shape, w.dtype) for w in wholes],
                  aliases={t: t for t in range(n)}, scratch=[pltpu.SemaphoreType.DMA((n, N_CHIPS - 1))] * 2,
                  start=start, finish=finish)


def _chain(first, second):
    cut = len(first.scratch)

    def start(ins, outs, sems):
        first.start(ins, outs, sems[:cut])
        first.finish(ins, outs, sems[:cut])
        second.start(ins, outs, sems[cut:])

    def finish(ins, outs, sems):
        second.finish(ins, outs, sems[cut:])

    return first._replace(scratch=list(first.scratch) + list(second.scratch), start=start, finish=finish)


def _swap_rider(parts, core_axes):
    n = len(parts)
    halves = []
    for arr, axis in zip(parts, core_axes):
        shape = list(arr.shape)
        shape[axis] //= 2
        halves.append(jax.ShapeDtypeStruct(tuple(shape), arr.dtype))

    def copies(ins, outs, sems):
        x, y, c, _ = _position()
        out = []
        for t in range(n):
            size = halves[t].shape[core_axes[t]]
            piece = _cut(ins[t], core_axes[t], pl.multiple_of((1 - c) * size, size), size)
            out.append(_remote(piece, outs[t], sems[0].at[t], sems[1].at[t], (x, y, 1 - c)))
        return out

    def start(ins, outs, sems):
        for cp in copies(ins, outs, sems):
            cp.start()

    def finish(ins, outs, sems):
        for cp in copies(ins, outs, sems):
            cp.wait()

    return _Rider(operands=list(parts), out_shape=halves, aliases={}, scratch=[pltpu.SemaphoreType.DMA((n,))] * 2,
                  start=start, finish=finish)


def _exchange_rider(partials, chip_axes):
    n = len(partials)
    quarters = []
    for arr, axis in zip(partials, chip_axes):
        shape = list(arr.shape)
        shape[axis] //= N_CHIPS
        quarters.append(jax.ShapeDtypeStruct((N_CHIPS - 1, *shape), arr.dtype))

    def copies(ins, outs, sems):
        x, y, c, chips = _position()
        out = []
        for t in range(n):
            size = quarters[t].shape[1 + chip_axes[t]]
            for j, chip in enumerate(chips):
                piece = _cut(ins[t], chip_axes[t], pl.multiple_of((2 * chip[0] + chip[1]) * size, size), size)
                out.append(_remote(piece, outs[t].at[j], sems[0].at[t, j], sems[1].at[t, j], (*chip, c)))
        return out

    def start(ins, outs, sems):
        for cp in copies(ins, outs, sems):
            cp.start()

    def finish(ins, outs, sems):
        for cp in copies(ins, outs, sems):
            cp.wait()

    return _Rider(operands=list(partials), out_shape=quarters, aliases={}, scratch=[pltpu.SemaphoreType.DMA((n, N_CHIPS - 1))] * 2,
                  start=start, finish=finish)


def _place_block(shard, chip_axis, place, dtype, *, name):
    depth, rows, cols = shard.shape
    rows_t = _row_tile(rows, cols * 4 * 4, 16 * 1024 * 1024)
    steps = rows // rows_t
    shape = list(shard.shape)
    shape[chip_axis] *= N_CHIPS
    if chip_axis == 1:
        out_spec = pl.BlockSpec((None, rows_t, cols), lambda l, i, p: (l, p[1] * steps + i, 0))
    else:
        out_spec = pl.BlockSpec((None, rows_t, cols), lambda l, i, p: (l, i, p[1]))

    def body(place_ref, src_ref, out_ref):
        out_ref[...] = src_ref[...].astype(dtype)

    return pl.pallas_call(
        body, name=name, out_shape=jax.ShapeDtypeStruct(tuple(shape), dtype),
        grid_spec=pltpu.PrefetchScalarGridSpec(num_scalar_prefetch=1, grid=(depth, steps),
                                               in_specs=[pl.BlockSpec((None, rows_t, cols), lambda l, i, p: (l, i, 0))],
                                               out_specs=out_spec),
        compiler_params=_cparams(("arbitrary", "arbitrary")),
    )(place, shard)


def _join_rider(blocks, core_axes, first, count):
    n = len(blocks)

    def copies(outs, sems, receiving):
        x, y, c, _ = _position()
        out = []
        for t in range(n):
            size = blocks[t].shape[core_axes[t]] // 2
            half = 1 - c if receiving else c
            ref = _cut(_cut(outs[t], core_axes[t], pl.multiple_of(half * size, size), size), 0, first, count)
            out.append(_remote(ref, ref, sems[0].at[t], sems[1].at[t], (x, y, 1 - c)))
        return out

    def start(ins, outs, sems):
        for cp in copies(outs, sems, False):
            cp.start()

    def finish(ins, outs, sems):
        for cp in copies(outs, sems, True):
            cp.wait_recv()
        for cp in copies(outs, sems, False):
            cp.wait_send()

    return _Rider(operands=list(blocks), out_shape=[jax.ShapeDtypeStruct(b.shape, b.dtype) for b in blocks],
                  aliases={t: t for t in range(n)}, scratch=[pltpu.SemaphoreType.DMA((n,))] * 2, start=start, finish=finish)


def _small_rider(vec):
    n_dev = 2 * N_CHIPS

    def copies(ins, outs, sems, receiving):
        x, y, c, _ = _position()
        flip = lambda v, bit: 1 - v if bit else v
        out = []
        for mask in range(1, n_dev):
            peer = (flip(x, mask & 4), flip(y, mask & 2), flip(c, mask & 1))
            row = 4 * peer[0] + 2 * peer[1] + peer[2] if receiving else 4 * x + 2 * y + c
            out.append(_remote(ins[0], outs[0].at[row], sems[0].at[mask - 1], sems[1].at[mask - 1], peer))
        return out

    def own(ins, outs, sems):
        x, y, c, _ = _position()
        return pltpu.make_async_copy(ins[0], outs[0].at[4 * x + 2 * y + c], sems[2])

    def start(ins, outs, sems):
        own(ins, outs, sems).start()
        for cp in copies(ins, outs, sems, False):
            cp.start()

    def finish(ins, outs, sems):
        for cp in copies(ins, outs, sems, True):
            cp.wait_recv()
        for cp in copies(ins, outs, sems, False):
            cp.wait_send()
        own(ins, outs, sems).wait()

    return _Rider(operands=[vec], out_shape=[jax.ShapeDtypeStruct((n_dev, *vec.shape), vec.dtype)], aliases={},
                  scratch=[pltpu.SemaphoreType.DMA((n_dev - 1,))] * 2 + [pltpu.SemaphoreType.DMA(())], start=start, finish=finish)


def _both(first, second):
    n_in, n_out, n_sem = len(first.operands), len(first.out_shape), len(first.scratch)

    def start(ins, outs, sems):
        first.start(ins[:n_in], outs[:n_out], sems[:n_sem])
        second.start(ins[n_in:], outs[n_out:], sems[n_sem:])

    def finish(ins, outs, sems):
        first.finish(ins[:n_in], outs[:n_out], sems[:n_sem])
        second.finish(ins[n_in:], outs[n_out:], sems[n_sem:])

    aliases = dict(first.aliases)
    aliases.update({n_in + i: n_out + o for i, o in second.aliases.items()})
    return _Rider(operands=list(first.operands) + list(second.operands), out_shape=list(first.out_shape) + list(second.out_shape),
                  aliases=aliases, scratch=list(first.scratch) + list(second.scratch), start=start, finish=finish)


def _row_tile(rows, row_bytes, budget):
    tile = rows
    for cand in (512, 256, 128, 64, 32, 16, 8):
        if rows % cand == 0:
            tile = cand
            if cand * row_bytes <= budget:
                break
    return tile


def _pair_sum_layer(part, got, core_axis, place, *, name):
    rows, cols = got.shape
    rows_t = _row_tile(rows, cols * 4 * 6, 16 * 1024 * 1024)
    steps = rows // rows_t
    if core_axis == 0:
        part_spec = pl.BlockSpec((rows_t, cols), lambda i, p: (p[0] * steps + i, 0))
    else:
        part_spec = pl.BlockSpec((rows_t, cols), lambda i, p: (i, p[0]))
    own_spec = pl.BlockSpec((rows_t, cols), lambda i, p: (i, 0))

    def body(place_ref, part_ref, got_ref, out_ref):
        out_ref[...] = (part_ref[...] + got_ref[...]).astype(BF16)

    return pl.pallas_call(
        body, name=name, out_shape=jax.ShapeDtypeStruct(got.shape, BF16),
        grid_spec=pltpu.PrefetchScalarGridSpec(num_scalar_prefetch=1, grid=(steps,), in_specs=[part_spec, own_spec], out_specs=own_spec),
        compiler_params=_cparams(("arbitrary",)),
    )(place, part, got)


def _chip_sum_layer(partial, got, blocks, layer, depth, chip_axis, core_axis, place, *, name):
    _, rows, cols = got.shape
    rows_t = _row_tile(rows, cols * 4 * 10, 24 * 1024 * 1024)
    steps = rows // rows_t
    if chip_axis == 0:
        own_spec = pl.BlockSpec((rows_t, cols), lambda i, p: (p[1] * steps + i, 0))
    else:
        own_spec = pl.BlockSpec((rows_t, cols), lambda i, p: (i, p[1]))
    got_spec = pl.BlockSpec((N_CHIPS - 1, rows_t, cols), lambda i, p: (0, i, 0))
    shape = [depth, rows, cols]
    shape[1 + core_axis] *= 2
    if core_axis == 0:
        out_spec = pl.BlockSpec((None, rows_t, cols), lambda i, p: (layer, p[0] * steps + i, 0))
    else:
        out_spec = pl.BlockSpec((None, rows_t, cols), lambda i, p: (layer, i, p[0]))

    def body(place_ref, own_ref, got_ref, *rest):
        out_ref = rest[-1]
        up = lambda val: val.astype(F32)
        out_ref[...] = ((up(own_ref[...]) + up(got_ref[0])) + up(got_ref[1])) + up(got_ref[2])

    in_specs, operands, aliases = [own_spec, got_spec], [place, partial, got], {}
    if blocks is not None:
        in_specs.append(pl.BlockSpec(memory_space=pl.ANY))
        operands.append(blocks)
        aliases = {3: 0}
    return pl.pallas_call(
        body, name=name, out_shape=jax.ShapeDtypeStruct(tuple(shape), F32),
        grid_spec=pltpu.PrefetchScalarGridSpec(num_scalar_prefetch=1, grid=(steps,), in_specs=in_specs, out_specs=out_spec),
        input_output_aliases=aliases, compiler_params=_cparams(("arbitrary",)),
    )(*operands)


def _sum_devices(stack, *, name):
    def body(src_ref, out_ref):
        total = src_ref[0]
        for d in range(1, stack.shape[0]):
            total = total + src_ref[d]
        out_ref[...] = total

    return pl.pallas_call(body, name=name, out_shape=jax.ShapeDtypeStruct(stack.shape[1:], F32))(stack)


def kernel(x, w_in, b_in, conv_w, conv_b, conv_ln_g, conv_ln_b, w_att_proj, w_conv_proj, b_conv_proj, w_out, ln_g, ln_b, loss_target, m_w_in, m_b_in, m_conv_w, m_conv_b, m_conv_ln_g, m_conv_ln_b, m_w_att_proj, m_w_conv_proj, m_b_conv_proj, m_w_out, m_ln_g, m_ln_b, v_w_in, v_b_in, v_conv_w, v_conv_b, v_conv_ln_g, v_conv_ln_b, v_w_att_proj, v_w_conv_proj, v_b_conv_proj, v_w_out, v_ln_g, v_ln_b):
    depth = w_in.shape[0]
    d_model = x.shape[-1]
    half = d_model // 2
    chip = 2 * lax.axis_index("x") + lax.axis_index("y")
    place = jnp.stack([lax.axis_index("c"), chip]).astype(jnp.int32)
    vec3 = lambda v: v.reshape(depth, 1, -1)

    taps = jnp.pad(conv_w, ((0, 0), (0, CONV_PAD - CONV_WIDTH), (0, 0)))
    gathered = [("w_in", w_in, 2, BF16), ("w_att_proj", w_att_proj, 2, BF16), ("w_conv_proj", w_conv_proj, 2, BF16),
                ("w_out", w_out, 1, BF16), ("conv_w", taps, 2, F32)]
    wholes = [_place_block(arr, axis, place, dtype, name="place_" + n) for n, arr, axis, dtype in gathered]
    loss, grad_x, packed, waiting, blocks = _train_pass(x[0], loss_target[0], wholes[:4], wholes[4], b_in, vec3(conv_b),
                                                        vec3(conv_ln_g), vec3(conv_ln_b), vec3(b_conv_proj), vec3(ln_g), vec3(ln_b), place)
    loss = lax.psum(loss[0, 0], ("x", "y", "c"))

    names = ["w_in", "b_in", "conv_w", "conv_b", "conv_ln_g", "conv_ln_b", "w_att_proj", "w_conv_proj", "b_conv_proj", "w_out", "ln_g", "ln_b"]
    weights = dict(zip(names, (w_in, b_in, conv_w, conv_b, conv_ln_g, conv_ln_b, w_att_proj, w_conv_proj, b_conv_proj, w_out, ln_g, ln_b)))
    first = dict(zip(names, (m_w_in, m_b_in, m_conv_w, m_conv_b, m_conv_ln_g, m_conv_ln_b, m_w_att_proj, m_w_conv_proj, m_b_conv_proj, m_w_out, m_ln_g, m_ln_b)))
    second = dict(zip(names, (v_w_in, v_b_in, v_conv_w, v_conv_b, v_conv_ln_g, v_conv_ln_b, v_w_att_proj, v_w_conv_proj, v_b_conv_proj, v_w_out, v_ln_g, v_ln_b)))
    flat = lambda arr: arr.reshape(-1, arr.shape[-1])

    *got, gathered = _lone_call(_both(_exchange_rider(waiting, MATRIX_CHIP_AXIS), _small_rider(packed)), name="exchange_last")
    blocks = _lone_call(_join_rider(_chip_sums(0, depth, waiting, got, blocks, place), [axis + 1 for axis in MATRIX_CORE_AXIS], 0, 1),
                        name="pair_join")
    reduced = dict(zip(MATRICES, blocks))

    total = _sum_devices(gathered, name="sum_devices")
    widths = dict(b_in=b_in.shape[1], conv_b=half, conv_ln_g=half, conv_ln_b=half, b_conv_proj=d_model, ln_g=d_model, ln_b=d_model)
    offset = 0
    for n in VECTORS:
        reduced[n] = total[:, offset:offset + widths[n]]
        offset += widths[n]
    taps = total[:, offset:].reshape(depth, CONV_WIDTH, half)
    reduced["conv_w"] = lax.dynamic_slice_in_dim(taps, chip * conv_w.shape[2], conv_w.shape[2], axis=2)

    delta, new_m, new_v = {}, {}, {}
    for n in names:
        shape = weights[n].shape
        *grad, d, m, v = _adamw(flat(weights[n]), flat(reduced[n]), flat(first[n]), flat(second[n]), name="adamw_" + n,
                                echo=n in MATRICES)
        if grad:
            reduced[n] = grad[0]
        delta[n], new_m[n], new_v[n] = d.reshape(shape), m.reshape(shape), v.reshape(shape)
    return (loss, grad_x[None], *[reduced[n].reshape(weights[n].shape) for n in names], *[delta[n] for n in names],
            *[new_m[n] for n in names], *[new_v[n] for n in names])
```

```python
import functools
from typing import Callable, NamedTuple

import jax
import jax.numpy as jnp
from jax import lax
from jax.experimental import pallas as pl
from jax.experimental.pallas import tpu as pltpu

F32 = jnp.float32
BF16 = jnp.bfloat16

HEAD_DIM = 64
LANES = 128
CONV_WIDTH = 31
CONV_PAD = 32
SUBLANES = 8
LN_EPS = 1e-5
DEPTH = 4
DEEPNORM_ALPHA = (2 * DEPTH) ** 0.25
ATT_SCALE = HEAD_DIM ** -0.5
ATT_TILE = 128
ATT_DEAD = -88.0
ATT_GROUP = 2
ATT_GROUP_FWD = 8
ATT_FILL = -1e30

ADAM_LR = 0.001
ADAM_B1 = 0.9
ADAM_B2 = 0.999
ADAM_EPS = 1e-08
ADAM_WD = 0.01
ADAM_STEP = 10

VMEM_LIMIT = 56 * 1024 * 1024

N_CHIPS = 4


def _cparams(sem):
    return pltpu.CompilerParams(dimension_semantics=sem, vmem_limit_bytes=VMEM_LIMIT)


def _sigmoid(x):
    return 1.0 / (1.0 + jnp.exp(-x))


class _Rider(NamedTuple):
    operands: list
    out_shape: list
    aliases: dict
    scratch: list
    start: Callable
    finish: Callable


def _host_call(body, rider, *, name, grid, in_specs, out_specs, out_shape, scratch, operands, semantics, host_aliases=None):
    n_in, n_out = len(in_specs), len(out_specs)
    aliases = dict(host_aliases or {})
    if rider is not None:
        r_in, r_out = len(rider.operands), len(rider.out_shape)
        host_body = body

        def body(*refs):
            base = n_in + r_in
            ins, rins = refs[:n_in], refs[n_in:base]
            outs, routs = refs[base:base + n_out], refs[base + n_out:base + n_out + r_out]
            rest = refs[base + n_out + r_out:]
            split = len(rest) - len(rider.scratch)
            ids = [pl.program_id(axis) for axis in range(len(grid))]
            first = functools.reduce(jnp.logical_and, [i == 0 for i in ids])
            last = functools.reduce(jnp.logical_and, [i == g - 1 for i, g in zip(ids, grid)])

            @pl.when(first)
            def _():
                rider.start(rins, routs, rest[split:])

            host_body(*ins, *outs, *rest[:split])

            @pl.when(last)
            def _():
                rider.finish(rins, routs, rest[split:])

        hbm = pl.BlockSpec(memory_space=pl.ANY)
        in_specs = list(in_specs) + [hbm] * r_in
        out_specs = list(out_specs) + [hbm] * r_out
        out_shape = list(out_shape) + list(rider.out_shape)
        scratch = list(scratch) + list(rider.scratch)
        operands = list(operands) + list(rider.operands)
        aliases.update({n_in + i: n_out + o for i, o in rider.aliases.items()})
    res = pl.pallas_call(
        body, name=name, grid=grid, in_specs=list(in_specs), out_specs=list(out_specs), out_shape=list(out_shape),
        scratch_shapes=list(scratch), input_output_aliases=aliases, compiler_params=_cparams(semantics),
    )(*operands)
    return list(res[:n_out]), list(res[n_out:])


def _lone_call(rider, *, name):
    r_in = len(rider.operands)

    def body(*refs):
        ins, outs, sems = refs[:r_in], refs[r_in:r_in + len(rider.out_shape)], refs[r_in + len(rider.out_shape):]
        rider.start(ins, outs, sems)
        rider.finish(ins, outs, sems)

    hbm = pl.BlockSpec(memory_space=pl.ANY)
    return list(pl.pallas_call(
        body, name=name, in_specs=[hbm] * r_in, out_specs=[hbm] * len(rider.out_shape), out_shape=list(rider.out_shape),
        scratch_shapes=list(rider.scratch), input_output_aliases=dict(rider.aliases),
    )(*rider.operands))


def _fit(tile, dim):
    assert dim % LANES == 0
    tile = min(tile, dim) // LANES * LANES
    while dim % tile:
        tile -= LANES
    return tile


_DIMS = {"nn": ((1,), (0,)), "nt": ((1,), (1,)), "tn": ((0,), (0,))}


def _matmul(a, b, *, mode, name, layer=None, bias=None, add=None, colsum=False, out_dtype=F32, tm=1024, tn=512, tk=1024,
            rider=None):
    b_shape = b.shape if layer is None else b.shape[1:]
    if mode == "nn":
        (m, k), (k2, n) = a.shape, b_shape
    elif mode == "nt":
        (m, k), (n, k2) = a.shape, b_shape
    else:
        (k, m), (k2, n) = a.shape, b_shape
    assert k == k2
    tm, tn, tk = _fit(tm, m), _fit(tn, n), _fit(tk, k)
    gm, gn, nk = m // tm, n // tn, k // tk

    a_spec = pl.BlockSpec((tk, tm), lambda i, j, kk: (kk, i)) if mode == "tn" else pl.BlockSpec((tm, tk), lambda i, j, kk: (i, kk))
    if layer is None:
        b_spec = pl.BlockSpec((tn, tk), lambda i, j, kk: (j, kk)) if mode == "nt" else pl.BlockSpec((tk, tn), lambda i, j, kk: (kk, j))
    elif mode == "nt":
        b_spec = pl.BlockSpec((None, tn, tk), lambda i, j, kk: (layer, j, kk))
    else:
        b_spec = pl.BlockSpec((None, tk, tn), lambda i, j, kk: (layer, kk, j))
    in_specs, operands = [a_spec, b_spec], [a, b]
    if bias is not None:
        in_specs.append(pl.BlockSpec((1, tn), lambda i, j, kk: (0, j)))
        operands.append(bias)
    if add is not None:
        in_specs.append(pl.BlockSpec((tm, tn), lambda i, j, kk: (i, j)))
        operands.append(add)
    out_shape = [jax.ShapeDtypeStruct((m, n), out_dtype)]
    out_specs = [pl.BlockSpec((tm, tn), lambda i, j, kk: (i, j))]
    scratch = [pltpu.VMEM((tm, tn), F32)] if nk > 1 else []
    if colsum:
        assert mode == "tn"
        out_shape.append(jax.ShapeDtypeStruct((gm, 1, n), F32))
        out_specs.append(pl.BlockSpec((1, 1, tn), lambda i, j, kk: (i, 0, j)))
        if nk > 1:
            scratch.append(pltpu.VMEM((1, tn), F32))
    has_bias, has_add = bias is not None, add is not None

    def body(*refs):
        refs = list(refs)
        a_ref, b_ref = refs[0], refs[1]
        pos = 2
        bias_ref = add_ref = None
        if has_bias:
            bias_ref = refs[pos]
            pos += 1
        if has_add:
            add_ref = refs[pos]
            pos += 1
        o_ref = refs[pos]
        pos += 1
        cs_ref = None
        if colsum:
            cs_ref = refs[pos]
            pos += 1

        def finish(out, sums):
            if has_bias:
                out = out + bias_ref[...]
            if has_add:
                out = out + add_ref[...]
            o_ref[...] = out.astype(out_dtype)
            if colsum:
                cs_ref[0] = sums

        bv = b_ref[...]
        prod = lax.dot_general(a_ref[...].astype(BF16), bv.astype(BF16), (_DIMS[mode], ((), ())), preferred_element_type=F32)
        sums = jnp.sum(bv.astype(F32), axis=0, keepdims=True) if colsum else None
        if nk == 1:
            finish(prod, sums)
            return
        acc_ref = refs[pos]
        cs_acc = refs[pos + 1] if colsum else None
        kk = pl.program_id(2)

        @pl.when(kk == 0)
        def _():
            acc_ref[...] = jnp.zeros_like(acc_ref)
            if colsum:
                cs_acc[...] = jnp.zeros_like(cs_acc)

        acc_ref[...] += prod
        if colsum:
            cs_acc[...] += sums

        @pl.when(kk == nk - 1)
        def _():
            finish(acc_ref[...], cs_acc[...] if colsum else None)

    res, rode = _host_call(body, rider, name=name, grid=(gm, gn, nk), in_specs=in_specs, out_specs=out_specs, out_shape=out_shape,
                           scratch=scratch, operands=operands, semantics=("arbitrary", "arbitrary", "arbitrary"))
    out = (res[0], res[1][0]) if colsum else res[0]
    return out if rider is None else (out, rode)


def _scan_matrices():
    t = ATT_TILE
    r = lax.broadcasted_iota(jnp.int32, (t, t), 0)
    c = lax.broadcasted_iota(jnp.int32, (t, t), 1)
    ones = jnp.ones((t, t), F32)
    suffix = jnp.concatenate([(r > c).astype(F32), ones], axis=1)
    prefix = jnp.concatenate([(r < c).astype(F32), ones], axis=1)
    stack = lambda mat: jnp.concatenate([mat, mat], axis=0).astype(BF16)
    return stack(suffix), stack(prefix)


def _split_halves(val):
    hi = val.astype(BF16)
    lo = (val - hi.astype(F32)).astype(BF16)
    return jnp.concatenate([hi, lo], axis=1)


def _split_scan(val, mat_ref):
    return jnp.dot(_split_halves(val), mat_ref[...], preferred_element_type=F32)


def _pair_scores(q, k_lo, k_hi, masked):
    t = ATT_TILE
    z2 = lax.dot_general(q, jnp.concatenate([k_lo, k_hi], axis=0), (((1,), (1,)), ((), ())), preferred_element_type=F32)
    out = []
    for h in range(2):
        z = z2[:, h * t:(h + 1) * t]
        sp = jnp.log(1.0 + jnp.exp(-jnp.abs(z)))
        f = jnp.minimum(-z, 0.0) - sp
        a = f + z
        if masked:
            causal = lax.broadcasted_iota(jnp.int32, (t, t), 1) < lax.broadcasted_iota(jnp.int32, (t, t), 0)
            f = jnp.where(causal, f, 0.0)
        out.append((_split_halves(f), a))
    return out


def _any_alive(runs):
    top = functools.reduce(jnp.maximum, [run for per_head in runs for run in per_head])
    return (jnp.max(top) > ATT_DEAD).astype(jnp.int32)


def _head_copies(seq, src_ref, scale, lo_ref, hi_ref, plain_ref):
    chunk = min(256, seq)
    low = lax.broadcasted_iota(jnp.int32, (chunk, LANES), 1) < HEAD_DIM

    def step(r, carry):
        rows = pl.ds(pl.multiple_of(r * chunk, chunk), chunk)
        val = src_ref[rows, :]
        if scale != 1.0:
            val = val * scale
        if lo_ref is not None:
            lo_ref[rows, :] = jnp.where(low, val, 0.0).astype(BF16)
            hi_ref[rows, :] = jnp.where(low, 0.0, val).astype(BF16)
        if plain_ref is not None:
            plain_ref[rows, :] = val.astype(BF16)
        return carry

    lax.fori_loop(0, seq // chunk, step, 0)


def _attn_fwd(u, scan_suffix, *, seq, d_att, name, rider=None):
    t = ATT_TILE
    nq = seq // t
    pairs = d_att // LANES
    grp = ATT_GROUP_FWD
    assert nq % grp == 0

    def body(q_ref, k_ref, v_ref, um_ref, o_ref, qq, k0, k1, v0, v1, f2_s, a_s, lg_s, tot_s, run_s, acc_s):
        _head_copies(seq, q_ref, ATT_SCALE, None, None, qq)
        _head_copies(seq, k_ref, 1.0, k0, k1, None)
        _head_copies(seq, v_ref, 1.0, v0, v1, None)

        def group(gi, carry):
            qb0 = gi * grp
            qrows = [pl.ds(pl.multiple_of((qb0 + g) * t, t), t) for g in range(grp)]
            qv = [qq[qrows[g], :] for g in range(grp)]

            chains = [(h, g) for g in range(grp) for h in range(2)]

            def key_rows(g, i):
                return pl.ds(pl.multiple_of(jnp.maximum(qb0 + g - i, 0) * t, t), t)

            def stage1(i, masked):
                out = []
                for g in range(grp):
                    krows = key_rows(g, i)
                    out += _pair_scores(qv[g], k0[krows, :], k1[krows, :], masked)
                return out

            def stage2(halves, a, masked):
                scan = jnp.dot(halves, um_ref[...], preferred_element_type=F32)
                logit = a + scan[:, :t]
                if masked:
                    causal = lax.broadcasted_iota(jnp.int32, (t, t), 1) < lax.broadcasted_iota(jnp.int32, (t, t), 0)
                    logit = jnp.where(causal, logit, ATT_FILL)
                return logit, scan[:, t:]

            def put(halves_a=None, logit_total=None):
                for c in range(len(chains)):
                    if halves_a is not None:
                        f2_s[c], a_s[c] = halves_a[c]
                    if logit_total is not None:
                        lg_s[c], tot_s[c] = logit_total[c]

            first = stage1(0, True)
            put(halves_a=stage1(1, False), logit_total=[stage2(f2, a, True) for f2, a in first])
            for c in range(len(chains)):
                run_s[c] = jnp.zeros((t, t), F32)
            for g in range(grp):
                acc_s[g] = jnp.zeros((t, LANES), F32)

            def wbody(st):
                i = st[0]
                held = [(f2_s[c], a_s[c]) for c in range(len(chains))]
                logits = [lg_s[c] for c in range(len(chains))]
                totals = [tot_s[c] for c in range(len(chains))]
                runs = [run_s[c] for c in range(len(chains))]
                accs = [acc_s[g] for g in range(grp)]
                for g in range(grp):
                    krows = key_rows(g, i)
                    gone = jnp.where(qb0 + g - i >= 0, 0.0, ATT_FILL)
                    weights = []
                    for c in (2 * g, 2 * g + 1):
                        run = runs[c] + gone
                        weights.append(jnp.exp(logits[c] + run).astype(BF16))
                        runs[c] = run + totals[c]
                    accs[g] = accs[g] + jnp.dot(jnp.concatenate(weights, axis=1), jnp.concatenate([v0[krows, :], v1[krows, :]], axis=0),
                                                preferred_element_type=F32)
                ahead2 = [stage2(f2, a, False) for f2, a in held]
                ahead1 = stage1(i + 2, False)
                put(halves_a=ahead1, logit_total=ahead2)
                for c in range(len(chains)):
                    run_s[c] = runs[c]
                for g in range(grp):
                    acc_s[g] = accs[g]
                more = jnp.logical_and(i + 1 <= qb0 + grp - 1, _any_alive([runs]) > 0)
                return i + 1, more.astype(jnp.int32)

            lax.while_loop(lambda st: st[1] > 0, wbody, (jnp.int32(0), jnp.int32(1)))
            for g in range(grp):
                o_ref[qrows[g], :] = acc_s[g]
            return carry

        lax.fori_loop(0, nq // grp, group, 0)

    blk = lambda base: pl.BlockSpec((seq, LANES), lambda j, base=base: (0, base + j))
    res, rode = _host_call(
        body, rider, name=name, grid=(pairs,),
        in_specs=[blk(0), blk(pairs), blk(2 * pairs), pl.BlockSpec((2 * t, 2 * t), lambda j: (0, 0))],
        out_specs=[pl.BlockSpec((seq, LANES), lambda j: (0, j))],
        out_shape=[jax.ShapeDtypeStruct((seq, d_att), F32)],
        scratch=[pltpu.VMEM((seq, LANES), BF16)] * 5 + [pltpu.VMEM((2 * grp, t, 2 * t), BF16)]
        + [pltpu.VMEM((2 * grp, t, t), F32)] * 4 + [pltpu.VMEM((grp, t, LANES), F32)],
        operands=[u, u, u, scan_suffix], semantics=("arbitrary",))
    return res[0] if rider is None else (res[0], rode)


def _columns_copy(stage_ref, du_ref, rows, col, sem):
    width = stage_ref.shape[-1]
    cols = pl.ds(pl.multiple_of(col, LANES), width)
    return pltpu.make_async_copy(stage_ref, du_ref.at[slice(None) if rows is None else rows, cols], sem)


def _attn_bwd(u, d_att_out, du, scan_suffix, scan_prefix, *, seq, d_att, name, rider=None):
    t = ATT_TILE
    nq = seq // t
    pairs = d_att // LANES
    grp = ATT_GROUP
    assert nq % grp == 0

    def body(q_ref, k_ref, v_ref, do_ref, um_ref, pm_ref, du_old, du_ref,
             qq, q0, q1, k0, k1, v0, v1, dd, do0, do1, dq_acc, dk_acc, dv_acc, g_st, b_st,
             f2_s, a_s, lg_s, tot_s, dw_s, run_s, p_s, pt_s, grun_s, dq_s, dq_o, dk_o, dv_o, out_sem):
        _head_copies(seq, q_ref, ATT_SCALE, q0, q1, qq)
        _head_copies(seq, k_ref, 1.0, k0, k1, None)
        _head_copies(seq, v_ref, 1.0, v0, v1, None)
        _head_copies(seq, do_ref, 1.0, do0, do1, dd)
        dk_acc[...] = jnp.zeros_like(dk_acc)
        dv_acc[...] = jnp.zeros_like(dv_acc)

        tn = (((0,), (0,)), ((), ()))
        nt = (((1,), (1,)), ((), ()))

        def stacked(lo_ref, hi_ref, rows):
            return jnp.concatenate([lo_ref[rows, :], hi_ref[rows, :]], axis=0)

        def group(gi, carry):
            qb0 = gi * grp
            qrows = [pl.ds(pl.multiple_of((qb0 + g) * t, t), t) for g in range(grp)]
            qv = [qq[qrows[g], :] for g in range(grp)]
            dov = [dd[qrows[g], :] for g in range(grp)]
            q_heads = [stacked(q0, q1, qrows[g]) for g in range(grp)]
            do_heads = [stacked(do0, do1, qrows[g]) for g in range(grp)]

            def key_rows(g, i):
                return pl.ds(pl.multiple_of(jnp.maximum(qb0 + g - i, 0) * t, t), t)

            chains = [(h, g) for g in range(grp) for h in range(2)]
            every = range(len(chains))

            def stage1(i, masked):
                out = []
                for g in range(grp):
                    krows = key_rows(g, i)
                    out += _pair_scores(qv[g], k0[krows, :], k1[krows, :], masked)
                return out

            def stage2(i, held, masked):
                out = []
                for g in range(grp):
                    dw2 = lax.dot_general(dov[g], stacked(v0, v1, key_rows(g, i)), nt, preferred_element_type=F32)
                    for h in range(2):
                        c = 2 * g + h
                        halves, a = held[c]
                        scan = jnp.dot(halves, um_ref[...], preferred_element_type=F32)
                        logit = a + scan[:, :t]
                        if masked:
                            causal = lax.broadcasted_iota(jnp.int32, (t, t), 1) < lax.broadcasted_iota(jnp.int32, (t, t), 0)
                            logit = jnp.where(causal, logit, ATT_FILL)
                        b_st[c, i] = jnp.exp(a)
                        out.append((logit, scan[:, t:], dw2[:, h * t:(h + 1) * t]))
                return out

            def put(held=None, ready=None):
                for c in every:
                    if held is not None:
                        f2_s[c], a_s[c] = held[c]
                    if ready is not None:
                        lg_s[c], tot_s[c], dw_s[c] = ready[c]

            put(held=stage1(1, False), ready=stage2(0, stage1(0, True), True))
            for c in every:
                run_s[c] = jnp.zeros((t, t), F32)

            def wbody(st):
                i = st[0]
                held = [(f2_s[c], a_s[c]) for c in every]
                ready = [(lg_s[c], tot_s[c], dw_s[c]) for c in every]
                runs = [run_s[c] for c in every]
                for g in range(grp):
                    gone = jnp.where(qb0 + g - i >= 0, 0.0, ATT_FILL)
                    weights = []
                    for c in (2 * g, 2 * g + 1):
                        logit, total, dw = ready[c]
                        run = runs[c] + gone
                        w = jnp.exp(logit + run)
                        g_st[c, i] = w * dw
                        weights.append(w.astype(BF16))
                        runs[c] = run + total
                    dv_acc[key_rows(g, i), :] += lax.dot_general(jnp.concatenate(weights, axis=0), do_heads[g], tn,
                                                                 preferred_element_type=F32)
                ahead2 = stage2(i + 1, held, False)
                ahead1 = stage1(i + 2, False)
                put(held=ahead1, ready=ahead2)
                for c in every:
                    run_s[c] = runs[c]
                more = jnp.logical_and(i + 1 <= qb0 + grp - 1, _any_alive([runs]) > 0)
                return i + 1, more.astype(jnp.int32)

            steps = lax.while_loop(lambda st: st[1] > 0, wbody, (jnp.int32(0), jnp.int32(1)))[0]

            def prefix(i):
                out = []
                for c in every:
                    scan = _split_scan(g_st[c, i], pm_ref)
                    out.append((scan[:, :t], scan[:, t:]))
                return out

            def back(i, masked):
                sums = [(p_s[c], pt_s[c]) for c in every]
                gruns = [grun_s[c] for c in every]
                dqs = [dq_s[g] for g in range(grp)]
                for g in range(grp):
                    krows = key_rows(g, i)
                    dzs = []
                    for c in (2 * g, 2 * g + 1):
                        gt = g_st[c, i]
                        dz = gt - b_st[c, i] * (gt + gruns[c] + sums[c][0])
                        if masked:
                            causal = lax.broadcasted_iota(jnp.int32, (t, t), 1) < lax.broadcasted_iota(jnp.int32, (t, t), 0)
                            dz = jnp.where(causal, dz, 0.0)
                        dzs.append(dz.astype(BF16))
                        gruns[c] = gruns[c] + sums[c][1]
                    dqs[g] = dqs[g] + jnp.dot(jnp.concatenate(dzs, axis=1), stacked(k0, k1, krows), preferred_element_type=F32)
                    dk_acc[krows, :] += lax.dot_general(jnp.concatenate(dzs, axis=0), q_heads[g], tn, preferred_element_type=F32)
                return gruns, dqs

            def keep(sums=None, gruns=None, dqs=None):
                for c in every:
                    if sums is not None:
                        p_s[c], pt_s[c] = sums[c]
                    if gruns is not None:
                        grun_s[c] = gruns[c]
                if dqs is not None:
                    for g in range(grp):
                        dq_s[g] = dqs[g]

            keep(sums=prefix(steps - 1), gruns=[jnp.zeros((t, t), F32)] * len(chains), dqs=[jnp.zeros((t, LANES), F32)] * grp)

            def bbody(j, carry2):
                i = steps - 1 - j
                gruns, dqs = back(i, False)
                keep(sums=prefix(i - 1), gruns=gruns, dqs=dqs)
                return carry2

            lax.fori_loop(0, steps - 1, bbody, 0)
            _, dqs = back(0, True)
            for g in range(grp):
                dq_acc[qrows[g], :] = dqs[g]
            return carry

        lax.fori_loop(0, nq // grp, group, 0)

        pair = pl.program_id(0)
        stages = (dq_o, dk_o, dv_o)

        def copies(j):
            return [_columns_copy(stage, du_ref, None, k * d_att + j * LANES, out_sem.at[k]) for k, stage in enumerate(stages)]

        @pl.when(pair > 0)
        def _():
            for cp in copies(pair - 1):
                cp.wait()

        chunk = min(256, seq)

        def emit(r, carry):
            rows = pl.ds(pl.multiple_of(r * chunk, chunk), chunk)
            dq_o[rows, :] = (dq_acc[rows, :] * ATT_SCALE).astype(BF16)
            dk_o[rows, :] = dk_acc[rows, :].astype(BF16)
            dv_o[rows, :] = dv_acc[rows, :].astype(BF16)
            return carry

        lax.fori_loop(0, seq // chunk, emit, 0)
        for cp in copies(pair):
            cp.start()

        @pl.when(pair == pairs - 1)
        def _():
            for cp in copies(pair):
                cp.wait()

    blk = lambda base: pl.BlockSpec((seq, LANES), lambda j, base=base: (0, base + j), pipeline_mode=pl.Buffered(1))
    mat = pl.BlockSpec((2 * t, 2 * t), lambda j: (0, 0))
    hbm = pl.BlockSpec(memory_space=pl.ANY)
    res, rode = _host_call(
        body, rider, name=name, grid=(pairs,),
        in_specs=[blk(0), blk(pairs), blk(2 * pairs), blk(0), mat, mat, hbm],
        out_specs=[hbm], out_shape=[jax.ShapeDtypeStruct(du.shape, du.dtype)], host_aliases={6: 0},
        scratch=[pltpu.VMEM((seq, LANES), BF16)] * 10 + [pltpu.VMEM((seq, LANES), F32)] * 3
        + [pltpu.VMEM((2 * grp, nq + 1, t, t), F32)] * 2 + [pltpu.VMEM((2 * grp, t, 2 * t), BF16)]
        + [pltpu.VMEM((2 * grp, t, t), F32)] * 8 + [pltpu.VMEM((grp, t, LANES), F32)]
        + [pltpu.VMEM((seq, LANES), BF16)] * 3 + [pltpu.SemaphoreType.DMA((3,))],
        operands=[u, u, u, d_att_out, scan_suffix, scan_prefix, du], semantics=("arbitrary",))
    return res[0] if rider is None else (res[0], rode)


CONV_ROWS = 256


def _shifted(window, residue, rows):
    total = rows + CONV_PAD
    return window if residue == 0 else pltpu.roll(window, total - residue, 0)


def _glu_to_pad(seq, a_ref, b_ref, pad_ref):
    chunk = min(CONV_ROWS, seq)
    pad_ref[pl.ds(0, CONV_PAD), :] = jnp.zeros((CONV_PAD, LANES), F32)

    def step(r, carry):
        rows = pl.ds(pl.multiple_of(r * chunk, chunk), chunk)
        pad_ref[pl.ds(pl.multiple_of(r * chunk + CONV_PAD, SUBLANES), chunk), :] = a_ref[rows, :] * _sigmoid(b_ref[rows, :])
        return carry

    lax.fori_loop(0, seq // chunk, step, 0)


def _conv_fwd(u, conv_w, conv_b, *, layer, seq, d_conv, col_a, col_b, name, rider=None):
    blocks = d_conv // LANES
    rows_t = min(CONV_ROWS, seq)
    shift0 = CONV_PAD - (CONV_WIDTH - 1)

    def body(a_ref, b_ref, w_ref, bias_ref, o_ref, pad_ref):
        _glu_to_pad(seq, a_ref, b_ref, pad_ref)

        def step(r, carry):
            base = pl.multiple_of(r * rows_t, rows_t)
            window = pad_ref[pl.ds(base, rows_t + CONV_PAD), :]
            acc = jnp.zeros((rows_t, LANES), F32) + bias_ref[...]
            for residue in range(SUBLANES):
                moved = _shifted(window, residue, rows_t)
                for tap in range(CONV_WIDTH):
                    if (shift0 + tap) % SUBLANES == residue:
                        lo = (shift0 + tap) - residue
                        acc = acc + w_ref[tap:tap + 1, :] * moved[lo:lo + rows_t, :]
            o_ref[pl.ds(base, rows_t), :] = acc
            return carry

        lax.fori_loop(0, seq // rows_t, step, 0)

    res, rode = _host_call(
        body, rider, name=name, grid=(blocks,),
        in_specs=[pl.BlockSpec((seq, LANES), lambda j: (0, col_a + j)), pl.BlockSpec((seq, LANES), lambda j: (0, col_b + j)),
                  pl.BlockSpec((None, CONV_PAD, LANES), lambda j: (layer, 0, j)),
                  pl.BlockSpec((None, 1, LANES), lambda j: (layer, 0, j))],
        out_specs=[pl.BlockSpec((seq, LANES), lambda j: (0, j))],
        out_shape=[jax.ShapeDtypeStruct((seq, d_conv), F32)],
        scratch=[pltpu.VMEM((seq + CONV_PAD, LANES), F32)],
        operands=[u, u, conv_w, conv_b], semantics=("arbitrary",))
    return res[0] if rider is None else (res[0], rode)


def _conv_bwd(u, dc1, du, conv_w, *, layer, seq, d_conv, col_a, col_b, name):
    blocks = d_conv // LANES
    rows_t = min(CONV_ROWS, seq)
    shift0 = CONV_PAD - (CONV_WIDTH - 1)

    def body(a_ref, b_ref, d_ref, w_ref, du_old, du_ref, dw_ref, pad_ref, dpad_ref, dw_acc, da_ref, db_ref, out_sem):
        block = pl.program_id(0)

        def copies(j):
            return [_columns_copy(stage, du_ref, None, (base + j) * LANES, out_sem.at[k])
                    for k, (stage, base) in enumerate(((da_ref, col_a), (db_ref, col_b)))]

        _glu_to_pad(seq, a_ref, b_ref, pad_ref)
        dpad_ref[pl.ds(seq, CONV_PAD), :] = jnp.zeros((CONV_PAD, LANES), F32)

        def fill(r, carry):
            rows = pl.ds(pl.multiple_of(r * rows_t, rows_t), rows_t)
            dpad_ref[rows, :] = d_ref[rows, :]
            return carry

        lax.fori_loop(0, seq // rows_t, fill, 0)
        dw_acc[...] = jnp.zeros_like(dw_acc)

        @pl.when(block > 0)
        def _():
            for cp in copies(block - 1):
                cp.wait()

        def step(r, carry):
            base = pl.multiple_of(r * rows_t, rows_t)
            rows = pl.ds(base, rows_t)
            window = dpad_ref[pl.ds(base, rows_t + CONV_PAD), :]
            acc = jnp.zeros((rows_t, LANES), F32)
            for residue in range(SUBLANES):
                moved = _shifted(window, residue, rows_t)
                for tap in range(CONV_WIDTH):
                    off = CONV_WIDTH - 1 - tap
                    if off % SUBLANES == residue:
                        lo = off - residue
                        acc = acc + w_ref[tap:tap + 1, :] * moved[lo:lo + rows_t, :]
            sig = _sigmoid(b_ref[rows, :])
            a = a_ref[rows, :]
            da_ref[rows, :] = (acc * sig).astype(BF16)
            db_ref[rows, :] = (acc * a * sig * (1.0 - sig)).astype(BF16)
            dcur = d_ref[rows, :]
            cwin = pad_ref[pl.ds(base, rows_t + CONV_PAD), :]
            for residue in range(SUBLANES):
                moved = _shifted(cwin, residue, rows_t)
                for tap in range(CONV_WIDTH):
                    if (shift0 + tap) % SUBLANES == residue:
                        lo = (shift0 + tap) - residue
                        prod = dcur * moved[lo:lo + rows_t, :]
                        dw_acc[tap] += jnp.sum(prod.reshape(rows_t // SUBLANES, SUBLANES, LANES), axis=0)
            return carry

        lax.fori_loop(0, seq // rows_t, step, 0)
        dw_ref[...] = jnp.sum(dw_acc[...], axis=1)
        for cp in copies(block):
            cp.start()

        @pl.when(block == blocks - 1)
        def _():
            for cp in copies(block):
                cp.wait()

    col = lambda base: pl.BlockSpec((seq, LANES), lambda j, base=base: (0, base + j))
    own = pl.BlockSpec((seq, LANES), lambda j: (0, j))
    hbm = pl.BlockSpec(memory_space=pl.ANY)
    return pl.pallas_call(
        body, name=name, grid=(blocks,),
        in_specs=[col(col_a), col(col_b), own, pl.BlockSpec((None, CONV_PAD, LANES), lambda j: (layer, 0, j)), hbm],
        out_specs=[hbm, pl.BlockSpec((CONV_PAD, LANES), lambda j: (0, j))],
        out_shape=[jax.ShapeDtypeStruct(du.shape, du.dtype), jax.ShapeDtypeStruct((CONV_PAD, d_conv), F32)],
        scratch_shapes=[pltpu.VMEM((seq + CONV_PAD, LANES), F32), pltpu.VMEM((seq + CONV_PAD, LANES), F32),
                        pltpu.VMEM((CONV_PAD, SUBLANES, LANES), F32), pltpu.VMEM((seq, LANES), BF16), pltpu.VMEM((seq, LANES), BF16),
                        pltpu.SemaphoreType.DMA((2,))],
        input_output_aliases={4: 0}, compiler_params=_cparams(("arbitrary",)),
    )(u, u, dc1, conv_w, du)


MIX_ROWS = 256


def _layer_norm_stats(val):
    mu = jnp.mean(val, axis=-1, keepdims=True)
    cen = val - mu
    var = jnp.mean(cen * cen, axis=-1, keepdims=True)
    rstd = lax.rsqrt(var + LN_EPS)
    return cen * rstd, rstd


def _layer_norm_bwd(dy, xhat, rstd, gain):
    dxhat = dy * gain
    m1 = jnp.mean(dxhat, axis=-1, keepdims=True)
    m2 = jnp.mean(dxhat * xhat, axis=-1, keepdims=True)
    dx = rstd * (dxhat - m1 - xhat * m2)
    return dx, jnp.sum(dy * xhat, axis=0, keepdims=True), jnp.sum(dy, axis=0, keepdims=True)


def _mix_forward(zatt, att, c1, zconv, gatt, gconv, x, w_att, w_conv, w_out, cln_g, cln_b, b_proj):
    s_zatt = _sigmoid(zatt)
    a_in = att * (zatt * s_zatt)
    chat, c_rstd = _layer_norm_stats(c1)
    c2 = chat * cln_g + cln_b
    s_c2 = _sigmoid(c2)
    c3 = c2 * s_c2
    s_zconv = _sigmoid(zconv)
    c_in = c3 * (zconv * s_zconv)
    a_in_b, c_in_b = a_in.astype(BF16), c_in.astype(BF16)
    ab = jnp.dot(a_in_b, w_att, preferred_element_type=F32)
    cb = jnp.dot(c_in_b, w_conv, preferred_element_type=F32) + b_proj
    s_gatt, s_gconv = _sigmoid(gatt), _sigmoid(gconv)
    merged_b = (s_gatt * ab + s_gconv * cb).astype(BF16)
    y = jnp.dot(merged_b, w_out, preferred_element_type=F32)
    h = DEEPNORM_ALPHA * x + y
    return dict(s_zatt=s_zatt, a_in_b=a_in_b, chat=chat, c_rstd=c_rstd, c2=c2, s_c2=s_c2, c3=c3, s_zconv=s_zconv,
                c_in_b=c_in_b, ab=ab, cb=cb, s_gatt=s_gatt, s_gconv=s_gconv, merged_b=merged_b, h=h)


def _u_blocks(rows_t, width, half):
    return [pl.BlockSpec((rows_t, half), lambda i, c=c: (i, c)) for c in (3, 6, 7, 8, 9, 10)]


def _of_layer(arr, layer):
    return pl.BlockSpec((None,) + arr.shape[1:], lambda i: (layer, 0, 0))


def _mix_fwd(u, att, c1, x, w_att, w_conv, w_out, cln_g, cln_b, b_proj, ln_g, ln_b, *, layer, seq, d_model, name):
    half = d_model // 2
    rows_t = min(MIX_ROWS, seq)

    def body(zatt_ref, zconv_ref, ga0, ga1, gc0, gc1, att_ref, c1_ref, x_ref, wa_ref, wc_ref, wo_ref,
             cg_ref, cb_ref, bp_ref, lg_ref, lb_ref, o_ref, ob_ref):
        gatt = jnp.concatenate([ga0[...], ga1[...]], axis=1)
        gconv = jnp.concatenate([gc0[...], gc1[...]], axis=1)
        mid = _mix_forward(zatt_ref[...], att_ref[...], c1_ref[...], zconv_ref[...], gatt, gconv, x_ref[...],
                           wa_ref[...], wc_ref[...], wo_ref[...], cg_ref[...], cb_ref[...], bp_ref[...])
        xhat, _ = _layer_norm_stats(mid["h"])
        out = xhat * lg_ref[...] + lb_ref[...]
        o_ref[...] = out
        ob_ref[...] = out.astype(BF16)

    row = lambda width: pl.BlockSpec((rows_t, width), lambda i: (i, 0))
    full = lambda arr: _of_layer(arr, layer)
    out = pl.BlockSpec((rows_t, d_model), lambda i: (i, 0))
    return pl.pallas_call(
        body, name=name, grid=(seq // rows_t,),
        in_specs=_u_blocks(rows_t, d_model, half) + [row(half), row(half), row(d_model), full(w_att), full(w_conv), full(w_out),
                                                     full(cln_g), full(cln_b), full(b_proj), full(ln_g), full(ln_b)],
        out_specs=[out, out],
        out_shape=[jax.ShapeDtypeStruct((seq, d_model), F32), jax.ShapeDtypeStruct((seq, d_model), BF16)],
        compiler_params=_cparams(("parallel",)),
    )(u, u, u, u, u, u, att, c1, x, w_att, w_conv, w_out, cln_g, cln_b, b_proj, ln_g, ln_b)


def _mix_bwd(u, att, c1, x, dxn, w_att, w_conv, w_out, cln_g, cln_b, b_proj, ln_g, *, layer, seq, d_model, name):
    half = d_model // 2
    rows_t = min(MIX_ROWS, seq)
    steps = seq // rows_t
    nt = ((1,), (1,))
    tn = ((0,), (0,))

    def body(zatt_ref, zconv_ref, ga0, ga1, gc0, gc1, att_ref, c1_ref, x_ref, dxn_ref, wa_ref, wc_ref, wo_ref,
             cg_ref, cb_ref, bp_ref, lg_ref,
             du_ref, datt_ref, dc1_ref, dxres_ref, dwa_ref, dwc_ref, dwo_ref,
             dcg_ref, dcb_ref, dcbias_ref, dbp_ref, dlg_ref, dlb_ref, zatt_stage, tail_stage, out_sem):
        sums = (dwa_ref, dwc_ref, dwo_ref, dcg_ref, dcb_ref, dcbias_ref, dbp_ref, dlg_ref, dlb_ref)
        tile = pl.program_id(0)
        slot = tile % 2
        dzatt_ref, tail_ref = zatt_stage.at[slot], tail_stage.at[slot]
        dzconv_ref, dgatt_ref, dgconv_ref = tail_ref.at[:, :half], tail_ref.at[:, half:3 * half], tail_ref.at[:, 3 * half:]

        def copies(i):
            rows = pl.ds(pl.multiple_of(i * rows_t, rows_t), rows_t)
            return [_columns_copy(zatt_stage.at[i % 2], du_ref, rows, 3 * half, out_sem.at[i % 2, 0]),
                    _columns_copy(tail_stage.at[i % 2], du_ref, rows, 6 * half, out_sem.at[i % 2, 1])]

        @pl.when(tile == 0)
        def _():
            for ref in sums:
                ref[...] = jnp.zeros_like(ref)

        zatt, zconv, att = zatt_ref[...], zconv_ref[...], att_ref[...]
        gatt = jnp.concatenate([ga0[...], ga1[...]], axis=1)
        gconv = jnp.concatenate([gc0[...], gc1[...]], axis=1)
        wa, wc, wo = wa_ref[...], wc_ref[...], wo_ref[...]
        mid = _mix_forward(zatt, att, c1_ref[...], zconv, gatt, gconv, x_ref[...], wa, wc, wo,
                           cg_ref[...], cb_ref[...], bp_ref[...])
        xhat, rstd = _layer_norm_stats(mid["h"])
        dh, dlg, dlb = _layer_norm_bwd(dxn_ref[...], xhat, rstd, lg_ref[...])
        dlg_ref[...] += dlg
        dlb_ref[...] += dlb
        dxres_ref[...] = DEEPNORM_ALPHA * dh
        dy = dh.astype(BF16)
        dwo_ref[...] += lax.dot_general(mid["merged_b"], dy, (tn, ((), ())), preferred_element_type=F32)
        dmerged = lax.dot_general(dy, wo, (nt, ((), ())), preferred_element_type=F32)
        s_ga, s_gc, ab, cb = mid["s_gatt"], mid["s_gconv"], mid["ab"], mid["cb"]
        dgatt_ref[...] = (dmerged * ab * s_ga * (1.0 - s_ga)).astype(BF16)
        dgconv_ref[...] = (dmerged * cb * s_gc * (1.0 - s_gc)).astype(BF16)
        dab = dmerged * s_ga
        dcb = dmerged * s_gc
        dbp_ref[...] += jnp.sum(dcb, axis=0, keepdims=True)
        dab_b, dcb_b = dab.astype(BF16), dcb.astype(BF16)
        dwa_ref[...] += lax.dot_general(mid["a_in_b"], dab_b, (tn, ((), ())), preferred_element_type=F32)
        da_in = lax.dot_general(dab_b, wa, (nt, ((), ())), preferred_element_type=F32)
        s_za = mid["s_zatt"]
        datt_ref[...] = da_in * (zatt * s_za)
        dzatt_ref[...] = (da_in * att * (s_za * (1.0 + zatt * (1.0 - s_za)))).astype(BF16)
        dwc_ref[...] += lax.dot_general(mid["c_in_b"], dcb_b, (tn, ((), ())), preferred_element_type=F32)
        dc_in = lax.dot_general(dcb_b, wc, (nt, ((), ())), preferred_element_type=F32)
        s_zc, c2, s_c2 = mid["s_zconv"], mid["c2"], mid["s_c2"]
        dzconv_ref[...] = (dc_in * mid["c3"] * (s_zc * (1.0 + zconv * (1.0 - s_zc)))).astype(BF16)
        dc3 = dc_in * (zconv * s_zc)
        dc2 = dc3 * (s_c2 * (1.0 + c2 * (1.0 - s_c2)))
        dc1, dcg, dcbeta = _layer_norm_bwd(dc2, mid["chat"], mid["c_rstd"], cg_ref[...])
        dcg_ref[...] += dcg
        dcb_ref[...] += dcbeta
        dcbias_ref[...] += jnp.sum(dc1, axis=0, keepdims=True)
        dc1_ref[...] = dc1
        for cp in copies(tile):
            cp.start()

        @pl.when(tile > 0)
        def _():
            for cp in copies(tile - 1):
                cp.wait()

        @pl.when(tile == steps - 1)
        def _():
            for cp in copies(tile):
                cp.wait()

    row = lambda width: pl.BlockSpec((rows_t, width), lambda i: (i, 0))
    full = lambda arr: _of_layer(arr, layer)
    whole = lambda r, c: pl.BlockSpec((r, c), lambda i: (0, 0))
    sds = jax.ShapeDtypeStruct
    out_specs = [pl.BlockSpec(memory_space=pl.ANY), row(half), row(half), row(d_model),
                 whole(half, d_model), whole(half, d_model), whole(d_model, d_model),
                 whole(1, half), whole(1, half), whole(1, half), whole(1, d_model), whole(1, d_model), whole(1, d_model)]
    out_shape = [sds((seq, u.shape[1]), BF16), sds((seq, half), F32), sds((seq, half), F32), sds((seq, d_model), F32),
                 sds((half, d_model), F32), sds((half, d_model), F32), sds((d_model, d_model), F32),
                 sds((1, half), F32), sds((1, half), F32), sds((1, half), F32),
                 sds((1, d_model), F32), sds((1, d_model), F32), sds((1, d_model), F32)]
    return pl.pallas_call(
        body, name=name, grid=(steps,),
        in_specs=_u_blocks(rows_t, d_model, half) + [row(half), row(half), row(d_model), row(d_model), full(w_att), full(w_conv),
                                                     full(w_out), full(cln_g), full(cln_b), full(b_proj), full(ln_g)],
        out_specs=out_specs, out_shape=out_shape,
        scratch_shapes=[pltpu.VMEM((2, rows_t, half), BF16), pltpu.VMEM((2, rows_t, 5 * half), BF16), pltpu.SemaphoreType.DMA((2, 2))],
        compiler_params=_cparams(("arbitrary",)),
    )(u, u, u, u, u, u, att, c1, x, dxn, w_att, w_conv, w_out, cln_g, cln_b, b_proj, ln_g)


def _loss_head(y, target, *, seq, d_model, name):
    rows_t = min(512, seq)

    def body(y_ref, t_ref, dy_ref, loss_ref):
        @pl.when(pl.program_id(0) == 0)
        def _():
            loss_ref[...] = jnp.zeros_like(loss_ref)

        err = y_ref[...] - t_ref[...]
        dy_ref[...] = err * (1.0 / d_model)
        per_token = jnp.sum(err * err, axis=-1, keepdims=True) * (1.0 / d_model)
        loss_ref[...] += 0.5 * jnp.sum(per_token, axis=0, keepdims=True)

    row = pl.BlockSpec((rows_t, d_model), lambda i: (i, 0))
    return pl.pallas_call(
        body, name=name, grid=(seq // rows_t,), in_specs=[row, row],
        out_specs=[row, pl.BlockSpec((1, 1), lambda i: (0, 0))],
        out_shape=[jax.ShapeDtypeStruct((seq, d_model), F32), jax.ShapeDtypeStruct((1, 1), F32)],
        compiler_params=_cparams(("arbitrary",)),
    )(y, target)


def _adamw(w, g, m, v, *, name, echo=False):
    rows, cols = w.shape
    rows_t = rows
    for cand in (512, 256, 128, 64, 32, 16, 8):
        if rows % cand == 0 and cand * cols * 4 <= 2 * 1024 * 1024:
            rows_t = cand
            break
    n_out = 4 if echo else 3

    def body(w_ref, g_ref, m_ref, v_ref, *outs):
        d_ref, nm_ref, nv_ref = outs[-3:]
        grad = g_ref[...]
        if echo:
            outs[0][...] = grad
        new_m = ADAM_B1 * m_ref[...] + (1.0 - ADAM_B1) * grad
        new_v = ADAM_B2 * v_ref[...] + (1.0 - ADAM_B2) * (grad * grad)
        m_hat = new_m / (1.0 - ADAM_B1 ** ADAM_STEP)
        v_hat = new_v / (1.0 - ADAM_B2 ** ADAM_STEP)
        d_ref[...] = -ADAM_LR * (m_hat / (jnp.sqrt(v_hat) + ADAM_EPS) + ADAM_WD * w_ref[...])
        nm_ref[...] = new_m
        nv_ref[...] = new_v

    blk = pl.BlockSpec((rows_t, cols), lambda i: (i, 0))
    out = jax.ShapeDtypeStruct((rows, cols), F32)
    return pl.pallas_call(
        body, name=name, grid=(rows // rows_t,), in_specs=[blk] * 4, out_specs=[blk] * n_out, out_shape=[out] * n_out,
        compiler_params=_cparams(("parallel",)),
    )(w, g, m, v)


MATRICES = ("w_in", "w_att_proj", "w_conv_proj", "w_out")
VECTORS = ("b_in", "conv_b", "conv_ln_g", "conv_ln_b", "b_conv_proj", "ln_g", "ln_b")
MATRIX_CHIP_AXIS = (1, 1, 1, 0)
MATRIX_CORE_AXIS = (0, 0, 0, 1)


def _chip_sums(layer, depth, partials, got, blocks, place):
    return [_chip_sum_layer(partials[t], got[t], blocks[t], layer, depth, MATRIX_CHIP_AXIS[t], MATRIX_CORE_AXIS[t], place,
                            name="chip_sum_" + MATRICES[t]) for t in range(len(MATRICES))]


def _train_pass(x, target, mats, taps, b_in, conv_b, cln_g, cln_b, b_proj, ln_g, ln_b, place):
    seq, d_model = x.shape
    half = d_model // 2
    depth = b_in.shape[0]
    scan_suffix, scan_prefix = _scan_matrices()
    cols = half // LANES
    dims = dict(seq=seq, d_model=d_model)
    conv_dims = dict(seq=seq, d_conv=half, col_a=4 * cols, col_b=5 * cols)
    axes3 = [axis + 1 for axis in MATRIX_CHIP_AXIS]
    n_mat = len(mats)

    def first_layers(relay):
        return _gather_rider([mats[0], taps], [axes3[0], 2], [(0, 1), (0, depth)], relay)

    mats = list(mats)
    mats[0], taps = _lone_call(_chain(first_layers(False), first_layers(True)), name="gather_first")

    xs, xbs, us, atts, c1s = [x], [x.astype(BF16)], [], [], []
    for l in range(depth):
        proj = dict(mode="nn", bias=b_in[l].reshape(1, -1), tm=256, tn=b_in.shape[1], tk=d_model)
        nxt = [(l + 1, 1)] + [(l + 1, 1) if l else (0, 2)] * (n_mat - 1)
        if l + 1 < depth:
            u, mats[1:] = _matmul(xbs[l], mats[0], layer=l, name="in_proj", rider=_gather_rider(mats[1:], axes3[1:], nxt[1:], False),
                                  **proj)
            att, mats[:1] = _attn_fwd(u, scan_suffix, seq=seq, d_att=half, name="attn_fwd",
                                      rider=_gather_rider(mats[:1], axes3[:1], nxt[:1], False))
            c1, mats = _conv_fwd(u, taps, conv_b, layer=l, name="conv_fwd", rider=_gather_rider(mats, axes3, nxt, True), **conv_dims)
        else:
            u = _matmul(xbs[l], mats[0], layer=l, name="in_proj_last", **proj)
            att = _attn_fwd(u, scan_suffix, seq=seq, d_att=half, name="attn_fwd_last")
            c1 = _conv_fwd(u, taps, conv_b, layer=l, name="conv_fwd_last", **conv_dims)
        xn, xnb = _mix_fwd(u, att, c1, xs[l], mats[1], mats[2], mats[3], cln_g, cln_b, b_proj, ln_g, ln_b, layer=l, name="mix_fwd", **dims)
        us.append(u)
        atts.append(att)
        c1s.append(c1)
        xs.append(xn)
        xbs.append(xnb)
    w_in, w_att, w_conv, w_out = mats

    dx, loss = _loss_head(xs[depth], target, name="loss_head", **dims)
    grads = [None] * depth
    blocks = [None] * n_mat
    waiting = None
    for l in reversed(range(depth)):
        u = us[l]
        (du, datt, dc1, dxres, dwa, dwc, dwo, dcg, dcb, dcbias, dbp, dlg, dlb) = _mix_bwd(
            u, atts[l], c1s[l], xs[l], dx, w_att, w_conv, w_out, cln_g, cln_b, b_proj, ln_g, layer=l, name="mix_bwd", **dims)
        if waiting is None:
            du = _attn_bwd(u, datt, du, scan_suffix, scan_prefix, seq=seq, d_att=half, name="attn_bwd_first")
        else:
            du, got = _attn_bwd(u, datt, du, scan_suffix, scan_prefix, seq=seq, d_att=half, name="attn_bwd",
                                rider=_exchange_rider(waiting[1], MATRIX_CHIP_AXIS))
            blocks = _chip_sums(waiting[0], depth, waiting[1], got, blocks, place)
        du, dconvw = _conv_bwd(u, dc1, du, taps, layer=l, name="conv_bwd", **conv_dims)
        dwin, dbin = _matmul(xbs[l], du, mode="tn", colsum=True, name="in_proj_dw", tm=1024, tn=512, tk=seq)
        parts = [dwin, dwa, dwc, dwo]
        swap = _swap_rider(parts, MATRIX_CORE_AXIS)
        if l == 0:
            swap = _both(swap, _join_rider(blocks, [axis + 1 for axis in MATRIX_CORE_AXIS], 1, depth - 1))
        dx, rode = _matmul(du, w_in, layer=l, mode="nt", add=dxres, name="in_proj_dx", tm=512, tn=1024, tk=du.shape[1], rider=swap)
        got, blocks = rode[:n_mat], (rode[n_mat:] if l == 0 else blocks)
        waiting = (l, [_pair_sum_layer(parts[t], got[t], MATRIX_CORE_AXIS[t], place, name="pair_sum_" + MATRICES[t])
                       for t in range(n_mat)])
        grads[l] = dict(b_in=dbin[0], conv_w=dconvw[:CONV_WIDTH], conv_b=dcbias[0], conv_ln_g=dcg[0], conv_ln_b=dcb[0],
                        b_conv_proj=dbp[0], ln_g=dlg[0], ln_b=dlb[0])
    packed = jnp.stack([jnp.concatenate([grads[l][n] for n in VECTORS] + [grads[l]["conv_w"].reshape(-1)]) for l in range(depth)])
    return loss, dx, packed, waiting[1], blocks


MESH = pl.DeviceIdType.MESH


def _position():
    x, y, c = lax.axis_index("x"), lax.axis_index("y"), lax.axis_index("c")
    return x, y, c, [(1 - x, y), (x, 1 - y), (1 - x, 1 - y)]


def _cut(ref, axis, start, size):
    idx = [slice(None)] * len(ref.shape)
    idx[axis] = pl.ds(start, size)
    return ref.at[tuple(idx)]


def _remote(src, dst, send_sem, recv_sem, device):
    return pltpu.make_async_remote_copy(src_ref=src, dst_ref=dst, send_sem=send_sem, recv_sem=recv_sem,
                                        device_id=device, device_id_type=MESH)


def _gather_rider(wholes, chip_axes, spans, relay):
    n = len(wholes)

    def region(dst, t, chip, half):
        first, count = spans[t]
        _, rows, cols = wholes[t].shape
        if chip_axes[t] == 2:
            size, part = cols // N_CHIPS, rows // 2
            ref = _cut(_cut(dst[t], 2, pl.multiple_of(chip * size, size), size), 1, pl.multiple_of(half * part, part), part)
        else:
            size = rows // N_CHIPS
            part = size // 2
            ref = _cut(dst[t], 1, pl.multiple_of(chip * size + half * part, part), part)
        return _cut(ref, 0, first, count)

    def copies(dst, sems, receiving):
        x, y, c, chips = _position()
        send_sem, recv_sem = sems
        out = []
        for t in range(n):
            for j, chip in enumerate(chips):
                theirs = 2 * chip[0] + chip[1]
                if relay:
                    ref = region(dst, t, theirs, 1 - c if receiving else c)
                    peer = (x, y, 1 - c)
                else:
                    ref = region(dst, t, theirs if receiving else 2 * x + y, c)
                    peer = (*chip, c)
                out.append(_remote(ref, ref, send_sem.at[t, j], recv_sem.at[t, j], peer))
        return out

    def start(ins, outs, sems):
        for cp in copies(outs, sems, False):
            cp.start()

    def finish(ins, outs, sems):
        for cp in copies(outs, sems, True):
            cp.wait_recv()
        for cp in copies(outs, sems, False):
            cp.wait_send()

    return _Rider(operands=list(wholes), out_shape=[jax.ShapeDtypeStruct(w.shape, w.dtype) for w in wholes],
                  aliases={t: t for t in range(n)}, scratch=[pltpu.SemaphoreType.DMA((n, N_CHIPS - 1))] * 2,
                  start=start, finish=finish)


def _chain(first, second):
    cut = len(first.scratch)

    def start(ins, outs, sems):
        first.start(ins, outs, sems[:cut])
        first.finish(ins, outs, sems[:cut])
        second.start(ins, outs, sems[cut:])

    def finish(ins, outs, sems):
        second.finish(ins, outs, sems[cut:])

    return first._replace(scratch=list(first.scratch) + list(second.scratch), start=start, finish=finish)


def _swap_rider(parts, core_axes):
    n = len(parts)
    halves = []
    for arr, axis in zip(parts, core_axes):
        shape = list(arr.shape)
        shape[axis] //= 2
        halves.append(jax.ShapeDtypeStruct(tuple(shape), arr.dtype))

    def copies(ins, outs, sems):
        x, y, c, _ = _position()
        out = []
        for t in range(n):
            size = halves[t].shape[core_axes[t]]
            piece = _cut(ins[t], core_axes[t], pl.multiple_of((1 - c) * size, size), size)
            out.append(_remote(piece, outs[t], sems[0].at[t], sems[1].at[t], (x, y, 1 - c)))
        return out

    def start(ins, outs, sems):
        for cp in copies(ins, outs, sems):
            cp.start()

    def finish(ins, outs, sems):
        for cp in copies(ins, outs, sems):
            cp.wait()

    return _Rider(operands=list(parts), out_shape=halves, aliases={}, scratch=[pltpu.SemaphoreType.DMA((n,))] * 2,
                  start=start, finish=finish)


def _exchange_rider(partials, chip_axes):
    n = len(partials)
    quarters = []
    for arr, axis in zip(partials, chip_axes):
        shape = list(arr.shape)
        shape[axis] //= N_CHIPS
        quarters.append(jax.ShapeDtypeStruct((N_CHIPS - 1, *shape), arr.dtype))

    def copies(ins, outs, sems):
        x, y, c, chips = _position()
        out = []
        for t in range(n):
            size = quarters[t].shape[1 + chip_axes[t]]
            for j, chip in enumerate(chips):
                piece = _cut(ins[t], chip_axes[t], pl.multiple_of((2 * chip[0] + chip[1]) * size, size), size)
                out.append(_remote(piece, outs[t].at[j], sems[0].at[t, j], sems[1].at[t, j], (*chip, c)))
        return out

    def start(ins, outs, sems):
        for cp in copies(ins, outs, sems):
            cp.start()

    def finish(ins, outs, sems):
        for cp in copies(ins, outs, sems):
            cp.wait()

    return _Rider(operands=list(partials), out_shape=quarters, aliases={}, scratch=[pltpu.SemaphoreType.DMA((n, N_CHIPS - 1))] * 2,
                  start=start, finish=finish)


def _place_block(shard, chip_axis, place, dtype, *, name):
    depth, rows, cols = shard.shape
    rows_t = _row_tile(rows, cols * 4 * 4, 16 * 1024 * 1024)
    steps = rows // rows_t
    shape = list(shard.shape)
    shape[chip_axis] *= N_CHIPS
    if chip_axis == 1:
        out_spec = pl.BlockSpec((None, rows_t, cols), lambda l, i, p: (l, p[1] * steps + i, 0))
    else:
        out_spec = pl.BlockSpec((None, rows_t, cols), lambda l, i, p: (l, i, p[1]))

    def body(place_ref, src_ref, out_ref):
        out_ref[...] = src_ref[...].astype(dtype)

    return pl.pallas_call(
        body, name=name, out_shape=jax.ShapeDtypeStruct(tuple(shape), dtype),
        grid_spec=pltpu.PrefetchScalarGridSpec(num_scalar_prefetch=1, grid=(depth, steps),
                                               in_specs=[pl.BlockSpec((None, rows_t, cols), lambda l, i, p: (l, i, 0))],
                                               out_specs=out_spec),
        compiler_params=_cparams(("arbitrary", "arbitrary")),
    )(place, shard)


def _join_rider(blocks, core_axes, first, count):
    n = len(blocks)

    def copies(outs, sems, receiving):
        x, y, c, _ = _position()
        out = []
        for t in range(n):
            size = blocks[t].shape[core_axes[t]] // 2
            half = 1 - c if receiving else c
            ref = _cut(_cut(outs[t], core_axes[t], pl.multiple_of(half * size, size), size), 0, first, count)
            out.append(_remote(ref, ref, sems[0].at[t], sems[1].at[t], (x, y, 1 - c)))
        return out

    def start(ins, outs, sems):
        for cp in copies(outs, sems, False):
            cp.start()

    def finish(ins, outs, sems):
        for cp in copies(outs, sems, True):
            cp.wait_recv()
        for cp in copies(outs, sems, False):
            cp.wait_send()

    return _Rider(operands=list(blocks), out_shape=[jax.ShapeDtypeStruct(b.shape, b.dtype) for b in blocks],
                  aliases={t: t for t in range(n)}, scratch=[pltpu.SemaphoreType.DMA((n,))] * 2, start=start, finish=finish)


def _small_rider(vec):
    n_dev = 2 * N_CHIPS

    def copies(ins, outs, sems, receiving):
        x, y, c, _ = _position()
        flip = lambda v, bit: 1 - v if bit else v
        out = []
        for mask in range(1, n_dev):
            peer = (flip(x, mask & 4), flip(y, mask & 2), flip(c, mask & 1))
            row = 4 * peer[0] + 2 * peer[1] + peer[2] if receiving else 4 * x + 2 * y + c
            out.append(_remote(ins[0], outs[0].at[row], sems[0].at[mask - 1], sems[1].at[mask - 1], peer))
        return out

    def own(ins, outs, sems):
        x, y, c, _ = _position()
        return pltpu.make_async_copy(ins[0], outs[0].at[4 * x + 2 * y + c], sems[2])

    def start(ins, outs, sems):
        own(ins, outs, sems).start()
        for cp in copies(ins, outs, sems, False):
            cp.start()

    def finish(ins, outs, sems):
        for cp in copies(ins, outs, sems, True):
            cp.wait_recv()
        for cp in copies(ins, outs, sems, False):
            cp.wait_send()
        own(ins, outs, sems).wait()

    return _Rider(operands=[vec], out_shape=[jax.ShapeDtypeStruct((n_dev, *vec.shape), vec.dtype)], aliases={},
                  scratch=[pltpu.SemaphoreType.DMA((n_dev - 1,))] * 2 + [pltpu.SemaphoreType.DMA(())], start=start, finish=finish)


def _both(first, second):
    n_in, n_out, n_sem = len(first.operands), len(first.out_shape), len(first.scratch)

    def start(ins, outs, sems):
        first.start(ins[:n_in], outs[:n_out], sems[:n_sem])
        second.start(ins[n_in:], outs[n_out:], sems[n_sem:])

    def finish(ins, outs, sems):
        first.finish(ins[:n_in], outs[:n_out], sems[:n_sem])
        second.finish(ins[n_in:], outs[n_out:], sems[n_sem:])

    aliases = dict(first.aliases)
    aliases.update({n_in + i: n_out + o for i, o in second.aliases.items()})
    return _Rider(operands=list(first.operands) + list(second.operands), out_shape=list(first.out_shape) + list(second.out_shape),
                  aliases=aliases, scratch=list(first.scratch) + list(second.scratch), start=start, finish=finish)


def _row_tile(rows, row_bytes, budget):
    tile = rows
    for cand in (512, 256, 128, 64, 32, 16, 8):
        if rows % cand == 0:
            tile = cand
            if cand * row_bytes <= budget:
                break
    return tile


def _pair_sum_layer(part, got, core_axis, place, *, name):
    rows, cols = got.shape
    rows_t = _row_tile(rows, cols * 4 * 6, 16 * 1024 * 1024)
    steps = rows // rows_t
    if core_axis == 0:
        part_spec = pl.BlockSpec((rows_t, cols), lambda i, p: (p[0] * steps + i, 0))
    else:
        part_spec = pl.BlockSpec((rows_t, cols), lambda i, p: (i, p[0]))
    own_spec = pl.BlockSpec((rows_t, cols), lambda i, p: (i, 0))

    def body(place_ref, part_ref, got_ref, out_ref):
        out_ref[...] = (part_ref[...] + got_ref[...]).astype(BF16)

    return pl.pallas_call(
        body, name=name, out_shape=jax.ShapeDtypeStruct(got.shape, BF16),
        grid_spec=pltpu.PrefetchScalarGridSpec(num_scalar_prefetch=1, grid=(steps,), in_specs=[part_spec, own_spec], out_specs=own_spec),
        compiler_params=_cparams(("arbitrary",)),
    )(place, part, got)


def _chip_sum_layer(partial, got, blocks, layer, depth, chip_axis, core_axis, place, *, name):
    _, rows, cols = got.shape
    rows_t = _row_tile(rows, cols * 4 * 10, 24 * 1024 * 1024)
    steps = rows // rows_t
    if chip_axis == 0:
        own_spec = pl.BlockSpec((rows_t, cols), lambda i, p: (p[1] * steps + i, 0))
    else:
        own_spec = pl.BlockSpec((rows_t, cols), lambda i, p: (i, p[1]))
    got_spec = pl.BlockSpec((N_CHIPS - 1, rows_t, cols), lambda i, p: (0, i, 0))
    shape = [depth, rows, cols]
    shape[1 + core_axis] *= 2
    if core_axis == 0:
        out_spec = pl.BlockSpec((None, rows_t, cols), lambda i, p: (layer, p[0] * steps + i, 0))
    else:
        out_spec = pl.BlockSpec((None, rows_t, cols), lambda i, p: (layer, i, p[0]))

    def body(place_ref, own_ref, got_ref, *rest):
        out_ref = rest[-1]
        up = lambda val: val.astype(F32)
        out_ref[...] = ((up(own_ref[...]) + up(got_ref[0])) + up(got_ref[1])) + up(got_ref[2])

    in_specs, operands, aliases = [own_spec, got_spec], [place, partial, got], {}
    if blocks is not None:
        in_specs.append(pl.BlockSpec(memory_space=pl.ANY))
        operands.append(blocks)
        aliases = {3: 0}
    return pl.pallas_call(
        body, name=name, out_shape=jax.ShapeDtypeStruct(tuple(shape), F32),
        grid_spec=pltpu.PrefetchScalarGridSpec(num_scalar_prefetch=1, grid=(steps,), in_specs=in_specs, out_specs=out_spec),
        input_output_aliases=aliases, compiler_params=_cparams(("arbitrary",)),
    )(*operands)


def _sum_devices(stack, *, name):
    def body(src_ref, out_ref):
        total = src_ref[0]
        for d in range(1, stack.shape[0]):
            total = total + src_ref[d]
        out_ref[...] = total

    return pl.pallas_call(body, name=name, out_shape=jax.ShapeDtypeStruct(stack.shape[1:], F32))(stack)


def kernel(x, w_in, b_in, conv_w, conv_b, conv_ln_g, conv_ln_b, w_att_proj, w_conv_proj, b_conv_proj, w_out, ln_g, ln_b, loss_target, m_w_in, m_b_in, m_conv_w, m_conv_b, m_conv_ln_g, m_conv_ln_b, m_w_att_proj, m_w_conv_proj, m_b_conv_proj, m_w_out, m_ln_g, m_ln_b, v_w_in, v_b_in, v_conv_w, v_conv_b, v_conv_ln_g, v_conv_ln_b, v_w_att_proj, v_w_conv_proj, v_b_conv_proj, v_w_out, v_ln_g, v_ln_b):
    depth = w_in.shape[0]
    d_model = x.shape[-1]
    half = d_model // 2
    chip = 2 * lax.axis_index("x") + lax.axis_index("y")
    place = jnp.stack([lax.axis_index("c"), chip]).astype(jnp.int32)
    vec3 = lambda v: v.reshape(depth, 1, -1)

    taps = jnp.pad(conv_w, ((0, 0), (0, CONV_PAD - CONV_WIDTH), (0, 0)))
    gathered = [("w_in", w_in, 2, BF16), ("w_att_proj", w_att_proj, 2, BF16), ("w_conv_proj", w_conv_proj, 2, BF16),
                ("w_out", w_out, 1, BF16), ("conv_w", taps, 2, F32)]
    wholes = [_place_block(arr, axis, place, dtype, name="place_" + n) for n, arr, axis, dtype in gathered]
    loss, grad_x, packed, waiting, blocks = _train_pass(x[0], loss_target[0], wholes[:4], wholes[4], b_in, vec3(conv_b),
                                                        vec3(conv_ln_g), vec3(conv_ln_b), vec3(b_conv_proj), vec3(ln_g), vec3(ln_b), place)
    loss = lax.psum(loss[0, 0], ("x", "y", "c"))

    names = ["w_in", "b_in", "conv_w", "conv_b", "conv_ln_g", "conv_ln_b", "w_att_proj", "w_conv_proj", "b_conv_proj", "w_out", "ln_g", "ln_b"]
    weights = dict(zip(names, (w_in, b_in, conv_w, conv_b, conv_ln_g, conv_ln_b, w_att_proj, w_conv_proj, b_conv_proj, w_out, ln_g, ln_b)))
    first = dict(zip(names, (m_w_in, m_b_in, m_conv_w, m_conv_b, m_conv_ln_g, m_conv_ln_b, m_w_att_proj, m_w_conv_proj, m_b_conv_proj, m_w_out, m_ln_g, m_ln_b)))
    second = dict(zip(names, (v_w_in, v_b_in, v_conv_w, v_conv_b, v_conv_ln_g, v_conv_ln_b, v_w_att_proj, v_w_conv_proj, v_b_conv_proj, v_w_out, v_ln_g, v_ln_b)))
    flat = lambda arr: arr.reshape(-1, arr.shape[-1])

    *got, gathered = _lone_call(_both(_exchange_rider(waiting, MATRIX_CHIP_AXIS), _small_rider(packed)), name="exchange_last")
    blocks = _lone_call(_join_rider(_chip_sums(0, depth, waiting, got, blocks, place), [axis + 1 for axis in MATRIX_CORE_AXIS], 0, 1),
                        name="pair_join")
    reduced = dict(zip(MATRICES, blocks))

    total = _sum_devices(gathered, name="sum_devices")
    widths = dict(b_in=b_in.shape[1], conv_b=half, conv_ln_g=half, conv_ln_b=half, b_conv_proj=d_model, ln_g=d_model, ln_b=d_model)
    offset = 0
    for n in VECTORS:
        reduced[n] = total[:, offset:offset + widths[n]]
        offset += widths[n]
    taps = total[:, offset:].reshape(depth, CONV_WIDTH, half)
    reduced["conv_w"] = lax.dynamic_slice_in_dim(taps, chip * conv_w.shape[2], conv_w.shape[2], axis=2)

    delta, new_m, new_v = {}, {}, {}
    for n in names:
        shape = weights[n].shape
        *grad, d, m, v = _adamw(flat(weights[n]), flat(reduced[n]), flat(first[n]), flat(second[n]), name="adamw_" + n,
                                echo=n in MATRICES)
        if grad:
            reduced[n] = grad[0]
        delta[n], new_m[n], new_v[n] = d.reshape(shape), m.reshape(shape), v.reshape(shape)
    return (loss, grad_x[None], *[reduced[n].reshape(weights[n].shape) for n in names], *[delta[n] for n in names],
            *[new_m[n] for n in names], *[new_v[n] for n in names])
```
